```python
import jax, jax.numpy as jnp
from jax import lax
import numpy as np

D_MODEL = 1024
BATCH = 4
SEQ = 8192
DEPTH = 2
DEC_BATCH = 8
DEC_SEQ = 64
PAST_LEN = 2048

CHUNK = 64
N_HEADS = 8
HEAD_DIM = 64
KV_HEADS = 2
GROUPS = N_HEADS // KV_HEADS
IDX_HEADS = 4
IDX_DIM = 64
TOPK_MAX = 256
Q_BLOCK = 128
ROPE_THETA = 500000.0
ATTN_SCALE = HEAD_DIM ** -0.5
IDX_SCALE = (IDX_DIM ** -0.5) * (IDX_HEADS ** -0.5)
NEG = -1e30
HGRN_HEADS = 4
HGRN_DK = 128
HGRN_DV = 128
SCONV_WIDTH = 512
CONV_W = 3
BRANCH_WIDTH = 512
N_BRANCH = 3
D_FF = 2816
ALPHA = (2 * DEPTH) ** 0.25
BETA = (8 * DEPTH) ** -0.25
LN_EPS = 1e-5
IN_SIZES = (N_HEADS * HEAD_DIM, KV_HEADS * HEAD_DIM, KV_HEADS * HEAD_DIM,
            IDX_HEADS * IDX_DIM, IDX_DIM, IDX_HEADS,
            HGRN_HEADS * HGRN_DK, HGRN_HEADS * HGRN_DK, HGRN_HEADS * HGRN_DV, HGRN_HEADS * HGRN_DV,
            SCONV_WIDTH, SCONV_WIDTH, SCONV_WIDTH,
            N_BRANCH * D_MODEL)
IN_COLS = sum(IN_SIZES)

kernel_name = "hybrid_dsa_hgrn2_shortconv_stream_step"


def layer_norm(x, g, b):
    xf = x.astype(jnp.float32)
    mu = jnp.mean(xf, axis=-1, keepdims=True)
    var = jnp.mean(jnp.square(xf - mu), axis=-1, keepdims=True)
    return ((xf - mu) * lax.rsqrt(var + LN_EPS) * g + b).astype(x.dtype)


def partial_rotary(x, pos):
    rot = x.shape[-1] // 4
    half = rot // 2
    inv_freq = ROPE_THETA ** (-jnp.arange(half, dtype=jnp.float32) / half)
    ang = pos.astype(jnp.float32)[:, None] * inv_freq[None, :]
    cos = jnp.cos(ang)[:, None, :]
    sin = jnp.sin(ang)[:, None, :]
    x1 = x[..., :half].astype(jnp.float32)
    x2 = x[..., half:rot].astype(jnp.float32)
    return jnp.concatenate([(x1 * cos - x2 * sin).astype(x.dtype),
                            (x2 * cos + x1 * sin).astype(x.dtype),
                            x[..., rot:]], axis=-1)


def causal_dwconv(u, hist, w, b):
    S = u.shape[1]
    up = jnp.concatenate([hist.astype(u.dtype), u], axis=1)
    y = b
    for j in range(CONV_W):
        y = y + w[j] * up[:, j:j + S]
    return y, up[:, S:]


def dsa_attention(q, qi, wi, k_all, v_all, ki_all, q_pos):
    f32 = jnp.float32
    B, S = q.shape[:2]
    L = k_all.shape[1]
    top = min(TOPK_MAX, L // 4)
    qb = min(Q_BLOCK, S)
    nb = S // qb
    key_chunk = jnp.arange(L) // CHUNK
    kif = ki_all.astype(f32)

    def blocks(a):
        return jnp.swapaxes(a.reshape((B, nb, qb) + a.shape[2:]), 0, 1)

    def one_block(args):
        q_blk, qi_blk, wi_blk, qpos = args
        admissible = key_chunk[None, :] <= (qpos // CHUNK)[:, None]
        rel = jax.nn.relu(jnp.einsum('bqhd,bsd->bqhs', qi_blk.astype(f32), kif))
        score = jnp.einsum('bqh,bqhs->bqs', wi_blk.astype(f32), rel) * IDX_SCALE
        score = jnp.where(admissible[None], score, NEG)
        _, idx = lax.top_k(score, top)
        valid = admissible[jnp.arange(qb)[None, :, None], idx]
        k_sel = jax.vmap(lambda kk, ii: kk[ii])(k_all, idx)
        v_sel = jax.vmap(lambda vv, ii: vv[ii])(v_all, idx)
        qg = q_blk.reshape(B, qb, KV_HEADS, GROUPS, HEAD_DIM)
        logits = jnp.einsum('bqhgd,bqnhd->bqhgn', qg, k_sel, preferred_element_type=f32) * ATTN_SCALE
        logits = jnp.where(valid[:, :, None, None, :], logits, NEG)
        p = jax.nn.softmax(logits, axis=-1).astype(v_sel.dtype)
        o = jnp.einsum('bqhgn,bqnhd->bqhgd', p, v_sel)
        return o.reshape(B, qb, N_HEADS * HEAD_DIM)

    out = lax.map(one_block, (blocks(q), blocks(qi), blocks(wi), q_pos.reshape(nb, qb)))
    return jnp.swapaxes(out, 0, 1).reshape(B, S, N_HEADS * HEAD_DIM)


def hgrn2_scan(q, k, v, logf, s0):
    B, S, H = q.shape[:3]
    c = min(CHUNK, S)
    n = S // c
    tri = jnp.tril(jnp.ones((c, c), dtype=bool))[:, :, None]

    def chunks(a):
        return a.reshape(B, n, c, a.shape[2], a.shape[3]).transpose(1, 0, 3, 2, 4)

    def step(state, inp):
        qc, kc, vc, lc = inp
        cum = jnp.cumsum(lc, axis=2)
        diff = cum[:, :, :, None, :] - cum[:, :, None, :, :]
        decay = jnp.where(tri, jnp.exp(jnp.where(tri, diff, 0.0)), 0.0)
        scores = jnp.einsum('bhtd,bhsd,bhtsd->bhts', qc, kc, decay)
        o = (jnp.einsum('bhts,bhsv->bhtv', scores, vc)
             + jnp.einsum('bhtd,bhdv->bhtv', qc * jnp.exp(cum), state))
        last = cum[:, :, -1:, :]
        new_state = (jnp.exp(last[:, :, 0, :, None]) * state
                     + jnp.einsum('bhsd,bhsv->bhdv', kc * jnp.exp(last - cum), vc))
        return new_state, o

    s_fin, o = lax.scan(step, s0, (chunks(q), chunks(k), chunks(v), chunks(logf)))
    return o.transpose(1, 0, 3, 2, 4).reshape(B, S, H, v.shape[3]), s_fin


def trunk_layer(x, pos, k_past, v_past, ki_past, s0, sc_hist, ffn_hist, l, weights):
    (w_in, hgrn_lb_logits, hgrn_norm_g, sconv_w, sconv_b, w_branch, w_out, ln1_g, ln1_b,
     w_up, ffn_conv_w, ffn_conv_b, w_down, ln2_g, ln2_b) = weights
    f32 = jnp.float32
    B, S, _ = x.shape
    proj = x @ w_in[l]
    (a_q, a_k, a_v, i_q, i_k, i_w, h_q, h_f, h_i, h_g, c_b, c_c, c_x, g_pre) = jnp.split(
        proj, np.cumsum(IN_SIZES)[:-1].tolist(), axis=-1)

    q = partial_rotary(a_q.reshape(B, S, N_HEADS, HEAD_DIM), pos)
    k = partial_rotary(a_k.reshape(B, S, KV_HEADS, HEAD_DIM), pos)
    v = a_v.reshape(B, S, KV_HEADS, HEAD_DIM)
    qi = partial_rotary(i_q.reshape(B, S, IDX_HEADS, IDX_DIM), pos)
    ki = partial_rotary(i_k.reshape(B, S, 1, IDX_DIM), pos)[:, :, 0]
    if k_past is None:
        k_all, v_all, ki_all = k, v, ki
    else:
        k_all = jnp.concatenate([k_past.astype(k.dtype), k], axis=1)
        v_all = jnp.concatenate([v_past.astype(v.dtype), v], axis=1)
        ki_all = jnp.concatenate([ki_past.astype(ki.dtype), ki], axis=1)
    y_a = dsa_attention(q, qi, i_w, k_all, v_all, ki_all, pos).astype(x.dtype)

    lbp = jax.nn.softmax(hgrn_lb_logits.astype(f32), axis=0)
    lb = (jnp.cumsum(lbp, axis=0) - lbp[0])[l].reshape(HGRN_HEADS, HGRN_DK)
    f_pre = h_f.reshape(B, S, HGRN_HEADS, HGRN_DK).astype(f32)
    logf = jax.nn.log_sigmoid(f_pre) + jnp.log1p(lb * jnp.exp(-f_pre))
    k_h = (1.0 - lb) * jax.nn.sigmoid(-f_pre)
    q_h = jax.nn.silu(h_q.reshape(B, S, HGRN_HEADS, HGRN_DK).astype(f32))
    v_h = h_i.reshape(B, S, HGRN_HEADS, HGRN_DV).astype(f32)
    o_h, s_new = hgrn2_scan(q_h, k_h, v_h, logf, s0.astype(f32))
    o_h = o_h * lax.rsqrt(jnp.mean(jnp.square(o_h), axis=-1, keepdims=True) + LN_EPS) * hgrn_norm_g[l].astype(f32)
    y_b = (o_h.reshape(B, S, HGRN_HEADS * HGRN_DV) * jax.nn.silu(h_g.astype(f32))).astype(x.dtype)

    u = c_c * c_x
    u_conv, sc_new = causal_dwconv(u, sc_hist, sconv_w[l], sconv_b[l])
    y_c = c_b * u_conv

    gates = jax.nn.sigmoid(g_pre.reshape(B, S, N_BRANCH, D_MODEL).astype(f32)).astype(x.dtype)
    merged = (gates[:, :, 0] * (y_a @ w_branch[l, 0])
              + gates[:, :, 1] * (y_b @ w_branch[l, 1])
              + gates[:, :, 2] * (y_c @ w_branch[l, 2]))
    x = layer_norm(ALPHA * x + merged @ w_out[l], ln1_g[l], ln1_b[l])

    h = x @ w_up[l]
    h_conv, ffn_new = causal_dwconv(h, ffn_hist, ffn_conv_w[l], ffn_conv_b[l])
    a_g, b_v = jnp.split(h_conv, 2, axis=-1)
    ffn = (jax.nn.silu(a_g) * b_v) @ w_down[l]
    x = layer_norm(ALPHA * x + ffn, ln2_g[l], ln2_b[l])
    return x, (k, v, ki, s_new.astype(x.dtype), sc_new, ffn_new)


def setup_inputs(seed: int = 0) -> dict:
    key = jax.random.key(seed)
    ks = jax.random.split(key, 24)

    def nrm(k, shape, scale=1.0):
        return jax.random.normal(k, shape, jnp.float32) * scale

    return {
        'x_prompt': nrm(ks[0], (BATCH, SEQ, D_MODEL)),
        'x_sample': nrm(ks[1], (DEC_BATCH, DEC_SEQ, D_MODEL)),
        'cache_attn_k': nrm(ks[2], (DEPTH, DEC_BATCH, PAST_LEN, KV_HEADS, HEAD_DIM)),
        'cache_attn_v': nrm(ks[3], (DEPTH, DEC_BATCH, PAST_LEN, KV_HEADS, HEAD_DIM)),
        'cache_idx_k': nrm(ks[4], (DEPTH, DEC_BATCH, PAST_LEN, IDX_DIM)),
        'state_hgrn': nrm(ks[5], (DEPTH, DEC_BATCH, HGRN_HEADS, HGRN_DK, HGRN_DV), 0.3),
        'state_sconv': nrm(ks[6], (DEPTH, DEC_BATCH, CONV_W - 1, SCONV_WIDTH)),
        'state_ffn_conv': nrm(ks[7], (DEPTH, DEC_BATCH, CONV_W - 1, 2 * D_FF)),
        'w_in': nrm(ks[8], (DEPTH, D_MODEL, IN_COLS), D_MODEL ** -0.5),
        'hgrn_lb_logits': nrm(ks[9], (DEPTH, HGRN_HEADS * HGRN_DK), 0.5),
        'hgrn_norm_g': 1.0 + nrm(ks[10], (DEPTH, HGRN_DV), 0.02),
        'sconv_w': nrm(ks[11], (DEPTH, CONV_W, SCONV_WIDTH), CONV_W ** -0.5),
        'sconv_b': nrm(ks[12], (DEPTH, SCONV_WIDTH), 0.02),
        'w_branch': nrm(ks[13], (DEPTH, N_BRANCH, BRANCH_WIDTH, D_MODEL), BETA * BRANCH_WIDTH ** -0.5),
        'w_out': nrm(ks[14], (DEPTH, D_MODEL, D_MODEL), BETA * D_MODEL ** -0.5),
        'ln1_g': 1.0 + nrm(ks[15], (DEPTH, D_MODEL), 0.02),
        'ln1_b': nrm(ks[16], (DEPTH, D_MODEL), 0.02),
        'w_up': nrm(ks[17], (DEPTH, D_MODEL, 2 * D_FF), D_MODEL ** -0.5),
        'ffn_conv_w': nrm(ks[18], (DEPTH, CONV_W, 2 * D_FF), CONV_W ** -0.5),
        'ffn_conv_b': nrm(ks[19], (DEPTH, 2 * D_FF), 0.02),
        'w_down': nrm(ks[20], (DEPTH, D_FF, D_MODEL), BETA * D_FF ** -0.5),
        'ln2_g': 1.0 + nrm(ks[21], (DEPTH, D_MODEL), 0.02),
        'ln2_b': nrm(ks[22], (DEPTH, D_MODEL), 0.02),
    }


def reference(x_prompt, x_sample, cache_attn_k, cache_attn_v, cache_idx_k, state_hgrn, state_sconv,
              state_ffn_conv, w_in, hgrn_lb_logits, hgrn_norm_g, sconv_w, sconv_b, w_branch, w_out,
              ln1_g, ln1_b, w_up, ffn_conv_w, ffn_conv_b, w_down, ln2_g, ln2_b):
    weights = (w_in, hgrn_lb_logits, hgrn_norm_g, sconv_w, sconv_b, w_branch, w_out, ln1_g, ln1_b,
               w_up, ffn_conv_w, ffn_conv_b, w_down, ln2_g, ln2_b)
    B, S, _ = x_prompt.shape
    DB, DS, _ = x_sample.shape
    P = cache_attn_k.shape[2]
    pos_p = jnp.arange(S)
    pos_s = P + jnp.arange(DS)
    xp, xs = x_prompt, x_sample
    st_p = [[] for _ in range(6)]
    st_s = [[] for _ in range(6)]
    for l in range(DEPTH):
        xp, new_p = trunk_layer(
            xp, pos_p, None, None, None,
            jnp.zeros((B, HGRN_HEADS, HGRN_DK, HGRN_DV), jnp.float32),
            jnp.zeros((B, CONV_W - 1, SCONV_WIDTH), xp.dtype),
            jnp.zeros((B, CONV_W - 1, 2 * D_FF), xp.dtype), l, weights)
        xs, new_s = trunk_layer(
            xs, pos_s, cache_attn_k[l], cache_attn_v[l], cache_idx_k[l], state_hgrn[l],
            state_sconv[l], state_ffn_conv[l], l, weights)
        for j in range(6):
            st_p[j].append(new_p[j])
            st_s[j].append(new_s[j])
    k_p, v_p, ki_p, h_p, sc_p, ff_p = [jnp.stack(a, axis=0) for a in st_p]
    k_s, v_s, ki_s, h_s, sc_s, ff_s = [jnp.stack(a, axis=0) for a in st_s]
    return (xp, xs, k_p, v_p, ki_p, h_p, sc_p, ff_p, k_s, v_s, ki_s, h_s, sc_s, ff_s)
```

```python
import functools

import numpy as np
import jax
import jax.numpy as jnp
from jax import lax
from jax.experimental import pallas as pl
from jax.experimental.pallas import tpu as pltpu

D_MODEL = 1024
CHUNK = 64
N_HEADS = 8
HEAD_DIM = 64
KV_HEADS = 2
GROUPS = N_HEADS // KV_HEADS
IDX_HEADS = 4
IDX_DIM = 64
TOPK_MAX = 256
ROPE_THETA = 500000.0
ATTN_SCALE = HEAD_DIM ** -0.5
IDX_SCALE = (IDX_DIM ** -0.5) * (IDX_HEADS ** -0.5)
NEG = -1e30
HGRN_HEADS = 4
HGRN_DK = 128
HGRN_DV = 128
HGRN_W = HGRN_HEADS * HGRN_DK
SCONV_WIDTH = 512
CONV_W = 3
BRANCH_WIDTH = 512
N_BRANCH = 3
D_FF = 2816
LN_EPS = 1e-5
IN_SIZES = (N_HEADS * HEAD_DIM, KV_HEADS * HEAD_DIM, KV_HEADS * HEAD_DIM,
            IDX_HEADS * IDX_DIM, IDX_DIM, IDX_HEADS,
            HGRN_W, HGRN_W, HGRN_W, HGRN_W,
            SCONV_WIDTH, SCONV_WIDTH, SCONV_WIDTH,
            N_BRANCH * D_MODEL)

LANES = 128
VMEM_LIMIT = 56 * 1024 * 1024
KEY_TILE = 512
FFN_COLS = 256
INT_MIN = -2 ** 31

f32 = jnp.float32
bf16 = jnp.bfloat16
i32 = jnp.int32


def _row_tile(s, want):
    return want if s % want == 0 else s


def _const_spec(shape):
    nd = len(shape)
    return pl.BlockSpec(shape, lambda *_: (0,) * nd, pipeline_mode=pl.Buffered(1))


def _params(sem):
    return pltpu.CompilerParams(dimension_semantics=sem, vmem_limit_bytes=VMEM_LIMIT)


def _layer_norm(z, g, b):
    mu = jnp.mean(z, axis=-1, keepdims=True)
    d = z - mu
    var = jnp.mean(d * d, axis=-1, keepdims=True)
    return d * lax.rsqrt(var + LN_EPS) * g + b


def _sigmoid(x):
    return 1.0 / (1.0 + jnp.exp(-x))


ROT_COLS = 1024
AP_COLS = 1280


def _attn_proj_kernel(x_ref, w_ref, cs_ref, q_ref, k_ref, v_ref, qi_ref, ki_ref, wi_ref):
    xb = x_ref[0].astype(bf16)
    proj = jnp.dot(xb, w_ref[...], preferred_element_type=f32)
    c = cs_ref[0]
    sa = cs_ref[1]
    sb = cs_ref[2]

    def rot(g):
        xg = proj[:, g * LANES:(g + 1) * LANES]
        return xg * c + pltpu.roll(xg, LANES - 8, 1) * sa + pltpu.roll(xg, 8, 1) * sb

    for g in range(4):
        q_ref[0, :, g * LANES:(g + 1) * LANES] = rot(g).astype(bf16)
    k_ref[0] = rot(4)
    for g in range(2):
        qi_ref[0, :, g * LANES:(g + 1) * LANES] = rot(5 + g).astype(bf16)
    ki_ref[0] = rot(7)[:, :IDX_DIM]
    v_ref[0] = proj[:, ROT_COLS:ROT_COLS + LANES]
    wi_ref[0] = proj[:, ROT_COLS + LANES:ROT_COLS + LANES + IDX_HEADS]


def _attn_proj(x, w_ap, cs):
    B, S, _ = x.shape
    T = _row_tile(S, 256)
    row = lambda w: pl.BlockSpec((1, T, w), lambda b, t: (b, t, 0))
    return pl.pallas_call(
        _attn_proj_kernel,
        grid=(B, S // T),
        in_specs=[row(D_MODEL), _const_spec((D_MODEL, AP_COLS)),
                  pl.BlockSpec((3, T, LANES), lambda b, t: (0, t, 0))],
        out_specs=[row(512), row(128), row(128), row(256), row(IDX_DIM), row(IDX_HEADS)],
        out_shape=[jax.ShapeDtypeStruct((B, S, 512), bf16),
                   jax.ShapeDtypeStruct((B, S, 128), f32),
                   jax.ShapeDtypeStruct((B, S, 128), f32),
                   jax.ShapeDtypeStruct((B, S, 256), bf16),
                   jax.ShapeDtypeStruct((B, S, IDX_DIM), f32),
                   jax.ShapeDtypeStruct((B, S, IDX_HEADS), f32)],
        compiler_params=_params(("parallel", "parallel")),
        name="attn_proj",
    )(x, w_ap, cs)


def _rotary_tables(pos):
    rot = HEAD_DIM // 4
    half = rot // 2
    inv_freq = ROPE_THETA ** (-jnp.arange(half, dtype=f32) / half)
    ang = pos.astype(f32)[:, None] * inv_freq[None, :]
    cos = jnp.cos(ang)
    sin = jnp.sin(ang)
    S = pos.shape[0]
    one = jnp.ones((S, HEAD_DIM - rot), f32)
    zero = jnp.zeros((S, HEAD_DIM - rot), f32)
    zh = jnp.zeros((S, half), f32)
    c = jnp.concatenate([cos, cos, one], axis=1)
    sa = jnp.concatenate([-sin, zh, zero], axis=1)
    sb = jnp.concatenate([zh, sin, zero], axis=1)
    tile2 = lambda a: jnp.concatenate([a, a], axis=1)
    return jnp.stack([tile2(c), tile2(sa), tile2(sb)], axis=0)


def _dsa_kernel(q_ref, qi_ref, wi_ref, kit_ref, kt_ref, v_ref, tri_ref, o_ref,
                s_ref, m_ref, l_ref, acc_ref, *, TQ, TK, P, top):
    qb = pl.program_id(1)
    nk = (P + (qb + 1) * TQ + TK - 1) // TK
    NC = TK // LANES
    q_chunk = (P + qb * TQ + lax.broadcasted_iota(i32, (TQ, 1), 0)) // CHUNK

    def score_tile(j, carry):
        kit = kit_ref[0, j]
        acc = jnp.zeros((TQ, TK), f32)
        for h in range(IDX_HEADS):
            rel = jnp.maximum(jnp.dot(qi_ref[0, h], kit, preferred_element_type=f32), 0.0)
            acc = acc + wi_ref[0, :, h:h + 1] * rel
        score = acc * IDX_SCALE
        k_chunk = (j * TK + lax.broadcasted_iota(i32, (1, TK), 1)) // CHUNK
        score = jnp.where(k_chunk <= q_chunk, score, NEG)
        bits = pltpu.bitcast(score, i32)
        s_ref[j] = jnp.where(bits >= 0, bits, bits ^ 0x7FFFFFFF)
        return carry

    lax.fori_loop(0, nk, score_tile, 0)

    def count(pred):
        def body(j, acc):
            keys = s_ref[j]
            for c in range(NC):
                acc = acc + jnp.where(pred(keys[:, c * LANES:(c + 1) * LANES]), 1, 0)
            return acc
        acc = lax.fori_loop(0, nk, body, jnp.zeros((TQ, LANES), i32))
        tot = jnp.sum(acc.astype(f32), axis=-1, keepdims=True)
        return jnp.broadcast_to(tot, (TQ, LANES))

    topf = float(top)
    t0 = jnp.where(count(lambda k: k >= 0) >= topf, 0, INT_MIN).astype(i32)

    def search(i, t):
        cand = t | jnp.left_shift(jnp.int32(1), 30 - i)
        return jnp.where(count(lambda k: k >= cand) >= topf, cand, t)

    t = lax.fori_loop(0, 31, search, t0)
    n_gt = count(lambda k: k > t)
    neg_key = np.array(NEG, np.float32).view(np.int32) ^ 0x7FFFFFFF
    need = jnp.where(t == neg_key, 0.0, topf - n_gt)

    m_ref[...] = jnp.full(m_ref.shape, NEG, f32)
    l_ref[...] = jnp.zeros(l_ref.shape, f32)
    acc_ref[...] = jnp.zeros(acc_ref.shape, f32)

    def attend(j, taken):
        keys = s_ref[j]
        sel = []
        for c in range(NC):
            kc = keys[:, c * LANES:(c + 1) * LANES]
            eq = jnp.where(kc == t, 1.0, 0.0)
            rank = taken + jnp.dot(eq.astype(bf16), tri_ref[...], preferred_element_type=f32)
            sel.append(jnp.where(kc > t, 1.0, jnp.where(rank <= need, eq, 0.0)))
            taken = taken + jnp.sum(eq, axis=-1, keepdims=True)
        sel = jnp.concatenate(sel, axis=1)
        mask = jnp.concatenate([sel] * GROUPS, axis=0) > 0.5
        for g in range(KV_HEADS):
            logits = jnp.dot(q_ref[0, 0, g], kt_ref[0, j, g], preferred_element_type=f32) * ATTN_SCALE
            m_old = m_ref[g]
            m_new = jnp.maximum(m_old, jnp.max(jnp.where(mask, logits, NEG), axis=-1, keepdims=True))
            p = jnp.where(mask, jnp.exp(logits - m_new), 0.0)
            alpha = jnp.exp(m_old - m_new)
            l_ref[g] = alpha * l_ref[g] + jnp.sum(p, axis=-1, keepdims=True)
            vg = v_ref[0, j][:, g * HEAD_DIM:(g + 1) * HEAD_DIM]
            acc_ref[g] = alpha * acc_ref[g] + jnp.dot(p.astype(bf16), vg, preferred_element_type=f32)
            m_ref[g] = m_new
        return taken

    lax.fori_loop(0, nk, attend, jnp.zeros((TQ, LANES), f32))
    for g in range(KV_HEADS):
        o_ref[0, 0, g] = (acc_ref[g] / l_ref[g]).astype(bf16)


def _dsa(q, qi, wi, k_all, v_all, ki_all, P):
    B, S, _ = q.shape
    L = k_all.shape[1]
    top = min(TOPK_MAX, L // 4)
    TQ = _row_tile(S, 2 * CHUNK)
    TK = KEY_TILE
    nq = S // TQ
    NT = -(-L // TK)
    pad = NT * TK - L

    def tiles(a):
        a = jnp.pad(a.astype(bf16), ((0, 0), (0, pad), (0, 0)))
        return a.reshape(B, NT, TK, a.shape[-1])

    kit = jnp.swapaxes(tiles(ki_all), 2, 3)
    kt = tiles(k_all).reshape(B, NT, TK, KV_HEADS, HEAD_DIM).transpose(0, 1, 3, 4, 2)
    vt = tiles(v_all)
    qs = q.reshape(B, nq, TQ, KV_HEADS, GROUPS, HEAD_DIM).transpose(0, 1, 3, 4, 2, 5)
    qs = qs.reshape(B, nq, KV_HEADS, GROUPS * TQ, HEAD_DIM)
    qis = qi.reshape(B, S, IDX_HEADS, IDX_DIM).transpose(0, 2, 1, 3)
    tri = jnp.asarray(np.triu(np.ones((LANES, LANES), np.float32)), bf16)
    R = GROUPS * TQ
    out = pl.pallas_call(
        functools.partial(_dsa_kernel, TQ=TQ, TK=TK, P=P, top=top),
        grid=(B, nq),
        in_specs=[pl.BlockSpec((1, 1, KV_HEADS, R, HEAD_DIM), lambda b, i: (b, i, 0, 0, 0)),
                  pl.BlockSpec((1, IDX_HEADS, TQ, IDX_DIM), lambda b, i: (b, 0, i, 0)),
                  pl.BlockSpec((1, TQ, IDX_HEADS), lambda b, i: (b, i, 0)),
                  pl.BlockSpec((1, NT, IDX_DIM, TK), lambda b, i: (b, 0, 0, 0)),
                  pl.BlockSpec((1, NT, KV_HEADS, HEAD_DIM, TK), lambda b, i: (b, 0, 0, 0, 0)),
                  pl.BlockSpec((1, NT, TK, KV_HEADS * HEAD_DIM), lambda b, i: (b, 0, 0, 0)),
                  _const_spec((LANES, LANES))],
        out_specs=pl.BlockSpec((1, 1, KV_HEADS, R, HEAD_DIM), lambda b, i: (b, i, 0, 0, 0)),
        out_shape=jax.ShapeDtypeStruct((B, nq, KV_HEADS, R, HEAD_DIM), bf16),
        scratch_shapes=[pltpu.VMEM((NT, TQ, TK), i32),
                        pltpu.VMEM((KV_HEADS, R, 1), f32),
                        pltpu.VMEM((KV_HEADS, R, 1), f32),
                        pltpu.VMEM((KV_HEADS, R, HEAD_DIM), f32)],
        compiler_params=_params(("parallel", "arbitrary")),
        name="dsa",
    )(qs, qis, wi, kit, kt, vt, tri)
    out = out.reshape(B, nq, KV_HEADS, GROUPS, TQ, HEAD_DIM).transpose(0, 1, 4, 2, 3, 5)
    return out.reshape(B, S, N_HEADS * HEAD_DIM)


def _hgrn_mats(T):
    r = np.arange(T)[:, None]
    c = np.arange(T)[None, :]
    mats = [c <= r, c > r]
    b = T // 2
    while b >= 1:
        blk_r = r // (2 * b)
        mid = blk_r * 2 * b + b
        same = blk_r == c // (2 * b)
        late = r >= mid
        mats.append(same & ((late & (c >= mid) & (c <= r)) | (~late & (c > r) & (c < mid))))
        b //= 2
    return np.concatenate(mats, axis=0).astype(np.float32)


def _hgrn_kernel(x_ref, w_ref, lb_ref, g_ref, mats_ref, s0_ref, y_ref, st_out_ref, st_ref, *, T):
    t_idx = pl.program_id(1)

    @pl.when(t_idx == 0)
    def _():
        st_ref[...] = s0_ref[0]

    xb = x_ref[0].astype(bf16)
    proj = jnp.dot(xb, w_ref[...], preferred_element_type=f32)
    hq = proj[:, 0:HGRN_W]
    z = proj[:, HGRN_W:2 * HGRN_W]
    hv = proj[:, 2 * HGRN_W:3 * HGRN_W]
    hg = proj[:, 3 * HGRN_W:4 * HGRN_W]
    lb = lb_ref[...]
    logf = (jnp.minimum(z, 0.0) - jnp.log1p(jnp.exp(-jnp.abs(z)))) + jnp.log1p(lb * jnp.exp(-z))
    kk = (1.0 - lb) * _sigmoid(-z)
    qq = hq * _sigmoid(hq)
    gate = hg * _sigmoid(hg)

    hi = logf.astype(bf16)
    r1 = logf - hi.astype(f32)
    mid = r1.astype(bf16)
    lo = (r1 - mid.astype(f32)).astype(bf16)
    mats = mats_ref[...]
    e_all = (jnp.dot(mats, hi, preferred_element_type=f32)
             + jnp.dot(mats, mid, preferred_element_type=f32)
             + jnp.dot(mats, lo, preferred_element_type=f32))

    row = lax.broadcasted_iota(i32, (T, 1), 0)
    col = lax.broadcasted_iota(i32, (1, T), 1)
    nt = (((1,), (1,)), ((), ()))
    tn = (((0,), (0,)), ((), ()))
    for h in range(HGRN_HEADS):
        sl = slice(h * HGRN_DK, (h + 1) * HGRN_DK)
        q_h, k_h, v_h = qq[:, sl], kk[:, sl], hv[:, sl]
        vb = v_h.astype(bf16)
        cum = e_all[0:T, sl]
        suf = e_all[T:2 * T, sl]
        st = st_ref[h]
        o = lax.dot_general((q_h * jnp.exp(cum)).astype(bf16), st.astype(bf16), nt,
                            preferred_element_type=f32)
        o = o + jnp.sum(q_h * k_h, axis=-1, keepdims=True) * v_h
        scores = jnp.zeros((T, T), f32)
        b = T // 2
        lvl = 2
        while b >= 1:
            xdec = jnp.exp(e_all[lvl * T:(lvl + 1) * T, sl])
            late = ((row // b) % 2) == 1
            a_m = jnp.where(late, q_h * xdec, 0.0).astype(bf16)
            b_m = jnp.where(late, 0.0, k_h * xdec).astype(bf16)
            sc = lax.dot_general(a_m, b_m, nt, preferred_element_type=f32)
            if 2 * b < T:
                sc = jnp.where((row // (2 * b)) == (col // (2 * b)), sc, 0.0)
            scores = scores + sc
            b //= 2
            lvl += 1
        o = o + jnp.dot(scores.astype(bf16), vb, preferred_element_type=f32)
        kd = (k_h * jnp.exp(suf)).astype(bf16)
        st_ref[h] = (st * jnp.exp(cum[T - 1:T, :])
                     + lax.dot_general(vb, kd, tn, preferred_element_type=f32))
        o = o * lax.rsqrt(jnp.mean(o * o, axis=-1, keepdims=True) + LN_EPS) * g_ref[...]
        y_ref[0, :, sl] = (o * gate[:, sl]).astype(bf16)

    st_out_ref[0] = st_ref[...]


def _hgrn(x, w_h, lb, norm_g, s0):
    B, S, _ = x.shape
    T = _row_tile(S, 2 * CHUNK)
    mats = jnp.asarray(_hgrn_mats(T), bf16)
    nm = mats.shape[0]
    st0 = jnp.swapaxes(s0.astype(f32), 2, 3)
    y, st = pl.pallas_call(
        functools.partial(_hgrn_kernel, T=T),
        grid=(B, S // T),
        in_specs=[pl.BlockSpec((1, T, D_MODEL), lambda b, t: (b, t, 0)),
                  _const_spec((D_MODEL, 4 * HGRN_W)),
                  _const_spec((1, HGRN_W)),
                  _const_spec((1, HGRN_DV)),
                  _const_spec((nm, T)),
                  pl.BlockSpec((1, HGRN_HEADS, HGRN_DV, HGRN_DK), lambda b, t: (b, 0, 0, 0))],
        out_specs=[pl.BlockSpec((1, T, HGRN_W), lambda b, t: (b, t, 0)),
                   pl.BlockSpec((1, HGRN_HEADS, HGRN_DV, HGRN_DK), lambda b, t: (b, 0, 0, 0))],
        out_shape=[jax.ShapeDtypeStruct((B, S, HGRN_W), bf16),
                   jax.ShapeDtypeStruct((B, HGRN_HEADS, HGRN_DV, HGRN_DK), f32)],
        scratch_shapes=[pltpu.VMEM((HGRN_HEADS, HGRN_DV, HGRN_DK), f32)],
        compiler_params=_params(("parallel", "arbitrary")),
        name="hgrn",
    )(x, w_h, lb, norm_g, mats, st0)
    return y, jnp.swapaxes(st, 2, 3)


HIST = CONV_W - 1
PADROWS = 8


def _conv_rows(buf_ref, cols, u, w, bias, T):
    buf_ref[PADROWS:PADROWS + T, cols] = u
    y = bias
    for j in range(CONV_W):
        y = y + w[j:j + 1, :] * buf_ref[PADROWS - HIST + j:PADROWS - HIST + j + T, cols]
    buf_ref[PADROWS - HIST:PADROWS, cols] = u[T - HIST:T, :]
    return y


def _merge_kernel(x_ref, ya_ref, yb_ref, hist_ref, wc_ref, wg_ref, wbr_ref, wout_ref, cw_ref, cb_ref,
                  g_ref, b_ref, o_ref, hist_out_ref, buf_ref, *, T, alpha):
    @pl.when(pl.program_id(1) == 0)
    def _():
        buf_ref[PADROWS - HIST:PADROWS, :] = hist_ref[0]

    x = x_ref[0]
    xb = x.astype(bf16)
    c = jnp.dot(xb, wc_ref[...], preferred_element_type=f32)
    W = SCONV_WIDTH
    u = c[:, W:2 * W] * c[:, 2 * W:3 * W]
    u_conv = _conv_rows(buf_ref, slice(0, W), u, cw_ref[...], cb_ref[...], T)
    yc = (c[:, 0:W] * u_conv).astype(bf16)
    hist_out_ref[0] = buf_ref[PADROWS - HIST:PADROWS, :]

    merged = None
    for i, yb in enumerate((ya_ref[0], yb_ref[0], yc)):
        gate = _sigmoid(jnp.dot(xb, wg_ref[:, i * D_MODEL:(i + 1) * D_MODEL], preferred_element_type=f32))
        term = gate * jnp.dot(yb, wbr_ref[i], preferred_element_type=f32)
        merged = term if merged is None else merged + term
    out = jnp.dot(merged.astype(bf16), wout_ref[...], preferred_element_type=f32)
    o_ref[0] = _layer_norm(alpha * x + out, g_ref[...], b_ref[...])


def _merge(x, ya, yb, hist, wc, wg, wbr, wout, cw, cb, g, b, alpha):
    B, S, _ = x.shape
    T = _row_tile(S, 256)
    row = lambda w: pl.BlockSpec((1, T, w), lambda bb, t: (bb, t, 0))
    hspec = pl.BlockSpec((1, HIST, SCONV_WIDTH), lambda bb, t: (bb, 0, 0))
    return pl.pallas_call(
        functools.partial(_merge_kernel, T=T, alpha=alpha),
        grid=(B, S // T),
        in_specs=[row(D_MODEL), row(BRANCH_WIDTH), row(BRANCH_WIDTH), hspec,
                  _const_spec((D_MODEL, 3 * SCONV_WIDTH)), _const_spec((D_MODEL, N_BRANCH * D_MODEL)),
                  _const_spec((N_BRANCH, BRANCH_WIDTH, D_MODEL)), _const_spec((D_MODEL, D_MODEL)),
                  _const_spec((CONV_W, SCONV_WIDTH)), _const_spec((1, SCONV_WIDTH)),
                  _const_spec((1, D_MODEL)), _const_spec((1, D_MODEL))],
        out_specs=[row(D_MODEL), hspec],
        out_shape=[jax.ShapeDtypeStruct((B, S, D_MODEL), f32),
                   jax.ShapeDtypeStruct((B, HIST, SCONV_WIDTH), f32)],
        scratch_shapes=[pltpu.VMEM((PADROWS + T, SCONV_WIDTH), f32)],
        compiler_params=_params(("parallel", "arbitrary")),
        name="merge",
    )(x, ya, yb, hist, wc, wg, wbr, wout, cw, cb, g, b)


def _ffn_kernel(x_ref, hist_ref, wup_ref, cw_ref, cb_ref, wdn_ref, g_ref, b_ref, o_ref, hist_out_ref,
                buf_ref, *, T, alpha):
    @pl.when(pl.program_id(1) == 0)
    def _():
        buf_ref[PADROWS - HIST:PADROWS, :] = hist_ref[0]

    x = x_ref[0]
    xb = x.astype(bf16)
    acc = jnp.zeros((T, D_MODEL), f32)
    for cidx in range(D_FF // FFN_COLS):
        halves = []
        for off in (0, D_FF):
            cols = slice(off + cidx * FFN_COLS, off + (cidx + 1) * FFN_COLS)
            h = jnp.dot(xb, wup_ref[:, cols], preferred_element_type=f32)
            halves.append(_conv_rows(buf_ref, cols, h, cw_ref[:, cols], cb_ref[:, cols], T))
        a_g, b_v = halves
        act = (a_g * _sigmoid(a_g) * b_v).astype(bf16)
        acc = acc + jnp.dot(act, wdn_ref[cidx * FFN_COLS:(cidx + 1) * FFN_COLS, :],
                            preferred_element_type=f32)
    hist_out_ref[0] = buf_ref[PADROWS - HIST:PADROWS, :]
    o_ref[0] = _layer_norm(alpha * x + acc, g_ref[...], b_ref[...])


def _ffn(x, hist, wup, cw, cb, wdn, g, b, alpha):
    B, S, _ = x.shape
    T = _row_tile(S, 256)
    row = pl.BlockSpec((1, T, D_MODEL), lambda bb, t: (bb, t, 0))
    hspec = pl.BlockSpec((1, HIST, 2 * D_FF), lambda bb, t: (bb, 0, 0))
    return pl.pallas_call(
        functools.partial(_ffn_kernel, T=T, alpha=alpha),
        grid=(B, S // T),
        in_specs=[row, hspec, _const_spec((D_MODEL, 2 * D_FF)), _const_spec((CONV_W, 2 * D_FF)),
                  _const_spec((1, 2 * D_FF)), _const_spec((D_FF, D_MODEL)),
                  _const_spec((1, D_MODEL)), _const_spec((1, D_MODEL))],
        out_specs=[row, hspec],
        out_shape=[jax.ShapeDtypeStruct((B, S, D_MODEL), f32),
                   jax.ShapeDtypeStruct((B, HIST, 2 * D_FF), f32)],
        scratch_shapes=[pltpu.VMEM((PADROWS + T, 2 * D_FF), f32)],
        compiler_params=_params(("parallel", "arbitrary")),
        name="ffn",
    )(x, hist, wup, cw, cb, wdn, g, b)


def _layer_weights(l, depth, w_in, hgrn_lb_logits, hgrn_norm_g, sconv_w, sconv_b, w_branch, w_out,
                   ln1_g, ln1_b, w_up, ffn_conv_w, ffn_conv_b, w_down, ln2_g, ln2_b):
    offs = np.concatenate([[0], np.cumsum(IN_SIZES)]).tolist()
    sec = lambda i, j=None: w_in[l][:, offs[i]:offs[(i if j is None else j) + 1]]
    a_q, a_k, a_v, i_q, i_k, i_w = (sec(i) for i in range(6))
    zpad = lambda n: jnp.zeros((D_MODEL, n), w_in.dtype)
    w_ap = jnp.concatenate([a_q, a_k, i_q, i_k, zpad(ROT_COLS - 960), a_v, i_w,
                            zpad(AP_COLS - ROT_COLS - 128 - IDX_HEADS)], axis=1).astype(bf16)
    lbp = jax.nn.softmax(hgrn_lb_logits.astype(f32), axis=0)
    lb = (jnp.cumsum(lbp, axis=0) - lbp[0])[l].reshape(1, HGRN_W)
    row = lambda a: a.reshape(1, -1).astype(f32)
    return dict(
        w_ap=w_ap, w_h=sec(6, 9).astype(bf16), lb=lb, norm_g=row(hgrn_norm_g[l]),
        w_c=sec(10, 12).astype(bf16), w_g=sec(13).astype(bf16),
        w_br=w_branch[l].astype(bf16), w_out=w_out[l].astype(bf16),
        sconv_w=sconv_w[l].astype(f32), sconv_b=row(sconv_b[l]),
        ln1_g=row(ln1_g[l]), ln1_b=row(ln1_b[l]),
        w_up=w_up[l].astype(bf16), ffn_w=ffn_conv_w[l].astype(f32), ffn_b=row(ffn_conv_b[l]),
        w_down=w_down[l].astype(bf16), ln2_g=row(ln2_g[l]), ln2_b=row(ln2_b[l]))


def _trunk_layer(x, cs, P, k_past, v_past, ki_past, s0, sc_hist, ffn_hist, w, alpha):
    B, S, _ = x.shape
    q, k, v, qi, ki, wi = _attn_proj(x, w["w_ap"], cs)
    if k_past is None:
        k_all, v_all, ki_all = k, v, ki
    else:
        k_all = jnp.concatenate([k_past.reshape(B, P, KV_HEADS * HEAD_DIM), k], axis=1)
        v_all = jnp.concatenate([v_past.reshape(B, P, KV_HEADS * HEAD_DIM), v], axis=1)
        ki_all = jnp.concatenate([ki_past, ki], axis=1)
    y_a = _dsa(q, qi, wi, k_all, v_all, ki_all, P)
    y_b, s_new = _hgrn(x, w["w_h"], w["lb"], w["norm_g"], s0)
    x1, sc_new = _merge(x, y_a, y_b, sc_hist, w["w_c"], w["w_g"], w["w_br"], w["w_out"],
                        w["sconv_w"], w["sconv_b"], w["ln1_g"], w["ln1_b"], alpha)
    x2, ffn_new = _ffn(x1, ffn_hist, w["w_up"], w["ffn_w"], w["ffn_b"], w["w_down"],
                       w["ln2_g"], w["ln2_b"], alpha)
    new = (k.reshape(B, S, KV_HEADS, HEAD_DIM), v.reshape(B, S, KV_HEADS, HEAD_DIM), ki,
           s_new, sc_new, ffn_new)
    return x2, new


def kernel(x_prompt, x_sample, cache_attn_k, cache_attn_v, cache_idx_k, state_hgrn, state_sconv,
           state_ffn_conv, w_in, hgrn_lb_logits, hgrn_norm_g, sconv_w, sconv_b, w_branch, w_out,
           ln1_g, ln1_b, w_up, ffn_conv_w, ffn_conv_b, w_down, ln2_g, ln2_b):
    depth = w_in.shape[0]
    alpha = (2 * depth) ** 0.25
    B, S, _ = x_prompt.shape
    DB, DS, _ = x_sample.shape
    P = cache_attn_k.shape[2]
    cs_p = _rotary_tables(jnp.arange(S))
    cs_s = _rotary_tables(P + jnp.arange(DS))
    xp, xs = x_prompt, x_sample
    st_p = [[] for _ in range(6)]
    st_s = [[] for _ in range(6)]
    for l in range(depth):
        w = _layer_weights(l, depth, w_in, hgrn_lb_logits, hgrn_norm_g, sconv_w, sconv_b, w_branch,
                           w_out, ln1_g, ln1_b, w_up, ffn_conv_w, ffn_conv_b, w_down, ln2_g, ln2_b)
        xp, new_p = _trunk_layer(
            xp, cs_p, 0, None, None, None,
            jnp.zeros((B, HGRN_HEADS, HGRN_DK, HGRN_DV), f32),
            jnp.zeros((B, HIST, SCONV_WIDTH), f32),
            jnp.zeros((B, HIST, 2 * D_FF), f32), w, alpha)
        xs, new_s = _trunk_layer(
            xs, cs_s, P, cache_attn_k[l], cache_attn_v[l], cache_idx_k[l], state_hgrn[l],
            state_sconv[l], state_ffn_conv[l], w, alpha)
        for j in range(6):
            st_p[j].append(new_p[j])
            st_s[j].append(new_s[j])
    outs_p = [jnp.stack(a, axis=0) for a in st_p]
    outs_s = [jnp.stack(a, axis=0) for a in st_s]
    return (xp, xs, *outs_p, *outs_s)
```

```python
import functools

import numpy as np
import jax
import jax.numpy as jnp
from jax import lax
from jax.experimental import pallas as pl
from jax.experimental.pallas import tpu as pltpu

D_MODEL = 1024
CHUNK = 64
N_HEADS = 8
HEAD_DIM = 64
KV_HEADS = 2
GROUPS = N_HEADS // KV_HEADS
IDX_HEADS = 4
IDX_DIM = 64
TOPK_MAX = 256
ROPE_THETA = 500000.0
ATTN_SCALE = HEAD_DIM ** -0.5
IDX_SCALE = (IDX_DIM ** -0.5) * (IDX_HEADS ** -0.5)
NEG = -1e30
HGRN_HEADS = 4
HGRN_DK = 128
HGRN_DV = 128
HGRN_W = HGRN_HEADS * HGRN_DK
SCONV_WIDTH = 512
CONV_W = 3
BRANCH_WIDTH = 512
N_BRANCH = 3
D_FF = 2816
LN_EPS = 1e-5
IN_SIZES = (N_HEADS * HEAD_DIM, KV_HEADS * HEAD_DIM, KV_HEADS * HEAD_DIM,
            IDX_HEADS * IDX_DIM, IDX_DIM, IDX_HEADS,
            HGRN_W, HGRN_W, HGRN_W, HGRN_W,
            SCONV_WIDTH, SCONV_WIDTH, SCONV_WIDTH,
            N_BRANCH * D_MODEL)

LANES = 128
VMEM_LIMIT = 56 * 1024 * 1024
KEY_TILE = 512
SEARCH_STRIDE = 4
VT_ROWS = HEAD_DIM + 16
FFN_COLS = 256
INT_MIN = -2 ** 31

f32 = jnp.float32
bf16 = jnp.bfloat16
i32 = jnp.int32


def _row_tile(s, want):
    return want if s % want == 0 else s


def _const_spec(shape):
    nd = len(shape)
    return pl.BlockSpec(shape, lambda *_: (0,) * nd, pipeline_mode=pl.Buffered(1))


def _params(sem):
    return pltpu.CompilerParams(dimension_semantics=sem, vmem_limit_bytes=VMEM_LIMIT)


def _layer_norm(z, g, b):
    mu = jnp.mean(z, axis=-1, keepdims=True)
    d = z - mu
    var = jnp.mean(d * d, axis=-1, keepdims=True)
    return d * lax.rsqrt(var + LN_EPS) * g + b


def _sigmoid(x):
    return 1.0 / (1.0 + jnp.exp(-x))


Q_PRESCALE = ATTN_SCALE * float(np.log2(np.e))
ROT_COLS = 1024
AP_COLS = 1280


def _attn_proj_kernel(x_ref, w_ref, cs_ref, q_ref, k_ref, v_ref, qi_ref, ki_ref, wi_ref):
    xb = x_ref[0].astype(bf16)
    proj = jnp.dot(xb, w_ref[...], preferred_element_type=f32)
    c = cs_ref[0]
    sa = cs_ref[1]
    sb = cs_ref[2]

    def rot(g):
        xg = proj[:, g * LANES:(g + 1) * LANES]
        return xg * c + pltpu.roll(xg, LANES - 8, 1) * sa + pltpu.roll(xg, 8, 1) * sb

    for g in range(4):
        q_ref[0, :, g * LANES:(g + 1) * LANES] = (rot(g) * Q_PRESCALE).astype(bf16)
    k_ref[0] = rot(4)
    for g in range(2):
        qi_ref[0, :, g * LANES:(g + 1) * LANES] = rot(5 + g).astype(bf16)
    ki_ref[0] = rot(7)[:, :IDX_DIM]
    v_ref[0] = proj[:, ROT_COLS:ROT_COLS + LANES]
    wi_ref[0] = proj[:, ROT_COLS + LANES:ROT_COLS + LANES + IDX_HEADS]


def _attn_proj(x, w_ap, cs):
    B, S, _ = x.shape
    T = _row_tile(S, 256)
    row = lambda w: pl.BlockSpec((1, T, w), lambda b, t: (b, t, 0))
    return pl.pallas_call(
        _attn_proj_kernel,
        grid=(B, S // T),
        in_specs=[row(D_MODEL), _const_spec((D_MODEL, AP_COLS)),
                  pl.BlockSpec((3, T, LANES), lambda b, t: (0, t, 0))],
        out_specs=[row(512), row(128), row(128), row(256), row(IDX_DIM), row(IDX_HEADS)],
        out_shape=[jax.ShapeDtypeStruct((B, S, 512), bf16),
                   jax.ShapeDtypeStruct((B, S, 128), f32),
                   jax.ShapeDtypeStruct((B, S, 128), f32),
                   jax.ShapeDtypeStruct((B, S, 256), bf16),
                   jax.ShapeDtypeStruct((B, S, IDX_DIM), f32),
                   jax.ShapeDtypeStruct((B, S, IDX_HEADS), f32)],
        compiler_params=_params(("parallel", "parallel")),
        name="attn_proj",
    )(x, w_ap, cs)


def _rotary_tables(pos):
    rot = HEAD_DIM // 4
    half = rot // 2
    inv_freq = ROPE_THETA ** (-jnp.arange(half, dtype=f32) / half)
    ang = pos.astype(f32)[:, None] * inv_freq[None, :]
    cos = jnp.cos(ang)
    sin = jnp.sin(ang)
    S = pos.shape[0]
    one = jnp.ones((S, HEAD_DIM - rot), f32)
    zero = jnp.zeros((S, HEAD_DIM - rot), f32)
    zh = jnp.zeros((S, half), f32)
    c = jnp.concatenate([cos, cos, one], axis=1)
    sa = jnp.concatenate([-sin, zh, zero], axis=1)
    sb = jnp.concatenate([zh, sin, zero], axis=1)
    tile2 = lambda a: jnp.concatenate([a, a], axis=1)
    return jnp.stack([tile2(c), tile2(sa), tile2(sb)], axis=0)


def _dsa_kernel(qt_ref, qit_ref, wit_ref, ki_ref, k_ref, vt_ref, tri_ref, o_ref,
                s_ref, h_ref, lm_ref, p_ref, m_ref, acc_ref, *, TQ, TK, P, top):
    qb = pl.program_id(1)
    q0 = P + qb * TQ
    nk = (q0 + TQ + TK - 1) // TK
    n_full = q0 // TK
    NC = TK // LANES
    R = GROUPS * TQ
    q_lim = ((q0 + lax.broadcasted_iota(i32, (1, TQ), 1)) // CHUNK + 1) * CHUNK
    wit = wit_ref[0, 0] * IDX_SCALE

    def score_tile(masked, j, carry):
        rel = jnp.maximum(jnp.dot(ki_ref[0, j], qit_ref[0, 0], preferred_element_type=f32), 0.0)
        score = wit[0:1, :] * rel[:, 0:TQ]
        for h in range(1, IDX_HEADS):
            score = score + wit[h:h + 1, :] * rel[:, h * TQ:(h + 1) * TQ]
        if masked:
            k_pos = j * TK + lax.broadcasted_iota(i32, (TK, TQ), 0)
            score = jnp.where(k_pos < q_lim, score, NEG)
        bits = pltpu.bitcast(score, i32)
        key = jnp.where(bits >= 0, bits, bits ^ 0x7FFFFFFF)
        s_ref[j] = key
        h_ref[j] = (key >> 16).astype(jnp.int16)
        return carry

    lax.fori_loop(0, n_full, functools.partial(score_tile, False), 0)
    lax.fori_loop(n_full, nk, functools.partial(score_tile, True), 0)

    def count16(pred):
        def body(j, acc):
            hit = jnp.where(pred(h_ref[j]), jnp.int16(1), jnp.int16(0))
            parts = [hit[r * 16:(r + 1) * 16] for r in range(TK // 16)]
            while len(parts) > 1:
                parts = [a + b for a, b in zip(parts[0::2], parts[1::2])]
            return acc + parts[0]
        acc = lax.fori_loop(0, nk, body, jnp.zeros((16, TQ), jnp.int16))
        return jnp.sum(acc.astype(i32), axis=0, keepdims=True)

    def search16(want, early_exit):
        c0 = count16(lambda h: h >= jnp.int16(0))
        ok = c0 >= want
        v = jnp.where(ok, 0, -32768).astype(i32)
        cnt = jnp.where(ok, c0, jnp.int32(2 ** 30))

        def step(bit, v, cnt):
            cand = v | jnp.where(bit >= 0, jnp.left_shift(jnp.int32(1), jnp.maximum(bit, 0)), 0)
            c = count16(lambda h: h >= cand.astype(jnp.int16))
            ok = c >= want
            return jnp.where(ok, cand, v), jnp.where(ok, c, cnt)

        if not early_exit:
            return lax.fori_loop(0, 15, lambda i, st: step(14 - i, *st), (v, cnt))

        def body(st):
            bit, v, cnt, _ = st
            for s in range(SEARCH_STRIDE):
                v, cnt = step(bit - s, v, cnt)
            done = jnp.min(jnp.where(cnt == want, 1, 0))
            return bit - SEARCH_STRIDE, v, cnt, done

        cond = lambda st: jnp.logical_and(st[0] >= 0, st[3] == 0)
        _, v, cnt, _ = lax.while_loop(cond, body, (jnp.int32(14), v, cnt, jnp.int32(0)))
        return v, cnt

    t_hi, _ = search16(top, False)
    n_hi_gt = count16(lambda h: h > t_hi.astype(jnp.int16))

    def low_half(j, carry):
        key = s_ref[j]
        low = (key & 0xFFFF) - 32768
        h_ref[j] = jnp.where((key >> 16) == t_hi, low, -32768).astype(jnp.int16)
        return carry

    lax.fori_loop(0, nk, low_half, 0)
    t_lo, cnt_lo = search16(top - n_hi_gt, True)
    t = t_hi * 65536 + (t_lo + 32768)
    n_gt = n_hi_gt + count16(lambda h: h > t_lo.astype(jnp.int16))
    neg_key = np.array(NEG, np.float32).view(np.int32) ^ 0x7FFFFFFF
    need = jnp.where(t == neg_key, 0, top - n_gt).astype(f32)
    excess = jnp.logical_or(t_lo == -32768, n_hi_gt + cnt_lo > top)
    any_excess = jnp.max(jnp.where(excess, 1, 0)) > 0

    @pl.when(jnp.logical_not(any_excess))
    def _():
        thr = jnp.where(t == neg_key, t + 1, t)

        def fast(j, carry):
            s_ref[j] = pltpu.bitcast(jnp.where(s_ref[j] >= thr, 0.0, NEG), i32)
            return carry
        lax.fori_loop(0, nk, fast, 0)

    @pl.when(any_excess)
    def _():
        def ranked(j, taken):
            keys = s_ref[j]
            for c in range(NC):
                kc = keys[c * LANES:(c + 1) * LANES, :]
                eq = jnp.where(kc == t, 1.0, 0.0)
                rank = taken + jnp.dot(tri_ref[...], eq.astype(bf16), preferred_element_type=f32)
                pick = jnp.where(kc > t, 1.0, jnp.where(rank <= need, eq, 0.0))
                s_ref[j, c * LANES:(c + 1) * LANES, :] = pltpu.bitcast(
                    jnp.where(pick > 0.5, 0.0, NEG), i32)
                taken = taken + jnp.sum(eq, axis=0, keepdims=True)
            return taken
        lax.fori_loop(0, nk, ranked, jnp.zeros((1, TQ), f32))

    m_ref[...] = jnp.full(m_ref.shape, NEG, f32)
    acc_ref[...] = jnp.zeros(acc_ref.shape, f32)

    def logits(j, slot):
        jc = jnp.minimum(j, nk - 1)
        tile_max = []
        for g in range(KV_HEADS):
            cmax = jnp.full((8, R), NEG, f32)
            for c in range(NC):
                rows = slice(c * LANES, (c + 1) * LANES)
                bias = pltpu.bitcast(s_ref[jc, rows, :], f32)
                lm = (jnp.dot(k_ref[0, jc, g, rows, :], qt_ref[0, 0, g], preferred_element_type=f32)
                      + jnp.concatenate([bias] * GROUPS, axis=1))
                lm_ref[slot, g, rows, :] = lm
                cmax = jnp.maximum(cmax, jnp.max(lm.reshape(LANES // 8, 8, R), axis=0))
            tile_max.append(jnp.max(cmax, axis=0, keepdims=True))
        return tuple(tile_max)

    def accumulate(j, slot, tile_max):
        for g in range(KV_HEADS):
            m_old = m_ref[g]
            m_new = jnp.maximum(m_old, tile_max[g])
            m_ref[g] = m_new
            for c in range(NC):
                rows = slice(c * LANES, (c + 1) * LANES)
                p_ref[slot, g, rows, :] = jnp.exp2(lm_ref[slot, g, rows, :] - m_new).astype(bf16)
            acc_ref[g] = (jnp.exp2(m_old - m_new) * acc_ref[g]
                          + jnp.dot(vt_ref[0, j, g], p_ref[slot, g], preferred_element_type=f32))

    def tile_pair(i, tile_max):
        nxt = logits(2 * i + 1, 1)
        accumulate(2 * i, 0, tile_max)
        tile_max = logits(2 * i + 2, 0)
        accumulate(2 * i + 1, 1, nxt)
        return tile_max

    last_max = lax.fori_loop(0, nk // 2, tile_pair, logits(0, 0))

    @pl.when(nk % 2 == 1)
    def _():
        accumulate(nk - 1, 0, last_max)

    for g in range(KV_HEADS):
        o_ref[0, 0, g] = (acc_ref[g, 0:HEAD_DIM, :] / acc_ref[g, HEAD_DIM:HEAD_DIM + 1, :]).astype(bf16)


def _dsa(q, qi, wi, k_all, v_all, ki_all, P):
    B, S, _ = q.shape
    L = k_all.shape[1]
    top = min(TOPK_MAX, L // 4)
    TQ = _row_tile(S, 2 * CHUNK)
    TK = KEY_TILE
    nq = S // TQ
    NT = -(-L // TK)
    pad = NT * TK - L
    R = GROUPS * TQ
    assert top <= TK and TQ % CHUNK == 0

    def tiles(a):
        a = jnp.pad(a.astype(bf16), ((0, 0), (0, pad), (0, 0)))
        return a.reshape(B, NT, TK, a.shape[-1])

    ki_t = tiles(ki_all)
    kv_heads = lambda a: tiles(a).reshape(B, NT, TK, KV_HEADS, HEAD_DIM)
    k_t = kv_heads(k_all).transpose(0, 1, 3, 2, 4)
    vt = kv_heads(v_all).transpose(0, 1, 3, 4, 2)
    ones_row = jnp.zeros((B, NT, KV_HEADS, VT_ROWS - HEAD_DIM, TK), bf16).at[:, :, :, 0, :].set(1.0)
    vt = jnp.concatenate([vt, ones_row], axis=3)
    qt = q.reshape(B, nq, TQ, KV_HEADS, GROUPS, HEAD_DIM).transpose(0, 1, 3, 5, 4, 2)
    qt = qt.reshape(B, nq, KV_HEADS, HEAD_DIM, R)
    qit = qi.reshape(B, nq, TQ, IDX_HEADS, IDX_DIM).transpose(0, 1, 4, 3, 2)
    qit = qit.reshape(B, nq, IDX_DIM, IDX_HEADS * TQ)
    wit = wi.reshape(B, nq, TQ, IDX_HEADS).transpose(0, 1, 3, 2)
    tri = jnp.asarray(np.tril(np.ones((LANES, LANES), np.float32)), bf16)
    out = pl.pallas_call(
        functools.partial(_dsa_kernel, TQ=TQ, TK=TK, P=P, top=top),
        grid=(B, nq),
        in_specs=[pl.BlockSpec((1, 1, KV_HEADS, HEAD_DIM, R), lambda b, i: (b, i, 0, 0, 0)),
                  pl.BlockSpec((1, 1, IDX_DIM, IDX_HEADS * TQ), lambda b, i: (b, i, 0, 0)),
                  pl.BlockSpec((1, 1, IDX_HEADS, TQ), lambda b, i: (b, i, 0, 0)),
                  pl.BlockSpec((1, NT, TK, IDX_DIM), lambda b, i: (b, 0, 0, 0)),
                  pl.BlockSpec((1, NT, KV_HEADS, TK, HEAD_DIM), lambda b, i: (b, 0, 0, 0, 0)),
                  pl.BlockSpec((1, NT, KV_HEADS, VT_ROWS, TK), lambda b, i: (b, 0, 0, 0, 0)),
                  _const_spec((LANES, LANES))],
        out_specs=pl.BlockSpec((1, 1, KV_HEADS, HEAD_DIM, R), lambda b, i: (b, i, 0, 0, 0)),
        out_shape=jax.ShapeDtypeStruct((B, nq, KV_HEADS, HEAD_DIM, R), bf16),
        scratch_shapes=[pltpu.VMEM((NT, TK, TQ), i32),
                        pltpu.VMEM((NT, TK, TQ), jnp.int16),
                        pltpu.VMEM((2, KV_HEADS, TK, R), f32),
                        pltpu.VMEM((2, KV_HEADS, TK, R), bf16),
                        pltpu.VMEM((KV_HEADS, 1, R), f32),
                        pltpu.VMEM((KV_HEADS, VT_ROWS, R), f32)],
        compiler_params=_params(("parallel", "arbitrary")),
        name="dsa",
    )(qt, qit, wit, ki_t, k_t, vt, tri)
    out = out.reshape(B, nq, KV_HEADS, HEAD_DIM, GROUPS, TQ).transpose(0, 1, 5, 2, 4, 3)
    return out.reshape(B, S, N_HEADS * HEAD_DIM)


def _hgrn_mats(T):
    r = np.arange(T)[:, None]
    c = np.arange(T)[None, :]
    mats = [c <= r, c > r]
    b = T // 2
    while b >= 1:
        blk_r = r // (2 * b)
        mid = blk_r * 2 * b + b
        same = blk_r == c // (2 * b)
        late = r >= mid
        mats.append(same & ((late & (c >= mid) & (c <= r)) | (~late & (c > r) & (c < mid))))
        b //= 2
    return np.concatenate(mats, axis=0).astype(np.float32)


def _hgrn_kernel(x_ref, w_ref, lb_ref, g_ref, mats_ref, s0_ref, y_ref, st_out_ref, st_ref, *, T):
    t_idx = pl.program_id(1)

    @pl.when(t_idx == 0)
    def _():
        st_ref[...] = s0_ref[0]

    xb = x_ref[0].astype(bf16)
    proj = jnp.dot(xb, w_ref[...], preferred_element_type=f32)
    hq = proj[:, 0:HGRN_W]
    z = proj[:, HGRN_W:2 * HGRN_W]
    hv = proj[:, 2 * HGRN_W:3 * HGRN_W]
    hg = proj[:, 3 * HGRN_W:4 * HGRN_W]
    lb = lb_ref[...]
    logf = (jnp.minimum(z, 0.0) - jnp.log1p(jnp.exp(-jnp.abs(z)))) + jnp.log1p(lb * jnp.exp(-z))
    kk = (1.0 - lb) * _sigmoid(-z)
    qq = hq * _sigmoid(hq)
    gate = hg * _sigmoid(hg)

    hi = logf.astype(bf16)
    r1 = logf - hi.astype(f32)
    mid = r1.astype(bf16)
    lo = (r1 - mid.astype(f32)).astype(bf16)
    mats = mats_ref[...]
    e_all = (jnp.dot(mats, hi, preferred_element_type=f32)
             + jnp.dot(mats, mid, preferred_element_type=f32)
             + jnp.dot(mats, lo, preferred_element_type=f32))

    row = lax.broadcasted_iota(i32, (T, 1), 0)
    col = lax.broadcasted_iota(i32, (1, T), 1)
    nt = (((1,), (1,)), ((), ()))
    tn = (((0,), (0,)), ((), ()))
    for h in range(HGRN_HEADS):
        sl = slice(h * HGRN_DK, (h + 1) * HGRN_DK)
        q_h, k_h, v_h = qq[:, sl], kk[:, sl], hv[:, sl]
        vb = v_h.astype(bf16)
        cum = e_all[0:T, sl]
        suf = e_all[T:2 * T, sl]
        st = st_ref[h]
        o = lax.dot_general((q_h * jnp.exp(cum)).astype(bf16), st.astype(bf16), nt,
                            preferred_element_type=f32)
        o = o + jnp.sum(q_h * k_h, axis=-1, keepdims=True) * v_h
        scores = jnp.zeros((T, T), f32)
        b = T // 2
        lvl = 2
        while b >= 1:
            xdec = jnp.exp(e_all[lvl * T:(lvl + 1) * T, sl])
            late = ((row // b) % 2) == 1
            a_m = jnp.where(late, q_h * xdec, 0.0).astype(bf16)
            b_m = jnp.where(late, 0.0, k_h * xdec).astype(bf16)
            sc = lax.dot_general(a_m, b_m, nt, preferred_element_type=f32)
            if 2 * b < T:
                sc = jnp.where((row // (2 * b)) == (col // (2 * b)), sc, 0.0)
            scores = scores + sc
            b //= 2
            lvl += 1
        o = o + jnp.dot(scores.astype(bf16), vb, preferred_element_type=f32)
        kd = (k_h * jnp.exp(suf)).astype(bf16)
        st_ref[h] = (st * jnp.exp(cum[T - 1:T, :])
                     + lax.dot_general(vb, kd, tn, preferred_element_type=f32))
        o = o * lax.rsqrt(jnp.mean(o * o, axis=-1, keepdims=True) + LN_EPS) * g_ref[...]
        y_ref[0, :, sl] = (o * gate[:, sl]).astype(bf16)

    st_out_ref[0] = st_ref[...]


def _hgrn(x, w_h, lb, norm_g, s0):
    B, S, _ = x.shape
    T = _row_tile(S, 2 * CHUNK)
    mats = jnp.asarray(_hgrn_mats(T), bf16)
    nm = mats.shape[0]
    st0 = jnp.swapaxes(s0.astype(f32), 2, 3)
    y, st = pl.pallas_call(
        functools.partial(_hgrn_kernel, T=T),
        grid=(B, S // T),
        in_specs=[pl.BlockSpec((1, T, D_MODEL), lambda b, t: (b, t, 0)),
                  _const_spec((D_MODEL, 4 * HGRN_W)),
                  _const_spec((1, HGRN_W)),
                  _const_spec((1, HGRN_DV)),
                  _const_spec((nm, T)),
                  pl.BlockSpec((1, HGRN_HEADS, HGRN_DV, HGRN_DK), lambda b, t: (b, 0, 0, 0))],
        out_specs=[pl.BlockSpec((1, T, HGRN_W), lambda b, t: (b, t, 0)),
                   pl.BlockSpec((1, HGRN_HEADS, HGRN_DV, HGRN_DK), lambda b, t: (b, 0, 0, 0))],
        out_shape=[jax.ShapeDtypeStruct((B, S, HGRN_W), bf16),
                   jax.ShapeDtypeStruct((B, HGRN_HEADS, HGRN_DV, HGRN_DK), f32)],
        scratch_shapes=[pltpu.VMEM((HGRN_HEADS, HGRN_DV, HGRN_DK), f32)],
        compiler_params=_params(("parallel", "arbitrary")),
        name="hgrn",
    )(x, w_h, lb, norm_g, mats, st0)
    return y, jnp.swapaxes(st, 2, 3)


HIST = CONV_W - 1
PADROWS = 8


def _conv_rows(buf_ref, cols, u, w, bias, T):
    buf_ref[PADROWS:PADROWS + T, cols] = u
    y = bias
    for j in range(CONV_W):
        y = y + w[j:j + 1, :] * buf_ref[PADROWS - HIST + j:PADROWS - HIST + j + T, cols]
    buf_ref[PADROWS - HIST:PADROWS, cols] = u[T - HIST:T, :]
    return y


def _merge_kernel(x_ref, ya_ref, yb_ref, hist_ref, wc_ref, wg_ref, wbr_ref, wout_ref, cw_ref, cb_ref,
                  g_ref, b_ref, o_ref, hist_out_ref, buf_ref, *, T, alpha):
    @pl.when(pl.program_id(1) == 0)
    def _():
        buf_ref[PADROWS - HIST:PADROWS, :] = hist_ref[0]

    x = x_ref[0]
    xb = x.astype(bf16)
    c = jnp.dot(xb, wc_ref[...], preferred_element_type=f32)
    W = SCONV_WIDTH
    u = c[:, W:2 * W] * c[:, 2 * W:3 * W]
    u_conv = _conv_rows(buf_ref, slice(0, W), u, cw_ref[...], cb_ref[...], T)
    yc = (c[:, 0:W] * u_conv).astype(bf16)
    hist_out_ref[0] = buf_ref[PADROWS - HIST:PADROWS, :]

    merged = None
    for i, yb in enumerate((ya_ref[0], yb_ref[0], yc)):
        gate = _sigmoid(jnp.dot(xb, wg_ref[:, i * D_MODEL:(i + 1) * D_MODEL], preferred_element_type=f32))
        term = gate * jnp.dot(yb, wbr_ref[i], preferred_element_type=f32)
        merged = term if merged is None else merged + term
    out = jnp.dot(merged.astype(bf16), wout_ref[...], preferred_element_type=f32)
    o_ref[0] = _layer_norm(alpha * x + out, g_ref[...], b_ref[...])


def _merge(x, ya, yb, hist, wc, wg, wbr, wout, cw, cb, g, b, alpha):
    B, S, _ = x.shape
    T = _row_tile(S, 256)
    row = lambda w: pl.BlockSpec((1, T, w), lambda bb, t: (bb, t, 0))
    hspec = pl.BlockSpec((1, HIST, SCONV_WIDTH), lambda bb, t: (bb, 0, 0))
    return pl.pallas_call(
        functools.partial(_merge_kernel, T=T, alpha=alpha),
        grid=(B, S // T),
        in_specs=[row(D_MODEL), row(BRANCH_WIDTH), row(BRANCH_WIDTH), hspec,
                  _const_spec((D_MODEL, 3 * SCONV_WIDTH)), _const_spec((D_MODEL, N_BRANCH * D_MODEL)),
                  _const_spec((N_BRANCH, BRANCH_WIDTH, D_MODEL)), _const_spec((D_MODEL, D_MODEL)),
                  _const_spec((CONV_W, SCONV_WIDTH)), _const_spec((1, SCONV_WIDTH)),
                  _const_spec((1, D_MODEL)), _const_spec((1, D_MODEL))],
        out_specs=[row(D_MODEL), hspec],
        out_shape=[jax.ShapeDtypeStruct((B, S, D_MODEL), f32),
                   jax.ShapeDtypeStruct((B, HIST, SCONV_WIDTH), f32)],
        scratch_shapes=[pltpu.VMEM((PADROWS + T, SCONV_WIDTH), f32)],
        compiler_params=_params(("parallel", "arbitrary")),
        name="merge",
    )(x, ya, yb, hist, wc, wg, wbr, wout, cw, cb, g, b)


def _ffn_kernel(x_ref, hist_ref, wup_ref, cw_ref, cb_ref, wdn_ref, g_ref, b_ref, o_ref, hist_out_ref,
                buf_ref, *, T, alpha):
    @pl.when(pl.program_id(1) == 0)
    def _():
        buf_ref[PADROWS - HIST:PADROWS, :] = hist_ref[0]

    x = x_ref[0]
    xb = x.astype(bf16)
    acc = jnp.zeros((T, D_MODEL), f32)
    for cidx in range(D_FF // FFN_COLS):
        halves = []
        for off in (0, D_FF):
            cols = slice(off + cidx * FFN_COLS, off + (cidx + 1) * FFN_COLS)
            h = jnp.dot(xb, wup_ref[:, cols], preferred_element_type=f32)
            halves.append(_conv_rows(buf_ref, cols, h, cw_ref[:, cols], cb_ref[:, cols], T))
        a_g, b_v = halves
        act = (a_g * _sigmoid(a_g) * b_v).astype(bf16)
        acc = acc + jnp.dot(act, wdn_ref[cidx * FFN_COLS:(cidx + 1) * FFN_COLS, :],
                            preferred_element_type=f32)
    hist_out_ref[0] = buf_ref[PADROWS - HIST:PADROWS, :]
    o_ref[0] = _layer_norm(alpha * x + acc, g_ref[...], b_ref[...])


def _ffn(x, hist, wup, cw, cb, wdn, g, b, alpha):
    B, S, _ = x.shape
    T = _row_tile(S, 256)
    row = pl.BlockSpec((1, T, D_MODEL), lambda bb, t: (bb, t, 0))
    hspec = pl.BlockSpec((1, HIST, 2 * D_FF), lambda bb, t: (bb, 0, 0))
    return pl.pallas_call(
        functools.partial(_ffn_kernel, T=T, alpha=alpha),
        grid=(B, S // T),
        in_specs=[row, hspec, _const_spec((D_MODEL, 2 * D_FF)), _const_spec((CONV_W, 2 * D_FF)),
                  _const_spec((1, 2 * D_FF)), _const_spec((D_FF, D_MODEL)),
                  _const_spec((1, D_MODEL)), _const_spec((1, D_MODEL))],
        out_specs=[row, hspec],
        out_shape=[jax.ShapeDtypeStruct((B, S, D_MODEL), f32),
                   jax.ShapeDtypeStruct((B, HIST, 2 * D_FF), f32)],
        scratch_shapes=[pltpu.VMEM((PADROWS + T, 2 * D_FF), f32)],
        compiler_params=_params(("parallel", "arbitrary")),
        name="ffn",
    )(x, hist, wup, cw, cb, wdn, g, b)


def _layer_weights(l, depth, w_in, hgrn_lb_logits, hgrn_norm_g, sconv_w, sconv_b, w_branch, w_out,
                   ln1_g, ln1_b, w_up, ffn_conv_w, ffn_conv_b, w_down, ln2_g, ln2_b):
    offs = np.concatenate([[0], np.cumsum(IN_SIZES)]).tolist()
    sec = lambda i, j=None: w_in[l][:, offs[i]:offs[(i if j is None else j) + 1]]
    a_q, a_k, a_v, i_q, i_k, i_w = (sec(i) for i in range(6))
    zpad = lambda n: jnp.zeros((D_MODEL, n), w_in.dtype)
    w_ap = jnp.concatenate([a_q, a_k, i_q, i_k, zpad(ROT_COLS - 960), a_v, i_w,
                            zpad(AP_COLS - ROT_COLS - 128 - IDX_HEADS)], axis=1).astype(bf16)
    lbp = jax.nn.softmax(hgrn_lb_logits.astype(f32), axis=0)
    lb = (jnp.cumsum(lbp, axis=0) - lbp[0])[l].reshape(1, HGRN_W)
    row = lambda a: a.reshape(1, -1).astype(f32)
    return dict(
        w_ap=w_ap, w_h=sec(6, 9).astype(bf16), lb=lb, norm_g=row(hgrn_norm_g[l]),
        w_c=sec(10, 12).astype(bf16), w_g=sec(13).astype(bf16),
        w_br=w_branch[l].astype(bf16), w_out=w_out[l].astype(bf16),
        sconv_w=sconv_w[l].astype(f32), sconv_b=row(sconv_b[l]),
        ln1_g=row(ln1_g[l]), ln1_b=row(ln1_b[l]),
        w_up=w_up[l].astype(bf16), ffn_w=ffn_conv_w[l].astype(f32), ffn_b=row(ffn_conv_b[l]),
        w_down=w_down[l].astype(bf16), ln2_g=row(ln2_g[l]), ln2_b=row(ln2_b[l]))


def _trunk_layer(x, cs, P, k_past, v_past, ki_past, s0, sc_hist, ffn_hist, w, alpha):
    B, S, _ = x.shape
    q, k, v, qi, ki, wi = _attn_proj(x, w["w_ap"], cs)
    if k_past is None:
        k_all, v_all, ki_all = k, v, ki
    else:
        k_all = jnp.concatenate([k_past.reshape(B, P, KV_HEADS * HEAD_DIM), k], axis=1)
        v_all = jnp.concatenate([v_past.reshape(B, P, KV_HEADS * HEAD_DIM), v], axis=1)
        ki_all = jnp.concatenate([ki_past, ki], axis=1)
    y_a = _dsa(q, qi, wi, k_all, v_all, ki_all, P)
    y_b, s_new = _hgrn(x, w["w_h"], w["lb"], w["norm_g"], s0)
    x1, sc_new = _merge(x, y_a, y_b, sc_hist, w["w_c"], w["w_g"], w["w_br"], w["w_out"],
                        w["sconv_w"], w["sconv_b"], w["ln1_g"], w["ln1_b"], alpha)
    x2, ffn_new = _ffn(x1, ffn_hist, w["w_up"], w["ffn_w"], w["ffn_b"], w["w_down"],
                       w["ln2_g"], w["ln2_b"], alpha)
    new = (k.reshape(B, S, KV_HEADS, HEAD_DIM), v.reshape(B, S, KV_HEADS, HEAD_DIM), ki,
           s_new, sc_new, ffn_new)
    return x2, new


def kernel(x_prompt, x_sample, cache_attn_k, cache_attn_v, cache_idx_k, state_hgrn, state_sconv,
           state_ffn_conv, w_in, hgrn_lb_logits, hgrn_norm_g, sconv_w, sconv_b, w_branch, w_out,
           ln1_g, ln1_b, w_up, ffn_conv_w, ffn_conv_b, w_down, ln2_g, ln2_b):
    depth = w_in.shape[0]
    alpha = (2 * depth) ** 0.25
    B, S, _ = x_prompt.shape
    DB, DS, _ = x_sample.shape
    P = cache_attn_k.shape[2]
    cs_p = _rotary_tables(jnp.arange(S))
    cs_s = _rotary_tables(P + jnp.arange(DS))
    xp, xs = x_prompt, x_sample
    st_p = [[] for _ in range(6)]
    st_s = [[] for _ in range(6)]
    for l in range(depth):
        w = _layer_weights(l, depth, w_in, hgrn_lb_logits, hgrn_norm_g, sconv_w, sconv_b, w_branch,
                           w_out, ln1_g, ln1_b, w_up, ffn_conv_w, ffn_conv_b, w_down, ln2_g, ln2_b)
        xp, new_p = _trunk_layer(
            xp, cs_p, 0, None, None, None,
            jnp.zeros((B, HGRN_HEADS, HGRN_DK, HGRN_DV), f32),
            jnp.zeros((B, HIST, SCONV_WIDTH), f32),
            jnp.zeros((B, HIST, 2 * D_FF), f32), w, alpha)
        xs, new_s = _trunk_layer(
            xs, cs_s, P, cache_attn_k[l], cache_attn_v[l], cache_idx_k[l], state_hgrn[l],
            state_sconv[l], state_ffn_conv[l], w, alpha)
        for j in range(6):
            st_p[j].append(new_p[j])
            st_s[j].append(new_s[j])
    outs_p = [jnp.stack(a, axis=0) for a in st_p]
    outs_s = [jnp.stack(a, axis=0) for a in st_s]
    return (xp, xs, *outs_p, *outs_s)
```

```python
import functools

import numpy as np
import jax
import jax.numpy as jnp
from jax import lax
from jax.experimental import pallas as pl
from jax.experimental.pallas import tpu as pltpu

D_MODEL = 1024
CHUNK = 64
N_HEADS = 8
HEAD_DIM = 64
KV_HEADS = 2
GROUPS = N_HEADS // KV_HEADS
IDX_HEADS = 4
IDX_DIM = 64
TOPK_MAX = 256
ROPE_THETA = 500000.0
ATTN_SCALE = HEAD_DIM ** -0.5
IDX_SCALE = (IDX_DIM ** -0.5) * (IDX_HEADS ** -0.5)
NEG = -1e30
HGRN_HEADS = 4
HGRN_DK = 128
HGRN_DV = 128
HGRN_W = HGRN_HEADS * HGRN_DK
SCONV_WIDTH = 512
CONV_W = 3
BRANCH_WIDTH = 512
N_BRANCH = 3
D_FF = 2816
LN_EPS = 1e-5
IN_SIZES = (N_HEADS * HEAD_DIM, KV_HEADS * HEAD_DIM, KV_HEADS * HEAD_DIM,
            IDX_HEADS * IDX_DIM, IDX_DIM, IDX_HEADS,
            HGRN_W, HGRN_W, HGRN_W, HGRN_W,
            SCONV_WIDTH, SCONV_WIDTH, SCONV_WIDTH,
            N_BRANCH * D_MODEL)

LANES = 128
VMEM_LIMIT = 56 * 1024 * 1024
KEY_TILE = 512
SEARCH_STRIDE = 15
VT_ROWS = HEAD_DIM + 16
FFN_COLS = 256
INT_MIN = -2 ** 31

f32 = jnp.float32
bf16 = jnp.bfloat16
i32 = jnp.int32


def _row_tile(s, want):
    return want if s % want == 0 else s


def _const_spec(shape):
    nd = len(shape)
    return pl.BlockSpec(shape, lambda *_: (0,) * nd, pipeline_mode=pl.Buffered(1))


def _params(sem):
    return pltpu.CompilerParams(dimension_semantics=sem, vmem_limit_bytes=VMEM_LIMIT)


def _layer_norm(z, g, b):
    mu = jnp.mean(z, axis=-1, keepdims=True)
    d = z - mu
    var = jnp.mean(d * d, axis=-1, keepdims=True)
    return d * lax.rsqrt(var + LN_EPS) * g + b


def _sigmoid(x):
    return 1.0 / (1.0 + jnp.exp(-x))


Q_PRESCALE = ATTN_SCALE * float(np.log2(np.e))
ROT_COLS = 1024
AP_COLS = 1280


def _attn_proj_kernel(x_ref, w_ref, cs_ref, q_ref, k_ref, v_ref, qi_ref, ki_ref, wi_ref):
    xb = x_ref[0].astype(bf16)
    proj = jnp.dot(xb, w_ref[...], preferred_element_type=f32)
    c = cs_ref[0]
    sa = cs_ref[1]
    sb = cs_ref[2]

    def rot(g):
        xg = proj[:, g * LANES:(g + 1) * LANES]
        return xg * c + pltpu.roll(xg, LANES - 8, 1) * sa + pltpu.roll(xg, 8, 1) * sb

    for g in range(4):
        q_ref[0, :, g * LANES:(g + 1) * LANES] = (rot(g) * Q_PRESCALE).astype(bf16)
    k_ref[0] = rot(4)
    for g in range(2):
        qi_ref[0, :, g * LANES:(g + 1) * LANES] = rot(5 + g).astype(bf16)
    ki_ref[0] = rot(7)[:, :IDX_DIM]
    v_ref[0] = proj[:, ROT_COLS:ROT_COLS + LANES]
    wi_ref[0] = proj[:, ROT_COLS + LANES:ROT_COLS + LANES + IDX_HEADS]


def _attn_proj(x, w_ap, cs):
    B, S, _ = x.shape
    T = _row_tile(S, 256)
    row = lambda w: pl.BlockSpec((1, T, w), lambda b, t: (b, t, 0))
    return pl.pallas_call(
        _attn_proj_kernel,
        grid=(B, S // T),
        in_specs=[row(D_MODEL), _const_spec((D_MODEL, AP_COLS)),
                  pl.BlockSpec((3, T, LANES), lambda b, t: (0, t, 0))],
        out_specs=[row(512), row(128), row(128), row(256), row(IDX_DIM), row(IDX_HEADS)],
        out_shape=[jax.ShapeDtypeStruct((B, S, 512), bf16),
                   jax.ShapeDtypeStruct((B, S, 128), f32),
                   jax.ShapeDtypeStruct((B, S, 128), f32),
                   jax.ShapeDtypeStruct((B, S, 256), bf16),
                   jax.ShapeDtypeStruct((B, S, IDX_DIM), f32),
                   jax.ShapeDtypeStruct((B, S, IDX_HEADS), f32)],
        compiler_params=_params(("parallel", "parallel")),
        name="attn_proj",
    )(x, w_ap, cs)


def _rotary_tables(pos):
    rot = HEAD_DIM // 4
    half = rot // 2
    inv_freq = ROPE_THETA ** (-jnp.arange(half, dtype=f32) / half)
    ang = pos.astype(f32)[:, None] * inv_freq[None, :]
    cos = jnp.cos(ang)
    sin = jnp.sin(ang)
    S = pos.shape[0]
    one = jnp.ones((S, HEAD_DIM - rot), f32)
    zero = jnp.zeros((S, HEAD_DIM - rot), f32)
    zh = jnp.zeros((S, half), f32)
    c = jnp.concatenate([cos, cos, one], axis=1)
    sa = jnp.concatenate([-sin, zh, zero], axis=1)
    sb = jnp.concatenate([zh, sin, zero], axis=1)
    tile2 = lambda a: jnp.concatenate([a, a], axis=1)
    return jnp.stack([tile2(c), tile2(sa), tile2(sb)], axis=0)


def _dsa_kernel(qt_ref, qit_ref, wit_ref, ki_ref, k_ref, vt_ref, tri_ref, o_ref,
                s_ref, h_ref, lm_ref, p_ref, m_ref, acc_ref, *, TQ, TK, P, top):
    qb = pl.program_id(1)
    q0 = P + qb * TQ
    nk = (q0 + TQ + TK - 1) // TK
    n_full = q0 // TK
    NC = TK // LANES
    R = GROUPS * TQ
    q_lim = ((q0 + lax.broadcasted_iota(i32, (1, TQ), 1)) // CHUNK + 1) * CHUNK
    wit = wit_ref[0, 0] * IDX_SCALE

    def score_tile(masked, j, carry):
        rel = jnp.maximum(jnp.dot(ki_ref[0, j], qit_ref[0, 0], preferred_element_type=f32), 0.0)
        score = wit[0:1, :] * rel[:, 0:TQ]
        for h in range(1, IDX_HEADS):
            score = score + wit[h:h + 1, :] * rel[:, h * TQ:(h + 1) * TQ]
        if masked:
            k_pos = j * TK + lax.broadcasted_iota(i32, (TK, TQ), 0)
            score = jnp.where(k_pos < q_lim, score, NEG)
        bits = pltpu.bitcast(score, i32)
        key = jnp.where(bits >= 0, bits, bits ^ 0x7FFFFFFF)
        s_ref[j] = key
        h_ref[j] = (key >> 16).astype(jnp.int16)
        return carry

    lax.fori_loop(0, n_full, functools.partial(score_tile, False), 0)
    lax.fori_loop(n_full, nk, functools.partial(score_tile, True), 0)

    def count16(pred):
        def body(j, acc):
            hit = jnp.where(pred(h_ref[j]), jnp.int16(1), jnp.int16(0))
            parts = [hit[r * 16:(r + 1) * 16] for r in range(TK // 16)]
            while len(parts) > 1:
                parts = [a + b for a, b in zip(parts[0::2], parts[1::2])]
            return acc + parts[0]
        acc = lax.fori_loop(0, nk, body, jnp.zeros((16, TQ), jnp.int16))
        return jnp.sum(acc.astype(i32), axis=0, keepdims=True)

    def search16(want, early_exit):
        c0 = count16(lambda h: h >= jnp.int16(0))
        ok = c0 >= want
        v = jnp.where(ok, 0, -32768).astype(i32)
        cnt = jnp.where(ok, c0, jnp.int32(2 ** 30))

        def step(bit, v, cnt):
            cand = v | jnp.where(bit >= 0, jnp.left_shift(jnp.int32(1), jnp.maximum(bit, 0)), 0)
            c = count16(lambda h: h >= cand.astype(jnp.int16))
            ok = c >= want
            return jnp.where(ok, cand, v), jnp.where(ok, c, cnt)

        if not early_exit:
            return lax.fori_loop(0, 15, lambda i, st: step(14 - i, *st), (v, cnt))

        def body(st):
            bit, v, cnt, _ = st
            for s in range(SEARCH_STRIDE):
                v, cnt = step(bit - s, v, cnt)
            done = jnp.min(jnp.where(cnt == want, 1, 0))
            return bit - SEARCH_STRIDE, v, cnt, done

        cond = lambda st: jnp.logical_and(st[0] >= 0, st[3] == 0)
        _, v, cnt, _ = lax.while_loop(cond, body, (jnp.int32(14), v, cnt, jnp.int32(0)))
        return v, cnt

    t_hi, _ = search16(top, False)
    n_hi_gt = count16(lambda h: h > t_hi.astype(jnp.int16))

    def low_half(j, carry):
        key = s_ref[j]
        low = (key & 0xFFFF) - 32768
        h_ref[j] = jnp.where((key >> 16) == t_hi, low, -32768).astype(jnp.int16)
        return carry

    lax.fori_loop(0, nk, low_half, 0)
    t_lo, cnt_lo = search16(top - n_hi_gt, True)
    t = t_hi * 65536 + (t_lo + 32768)
    n_gt = n_hi_gt + count16(lambda h: h > t_lo.astype(jnp.int16))
    neg_key = np.array(NEG, np.float32).view(np.int32) ^ 0x7FFFFFFF
    need = jnp.where(t == neg_key, 0, top - n_gt).astype(f32)
    excess = jnp.logical_or(t_lo == -32768, n_hi_gt + cnt_lo > top)
    any_excess = jnp.max(jnp.where(excess, 1, 0)) > 0

    @pl.when(jnp.logical_not(any_excess))
    def _():
        thr = jnp.where(t == neg_key, t + 1, t)

        def fast(j, carry):
            s_ref[j] = pltpu.bitcast(jnp.where(s_ref[j] >= thr, 0.0, NEG), i32)
            return carry
        lax.fori_loop(0, nk, fast, 0)

    @pl.when(any_excess)
    def _():
        def ranked(j, taken):
            keys = s_ref[j]
            for c in range(NC):
                kc = keys[c * LANES:(c + 1) * LANES, :]
                eq = jnp.where(kc == t, 1.0, 0.0)
                rank = taken + jnp.dot(tri_ref[...], eq.astype(bf16), preferred_element_type=f32)
                pick = jnp.where(kc > t, 1.0, jnp.where(rank <= need, eq, 0.0))
                s_ref[j, c * LANES:(c + 1) * LANES, :] = pltpu.bitcast(
                    jnp.where(pick > 0.5, 0.0, NEG), i32)
                taken = taken + jnp.sum(eq, axis=0, keepdims=True)
            return taken
        lax.fori_loop(0, nk, ranked, jnp.zeros((1, TQ), f32))

    m_ref[...] = jnp.full(m_ref.shape, NEG, f32)
    acc_ref[...] = jnp.zeros(acc_ref.shape, f32)

    def logits(j, slot):
        jc = jnp.minimum(j, nk - 1)
        tile_max = []
        for g in range(KV_HEADS):
            cmax = jnp.full((8, R), NEG, f32)
            for c in range(NC):
                rows = slice(c * LANES, (c + 1) * LANES)
                bias = pltpu.bitcast(s_ref[jc, rows, :], f32)
                lm = (jnp.dot(k_ref[0, jc, g, rows, :], qt_ref[0, 0, g], preferred_element_type=f32)
                      + jnp.concatenate([bias] * GROUPS, axis=1))
                lm_ref[slot, g, rows, :] = lm
                cmax = jnp.maximum(cmax, jnp.max(lm.reshape(LANES // 8, 8, R), axis=0))
            tile_max.append(jnp.max(cmax, axis=0, keepdims=True))
        return tuple(tile_max)

    def accumulate(j, slot, tile_max):
        for g in range(KV_HEADS):
            m_old = m_ref[g]
            m_new = jnp.maximum(m_old, tile_max[g])
            m_ref[g] = m_new
            for c in range(NC):
                rows = slice(c * LANES, (c + 1) * LANES)
                p_ref[slot, g, rows, :] = jnp.exp2(lm_ref[slot, g, rows, :] - m_new).astype(bf16)
            acc_ref[g] = (jnp.exp2(m_old - m_new) * acc_ref[g]
                          + jnp.dot(vt_ref[0, j, g], p_ref[slot, g], preferred_element_type=f32))

    def tile_pair(i, tile_max):
        nxt = logits(2 * i + 1, 1)
        accumulate(2 * i, 0, tile_max)
        tile_max = logits(2 * i + 2, 0)
        accumulate(2 * i + 1, 1, nxt)
        return tile_max

    last_max = lax.fori_loop(0, nk // 2, tile_pair, logits(0, 0))

    @pl.when(nk % 2 == 1)
    def _():
        accumulate(nk - 1, 0, last_max)

    for g in range(KV_HEADS):
        o_ref[0, 0, g] = (acc_ref[g, 0:HEAD_DIM, :] / acc_ref[g, HEAD_DIM:HEAD_DIM + 1, :]).astype(bf16)


def _dsa(q, qi, wi, k_all, v_all, ki_all, P):
    B, S, _ = q.shape
    L = k_all.shape[1]
    top = min(TOPK_MAX, L // 4)
    TQ = _row_tile(S, 2 * CHUNK)
    TK = KEY_TILE
    nq = S // TQ
    NT = -(-L // TK)
    pad = NT * TK - L
    R = GROUPS * TQ
    assert top <= TK and TQ % CHUNK == 0

    def tiles(a):
        a = jnp.pad(a.astype(bf16), ((0, 0), (0, pad), (0, 0)))
        return a.reshape(B, NT, TK, a.shape[-1])

    ki_t = tiles(ki_all)
    kv_heads = lambda a: tiles(a).reshape(B, NT, TK, KV_HEADS, HEAD_DIM)
    k_t = kv_heads(k_all).transpose(0, 1, 3, 2, 4)
    vt = kv_heads(v_all).transpose(0, 1, 3, 4, 2)
    ones_row = jnp.zeros((B, NT, KV_HEADS, VT_ROWS - HEAD_DIM, TK), bf16).at[:, :, :, 0, :].set(1.0)
    vt = jnp.concatenate([vt, ones_row], axis=3)
    qt = q.reshape(B, nq, TQ, KV_HEADS, GROUPS, HEAD_DIM).transpose(0, 1, 3, 5, 4, 2)
    qt = qt.reshape(B, nq, KV_HEADS, HEAD_DIM, R)
    qit = qi.reshape(B, nq, TQ, IDX_HEADS, IDX_DIM).transpose(0, 1, 4, 3, 2)
    qit = qit.reshape(B, nq, IDX_DIM, IDX_HEADS * TQ)
    wit = wi.reshape(B, nq, TQ, IDX_HEADS).transpose(0, 1, 3, 2)
    tri = jnp.asarray(np.tril(np.ones((LANES, LANES), np.float32)), bf16)
    out = pl.pallas_call(
        functools.partial(_dsa_kernel, TQ=TQ, TK=TK, P=P, top=top),
        grid=(B, nq),
        in_specs=[pl.BlockSpec((1, 1, KV_HEADS, HEAD_DIM, R), lambda b, i: (b, i, 0, 0, 0)),
                  pl.BlockSpec((1, 1, IDX_DIM, IDX_HEADS * TQ), lambda b, i: (b, i, 0, 0)),
                  pl.BlockSpec((1, 1, IDX_HEADS, TQ), lambda b, i: (b, i, 0, 0)),
                  pl.BlockSpec((1, NT, TK, IDX_DIM), lambda b, i: (b, 0, 0, 0)),
                  pl.BlockSpec((1, NT, KV_HEADS, TK, HEAD_DIM), lambda b, i: (b, 0, 0, 0, 0)),
                  pl.BlockSpec((1, NT, KV_HEADS, VT_ROWS, TK), lambda b, i: (b, 0, 0, 0, 0)),
                  _const_spec((LANES, LANES))],
        out_specs=pl.BlockSpec((1, 1, KV_HEADS, HEAD_DIM, R), lambda b, i: (b, i, 0, 0, 0)),
        out_shape=jax.ShapeDtypeStruct((B, nq, KV_HEADS, HEAD_DIM, R), bf16),
        scratch_shapes=[pltpu.VMEM((NT, TK, TQ), i32),
                        pltpu.VMEM((NT, TK, TQ), jnp.int16),
                        pltpu.VMEM((2, KV_HEADS, TK, R), f32),
                        pltpu.VMEM((2, KV_HEADS, TK, R), bf16),
                        pltpu.VMEM((KV_HEADS, 1, R), f32),
                        pltpu.VMEM((KV_HEADS, VT_ROWS, R), f32)],
        compiler_params=_params(("parallel", "arbitrary")),
        name="dsa",
    )(qt, qit, wit, ki_t, k_t, vt, tri)
    out = out.reshape(B, nq, KV_HEADS, HEAD_DIM, GROUPS, TQ).transpose(0, 1, 5, 2, 4, 3)
    return out.reshape(B, S, N_HEADS * HEAD_DIM)


def _hgrn_mats(T):
    r = np.arange(T)[:, None]
    c = np.arange(T)[None, :]
    mats = [c <= r, c > r]
    b = T // 2
    while b >= 1:
        blk_r = r // (2 * b)
        mid = blk_r * 2 * b + b
        same = blk_r == c // (2 * b)
        late = r >= mid
        mats.append(same & ((late & (c >= mid) & (c <= r)) | (~late & (c > r) & (c < mid))))
        b //= 2
    return np.concatenate(mats, axis=0).astype(np.float32)


def _hgrn_kernel(x_ref, w_ref, lb_ref, g_ref, mats_ref, s0_ref, y_ref, st_out_ref, st_ref, *, T):
    t_idx = pl.program_id(1)

    @pl.when(t_idx == 0)
    def _():
        st_ref[...] = s0_ref[0]

    xb = x_ref[0].astype(bf16)
    proj = jnp.dot(xb, w_ref[...], preferred_element_type=f32)
    hq = proj[:, 0:HGRN_W]
    z = proj[:, HGRN_W:2 * HGRN_W]
    hv = proj[:, 2 * HGRN_W:3 * HGRN_W]
    hg = proj[:, 3 * HGRN_W:4 * HGRN_W]
    lb = lb_ref[...]
    logf = (jnp.minimum(z, 0.0) - jnp.log1p(jnp.exp(-jnp.abs(z)))) + jnp.log1p(lb * jnp.exp(-z))
    kk = (1.0 - lb) * _sigmoid(-z)
    qq = hq * _sigmoid(hq)
    gate = hg * _sigmoid(hg)

    hi = logf.astype(bf16)
    r1 = logf - hi.astype(f32)
    mid = r1.astype(bf16)
    lo = (r1 - mid.astype(f32)).astype(bf16)
    mats = mats_ref[...]
    e_all = (jnp.dot(mats, hi, preferred_element_type=f32)
             + jnp.dot(mats, mid, preferred_element_type=f32)
             + jnp.dot(mats, lo, preferred_element_type=f32))

    row = lax.broadcasted_iota(i32, (T, 1), 0)
    col = lax.broadcasted_iota(i32, (1, T), 1)
    nt = (((1,), (1,)), ((), ()))
    tn = (((0,), (0,)), ((), ()))
    for h in range(HGRN_HEADS):
        sl = slice(h * HGRN_DK, (h + 1) * HGRN_DK)
        q_h, k_h, v_h = qq[:, sl], kk[:, sl], hv[:, sl]
        vb = v_h.astype(bf16)
        cum = e_all[0:T, sl]
        suf = e_all[T:2 * T, sl]
        st = st_ref[h]
        o = lax.dot_general((q_h * jnp.exp(cum)).astype(bf16), st.astype(bf16), nt,
                            preferred_element_type=f32)
        o = o + jnp.sum(q_h * k_h, axis=-1, keepdims=True) * v_h
        scores = jnp.zeros((T, T), f32)
        b = T // 2
        lvl = 2
        while b >= 1:
            xdec = jnp.exp(e_all[lvl * T:(lvl + 1) * T, sl])
            late = ((row // b) % 2) == 1
            a_m = jnp.where(late, q_h * xdec, 0.0).astype(bf16)
            b_m = jnp.where(late, 0.0, k_h * xdec).astype(bf16)
            sc = lax.dot_general(a_m, b_m, nt, preferred_element_type=f32)
            if 2 * b < T:
                sc = jnp.where((row // (2 * b)) == (col // (2 * b)), sc, 0.0)
            scores = scores + sc
            b //= 2
            lvl += 1
        o = o + jnp.dot(scores.astype(bf16), vb, preferred_element_type=f32)
        kd = (k_h * jnp.exp(suf)).astype(bf16)
        st_ref[h] = (st * jnp.exp(cum[T - 1:T, :])
                     + lax.dot_general(vb, kd, tn, preferred_element_type=f32))
        o = o * lax.rsqrt(jnp.mean(o * o, axis=-1, keepdims=True) + LN_EPS) * g_ref[...]
        y_ref[0, :, sl] = (o * gate[:, sl]).astype(bf16)

    st_out_ref[0] = st_ref[...]


def _hgrn(x, w_h, lb, norm_g, s0):
    B, S, _ = x.shape
    T = _row_tile(S, 2 * CHUNK)
    mats = jnp.asarray(_hgrn_mats(T), bf16)
    nm = mats.shape[0]
    st0 = jnp.swapaxes(s0.astype(f32), 2, 3)
    y, st = pl.pallas_call(
        functools.partial(_hgrn_kernel, T=T),
        grid=(B, S // T),
        in_specs=[pl.BlockSpec((1, T, D_MODEL), lambda b, t: (b, t, 0)),
                  _const_spec((D_MODEL, 4 * HGRN_W)),
                  _const_spec((1, HGRN_W)),
                  _const_spec((1, HGRN_DV)),
                  _const_spec((nm, T)),
                  pl.BlockSpec((1, HGRN_HEADS, HGRN_DV, HGRN_DK), lambda b, t: (b, 0, 0, 0))],
        out_specs=[pl.BlockSpec((1, T, HGRN_W), lambda b, t: (b, t, 0)),
                   pl.BlockSpec((1, HGRN_HEADS, HGRN_DV, HGRN_DK), lambda b, t: (b, 0, 0, 0))],
        out_shape=[jax.ShapeDtypeStruct((B, S, HGRN_W), bf16),
                   jax.ShapeDtypeStruct((B, HGRN_HEADS, HGRN_DV, HGRN_DK), f32)],
        scratch_shapes=[pltpu.VMEM((HGRN_HEADS, HGRN_DV, HGRN_DK), f32)],
        compiler_params=_params(("parallel", "arbitrary")),
        name="hgrn",
    )(x, w_h, lb, norm_g, mats, st0)
    return y, jnp.swapaxes(st, 2, 3)


HIST = CONV_W - 1
PADROWS = 8


def _conv_rows(buf_ref, cols, u, w, bias, T):
    buf_ref[PADROWS:PADROWS + T, cols] = u
    y = bias
    for j in range(CONV_W):
        y = y + w[j:j + 1, :] * buf_ref[PADROWS - HIST + j:PADROWS - HIST + j + T, cols]
    buf_ref[PADROWS - HIST:PADROWS, cols] = u[T - HIST:T, :]
    return y


def _merge_kernel(x_ref, ya_ref, yb_ref, hist_ref, wc_ref, wg_ref, wbr_ref, wout_ref, cw_ref, cb_ref,
                  g_ref, b_ref, o_ref, hist_out_ref, buf_ref, *, T, alpha):
    @pl.when(pl.program_id(1) == 0)
    def _():
        buf_ref[PADROWS - HIST:PADROWS, :] = hist_ref[0]

    x = x_ref[0]
    xb = x.astype(bf16)
    c = jnp.dot(xb, wc_ref[...], preferred_element_type=f32)
    W = SCONV_WIDTH
    u = c[:, W:2 * W] * c[:, 2 * W:3 * W]
    u_conv = _conv_rows(buf_ref, slice(0, W), u, cw_ref[...], cb_ref[...], T)
    yc = (c[:, 0:W] * u_conv).astype(bf16)
    hist_out_ref[0] = buf_ref[PADROWS - HIST:PADROWS, :]

    merged = None
    for i, yb in enumerate((ya_ref[0], yb_ref[0], yc)):
        gate = _sigmoid(jnp.dot(xb, wg_ref[:, i * D_MODEL:(i + 1) * D_MODEL], preferred_element_type=f32))
        term = gate * jnp.dot(yb, wbr_ref[i], preferred_element_type=f32)
        merged = term if merged is None else merged + term
    out = jnp.dot(merged.astype(bf16), wout_ref[...], preferred_element_type=f32)
    o_ref[0] = _layer_norm(alpha * x + out, g_ref[...], b_ref[...])


def _merge(x, ya, yb, hist, wc, wg, wbr, wout, cw, cb, g, b, alpha):
    B, S, _ = x.shape
    T = _row_tile(S, 256)
    row = lambda w: pl.BlockSpec((1, T, w), lambda bb, t: (bb, t, 0))
    hspec = pl.BlockSpec((1, HIST, SCONV_WIDTH), lambda bb, t: (bb, 0, 0))
    return pl.pallas_call(
        functools.partial(_merge_kernel, T=T, alpha=alpha),
        grid=(B, S // T),
        in_specs=[row(D_MODEL), row(BRANCH_WIDTH), row(BRANCH_WIDTH), hspec,
                  _const_spec((D_MODEL, 3 * SCONV_WIDTH)), _const_spec((D_MODEL, N_BRANCH * D_MODEL)),
                  _const_spec((N_BRANCH, BRANCH_WIDTH, D_MODEL)), _const_spec((D_MODEL, D_MODEL)),
                  _const_spec((CONV_W, SCONV_WIDTH)), _const_spec((1, SCONV_WIDTH)),
                  _const_spec((1, D_MODEL)), _const_spec((1, D_MODEL))],
        out_specs=[row(D_MODEL), hspec],
        out_shape=[jax.ShapeDtypeStruct((B, S, D_MODEL), f32),
                   jax.ShapeDtypeStruct((B, HIST, SCONV_WIDTH), f32)],
        scratch_shapes=[pltpu.VMEM((PADROWS + T, SCONV_WIDTH), f32)],
        compiler_params=_params(("parallel", "arbitrary")),
        name="merge",
    )(x, ya, yb, hist, wc, wg, wbr, wout, cw, cb, g, b)


def _ffn_kernel(x_ref, hist_ref, wup_ref, cw_ref, cb_ref, wdn_ref, g_ref, b_ref, o_ref, hist_out_ref,
                buf_ref, *, T, alpha):
    @pl.when(pl.program_id(1) == 0)
    def _():
        buf_ref[PADROWS - HIST:PADROWS, :] = hist_ref[0]

    x = x_ref[0]
    xb = x.astype(bf16)
    acc = jnp.zeros((T, D_MODEL), f32)
    for cidx in range(D_FF // FFN_COLS):
        halves = []
        for off in (0, D_FF):
            cols = slice(off + cidx * FFN_COLS, off + (cidx + 1) * FFN_COLS)
            h = jnp.dot(xb, wup_ref[:, cols], preferred_element_type=f32)
            halves.append(_conv_rows(buf_ref, cols, h, cw_ref[:, cols], cb_ref[:, cols], T))
        a_g, b_v = halves
        act = (a_g * _sigmoid(a_g) * b_v).astype(bf16)
        acc = acc + jnp.dot(act, wdn_ref[cidx * FFN_COLS:(cidx + 1) * FFN_COLS, :],
                            preferred_element_type=f32)
    hist_out_ref[0] = buf_ref[PADROWS - HIST:PADROWS, :]
    o_ref[0] = _layer_norm(alpha * x + acc, g_ref[...], b_ref[...])


def _ffn(x, hist, wup, cw, cb, wdn, g, b, alpha):
    B, S, _ = x.shape
    T = _row_tile(S, 256)
    row = pl.BlockSpec((1, T, D_MODEL), lambda bb, t: (bb, t, 0))
    hspec = pl.BlockSpec((1, HIST, 2 * D_FF), lambda bb, t: (bb, 0, 0))
    return pl.pallas_call(
        functools.partial(_ffn_kernel, T=T, alpha=alpha),
        grid=(B, S // T),
        in_specs=[row, hspec, _const_spec((D_MODEL, 2 * D_FF)), _const_spec((CONV_W, 2 * D_FF)),
                  _const_spec((1, 2 * D_FF)), _const_spec((D_FF, D_MODEL)),
                  _const_spec((1, D_MODEL)), _const_spec((1, D_MODEL))],
        out_specs=[row, hspec],
        out_shape=[jax.ShapeDtypeStruct((B, S, D_MODEL), f32),
                   jax.ShapeDtypeStruct((B, HIST, 2 * D_FF), f32)],
        scratch_shapes=[pltpu.VMEM((PADROWS + T, 2 * D_FF), f32)],
        compiler_params=_params(("parallel", "arbitrary")),
        name="ffn",
    )(x, hist, wup, cw, cb, wdn, g, b)


def _layer_weights(l, depth, w_in, hgrn_lb_logits, hgrn_norm_g, sconv_w, sconv_b, w_branch, w_out,
                   ln1_g, ln1_b, w_up, ffn_conv_w, ffn_conv_b, w_down, ln2_g, ln2_b):
    offs = np.concatenate([[0], np.cumsum(IN_SIZES)]).tolist()
    sec = lambda i, j=None: w_in[l][:, offs[i]:offs[(i if j is None else j) + 1]]
    a_q, a_k, a_v, i_q, i_k, i_w = (sec(i) for i in range(6))
    zpad = lambda n: jnp.zeros((D_MODEL, n), w_in.dtype)
    w_ap = jnp.concatenate([a_q, a_k, i_q, i_k, zpad(ROT_COLS - 960), a_v, i_w,
                            zpad(AP_COLS - ROT_COLS - 128 - IDX_HEADS)], axis=1).astype(bf16)
    lbp = jax.nn.softmax(hgrn_lb_logits.astype(f32), axis=0)
    lb = (jnp.cumsum(lbp, axis=0) - lbp[0])[l].reshape(1, HGRN_W)
    row = lambda a: a.reshape(1, -1).astype(f32)
    return dict(
        w_ap=w_ap, w_h=sec(6, 9).astype(bf16), lb=lb, norm_g=row(hgrn_norm_g[l]),
        w_c=sec(10, 12).astype(bf16), w_g=sec(13).astype(bf16),
        w_br=w_branch[l].astype(bf16), w_out=w_out[l].astype(bf16),
        sconv_w=sconv_w[l].astype(f32), sconv_b=row(sconv_b[l]),
        ln1_g=row(ln1_g[l]), ln1_b=row(ln1_b[l]),
        w_up=w_up[l].astype(bf16), ffn_w=ffn_conv_w[l].astype(f32), ffn_b=row(ffn_conv_b[l]),
        w_down=w_down[l].astype(bf16), ln2_g=row(ln2_g[l]), ln2_b=row(ln2_b[l]))


def _trunk_layer(x, cs, P, k_past, v_past, ki_past, s0, sc_hist, ffn_hist, w, alpha):
    B, S, _ = x.shape
    q, k, v, qi, ki, wi = _attn_proj(x, w["w_ap"], cs)
    if k_past is None:
        k_all, v_all, ki_all = k, v, ki
    else:
        k_all = jnp.concatenate([k_past.reshape(B, P, KV_HEADS * HEAD_DIM), k], axis=1)
        v_all = jnp.concatenate([v_past.reshape(B, P, KV_HEADS * HEAD_DIM), v], axis=1)
        ki_all = jnp.concatenate([ki_past, ki], axis=1)
    y_a = _dsa(q, qi, wi, k_all, v_all, ki_all, P)
    y_b, s_new = _hgrn(x, w["w_h"], w["lb"], w["norm_g"], s0)
    x1, sc_new = _merge(x, y_a, y_b, sc_hist, w["w_c"], w["w_g"], w["w_br"], w["w_out"],
                        w["sconv_w"], w["sconv_b"], w["ln1_g"], w["ln1_b"], alpha)
    x2, ffn_new = _ffn(x1, ffn_hist, w["w_up"], w["ffn_w"], w["ffn_b"], w["w_down"],
                       w["ln2_g"], w["ln2_b"], alpha)
    new = (k.reshape(B, S, KV_HEADS, HEAD_DIM), v.reshape(B, S, KV_HEADS, HEAD_DIM), ki,
           s_new, sc_new, ffn_new)
    return x2, new


def kernel(x_prompt, x_sample, cache_attn_k, cache_attn_v, cache_idx_k, state_hgrn, state_sconv,
           state_ffn_conv, w_in, hgrn_lb_logits, hgrn_norm_g, sconv_w, sconv_b, w_branch, w_out,
           ln1_g, ln1_b, w_up, ffn_conv_w, ffn_conv_b, w_down, ln2_g, ln2_b):
    depth = w_in.shape[0]
    alpha = (2 * depth) ** 0.25
    B, S, _ = x_prompt.shape
    DB, DS, _ = x_sample.shape
    P = cache_attn_k.shape[2]
    cs_p = _rotary_tables(jnp.arange(S))
    cs_s = _rotary_tables(P + jnp.arange(DS))
    xp, xs = x_prompt, x_sample
    st_p = [[] for _ in range(6)]
    st_s = [[] for _ in range(6)]
    for l in range(depth):
        w = _layer_weights(l, depth, w_in, hgrn_lb_logits, hgrn_norm_g, sconv_w, sconv_b, w_branch,
                           w_out, ln1_g, ln1_b, w_up, ffn_conv_w, ffn_conv_b, w_down, ln2_g, ln2_b)
        xp, new_p = _trunk_layer(
            xp, cs_p, 0, None, None, None,
            jnp.zeros((B, HGRN_HEADS, HGRN_DK, HGRN_DV), f32),
            jnp.zeros((B, HIST, SCONV_WIDTH), f32),
            jnp.zeros((B, HIST, 2 * D_FF), f32), w, alpha)
        xs, new_s = _trunk_layer(
            xs, cs_s, P, cache_attn_k[l], cache_attn_v[l], cache_idx_k[l], state_hgrn[l],
            state_sconv[l], state_ffn_conv[l], w, alpha)
        for j in range(6):
            st_p[j].append(new_p[j])
            st_s[j].append(new_s[j])
    outs_p = [jnp.stack(a, axis=0) for a in st_p]
    outs_s = [jnp.stack(a, axis=0) for a in st_s]
    return (xp, xs, *outs_p, *outs_s)
```

```python
import functools

import numpy as np
import jax
import jax.numpy as jnp
from jax import lax
from jax.experimental import pallas as pl
from jax.experimental.pallas import tpu as pltpu

D_MODEL = 1024
CHUNK = 64
N_HEADS = 8
HEAD_DIM = 64
KV_HEADS = 2
GROUPS = N_HEADS // KV_HEADS
IDX_HEADS = 4
IDX_DIM = 64
TOPK_MAX = 256
ROPE_THETA = 500000.0
ATTN_SCALE = HEAD_DIM ** -0.5
IDX_SCALE = (IDX_DIM ** -0.5) * (IDX_HEADS ** -0.5)
NEG = -1e30
HGRN_HEADS = 4
HGRN_DK = 128
HGRN_DV = 128
HGRN_W = HGRN_HEADS * HGRN_DK
SCONV_WIDTH = 512
CONV_W = 3
BRANCH_WIDTH = 512
N_BRANCH = 3
D_FF = 2816
LN_EPS = 1e-5
IN_SIZES = (N_HEADS * HEAD_DIM, KV_HEADS * HEAD_DIM, KV_HEADS * HEAD_DIM,
            IDX_HEADS * IDX_DIM, IDX_DIM, IDX_HEADS,
            HGRN_W, HGRN_W, HGRN_W, HGRN_W,
            SCONV_WIDTH, SCONV_WIDTH, SCONV_WIDTH,
            N_BRANCH * D_MODEL)

LANES = 128
VMEM_LIMIT = 56 * 1024 * 1024
KEY_TILE = 512
HALF = jnp.int32
HALF_ROWS = 8
SEARCH_STRIDE = 15
VT_ROWS = HEAD_DIM + 16
FFN_COLS = 256
INT_MIN = -2 ** 31

f32 = jnp.float32
bf16 = jnp.bfloat16
i32 = jnp.int32


def _row_tile(s, want):
    return want if s % want == 0 else s


def _const_spec(shape):
    nd = len(shape)
    return pl.BlockSpec(shape, lambda *_: (0,) * nd, pipeline_mode=pl.Buffered(1))


def _params(sem):
    return pltpu.CompilerParams(dimension_semantics=sem, vmem_limit_bytes=VMEM_LIMIT)


def _layer_norm(z, g, b):
    mu = jnp.mean(z, axis=-1, keepdims=True)
    d = z - mu
    var = jnp.mean(d * d, axis=-1, keepdims=True)
    return d * lax.rsqrt(var + LN_EPS) * g + b


def _sigmoid(x):
    return 1.0 / (1.0 + jnp.exp(-x))


Q_PRESCALE = ATTN_SCALE * float(np.log2(np.e))
ROT_COLS = 1024
AP_COLS = 1280


def _attn_proj_kernel(x_ref, w_ref, cs_ref, q_ref, k_ref, v_ref, qi_ref, ki_ref, wi_ref):
    xb = x_ref[0].astype(bf16)
    proj = jnp.dot(xb, w_ref[...], preferred_element_type=f32)
    c = cs_ref[0]
    sa = cs_ref[1]
    sb = cs_ref[2]

    def rot(g):
        xg = proj[:, g * LANES:(g + 1) * LANES]
        return xg * c + pltpu.roll(xg, LANES - 8, 1) * sa + pltpu.roll(xg, 8, 1) * sb

    for g in range(4):
        q_ref[0, :, g * LANES:(g + 1) * LANES] = (rot(g) * Q_PRESCALE).astype(bf16)
    k_ref[0] = rot(4)
    for g in range(2):
        qi_ref[0, :, g * LANES:(g + 1) * LANES] = rot(5 + g).astype(bf16)
    ki_ref[0] = rot(7)[:, :IDX_DIM]
    v_ref[0] = proj[:, ROT_COLS:ROT_COLS + LANES]
    wi_ref[0] = proj[:, ROT_COLS + LANES:ROT_COLS + LANES + IDX_HEADS]


def _attn_proj(x, w_ap, cs):
    B, S, _ = x.shape
    T = _row_tile(S, 256)
    row = lambda w: pl.BlockSpec((1, T, w), lambda b, t: (b, t, 0))
    return pl.pallas_call(
        _attn_proj_kernel,
        grid=(B, S // T),
        in_specs=[row(D_MODEL), _const_spec((D_MODEL, AP_COLS)),
                  pl.BlockSpec((3, T, LANES), lambda b, t: (0, t, 0))],
        out_specs=[row(512), row(128), row(128), row(256), row(IDX_DIM), row(IDX_HEADS)],
        out_shape=[jax.ShapeDtypeStruct((B, S, 512), bf16),
                   jax.ShapeDtypeStruct((B, S, 128), f32),
                   jax.ShapeDtypeStruct((B, S, 128), f32),
                   jax.ShapeDtypeStruct((B, S, 256), bf16),
                   jax.ShapeDtypeStruct((B, S, IDX_DIM), f32),
                   jax.ShapeDtypeStruct((B, S, IDX_HEADS), f32)],
        compiler_params=_params(("parallel", "parallel")),
        name="attn_proj",
    )(x, w_ap, cs)


def _rotary_tables(pos):
    rot = HEAD_DIM // 4
    half = rot // 2
    inv_freq = ROPE_THETA ** (-jnp.arange(half, dtype=f32) / half)
    ang = pos.astype(f32)[:, None] * inv_freq[None, :]
    cos = jnp.cos(ang)
    sin = jnp.sin(ang)
    S = pos.shape[0]
    one = jnp.ones((S, HEAD_DIM - rot), f32)
    zero = jnp.zeros((S, HEAD_DIM - rot), f32)
    zh = jnp.zeros((S, half), f32)
    c = jnp.concatenate([cos, cos, one], axis=1)
    sa = jnp.concatenate([-sin, zh, zero], axis=1)
    sb = jnp.concatenate([zh, sin, zero], axis=1)
    tile2 = lambda a: jnp.concatenate([a, a], axis=1)
    return jnp.stack([tile2(c), tile2(sa), tile2(sb)], axis=0)


def _dsa_kernel(qt_ref, qit_ref, wit_ref, ki_ref, k_ref, vt_ref, tri_ref, o_ref,
                s_ref, h_ref, lm_ref, p_ref, m_ref, acc_ref, *, TQ, TK, P, top):
    qb = pl.program_id(1)
    q0 = P + qb * TQ
    nk = (q0 + TQ + TK - 1) // TK
    n_full = q0 // TK
    NC = TK // LANES
    R = GROUPS * TQ
    q_lim = ((q0 + lax.broadcasted_iota(i32, (1, TQ), 1)) // CHUNK + 1) * CHUNK
    wit = wit_ref[0, 0] * IDX_SCALE

    def score_tile(masked, j, carry):
        rel = jnp.maximum(jnp.dot(ki_ref[0, j], qit_ref[0, 0], preferred_element_type=f32), 0.0)
        score = wit[0:1, :] * rel[:, 0:TQ]
        for h in range(1, IDX_HEADS):
            score = score + wit[h:h + 1, :] * rel[:, h * TQ:(h + 1) * TQ]
        if masked:
            k_pos = j * TK + lax.broadcasted_iota(i32, (TK, TQ), 0)
            score = jnp.where(k_pos < q_lim, score, NEG)
        bits = pltpu.bitcast(score, i32)
        key = jnp.where(bits >= 0, bits, bits ^ 0x7FFFFFFF)
        s_ref[j] = key
        h_ref[j] = (key >> 16).astype(HALF)
        return carry

    lax.fori_loop(0, n_full, functools.partial(score_tile, False), 0)
    lax.fori_loop(n_full, nk, functools.partial(score_tile, True), 0)

    def count16(pred):
        def body(j, acc):
            hit = jnp.where(pred(h_ref[j]), HALF(1), HALF(0))
            parts = [hit[r * HALF_ROWS:(r + 1) * HALF_ROWS] for r in range(TK // HALF_ROWS)]
            while len(parts) > 1:
                parts = [a + b for a, b in zip(parts[0::2], parts[1::2])]
            return acc + parts[0]
        acc = lax.fori_loop(0, nk, body, jnp.zeros((HALF_ROWS, TQ), HALF))
        return jnp.sum(acc.astype(i32), axis=0, keepdims=True)

    def search16(want, early_exit):
        c0 = count16(lambda h: h >= HALF(0))
        ok = c0 >= want
        v = jnp.where(ok, 0, -32768).astype(i32)
        cnt = jnp.where(ok, c0, jnp.int32(2 ** 30))

        def step(bit, v, cnt):
            cand = v | jnp.where(bit >= 0, jnp.left_shift(jnp.int32(1), jnp.maximum(bit, 0)), 0)
            c = count16(lambda h: h >= cand.astype(HALF))
            ok = c >= want
            return jnp.where(ok, cand, v), jnp.where(ok, c, cnt)

        if not early_exit:
            return lax.fori_loop(0, 15, lambda i, st: step(14 - i, *st), (v, cnt))

        def body(st):
            bit, v, cnt, _ = st
            for s in range(SEARCH_STRIDE):
                v, cnt = step(bit - s, v, cnt)
            done = jnp.min(jnp.where(cnt == want, 1, 0))
            return bit - SEARCH_STRIDE, v, cnt, done

        cond = lambda st: jnp.logical_and(st[0] >= 0, st[3] == 0)
        _, v, cnt, _ = lax.while_loop(cond, body, (jnp.int32(14), v, cnt, jnp.int32(0)))
        return v, cnt

    t_hi, _ = search16(top, False)
    n_hi_gt = count16(lambda h: h > t_hi.astype(HALF))

    def low_half(j, carry):
        key = s_ref[j]
        low = (key & 0xFFFF) - 32768
        h_ref[j] = jnp.where((key >> 16) == t_hi, low, -32768).astype(HALF)
        return carry

    lax.fori_loop(0, nk, low_half, 0)
    t_lo, cnt_lo = search16(top - n_hi_gt, True)
    t = t_hi * 65536 + (t_lo + 32768)
    n_gt = n_hi_gt + count16(lambda h: h > t_lo.astype(HALF))
    neg_key = np.array(NEG, np.float32).view(np.int32) ^ 0x7FFFFFFF
    need = jnp.where(t == neg_key, 0, top - n_gt).astype(f32)
    excess = jnp.logical_or(t_lo == -32768, n_hi_gt + cnt_lo > top)
    any_excess = jnp.max(jnp.where(excess, 1, 0)) > 0

    @pl.when(jnp.logical_not(any_excess))
    def _():
        thr = jnp.where(t == neg_key, t + 1, t)

        def fast(j, carry):
            s_ref[j] = pltpu.bitcast(jnp.where(s_ref[j] >= thr, 0.0, NEG), i32)
            return carry
        lax.fori_loop(0, nk, fast, 0)

    @pl.when(any_excess)
    def _():
        def ranked(j, taken):
            keys = s_ref[j]
            for c in range(NC):
                kc = keys[c * LANES:(c + 1) * LANES, :]
                eq = jnp.where(kc == t, 1.0, 0.0)
                rank = taken + jnp.dot(tri_ref[...], eq.astype(bf16), preferred_element_type=f32)
                pick = jnp.where(kc > t, 1.0, jnp.where(rank <= need, eq, 0.0))
                s_ref[j, c * LANES:(c + 1) * LANES, :] = pltpu.bitcast(
                    jnp.where(pick > 0.5, 0.0, NEG), i32)
                taken = taken + jnp.sum(eq, axis=0, keepdims=True)
            return taken
        lax.fori_loop(0, nk, ranked, jnp.zeros((1, TQ), f32))

    m_ref[...] = jnp.full(m_ref.shape, NEG, f32)
    acc_ref[...] = jnp.zeros(acc_ref.shape, f32)

    def logits(j, slot):
        jc = jnp.minimum(j, nk - 1)
        tile_max = []
        for g in range(KV_HEADS):
            cmax = jnp.full((8, R), NEG, f32)
            for c in range(NC):
                rows = slice(c * LANES, (c + 1) * LANES)
                bias = pltpu.bitcast(s_ref[jc, rows, :], f32)
                lm = (jnp.dot(k_ref[0, jc, g, rows, :], qt_ref[0, 0, g], preferred_element_type=f32)
                      + jnp.concatenate([bias] * GROUPS, axis=1))
                lm_ref[slot, g, rows, :] = lm
                cmax = jnp.maximum(cmax, jnp.max(lm.reshape(LANES // 8, 8, R), axis=0))
            tile_max.append(jnp.max(cmax, axis=0, keepdims=True))
        return tuple(tile_max)

    def accumulate(j, slot, tile_max):
        for g in range(KV_HEADS):
            m_old = m_ref[g]
            m_new = jnp.maximum(m_old, tile_max[g])
            m_ref[g] = m_new
            for c in range(NC):
                rows = slice(c * LANES, (c + 1) * LANES)
                p_ref[slot, g, rows, :] = jnp.exp2(lm_ref[slot, g, rows, :] - m_new).astype(bf16)
            acc_ref[g] = (jnp.exp2(m_old - m_new) * acc_ref[g]
                          + jnp.dot(vt_ref[0, j, g], p_ref[slot, g], preferred_element_type=f32))

    def tile_pair(i, tile_max):
        nxt = logits(2 * i + 1, 1)
        accumulate(2 * i, 0, tile_max)
        tile_max = logits(2 * i + 2, 0)
        accumulate(2 * i + 1, 1, nxt)
        return tile_max

    last_max = lax.fori_loop(0, nk // 2, tile_pair, logits(0, 0))

    @pl.when(nk % 2 == 1)
    def _():
        accumulate(nk - 1, 0, last_max)

    for g in range(KV_HEADS):
        o_ref[0, 0, g] = (acc_ref[g, 0:HEAD_DIM, :] / acc_ref[g, HEAD_DIM:HEAD_DIM + 1, :]).astype(bf16)


def _dsa(q, qi, wi, k_all, v_all, ki_all, P):
    B, S, _ = q.shape
    L = k_all.shape[1]
    top = min(TOPK_MAX, L // 4)
    TQ = _row_tile(S, 2 * CHUNK)
    TK = KEY_TILE
    nq = S // TQ
    NT = -(-L // TK)
    pad = NT * TK - L
    R = GROUPS * TQ
    assert top <= TK and TQ % CHUNK == 0

    def tiles(a):
        a = jnp.pad(a.astype(bf16), ((0, 0), (0, pad), (0, 0)))
        return a.reshape(B, NT, TK, a.shape[-1])

    ki_t = tiles(ki_all)
    kv_heads = lambda a: tiles(a).reshape(B, NT, TK, KV_HEADS, HEAD_DIM)
    k_t = kv_heads(k_all).transpose(0, 1, 3, 2, 4)
    vt = kv_heads(v_all).transpose(0, 1, 3, 4, 2)
    ones_row = jnp.zeros((B, NT, KV_HEADS, VT_ROWS - HEAD_DIM, TK), bf16).at[:, :, :, 0, :].set(1.0)
    vt = jnp.concatenate([vt, ones_row], axis=3)
    qt = q.reshape(B, nq, TQ, KV_HEADS, GROUPS, HEAD_DIM).transpose(0, 1, 3, 5, 4, 2)
    qt = qt.reshape(B, nq, KV_HEADS, HEAD_DIM, R)
    qit = qi.reshape(B, nq, TQ, IDX_HEADS, IDX_DIM).transpose(0, 1, 4, 3, 2)
    qit = qit.reshape(B, nq, IDX_DIM, IDX_HEADS * TQ)
    wit = wi.reshape(B, nq, TQ, IDX_HEADS).transpose(0, 1, 3, 2)
    tri = jnp.asarray(np.tril(np.ones((LANES, LANES), np.float32)), bf16)
    out = pl.pallas_call(
        functools.partial(_dsa_kernel, TQ=TQ, TK=TK, P=P, top=top),
        grid=(B, nq),
        in_specs=[pl.BlockSpec((1, 1, KV_HEADS, HEAD_DIM, R), lambda b, i: (b, i, 0, 0, 0)),
                  pl.BlockSpec((1, 1, IDX_DIM, IDX_HEADS * TQ), lambda b, i: (b, i, 0, 0)),
                  pl.BlockSpec((1, 1, IDX_HEADS, TQ), lambda b, i: (b, i, 0, 0)),
                  pl.BlockSpec((1, NT, TK, IDX_DIM), lambda b, i: (b, 0, 0, 0)),
                  pl.BlockSpec((1, NT, KV_HEADS, TK, HEAD_DIM), lambda b, i: (b, 0, 0, 0, 0)),
                  pl.BlockSpec((1, NT, KV_HEADS, VT_ROWS, TK), lambda b, i: (b, 0, 0, 0, 0)),
                  _const_spec((LANES, LANES))],
        out_specs=pl.BlockSpec((1, 1, KV_HEADS, HEAD_DIM, R), lambda b, i: (b, i, 0, 0, 0)),
        out_shape=jax.ShapeDtypeStruct((B, nq, KV_HEADS, HEAD_DIM, R), bf16),
        scratch_shapes=[pltpu.VMEM((NT, TK, TQ), i32),
                        pltpu.VMEM((NT, TK, TQ), HALF),
                        pltpu.VMEM((2, KV_HEADS, TK, R), f32),
                        pltpu.VMEM((2, KV_HEADS, TK, R), bf16),
                        pltpu.VMEM((KV_HEADS, 1, R), f32),
                        pltpu.VMEM((KV_HEADS, VT_ROWS, R), f32)],
        compiler_params=_params(("parallel", "arbitrary")),
        name="dsa",
    )(qt, qit, wit, ki_t, k_t, vt, tri)
    out = out.reshape(B, nq, KV_HEADS, HEAD_DIM, GROUPS, TQ).transpose(0, 1, 5, 2, 4, 3)
    return out.reshape(B, S, N_HEADS * HEAD_DIM)


def _hgrn_mats(T):
    r = np.arange(T)[:, None]
    c = np.arange(T)[None, :]
    mats = [c <= r, c > r]
    b = T // 2
    while b >= 1:
        blk_r = r // (2 * b)
        mid = blk_r * 2 * b + b
        same = blk_r == c // (2 * b)
        late = r >= mid
        mats.append(same & ((late & (c >= mid) & (c <= r)) | (~late & (c > r) & (c < mid))))
        b //= 2
    return np.concatenate(mats, axis=0).astype(np.float32)


def _hgrn_kernel(x_ref, w_ref, lb_ref, g_ref, mats_ref, s0_ref, y_ref, st_out_ref, st_ref, *, T):
    t_idx = pl.program_id(1)

    @pl.when(t_idx == 0)
    def _():
        st_ref[...] = s0_ref[0]

    xb = x_ref[0].astype(bf16)
    proj = jnp.dot(xb, w_ref[...], preferred_element_type=f32)
    hq = proj[:, 0:HGRN_W]
    z = proj[:, HGRN_W:2 * HGRN_W]
    hv = proj[:, 2 * HGRN_W:3 * HGRN_W]
    hg = proj[:, 3 * HGRN_W:4 * HGRN_W]
    lb = lb_ref[...]
    logf = (jnp.minimum(z, 0.0) - jnp.log1p(jnp.exp(-jnp.abs(z)))) + jnp.log1p(lb * jnp.exp(-z))
    kk = (1.0 - lb) * _sigmoid(-z)
    qq = hq * _sigmoid(hq)
    gate = hg * _sigmoid(hg)

    hi = logf.astype(bf16)
    r1 = logf - hi.astype(f32)
    mid = r1.astype(bf16)
    lo = (r1 - mid.astype(f32)).astype(bf16)
    mats = mats_ref[...]
    e_all = (jnp.dot(mats, hi, preferred_element_type=f32)
             + jnp.dot(mats, mid, preferred_element_type=f32)
             + jnp.dot(mats, lo, preferred_element_type=f32))

    row = lax.broadcasted_iota(i32, (T, 1), 0)
    col = lax.broadcasted_iota(i32, (1, T), 1)
    nt = (((1,), (1,)), ((), ()))
    tn = (((0,), (0,)), ((), ()))
    for h in range(HGRN_HEADS):
        sl = slice(h * HGRN_DK, (h + 1) * HGRN_DK)
        q_h, k_h, v_h = qq[:, sl], kk[:, sl], hv[:, sl]
        vb = v_h.astype(bf16)
        cum = e_all[0:T, sl]
        suf = e_all[T:2 * T, sl]
        st = st_ref[h]
        o = lax.dot_general((q_h * jnp.exp(cum)).astype(bf16), st.astype(bf16), nt,
                            preferred_element_type=f32)
        o = o + jnp.sum(q_h * k_h, axis=-1, keepdims=True) * v_h
        scores = jnp.zeros((T, T), f32)
        b = T // 2
        lvl = 2
        while b >= 1:
            xdec = jnp.exp(e_all[lvl * T:(lvl + 1) * T, sl])
            late = ((row // b) % 2) == 1
            a_m = jnp.where(late, q_h * xdec, 0.0).astype(bf16)
            b_m = jnp.where(late, 0.0, k_h * xdec).astype(bf16)
            sc = lax.dot_general(a_m, b_m, nt, preferred_element_type=f32)
            if 2 * b < T:
                sc = jnp.where((row // (2 * b)) == (col // (2 * b)), sc, 0.0)
            scores = scores + sc
            b //= 2
            lvl += 1
        o = o + jnp.dot(scores.astype(bf16), vb, preferred_element_type=f32)
        kd = (k_h * jnp.exp(suf)).astype(bf16)
        st_ref[h] = (st * jnp.exp(cum[T - 1:T, :])
                     + lax.dot_general(vb, kd, tn, preferred_element_type=f32))
        o = o * lax.rsqrt(jnp.mean(o * o, axis=-1, keepdims=True) + LN_EPS) * g_ref[...]
        y_ref[0, :, sl] = (o * gate[:, sl]).astype(bf16)

    st_out_ref[0] = st_ref[...]


def _hgrn(x, w_h, lb, norm_g, s0):
    B, S, _ = x.shape
    T = _row_tile(S, 2 * CHUNK)
    mats = jnp.asarray(_hgrn_mats(T), bf16)
    nm = mats.shape[0]
    st0 = jnp.swapaxes(s0.astype(f32), 2, 3)
    y, st = pl.pallas_call(
        functools.partial(_hgrn_kernel, T=T),
        grid=(B, S // T),
        in_specs=[pl.BlockSpec((1, T, D_MODEL), lambda b, t: (b, t, 0)),
                  _const_spec((D_MODEL, 4 * HGRN_W)),
                  _const_spec((1, HGRN_W)),
                  _const_spec((1, HGRN_DV)),
                  _const_spec((nm, T)),
                  pl.BlockSpec((1, HGRN_HEADS, HGRN_DV, HGRN_DK), lambda b, t: (b, 0, 0, 0))],
        out_specs=[pl.BlockSpec((1, T, HGRN_W), lambda b, t: (b, t, 0)),
                   pl.BlockSpec((1, HGRN_HEADS, HGRN_DV, HGRN_DK), lambda b, t: (b, 0, 0, 0))],
        out_shape=[jax.ShapeDtypeStruct((B, S, HGRN_W), bf16),
                   jax.ShapeDtypeStruct((B, HGRN_HEADS, HGRN_DV, HGRN_DK), f32)],
        scratch_shapes=[pltpu.VMEM((HGRN_HEADS, HGRN_DV, HGRN_DK), f32)],
        compiler_params=_params(("parallel", "arbitrary")),
        name="hgrn",
    )(x, w_h, lb, norm_g, mats, st0)
    return y, jnp.swapaxes(st, 2, 3)


HIST = CONV_W - 1
PADROWS = 8


def _conv_rows(buf_ref, cols, u, w, bias, T):
    buf_ref[PADROWS:PADROWS + T, cols] = u
    y = bias
    for j in range(CONV_W):
        y = y + w[j:j + 1, :] * buf_ref[PADROWS - HIST + j:PADROWS - HIST + j + T, cols]
    buf_ref[PADROWS - HIST:PADROWS, cols] = u[T - HIST:T, :]
    return y


def _merge_kernel(x_ref, ya_ref, yb_ref, hist_ref, wc_ref, wg_ref, wbr_ref, wout_ref, cw_ref, cb_ref,
                  g_ref, b_ref, o_ref, hist_out_ref, buf_ref, *, T, alpha):
    @pl.when(pl.program_id(1) == 0)
    def _():
        buf_ref[PADROWS - HIST:PADROWS, :] = hist_ref[0]

    x = x_ref[0]
    xb = x.astype(bf16)
    c = jnp.dot(xb, wc_ref[...], preferred_element_type=f32)
    W = SCONV_WIDTH
    u = c[:, W:2 * W] * c[:, 2 * W:3 * W]
    u_conv = _conv_rows(buf_ref, slice(0, W), u, cw_ref[...], cb_ref[...], T)
    yc = (c[:, 0:W] * u_conv).astype(bf16)
    hist_out_ref[0] = buf_ref[PADROWS - HIST:PADROWS, :]

    merged = None
    for i, yb in enumerate((ya_ref[0], yb_ref[0], yc)):
        gate = _sigmoid(jnp.dot(xb, wg_ref[:, i * D_MODEL:(i + 1) * D_MODEL], preferred_element_type=f32))
        term = gate * jnp.dot(yb, wbr_ref[i], preferred_element_type=f32)
        merged = term if merged is None else merged + term
    out = jnp.dot(merged.astype(bf16), wout_ref[...], preferred_element_type=f32)
    o_ref[0] = _layer_norm(alpha * x + out, g_ref[...], b_ref[...])


def _merge(x, ya, yb, hist, wc, wg, wbr, wout, cw, cb, g, b, alpha):
    B, S, _ = x.shape
    T = _row_tile(S, 256)
    row = lambda w: pl.BlockSpec((1, T, w), lambda bb, t: (bb, t, 0))
    hspec = pl.BlockSpec((1, HIST, SCONV_WIDTH), lambda bb, t: (bb, 0, 0))
    return pl.pallas_call(
        functools.partial(_merge_kernel, T=T, alpha=alpha),
        grid=(B, S // T),
        in_specs=[row(D_MODEL), row(BRANCH_WIDTH), row(BRANCH_WIDTH), hspec,
                  _const_spec((D_MODEL, 3 * SCONV_WIDTH)), _const_spec((D_MODEL, N_BRANCH * D_MODEL)),
                  _const_spec((N_BRANCH, BRANCH_WIDTH, D_MODEL)), _const_spec((D_MODEL, D_MODEL)),
                  _const_spec((CONV_W, SCONV_WIDTH)), _const_spec((1, SCONV_WIDTH)),
                  _const_spec((1, D_MODEL)), _const_spec((1, D_MODEL))],
        out_specs=[row(D_MODEL), hspec],
        out_shape=[jax.ShapeDtypeStruct((B, S, D_MODEL), f32),
                   jax.ShapeDtypeStruct((B, HIST, SCONV_WIDTH), f32)],
        scratch_shapes=[pltpu.VMEM((PADROWS + T, SCONV_WIDTH), f32)],
        compiler_params=_params(("parallel", "arbitrary")),
        name="merge",
    )(x, ya, yb, hist, wc, wg, wbr, wout, cw, cb, g, b)


def _ffn_kernel(x_ref, hist_ref, wup_ref, cw_ref, cb_ref, wdn_ref, g_ref, b_ref, o_ref, hist_out_ref,
                buf_ref, *, T, alpha):
    @pl.when(pl.program_id(1) == 0)
    def _():
        buf_ref[PADROWS - HIST:PADROWS, :] = hist_ref[0]

    x = x_ref[0]
    xb = x.astype(bf16)
    acc = jnp.zeros((T, D_MODEL), f32)
    for cidx in range(D_FF // FFN_COLS):
        halves = []
        for off in (0, D_FF):
            cols = slice(off + cidx * FFN_COLS, off + (cidx + 1) * FFN_COLS)
            h = jnp.dot(xb, wup_ref[:, cols], preferred_element_type=f32)
            halves.append(_conv_rows(buf_ref, cols, h, cw_ref[:, cols], cb_ref[:, cols], T))
        a_g, b_v = halves
        act = (a_g * _sigmoid(a_g) * b_v).astype(bf16)
        acc = acc + jnp.dot(act, wdn_ref[cidx * FFN_COLS:(cidx + 1) * FFN_COLS, :],
                            preferred_element_type=f32)
    hist_out_ref[0] = buf_ref[PADROWS - HIST:PADROWS, :]
    o_ref[0] = _layer_norm(alpha * x + acc, g_ref[...], b_ref[...])


def _ffn(x, hist, wup, cw, cb, wdn, g, b, alpha):
    B, S, _ = x.shape
    T = _row_tile(S, 256)
    row = pl.BlockSpec((1, T, D_MODEL), lambda bb, t: (bb, t, 0))
    hspec = pl.BlockSpec((1, HIST, 2 * D_FF), lambda bb, t: (bb, 0, 0))
    return pl.pallas_call(
        functools.partial(_ffn_kernel, T=T, alpha=alpha),
        grid=(B, S // T),
        in_specs=[row, hspec, _const_spec((D_MODEL, 2 * D_FF)), _const_spec((CONV_W, 2 * D_FF)),
                  _const_spec((1, 2 * D_FF)), _const_spec((D_FF, D_MODEL)),
                  _const_spec((1, D_MODEL)), _const_spec((1, D_MODEL))],
        out_specs=[row, hspec],
        out_shape=[jax.ShapeDtypeStruct((B, S, D_MODEL), f32),
                   jax.ShapeDtypeStruct((B, HIST, 2 * D_FF), f32)],
        scratch_shapes=[pltpu.VMEM((PADROWS + T, 2 * D_FF), f32)],
        compiler_params=_params(("parallel", "arbitrary")),
        name="ffn",
    )(x, hist, wup, cw, cb, wdn, g, b)


def _layer_weights(l, depth, w_in, hgrn_lb_logits, hgrn_norm_g, sconv_w, sconv_b, w_branch, w_out,
                   ln1_g, ln1_b, w_up, ffn_conv_w, ffn_conv_b, w_down, ln2_g, ln2_b):
    offs = np.concatenate([[0], np.cumsum(IN_SIZES)]).tolist()
    sec = lambda i, j=None: w_in[l][:, offs[i]:offs[(i if j is None else j) + 1]]
    a_q, a_k, a_v, i_q, i_k, i_w = (sec(i) for i in range(6))
    zpad = lambda n: jnp.zeros((D_MODEL, n), w_in.dtype)
    w_ap = jnp.concatenate([a_q, a_k, i_q, i_k, zpad(ROT_COLS - 960), a_v, i_w,
                            zpad(AP_COLS - ROT_COLS - 128 - IDX_HEADS)], axis=1).astype(bf16)
    lbp = jax.nn.softmax(hgrn_lb_logits.astype(f32), axis=0)
    lb = (jnp.cumsum(lbp, axis=0) - lbp[0])[l].reshape(1, HGRN_W)
    row = lambda a: a.reshape(1, -1).astype(f32)
    return dict(
        w_ap=w_ap, w_h=sec(6, 9).astype(bf16), lb=lb, norm_g=row(hgrn_norm_g[l]),
        w_c=sec(10, 12).astype(bf16), w_g=sec(13).astype(bf16),
        w_br=w_branch[l].astype(bf16), w_out=w_out[l].astype(bf16),
        sconv_w=sconv_w[l].astype(f32), sconv_b=row(sconv_b[l]),
        ln1_g=row(ln1_g[l]), ln1_b=row(ln1_b[l]),
        w_up=w_up[l].astype(bf16), ffn_w=ffn_conv_w[l].astype(f32), ffn_b=row(ffn_conv_b[l]),
        w_down=w_down[l].astype(bf16), ln2_g=row(ln2_g[l]), ln2_b=row(ln2_b[l]))


def _trunk_layer(x, cs, P, k_past, v_past, ki_past, s0, sc_hist, ffn_hist, w, alpha):
    B, S, _ = x.shape
    q, k, v, qi, ki, wi = _attn_proj(x, w["w_ap"], cs)
    if k_past is None:
        k_all, v_all, ki_all = k, v, ki
    else:
        k_all = jnp.concatenate([k_past.reshape(B, P, KV_HEADS * HEAD_DIM), k], axis=1)
        v_all = jnp.concatenate([v_past.reshape(B, P, KV_HEADS * HEAD_DIM), v], axis=1)
        ki_all = jnp.concatenate([ki_past, ki], axis=1)
    y_a = _dsa(q, qi, wi, k_all, v_all, ki_all, P)
    y_b, s_new = _hgrn(x, w["w_h"], w["lb"], w["norm_g"], s0)
    x1, sc_new = _merge(x, y_a, y_b, sc_hist, w["w_c"], w["w_g"], w["w_br"], w["w_out"],
                        w["sconv_w"], w["sconv_b"], w["ln1_g"], w["ln1_b"], alpha)
    x2, ffn_new = _ffn(x1, ffn_hist, w["w_up"], w["ffn_w"], w["ffn_b"], w["w_down"],
                       w["ln2_g"], w["ln2_b"], alpha)
    new = (k.reshape(B, S, KV_HEADS, HEAD_DIM), v.reshape(B, S, KV_HEADS, HEAD_DIM), ki,
           s_new, sc_new, ffn_new)
    return x2, new


def kernel(x_prompt, x_sample, cache_attn_k, cache_attn_v, cache_idx_k, state_hgrn, state_sconv,
           state_ffn_conv, w_in, hgrn_lb_logits, hgrn_norm_g, sconv_w, sconv_b, w_branch, w_out,
           ln1_g, ln1_b, w_up, ffn_conv_w, ffn_conv_b, w_down, ln2_g, ln2_b):
    depth = w_in.shape[0]
    alpha = (2 * depth) ** 0.25
    B, S, _ = x_prompt.shape
    DB, DS, _ = x_sample.shape
    P = cache_attn_k.shape[2]
    cs_p = _rotary_tables(jnp.arange(S))
    cs_s = _rotary_tables(P + jnp.arange(DS))
    xp, xs = x_prompt, x_sample
    st_p = [[] for _ in range(6)]
    st_s = [[] for _ in range(6)]
    for l in range(depth):
        w = _layer_weights(l, depth, w_in, hgrn_lb_logits, hgrn_norm_g, sconv_w, sconv_b, w_branch,
                           w_out, ln1_g, ln1_b, w_up, ffn_conv_w, ffn_conv_b, w_down, ln2_g, ln2_b)
        xp, new_p = _trunk_layer(
            xp, cs_p, 0, None, None, None,
            jnp.zeros((B, HGRN_HEADS, HGRN_DK, HGRN_DV), f32),
            jnp.zeros((B, HIST, SCONV_WIDTH), f32),
            jnp.zeros((B, HIST, 2 * D_FF), f32), w, alpha)
        xs, new_s = _trunk_layer(
            xs, cs_s, P, cache_attn_k[l], cache_attn_v[l], cache_idx_k[l], state_hgrn[l],
            state_sconv[l], state_ffn_conv[l], w, alpha)
        for j in range(6):
            st_p[j].append(new_p[j])
            st_s[j].append(new_s[j])
    outs_p = [jnp.stack(a, axis=0) for a in st_p]
    outs_s = [jnp.stack(a, axis=0) for a in st_s]
    return (xp, xs, *outs_p, *outs_s)
```

```python
import functools

import numpy as np
import jax
import jax.numpy as jnp
from jax import lax
from jax.experimental import pallas as pl
from jax.experimental.pallas import tpu as pltpu

D_MODEL = 1024
CHUNK = 64
N_HEADS = 8
HEAD_DIM = 64
KV_HEADS = 2
GROUPS = N_HEADS // KV_HEADS
IDX_HEADS = 4
IDX_DIM = 64
TOPK_MAX = 256
ROPE_THETA = 500000.0
ATTN_SCALE = HEAD_DIM ** -0.5
IDX_SCALE = (IDX_DIM ** -0.5) * (IDX_HEADS ** -0.5)
NEG = -1e30
HGRN_HEADS = 4
HGRN_DK = 128
HGRN_DV = 128
HGRN_W = HGRN_HEADS * HGRN_DK
SCONV_WIDTH = 512
CONV_W = 3
BRANCH_WIDTH = 512
N_BRANCH = 3
D_FF = 2816
LN_EPS = 1e-5
IN_SIZES = (N_HEADS * HEAD_DIM, KV_HEADS * HEAD_DIM, KV_HEADS * HEAD_DIM,
            IDX_HEADS * IDX_DIM, IDX_DIM, IDX_HEADS,
            HGRN_W, HGRN_W, HGRN_W, HGRN_W,
            SCONV_WIDTH, SCONV_WIDTH, SCONV_WIDTH,
            N_BRANCH * D_MODEL)

LANES = 128
VMEM_LIMIT = 56 * 1024 * 1024
KEY_TILE = 512
SEARCH_CHECKED = 24
SEARCH_STRIDE = 8
VT_ROWS = HEAD_DIM + 16
FFN_COLS = 256
INT_MIN = -2 ** 31

f32 = jnp.float32
bf16 = jnp.bfloat16
i32 = jnp.int32


def _row_tile(s, want):
    return want if s % want == 0 else s


def _const_spec(shape):
    nd = len(shape)
    return pl.BlockSpec(shape, lambda *_: (0,) * nd, pipeline_mode=pl.Buffered(1))


def _params(sem):
    return pltpu.CompilerParams(dimension_semantics=sem, vmem_limit_bytes=VMEM_LIMIT)


def _layer_norm(z, g, b):
    mu = jnp.mean(z, axis=-1, keepdims=True)
    d = z - mu
    var = jnp.mean(d * d, axis=-1, keepdims=True)
    return d * lax.rsqrt(var + LN_EPS) * g + b


def _sigmoid(x):
    return 1.0 / (1.0 + jnp.exp(-x))


Q_PRESCALE = ATTN_SCALE * float(np.log2(np.e))
ROT_COLS = 1024
AP_COLS = 1280


def _attn_proj_kernel(x_ref, w_ref, cs_ref, q_ref, k_ref, v_ref, qi_ref, ki_ref, wi_ref):
    xb = x_ref[0].astype(bf16)
    proj = jnp.dot(xb, w_ref[...], preferred_element_type=f32)
    c = cs_ref[0]
    sa = cs_ref[1]
    sb = cs_ref[2]

    def rot(g):
        xg = proj[:, g * LANES:(g + 1) * LANES]
        return xg * c + pltpu.roll(xg, LANES - 8, 1) * sa + pltpu.roll(xg, 8, 1) * sb

    for g in range(4):
        q_ref[0, :, g * LANES:(g + 1) * LANES] = (rot(g) * Q_PRESCALE).astype(bf16)
    k_ref[0] = rot(4)
    for g in range(2):
        qi_ref[0, :, g * LANES:(g + 1) * LANES] = rot(5 + g).astype(bf16)
    ki_ref[0] = rot(7)[:, :IDX_DIM]
    v_ref[0] = proj[:, ROT_COLS:ROT_COLS + LANES]
    wi_ref[0] = proj[:, ROT_COLS + LANES:ROT_COLS + LANES + IDX_HEADS]


def _attn_proj(x, w_ap, cs):
    B, S, _ = x.shape
    T = _row_tile(S, 256)
    row = lambda w: pl.BlockSpec((1, T, w), lambda b, t: (b, t, 0))
    return pl.pallas_call(
        _attn_proj_kernel,
        grid=(B, S // T),
        in_specs=[row(D_MODEL), _const_spec((D_MODEL, AP_COLS)),
                  pl.BlockSpec((3, T, LANES), lambda b, t: (0, t, 0))],
        out_specs=[row(512), row(128), row(128), row(256), row(IDX_DIM), row(IDX_HEADS)],
        out_shape=[jax.ShapeDtypeStruct((B, S, 512), bf16),
                   jax.ShapeDtypeStruct((B, S, 128), f32),
                   jax.ShapeDtypeStruct((B, S, 128), f32),
                   jax.ShapeDtypeStruct((B, S, 256), bf16),
                   jax.ShapeDtypeStruct((B, S, IDX_DIM), f32),
                   jax.ShapeDtypeStruct((B, S, IDX_HEADS), f32)],
        compiler_params=_params(("parallel", "parallel")),
        name="attn_proj",
    )(x, w_ap, cs)


def _rotary_tables(pos):
    rot = HEAD_DIM // 4
    half = rot // 2
    inv_freq = ROPE_THETA ** (-jnp.arange(half, dtype=f32) / half)
    ang = pos.astype(f32)[:, None] * inv_freq[None, :]
    cos = jnp.cos(ang)
    sin = jnp.sin(ang)
    S = pos.shape[0]
    one = jnp.ones((S, HEAD_DIM - rot), f32)
    zero = jnp.zeros((S, HEAD_DIM - rot), f32)
    zh = jnp.zeros((S, half), f32)
    c = jnp.concatenate([cos, cos, one], axis=1)
    sa = jnp.concatenate([-sin, zh, zero], axis=1)
    sb = jnp.concatenate([zh, sin, zero], axis=1)
    tile2 = lambda a: jnp.concatenate([a, a], axis=1)
    return jnp.stack([tile2(c), tile2(sa), tile2(sb)], axis=0)


def _dsa_kernel(qt_ref, qit_ref, wit_ref, ki_ref, k_ref, vt_ref, tri_ref, o_ref,
                s_ref, lm_ref, p_ref, m_ref, acc_ref, *, TQ, TK, P, top):
    qb = pl.program_id(1)
    q0 = P + qb * TQ
    nk = (q0 + TQ + TK - 1) // TK
    n_full = q0 // TK
    NC = TK // LANES
    R = GROUPS * TQ
    q_lim = ((q0 + lax.broadcasted_iota(i32, (1, TQ), 1)) // CHUNK + 1) * CHUNK
    wit = wit_ref[0, 0] * IDX_SCALE

    def score_tile(masked, j, carry):
        rel = jnp.maximum(jnp.dot(ki_ref[0, j], qit_ref[0, 0], preferred_element_type=f32), 0.0)
        score = wit[0:1, :] * rel[:, 0:TQ]
        for h in range(1, IDX_HEADS):
            score = score + wit[h:h + 1, :] * rel[:, h * TQ:(h + 1) * TQ]
        if masked:
            k_pos = j * TK + lax.broadcasted_iota(i32, (TK, TQ), 0)
            score = jnp.where(k_pos < q_lim, score, NEG)
        bits = pltpu.bitcast(score, i32)
        s_ref[j] = jnp.where(bits >= 0, bits, bits ^ 0x7FFFFFFF)
        return carry

    lax.fori_loop(0, n_full, functools.partial(score_tile, False), 0)
    lax.fori_loop(n_full, nk, functools.partial(score_tile, True), 0)

    def count(pred):
        def body(j, acc):
            hit = jnp.where(pred(s_ref[j]), 1, 0)
            parts = [hit[r * 8:(r + 1) * 8] for r in range(TK // 8)]
            while len(parts) > 1:
                parts = [a + b for a, b in zip(parts[0::2], parts[1::2])]
            return acc + parts[0]
        acc = lax.fori_loop(0, nk, body, jnp.zeros((8, TQ), i32))
        return jnp.sum(acc, axis=0, keepdims=True)

    def step(bit, t, cnt):
        cand = t | jnp.left_shift(jnp.int32(1), bit)
        c = count(lambda k: k >= cand)
        ok = c >= top
        return jnp.where(ok, cand, t), jnp.where(ok, c, cnt)

    c0 = count(lambda k: k >= 0)
    t = jnp.where(c0 >= top, 0, INT_MIN).astype(i32)
    cnt = jnp.where(c0 >= top, c0, jnp.int32(2 ** 30))
    t, cnt = lax.fori_loop(0, 31 - SEARCH_CHECKED, lambda i, st: step(30 - i, *st), (t, cnt))

    def checked_steps(st):
        bit, t, cnt, _ = st
        for s in range(SEARCH_STRIDE):
            t, cnt = step(bit - s, t, cnt)
        done = jnp.min(jnp.where(cnt == top, 1, 0))
        return bit - SEARCH_STRIDE, t, cnt, done

    more = lambda st: jnp.logical_and(st[0] >= 0, st[3] == 0)
    _, t, cnt, _ = lax.while_loop(more, checked_steps,
                                  (jnp.int32(SEARCH_CHECKED - 1), t, cnt, jnp.int32(0)))
    neg_key = np.array(NEG, np.float32).view(np.int32) ^ 0x7FFFFFFF
    any_excess = jnp.max(jnp.where(cnt > top, 1, 0)) > 0

    @pl.when(jnp.logical_not(any_excess))
    def _():
        thr = jnp.where(t == neg_key, t + 1, t)

        def fast(j, carry):
            s_ref[j] = pltpu.bitcast(jnp.where(s_ref[j] >= thr, 0.0, NEG), i32)
            return carry
        lax.fori_loop(0, nk, fast, 0)

    @pl.when(any_excess)
    def _():
        n_gt = count(lambda k: k > t)
        need = jnp.where(t == neg_key, 0, top - n_gt).astype(f32)

        def ranked(j, taken):
            keys = s_ref[j]
            for c in range(NC):
                kc = keys[c * LANES:(c + 1) * LANES, :]
                eq = jnp.where(kc == t, 1.0, 0.0)
                rank = taken + jnp.dot(tri_ref[...], eq.astype(bf16), preferred_element_type=f32)
                pick = jnp.where(kc > t, 1.0, jnp.where(rank <= need, eq, 0.0))
                s_ref[j, c * LANES:(c + 1) * LANES, :] = pltpu.bitcast(
                    jnp.where(pick > 0.5, 0.0, NEG), i32)
                taken = taken + jnp.sum(eq, axis=0, keepdims=True)
            return taken
        lax.fori_loop(0, nk, ranked, jnp.zeros((1, TQ), f32))

    m_ref[...] = jnp.full(m_ref.shape, NEG, f32)
    acc_ref[...] = jnp.zeros(acc_ref.shape, f32)

    def logits(j, slot):
        jc = jnp.minimum(j, nk - 1)
        tile_max = []
        for g in range(KV_HEADS):
            cmax = jnp.full((8, R), NEG, f32)
            for c in range(NC):
                rows = slice(c * LANES, (c + 1) * LANES)
                bias = pltpu.bitcast(s_ref[jc, rows, :], f32)
                lm = (jnp.dot(k_ref[0, jc, g, rows, :], qt_ref[0, 0, g], preferred_element_type=f32)
                      + jnp.concatenate([bias] * GROUPS, axis=1))
                lm_ref[slot, g, rows, :] = lm
                cmax = jnp.maximum(cmax, jnp.max(lm.reshape(LANES // 8, 8, R), axis=0))
            tile_max.append(jnp.max(cmax, axis=0, keepdims=True))
        return tuple(tile_max)

    def accumulate(j, slot, tile_max):
        for g in range(KV_HEADS):
            m_old = m_ref[g]
            m_new = jnp.maximum(m_old, tile_max[g])
            m_ref[g] = m_new
            for c in range(NC):
                rows = slice(c * LANES, (c + 1) * LANES)
                p_ref[slot, g, rows, :] = jnp.exp2(lm_ref[slot, g, rows, :] - m_new).astype(bf16)
            acc_ref[g] = (jnp.exp2(m_old - m_new) * acc_ref[g]
                          + jnp.dot(vt_ref[0, j, g], p_ref[slot, g], preferred_element_type=f32))

    def tile_pair(i, tile_max):
        nxt = logits(2 * i + 1, 1)
        accumulate(2 * i, 0, tile_max)
        tile_max = logits(2 * i + 2, 0)
        accumulate(2 * i + 1, 1, nxt)
        return tile_max

    last_max = lax.fori_loop(0, nk // 2, tile_pair, logits(0, 0))

    @pl.when(nk % 2 == 1)
    def _():
        accumulate(nk - 1, 0, last_max)

    for g in range(KV_HEADS):
        o_ref[0, 0, g] = (acc_ref[g, 0:HEAD_DIM, :] / acc_ref[g, HEAD_DIM:HEAD_DIM + 1, :]).astype(bf16)


def _dsa(q, qi, wi, k_all, v_all, ki_all, P):
    B, S, _ = q.shape
    L = k_all.shape[1]
    top = min(TOPK_MAX, L // 4)
    TQ = _row_tile(S, 2 * CHUNK)
    TK = KEY_TILE
    nq = S // TQ
    NT = -(-L // TK)
    pad = NT * TK - L
    R = GROUPS * TQ
    assert top <= TK and TQ % CHUNK == 0

    def tiles(a):
        a = jnp.pad(a.astype(bf16), ((0, 0), (0, pad), (0, 0)))
        return a.reshape(B, NT, TK, a.shape[-1])

    ki_t = tiles(ki_all)
    kv_heads = lambda a: tiles(a).reshape(B, NT, TK, KV_HEADS, HEAD_DIM)
    k_t = kv_heads(k_all).transpose(0, 1, 3, 2, 4)
    vt = kv_heads(v_all).transpose(0, 1, 3, 4, 2)
    ones_row = jnp.zeros((B, NT, KV_HEADS, VT_ROWS - HEAD_DIM, TK), bf16).at[:, :, :, 0, :].set(1.0)
    vt = jnp.concatenate([vt, ones_row], axis=3)
    qt = q.reshape(B, nq, TQ, KV_HEADS, GROUPS, HEAD_DIM).transpose(0, 1, 3, 5, 4, 2)
    qt = qt.reshape(B, nq, KV_HEADS, HEAD_DIM, R)
    qit = qi.reshape(B, nq, TQ, IDX_HEADS, IDX_DIM).transpose(0, 1, 4, 3, 2)
    qit = qit.reshape(B, nq, IDX_DIM, IDX_HEADS * TQ)
    wit = wi.reshape(B, nq, TQ, IDX_HEADS).transpose(0, 1, 3, 2)
    tri = jnp.asarray(np.tril(np.ones((LANES, LANES), np.float32)), bf16)
    out = pl.pallas_call(
        functools.partial(_dsa_kernel, TQ=TQ, TK=TK, P=P, top=top),
        grid=(B, nq),
        in_specs=[pl.BlockSpec((1, 1, KV_HEADS, HEAD_DIM, R), lambda b, i: (b, i, 0, 0, 0)),
                  pl.BlockSpec((1, 1, IDX_DIM, IDX_HEADS * TQ), lambda b, i: (b, i, 0, 0)),
                  pl.BlockSpec((1, 1, IDX_HEADS, TQ), lambda b, i: (b, i, 0, 0)),
                  pl.BlockSpec((1, NT, TK, IDX_DIM), lambda b, i: (b, 0, 0, 0)),
                  pl.BlockSpec((1, NT, KV_HEADS, TK, HEAD_DIM), lambda b, i: (b, 0, 0, 0, 0)),
                  pl.BlockSpec((1, NT, KV_HEADS, VT_ROWS, TK), lambda b, i: (b, 0, 0, 0, 0)),
                  _const_spec((LANES, LANES))],
        out_specs=pl.BlockSpec((1, 1, KV_HEADS, HEAD_DIM, R), lambda b, i: (b, i, 0, 0, 0)),
        out_shape=jax.ShapeDtypeStruct((B, nq, KV_HEADS, HEAD_DIM, R), bf16),
        scratch_shapes=[pltpu.VMEM((NT, TK, TQ), i32),
                        pltpu.VMEM((2, KV_HEADS, TK, R), f32),
                        pltpu.VMEM((2, KV_HEADS, TK, R), bf16),
                        pltpu.VMEM((KV_HEADS, 1, R), f32),
                        pltpu.VMEM((KV_HEADS, VT_ROWS, R), f32)],
        compiler_params=_params(("parallel", "arbitrary")),
        name="dsa",
    )(qt, qit, wit, ki_t, k_t, vt, tri)
    out = out.reshape(B, nq, KV_HEADS, HEAD_DIM, GROUPS, TQ).transpose(0, 1, 5, 2, 4, 3)
    return out.reshape(B, S, N_HEADS * HEAD_DIM)


def _hgrn_mats(T):
    r = np.arange(T)[:, None]
    c = np.arange(T)[None, :]
    mats = [c <= r, c > r]
    b = T // 2
    while b >= 1:
        blk_r = r // (2 * b)
        mid = blk_r * 2 * b + b
        same = blk_r == c // (2 * b)
        late = r >= mid
        mats.append(same & ((late & (c >= mid) & (c <= r)) | (~late & (c > r) & (c < mid))))
        b //= 2
    return np.concatenate(mats, axis=0).astype(np.float32)


def _hgrn_kernel(x_ref, w_ref, lb_ref, g_ref, mats_ref, s0_ref, y_ref, st_out_ref, st_ref, *, T):
    t_idx = pl.program_id(1)

    @pl.when(t_idx == 0)
    def _():
        st_ref[...] = s0_ref[0]

    xb = x_ref[0].astype(bf16)
    proj = jnp.dot(xb, w_ref[...], preferred_element_type=f32)
    hq = proj[:, 0:HGRN_W]
    z = proj[:, HGRN_W:2 * HGRN_W]
    hv = proj[:, 2 * HGRN_W:3 * HGRN_W]
    hg = proj[:, 3 * HGRN_W:4 * HGRN_W]
    lb = lb_ref[...]
    logf = (jnp.minimum(z, 0.0) - jnp.log1p(jnp.exp(-jnp.abs(z)))) + jnp.log1p(lb * jnp.exp(-z))
    kk = (1.0 - lb) * _sigmoid(-z)
    qq = hq * _sigmoid(hq)
    gate = hg * _sigmoid(hg)

    hi = logf.astype(bf16)
    r1 = logf - hi.astype(f32)
    mid = r1.astype(bf16)
    lo = (r1 - mid.astype(f32)).astype(bf16)
    mats = mats_ref[...]
    e_all = (jnp.dot(mats, hi, preferred_element_type=f32)
             + jnp.dot(mats, mid, preferred_element_type=f32)
             + jnp.dot(mats, lo, preferred_element_type=f32))

    row = lax.broadcasted_iota(i32, (T, 1), 0)
    col = lax.broadcasted_iota(i32, (1, T), 1)
    nt = (((1,), (1,)), ((), ()))
    tn = (((0,), (0,)), ((), ()))
    for h in range(HGRN_HEADS):
        sl = slice(h * HGRN_DK, (h + 1) * HGRN_DK)
        q_h, k_h, v_h = qq[:, sl], kk[:, sl], hv[:, sl]
        vb = v_h.astype(bf16)
        cum = e_all[0:T, sl]
        suf = e_all[T:2 * T, sl]
        st = st_ref[h]
        o = lax.dot_general((q_h * jnp.exp(cum)).astype(bf16), st.astype(bf16), nt,
                            preferred_element_type=f32)
        o = o + jnp.sum(q_h * k_h, axis=-1, keepdims=True) * v_h
        scores = jnp.zeros((T, T), f32)
        b = T // 2
        lvl = 2
        while b >= 1:
            xdec = jnp.exp(e_all[lvl * T:(lvl + 1) * T, sl])
            late = ((row // b) % 2) == 1
            a_m = jnp.where(late, q_h * xdec, 0.0).astype(bf16)
            b_m = jnp.where(late, 0.0, k_h * xdec).astype(bf16)
            sc = lax.dot_general(a_m, b_m, nt, preferred_element_type=f32)
            if 2 * b < T:
                sc = jnp.where((row // (2 * b)) == (col // (2 * b)), sc, 0.0)
            scores = scores + sc
            b //= 2
            lvl += 1
        o = o + jnp.dot(scores.astype(bf16), vb, preferred_element_type=f32)
        kd = (k_h * jnp.exp(suf)).astype(bf16)
        st_ref[h] = (st * jnp.exp(cum[T - 1:T, :])
                     + lax.dot_general(vb, kd, tn, preferred_element_type=f32))
        o = o * lax.rsqrt(jnp.mean(o * o, axis=-1, keepdims=True) + LN_EPS) * g_ref[...]
        y_ref[0, :, sl] = (o * gate[:, sl]).astype(bf16)

    st_out_ref[0] = st_ref[...]


def _hgrn(x, w_h, lb, norm_g, s0):
    B, S, _ = x.shape
    T = _row_tile(S, 2 * CHUNK)
    mats = jnp.asarray(_hgrn_mats(T), bf16)
    nm = mats.shape[0]
    st0 = jnp.swapaxes(s0.astype(f32), 2, 3)
    y, st = pl.pallas_call(
        functools.partial(_hgrn_kernel, T=T),
        grid=(B, S // T),
        in_specs=[pl.BlockSpec((1, T, D_MODEL), lambda b, t: (b, t, 0)),
                  _const_spec((D_MODEL, 4 * HGRN_W)),
                  _const_spec((1, HGRN_W)),
                  _const_spec((1, HGRN_DV)),
                  _const_spec((nm, T)),
                  pl.BlockSpec((1, HGRN_HEADS, HGRN_DV, HGRN_DK), lambda b, t: (b, 0, 0, 0))],
        out_specs=[pl.BlockSpec((1, T, HGRN_W), lambda b, t: (b, t, 0)),
                   pl.BlockSpec((1, HGRN_HEADS, HGRN_DV, HGRN_DK), lambda b, t: (b, 0, 0, 0))],
        out_shape=[jax.ShapeDtypeStruct((B, S, HGRN_W), bf16),
                   jax.ShapeDtypeStruct((B, HGRN_HEADS, HGRN_DV, HGRN_DK), f32)],
        scratch_shapes=[pltpu.VMEM((HGRN_HEADS, HGRN_DV, HGRN_DK), f32)],
        compiler_params=_params(("parallel", "arbitrary")),
        name="hgrn",
    )(x, w_h, lb, norm_g, mats, st0)
    return y, jnp.swapaxes(st, 2, 3)


HIST = CONV_W - 1
PADROWS = 8


def _conv_rows(buf_ref, cols, u, w, bias, T):
    buf_ref[PADROWS:PADROWS + T, cols] = u
    y = bias
    for j in range(CONV_W):
        y = y + w[j:j + 1, :] * buf_ref[PADROWS - HIST + j:PADROWS - HIST + j + T, cols]
    buf_ref[PADROWS - HIST:PADROWS, cols] = u[T - HIST:T, :]
    return y


def _merge_kernel(x_ref, ya_ref, yb_ref, hist_ref, wc_ref, wg_ref, wbr_ref, wout_ref, cw_ref, cb_ref,
                  g_ref, b_ref, o_ref, hist_out_ref, buf_ref, *, T, alpha):
    @pl.when(pl.program_id(1) == 0)
    def _():
        buf_ref[PADROWS - HIST:PADROWS, :] = hist_ref[0]

    x = x_ref[0]
    xb = x.astype(bf16)
    c = jnp.dot(xb, wc_ref[...], preferred_element_type=f32)
    W = SCONV_WIDTH
    u = c[:, W:2 * W] * c[:, 2 * W:3 * W]
    u_conv = _conv_rows(buf_ref, slice(0, W), u, cw_ref[...], cb_ref[...], T)
    yc = (c[:, 0:W] * u_conv).astype(bf16)
    hist_out_ref[0] = buf_ref[PADROWS - HIST:PADROWS, :]

    merged = None
    for i, yb in enumerate((ya_ref[0], yb_ref[0], yc)):
        gate = _sigmoid(jnp.dot(xb, wg_ref[:, i * D_MODEL:(i + 1) * D_MODEL], preferred_element_type=f32))
        term = gate * jnp.dot(yb, wbr_ref[i], preferred_element_type=f32)
        merged = term if merged is None else merged + term
    out = jnp.dot(merged.astype(bf16), wout_ref[...], preferred_element_type=f32)
    o_ref[0] = _layer_norm(alpha * x + out, g_ref[...], b_ref[...])


def _merge(x, ya, yb, hist, wc, wg, wbr, wout, cw, cb, g, b, alpha):
    B, S, _ = x.shape
    T = _row_tile(S, 256)
    row = lambda w: pl.BlockSpec((1, T, w), lambda bb, t: (bb, t, 0))
    hspec = pl.BlockSpec((1, HIST, SCONV_WIDTH), lambda bb, t: (bb, 0, 0))
    return pl.pallas_call(
        functools.partial(_merge_kernel, T=T, alpha=alpha),
        grid=(B, S // T),
        in_specs=[row(D_MODEL), row(BRANCH_WIDTH), row(BRANCH_WIDTH), hspec,
                  _const_spec((D_MODEL, 3 * SCONV_WIDTH)), _const_spec((D_MODEL, N_BRANCH * D_MODEL)),
                  _const_spec((N_BRANCH, BRANCH_WIDTH, D_MODEL)), _const_spec((D_MODEL, D_MODEL)),
                  _const_spec((CONV_W, SCONV_WIDTH)), _const_spec((1, SCONV_WIDTH)),
                  _const_spec((1, D_MODEL)), _const_spec((1, D_MODEL))],
        out_specs=[row(D_MODEL), hspec],
        out_shape=[jax.ShapeDtypeStruct((B, S, D_MODEL), f32),
                   jax.ShapeDtypeStruct((B, HIST, SCONV_WIDTH), f32)],
        scratch_shapes=[pltpu.VMEM((PADROWS + T, SCONV_WIDTH), f32)],
        compiler_params=_params(("parallel", "arbitrary")),
        name="merge",
    )(x, ya, yb, hist, wc, wg, wbr, wout, cw, cb, g, b)


def _ffn_kernel(x_ref, hist_ref, wup_ref, cw_ref, cb_ref, wdn_ref, g_ref, b_ref, o_ref, hist_out_ref,
                buf_ref, *, T, alpha):
    @pl.when(pl.program_id(1) == 0)
    def _():
        buf_ref[PADROWS - HIST:PADROWS, :] = hist_ref[0]

    x = x_ref[0]
    xb = x.astype(bf16)
    acc = jnp.zeros((T, D_MODEL), f32)
    for cidx in range(D_FF // FFN_COLS):
        halves = []
        for off in (0, D_FF):
            cols = slice(off + cidx * FFN_COLS, off + (cidx + 1) * FFN_COLS)
            h = jnp.dot(xb, wup_ref[:, cols], preferred_element_type=f32)
            halves.append(_conv_rows(buf_ref, cols, h, cw_ref[:, cols], cb_ref[:, cols], T))
        a_g, b_v = halves
        act = (a_g * _sigmoid(a_g) * b_v).astype(bf16)
        acc = acc + jnp.dot(act, wdn_ref[cidx * FFN_COLS:(cidx + 1) * FFN_COLS, :],
                            preferred_element_type=f32)
    hist_out_ref[0] = buf_ref[PADROWS - HIST:PADROWS, :]
    o_ref[0] = _layer_norm(alpha * x + acc, g_ref[...], b_ref[...])


def _ffn(x, hist, wup, cw, cb, wdn, g, b, alpha):
    B, S, _ = x.shape
    T = _row_tile(S, 256)
    row = pl.BlockSpec((1, T, D_MODEL), lambda bb, t: (bb, t, 0))
    hspec = pl.BlockSpec((1, HIST, 2 * D_FF), lambda bb, t: (bb, 0, 0))
    return pl.pallas_call(
        functools.partial(_ffn_kernel, T=T, alpha=alpha),
        grid=(B, S // T),
        in_specs=[row, hspec, _const_spec((D_MODEL, 2 * D_FF)), _const_spec((CONV_W, 2 * D_FF)),
                  _const_spec((1, 2 * D_FF)), _const_spec((D_FF, D_MODEL)),
                  _const_spec((1, D_MODEL)), _const_spec((1, D_MODEL))],
        out_specs=[row, hspec],
        out_shape=[jax.ShapeDtypeStruct((B, S, D_MODEL), f32),
                   jax.ShapeDtypeStruct((B, HIST, 2 * D_FF), f32)],
        scratch_shapes=[pltpu.VMEM((PADROWS + T, 2 * D_FF), f32)],
        compiler_params=_params(("parallel", "arbitrary")),
        name="ffn",
    )(x, hist, wup, cw, cb, wdn, g, b)


def _layer_weights(l, depth, w_in, hgrn_lb_logits, hgrn_norm_g, sconv_w, sconv_b, w_branch, w_out,
                   ln1_g, ln1_b, w_up, ffn_conv_w, ffn_conv_b, w_down, ln2_g, ln2_b):
    offs = np.concatenate([[0], np.cumsum(IN_SIZES)]).tolist()
    sec = lambda i, j=None: w_in[l][:, offs[i]:offs[(i if j is None else j) + 1]]
    a_q, a_k, a_v, i_q, i_k, i_w = (sec(i) for i in range(6))
    zpad = lambda n: jnp.zeros((D_MODEL, n), w_in.dtype)
    w_ap = jnp.concatenate([a_q, a_k, i_q, i_k, zpad(ROT_COLS - 960), a_v, i_w,
                            zpad(AP_COLS - ROT_COLS - 128 - IDX_HEADS)], axis=1).astype(bf16)
    lbp = jax.nn.softmax(hgrn_lb_logits.astype(f32), axis=0)
    lb = (jnp.cumsum(lbp, axis=0) - lbp[0])[l].reshape(1, HGRN_W)
    row = lambda a: a.reshape(1, -1).astype(f32)
    return dict(
        w_ap=w_ap, w_h=sec(6, 9).astype(bf16), lb=lb, norm_g=row(hgrn_norm_g[l]),
        w_c=sec(10, 12).astype(bf16), w_g=sec(13).astype(bf16),
        w_br=w_branch[l].astype(bf16), w_out=w_out[l].astype(bf16),
        sconv_w=sconv_w[l].astype(f32), sconv_b=row(sconv_b[l]),
        ln1_g=row(ln1_g[l]), ln1_b=row(ln1_b[l]),
        w_up=w_up[l].astype(bf16), ffn_w=ffn_conv_w[l].astype(f32), ffn_b=row(ffn_conv_b[l]),
        w_down=w_down[l].astype(bf16), ln2_g=row(ln2_g[l]), ln2_b=row(ln2_b[l]))


def _trunk_layer(x, cs, P, k_past, v_past, ki_past, s0, sc_hist, ffn_hist, w, alpha):
    B, S, _ = x.shape
    q, k, v, qi, ki, wi = _attn_proj(x, w["w_ap"], cs)
    if k_past is None:
        k_all, v_all, ki_all = k, v, ki
    else:
        k_all = jnp.concatenate([k_past.reshape(B, P, KV_HEADS * HEAD_DIM), k], axis=1)
        v_all = jnp.concatenate([v_past.reshape(B, P, KV_HEADS * HEAD_DIM), v], axis=1)
        ki_all = jnp.concatenate([ki_past, ki], axis=1)
    y_a = _dsa(q, qi, wi, k_all, v_all, ki_all, P)
    y_b, s_new = _hgrn(x, w["w_h"], w["lb"], w["norm_g"], s0)
    x1, sc_new = _merge(x, y_a, y_b, sc_hist, w["w_c"], w["w_g"], w["w_br"], w["w_out"],
                        w["sconv_w"], w["sconv_b"], w["ln1_g"], w["ln1_b"], alpha)
    x2, ffn_new = _ffn(x1, ffn_hist, w["w_up"], w["ffn_w"], w["ffn_b"], w["w_down"],
                       w["ln2_g"], w["ln2_b"], alpha)
    new = (k.reshape(B, S, KV_HEADS, HEAD_DIM), v.reshape(B, S, KV_HEADS, HEAD_DIM), ki,
           s_new, sc_new, ffn_new)
    return x2, new


def kernel(x_prompt, x_sample, cache_attn_k, cache_attn_v, cache_idx_k, state_hgrn, state_sconv,
           state_ffn_conv, w_in, hgrn_lb_logits, hgrn_norm_g, sconv_w, sconv_b, w_branch, w_out,
           ln1_g, ln1_b, w_up, ffn_conv_w, ffn_conv_b, w_down, ln2_g, ln2_b):
    depth = w_in.shape[0]
    alpha = (2 * depth) ** 0.25
    B, S, _ = x_prompt.shape
    DB, DS, _ = x_sample.shape
    P = cache_attn_k.shape[2]
    cs_p = _rotary_tables(jnp.arange(S))
    cs_s = _rotary_tables(P + jnp.arange(DS))
    xp, xs = x_prompt, x_sample
    st_p = [[] for _ in range(6)]
    st_s = [[] for _ in range(6)]
    for l in range(depth):
        w = _layer_weights(l, depth, w_in, hgrn_lb_logits, hgrn_norm_g, sconv_w, sconv_b, w_branch,
                           w_out, ln1_g, ln1_b, w_up, ffn_conv_w, ffn_conv_b, w_down, ln2_g, ln2_b)
        xp, new_p = _trunk_layer(
            xp, cs_p, 0, None, None, None,
            jnp.zeros((B, HGRN_HEADS, HGRN_DK, HGRN_DV), f32),
            jnp.zeros((B, HIST, SCONV_WIDTH), f32),
            jnp.zeros((B, HIST, 2 * D_FF), f32), w, alpha)
        xs, new_s = _trunk_layer(
            xs, cs_s, P, cache_attn_k[l], cache_attn_v[l], cache_idx_k[l], state_hgrn[l],
            state_sconv[l], state_ffn_conv[l], w, alpha)
        for j in range(6):
            st_p[j].append(new_p[j])
            st_s[j].append(new_s[j])
    outs_p = [jnp.stack(a, axis=0) for a in st_p]
    outs_s = [jnp.stack(a, axis=0) for a in st_s]
    return (xp, xs, *outs_p, *outs_s)
```

```python
import functools

import numpy as np
import jax
import jax.numpy as jnp
from jax import lax
from jax.experimental import pallas as pl
from jax.experimental.pallas import tpu as pltpu

D_MODEL = 1024
CHUNK = 64
N_HEADS = 8
HEAD_DIM = 64
KV_HEADS = 2
GROUPS = N_HEADS // KV_HEADS
IDX_HEADS = 4
IDX_DIM = 64
TOPK_MAX = 256
ROPE_THETA = 500000.0
ATTN_SCALE = HEAD_DIM ** -0.5
IDX_SCALE = (IDX_DIM ** -0.5) * (IDX_HEADS ** -0.5)
NEG = -1e30
HGRN_HEADS = 4
HGRN_DK = 128
HGRN_DV = 128
HGRN_W = HGRN_HEADS * HGRN_DK
SCONV_WIDTH = 512
CONV_W = 3
BRANCH_WIDTH = 512
N_BRANCH = 3
D_FF = 2816
LN_EPS = 1e-5
IN_SIZES = (N_HEADS * HEAD_DIM, KV_HEADS * HEAD_DIM, KV_HEADS * HEAD_DIM,
            IDX_HEADS * IDX_DIM, IDX_DIM, IDX_HEADS,
            HGRN_W, HGRN_W, HGRN_W, HGRN_W,
            SCONV_WIDTH, SCONV_WIDTH, SCONV_WIDTH,
            N_BRANCH * D_MODEL)

LANES = 128
VMEM_LIMIT = 56 * 1024 * 1024
KEY_TILE = 512
SEARCH_CHECKED = 24
SEARCH_STRIDE = 8
VT_ROWS = HEAD_DIM + 16
FFN_COLS = 256
INT_MIN = -2 ** 31

f32 = jnp.float32
bf16 = jnp.bfloat16
i32 = jnp.int32


def _row_tile(s, want):
    return want if s % want == 0 else s


def _const_spec(shape):
    nd = len(shape)
    return pl.BlockSpec(shape, lambda *_: (0,) * nd, pipeline_mode=pl.Buffered(1))


def _params(sem):
    return pltpu.CompilerParams(dimension_semantics=sem, vmem_limit_bytes=VMEM_LIMIT)


def _layer_norm(z, g, b):
    mu = jnp.mean(z, axis=-1, keepdims=True)
    d = z - mu
    var = jnp.mean(d * d, axis=-1, keepdims=True)
    return d * lax.rsqrt(var + LN_EPS) * g + b


def _sigmoid(x):
    return 1.0 / (1.0 + jnp.exp(-x))


Q_PRESCALE = ATTN_SCALE * float(np.log2(np.e))
ROT = HEAD_DIM // 4
WR_COLS = 384
WT_ROWS = 912
WI_ROWS = 8


def _attn_proj_kernel(x_ref, wr_ref, wt_ref, cs_ref, cst_ref, k_ref, v_ref, ki_ref, kb_ref, kib_ref,
                      vt_ref, qt_ref, qit_ref, wit_ref, *, T, TQ):
    xb = x_ref[0].astype(bf16)
    pr = jnp.dot(xb, wr_ref[...], preferred_element_type=f32)
    c, sa, sb = cs_ref[0], cs_ref[1], cs_ref[2]

    def rot(xg):
        return xg * c + pltpu.roll(xg, LANES - ROT // 2, 1) * sa + pltpu.roll(xg, ROT // 2, 1) * sb

    k = rot(pr[:, 0:LANES])
    ki = rot(pr[:, LANES:2 * LANES])[:, :IDX_DIM]
    k_ref[0] = k
    kb_ref[0] = k.astype(bf16)
    ki_ref[0] = ki
    kib_ref[0] = ki.astype(bf16)
    v_ref[0] = pr[:, 2 * LANES:3 * LANES]

    nt = (((1,), (1,)), ((), ()))
    pt = lax.dot_general(wt_ref[...], xb, nt, preferred_element_type=f32)
    ct, st = cst_ref[0], cst_ref[1]

    def rot_t(hb):
        lead = hb[0:ROT] * ct + jnp.concatenate([hb[ROT // 2:ROT], hb[0:ROT // 2]], axis=0) * st
        return jnp.concatenate([lead, hb[ROT:HEAD_DIM]], axis=0)

    nqb = T // TQ
    for h in range(N_HEADS):
        hb = (rot_t(pt[h * HEAD_DIM:(h + 1) * HEAD_DIM]) * Q_PRESCALE).astype(bf16)
        g, i = divmod(h, GROUPS)
        for n in range(nqb):
            qt_ref[0, n, g, :, i * TQ:(i + 1) * TQ] = hb[:, n * TQ:(n + 1) * TQ]
    base = N_HEADS * HEAD_DIM
    for h in range(IDX_HEADS):
        hb = rot_t(pt[base + h * IDX_DIM:base + (h + 1) * IDX_DIM]).astype(bf16)
        for n in range(nqb):
            qit_ref[0, n, :, h * TQ:(h + 1) * TQ] = hb[:, n * TQ:(n + 1) * TQ]
    base += IDX_HEADS * IDX_DIM
    ones_rows = jnp.where(lax.broadcasted_iota(i32, (VT_ROWS - HEAD_DIM, T), 0) == 0, 1.0, 0.0)
    for g in range(KV_HEADS):
        vt_ref[0, 0, g, 0:HEAD_DIM, :] = pt[base + g * HEAD_DIM:base + (g + 1) * HEAD_DIM].astype(bf16)
        vt_ref[0, 0, g, HEAD_DIM:VT_ROWS, :] = ones_rows.astype(bf16)
    base += KV_HEADS * HEAD_DIM
    for n in range(nqb):
        wit_ref[0, n] = pt[base:base + WI_ROWS, n * TQ:(n + 1) * TQ]


def _attn_proj(x, w_row, w_t, cs, cst):
    B, S, _ = x.shape
    T = _row_tile(S, KEY_TILE)
    TQ = _row_tile(S, 2 * CHUNK)
    nqb = T // TQ
    R = GROUPS * TQ
    row = lambda w: pl.BlockSpec((1, T, w), lambda b, t: (b, t, 0))
    names = ("k", "v", "ki", "kb", "kib", "vt", "qt", "qit", "wit")
    outs = pl.pallas_call(
        functools.partial(_attn_proj_kernel, T=T, TQ=TQ),
        grid=(B, S // T),
        in_specs=[row(D_MODEL), _const_spec((D_MODEL, WR_COLS)), _const_spec((WT_ROWS, D_MODEL)),
                  pl.BlockSpec((3, T, LANES), lambda b, t: (0, t, 0)),
                  pl.BlockSpec((2, ROT, T), lambda b, t: (0, 0, t))],
        out_specs=[row(128), row(128), row(IDX_DIM), row(128), row(IDX_DIM),
                   pl.BlockSpec((1, 1, KV_HEADS, VT_ROWS, T), lambda b, t: (b, t, 0, 0, 0)),
                   pl.BlockSpec((1, nqb, KV_HEADS, HEAD_DIM, R), lambda b, t: (b, t, 0, 0, 0)),
                   pl.BlockSpec((1, nqb, IDX_DIM, IDX_HEADS * TQ), lambda b, t: (b, t, 0, 0)),
                   pl.BlockSpec((1, nqb, WI_ROWS, TQ), lambda b, t: (b, t, 0, 0))],
        out_shape=[jax.ShapeDtypeStruct((B, S, 128), f32),
                   jax.ShapeDtypeStruct((B, S, 128), f32),
                   jax.ShapeDtypeStruct((B, S, IDX_DIM), f32),
                   jax.ShapeDtypeStruct((B, S, 128), bf16),
                   jax.ShapeDtypeStruct((B, S, IDX_DIM), bf16),
                   jax.ShapeDtypeStruct((B, S // T, KV_HEADS, VT_ROWS, T), bf16),
                   jax.ShapeDtypeStruct((B, S // TQ, KV_HEADS, HEAD_DIM, R), bf16),
                   jax.ShapeDtypeStruct((B, S // TQ, IDX_DIM, IDX_HEADS * TQ), bf16),
                   jax.ShapeDtypeStruct((B, S // TQ, WI_ROWS, TQ), f32)],
        compiler_params=_params(("parallel", "parallel")),
        name="attn_proj",
    )(x, w_row, w_t, cs, cst)
    return dict(zip(names, outs))


def _rotary_tables(pos):
    half = ROT // 2
    inv_freq = ROPE_THETA ** (-jnp.arange(half, dtype=f32) / half)
    ang = pos.astype(f32)[:, None] * inv_freq[None, :]
    cos = jnp.cos(ang)
    sin = jnp.sin(ang)
    S = pos.shape[0]
    one = jnp.ones((S, HEAD_DIM - ROT), f32)
    zero = jnp.zeros((S, HEAD_DIM - ROT), f32)
    zh = jnp.zeros((S, half), f32)
    c = jnp.concatenate([cos, cos, one], axis=1)
    sa = jnp.concatenate([-sin, zh, zero], axis=1)
    sb = jnp.concatenate([zh, sin, zero], axis=1)
    tile2 = lambda a: jnp.concatenate([a, a], axis=1)
    cs = jnp.stack([tile2(c), tile2(sa), tile2(sb)], axis=0)
    cst = jnp.stack([jnp.concatenate([cos, cos], axis=1).T,
                     jnp.concatenate([-sin, sin], axis=1).T], axis=0)
    return cs, cst


def _dsa_kernel(qt_ref, qit_ref, wit_ref, ki_ref, k_ref, vt_ref, tri_ref, o_ref,
                s_ref, lm_ref, p_ref, qpad_ref, m_ref, acc_ref, *, TQ, TK, P, top):
    qb = pl.program_id(1)
    q0 = P + qb * TQ
    nk = (q0 + TQ + TK - 1) // TK
    n_full = q0 // TK
    NC = TK // LANES
    R = GROUPS * TQ
    q_lim = ((q0 + lax.broadcasted_iota(i32, (1, TQ), 1)) // CHUNK + 1) * CHUNK
    wit = wit_ref[0, 0] * IDX_SCALE
    zero_head = jnp.zeros((HEAD_DIM, R), bf16)
    qpad_ref[0] = jnp.concatenate([qt_ref[0, 0, 0], zero_head], axis=0)
    qpad_ref[1] = jnp.concatenate([zero_head, qt_ref[0, 0, 1]], axis=0)

    def score_tile(masked, j, carry):
        rel = jnp.maximum(jnp.dot(ki_ref[0, j], qit_ref[0, 0], preferred_element_type=f32), 0.0)
        score = wit[0:1, :] * rel[:, 0:TQ]
        for h in range(1, IDX_HEADS):
            score = score + wit[h:h + 1, :] * rel[:, h * TQ:(h + 1) * TQ]
        if masked:
            k_pos = j * TK + lax.broadcasted_iota(i32, (TK, TQ), 0)
            score = jnp.where(k_pos < q_lim, score, NEG)
        bits = pltpu.bitcast(score, i32)
        s_ref[j] = jnp.where(bits >= 0, bits, bits ^ 0x7FFFFFFF)
        return carry

    lax.fori_loop(0, n_full, functools.partial(score_tile, False), 0)
    lax.fori_loop(n_full, nk, functools.partial(score_tile, True), 0)

    def count(pred):
        def body(j, acc):
            hit = jnp.where(pred(s_ref[j]), 1, 0)
            parts = [hit[r * 8:(r + 1) * 8] for r in range(TK // 8)]
            while len(parts) > 1:
                parts = [a + b for a, b in zip(parts[0::2], parts[1::2])]
            return acc + parts[0]
        acc = lax.fori_loop(0, nk, body, jnp.zeros((8, TQ), i32))
        return jnp.sum(acc, axis=0, keepdims=True)

    def step(bit, t, cnt):
        cand = t | jnp.left_shift(jnp.int32(1), bit)
        c = count(lambda k: k >= cand)
        ok = c >= top
        return jnp.where(ok, cand, t), jnp.where(ok, c, cnt)

    c0 = count(lambda k: k >= 0)
    t = jnp.where(c0 >= top, 0, INT_MIN).astype(i32)
    cnt = jnp.where(c0 >= top, c0, jnp.int32(2 ** 30))
    t, cnt = lax.fori_loop(0, 31 - SEARCH_CHECKED, lambda i, st: step(30 - i, *st), (t, cnt))

    def checked_steps(st):
        bit, t, cnt, _ = st
        for s in range(SEARCH_STRIDE):
            t, cnt = step(bit - s, t, cnt)
        done = jnp.min(jnp.where(cnt == top, 1, 0))
        return bit - SEARCH_STRIDE, t, cnt, done

    more = lambda st: jnp.logical_and(st[0] >= 0, st[3] == 0)
    _, t, cnt, _ = lax.while_loop(more, checked_steps,
                                  (jnp.int32(SEARCH_CHECKED - 1), t, cnt, jnp.int32(0)))
    neg_key = np.array(NEG, np.float32).view(np.int32) ^ 0x7FFFFFFF
    any_excess = jnp.max(jnp.where(cnt > top, 1, 0)) > 0

    @pl.when(jnp.logical_not(any_excess))
    def _():
        thr = jnp.where(t == neg_key, t + 1, t)

        def fast(j, carry):
            s_ref[j] = pltpu.bitcast(jnp.where(s_ref[j] >= thr, 0.0, NEG), i32)
            return carry
        lax.fori_loop(0, nk, fast, 0)

    @pl.when(any_excess)
    def _():
        n_gt = count(lambda k: k > t)
        need = jnp.where(t == neg_key, 0, top - n_gt).astype(f32)

        def ranked(j, taken):
            keys = s_ref[j]
            for c in range(NC):
                kc = keys[c * LANES:(c + 1) * LANES, :]
                eq = jnp.where(kc == t, 1.0, 0.0)
                rank = taken + jnp.dot(tri_ref[...], eq.astype(bf16), preferred_element_type=f32)
                pick = jnp.where(kc > t, 1.0, jnp.where(rank <= need, eq, 0.0))
                s_ref[j, c * LANES:(c + 1) * LANES, :] = pltpu.bitcast(
                    jnp.where(pick > 0.5, 0.0, NEG), i32)
                taken = taken + jnp.sum(eq, axis=0, keepdims=True)
            return taken
        lax.fori_loop(0, nk, ranked, jnp.zeros((1, TQ), f32))

    m_ref[...] = jnp.full(m_ref.shape, NEG, f32)
    acc_ref[...] = jnp.zeros(acc_ref.shape, f32)

    def logits(j, slot):
        jc = jnp.minimum(j, nk - 1)
        tile_max = []
        for g in range(KV_HEADS):
            cmax = jnp.full((8, R), NEG, f32)
            for c in range(NC):
                rows = slice(c * LANES, (c + 1) * LANES)
                bias = pltpu.bitcast(s_ref[jc, rows, :], f32)
                lm = (jnp.dot(k_ref[0, jc, rows, :], qpad_ref[g], preferred_element_type=f32)
                      + jnp.concatenate([bias] * GROUPS, axis=1))
                lm_ref[slot, g, rows, :] = lm
                cmax = jnp.maximum(cmax, jnp.max(lm.reshape(LANES // 8, 8, R), axis=0))
            tile_max.append(jnp.max(cmax, axis=0, keepdims=True))
        return tuple(tile_max)

    def accumulate(j, slot, tile_max):
        for g in range(KV_HEADS):
            m_old = m_ref[g]
            m_new = jnp.maximum(m_old, tile_max[g])
            m_ref[g] = m_new
            for c in range(NC):
                rows = slice(c * LANES, (c + 1) * LANES)
                p_ref[slot, g, rows, :] = jnp.exp2(lm_ref[slot, g, rows, :] - m_new).astype(bf16)
            acc_ref[g] = (jnp.exp2(m_old - m_new) * acc_ref[g]
                          + jnp.dot(vt_ref[0, j, g], p_ref[slot, g], preferred_element_type=f32))

    def tile_pair(i, tile_max):
        nxt = logits(2 * i + 1, 1)
        accumulate(2 * i, 0, tile_max)
        tile_max = logits(2 * i + 2, 0)
        accumulate(2 * i + 1, 1, nxt)
        return tile_max

    last_max = lax.fori_loop(0, nk // 2, tile_pair, logits(0, 0))

    @pl.when(nk % 2 == 1)
    def _():
        accumulate(nk - 1, 0, last_max)

    for g in range(KV_HEADS):
        o_t = acc_ref[g, 0:HEAD_DIM, :] / acc_ref[g, HEAD_DIM:HEAD_DIM + 1, :]
        for i in range(0, GROUPS, 2):
            pair = jnp.concatenate([o_t[:, i * TQ:(i + 1) * TQ], o_t[:, (i + 1) * TQ:(i + 2) * TQ]], axis=0)
            h = g * GROUPS + i
            o_ref[0, :, h * HEAD_DIM:(h + 2) * HEAD_DIM] = pair.T.astype(bf16)


def _dsa(qt, qit, wit, kb, kib, vt, P, top):
    B, nq = qt.shape[:2]
    R = qt.shape[-1]
    TQ = R // GROUPS
    NT, TK = kb.shape[1:3]
    assert top <= TK and TQ % CHUNK == 0
    tri = jnp.asarray(np.tril(np.ones((LANES, LANES), np.float32)), bf16)
    return pl.pallas_call(
        functools.partial(_dsa_kernel, TQ=TQ, TK=TK, P=P, top=top),
        grid=(B, nq),
        in_specs=[pl.BlockSpec((1, 1, KV_HEADS, HEAD_DIM, R), lambda b, i: (b, i, 0, 0, 0)),
                  pl.BlockSpec((1, 1, IDX_DIM, IDX_HEADS * TQ), lambda b, i: (b, i, 0, 0)),
                  pl.BlockSpec((1, 1, WI_ROWS, TQ), lambda b, i: (b, i, 0, 0)),
                  pl.BlockSpec((1, NT, TK, IDX_DIM), lambda b, i: (b, 0, 0, 0)),
                  pl.BlockSpec((1, NT, TK, KV_HEADS * HEAD_DIM), lambda b, i: (b, 0, 0, 0)),
                  pl.BlockSpec((1, NT, KV_HEADS, VT_ROWS, TK), lambda b, i: (b, 0, 0, 0, 0)),
                  _const_spec((LANES, LANES))],
        out_specs=pl.BlockSpec((1, TQ, N_HEADS * HEAD_DIM), lambda b, i: (b, i, 0)),
        out_shape=jax.ShapeDtypeStruct((B, nq * TQ, N_HEADS * HEAD_DIM), bf16),
        scratch_shapes=[pltpu.VMEM((NT, TK, TQ), i32),
                        pltpu.VMEM((2, KV_HEADS, TK, R), f32),
                        pltpu.VMEM((2, KV_HEADS, TK, R), bf16),
                        pltpu.VMEM((KV_HEADS, KV_HEADS * HEAD_DIM, R), bf16),
                        pltpu.VMEM((KV_HEADS, 1, R), f32),
                        pltpu.VMEM((KV_HEADS, VT_ROWS, R), f32)],
        compiler_params=_params(("parallel", "arbitrary")),
        name="dsa",
    )(qt, qit, wit, kib, kb, vt, tri)


def _hgrn_mats(T):
    r = np.arange(T)[:, None]
    c = np.arange(T)[None, :]
    mats = [c <= r, c > r]
    b = T // 2
    while b >= 1:
        blk_r = r // (2 * b)
        mid = blk_r * 2 * b + b
        same = blk_r == c // (2 * b)
        late = r >= mid
        mats.append(same & ((late & (c >= mid) & (c <= r)) | (~late & (c > r) & (c < mid))))
        b //= 2
    return np.concatenate(mats, axis=0).astype(np.float32)


def _hgrn_kernel(x_ref, w_ref, lb_ref, g_ref, mats_ref, s0_ref, y_ref, st_out_ref, st_ref, *, T):
    t_idx = pl.program_id(1)

    @pl.when(t_idx == 0)
    def _():
        st_ref[...] = s0_ref[0]

    xb = x_ref[0].astype(bf16)
    proj = jnp.dot(xb, w_ref[...], preferred_element_type=f32)
    hq = proj[:, 0:HGRN_W]
    z = proj[:, HGRN_W:2 * HGRN_W]
    hv = proj[:, 2 * HGRN_W:3 * HGRN_W]
    hg = proj[:, 3 * HGRN_W:4 * HGRN_W]
    lb = lb_ref[...]
    logf = (jnp.minimum(z, 0.0) - jnp.log1p(jnp.exp(-jnp.abs(z)))) + jnp.log1p(lb * jnp.exp(-z))
    kk = (1.0 - lb) * _sigmoid(-z)
    qq = hq * _sigmoid(hq)
    gate = hg * _sigmoid(hg)

    hi = logf.astype(bf16)
    r1 = logf - hi.astype(f32)
    mid = r1.astype(bf16)
    lo = (r1 - mid.astype(f32)).astype(bf16)
    mats = mats_ref[...]
    e_all = (jnp.dot(mats, hi, preferred_element_type=f32)
             + jnp.dot(mats, mid, preferred_element_type=f32)
             + jnp.dot(mats, lo, preferred_element_type=f32))

    row = lax.broadcasted_iota(i32, (T, 1), 0)
    col = lax.broadcasted_iota(i32, (1, T), 1)
    nt = (((1,), (1,)), ((), ()))
    tn = (((0,), (0,)), ((), ()))
    for h in range(HGRN_HEADS):
        sl = slice(h * HGRN_DK, (h + 1) * HGRN_DK)
        q_h, k_h, v_h = qq[:, sl], kk[:, sl], hv[:, sl]
        vb = v_h.astype(bf16)
        cum = e_all[0:T, sl]
        suf = e_all[T:2 * T, sl]
        st = st_ref[h]
        o = lax.dot_general((q_h * jnp.exp(cum)).astype(bf16), st.astype(bf16), nt,
                            preferred_element_type=f32)
        o = o + jnp.sum(q_h * k_h, axis=-1, keepdims=True) * v_h
        scores = jnp.zeros((T, T), f32)
        b = T // 2
        lvl = 2
        while b >= 1:
            xdec = jnp.exp(e_all[lvl * T:(lvl + 1) * T, sl])
            late = ((row // b) % 2) == 1
            a_m = jnp.where(late, q_h * xdec, 0.0).astype(bf16)
            b_m = jnp.where(late, 0.0, k_h * xdec).astype(bf16)
            sc = lax.dot_general(a_m, b_m, nt, preferred_element_type=f32)
            if 2 * b < T:
                sc = jnp.where((row // (2 * b)) == (col // (2 * b)), sc, 0.0)
            scores = scores + sc
            b //= 2
            lvl += 1
        o = o + jnp.dot(scores.astype(bf16), vb, preferred_element_type=f32)
        kd = (k_h * jnp.exp(suf)).astype(bf16)
        st_ref[h] = (st * jnp.exp(cum[T - 1:T, :])
                     + lax.dot_general(vb, kd, tn, preferred_element_type=f32))
        o = o * lax.rsqrt(jnp.mean(o * o, axis=-1, keepdims=True) + LN_EPS) * g_ref[...]
        y_ref[0, :, sl] = (o * gate[:, sl]).astype(bf16)

    st_out_ref[0] = st_ref[...]


def _hgrn(x, w_h, lb, norm_g, s0):
    B, S, _ = x.shape
    T = _row_tile(S, 2 * CHUNK)
    mats = jnp.asarray(_hgrn_mats(T), bf16)
    nm = mats.shape[0]
    st0 = jnp.swapaxes(s0.astype(f32), 2, 3)
    y, st = pl.pallas_call(
        functools.partial(_hgrn_kernel, T=T),
        grid=(B, S // T),
        in_specs=[pl.BlockSpec((1, T, D_MODEL), lambda b, t: (b, t, 0)),
                  _const_spec((D_MODEL, 4 * HGRN_W)),
                  _const_spec((1, HGRN_W)),
                  _const_spec((1, HGRN_DV)),
                  _const_spec((nm, T)),
                  pl.BlockSpec((1, HGRN_HEADS, HGRN_DV, HGRN_DK), lambda b, t: (b, 0, 0, 0))],
        out_specs=[pl.BlockSpec((1, T, HGRN_W), lambda b, t: (b, t, 0)),
                   pl.BlockSpec((1, HGRN_HEADS, HGRN_DV, HGRN_DK), lambda b, t: (b, 0, 0, 0))],
        out_shape=[jax.ShapeDtypeStruct((B, S, HGRN_W), bf16),
                   jax.ShapeDtypeStruct((B, HGRN_HEADS, HGRN_DV, HGRN_DK), f32)],
        scratch_shapes=[pltpu.VMEM((HGRN_HEADS, HGRN_DV, HGRN_DK), f32)],
        compiler_params=_params(("parallel", "arbitrary")),
        name="hgrn",
    )(x, w_h, lb, norm_g, mats, st0)
    return y, jnp.swapaxes(st, 2, 3)


HIST = CONV_W - 1
PADROWS = 8


def _conv_rows(buf_ref, cols, u, w, bias, T):
    buf_ref[PADROWS:PADROWS + T, cols] = u
    y = bias
    for j in range(CONV_W):
        y = y + w[j:j + 1, :] * buf_ref[PADROWS - HIST + j:PADROWS - HIST + j + T, cols]
    buf_ref[PADROWS - HIST:PADROWS, cols] = u[T - HIST:T, :]
    return y


def _merge_kernel(x_ref, ya_ref, yb_ref, hist_ref, wc_ref, wg_ref, wbr_ref, wout_ref, cw_ref, cb_ref,
                  g_ref, b_ref, o_ref, hist_out_ref, buf_ref, *, T, alpha):
    @pl.when(pl.program_id(1) == 0)
    def _():
        buf_ref[PADROWS - HIST:PADROWS, :] = hist_ref[0]

    x = x_ref[0]
    xb = x.astype(bf16)
    c = jnp.dot(xb, wc_ref[...], preferred_element_type=f32)
    W = SCONV_WIDTH
    u = c[:, W:2 * W] * c[:, 2 * W:3 * W]
    u_conv = _conv_rows(buf_ref, slice(0, W), u, cw_ref[...], cb_ref[...], T)
    yc = (c[:, 0:W] * u_conv).astype(bf16)
    hist_out_ref[0] = buf_ref[PADROWS - HIST:PADROWS, :]

    merged = None
    for i, yb in enumerate((ya_ref[0], yb_ref[0], yc)):
        gate = _sigmoid(jnp.dot(xb, wg_ref[:, i * D_MODEL:(i + 1) * D_MODEL], preferred_element_type=f32))
        term = gate * jnp.dot(yb, wbr_ref[i], preferred_element_type=f32)
        merged = term if merged is None else merged + term
    out = jnp.dot(merged.astype(bf16), wout_ref[...], preferred_element_type=f32)
    o_ref[0] = _layer_norm(alpha * x + out, g_ref[...], b_ref[...])


def _merge(x, ya, yb, hist, wc, wg, wbr, wout, cw, cb, g, b, alpha):
    B, S, _ = x.shape
    T = _row_tile(S, 256)
    row = lambda w: pl.BlockSpec((1, T, w), lambda bb, t: (bb, t, 0))
    hspec = pl.BlockSpec((1, HIST, SCONV_WIDTH), lambda bb, t: (bb, 0, 0))
    return pl.pallas_call(
        functools.partial(_merge_kernel, T=T, alpha=alpha),
        grid=(B, S // T),
        in_specs=[row(D_MODEL), row(BRANCH_WIDTH), row(BRANCH_WIDTH), hspec,
                  _const_spec((D_MODEL, 3 * SCONV_WIDTH)), _const_spec((D_MODEL, N_BRANCH * D_MODEL)),
                  _const_spec((N_BRANCH, BRANCH_WIDTH, D_MODEL)), _const_spec((D_MODEL, D_MODEL)),
                  _const_spec((CONV_W, SCONV_WIDTH)), _const_spec((1, SCONV_WIDTH)),
                  _const_spec((1, D_MODEL)), _const_spec((1, D_MODEL))],
        out_specs=[row(D_MODEL), hspec],
        out_shape=[jax.ShapeDtypeStruct((B, S, D_MODEL), f32),
                   jax.ShapeDtypeStruct((B, HIST, SCONV_WIDTH), f32)],
        scratch_shapes=[pltpu.VMEM((PADROWS + T, SCONV_WIDTH), f32)],
        compiler_params=_params(("parallel", "arbitrary")),
        name="merge",
    )(x, ya, yb, hist, wc, wg, wbr, wout, cw, cb, g, b)


def _ffn_kernel(x_ref, hist_ref, wup_ref, cw_ref, cb_ref, wdn_ref, g_ref, b_ref, o_ref, hist_out_ref,
                buf_ref, *, T, alpha):
    @pl.when(pl.program_id(1) == 0)
    def _():
        buf_ref[PADROWS - HIST:PADROWS, :] = hist_ref[0]

    x = x_ref[0]
    xb = x.astype(bf16)
    acc = jnp.zeros((T, D_MODEL), f32)
    for cidx in range(D_FF // FFN_COLS):
        halves = []
        for off in (0, D_FF):
            cols = slice(off + cidx * FFN_COLS, off + (cidx + 1) * FFN_COLS)
            h = jnp.dot(xb, wup_ref[:, cols], preferred_element_type=f32)
            halves.append(_conv_rows(buf_ref, cols, h, cw_ref[:, cols], cb_ref[:, cols], T))
        a_g, b_v = halves
        act = (a_g * _sigmoid(a_g) * b_v).astype(bf16)
        acc = acc + jnp.dot(act, wdn_ref[cidx * FFN_COLS:(cidx + 1) * FFN_COLS, :],
                            preferred_element_type=f32)
    hist_out_ref[0] = buf_ref[PADROWS - HIST:PADROWS, :]
    o_ref[0] = _layer_norm(alpha * x + acc, g_ref[...], b_ref[...])


def _ffn(x, hist, wup, cw, cb, wdn, g, b, alpha):
    B, S, _ = x.shape
    T = _row_tile(S, 256)
    row = pl.BlockSpec((1, T, D_MODEL), lambda bb, t: (bb, t, 0))
    hspec = pl.BlockSpec((1, HIST, 2 * D_FF), lambda bb, t: (bb, 0, 0))
    return pl.pallas_call(
        functools.partial(_ffn_kernel, T=T, alpha=alpha),
        grid=(B, S // T),
        in_specs=[row, hspec, _const_spec((D_MODEL, 2 * D_FF)), _const_spec((CONV_W, 2 * D_FF)),
                  _const_spec((1, 2 * D_FF)), _const_spec((D_FF, D_MODEL)),
                  _const_spec((1, D_MODEL)), _const_spec((1, D_MODEL))],
        out_specs=[row, hspec],
        out_shape=[jax.ShapeDtypeStruct((B, S, D_MODEL), f32),
                   jax.ShapeDtypeStruct((B, HIST, 2 * D_FF), f32)],
        scratch_shapes=[pltpu.VMEM((PADROWS + T, 2 * D_FF), f32)],
        compiler_params=_params(("parallel", "arbitrary")),
        name="ffn",
    )(x, hist, wup, cw, cb, wdn, g, b)


def _layer_weights(l, depth, w_in, hgrn_lb_logits, hgrn_norm_g, sconv_w, sconv_b, w_branch, w_out,
                   ln1_g, ln1_b, w_up, ffn_conv_w, ffn_conv_b, w_down, ln2_g, ln2_b):
    offs = np.concatenate([[0], np.cumsum(IN_SIZES)]).tolist()
    sec = lambda i, j=None: w_in[l][:, offs[i]:offs[(i if j is None else j) + 1]]
    a_q, a_k, a_v, i_q, i_k, i_w = (sec(i) for i in range(6))
    zpad = lambda n: jnp.zeros((D_MODEL, n), w_in.dtype)
    w_row = jnp.concatenate([a_k, i_k, zpad(LANES - IDX_DIM), a_v], axis=1).astype(bf16)
    w_t = jnp.concatenate([a_q, i_q, a_v, i_w, zpad(WT_ROWS - 900)], axis=1).T.astype(bf16)
    lbp = jax.nn.softmax(hgrn_lb_logits.astype(f32), axis=0)
    lb = (jnp.cumsum(lbp, axis=0) - lbp[0])[l].reshape(1, HGRN_W)
    row = lambda a: a.reshape(1, -1).astype(f32)
    return dict(
        w_row=w_row, w_t=w_t, w_h=sec(6, 9).astype(bf16), lb=lb, norm_g=row(hgrn_norm_g[l]),
        w_c=sec(10, 12).astype(bf16), w_g=sec(13).astype(bf16),
        w_br=w_branch[l].astype(bf16), w_out=w_out[l].astype(bf16),
        sconv_w=sconv_w[l].astype(f32), sconv_b=row(sconv_b[l]),
        ln1_g=row(ln1_g[l]), ln1_b=row(ln1_b[l]),
        w_up=w_up[l].astype(bf16), ffn_w=ffn_conv_w[l].astype(f32), ffn_b=row(ffn_conv_b[l]),
        w_down=w_down[l].astype(bf16), ln2_g=row(ln2_g[l]), ln2_b=row(ln2_b[l]))


def _key_tiles(a, TK):
    B, L, C = a.shape
    NT = -(-L // TK)
    a = jnp.pad(a, ((0, 0), (0, NT * TK - L), (0, 0)))
    return a.reshape(B, NT, TK, C)


def _trunk_layer(x, tables, P, k_past, v_past, ki_past, s0, sc_hist, ffn_hist, w, alpha):
    B, S, _ = x.shape
    TK = KEY_TILE
    a = _attn_proj(x, w["w_row"], w["w_t"], *tables)
    k, v, ki = a["k"], a["v"], a["ki"]
    L = P + S
    if k_past is None and S % TK == 0:
        kb = a["kb"].reshape(B, S // TK, TK, -1)
        kib = a["kib"].reshape(B, S // TK, TK, -1)
        vt = a["vt"]
    else:
        kb, kib, v_all = a["kb"], a["kib"], v.astype(bf16)
        if k_past is not None:
            kb = jnp.concatenate([k_past.reshape(B, P, -1).astype(bf16), kb], axis=1)
            kib = jnp.concatenate([ki_past.astype(bf16), kib], axis=1)
            v_all = jnp.concatenate([v_past.reshape(B, P, -1).astype(bf16), v_all], axis=1)
        kb, kib = _key_tiles(kb, TK), _key_tiles(kib, TK)
        vt = _key_tiles(v_all, TK).reshape(B, -1, TK, KV_HEADS, HEAD_DIM).transpose(0, 1, 3, 4, 2)
        ones = jnp.zeros(vt.shape[:3] + (VT_ROWS - HEAD_DIM, TK), bf16).at[:, :, :, 0, :].set(1.0)
        vt = jnp.concatenate([vt, ones], axis=3)
    y_a = _dsa(a["qt"], a["qit"], a["wit"], kb, kib, vt, P, min(TOPK_MAX, L // 4))
    y_b, s_new = _hgrn(x, w["w_h"], w["lb"], w["norm_g"], s0)
    x1, sc_new = _merge(x, y_a, y_b, sc_hist, w["w_c"], w["w_g"], w["w_br"], w["w_out"],
                        w["sconv_w"], w["sconv_b"], w["ln1_g"], w["ln1_b"], alpha)
    x2, ffn_new = _ffn(x1, ffn_hist, w["w_up"], w["ffn_w"], w["ffn_b"], w["w_down"],
                       w["ln2_g"], w["ln2_b"], alpha)
    new = (k.reshape(B, S, KV_HEADS, HEAD_DIM), v.reshape(B, S, KV_HEADS, HEAD_DIM), ki,
           s_new, sc_new, ffn_new)
    return x2, new


def kernel(x_prompt, x_sample, cache_attn_k, cache_attn_v, cache_idx_k, state_hgrn, state_sconv,
           state_ffn_conv, w_in, hgrn_lb_logits, hgrn_norm_g, sconv_w, sconv_b, w_branch, w_out,
           ln1_g, ln1_b, w_up, ffn_conv_w, ffn_conv_b, w_down, ln2_g, ln2_b):
    depth = w_in.shape[0]
    alpha = (2 * depth) ** 0.25
    B, S, _ = x_prompt.shape
    DB, DS, _ = x_sample.shape
    P = cache_attn_k.shape[2]
    cs_p = _rotary_tables(jnp.arange(S))
    cs_s = _rotary_tables(P + jnp.arange(DS))
    xp, xs = x_prompt, x_sample
    st_p = [[] for _ in range(6)]
    st_s = [[] for _ in range(6)]
    for l in range(depth):
        w = _layer_weights(l, depth, w_in, hgrn_lb_logits, hgrn_norm_g, sconv_w, sconv_b, w_branch,
                           w_out, ln1_g, ln1_b, w_up, ffn_conv_w, ffn_conv_b, w_down, ln2_g, ln2_b)
        xp, new_p = _trunk_layer(
            xp, cs_p, 0, None, None, None,
            jnp.zeros((B, HGRN_HEADS, HGRN_DK, HGRN_DV), f32),
            jnp.zeros((B, HIST, SCONV_WIDTH), f32),
            jnp.zeros((B, HIST, 2 * D_FF), f32), w, alpha)
        xs, new_s = _trunk_layer(
            xs, cs_s, P, cache_attn_k[l], cache_attn_v[l], cache_idx_k[l], state_hgrn[l],
            state_sconv[l], state_ffn_conv[l], w, alpha)
        for j in range(6):
            st_p[j].append(new_p[j])
            st_s[j].append(new_s[j])
    outs_p = [jnp.stack(a, axis=0) for a in st_p]
    outs_s = [jnp.stack(a, axis=0) for a in st_s]
    return (xp, xs, *outs_p, *outs_s)
```

```python
import functools

import numpy as np
import jax
import jax.numpy as jnp
from jax import lax
from jax.experimental import pallas as pl
from jax.experimental.pallas import tpu as pltpu

D_MODEL = 1024
CHUNK = 64
N_HEADS = 8
HEAD_DIM = 64
KV_HEADS = 2
GROUPS = N_HEADS // KV_HEADS
IDX_HEADS = 4
IDX_DIM = 64
TOPK_MAX = 256
ROPE_THETA = 500000.0
ATTN_SCALE = HEAD_DIM ** -0.5
IDX_SCALE = (IDX_DIM ** -0.5) * (IDX_HEADS ** -0.5)
NEG = -1e30
HGRN_HEADS = 4
HGRN_DK = 128
HGRN_DV = 128
HGRN_W = HGRN_HEADS * HGRN_DK
SCONV_WIDTH = 512
CONV_W = 3
BRANCH_WIDTH = 512
N_BRANCH = 3
D_FF = 2816
LN_EPS = 1e-5
IN_SIZES = (N_HEADS * HEAD_DIM, KV_HEADS * HEAD_DIM, KV_HEADS * HEAD_DIM,
            IDX_HEADS * IDX_DIM, IDX_DIM, IDX_HEADS,
            HGRN_W, HGRN_W, HGRN_W, HGRN_W,
            SCONV_WIDTH, SCONV_WIDTH, SCONV_WIDTH,
            N_BRANCH * D_MODEL)

LANES = 128
VMEM_LIMIT = 56 * 1024 * 1024
KEY_TILE = 512
SEARCH_CHECKED = 24
SEARCH_STRIDE = 8
VT_ROWS = HEAD_DIM + 16
FFN_COLS = 256
INT_MIN = -2 ** 31

f32 = jnp.float32
bf16 = jnp.bfloat16
i32 = jnp.int32


def _row_tile(s, want):
    return want if s % want == 0 else s


def _const_spec(shape):
    nd = len(shape)
    return pl.BlockSpec(shape, lambda *_: (0,) * nd, pipeline_mode=pl.Buffered(1))


def _params(sem):
    return pltpu.CompilerParams(dimension_semantics=sem, vmem_limit_bytes=VMEM_LIMIT)


def _layer_norm(z, g, b):
    mu = jnp.mean(z, axis=-1, keepdims=True)
    d = z - mu
    var = jnp.mean(d * d, axis=-1, keepdims=True)
    return d * lax.rsqrt(var + LN_EPS) * g + b


def _sigmoid(x):
    return 1.0 / (1.0 + jnp.exp(-x))


Q_PRESCALE = ATTN_SCALE * float(np.log2(np.e))
ROT = HEAD_DIM // 4
WR_COLS = 384
WT_ROWS = 912
WI_ROWS = 8


def _attn_proj_kernel(x_ref, wr_ref, wt_ref, cs_ref, cst_ref, k_ref, v_ref, ki_ref, kb_ref, kib_ref,
                      vt_ref, qt_ref, qit_ref, wit_ref, *, T, TQ):
    xb = x_ref[0].astype(bf16)
    pr = jnp.dot(xb, wr_ref[...], preferred_element_type=f32)
    c, sa, sb = cs_ref[0], cs_ref[1], cs_ref[2]

    def rot(xg):
        return xg * c + pltpu.roll(xg, LANES - ROT // 2, 1) * sa + pltpu.roll(xg, ROT // 2, 1) * sb

    k = rot(pr[:, 0:LANES])
    ki = rot(pr[:, LANES:2 * LANES])[:, :IDX_DIM]
    k_ref[0] = k
    kb_ref[0] = k.astype(bf16)
    ki_ref[0] = ki
    kib_ref[0] = ki.astype(bf16)
    v_ref[0] = pr[:, 2 * LANES:3 * LANES]

    nt = (((1,), (1,)), ((), ()))
    pt = lax.dot_general(wt_ref[...], xb, nt, preferred_element_type=f32)
    ct, st = cst_ref[0], cst_ref[1]

    def rot_t(hb):
        lead = hb[0:ROT] * ct + jnp.concatenate([hb[ROT // 2:ROT], hb[0:ROT // 2]], axis=0) * st
        return jnp.concatenate([lead, hb[ROT:HEAD_DIM]], axis=0)

    nqb = T // TQ
    for h in range(N_HEADS):
        hb = (rot_t(pt[h * HEAD_DIM:(h + 1) * HEAD_DIM]) * Q_PRESCALE).astype(bf16)
        g, i = divmod(h, GROUPS)
        for n in range(nqb):
            qt_ref[0, n, g, :, i * TQ:(i + 1) * TQ] = hb[:, n * TQ:(n + 1) * TQ]
    base = N_HEADS * HEAD_DIM
    for h in range(IDX_HEADS):
        hb = rot_t(pt[base + h * IDX_DIM:base + (h + 1) * IDX_DIM]).astype(bf16)
        for n in range(nqb):
            qit_ref[0, n, :, h * TQ:(h + 1) * TQ] = hb[:, n * TQ:(n + 1) * TQ]
    base += IDX_HEADS * IDX_DIM
    ones_rows = jnp.where(lax.broadcasted_iota(i32, (VT_ROWS - HEAD_DIM, T), 0) == 0, 1.0, 0.0)
    for g in range(KV_HEADS):
        vt_ref[0, 0, g, 0:HEAD_DIM, :] = pt[base + g * HEAD_DIM:base + (g + 1) * HEAD_DIM].astype(bf16)
        vt_ref[0, 0, g, HEAD_DIM:VT_ROWS, :] = ones_rows.astype(bf16)
    base += KV_HEADS * HEAD_DIM
    for n in range(nqb):
        wit_ref[0, n] = pt[base:base + WI_ROWS, n * TQ:(n + 1) * TQ]


def _attn_proj(x, w_row, w_t, cs, cst):
    B, S, _ = x.shape
    T = _row_tile(S, KEY_TILE)
    TQ = _row_tile(S, 2 * CHUNK)
    nqb = T // TQ
    R = GROUPS * TQ
    row = lambda w: pl.BlockSpec((1, T, w), lambda b, t: (b, t, 0))
    names = ("k", "v", "ki", "kb", "kib", "vt", "qt", "qit", "wit")
    outs = pl.pallas_call(
        functools.partial(_attn_proj_kernel, T=T, TQ=TQ),
        grid=(B, S // T),
        in_specs=[row(D_MODEL), _const_spec((D_MODEL, WR_COLS)), _const_spec((WT_ROWS, D_MODEL)),
                  pl.BlockSpec((3, T, LANES), lambda b, t: (0, t, 0)),
                  pl.BlockSpec((2, ROT, T), lambda b, t: (0, 0, t))],
        out_specs=[row(128), row(128), row(IDX_DIM), row(128), row(IDX_DIM),
                   pl.BlockSpec((1, 1, KV_HEADS, VT_ROWS, T), lambda b, t: (b, t, 0, 0, 0)),
                   pl.BlockSpec((1, nqb, KV_HEADS, HEAD_DIM, R), lambda b, t: (b, t, 0, 0, 0)),
                   pl.BlockSpec((1, nqb, IDX_DIM, IDX_HEADS * TQ), lambda b, t: (b, t, 0, 0)),
                   pl.BlockSpec((1, nqb, WI_ROWS, TQ), lambda b, t: (b, t, 0, 0))],
        out_shape=[jax.ShapeDtypeStruct((B, S, 128), f32),
                   jax.ShapeDtypeStruct((B, S, 128), f32),
                   jax.ShapeDtypeStruct((B, S, IDX_DIM), f32),
                   jax.ShapeDtypeStruct((B, S, 128), bf16),
                   jax.ShapeDtypeStruct((B, S, IDX_DIM), bf16),
                   jax.ShapeDtypeStruct((B, S // T, KV_HEADS, VT_ROWS, T), bf16),
                   jax.ShapeDtypeStruct((B, S // TQ, KV_HEADS, HEAD_DIM, R), bf16),
                   jax.ShapeDtypeStruct((B, S // TQ, IDX_DIM, IDX_HEADS * TQ), bf16),
                   jax.ShapeDtypeStruct((B, S // TQ, WI_ROWS, TQ), f32)],
        compiler_params=_params(("parallel", "parallel")),
        name="attn_proj",
    )(x, w_row, w_t, cs, cst)
    return dict(zip(names, outs))


def _rotary_tables(pos):
    half = ROT // 2
    inv_freq = ROPE_THETA ** (-jnp.arange(half, dtype=f32) / half)
    ang = pos.astype(f32)[:, None] * inv_freq[None, :]
    cos = jnp.cos(ang)
    sin = jnp.sin(ang)
    S = pos.shape[0]
    one = jnp.ones((S, HEAD_DIM - ROT), f32)
    zero = jnp.zeros((S, HEAD_DIM - ROT), f32)
    zh = jnp.zeros((S, half), f32)
    c = jnp.concatenate([cos, cos, one], axis=1)
    sa = jnp.concatenate([-sin, zh, zero], axis=1)
    sb = jnp.concatenate([zh, sin, zero], axis=1)
    tile2 = lambda a: jnp.concatenate([a, a], axis=1)
    cs = jnp.stack([tile2(c), tile2(sa), tile2(sb)], axis=0)
    cst = jnp.stack([jnp.concatenate([cos, cos], axis=1).T,
                     jnp.concatenate([-sin, sin], axis=1).T], axis=0)
    return cs, cst


def _dsa_kernel(qt_ref, qit_ref, wit_ref, ki_ref, k_ref, vt_ref, tri_ref, o_ref,
                s_ref, lm_ref, p_ref, qpad_ref, m_ref, acc_ref, *, TQ, TK, P, top):
    qb = pl.program_id(1)
    q0 = P + qb * TQ
    nk = (q0 + TQ + TK - 1) // TK
    n_full = q0 // TK
    NC = TK // LANES
    R = GROUPS * TQ
    q_lim = ((q0 + lax.broadcasted_iota(i32, (1, TQ), 1)) // CHUNK + 1) * CHUNK
    wit = wit_ref[0, 0] * IDX_SCALE
    zero_head = jnp.zeros((HEAD_DIM, R), bf16)
    qpad_ref[0] = jnp.concatenate([qt_ref[0, 0, 0], zero_head], axis=0)
    qpad_ref[1] = jnp.concatenate([zero_head, qt_ref[0, 0, 1]], axis=0)

    def score_tile(masked, j, carry):
        rel = jnp.maximum(jnp.dot(ki_ref[0, j], qit_ref[0, 0], preferred_element_type=f32), 0.0)
        score = wit[0:1, :] * rel[:, 0:TQ]
        for h in range(1, IDX_HEADS):
            score = score + wit[h:h + 1, :] * rel[:, h * TQ:(h + 1) * TQ]
        if masked:
            k_pos = j * TK + lax.broadcasted_iota(i32, (TK, TQ), 0)
            score = jnp.where(k_pos < q_lim, score, NEG)
        bits = pltpu.bitcast(score, i32)
        s_ref[j] = jnp.where(bits >= 0, bits, bits ^ 0x7FFFFFFF)
        return carry

    lax.fori_loop(0, n_full, functools.partial(score_tile, False), 0)
    lax.fori_loop(n_full, nk, functools.partial(score_tile, True), 0)

    def count(pred):
        def body(j, acc):
            hit = jnp.where(pred(s_ref[j]), 1, 0)
            parts = [hit[r * 8:(r + 1) * 8] for r in range(TK // 8)]
            while len(parts) > 1:
                parts = [a + b for a, b in zip(parts[0::2], parts[1::2])]
            return acc + parts[0]
        acc = lax.fori_loop(0, nk, body, jnp.zeros((8, TQ), i32))
        return jnp.sum(acc, axis=0, keepdims=True)

    def step(bit, t, cnt):
        cand = t | jnp.left_shift(jnp.int32(1), bit)
        c = count(lambda k: k >= cand)
        ok = c >= top
        return jnp.where(ok, cand, t), jnp.where(ok, c, cnt)

    c0 = count(lambda k: k >= 0)
    t = jnp.where(c0 >= top, 0, INT_MIN).astype(i32)
    cnt = jnp.where(c0 >= top, c0, jnp.int32(2 ** 30))
    t, cnt = lax.fori_loop(0, 31 - SEARCH_CHECKED, lambda i, st: step(30 - i, *st), (t, cnt))

    def checked_steps(st):
        bit, t, cnt, _ = st
        for s in range(SEARCH_STRIDE):
            t, cnt = step(bit - s, t, cnt)
        done = jnp.min(jnp.where(cnt == top, 1, 0))
        return bit - SEARCH_STRIDE, t, cnt, done

    more = lambda st: jnp.logical_and(st[0] >= 0, st[3] == 0)
    _, t, cnt, _ = lax.while_loop(more, checked_steps,
                                  (jnp.int32(SEARCH_CHECKED - 1), t, cnt, jnp.int32(0)))
    neg_key = np.array(NEG, np.float32).view(np.int32) ^ 0x7FFFFFFF
    any_excess = jnp.max(jnp.where(cnt > top, 1, 0)) > 0

    @pl.when(jnp.logical_not(any_excess))
    def _():
        thr = jnp.where(t == neg_key, t + 1, t)

        def fast(j, carry):
            s_ref[j] = pltpu.bitcast(jnp.where(s_ref[j] >= thr, 0.0, NEG), i32)
            return carry
        lax.fori_loop(0, nk, fast, 0)

    @pl.when(any_excess)
    def _():
        n_gt = count(lambda k: k > t)
        need = jnp.where(t == neg_key, 0, top - n_gt).astype(f32)

        def ranked(j, taken):
            keys = s_ref[j]
            for c in range(NC):
                kc = keys[c * LANES:(c + 1) * LANES, :]
                eq = jnp.where(kc == t, 1.0, 0.0)
                rank = taken + jnp.dot(tri_ref[...], eq.astype(bf16), preferred_element_type=f32)
                pick = jnp.where(kc > t, 1.0, jnp.where(rank <= need, eq, 0.0))
                s_ref[j, c * LANES:(c + 1) * LANES, :] = pltpu.bitcast(
                    jnp.where(pick > 0.5, 0.0, NEG), i32)
                taken = taken + jnp.sum(eq, axis=0, keepdims=True)
            return taken
        lax.fori_loop(0, nk, ranked, jnp.zeros((1, TQ), f32))

    m_ref[...] = jnp.full(m_ref.shape, NEG, f32)
    acc_ref[...] = jnp.zeros(acc_ref.shape, f32)

    def logits(j, slot):
        jc = jnp.minimum(j, nk - 1)
        tile_max = []
        for g in range(KV_HEADS):
            cmax = jnp.full((8, R), NEG, f32)
            for c in range(NC):
                rows = slice(c * LANES, (c + 1) * LANES)
                bias = pltpu.bitcast(s_ref[jc, rows, :], f32)
                lm = (jnp.dot(k_ref[0, jc, rows, :], qpad_ref[g], preferred_element_type=f32)
                      + jnp.concatenate([bias] * GROUPS, axis=1))
                lm_ref[slot, g, rows, :] = lm
                cmax = jnp.maximum(cmax, jnp.max(lm.reshape(LANES // 8, 8, R), axis=0))
            tile_max.append(jnp.max(cmax, axis=0, keepdims=True))
        return tuple(tile_max)

    def accumulate(j, slot, tile_max):
        for g in range(KV_HEADS):
            m_old = m_ref[g]
            m_new = jnp.maximum(m_old, tile_max[g])
            m_ref[g] = m_new
            for c in range(NC):
                rows = slice(c * LANES, (c + 1) * LANES)
                p_ref[slot, g, rows, :] = jnp.exp2(lm_ref[slot, g, rows, :] - m_new).astype(bf16)
            acc_ref[g] = (jnp.exp2(m_old - m_new) * acc_ref[g]
                          + jnp.dot(vt_ref[0, j, g], p_ref[slot, g], preferred_element_type=f32))

    def tile_pair(i, tile_max):
        nxt = logits(2 * i + 1, 1)
        accumulate(2 * i, 0, tile_max)
        tile_max = logits(2 * i + 2, 0)
        accumulate(2 * i + 1, 1, nxt)
        return tile_max

    last_max = lax.fori_loop(0, nk // 2, tile_pair, logits(0, 0))

    @pl.when(nk % 2 == 1)
    def _():
        accumulate(nk - 1, 0, last_max)

    for g in range(KV_HEADS):
        o_t = acc_ref[g, 0:HEAD_DIM, :] / acc_ref[g, HEAD_DIM:HEAD_DIM + 1, :]
        for i in range(0, GROUPS, 2):
            pair = jnp.concatenate([o_t[:, i * TQ:(i + 1) * TQ], o_t[:, (i + 1) * TQ:(i + 2) * TQ]], axis=0)
            h = g * GROUPS + i
            o_ref[0, :, h * HEAD_DIM:(h + 2) * HEAD_DIM] = pair.T.astype(bf16)


def _dsa(qt, qit, wit, kb, kib, vt, P, top):
    B, nq = qt.shape[:2]
    R = qt.shape[-1]
    TQ = R // GROUPS
    NT, TK = kb.shape[1:3]
    assert top <= TK and TQ % CHUNK == 0
    tri = jnp.asarray(np.tril(np.ones((LANES, LANES), np.float32)), bf16)
    return pl.pallas_call(
        functools.partial(_dsa_kernel, TQ=TQ, TK=TK, P=P, top=top),
        grid=(B, nq),
        in_specs=[pl.BlockSpec((1, 1, KV_HEADS, HEAD_DIM, R), lambda b, i: (b, i, 0, 0, 0)),
                  pl.BlockSpec((1, 1, IDX_DIM, IDX_HEADS * TQ), lambda b, i: (b, i, 0, 0)),
                  pl.BlockSpec((1, 1, WI_ROWS, TQ), lambda b, i: (b, i, 0, 0)),
                  pl.BlockSpec((1, NT, TK, IDX_DIM), lambda b, i: (b, 0, 0, 0)),
                  pl.BlockSpec((1, NT, TK, KV_HEADS * HEAD_DIM), lambda b, i: (b, 0, 0, 0)),
                  pl.BlockSpec((1, NT, KV_HEADS, VT_ROWS, TK), lambda b, i: (b, 0, 0, 0, 0)),
                  _const_spec((LANES, LANES))],
        out_specs=pl.BlockSpec((1, TQ, N_HEADS * HEAD_DIM), lambda b, i: (b, i, 0)),
        out_shape=jax.ShapeDtypeStruct((B, nq * TQ, N_HEADS * HEAD_DIM), bf16),
        scratch_shapes=[pltpu.VMEM((NT, TK, TQ), i32),
                        pltpu.VMEM((2, KV_HEADS, TK, R), f32),
                        pltpu.VMEM((2, KV_HEADS, TK, R), bf16),
                        pltpu.VMEM((KV_HEADS, KV_HEADS * HEAD_DIM, R), bf16),
                        pltpu.VMEM((KV_HEADS, 1, R), f32),
                        pltpu.VMEM((KV_HEADS, VT_ROWS, R), f32)],
        compiler_params=_params(("parallel", "arbitrary")),
        name="dsa",
    )(qt, qit, wit, kib, kb, vt, tri)


def _hgrn_mats(T):
    r = np.arange(T)[:, None]
    c = np.arange(T)[None, :]
    mats = [c <= r, c > r]
    b = T // 2
    while b >= 1:
        blk_r = r // (2 * b)
        mid = blk_r * 2 * b + b
        same = blk_r == c // (2 * b)
        late = r >= mid
        mats.append(same & ((late & (c >= mid) & (c <= r)) | (~late & (c > r) & (c < mid))))
        b //= 2
    return np.concatenate(mats, axis=0).astype(np.float32)


def _hgrn_kernel(x_ref, w_ref, lb_ref, g_ref, mats_ref, s0_ref, y_ref, st_out_ref, st_ref, *, T):
    t_idx = pl.program_id(1)

    @pl.when(t_idx == 0)
    def _():
        st_ref[...] = s0_ref[0]

    xb = x_ref[0].astype(bf16)
    proj = jnp.dot(xb, w_ref[...], preferred_element_type=f32)
    hq = proj[:, 0:HGRN_W]
    z = proj[:, HGRN_W:2 * HGRN_W]
    hv = proj[:, 2 * HGRN_W:3 * HGRN_W]
    hg = proj[:, 3 * HGRN_W:4 * HGRN_W]
    lb = lb_ref[...]
    logf = (jnp.minimum(z, 0.0) - jnp.log1p(jnp.exp(-jnp.abs(z)))) + jnp.log1p(lb * jnp.exp(-z))
    kk = (1.0 - lb) * _sigmoid(-z)
    qq = hq * _sigmoid(hq)
    gate = hg * _sigmoid(hg)

    hi = logf.astype(bf16)
    r1 = logf - hi.astype(f32)
    mid = r1.astype(bf16)
    lo = (r1 - mid.astype(f32)).astype(bf16)
    mats = mats_ref[...]
    e_all = (jnp.dot(mats, hi, preferred_element_type=f32)
             + jnp.dot(mats, mid, preferred_element_type=f32)
             + jnp.dot(mats, lo, preferred_element_type=f32))

    row = lax.broadcasted_iota(i32, (T, 1), 0)
    col = lax.broadcasted_iota(i32, (1, T), 1)
    nt = (((1,), (1,)), ((), ()))
    tn = (((0,), (0,)), ((), ()))
    for h in range(HGRN_HEADS):
        sl = slice(h * HGRN_DK, (h + 1) * HGRN_DK)
        q_h, k_h, v_h = qq[:, sl], kk[:, sl], hv[:, sl]
        vb = v_h.astype(bf16)
        cum = e_all[0:T, sl]
        suf = e_all[T:2 * T, sl]
        st = st_ref[h]
        o = lax.dot_general((q_h * jnp.exp(cum)).astype(bf16), st.astype(bf16), nt,
                            preferred_element_type=f32)
        o = o + jnp.sum(q_h * k_h, axis=-1, keepdims=True) * v_h
        scores = jnp.zeros((T, T), f32)
        b = T // 2
        lvl = 2
        while b >= 1:
            xdec = jnp.exp(e_all[lvl * T:(lvl + 1) * T, sl])
            late = ((row // b) % 2) == 1
            a_m = jnp.where(late, q_h * xdec, 0.0).astype(bf16)
            b_m = jnp.where(late, 0.0, k_h * xdec).astype(bf16)
            sc = lax.dot_general(a_m, b_m, nt, preferred_element_type=f32)
            if 2 * b < T:
                sc = jnp.where((row // (2 * b)) == (col // (2 * b)), sc, 0.0)
            scores = scores + sc
            b //= 2
            lvl += 1
        o = o + jnp.dot(scores.astype(bf16), vb, preferred_element_type=f32)
        kd = (k_h * jnp.exp(suf)).astype(bf16)
        st_ref[h] = (st * jnp.exp(cum[T - 1:T, :])
                     + lax.dot_general(vb, kd, tn, preferred_element_type=f32))
        o = o * lax.rsqrt(jnp.mean(o * o, axis=-1, keepdims=True) + LN_EPS) * g_ref[...]
        y_ref[0, :, sl] = (o * gate[:, sl]).astype(bf16)

    st_out_ref[0] = st_ref[...]


def _hgrn(x, w_h, lb, norm_g, s0):
    B, S, _ = x.shape
    T = _row_tile(S, 2 * CHUNK)
    mats = jnp.asarray(_hgrn_mats(T), bf16)
    nm = mats.shape[0]
    st0 = jnp.swapaxes(s0.astype(f32), 2, 3)
    y, st = pl.pallas_call(
        functools.partial(_hgrn_kernel, T=T),
        grid=(B, S // T),
        in_specs=[pl.BlockSpec((1, T, D_MODEL), lambda b, t: (b, t, 0)),
                  _const_spec((D_MODEL, 4 * HGRN_W)),
                  _const_spec((1, HGRN_W)),
                  _const_spec((1, HGRN_DV)),
                  _const_spec((nm, T)),
                  pl.BlockSpec((1, HGRN_HEADS, HGRN_DV, HGRN_DK), lambda b, t: (b, 0, 0, 0))],
        out_specs=[pl.BlockSpec((1, T, HGRN_W), lambda b, t: (b, t, 0)),
                   pl.BlockSpec((1, HGRN_HEADS, HGRN_DV, HGRN_DK), lambda b, t: (b, 0, 0, 0))],
        out_shape=[jax.ShapeDtypeStruct((B, S, HGRN_W), bf16),
                   jax.ShapeDtypeStruct((B, HGRN_HEADS, HGRN_DV, HGRN_DK), f32)],
        scratch_shapes=[pltpu.VMEM((HGRN_HEADS, HGRN_DV, HGRN_DK), f32)],
        compiler_params=_params(("parallel", "arbitrary")),
        name="hgrn",
    )(x, w_h, lb, norm_g, mats, st0)
    return y, jnp.swapaxes(st, 2, 3)


HIST = CONV_W - 1
PADROWS = 8


def _conv_rows(buf_ref, cols, u, w, bias, T):
    buf_ref[PADROWS:PADROWS + T, cols] = u
    y = bias
    for j in range(CONV_W):
        y = y + w[j:j + 1, :] * buf_ref[PADROWS - HIST + j:PADROWS - HIST + j + T, cols]
    buf_ref[PADROWS - HIST:PADROWS, cols] = u[T - HIST:T, :]
    return y


def _merge_kernel(x_ref, ya_ref, yb_ref, hist_ref, wc_ref, wg_ref, wbr_ref, wout_ref, cw_ref, cb_ref,
                  g_ref, b_ref, o_ref, hist_out_ref, buf_ref, *, T, alpha):
    @pl.when(pl.program_id(1) == 0)
    def _():
        buf_ref[PADROWS - HIST:PADROWS, :] = hist_ref[0]

    x = x_ref[0]
    xb = x.astype(bf16)
    c = jnp.dot(xb, wc_ref[...], preferred_element_type=f32)
    W = SCONV_WIDTH
    u = c[:, W:2 * W] * c[:, 2 * W:3 * W]
    u_conv = _conv_rows(buf_ref, slice(0, W), u, cw_ref[...], cb_ref[...], T)
    yc = (c[:, 0:W] * u_conv).astype(bf16)
    hist_out_ref[0] = buf_ref[PADROWS - HIST:PADROWS, :]

    merged = None
    for i, yb in enumerate((ya_ref[0], yb_ref[0], yc)):
        gate = _sigmoid(jnp.dot(xb, wg_ref[:, i * D_MODEL:(i + 1) * D_MODEL], preferred_element_type=f32))
        term = gate * jnp.dot(yb, wbr_ref[i], preferred_element_type=f32)
        merged = term if merged is None else merged + term
    out = jnp.dot(merged.astype(bf16), wout_ref[...], preferred_element_type=f32)
    o_ref[0] = _layer_norm(alpha * x + out, g_ref[...], b_ref[...])


def _merge(x, ya, yb, hist, wc, wg, wbr, wout, cw, cb, g, b, alpha):
    B, S, _ = x.shape
    T = _row_tile(S, 256)
    row = lambda w: pl.BlockSpec((1, T, w), lambda bb, t: (bb, t, 0))
    hspec = pl.BlockSpec((1, HIST, SCONV_WIDTH), lambda bb, t: (bb, 0, 0))
    return pl.pallas_call(
        functools.partial(_merge_kernel, T=T, alpha=alpha),
        grid=(B, S // T),
        in_specs=[row(D_MODEL), row(BRANCH_WIDTH), row(BRANCH_WIDTH), hspec,
                  _const_spec((D_MODEL, 3 * SCONV_WIDTH)), _const_spec((D_MODEL, N_BRANCH * D_MODEL)),
                  _const_spec((N_BRANCH, BRANCH_WIDTH, D_MODEL)), _const_spec((D_MODEL, D_MODEL)),
                  _const_spec((CONV_W, SCONV_WIDTH)), _const_spec((1, SCONV_WIDTH)),
                  _const_spec((1, D_MODEL)), _const_spec((1, D_MODEL))],
        out_specs=[row(D_MODEL), hspec],
        out_shape=[jax.ShapeDtypeStruct((B, S, D_MODEL), f32),
                   jax.ShapeDtypeStruct((B, HIST, SCONV_WIDTH), f32)],
        scratch_shapes=[pltpu.VMEM((PADROWS + T, SCONV_WIDTH), f32)],
        compiler_params=_params(("parallel", "arbitrary")),
        name="merge",
    )(x, ya, yb, hist, wc, wg, wbr, wout, cw, cb, g, b)


def _ffn_kernel(x_ref, hist_ref, wup_ref, cw_ref, cb_ref, wdn_ref, g_ref, b_ref, o_ref, hist_out_ref,
                buf_ref, *, T, alpha):
    @pl.when(pl.program_id(1) == 0)
    def _():
        buf_ref[PADROWS - HIST:PADROWS, :] = hist_ref[0]

    x = x_ref[0]
    xb = x.astype(bf16)
    acc = jnp.zeros((T, D_MODEL), f32)
    for cidx in range(D_FF // FFN_COLS):
        halves = []
        for off in (0, D_FF):
            cols = slice(off + cidx * FFN_COLS, off + (cidx + 1) * FFN_COLS)
            h = jnp.dot(xb, wup_ref[:, cols], preferred_element_type=f32)
            halves.append(_conv_rows(buf_ref, cols, h, cw_ref[:, cols], cb_ref[:, cols], T))
        a_g, b_v = halves
        act = (a_g * _sigmoid(a_g) * b_v).astype(bf16)
        acc = acc + jnp.dot(act, wdn_ref[cidx * FFN_COLS:(cidx + 1) * FFN_COLS, :],
                            preferred_element_type=f32)
    hist_out_ref[0] = buf_ref[PADROWS - HIST:PADROWS, :]
    o_ref[0] = _layer_norm(alpha * x + acc, g_ref[...], b_ref[...])


def _ffn(x, hist, wup, cw, cb, wdn, g, b, alpha):
    B, S, _ = x.shape
    T = _row_tile(S, 256)
    row = pl.BlockSpec((1, T, D_MODEL), lambda bb, t: (bb, t, 0))
    hspec = pl.BlockSpec((1, HIST, 2 * D_FF), lambda bb, t: (bb, 0, 0))
    return pl.pallas_call(
        functools.partial(_ffn_kernel, T=T, alpha=alpha),
        grid=(B, S // T),
        in_specs=[row, hspec, _const_spec((D_MODEL, 2 * D_FF)), _const_spec((CONV_W, 2 * D_FF)),
                  _const_spec((1, 2 * D_FF)), _const_spec((D_FF, D_MODEL)),
                  _const_spec((1, D_MODEL)), _const_spec((1, D_MODEL))],
        out_specs=[row, hspec],
        out_shape=[jax.ShapeDtypeStruct((B, S, D_MODEL), f32),
                   jax.ShapeDtypeStruct((B, HIST, 2 * D_FF), f32)],
        scratch_shapes=[pltpu.VMEM((PADROWS + T, 2 * D_FF), f32)],
        compiler_params=_params(("parallel", "arbitrary")),
        name="ffn",
    )(x, hist, wup, cw, cb, wdn, g, b)


def _cast_kernel(x_ref, o_ref):
    o_ref[...] = x_ref[...].astype(o_ref.dtype)


def _to_bf16(w):
    D, Rw, C = w.shape
    T = _row_tile(Rw, LANES)
    spec = pl.BlockSpec((1, T, C), lambda d, t: (d, t, 0))
    return pl.pallas_call(_cast_kernel, grid=(D, Rw // T), in_specs=[spec], out_specs=spec,
                          out_shape=jax.ShapeDtypeStruct(w.shape, bf16),
                          compiler_params=_params(("parallel", "parallel")), name="to_bf16")(w)


def _layer_weights(l, depth, w_in, hgrn_lb_logits, hgrn_norm_g, sconv_w, sconv_b, w_branch, w_out,
                   ln1_g, ln1_b, w_up, ffn_conv_w, ffn_conv_b, w_down, ln2_g, ln2_b):
    offs = np.concatenate([[0], np.cumsum(IN_SIZES)]).tolist()
    sec = lambda i, j=None: w_in[l][:, offs[i]:offs[(i if j is None else j) + 1]]
    a_q, a_k, a_v, i_q, i_k, i_w = (sec(i) for i in range(6))
    zpad = lambda n: jnp.zeros((D_MODEL, n), w_in.dtype)
    w_row = jnp.concatenate([a_k, i_k, zpad(LANES - IDX_DIM), a_v], axis=1).astype(bf16)
    w_t = jnp.concatenate([a_q, i_q, a_v, i_w, zpad(WT_ROWS - 900)], axis=1).T.astype(bf16)
    lbp = jax.nn.softmax(hgrn_lb_logits.astype(f32), axis=0)
    lb = (jnp.cumsum(lbp, axis=0) - lbp[0])[l].reshape(1, HGRN_W)
    row = lambda a: a.reshape(1, -1).astype(f32)
    return dict(
        w_row=w_row, w_t=w_t, w_h=sec(6, 9).astype(bf16), lb=lb, norm_g=row(hgrn_norm_g[l]),
        w_c=sec(10, 12).astype(bf16), w_g=sec(13).astype(bf16),
        w_br=w_branch[l].astype(bf16), w_out=w_out[l].astype(bf16),
        sconv_w=sconv_w[l].astype(f32), sconv_b=row(sconv_b[l]),
        ln1_g=row(ln1_g[l]), ln1_b=row(ln1_b[l]),
        w_up=w_up[l].astype(bf16), ffn_w=ffn_conv_w[l].astype(f32), ffn_b=row(ffn_conv_b[l]),
        w_down=w_down[l].astype(bf16), ln2_g=row(ln2_g[l]), ln2_b=row(ln2_b[l]))


def _key_tiles(a, TK):
    B, L, C = a.shape
    NT = -(-L // TK)
    a = jnp.pad(a, ((0, 0), (0, NT * TK - L), (0, 0)))
    return a.reshape(B, NT, TK, C)


def _trunk_layer(x, tables, P, k_past, v_past, ki_past, s0, sc_hist, ffn_hist, w, alpha):
    B, S, _ = x.shape
    TK = KEY_TILE
    a = _attn_proj(x, w["w_row"], w["w_t"], *tables)
    k, v, ki = a["k"], a["v"], a["ki"]
    L = P + S
    if k_past is None and S % TK == 0:
        kb = a["kb"].reshape(B, S // TK, TK, -1)
        kib = a["kib"].reshape(B, S // TK, TK, -1)
        vt = a["vt"]
    else:
        kb, kib, v_all = a["kb"], a["kib"], v.astype(bf16)
        if k_past is not None:
            kb = jnp.concatenate([k_past.reshape(B, P, -1).astype(bf16), kb], axis=1)
            kib = jnp.concatenate([ki_past.astype(bf16), kib], axis=1)
            v_all = jnp.concatenate([v_past.reshape(B, P, -1).astype(bf16), v_all], axis=1)
        kb, kib = _key_tiles(kb, TK), _key_tiles(kib, TK)
        vt = _key_tiles(v_all, TK).reshape(B, -1, TK, KV_HEADS, HEAD_DIM).transpose(0, 1, 3, 4, 2)
        ones = jnp.zeros(vt.shape[:3] + (VT_ROWS - HEAD_DIM, TK), bf16).at[:, :, :, 0, :].set(1.0)
        vt = jnp.concatenate([vt, ones], axis=3)
    y_a = _dsa(a["qt"], a["qit"], a["wit"], kb, kib, vt, P, min(TOPK_MAX, L // 4))
    y_b, s_new = _hgrn(x, w["w_h"], w["lb"], w["norm_g"], s0)
    x1, sc_new = _merge(x, y_a, y_b, sc_hist, w["w_c"], w["w_g"], w["w_br"], w["w_out"],
                        w["sconv_w"], w["sconv_b"], w["ln1_g"], w["ln1_b"], alpha)
    x2, ffn_new = _ffn(x1, ffn_hist, w["w_up"], w["ffn_w"], w["ffn_b"], w["w_down"],
                       w["ln2_g"], w["ln2_b"], alpha)
    new = (k.reshape(B, S, KV_HEADS, HEAD_DIM), v.reshape(B, S, KV_HEADS, HEAD_DIM), ki,
           s_new, sc_new, ffn_new)
    return x2, new


def kernel(x_prompt, x_sample, cache_attn_k, cache_attn_v, cache_idx_k, state_hgrn, state_sconv,
           state_ffn_conv, w_in, hgrn_lb_logits, hgrn_norm_g, sconv_w, sconv_b, w_branch, w_out,
           ln1_g, ln1_b, w_up, ffn_conv_w, ffn_conv_b, w_down, ln2_g, ln2_b):
    depth = w_in.shape[0]
    alpha = (2 * depth) ** 0.25
    B, S, _ = x_prompt.shape
    DB, DS, _ = x_sample.shape
    P = cache_attn_k.shape[2]
    cs_p = _rotary_tables(jnp.arange(S))
    cs_s = _rotary_tables(P + jnp.arange(DS))
    xp, xs = x_prompt, x_sample
    st_p = [[] for _ in range(6)]
    st_s = [[] for _ in range(6)]
    w_in_b = _to_bf16(w_in)
    for l in range(depth):
        w = _layer_weights(l, depth, w_in_b, hgrn_lb_logits, hgrn_norm_g, sconv_w, sconv_b, w_branch,
                           w_out, ln1_g, ln1_b, w_up, ffn_conv_w, ffn_conv_b, w_down, ln2_g, ln2_b)
        xp, new_p = _trunk_layer(
            xp, cs_p, 0, None, None, None,
            jnp.zeros((B, HGRN_HEADS, HGRN_DK, HGRN_DV), f32),
            jnp.zeros((B, HIST, SCONV_WIDTH), f32),
            jnp.zeros((B, HIST, 2 * D_FF), f32), w, alpha)
        xs, new_s = _trunk_layer(
            xs, cs_s, P, cache_attn_k[l], cache_attn_v[l], cache_idx_k[l], state_hgrn[l],
            state_sconv[l], state_ffn_conv[l], w, alpha)
        for j in range(6):
            st_p[j].append(new_p[j])
            st_s[j].append(new_s[j])
    outs_p = [jnp.stack(a, axis=0) for a in st_p]
    outs_s = [jnp.stack(a, axis=0) for a in st_s]
    return (xp, xs, *outs_p, *outs_s)
```

```python
import functools

import numpy as np
import jax
import jax.numpy as jnp
from jax import lax
from jax.experimental import pallas as pl
from jax.experimental.pallas import tpu as pltpu

D_MODEL = 1024
CHUNK = 64
N_HEADS = 8
HEAD_DIM = 64
KV_HEADS = 2
GROUPS = N_HEADS // KV_HEADS
IDX_HEADS = 4
IDX_DIM = 64
TOPK_MAX = 256
ROPE_THETA = 500000.0
ATTN_SCALE = HEAD_DIM ** -0.5
IDX_SCALE = (IDX_DIM ** -0.5) * (IDX_HEADS ** -0.5)
NEG = -1e30
HGRN_HEADS = 4
HGRN_DK = 128
HGRN_DV = 128
HGRN_W = HGRN_HEADS * HGRN_DK
SCONV_WIDTH = 512
CONV_W = 3
BRANCH_WIDTH = 512
N_BRANCH = 3
D_FF = 2816
LN_EPS = 1e-5
IN_SIZES = (N_HEADS * HEAD_DIM, KV_HEADS * HEAD_DIM, KV_HEADS * HEAD_DIM,
            IDX_HEADS * IDX_DIM, IDX_DIM, IDX_HEADS,
            HGRN_W, HGRN_W, HGRN_W, HGRN_W,
            SCONV_WIDTH, SCONV_WIDTH, SCONV_WIDTH,
            N_BRANCH * D_MODEL)

LANES = 128
VMEM_LIMIT = 56 * 1024 * 1024
KEY_TILE = 512
QUERY_TILE = 256
SEARCH_CHECKED = 24
SEARCH_STRIDE = 8
VT_ROWS = HEAD_DIM + 16
FFN_COLS = 256
INT_MIN = -2 ** 31

f32 = jnp.float32
bf16 = jnp.bfloat16
i32 = jnp.int32


def _row_tile(s, want):
    return want if s % want == 0 else s


def _const_spec(shape):
    nd = len(shape)
    return pl.BlockSpec(shape, lambda *_: (0,) * nd, pipeline_mode=pl.Buffered(1))


def _params(sem):
    return pltpu.CompilerParams(dimension_semantics=sem, vmem_limit_bytes=VMEM_LIMIT)


def _layer_norm(z, g, b):
    mu = jnp.mean(z, axis=-1, keepdims=True)
    d = z - mu
    var = jnp.mean(d * d, axis=-1, keepdims=True)
    return d * lax.rsqrt(var + LN_EPS) * g + b


def _sigmoid(x):
    return 1.0 / (1.0 + jnp.exp(-x))


Q_PRESCALE = ATTN_SCALE * float(np.log2(np.e))
ROT = HEAD_DIM // 4
WR_COLS = 384
WT_ROWS = 912
WI_ROWS = 8


def _attn_proj_kernel(x_ref, wr_ref, wt_ref, cs_ref, cst_ref, k_ref, v_ref, ki_ref, kb_ref, kib_ref,
                      vt_ref, qt_ref, qit_ref, wit_ref, *, T, TQ):
    xb = x_ref[0].astype(bf16)
    pr = jnp.dot(xb, wr_ref[...], preferred_element_type=f32)
    c, sa, sb = cs_ref[0], cs_ref[1], cs_ref[2]

    def rot(xg):
        return xg * c + pltpu.roll(xg, LANES - ROT // 2, 1) * sa + pltpu.roll(xg, ROT // 2, 1) * sb

    k = rot(pr[:, 0:LANES])
    ki = rot(pr[:, LANES:2 * LANES])[:, :IDX_DIM]
    k_ref[0] = k
    kb_ref[0] = k.astype(bf16)
    ki_ref[0] = ki
    kib_ref[0] = ki.astype(bf16)
    v_ref[0] = pr[:, 2 * LANES:3 * LANES]

    nt = (((1,), (1,)), ((), ()))
    pt = lax.dot_general(wt_ref[...], xb, nt, preferred_element_type=f32)
    ct, st = cst_ref[0], cst_ref[1]

    def rot_t(hb):
        lead = hb[0:ROT] * ct + jnp.concatenate([hb[ROT // 2:ROT], hb[0:ROT // 2]], axis=0) * st
        return jnp.concatenate([lead, hb[ROT:HEAD_DIM]], axis=0)

    nqb = T // TQ
    for h in range(N_HEADS):
        hb = (rot_t(pt[h * HEAD_DIM:(h + 1) * HEAD_DIM]) * Q_PRESCALE).astype(bf16)
        g, i = divmod(h, GROUPS)
        for n in range(nqb):
            qt_ref[0, n, g, :, i * TQ:(i + 1) * TQ] = hb[:, n * TQ:(n + 1) * TQ]
    base = N_HEADS * HEAD_DIM
    for h in range(IDX_HEADS):
        hb = rot_t(pt[base + h * IDX_DIM:base + (h + 1) * IDX_DIM]).astype(bf16)
        for n in range(nqb):
            qit_ref[0, n, :, h * TQ:(h + 1) * TQ] = hb[:, n * TQ:(n + 1) * TQ]
    base += IDX_HEADS * IDX_DIM
    ones_rows = jnp.where(lax.broadcasted_iota(i32, (VT_ROWS - HEAD_DIM, T), 0) == 0, 1.0, 0.0)
    for g in range(KV_HEADS):
        vt_ref[0, 0, g, 0:HEAD_DIM, :] = pt[base + g * HEAD_DIM:base + (g + 1) * HEAD_DIM].astype(bf16)
        vt_ref[0, 0, g, HEAD_DIM:VT_ROWS, :] = ones_rows.astype(bf16)
    base += KV_HEADS * HEAD_DIM
    for n in range(nqb):
        wit_ref[0, n] = pt[base:base + WI_ROWS, n * TQ:(n + 1) * TQ]


def _attn_proj(x, w_row, w_t, cs, cst):
    B, S, _ = x.shape
    T = _row_tile(S, KEY_TILE)
    TQ = _row_tile(S, QUERY_TILE)
    nqb = T // TQ
    R = GROUPS * TQ
    row = lambda w: pl.BlockSpec((1, T, w), lambda b, t: (b, t, 0))
    names = ("k", "v", "ki", "kb", "kib", "vt", "qt", "qit", "wit")
    outs = pl.pallas_call(
        functools.partial(_attn_proj_kernel, T=T, TQ=TQ),
        grid=(B, S // T),
        in_specs=[row(D_MODEL), _const_spec((D_MODEL, WR_COLS)), _const_spec((WT_ROWS, D_MODEL)),
                  pl.BlockSpec((3, T, LANES), lambda b, t: (0, t, 0)),
                  pl.BlockSpec((2, ROT, T), lambda b, t: (0, 0, t))],
        out_specs=[row(128), row(128), row(IDX_DIM), row(128), row(IDX_DIM),
                   pl.BlockSpec((1, 1, KV_HEADS, VT_ROWS, T), lambda b, t: (b, t, 0, 0, 0)),
                   pl.BlockSpec((1, nqb, KV_HEADS, HEAD_DIM, R), lambda b, t: (b, t, 0, 0, 0)),
                   pl.BlockSpec((1, nqb, IDX_DIM, IDX_HEADS * TQ), lambda b, t: (b, t, 0, 0)),
                   pl.BlockSpec((1, nqb, WI_ROWS, TQ), lambda b, t: (b, t, 0, 0))],
        out_shape=[jax.ShapeDtypeStruct((B, S, 128), f32),
                   jax.ShapeDtypeStruct((B, S, 128), f32),
                   jax.ShapeDtypeStruct((B, S, IDX_DIM), f32),
                   jax.ShapeDtypeStruct((B, S, 128), bf16),
                   jax.ShapeDtypeStruct((B, S, IDX_DIM), bf16),
                   jax.ShapeDtypeStruct((B, S // T, KV_HEADS, VT_ROWS, T), bf16),
                   jax.ShapeDtypeStruct((B, S // TQ, KV_HEADS, HEAD_DIM, R), bf16),
                   jax.ShapeDtypeStruct((B, S // TQ, IDX_DIM, IDX_HEADS * TQ), bf16),
                   jax.ShapeDtypeStruct((B, S // TQ, WI_ROWS, TQ), f32)],
        compiler_params=_params(("parallel", "parallel")),
        name="attn_proj",
    )(x, w_row, w_t, cs, cst)
    return dict(zip(names, outs))


def _rotary_tables(pos):
    half = ROT // 2
    inv_freq = ROPE_THETA ** (-jnp.arange(half, dtype=f32) / half)
    ang = pos.astype(f32)[:, None] * inv_freq[None, :]
    cos = jnp.cos(ang)
    sin = jnp.sin(ang)
    S = pos.shape[0]
    one = jnp.ones((S, HEAD_DIM - ROT), f32)
    zero = jnp.zeros((S, HEAD_DIM - ROT), f32)
    zh = jnp.zeros((S, half), f32)
    c = jnp.concatenate([cos, cos, one], axis=1)
    sa = jnp.concatenate([-sin, zh, zero], axis=1)
    sb = jnp.concatenate([zh, sin, zero], axis=1)
    tile2 = lambda a: jnp.concatenate([a, a], axis=1)
    cs = jnp.stack([tile2(c), tile2(sa), tile2(sb)], axis=0)
    cst = jnp.stack([jnp.concatenate([cos, cos], axis=1).T,
                     jnp.concatenate([-sin, sin], axis=1).T], axis=0)
    return cs, cst


def _dsa_kernel(qt_ref, qit_ref, wit_ref, ki_ref, k_ref, vt_ref, tri_ref, o_ref,
                s_ref, qpad_ref, m_ref, acc_ref, *bufs, TQ, TK, P, top):
    lm_refs = (bufs[0:KV_HEADS], bufs[KV_HEADS:2 * KV_HEADS])
    p_refs = (bufs[2 * KV_HEADS:3 * KV_HEADS], bufs[3 * KV_HEADS:4 * KV_HEADS])
    qb = pl.program_id(1)
    q0 = P + qb * TQ
    nk = (q0 + TQ + TK - 1) // TK
    n_full = q0 // TK
    NC = TK // LANES
    R = GROUPS * TQ
    q_lim = ((q0 + lax.broadcasted_iota(i32, (1, TQ), 1)) // CHUNK + 1) * CHUNK
    wit = wit_ref[0, 0] * IDX_SCALE
    zero_head = jnp.zeros((HEAD_DIM, R), bf16)
    qpad_ref[0] = jnp.concatenate([qt_ref[0, 0, 0], zero_head], axis=0)
    qpad_ref[1] = jnp.concatenate([zero_head, qt_ref[0, 0, 1]], axis=0)

    def score_tile(masked, j, carry):
        rel = jnp.maximum(jnp.dot(ki_ref[0, j], qit_ref[0, 0], preferred_element_type=f32), 0.0)
        score = wit[0:1, :] * rel[:, 0:TQ]
        for h in range(1, IDX_HEADS):
            score = score + wit[h:h + 1, :] * rel[:, h * TQ:(h + 1) * TQ]
        if masked:
            k_pos = j * TK + lax.broadcasted_iota(i32, (TK, TQ), 0)
            score = jnp.where(k_pos < q_lim, score, NEG)
        bits = pltpu.bitcast(score, i32)
        s_ref[j] = jnp.where(bits >= 0, bits, bits ^ 0x7FFFFFFF)
        return carry

    lax.fori_loop(0, n_full, functools.partial(score_tile, False), 0)
    lax.fori_loop(n_full, nk, functools.partial(score_tile, True), 0)

    def count(pred):
        def body(j, acc):
            hit = jnp.where(pred(s_ref[j]), 1, 0)
            parts = [hit[r * 8:(r + 1) * 8] for r in range(TK // 8)]
            while len(parts) > 1:
                parts = [a + b for a, b in zip(parts[0::2], parts[1::2])]
            return acc + parts[0]
        acc = lax.fori_loop(0, nk, body, jnp.zeros((8, TQ), i32))
        return jnp.sum(acc, axis=0, keepdims=True)

    def step(bit, t, cnt):
        cand = t | jnp.left_shift(jnp.int32(1), bit)
        c = count(lambda k: k >= cand)
        ok = c >= top
        return jnp.where(ok, cand, t), jnp.where(ok, c, cnt)

    c0 = count(lambda k: k >= 0)
    t = jnp.where(c0 >= top, 0, INT_MIN).astype(i32)
    cnt = jnp.where(c0 >= top, c0, jnp.int32(2 ** 30))
    t, cnt = lax.fori_loop(0, 31 - SEARCH_CHECKED, lambda i, st: step(30 - i, *st), (t, cnt))

    def checked_steps(st):
        bit, t, cnt, _ = st
        for s in range(SEARCH_STRIDE):
            t, cnt = step(bit - s, t, cnt)
        done = jnp.min(jnp.where(cnt == top, 1, 0))
        return bit - SEARCH_STRIDE, t, cnt, done

    more = lambda st: jnp.logical_and(st[0] >= 0, st[3] == 0)
    _, t, cnt, _ = lax.while_loop(more, checked_steps,
                                  (jnp.int32(SEARCH_CHECKED - 1), t, cnt, jnp.int32(0)))
    neg_key = np.array(NEG, np.float32).view(np.int32) ^ 0x7FFFFFFF
    any_excess = jnp.max(jnp.where(cnt > top, 1, 0)) > 0

    @pl.when(jnp.logical_not(any_excess))
    def _():
        thr = jnp.where(t == neg_key, t + 1, t)

        def fast(j, carry):
            s_ref[j] = pltpu.bitcast(jnp.where(s_ref[j] >= thr, 0.0, NEG), i32)
            return carry
        lax.fori_loop(0, nk, fast, 0)

    @pl.when(any_excess)
    def _():
        n_gt = count(lambda k: k > t)
        need = jnp.where(t == neg_key, 0, top - n_gt).astype(f32)

        def ranked(j, taken):
            keys = s_ref[j]
            for c in range(NC):
                kc = keys[c * LANES:(c + 1) * LANES, :]
                eq = jnp.where(kc == t, 1.0, 0.0)
                rank = taken + jnp.dot(tri_ref[...], eq.astype(bf16), preferred_element_type=f32)
                pick = jnp.where(kc > t, 1.0, jnp.where(rank <= need, eq, 0.0))
                s_ref[j, c * LANES:(c + 1) * LANES, :] = pltpu.bitcast(
                    jnp.where(pick > 0.5, 0.0, NEG), i32)
                taken = taken + jnp.sum(eq, axis=0, keepdims=True)
            return taken
        lax.fori_loop(0, nk, ranked, jnp.zeros((1, TQ), f32))

    m_ref[...] = jnp.full(m_ref.shape, NEG, f32)
    acc_ref[...] = jnp.zeros(acc_ref.shape, f32)

    def tile_step(nxt, cur):
        if cur is not None:
            j, slot, tile_max = cur
            m_old = [m_ref[g] for g in range(KV_HEADS)]
            m_new = [jnp.maximum(m_old[g], tile_max[g]) for g in range(KV_HEADS)]
            for g in range(KV_HEADS):
                m_ref[g] = m_new[g]
        if nxt is not None:
            jn, slot_n = nxt
            jc = jnp.minimum(jn, nk - 1)
        out = []
        for g in range(KV_HEADS):
            cmax = jnp.full((8, R), NEG, f32)
            for c in range(NC):
                rows = slice(c * LANES, (c + 1) * LANES)
                if nxt is not None:
                    bias = pltpu.bitcast(s_ref[jc, rows, :], f32)
                    lm = (jnp.dot(k_ref[0, jc, rows, :], qpad_ref[g], preferred_element_type=f32)
                          + jnp.concatenate([bias] * GROUPS, axis=1))
                    lm_refs[slot_n][g][rows, :] = lm
                    cmax = jnp.maximum(cmax, jnp.max(lm.reshape(LANES // 8, 8, R), axis=0))
                if cur is not None:
                    p_refs[slot][g][rows, :] = jnp.exp2(lm_refs[slot][g][rows, :] - m_new[g]).astype(bf16)
            out.append(jnp.max(cmax, axis=0, keepdims=True))
        if cur is not None:
            for g in range(KV_HEADS):
                acc_ref[g] = (jnp.exp2(m_old[g] - m_new[g]) * acc_ref[g]
                              + jnp.dot(vt_ref[0, j, g], p_refs[slot][g][...], preferred_element_type=f32))
        return tuple(out)

    def tile_pair(i, tile_max):
        nxt_max = tile_step((2 * i + 1, 1), (2 * i, 0, tile_max))
        return tile_step((2 * i + 2, 0), (2 * i + 1, 1, nxt_max))

    last_max = lax.fori_loop(0, nk // 2, tile_pair, tile_step((0, 0), None))

    @pl.when(nk % 2 == 1)
    def _():
        tile_step(None, (nk - 1, 0, last_max))

    for g in range(KV_HEADS):
        o_t = acc_ref[g, 0:HEAD_DIM, :] / acc_ref[g, HEAD_DIM:HEAD_DIM + 1, :]
        for i in range(0, GROUPS, 2):
            pair = jnp.concatenate([o_t[:, i * TQ:(i + 1) * TQ], o_t[:, (i + 1) * TQ:(i + 2) * TQ]], axis=0)
            h = g * GROUPS + i
            o_ref[0, :, h * HEAD_DIM:(h + 2) * HEAD_DIM] = pair.T.astype(bf16)


def _dsa(qt, qit, wit, kb, kib, vt, P, top):
    B, nq = qt.shape[:2]
    R = qt.shape[-1]
    TQ = R // GROUPS
    NT, TK = kb.shape[1:3]
    assert top <= TK and TQ % CHUNK == 0
    tri = jnp.asarray(np.tril(np.ones((LANES, LANES), np.float32)), bf16)
    return pl.pallas_call(
        functools.partial(_dsa_kernel, TQ=TQ, TK=TK, P=P, top=top),
        grid=(B, nq),
        in_specs=[pl.BlockSpec((1, 1, KV_HEADS, HEAD_DIM, R), lambda b, i: (b, i, 0, 0, 0)),
                  pl.BlockSpec((1, 1, IDX_DIM, IDX_HEADS * TQ), lambda b, i: (b, i, 0, 0)),
                  pl.BlockSpec((1, 1, WI_ROWS, TQ), lambda b, i: (b, i, 0, 0)),
                  pl.BlockSpec((1, NT, TK, IDX_DIM), lambda b, i: (b, 0, 0, 0)),
                  pl.BlockSpec((1, NT, TK, KV_HEADS * HEAD_DIM), lambda b, i: (b, 0, 0, 0)),
                  pl.BlockSpec((1, NT, KV_HEADS, VT_ROWS, TK), lambda b, i: (b, 0, 0, 0, 0)),
                  _const_spec((LANES, LANES))],
        out_specs=pl.BlockSpec((1, TQ, N_HEADS * HEAD_DIM), lambda b, i: (b, i, 0)),
        out_shape=jax.ShapeDtypeStruct((B, nq * TQ, N_HEADS * HEAD_DIM), bf16),
        scratch_shapes=[pltpu.VMEM((NT, TK, TQ), i32),
                        pltpu.VMEM((KV_HEADS, KV_HEADS * HEAD_DIM, R), bf16),
                        pltpu.VMEM((KV_HEADS, 1, R), f32),
                        pltpu.VMEM((KV_HEADS, VT_ROWS, R), f32)]
        + [pltpu.VMEM((TK, R), f32)] * (2 * KV_HEADS)
        + [pltpu.VMEM((TK, R), bf16)] * (2 * KV_HEADS),
        compiler_params=_params(("parallel", "arbitrary")),
        name="dsa",
    )(qt, qit, wit, kib, kb, vt, tri)


def _hgrn_mats(T):
    r = np.arange(T)[:, None]
    c = np.arange(T)[None, :]
    mats = [c <= r, c > r]
    b = T // 2
    while b >= 1:
        blk_r = r // (2 * b)
        mid = blk_r * 2 * b + b
        same = blk_r == c // (2 * b)
        late = r >= mid
        mats.append(same & ((late & (c >= mid) & (c <= r)) | (~late & (c > r) & (c < mid))))
        b //= 2
    return np.concatenate(mats, axis=0).astype(np.float32)


def _hgrn_kernel(x_ref, w_ref, lb_ref, g_ref, mats_ref, s0_ref, y_ref, st_out_ref, st_ref, *, T):
    t_idx = pl.program_id(1)

    @pl.when(t_idx == 0)
    def _():
        st_ref[...] = s0_ref[0]

    xb = x_ref[0].astype(bf16)
    proj = jnp.dot(xb, w_ref[...], preferred_element_type=f32)
    hq = proj[:, 0:HGRN_W]
    z = proj[:, HGRN_W:2 * HGRN_W]
    hv = proj[:, 2 * HGRN_W:3 * HGRN_W]
    hg = proj[:, 3 * HGRN_W:4 * HGRN_W]
    lb = lb_ref[...]
    logf = (jnp.minimum(z, 0.0) - jnp.log1p(jnp.exp(-jnp.abs(z)))) + jnp.log1p(lb * jnp.exp(-z))
    kk = (1.0 - lb) * _sigmoid(-z)
    qq = hq * _sigmoid(hq)
    gate = hg * _sigmoid(hg)

    hi = logf.astype(bf16)
    r1 = logf - hi.astype(f32)
    mid = r1.astype(bf16)
    lo = (r1 - mid.astype(f32)).astype(bf16)
    mats = mats_ref[...]
    e_all = (jnp.dot(mats, hi, preferred_element_type=f32)
             + jnp.dot(mats, mid, preferred_element_type=f32)
             + jnp.dot(mats, lo, preferred_element_type=f32))

    row = lax.broadcasted_iota(i32, (T, 1), 0)
    col = lax.broadcasted_iota(i32, (1, T), 1)
    nt = (((1,), (1,)), ((), ()))
    tn = (((0,), (0,)), ((), ()))
    for h in range(HGRN_HEADS):
        sl = slice(h * HGRN_DK, (h + 1) * HGRN_DK)
        q_h, k_h, v_h = qq[:, sl], kk[:, sl], hv[:, sl]
        vb = v_h.astype(bf16)
        cum = e_all[0:T, sl]
        suf = e_all[T:2 * T, sl]
        st = st_ref[h]
        o = lax.dot_general((q_h * jnp.exp(cum)).astype(bf16), st.astype(bf16), nt,
                            preferred_element_type=f32)
        o = o + jnp.sum(q_h * k_h, axis=-1, keepdims=True) * v_h
        scores = jnp.zeros((T, T), f32)
        b = T // 2
        lvl = 2
        while b >= 1:
            xdec = jnp.exp(e_all[lvl * T:(lvl + 1) * T, sl])
            late = ((row // b) % 2) == 1
            a_m = jnp.where(late, q_h * xdec, 0.0).astype(bf16)
            b_m = jnp.where(late, 0.0, k_h * xdec).astype(bf16)
            sc = lax.dot_general(a_m, b_m, nt, preferred_element_type=f32)
            if 2 * b < T:
                sc = jnp.where((row // (2 * b)) == (col // (2 * b)), sc, 0.0)
            scores = scores + sc
            b //= 2
            lvl += 1
        o = o + jnp.dot(scores.astype(bf16), vb, preferred_element_type=f32)
        kd = (k_h * jnp.exp(suf)).astype(bf16)
        st_ref[h] = (st * jnp.exp(cum[T - 1:T, :])
                     + lax.dot_general(vb, kd, tn, preferred_element_type=f32))
        o = o * lax.rsqrt(jnp.mean(o * o, axis=-1, keepdims=True) + LN_EPS) * g_ref[...]
        y_ref[0, :, sl] = (o * gate[:, sl]).astype(bf16)

    st_out_ref[0] = st_ref[...]


def _hgrn(x, w_h, lb, norm_g, s0):
    B, S, _ = x.shape
    T = _row_tile(S, 2 * CHUNK)
    mats = jnp.asarray(_hgrn_mats(T), bf16)
    nm = mats.shape[0]
    st0 = jnp.swapaxes(s0.astype(f32), 2, 3)
    y, st = pl.pallas_call(
        functools.partial(_hgrn_kernel, T=T),
        grid=(B, S // T),
        in_specs=[pl.BlockSpec((1, T, D_MODEL), lambda b, t: (b, t, 0)),
                  _const_spec((D_MODEL, 4 * HGRN_W)),
                  _const_spec((1, HGRN_W)),
                  _const_spec((1, HGRN_DV)),
                  _const_spec((nm, T)),
                  pl.BlockSpec((1, HGRN_HEADS, HGRN_DV, HGRN_DK), lambda b, t: (b, 0, 0, 0))],
        out_specs=[pl.BlockSpec((1, T, HGRN_W), lambda b, t: (b, t, 0)),
                   pl.BlockSpec((1, HGRN_HEADS, HGRN_DV, HGRN_DK), lambda b, t: (b, 0, 0, 0))],
        out_shape=[jax.ShapeDtypeStruct((B, S, HGRN_W), bf16),
                   jax.ShapeDtypeStruct((B, HGRN_HEADS, HGRN_DV, HGRN_DK), f32)],
        scratch_shapes=[pltpu.VMEM((HGRN_HEADS, HGRN_DV, HGRN_DK), f32)],
        compiler_params=_params(("parallel", "arbitrary")),
        name="hgrn",
    )(x, w_h, lb, norm_g, mats, st0)
    return y, jnp.swapaxes(st, 2, 3)


HIST = CONV_W - 1
PADROWS = 8


def _conv_rows(buf_ref, cols, u, w, bias, T):
    buf_ref[PADROWS:PADROWS + T, cols] = u
    y = bias
    for j in range(CONV_W):
        y = y + w[j:j + 1, :] * buf_ref[PADROWS - HIST + j:PADROWS - HIST + j + T, cols]
    buf_ref[PADROWS - HIST:PADROWS, cols] = u[T - HIST:T, :]
    return y


def _merge_kernel(x_ref, ya_ref, yb_ref, hist_ref, wc_ref, wg_ref, wbr_ref, wout_ref, cw_ref, cb_ref,
                  g_ref, b_ref, o_ref, hist_out_ref, buf_ref, *, T, alpha):
    @pl.when(pl.program_id(1) == 0)
    def _():
        buf_ref[PADROWS - HIST:PADROWS, :] = hist_ref[0]

    x = x_ref[0]
    xb = x.astype(bf16)
    c = jnp.dot(xb, wc_ref[...], preferred_element_type=f32)
    W = SCONV_WIDTH
    u = c[:, W:2 * W] * c[:, 2 * W:3 * W]
    u_conv = _conv_rows(buf_ref, slice(0, W), u, cw_ref[...], cb_ref[...], T)
    yc = (c[:, 0:W] * u_conv).astype(bf16)
    hist_out_ref[0] = buf_ref[PADROWS - HIST:PADROWS, :]

    merged = None
    for i, yb in enumerate((ya_ref[0], yb_ref[0], yc)):
        gate = _sigmoid(jnp.dot(xb, wg_ref[:, i * D_MODEL:(i + 1) * D_MODEL], preferred_element_type=f32))
        term = gate * jnp.dot(yb, wbr_ref[i], preferred_element_type=f32)
        merged = term if merged is None else merged + term
    out = jnp.dot(merged.astype(bf16), wout_ref[...], preferred_element_type=f32)
    o_ref[0] = _layer_norm(alpha * x + out, g_ref[...], b_ref[...])


def _merge(x, ya, yb, hist, wc, wg, wbr, wout, cw, cb, g, b, alpha):
    B, S, _ = x.shape
    T = _row_tile(S, 256)
    row = lambda w: pl.BlockSpec((1, T, w), lambda bb, t: (bb, t, 0))
    hspec = pl.BlockSpec((1, HIST, SCONV_WIDTH), lambda bb, t: (bb, 0, 0))
    return pl.pallas_call(
        functools.partial(_merge_kernel, T=T, alpha=alpha),
        grid=(B, S // T),
        in_specs=[row(D_MODEL), row(BRANCH_WIDTH), row(BRANCH_WIDTH), hspec,
                  _const_spec((D_MODEL, 3 * SCONV_WIDTH)), _const_spec((D_MODEL, N_BRANCH * D_MODEL)),
                  _const_spec((N_BRANCH, BRANCH_WIDTH, D_MODEL)), _const_spec((D_MODEL, D_MODEL)),
                  _const_spec((CONV_W, SCONV_WIDTH)), _const_spec((1, SCONV_WIDTH)),
                  _const_spec((1, D_MODEL)), _const_spec((1, D_MODEL))],
        out_specs=[row(D_MODEL), hspec],
        out_shape=[jax.ShapeDtypeStruct((B, S, D_MODEL), f32),
                   jax.ShapeDtypeStruct((B, HIST, SCONV_WIDTH), f32)],
        scratch_shapes=[pltpu.VMEM((PADROWS + T, SCONV_WIDTH), f32)],
        compiler_params=_params(("parallel", "arbitrary")),
        name="merge",
    )(x, ya, yb, hist, wc, wg, wbr, wout, cw, cb, g, b)


def _ffn_kernel(x_ref, hist_ref, wup_ref, cw_ref, cb_ref, wdn_ref, g_ref, b_ref, o_ref, hist_out_ref,
                buf_ref, *, T, alpha):
    @pl.when(pl.program_id(1) == 0)
    def _():
        buf_ref[PADROWS - HIST:PADROWS, :] = hist_ref[0]

    x = x_ref[0]
    xb = x.astype(bf16)
    acc = jnp.zeros((T, D_MODEL), f32)
    for cidx in range(D_FF // FFN_COLS):
        halves = []
        for off in (0, D_FF):
            cols = slice(off + cidx * FFN_COLS, off + (cidx + 1) * FFN_COLS)
            h = jnp.dot(xb, wup_ref[:, cols], preferred_element_type=f32)
            halves.append(_conv_rows(buf_ref, cols, h, cw_ref[:, cols], cb_ref[:, cols], T))
        a_g, b_v = halves
        act = (a_g * _sigmoid(a_g) * b_v).astype(bf16)
        acc = acc + jnp.dot(act, wdn_ref[cidx * FFN_COLS:(cidx + 1) * FFN_COLS, :],
                            preferred_element_type=f32)
    hist_out_ref[0] = buf_ref[PADROWS - HIST:PADROWS, :]
    o_ref[0] = _layer_norm(alpha * x + acc, g_ref[...], b_ref[...])


def _ffn(x, hist, wup, cw, cb, wdn, g, b, alpha):
    B, S, _ = x.shape
    T = _row_tile(S, 256)
    row = pl.BlockSpec((1, T, D_MODEL), lambda bb, t: (bb, t, 0))
    hspec = pl.BlockSpec((1, HIST, 2 * D_FF), lambda bb, t: (bb, 0, 0))
    return pl.pallas_call(
        functools.partial(_ffn_kernel, T=T, alpha=alpha),
        grid=(B, S // T),
        in_specs=[row, hspec, _const_spec((D_MODEL, 2 * D_FF)), _const_spec((CONV_W, 2 * D_FF)),
                  _const_spec((1, 2 * D_FF)), _const_spec((D_FF, D_MODEL)),
                  _const_spec((1, D_MODEL)), _const_spec((1, D_MODEL))],
        out_specs=[row, hspec],
        out_shape=[jax.ShapeDtypeStruct((B, S, D_MODEL), f32),
                   jax.ShapeDtypeStruct((B, HIST, 2 * D_FF), f32)],
        scratch_shapes=[pltpu.VMEM((PADROWS + T, 2 * D_FF), f32)],
        compiler_params=_params(("parallel", "arbitrary")),
        name="ffn",
    )(x, hist, wup, cw, cb, wdn, g, b)


def _cast_kernel(x_ref, o_ref):
    o_ref[...] = x_ref[...].astype(o_ref.dtype)


def _to_bf16(w):
    D, Rw, C = w.shape
    T = _row_tile(Rw, LANES)
    spec = pl.BlockSpec((1, T, C), lambda d, t: (d, t, 0))
    return pl.pallas_call(_cast_kernel, grid=(D, Rw // T), in_specs=[spec], out_specs=spec,
                          out_shape=jax.ShapeDtypeStruct(w.shape, bf16),
                          compiler_params=_params(("parallel", "parallel")), name="to_bf16")(w)


def _layer_weights(l, depth, w_in, hgrn_lb_logits, hgrn_norm_g, sconv_w, sconv_b, w_branch, w_out,
                   ln1_g, ln1_b, w_up, ffn_conv_w, ffn_conv_b, w_down, ln2_g, ln2_b):
    offs = np.concatenate([[0], np.cumsum(IN_SIZES)]).tolist()
    sec = lambda i, j=None: w_in[l][:, offs[i]:offs[(i if j is None else j) + 1]]
    a_q, a_k, a_v, i_q, i_k, i_w = (sec(i) for i in range(6))
    zpad = lambda n: jnp.zeros((D_MODEL, n), w_in.dtype)
    w_row = jnp.concatenate([a_k, i_k, zpad(LANES - IDX_DIM), a_v], axis=1).astype(bf16)
    w_t = jnp.concatenate([a_q, i_q, a_v, i_w, zpad(WT_ROWS - 900)], axis=1).T.astype(bf16)
    lbp = jax.nn.softmax(hgrn_lb_logits.astype(f32), axis=0)
    lb = (jnp.cumsum(lbp, axis=0) - lbp[0])[l].reshape(1, HGRN_W)
    row = lambda a: a.reshape(1, -1).astype(f32)
    return dict(
        w_row=w_row, w_t=w_t, w_h=sec(6, 9).astype(bf16), lb=lb, norm_g=row(hgrn_norm_g[l]),
        w_c=sec(10, 12).astype(bf16), w_g=sec(13).astype(bf16),
        w_br=w_branch[l].astype(bf16), w_out=w_out[l].astype(bf16),
        sconv_w=sconv_w[l].astype(f32), sconv_b=row(sconv_b[l]),
        ln1_g=row(ln1_g[l]), ln1_b=row(ln1_b[l]),
        w_up=w_up[l].astype(bf16), ffn_w=ffn_conv_w[l].astype(f32), ffn_b=row(ffn_conv_b[l]),
        w_down=w_down[l].astype(bf16), ln2_g=row(ln2_g[l]), ln2_b=row(ln2_b[l]))


def _key_tiles(a, TK):
    B, L, C = a.shape
    NT = -(-L // TK)
    a = jnp.pad(a, ((0, 0), (0, NT * TK - L), (0, 0)))
    return a.reshape(B, NT, TK, C)


def _trunk_layer(x, tables, P, k_past, v_past, ki_past, s0, sc_hist, ffn_hist, w, alpha):
    B, S, _ = x.shape
    TK = KEY_TILE
    a = _attn_proj(x, w["w_row"], w["w_t"], *tables)
    k, v, ki = a["k"], a["v"], a["ki"]
    L = P + S
    if k_past is None and S % TK == 0:
        kb = a["kb"].reshape(B, S // TK, TK, -1)
        kib = a["kib"].reshape(B, S // TK, TK, -1)
        vt = a["vt"]
    else:
        kb, kib, v_all = a["kb"], a["kib"], v.astype(bf16)
        if k_past is not None:
            kb = jnp.concatenate([k_past.reshape(B, P, -1).astype(bf16), kb], axis=1)
            kib = jnp.concatenate([ki_past.astype(bf16), kib], axis=1)
            v_all = jnp.concatenate([v_past.reshape(B, P, -1).astype(bf16), v_all], axis=1)
        kb, kib = _key_tiles(kb, TK), _key_tiles(kib, TK)
        vt = _key_tiles(v_all, TK).reshape(B, -1, TK, KV_HEADS, HEAD_DIM).transpose(0, 1, 3, 4, 2)
        ones = jnp.zeros(vt.shape[:3] + (VT_ROWS - HEAD_DIM, TK), bf16).at[:, :, :, 0, :].set(1.0)
        vt = jnp.concatenate([vt, ones], axis=3)
    y_a = _dsa(a["qt"], a["qit"], a["wit"], kb, kib, vt, P, min(TOPK_MAX, L // 4))
    y_b, s_new = _hgrn(x, w["w_h"], w["lb"], w["norm_g"], s0)
    x1, sc_new = _merge(x, y_a, y_b, sc_hist, w["w_c"], w["w_g"], w["w_br"], w["w_out"],
                        w["sconv_w"], w["sconv_b"], w["ln1_g"], w["ln1_b"], alpha)
    x2, ffn_new = _ffn(x1, ffn_hist, w["w_up"], w["ffn_w"], w["ffn_b"], w["w_down"],
                       w["ln2_g"], w["ln2_b"], alpha)
    new = (k.reshape(B, S, KV_HEADS, HEAD_DIM), v.reshape(B, S, KV_HEADS, HEAD_DIM), ki,
           s_new, sc_new, ffn_new)
    return x2, new


def kernel(x_prompt, x_sample, cache_attn_k, cache_attn_v, cache_idx_k, state_hgrn, state_sconv,
           state_ffn_conv, w_in, hgrn_lb_logits, hgrn_norm_g, sconv_w, sconv_b, w_branch, w_out,
           ln1_g, ln1_b, w_up, ffn_conv_w, ffn_conv_b, w_down, ln2_g, ln2_b):
    depth = w_in.shape[0]
    alpha = (2 * depth) ** 0.25
    B, S, _ = x_prompt.shape
    DB, DS, _ = x_sample.shape
    P = cache_attn_k.shape[2]
    cs_p = _rotary_tables(jnp.arange(S))
    cs_s = _rotary_tables(P + jnp.arange(DS))
    xp, xs = x_prompt, x_sample
    st_p = [[] for _ in range(6)]
    st_s = [[] for _ in range(6)]
    w_in_b = _to_bf16(w_in)
    for l in range(depth):
        w = _layer_weights(l, depth, w_in_b, hgrn_lb_logits, hgrn_norm_g, sconv_w, sconv_b, w_branch,
                           w_out, ln1_g, ln1_b, w_up, ffn_conv_w, ffn_conv_b, w_down, ln2_g, ln2_b)
        xp, new_p = _trunk_layer(
            xp, cs_p, 0, None, None, None,
            jnp.zeros((B, HGRN_HEADS, HGRN_DK, HGRN_DV), f32),
            jnp.zeros((B, HIST, SCONV_WIDTH), f32),
            jnp.zeros((B, HIST, 2 * D_FF), f32), w, alpha)
        xs, new_s = _trunk_layer(
            xs, cs_s, P, cache_attn_k[l], cache_attn_v[l], cache_idx_k[l], state_hgrn[l],
            state_sconv[l], state_ffn_conv[l], w, alpha)
        for j in range(6):
            st_p[j].append(new_p[j])
            st_s[j].append(new_s[j])
    outs_p = [jnp.stack(a, axis=0) for a in st_p]
    outs_s = [jnp.stack(a, axis=0) for a in st_s]
    return (xp, xs, *outs_p, *outs_s)
```

```python
import functools

import numpy as np
import jax
import jax.numpy as jnp
from jax import lax
from jax.experimental import pallas as pl
from jax.experimental.pallas import tpu as pltpu

D_MODEL = 1024
CHUNK = 64
N_HEADS = 8
HEAD_DIM = 64
KV_HEADS = 2
GROUPS = N_HEADS // KV_HEADS
IDX_HEADS = 4
IDX_DIM = 64
TOPK_MAX = 256
ROPE_THETA = 500000.0
ATTN_SCALE = HEAD_DIM ** -0.5
IDX_SCALE = (IDX_DIM ** -0.5) * (IDX_HEADS ** -0.5)
NEG = -1e30
HGRN_HEADS = 4
HGRN_DK = 128
HGRN_DV = 128
HGRN_W = HGRN_HEADS * HGRN_DK
SCONV_WIDTH = 512
CONV_W = 3
BRANCH_WIDTH = 512
N_BRANCH = 3
D_FF = 2816
LN_EPS = 1e-5
IN_SIZES = (N_HEADS * HEAD_DIM, KV_HEADS * HEAD_DIM, KV_HEADS * HEAD_DIM,
            IDX_HEADS * IDX_DIM, IDX_DIM, IDX_HEADS,
            HGRN_W, HGRN_W, HGRN_W, HGRN_W,
            SCONV_WIDTH, SCONV_WIDTH, SCONV_WIDTH,
            N_BRANCH * D_MODEL)

LANES = 128
VMEM_LIMIT = 56 * 1024 * 1024
KEY_TILE = 512
QUERY_TILE = 256
SEARCH_CHECKED = 24
SEARCH_STRIDE = 8
VT_ROWS = HEAD_DIM + 16
FFN_COLS = 256
INT_MIN = -2 ** 31

f32 = jnp.float32
bf16 = jnp.bfloat16
i32 = jnp.int32


def _row_tile(s, want):
    return want if s % want == 0 else s


def _const_spec(shape):
    nd = len(shape)
    return pl.BlockSpec(shape, lambda *_: (0,) * nd, pipeline_mode=pl.Buffered(1))


def _params(sem):
    return pltpu.CompilerParams(dimension_semantics=sem, vmem_limit_bytes=VMEM_LIMIT)


def _layer_norm(z, g, b):
    mu = jnp.mean(z, axis=-1, keepdims=True)
    d = z - mu
    var = jnp.mean(d * d, axis=-1, keepdims=True)
    return d * lax.rsqrt(var + LN_EPS) * g + b


def _sigmoid(x):
    return 1.0 / (1.0 + jnp.exp(-x))


Q_PRESCALE = ATTN_SCALE * float(np.log2(np.e))
ROT = HEAD_DIM // 4
WR_COLS = 384
WT_ROWS = 912
WI_ROWS = 8


def _attn_proj_kernel(x_ref, wr_ref, wt_ref, cs_ref, cst_ref, k_ref, v_ref, ki_ref, kb_ref, kib_ref,
                      vt_ref, qt_ref, qit_ref, wit_ref, *, T, TQ):
    xb = x_ref[0].astype(bf16)
    pr = jnp.dot(xb, wr_ref[...], preferred_element_type=f32)
    c, sa, sb = cs_ref[0], cs_ref[1], cs_ref[2]

    def rot(xg):
        return xg * c + pltpu.roll(xg, LANES - ROT // 2, 1) * sa + pltpu.roll(xg, ROT // 2, 1) * sb

    k = rot(pr[:, 0:LANES])
    ki = rot(pr[:, LANES:2 * LANES])[:, :IDX_DIM]
    k_ref[0] = k
    kb_ref[0] = k.astype(bf16)
    ki_ref[0] = ki
    kib_ref[0] = ki.astype(bf16)
    v_ref[0] = pr[:, 2 * LANES:3 * LANES]

    nt = (((1,), (1,)), ((), ()))
    pt = lax.dot_general(wt_ref[...], xb, nt, preferred_element_type=f32)
    ct, st = cst_ref[0], cst_ref[1]

    def rot_t(hb):
        lead = hb[0:ROT] * ct + jnp.concatenate([hb[ROT // 2:ROT], hb[0:ROT // 2]], axis=0) * st
        return jnp.concatenate([lead, hb[ROT:HEAD_DIM]], axis=0)

    nqb = T // TQ
    for h in range(N_HEADS):
        hb = (rot_t(pt[h * HEAD_DIM:(h + 1) * HEAD_DIM]) * Q_PRESCALE).astype(bf16)
        g, i = divmod(h, GROUPS)
        for n in range(nqb):
            qt_ref[0, n, g, :, i * TQ:(i + 1) * TQ] = hb[:, n * TQ:(n + 1) * TQ]
    base = N_HEADS * HEAD_DIM
    for h in range(IDX_HEADS):
        hb = rot_t(pt[base + h * IDX_DIM:base + (h + 1) * IDX_DIM]).astype(bf16)
        for n in range(nqb):
            qit_ref[0, n, :, h * TQ:(h + 1) * TQ] = hb[:, n * TQ:(n + 1) * TQ]
    base += IDX_HEADS * IDX_DIM
    ones_rows = jnp.where(lax.broadcasted_iota(i32, (VT_ROWS - HEAD_DIM, T), 0) == 0, 1.0, 0.0)
    for g in range(KV_HEADS):
        vt_ref[0, 0, g, 0:HEAD_DIM, :] = pt[base + g * HEAD_DIM:base + (g + 1) * HEAD_DIM].astype(bf16)
        vt_ref[0, 0, g, HEAD_DIM:VT_ROWS, :] = ones_rows.astype(bf16)
    base += KV_HEADS * HEAD_DIM
    for n in range(nqb):
        wit_ref[0, n] = pt[base:base + WI_ROWS, n * TQ:(n + 1) * TQ]


def _attn_proj(x, w_row, w_t, cs, cst):
    B, S, _ = x.shape
    T = _row_tile(S, KEY_TILE)
    TQ = _row_tile(S, QUERY_TILE)
    nqb = T // TQ
    R = GROUPS * TQ
    row = lambda w: pl.BlockSpec((1, T, w), lambda b, t: (b, t, 0))
    names = ("k", "v", "ki", "kb", "kib", "vt", "qt", "qit", "wit")
    outs = pl.pallas_call(
        functools.partial(_attn_proj_kernel, T=T, TQ=TQ),
        grid=(B, S // T),
        in_specs=[row(D_MODEL), _const_spec((D_MODEL, WR_COLS)), _const_spec((WT_ROWS, D_MODEL)),
                  pl.BlockSpec((3, T, LANES), lambda b, t: (0, t, 0)),
                  pl.BlockSpec((2, ROT, T), lambda b, t: (0, 0, t))],
        out_specs=[row(128), row(128), row(IDX_DIM), row(128), row(IDX_DIM),
                   pl.BlockSpec((1, 1, KV_HEADS, VT_ROWS, T), lambda b, t: (b, t, 0, 0, 0)),
                   pl.BlockSpec((1, nqb, KV_HEADS, HEAD_DIM, R), lambda b, t: (b, t, 0, 0, 0)),
                   pl.BlockSpec((1, nqb, IDX_DIM, IDX_HEADS * TQ), lambda b, t: (b, t, 0, 0)),
                   pl.BlockSpec((1, nqb, WI_ROWS, TQ), lambda b, t: (b, t, 0, 0))],
        out_shape=[jax.ShapeDtypeStruct((B, S, 128), f32),
                   jax.ShapeDtypeStruct((B, S, 128), f32),
                   jax.ShapeDtypeStruct((B, S, IDX_DIM), f32),
                   jax.ShapeDtypeStruct((B, S, 128), bf16),
                   jax.ShapeDtypeStruct((B, S, IDX_DIM), bf16),
                   jax.ShapeDtypeStruct((B, S // T, KV_HEADS, VT_ROWS, T), bf16),
                   jax.ShapeDtypeStruct((B, S // TQ, KV_HEADS, HEAD_DIM, R), bf16),
                   jax.ShapeDtypeStruct((B, S // TQ, IDX_DIM, IDX_HEADS * TQ), bf16),
                   jax.ShapeDtypeStruct((B, S // TQ, WI_ROWS, TQ), f32)],
        compiler_params=_params(("parallel", "parallel")),
        name="attn_proj",
    )(x, w_row, w_t, cs, cst)
    return dict(zip(names, outs))


def _rotary_tables(pos):
    half = ROT // 2
    inv_freq = ROPE_THETA ** (-jnp.arange(half, dtype=f32) / half)
    ang = pos.astype(f32)[:, None] * inv_freq[None, :]
    cos = jnp.cos(ang)
    sin = jnp.sin(ang)
    S = pos.shape[0]
    one = jnp.ones((S, HEAD_DIM - ROT), f32)
    zero = jnp.zeros((S, HEAD_DIM - ROT), f32)
    zh = jnp.zeros((S, half), f32)
    c = jnp.concatenate([cos, cos, one], axis=1)
    sa = jnp.concatenate([-sin, zh, zero], axis=1)
    sb = jnp.concatenate([zh, sin, zero], axis=1)
    tile2 = lambda a: jnp.concatenate([a, a], axis=1)
    cs = jnp.stack([tile2(c), tile2(sa), tile2(sb)], axis=0)
    cst = jnp.stack([jnp.concatenate([cos, cos], axis=1).T,
                     jnp.concatenate([-sin, sin], axis=1).T], axis=0)
    return cs, cst


def _dsa_kernel(qt_ref, qit_ref, wit_ref, ki_ref, k_ref, vt_ref, tri_ref, o_ref,
                s_ref, qpad_ref, m_ref, acc_ref, *bufs, TQ, TK, P, top):
    lm_refs = (bufs[0:KV_HEADS], bufs[KV_HEADS:2 * KV_HEADS])
    p_refs = (bufs[2 * KV_HEADS:3 * KV_HEADS], bufs[3 * KV_HEADS:4 * KV_HEADS])
    qb = pl.program_id(1)
    q0 = P + qb * TQ
    nk = (q0 + TQ + TK - 1) // TK
    n_full = q0 // TK
    NC = TK // LANES
    R = GROUPS * TQ
    q_lim = ((q0 + lax.broadcasted_iota(i32, (1, TQ), 1)) // CHUNK + 1) * CHUNK
    wit = wit_ref[0, 0] * IDX_SCALE
    zero_head = jnp.zeros((HEAD_DIM, R), bf16)
    qpad_ref[0] = jnp.concatenate([qt_ref[0, 0, 0], zero_head], axis=0)
    qpad_ref[1] = jnp.concatenate([zero_head, qt_ref[0, 0, 1]], axis=0)

    def score_tile(masked, j, carry):
        rel = jnp.maximum(jnp.dot(ki_ref[0, j], qit_ref[0, 0], preferred_element_type=f32), 0.0)
        score = wit[0:1, :] * rel[:, 0:TQ]
        for h in range(1, IDX_HEADS):
            score = score + wit[h:h + 1, :] * rel[:, h * TQ:(h + 1) * TQ]
        if masked:
            k_pos = j * TK + lax.broadcasted_iota(i32, (TK, TQ), 0)
            score = jnp.where(k_pos < q_lim, score, NEG)
        bits = pltpu.bitcast(score, i32)
        s_ref[j] = jnp.where(bits >= 0, bits, bits ^ 0x7FFFFFFF)
        return carry

    lax.fori_loop(0, n_full, functools.partial(score_tile, False), 0)
    lax.fori_loop(n_full, nk, functools.partial(score_tile, True), 0)

    def count(pred):
        def hits(j):
            accs = [None] * 4
            for r in range(TK // 8):
                hit = jnp.where(pred(s_ref[j, r * 8:(r + 1) * 8, :]), 1, 0)
                accs[r % 4] = hit if accs[r % 4] is None else accs[r % 4] + hit
            return (accs[0] + accs[1]) + (accs[2] + accs[3])
        acc = lax.fori_loop(0, nk // 2, lambda i, acc: acc + (hits(2 * i) + hits(2 * i + 1)),
                            jnp.zeros((8, TQ), i32))
        acc = lax.cond(nk % 2 == 1, lambda a: a + hits(nk - 1), lambda a: a, acc)
        return jnp.sum(acc, axis=0, keepdims=True)

    def step(bit, t, cnt):
        cand = t | jnp.left_shift(jnp.int32(1), bit)
        c = count(lambda k: k >= cand)
        ok = c >= top
        return jnp.where(ok, cand, t), jnp.where(ok, c, cnt)

    c0 = count(lambda k: k >= 0)
    t = jnp.where(c0 >= top, 0, INT_MIN).astype(i32)
    cnt = jnp.where(c0 >= top, c0, jnp.int32(2 ** 30))
    t, cnt = lax.fori_loop(0, 31 - SEARCH_CHECKED, lambda i, st: step(30 - i, *st), (t, cnt))

    def checked_steps(st):
        bit, t, cnt, _ = st
        for s in range(SEARCH_STRIDE):
            t, cnt = step(bit - s, t, cnt)
        done = jnp.min(jnp.where(cnt == top, 1, 0))
        return bit - SEARCH_STRIDE, t, cnt, done

    more = lambda st: jnp.logical_and(st[0] >= 0, st[3] == 0)
    _, t, cnt, _ = lax.while_loop(more, checked_steps,
                                  (jnp.int32(SEARCH_CHECKED - 1), t, cnt, jnp.int32(0)))
    neg_key = np.array(NEG, np.float32).view(np.int32) ^ 0x7FFFFFFF
    any_excess = jnp.max(jnp.where(cnt > top, 1, 0)) > 0

    @pl.when(jnp.logical_not(any_excess))
    def _():
        thr = jnp.where(t == neg_key, t + 1, t)

        def fast(j, carry):
            s_ref[j] = pltpu.bitcast(jnp.where(s_ref[j] >= thr, 0.0, NEG), i32)
            return carry
        lax.fori_loop(0, nk, fast, 0)

    @pl.when(any_excess)
    def _():
        n_gt = count(lambda k: k > t)
        need = jnp.where(t == neg_key, 0, top - n_gt).astype(f32)

        def ranked(j, taken):
            keys = s_ref[j]
            for c in range(NC):
                kc = keys[c * LANES:(c + 1) * LANES, :]
                eq = jnp.where(kc == t, 1.0, 0.0)
                rank = taken + jnp.dot(tri_ref[...], eq.astype(bf16), preferred_element_type=f32)
                pick = jnp.where(kc > t, 1.0, jnp.where(rank <= need, eq, 0.0))
                s_ref[j, c * LANES:(c + 1) * LANES, :] = pltpu.bitcast(
                    jnp.where(pick > 0.5, 0.0, NEG), i32)
                taken = taken + jnp.sum(eq, axis=0, keepdims=True)
            return taken
        lax.fori_loop(0, nk, ranked, jnp.zeros((1, TQ), f32))

    m_ref[...] = jnp.full(m_ref.shape, NEG, f32)
    acc_ref[...] = jnp.zeros(acc_ref.shape, f32)

    def tile_step(nxt, cur):
        if cur is not None:
            j, slot, tile_max = cur
            m_old = [m_ref[g] for g in range(KV_HEADS)]
            m_new = [jnp.maximum(m_old[g], tile_max[g]) for g in range(KV_HEADS)]
            for g in range(KV_HEADS):
                m_ref[g] = m_new[g]
        if nxt is not None:
            jn, slot_n = nxt
            jc = jnp.minimum(jn, nk - 1)
        out = []
        for g in range(KV_HEADS):
            cmax = jnp.full((8, R), NEG, f32)
            for c in range(NC):
                rows = slice(c * LANES, (c + 1) * LANES)
                if nxt is not None:
                    bias = pltpu.bitcast(s_ref[jc, rows, :], f32)
                    lm = (jnp.dot(k_ref[0, jc, rows, :], qpad_ref[g], preferred_element_type=f32)
                          + jnp.concatenate([bias] * GROUPS, axis=1))
                    lm_refs[slot_n][g][rows, :] = lm
                    cmax = jnp.maximum(cmax, jnp.max(lm.reshape(LANES // 8, 8, R), axis=0))
                if cur is not None:
                    p_refs[slot][g][rows, :] = jnp.exp2(lm_refs[slot][g][rows, :] - m_new[g]).astype(bf16)
            out.append(jnp.max(cmax, axis=0, keepdims=True))
        if cur is not None:
            for g in range(KV_HEADS):
                acc_ref[g] = (jnp.exp2(m_old[g] - m_new[g]) * acc_ref[g]
                              + jnp.dot(vt_ref[0, j, g], p_refs[slot][g][...], preferred_element_type=f32))
        return tuple(out)

    def tile_pair(i, tile_max):
        nxt_max = tile_step((2 * i + 1, 1), (2 * i, 0, tile_max))
        return tile_step((2 * i + 2, 0), (2 * i + 1, 1, nxt_max))

    last_max = lax.fori_loop(0, nk // 2, tile_pair, tile_step((0, 0), None))

    @pl.when(nk % 2 == 1)
    def _():
        tile_step(None, (nk - 1, 0, last_max))

    for g in range(KV_HEADS):
        o_t = acc_ref[g, 0:HEAD_DIM, :] / acc_ref[g, HEAD_DIM:HEAD_DIM + 1, :]
        for i in range(0, GROUPS, 2):
            pair = jnp.concatenate([o_t[:, i * TQ:(i + 1) * TQ], o_t[:, (i + 1) * TQ:(i + 2) * TQ]], axis=0)
            h = g * GROUPS + i
            o_ref[0, :, h * HEAD_DIM:(h + 2) * HEAD_DIM] = pair.T.astype(bf16)


def _dsa(qt, qit, wit, kb, kib, vt, P, top):
    B, nq = qt.shape[:2]
    R = qt.shape[-1]
    TQ = R // GROUPS
    NT, TK = kb.shape[1:3]
    assert top <= TK and TQ % CHUNK == 0
    tri = jnp.asarray(np.tril(np.ones((LANES, LANES), np.float32)), bf16)
    return pl.pallas_call(
        functools.partial(_dsa_kernel, TQ=TQ, TK=TK, P=P, top=top),
        grid=(B, nq),
        in_specs=[pl.BlockSpec((1, 1, KV_HEADS, HEAD_DIM, R), lambda b, i: (b, i, 0, 0, 0)),
                  pl.BlockSpec((1, 1, IDX_DIM, IDX_HEADS * TQ), lambda b, i: (b, i, 0, 0)),
                  pl.BlockSpec((1, 1, WI_ROWS, TQ), lambda b, i: (b, i, 0, 0)),
                  pl.BlockSpec((1, NT, TK, IDX_DIM), lambda b, i: (b, 0, 0, 0)),
                  pl.BlockSpec((1, NT, TK, KV_HEADS * HEAD_DIM), lambda b, i: (b, 0, 0, 0)),
                  pl.BlockSpec((1, NT, KV_HEADS, VT_ROWS, TK), lambda b, i: (b, 0, 0, 0, 0)),
                  _const_spec((LANES, LANES))],
        out_specs=pl.BlockSpec((1, TQ, N_HEADS * HEAD_DIM), lambda b, i: (b, i, 0)),
        out_shape=jax.ShapeDtypeStruct((B, nq * TQ, N_HEADS * HEAD_DIM), bf16),
        scratch_shapes=[pltpu.VMEM((NT, TK, TQ), i32),
                        pltpu.VMEM((KV_HEADS, KV_HEADS * HEAD_DIM, R), bf16),
                        pltpu.VMEM((KV_HEADS, 1, R), f32),
                        pltpu.VMEM((KV_HEADS, VT_ROWS, R), f32)]
        + [pltpu.VMEM((TK, R), f32)] * (2 * KV_HEADS)
        + [pltpu.VMEM((TK, R), bf16)] * (2 * KV_HEADS),
        compiler_params=_params(("parallel", "arbitrary")),
        name="dsa",
    )(qt, qit, wit, kib, kb, vt, tri)


def _hgrn_mats(T):
    r = np.arange(T)[:, None]
    c = np.arange(T)[None, :]
    mats = [c <= r, c > r]
    b = T // 2
    while b >= 1:
        blk_r = r // (2 * b)
        mid = blk_r * 2 * b + b
        same = blk_r == c // (2 * b)
        late = r >= mid
        mats.append(same & ((late & (c >= mid) & (c <= r)) | (~late & (c > r) & (c < mid))))
        b //= 2
    return np.concatenate(mats, axis=0).astype(np.float32)


def _hgrn_kernel(x_ref, w_ref, lb_ref, g_ref, mats_ref, s0_ref, y_ref, st_out_ref, st_ref, *, T):
    t_idx = pl.program_id(1)

    @pl.when(t_idx == 0)
    def _():
        st_ref[...] = s0_ref[0]

    xb = x_ref[0].astype(bf16)
    proj = jnp.dot(xb, w_ref[...], preferred_element_type=f32)
    hq = proj[:, 0:HGRN_W]
    z = proj[:, HGRN_W:2 * HGRN_W]
    hv = proj[:, 2 * HGRN_W:3 * HGRN_W]
    hg = proj[:, 3 * HGRN_W:4 * HGRN_W]
    lb = lb_ref[...]
    logf = (jnp.minimum(z, 0.0) - jnp.log1p(jnp.exp(-jnp.abs(z)))) + jnp.log1p(lb * jnp.exp(-z))
    kk = (1.0 - lb) * _sigmoid(-z)
    qq = hq * _sigmoid(hq)
    gate = hg * _sigmoid(hg)

    hi = logf.astype(bf16)
    r1 = logf - hi.astype(f32)
    mid = r1.astype(bf16)
    lo = (r1 - mid.astype(f32)).astype(bf16)
    mats = mats_ref[...]
    e_all = (jnp.dot(mats, hi, preferred_element_type=f32)
             + jnp.dot(mats, mid, preferred_element_type=f32)
             + jnp.dot(mats, lo, preferred_element_type=f32))

    row = lax.broadcasted_iota(i32, (T, 1), 0)
    col = lax.broadcasted_iota(i32, (1, T), 1)
    nt = (((1,), (1,)), ((), ()))
    tn = (((0,), (0,)), ((), ()))
    for h in range(HGRN_HEADS):
        sl = slice(h * HGRN_DK, (h + 1) * HGRN_DK)
        q_h, k_h, v_h = qq[:, sl], kk[:, sl], hv[:, sl]
        vb = v_h.astype(bf16)
        cum = e_all[0:T, sl]
        suf = e_all[T:2 * T, sl]
        st = st_ref[h]
        o = lax.dot_general((q_h * jnp.exp(cum)).astype(bf16), st.astype(bf16), nt,
                            preferred_element_type=f32)
        o = o + jnp.sum(q_h * k_h, axis=-1, keepdims=True) * v_h
        scores = jnp.zeros((T, T), f32)
        b = T // 2
        lvl = 2
        while b >= 1:
            xdec = jnp.exp(e_all[lvl * T:(lvl + 1) * T, sl])
            late = ((row // b) % 2) == 1
            a_m = jnp.where(late, q_h * xdec, 0.0).astype(bf16)
            b_m = jnp.where(late, 0.0, k_h * xdec).astype(bf16)
            sc = lax.dot_general(a_m, b_m, nt, preferred_element_type=f32)
            if 2 * b < T:
                sc = jnp.where((row // (2 * b)) == (col // (2 * b)), sc, 0.0)
            scores = scores + sc
            b //= 2
            lvl += 1
        o = o + jnp.dot(scores.astype(bf16), vb, preferred_element_type=f32)
        kd = (k_h * jnp.exp(suf)).astype(bf16)
        st_ref[h] = (st * jnp.exp(cum[T - 1:T, :])
                     + lax.dot_general(vb, kd, tn, preferred_element_type=f32))
        o = o * lax.rsqrt(jnp.mean(o * o, axis=-1, keepdims=True) + LN_EPS) * g_ref[...]
        y_ref[0, :, sl] = (o * gate[:, sl]).astype(bf16)

    st_out_ref[0] = st_ref[...]


def _hgrn(x, w_h, lb, norm_g, s0):
    B, S, _ = x.shape
    T = _row_tile(S, 2 * CHUNK)
    mats = jnp.asarray(_hgrn_mats(T), bf16)
    nm = mats.shape[0]
    st0 = jnp.swapaxes(s0.astype(f32), 2, 3)
    y, st = pl.pallas_call(
        functools.partial(_hgrn_kernel, T=T),
        grid=(B, S // T),
        in_specs=[pl.BlockSpec((1, T, D_MODEL), lambda b, t: (b, t, 0)),
                  _const_spec((D_MODEL, 4 * HGRN_W)),
                  _const_spec((1, HGRN_W)),
                  _const_spec((1, HGRN_DV)),
                  _const_spec((nm, T)),
                  pl.BlockSpec((1, HGRN_HEADS, HGRN_DV, HGRN_DK), lambda b, t: (b, 0, 0, 0))],
        out_specs=[pl.BlockSpec((1, T, HGRN_W), lambda b, t: (b, t, 0)),
                   pl.BlockSpec((1, HGRN_HEADS, HGRN_DV, HGRN_DK), lambda b, t: (b, 0, 0, 0))],
        out_shape=[jax.ShapeDtypeStruct((B, S, HGRN_W), bf16),
                   jax.ShapeDtypeStruct((B, HGRN_HEADS, HGRN_DV, HGRN_DK), f32)],
        scratch_shapes=[pltpu.VMEM((HGRN_HEADS, HGRN_DV, HGRN_DK), f32)],
        compiler_params=_params(("parallel", "arbitrary")),
        name="hgrn",
    )(x, w_h, lb, norm_g, mats, st0)
    return y, jnp.swapaxes(st, 2, 3)


HIST = CONV_W - 1
PADROWS = 8


def _conv_rows(buf_ref, cols, u, w, bias, T):
    buf_ref[PADROWS:PADROWS + T, cols] = u
    y = bias
    for j in range(CONV_W):
        y = y + w[j:j + 1, :] * buf_ref[PADROWS - HIST + j:PADROWS - HIST + j + T, cols]
    buf_ref[PADROWS - HIST:PADROWS, cols] = u[T - HIST:T, :]
    return y


def _merge_kernel(x_ref, ya_ref, yb_ref, hist_ref, wc_ref, wg_ref, wbr_ref, wout_ref, cw_ref, cb_ref,
                  g_ref, b_ref, o_ref, hist_out_ref, buf_ref, *, T, alpha):
    @pl.when(pl.program_id(1) == 0)
    def _():
        buf_ref[PADROWS - HIST:PADROWS, :] = hist_ref[0]

    x = x_ref[0]
    xb = x.astype(bf16)
    c = jnp.dot(xb, wc_ref[...], preferred_element_type=f32)
    W = SCONV_WIDTH
    u = c[:, W:2 * W] * c[:, 2 * W:3 * W]
    u_conv = _conv_rows(buf_ref, slice(0, W), u, cw_ref[...], cb_ref[...], T)
    yc = (c[:, 0:W] * u_conv).astype(bf16)
    hist_out_ref[0] = buf_ref[PADROWS - HIST:PADROWS, :]

    merged = None
    for i, yb in enumerate((ya_ref[0], yb_ref[0], yc)):
        gate = _sigmoid(jnp.dot(xb, wg_ref[:, i * D_MODEL:(i + 1) * D_MODEL], preferred_element_type=f32))
        term = gate * jnp.dot(yb, wbr_ref[i], preferred_element_type=f32)
        merged = term if merged is None else merged + term
    out = jnp.dot(merged.astype(bf16), wout_ref[...], preferred_element_type=f32)
    o_ref[0] = _layer_norm(alpha * x + out, g_ref[...], b_ref[...])


def _merge(x, ya, yb, hist, wc, wg, wbr, wout, cw, cb, g, b, alpha):
    B, S, _ = x.shape
    T = _row_tile(S, 256)
    row = lambda w: pl.BlockSpec((1, T, w), lambda bb, t: (bb, t, 0))
    hspec = pl.BlockSpec((1, HIST, SCONV_WIDTH), lambda bb, t: (bb, 0, 0))
    return pl.pallas_call(
        functools.partial(_merge_kernel, T=T, alpha=alpha),
        grid=(B, S // T),
        in_specs=[row(D_MODEL), row(BRANCH_WIDTH), row(BRANCH_WIDTH), hspec,
                  _const_spec((D_MODEL, 3 * SCONV_WIDTH)), _const_spec((D_MODEL, N_BRANCH * D_MODEL)),
                  _const_spec((N_BRANCH, BRANCH_WIDTH, D_MODEL)), _const_spec((D_MODEL, D_MODEL)),
                  _const_spec((CONV_W, SCONV_WIDTH)), _const_spec((1, SCONV_WIDTH)),
                  _const_spec((1, D_MODEL)), _const_spec((1, D_MODEL))],
        out_specs=[row(D_MODEL), hspec],
        out_shape=[jax.ShapeDtypeStruct((B, S, D_MODEL), f32),
                   jax.ShapeDtypeStruct((B, HIST, SCONV_WIDTH), f32)],
        scratch_shapes=[pltpu.VMEM((PADROWS + T, SCONV_WIDTH), f32)],
        compiler_params=_params(("parallel", "arbitrary")),
        name="merge",
    )(x, ya, yb, hist, wc, wg, wbr, wout, cw, cb, g, b)


def _ffn_kernel(x_ref, hist_ref, wup_ref, cw_ref, cb_ref, wdn_ref, g_ref, b_ref, o_ref, hist_out_ref,
                buf_ref, *, T, alpha):
    @pl.when(pl.program_id(1) == 0)
    def _():
        buf_ref[PADROWS - HIST:PADROWS, :] = hist_ref[0]

    x = x_ref[0]
    xb = x.astype(bf16)
    acc = jnp.zeros((T, D_MODEL), f32)
    for cidx in range(D_FF // FFN_COLS):
        halves = []
        for off in (0, D_FF):
            cols = slice(off + cidx * FFN_COLS, off + (cidx + 1) * FFN_COLS)
            h = jnp.dot(xb, wup_ref[:, cols], preferred_element_type=f32)
            halves.append(_conv_rows(buf_ref, cols, h, cw_ref[:, cols], cb_ref[:, cols], T))
        a_g, b_v = halves
        act = (a_g * _sigmoid(a_g) * b_v).astype(bf16)
        acc = acc + jnp.dot(act, wdn_ref[cidx * FFN_COLS:(cidx + 1) * FFN_COLS, :],
                            preferred_element_type=f32)
    hist_out_ref[0] = buf_ref[PADROWS - HIST:PADROWS, :]
    o_ref[0] = _layer_norm(alpha * x + acc, g_ref[...], b_ref[...])


def _ffn(x, hist, wup, cw, cb, wdn, g, b, alpha):
    B, S, _ = x.shape
    T = _row_tile(S, 256)
    row = pl.BlockSpec((1, T, D_MODEL), lambda bb, t: (bb, t, 0))
    hspec = pl.BlockSpec((1, HIST, 2 * D_FF), lambda bb, t: (bb, 0, 0))
    return pl.pallas_call(
        functools.partial(_ffn_kernel, T=T, alpha=alpha),
        grid=(B, S // T),
        in_specs=[row, hspec, _const_spec((D_MODEL, 2 * D_FF)), _const_spec((CONV_W, 2 * D_FF)),
                  _const_spec((1, 2 * D_FF)), _const_spec((D_FF, D_MODEL)),
                  _const_spec((1, D_MODEL)), _const_spec((1, D_MODEL))],
        out_specs=[row, hspec],
        out_shape=[jax.ShapeDtypeStruct((B, S, D_MODEL), f32),
                   jax.ShapeDtypeStruct((B, HIST, 2 * D_FF), f32)],
        scratch_shapes=[pltpu.VMEM((PADROWS + T, 2 * D_FF), f32)],
        compiler_params=_params(("parallel", "arbitrary")),
        name="ffn",
    )(x, hist, wup, cw, cb, wdn, g, b)


def _cast_kernel(x_ref, o_ref):
    o_ref[...] = x_ref[...].astype(o_ref.dtype)


def _to_bf16(w):
    D, Rw, C = w.shape
    T = _row_tile(Rw, LANES)
    spec = pl.BlockSpec((1, T, C), lambda d, t: (d, t, 0))
    return pl.pallas_call(_cast_kernel, grid=(D, Rw // T), in_specs=[spec], out_specs=spec,
                          out_shape=jax.ShapeDtypeStruct(w.shape, bf16),
                          compiler_params=_params(("parallel", "parallel")), name="to_bf16")(w)


def _layer_weights(l, depth, w_in, hgrn_lb_logits, hgrn_norm_g, sconv_w, sconv_b, w_branch, w_out,
                   ln1_g, ln1_b, w_up, ffn_conv_w, ffn_conv_b, w_down, ln2_g, ln2_b):
    offs = np.concatenate([[0], np.cumsum(IN_SIZES)]).tolist()
    sec = lambda i, j=None: w_in[l][:, offs[i]:offs[(i if j is None else j) + 1]]
    a_q, a_k, a_v, i_q, i_k, i_w = (sec(i) for i in range(6))
    zpad = lambda n: jnp.zeros((D_MODEL, n), w_in.dtype)
    w_row = jnp.concatenate([a_k, i_k, zpad(LANES - IDX_DIM), a_v], axis=1).astype(bf16)
    w_t = jnp.concatenate([a_q, i_q, a_v, i_w, zpad(WT_ROWS - 900)], axis=1).T.astype(bf16)
    lbp = jax.nn.softmax(hgrn_lb_logits.astype(f32), axis=0)
    lb = (jnp.cumsum(lbp, axis=0) - lbp[0])[l].reshape(1, HGRN_W)
    row = lambda a: a.reshape(1, -1).astype(f32)
    return dict(
        w_row=w_row, w_t=w_t, w_h=sec(6, 9).astype(bf16), lb=lb, norm_g=row(hgrn_norm_g[l]),
        w_c=sec(10, 12).astype(bf16), w_g=sec(13).astype(bf16),
        w_br=w_branch[l].astype(bf16), w_out=w_out[l].astype(bf16),
        sconv_w=sconv_w[l].astype(f32), sconv_b=row(sconv_b[l]),
        ln1_g=row(ln1_g[l]), ln1_b=row(ln1_b[l]),
        w_up=w_up[l].astype(bf16), ffn_w=ffn_conv_w[l].astype(f32), ffn_b=row(ffn_conv_b[l]),
        w_down=w_down[l].astype(bf16), ln2_g=row(ln2_g[l]), ln2_b=row(ln2_b[l]))


def _key_tiles(a, TK):
    B, L, C = a.shape
    NT = -(-L // TK)
    a = jnp.pad(a, ((0, 0), (0, NT * TK - L), (0, 0)))
    return a.reshape(B, NT, TK, C)


def _trunk_layer(x, tables, P, k_past, v_past, ki_past, s0, sc_hist, ffn_hist, w, alpha):
    B, S, _ = x.shape
    TK = KEY_TILE
    a = _attn_proj(x, w["w_row"], w["w_t"], *tables)
    k, v, ki = a["k"], a["v"], a["ki"]
    L = P + S
    if k_past is None and S % TK == 0:
        kb = a["kb"].reshape(B, S // TK, TK, -1)
        kib = a["kib"].reshape(B, S // TK, TK, -1)
        vt = a["vt"]
    else:
        kb, kib, v_all = a["kb"], a["kib"], v.astype(bf16)
        if k_past is not None:
            kb = jnp.concatenate([k_past.reshape(B, P, -1).astype(bf16), kb], axis=1)
            kib = jnp.concatenate([ki_past.astype(bf16), kib], axis=1)
            v_all = jnp.concatenate([v_past.reshape(B, P, -1).astype(bf16), v_all], axis=1)
        kb, kib = _key_tiles(kb, TK), _key_tiles(kib, TK)
        vt = _key_tiles(v_all, TK).reshape(B, -1, TK, KV_HEADS, HEAD_DIM).transpose(0, 1, 3, 4, 2)
        ones = jnp.zeros(vt.shape[:3] + (VT_ROWS - HEAD_DIM, TK), bf16).at[:, :, :, 0, :].set(1.0)
        vt = jnp.concatenate([vt, ones], axis=3)
    y_a = _dsa(a["qt"], a["qit"], a["wit"], kb, kib, vt, P, min(TOPK_MAX, L // 4))
    y_b, s_new = _hgrn(x, w["w_h"], w["lb"], w["norm_g"], s0)
    x1, sc_new = _merge(x, y_a, y_b, sc_hist, w["w_c"], w["w_g"], w["w_br"], w["w_out"],
                        w["sconv_w"], w["sconv_b"], w["ln1_g"], w["ln1_b"], alpha)
    x2, ffn_new = _ffn(x1, ffn_hist, w["w_up"], w["ffn_w"], w["ffn_b"], w["w_down"],
                       w["ln2_g"], w["ln2_b"], alpha)
    new = (k.reshape(B, S, KV_HEADS, HEAD_DIM), v.reshape(B, S, KV_HEADS, HEAD_DIM), ki,
           s_new, sc_new, ffn_new)
    return x2, new


def kernel(x_prompt, x_sample, cache_attn_k, cache_attn_v, cache_idx_k, state_hgrn, state_sconv,
           state_ffn_conv, w_in, hgrn_lb_logits, hgrn_norm_g, sconv_w, sconv_b, w_branch, w_out,
           ln1_g, ln1_b, w_up, ffn_conv_w, ffn_conv_b, w_down, ln2_g, ln2_b):
    depth = w_in.shape[0]
    alpha = (2 * depth) ** 0.25
    B, S, _ = x_prompt.shape
    DB, DS, _ = x_sample.shape
    P = cache_attn_k.shape[2]
    cs_p = _rotary_tables(jnp.arange(S))
    cs_s = _rotary_tables(P + jnp.arange(DS))
    xp, xs = x_prompt, x_sample
    st_p = [[] for _ in range(6)]
    st_s = [[] for _ in range(6)]
    w_in_b = _to_bf16(w_in)
    for l in range(depth):
        w = _layer_weights(l, depth, w_in_b, hgrn_lb_logits, hgrn_norm_g, sconv_w, sconv_b, w_branch,
                           w_out, ln1_g, ln1_b, w_up, ffn_conv_w, ffn_conv_b, w_down, ln2_g, ln2_b)
        xp, new_p = _trunk_layer(
            xp, cs_p, 0, None, None, None,
            jnp.zeros((B, HGRN_HEADS, HGRN_DK, HGRN_DV), f32),
            jnp.zeros((B, HIST, SCONV_WIDTH), f32),
            jnp.zeros((B, HIST, 2 * D_FF), f32), w, alpha)
        xs, new_s = _trunk_layer(
            xs, cs_s, P, cache_attn_k[l], cache_attn_v[l], cache_idx_k[l], state_hgrn[l],
            state_sconv[l], state_ffn_conv[l], w, alpha)
        for j in range(6):
            st_p[j].append(new_p[j])
            st_s[j].append(new_s[j])
    outs_p = [jnp.stack(a, axis=0) for a in st_p]
    outs_s = [jnp.stack(a, axis=0) for a in st_s]
    return (xp, xs, *outs_p, *outs_s)
```

```python
import functools

import numpy as np
import jax
import jax.numpy as jnp
from jax import lax
from jax.experimental import pallas as pl
from jax.experimental.pallas import tpu as pltpu

D_MODEL = 1024
CHUNK = 64
N_HEADS = 8
HEAD_DIM = 64
KV_HEADS = 2
GROUPS = N_HEADS // KV_HEADS
IDX_HEADS = 4
IDX_DIM = 64
TOPK_MAX = 256
ROPE_THETA = 500000.0
ATTN_SCALE = HEAD_DIM ** -0.5
IDX_SCALE = (IDX_DIM ** -0.5) * (IDX_HEADS ** -0.5)
NEG = -1e30
HGRN_HEADS = 4
HGRN_DK = 128
HGRN_DV = 128
HGRN_W = HGRN_HEADS * HGRN_DK
SCONV_WIDTH = 512
CONV_W = 3
BRANCH_WIDTH = 512
N_BRANCH = 3
D_FF = 2816
LN_EPS = 1e-5
IN_SIZES = (N_HEADS * HEAD_DIM, KV_HEADS * HEAD_DIM, KV_HEADS * HEAD_DIM,
            IDX_HEADS * IDX_DIM, IDX_DIM, IDX_HEADS,
            HGRN_W, HGRN_W, HGRN_W, HGRN_W,
            SCONV_WIDTH, SCONV_WIDTH, SCONV_WIDTH,
            N_BRANCH * D_MODEL)

LANES = 128
VMEM_LIMIT = 56 * 1024 * 1024
KEY_TILE = 512
QUERY_TILE = 256
SEARCH_CHECKED = 24
SEARCH_STRIDE = 8
VT_ROWS = HEAD_DIM + 16
FFN_COLS = 256
INT_MIN = -2 ** 31

f32 = jnp.float32
bf16 = jnp.bfloat16
i32 = jnp.int32


def _row_tile(s, want):
    return want if s % want == 0 else s


def _const_spec(shape):
    nd = len(shape)
    return pl.BlockSpec(shape, lambda *_: (0,) * nd, pipeline_mode=pl.Buffered(1))


def _params(sem):
    return pltpu.CompilerParams(dimension_semantics=sem, vmem_limit_bytes=VMEM_LIMIT)


def _layer_norm(z, g, b):
    mu = jnp.mean(z, axis=-1, keepdims=True)
    d = z - mu
    var = jnp.mean(d * d, axis=-1, keepdims=True)
    return d * lax.rsqrt(var + LN_EPS) * g + b


def _sigmoid(x):
    return 1.0 / (1.0 + jnp.exp(-x))


Q_PRESCALE = ATTN_SCALE * float(np.log2(np.e))
ROT = HEAD_DIM // 4
WR_COLS = 384
WT_ROWS = 912
WI_ROWS = 8


def _attn_proj_kernel(x_ref, wr_ref, wt_ref, cs_ref, cst_ref, k_ref, v_ref, ki_ref, kb_ref, kib_ref,
                      vt_ref, qt_ref, qit_ref, wit_ref, *, T, TQ):
    xb = x_ref[0].astype(bf16)
    pr = jnp.dot(xb, wr_ref[...], preferred_element_type=f32)
    c, sa, sb = cs_ref[0], cs_ref[1], cs_ref[2]

    def rot(xg):
        return xg * c + pltpu.roll(xg, LANES - ROT // 2, 1) * sa + pltpu.roll(xg, ROT // 2, 1) * sb

    k = rot(pr[:, 0:LANES])
    ki = rot(pr[:, LANES:2 * LANES])[:, :IDX_DIM]
    k_ref[0] = k
    kb_ref[0] = k.astype(bf16)
    ki_ref[0] = ki
    kib_ref[0] = ki.astype(bf16)
    v_ref[0] = pr[:, 2 * LANES:3 * LANES]

    nt = (((1,), (1,)), ((), ()))
    pt = lax.dot_general(wt_ref[...], xb, nt, preferred_element_type=f32)
    ct, st = cst_ref[0], cst_ref[1]

    def rot_t(hb):
        lead = hb[0:ROT] * ct + jnp.concatenate([hb[ROT // 2:ROT], hb[0:ROT // 2]], axis=0) * st
        return jnp.concatenate([lead, hb[ROT:HEAD_DIM]], axis=0)

    nqb = T // TQ
    for h in range(N_HEADS):
        hb = (rot_t(pt[h * HEAD_DIM:(h + 1) * HEAD_DIM]) * Q_PRESCALE).astype(bf16)
        g, i = divmod(h, GROUPS)
        for n in range(nqb):
            qt_ref[0, n, g, :, i * TQ:(i + 1) * TQ] = hb[:, n * TQ:(n + 1) * TQ]
    base = N_HEADS * HEAD_DIM
    for h in range(IDX_HEADS):
        hb = rot_t(pt[base + h * IDX_DIM:base + (h + 1) * IDX_DIM]).astype(bf16)
        for n in range(nqb):
            qit_ref[0, n, :, h * TQ:(h + 1) * TQ] = hb[:, n * TQ:(n + 1) * TQ]
    base += IDX_HEADS * IDX_DIM
    ones_rows = jnp.where(lax.broadcasted_iota(i32, (VT_ROWS - HEAD_DIM, T), 0) == 0, 1.0, 0.0)
    for g in range(KV_HEADS):
        vt_ref[0, 0, g, 0:HEAD_DIM, :] = pt[base + g * HEAD_DIM:base + (g + 1) * HEAD_DIM].astype(bf16)
        vt_ref[0, 0, g, HEAD_DIM:VT_ROWS, :] = ones_rows.astype(bf16)
    base += KV_HEADS * HEAD_DIM
    for n in range(nqb):
        wit_ref[0, n] = pt[base:base + WI_ROWS, n * TQ:(n + 1) * TQ]


def _attn_proj(x, w_row, w_t, cs, cst):
    B, S, _ = x.shape
    T = _row_tile(S, KEY_TILE)
    TQ = _row_tile(S, QUERY_TILE)
    nqb = T // TQ
    R = GROUPS * TQ
    row = lambda w: pl.BlockSpec((1, T, w), lambda b, t: (b, t, 0))
    names = ("k", "v", "ki", "kb", "kib", "vt", "qt", "qit", "wit")
    outs = pl.pallas_call(
        functools.partial(_attn_proj_kernel, T=T, TQ=TQ),
        grid=(B, S // T),
        in_specs=[row(D_MODEL), _const_spec((D_MODEL, WR_COLS)), _const_spec((WT_ROWS, D_MODEL)),
                  pl.BlockSpec((3, T, LANES), lambda b, t: (0, t, 0)),
                  pl.BlockSpec((2, ROT, T), lambda b, t: (0, 0, t))],
        out_specs=[row(128), row(128), row(IDX_DIM), row(128), row(IDX_DIM),
                   pl.BlockSpec((1, 1, KV_HEADS, VT_ROWS, T), lambda b, t: (b, t, 0, 0, 0)),
                   pl.BlockSpec((1, nqb, KV_HEADS, HEAD_DIM, R), lambda b, t: (b, t, 0, 0, 0)),
                   pl.BlockSpec((1, nqb, IDX_DIM, IDX_HEADS * TQ), lambda b, t: (b, t, 0, 0)),
                   pl.BlockSpec((1, nqb, WI_ROWS, TQ), lambda b, t: (b, t, 0, 0))],
        out_shape=[jax.ShapeDtypeStruct((B, S, 128), f32),
                   jax.ShapeDtypeStruct((B, S, 128), f32),
                   jax.ShapeDtypeStruct((B, S, IDX_DIM), f32),
                   jax.ShapeDtypeStruct((B, S, 128), bf16),
                   jax.ShapeDtypeStruct((B, S, IDX_DIM), bf16),
                   jax.ShapeDtypeStruct((B, S // T, KV_HEADS, VT_ROWS, T), bf16),
                   jax.ShapeDtypeStruct((B, S // TQ, KV_HEADS, HEAD_DIM, R), bf16),
                   jax.ShapeDtypeStruct((B, S // TQ, IDX_DIM, IDX_HEADS * TQ), bf16),
                   jax.ShapeDtypeStruct((B, S // TQ, WI_ROWS, TQ), f32)],
        compiler_params=_params(("parallel", "parallel")),
        name="attn_proj",
    )(x, w_row, w_t, cs, cst)
    return dict(zip(names, outs))


def _rotary_tables(pos):
    half = ROT // 2
    inv_freq = ROPE_THETA ** (-jnp.arange(half, dtype=f32) / half)
    ang = pos.astype(f32)[:, None] * inv_freq[None, :]
    cos = jnp.cos(ang)
    sin = jnp.sin(ang)
    S = pos.shape[0]
    one = jnp.ones((S, HEAD_DIM - ROT), f32)
    zero = jnp.zeros((S, HEAD_DIM - ROT), f32)
    zh = jnp.zeros((S, half), f32)
    c = jnp.concatenate([cos, cos, one], axis=1)
    sa = jnp.concatenate([-sin, zh, zero], axis=1)
    sb = jnp.concatenate([zh, sin, zero], axis=1)
    tile2 = lambda a: jnp.concatenate([a, a], axis=1)
    cs = jnp.stack([tile2(c), tile2(sa), tile2(sb)], axis=0)
    cst = jnp.stack([jnp.concatenate([cos, cos], axis=1).T,
                     jnp.concatenate([-sin, sin], axis=1).T], axis=0)
    return cs, cst


def _dsa_kernel(qt_ref, qit_ref, wit_ref, ki_ref, k_ref, vt_ref, tri_ref, o_ref,
                s_ref, qpad_ref, m_ref, acc_ref, *bufs, TQ, TK, P, top):
    lm_refs = (bufs[0:KV_HEADS], bufs[KV_HEADS:2 * KV_HEADS])
    p_refs = (bufs[2 * KV_HEADS:3 * KV_HEADS], bufs[3 * KV_HEADS:4 * KV_HEADS])
    qb = pl.program_id(1)
    q0 = P + qb * TQ
    nk = (q0 + TQ + TK - 1) // TK
    n_full = q0 // TK
    NC = TK // LANES
    R = GROUPS * TQ
    q_lim = ((q0 + lax.broadcasted_iota(i32, (1, TQ), 1)) // CHUNK + 1) * CHUNK
    wit = wit_ref[0, 0] * IDX_SCALE
    zero_head = jnp.zeros((HEAD_DIM, R), bf16)
    qpad_ref[0] = jnp.concatenate([qt_ref[0, 0, 0], zero_head], axis=0)
    qpad_ref[1] = jnp.concatenate([zero_head, qt_ref[0, 0, 1]], axis=0)

    def score_tile(masked, j, carry):
        for c in range(NC):
            rows = slice(c * LANES, (c + 1) * LANES)
            rel = jnp.maximum(jnp.dot(ki_ref[0, j, rows, :], qit_ref[0, 0], preferred_element_type=f32), 0.0)
            score = wit[0:1, :] * rel[:, 0:TQ]
            for h in range(1, IDX_HEADS):
                score = score + wit[h:h + 1, :] * rel[:, h * TQ:(h + 1) * TQ]
            if masked:
                k_pos = j * TK + c * LANES + lax.broadcasted_iota(i32, (LANES, TQ), 0)
                score = jnp.where(k_pos < q_lim, score, NEG)
            bits = pltpu.bitcast(score, i32)
            s_ref[j, rows, :] = jnp.where(bits >= 0, bits, bits ^ 0x7FFFFFFF)
        return carry

    lax.fori_loop(0, n_full, functools.partial(score_tile, False), 0)
    lax.fori_loop(n_full, nk, functools.partial(score_tile, True), 0)

    def count(pred):
        def hits(j):
            accs = [None] * 4
            for r in range(TK // 8):
                hit = jnp.where(pred(s_ref[j, r * 8:(r + 1) * 8, :]), 1, 0)
                accs[r % 4] = hit if accs[r % 4] is None else accs[r % 4] + hit
            return (accs[0] + accs[1]) + (accs[2] + accs[3])
        acc = lax.fori_loop(0, nk // 2, lambda i, acc: acc + (hits(2 * i) + hits(2 * i + 1)),
                            jnp.zeros((8, TQ), i32))
        acc = lax.cond(nk % 2 == 1, lambda a: a + hits(nk - 1), lambda a: a, acc)
        return jnp.sum(acc, axis=0, keepdims=True)

    def step(bit, t, cnt):
        cand = t | jnp.left_shift(jnp.int32(1), bit)
        c = count(lambda k: k >= cand)
        ok = c >= top
        return jnp.where(ok, cand, t), jnp.where(ok, c, cnt)

    c0 = count(lambda k: k >= 0)
    t = jnp.where(c0 >= top, 0, INT_MIN).astype(i32)
    cnt = jnp.where(c0 >= top, c0, jnp.int32(2 ** 30))
    t, cnt = lax.fori_loop(0, 31 - SEARCH_CHECKED, lambda i, st: step(30 - i, *st), (t, cnt))

    def checked_steps(st):
        bit, t, cnt, _ = st
        for s in range(SEARCH_STRIDE):
            t, cnt = step(bit - s, t, cnt)
        done = jnp.min(jnp.where(cnt == top, 1, 0))
        return bit - SEARCH_STRIDE, t, cnt, done

    more = lambda st: jnp.logical_and(st[0] >= 0, st[3] == 0)
    _, t, cnt, _ = lax.while_loop(more, checked_steps,
                                  (jnp.int32(SEARCH_CHECKED - 1), t, cnt, jnp.int32(0)))
    neg_key = np.array(NEG, np.float32).view(np.int32) ^ 0x7FFFFFFF
    any_excess = jnp.max(jnp.where(cnt > top, 1, 0)) > 0

    @pl.when(jnp.logical_not(any_excess))
    def _():
        thr = jnp.where(t == neg_key, t + 1, t)

        def fast(j, carry):
            s_ref[j] = pltpu.bitcast(jnp.where(s_ref[j] >= thr, 0.0, NEG), i32)
            return carry
        lax.fori_loop(0, nk, fast, 0)

    @pl.when(any_excess)
    def _():
        n_gt = count(lambda k: k > t)
        need = jnp.where(t == neg_key, 0, top - n_gt).astype(f32)

        def ranked(j, taken):
            keys = s_ref[j]
            for c in range(NC):
                kc = keys[c * LANES:(c + 1) * LANES, :]
                eq = jnp.where(kc == t, 1.0, 0.0)
                rank = taken + jnp.dot(tri_ref[...], eq.astype(bf16), preferred_element_type=f32)
                pick = jnp.where(kc > t, 1.0, jnp.where(rank <= need, eq, 0.0))
                s_ref[j, c * LANES:(c + 1) * LANES, :] = pltpu.bitcast(
                    jnp.where(pick > 0.5, 0.0, NEG), i32)
                taken = taken + jnp.sum(eq, axis=0, keepdims=True)
            return taken
        lax.fori_loop(0, nk, ranked, jnp.zeros((1, TQ), f32))

    m_ref[...] = jnp.full(m_ref.shape, NEG, f32)
    acc_ref[...] = jnp.zeros(acc_ref.shape, f32)

    def tile_step(nxt, cur):
        if cur is not None:
            j, slot, tile_max = cur
            m_old = [m_ref[g] for g in range(KV_HEADS)]
            m_new = [jnp.maximum(m_old[g], tile_max[g]) for g in range(KV_HEADS)]
            for g in range(KV_HEADS):
                m_ref[g] = m_new[g]
        if nxt is not None:
            jn, slot_n = nxt
            jc = jnp.minimum(jn, nk - 1)
        out = []
        for g in range(KV_HEADS):
            cmax = jnp.full((8, R), NEG, f32)
            for c in range(NC):
                rows = slice(c * LANES, (c + 1) * LANES)
                if nxt is not None:
                    bias = pltpu.bitcast(s_ref[jc, rows, :], f32)
                    lm = (jnp.dot(k_ref[0, jc, rows, :], qpad_ref[g], preferred_element_type=f32)
                          + jnp.concatenate([bias] * GROUPS, axis=1))
                    lm_refs[slot_n][g][rows, :] = lm
                    cmax = jnp.maximum(cmax, jnp.max(lm.reshape(LANES // 8, 8, R), axis=0))
                if cur is not None:
                    p_refs[slot][g][rows, :] = jnp.exp2(lm_refs[slot][g][rows, :] - m_new[g]).astype(bf16)
            out.append(jnp.max(cmax, axis=0, keepdims=True))
        if cur is not None:
            for g in range(KV_HEADS):
                acc_ref[g] = (jnp.exp2(m_old[g] - m_new[g]) * acc_ref[g]
                              + jnp.dot(vt_ref[0, j, g], p_refs[slot][g][...], preferred_element_type=f32))
        return tuple(out)

    def tile_pair(i, tile_max):
        nxt_max = tile_step((2 * i + 1, 1), (2 * i, 0, tile_max))
        return tile_step((2 * i + 2, 0), (2 * i + 1, 1, nxt_max))

    last_max = lax.fori_loop(0, nk // 2, tile_pair, tile_step((0, 0), None))

    @pl.when(nk % 2 == 1)
    def _():
        tile_step(None, (nk - 1, 0, last_max))

    for g in range(KV_HEADS):
        o_t = acc_ref[g, 0:HEAD_DIM, :] / acc_ref[g, HEAD_DIM:HEAD_DIM + 1, :]
        for i in range(0, GROUPS, 2):
            pair = jnp.concatenate([o_t[:, i * TQ:(i + 1) * TQ], o_t[:, (i + 1) * TQ:(i + 2) * TQ]], axis=0)
            h = g * GROUPS + i
            o_ref[0, :, h * HEAD_DIM:(h + 2) * HEAD_DIM] = pair.T.astype(bf16)


def _dsa(qt, qit, wit, kb, kib, vt, P, top):
    B, nq = qt.shape[:2]
    R = qt.shape[-1]
    TQ = R // GROUPS
    NT, TK = kb.shape[1:3]
    assert top <= TK and TQ % CHUNK == 0
    tri = jnp.asarray(np.tril(np.ones((LANES, LANES), np.float32)), bf16)
    return pl.pallas_call(
        functools.partial(_dsa_kernel, TQ=TQ, TK=TK, P=P, top=top),
        grid=(B, nq),
        in_specs=[pl.BlockSpec((1, 1, KV_HEADS, HEAD_DIM, R), lambda b, i: (b, i, 0, 0, 0)),
                  pl.BlockSpec((1, 1, IDX_DIM, IDX_HEADS * TQ), lambda b, i: (b, i, 0, 0)),
                  pl.BlockSpec((1, 1, WI_ROWS, TQ), lambda b, i: (b, i, 0, 0)),
                  pl.BlockSpec((1, NT, TK, IDX_DIM), lambda b, i: (b, 0, 0, 0)),
                  pl.BlockSpec((1, NT, TK, KV_HEADS * HEAD_DIM), lambda b, i: (b, 0, 0, 0)),
                  pl.BlockSpec((1, NT, KV_HEADS, VT_ROWS, TK), lambda b, i: (b, 0, 0, 0, 0)),
                  _const_spec((LANES, LANES))],
        out_specs=pl.BlockSpec((1, TQ, N_HEADS * HEAD_DIM), lambda b, i: (b, i, 0)),
        out_shape=jax.ShapeDtypeStruct((B, nq * TQ, N_HEADS * HEAD_DIM), bf16),
        scratch_shapes=[pltpu.VMEM((NT, TK, TQ), i32),
                        pltpu.VMEM((KV_HEADS, KV_HEADS * HEAD_DIM, R), bf16),
                        pltpu.VMEM((KV_HEADS, 1, R), f32),
                        pltpu.VMEM((KV_HEADS, VT_ROWS, R), f32)]
        + [pltpu.VMEM((TK, R), f32)] * (2 * KV_HEADS)
        + [pltpu.VMEM((TK, R), bf16)] * (2 * KV_HEADS),
        compiler_params=_params(("parallel", "arbitrary")),
        name="dsa",
    )(qt, qit, wit, kib, kb, vt, tri)


def _hgrn_mats(T):
    r = np.arange(T)[:, None]
    c = np.arange(T)[None, :]
    mats = [c <= r, c > r]
    b = T // 2
    while b >= 1:
        blk_r = r // (2 * b)
        mid = blk_r * 2 * b + b
        same = blk_r == c // (2 * b)
        late = r >= mid
        mats.append(same & ((late & (c >= mid) & (c <= r)) | (~late & (c > r) & (c < mid))))
        b //= 2
    return np.concatenate(mats, axis=0).astype(np.float32)


def _hgrn_kernel(x_ref, w_ref, lb_ref, g_ref, mats_ref, s0_ref, y_ref, st_out_ref, st_ref, *, T):
    t_idx = pl.program_id(1)

    @pl.when(t_idx == 0)
    def _():
        st_ref[...] = s0_ref[0]

    xb = x_ref[0].astype(bf16)
    proj = jnp.dot(xb, w_ref[...], preferred_element_type=f32)
    hq = proj[:, 0:HGRN_W]
    z = proj[:, HGRN_W:2 * HGRN_W]
    hv = proj[:, 2 * HGRN_W:3 * HGRN_W]
    hg = proj[:, 3 * HGRN_W:4 * HGRN_W]
    lb = lb_ref[...]
    logf = (jnp.minimum(z, 0.0) - jnp.log1p(jnp.exp(-jnp.abs(z)))) + jnp.log1p(lb * jnp.exp(-z))
    kk = (1.0 - lb) * _sigmoid(-z)
    qq = hq * _sigmoid(hq)
    gate = hg * _sigmoid(hg)

    hi = logf.astype(bf16)
    r1 = logf - hi.astype(f32)
    mid = r1.astype(bf16)
    lo = (r1 - mid.astype(f32)).astype(bf16)
    mats = mats_ref[...]
    e_all = (jnp.dot(mats, hi, preferred_element_type=f32)
             + jnp.dot(mats, mid, preferred_element_type=f32)
             + jnp.dot(mats, lo, preferred_element_type=f32))

    row = lax.broadcasted_iota(i32, (T, 1), 0)
    col = lax.broadcasted_iota(i32, (1, T), 1)
    nt = (((1,), (1,)), ((), ()))
    tn = (((0,), (0,)), ((), ()))
    for h in range(HGRN_HEADS):
        sl = slice(h * HGRN_DK, (h + 1) * HGRN_DK)
        q_h, k_h, v_h = qq[:, sl], kk[:, sl], hv[:, sl]
        vb = v_h.astype(bf16)
        cum = e_all[0:T, sl]
        suf = e_all[T:2 * T, sl]
        st = st_ref[h]
        o = lax.dot_general((q_h * jnp.exp(cum)).astype(bf16), st.astype(bf16), nt,
                            preferred_element_type=f32)
        o = o + jnp.sum(q_h * k_h, axis=-1, keepdims=True) * v_h
        scores = jnp.zeros((T, T), f32)
        b = T // 2
        lvl = 2
        while b >= 1:
            xdec = jnp.exp(e_all[lvl * T:(lvl + 1) * T, sl])
            late = ((row // b) % 2) == 1
            a_m = jnp.where(late, q_h * xdec, 0.0).astype(bf16)
            b_m = jnp.where(late, 0.0, k_h * xdec).astype(bf16)
            sc = lax.dot_general(a_m, b_m, nt, preferred_element_type=f32)
            if 2 * b < T:
                sc = jnp.where((row // (2 * b)) == (col // (2 * b)), sc, 0.0)
            scores = scores + sc
            b //= 2
            lvl += 1
        o = o + jnp.dot(scores.astype(bf16), vb, preferred_element_type=f32)
        kd = (k_h * jnp.exp(suf)).astype(bf16)
        st_ref[h] = (st * jnp.exp(cum[T - 1:T, :])
                     + lax.dot_general(vb, kd, tn, preferred_element_type=f32))
        o = o * lax.rsqrt(jnp.mean(o * o, axis=-1, keepdims=True) + LN_EPS) * g_ref[...]
        y_ref[0, :, sl] = (o * gate[:, sl]).astype(bf16)

    st_out_ref[0] = st_ref[...]


def _hgrn(x, w_h, lb, norm_g, s0):
    B, S, _ = x.shape
    T = _row_tile(S, 2 * CHUNK)
    mats = jnp.asarray(_hgrn_mats(T), bf16)
    nm = mats.shape[0]
    st0 = jnp.swapaxes(s0.astype(f32), 2, 3)
    y, st = pl.pallas_call(
        functools.partial(_hgrn_kernel, T=T),
        grid=(B, S // T),
        in_specs=[pl.BlockSpec((1, T, D_MODEL), lambda b, t: (b, t, 0)),
                  _const_spec((D_MODEL, 4 * HGRN_W)),
                  _const_spec((1, HGRN_W)),
                  _const_spec((1, HGRN_DV)),
                  _const_spec((nm, T)),
                  pl.BlockSpec((1, HGRN_HEADS, HGRN_DV, HGRN_DK), lambda b, t: (b, 0, 0, 0))],
        out_specs=[pl.BlockSpec((1, T, HGRN_W), lambda b, t: (b, t, 0)),
                   pl.BlockSpec((1, HGRN_HEADS, HGRN_DV, HGRN_DK), lambda b, t: (b, 0, 0, 0))],
        out_shape=[jax.ShapeDtypeStruct((B, S, HGRN_W), bf16),
                   jax.ShapeDtypeStruct((B, HGRN_HEADS, HGRN_DV, HGRN_DK), f32)],
        scratch_shapes=[pltpu.VMEM((HGRN_HEADS, HGRN_DV, HGRN_DK), f32)],
        compiler_params=_params(("parallel", "arbitrary")),
        name="hgrn",
    )(x, w_h, lb, norm_g, mats, st0)
    return y, jnp.swapaxes(st, 2, 3)


HIST = CONV_W - 1
PADROWS = 8


def _conv_rows(buf_ref, cols, u, w, bias, T):
    buf_ref[PADROWS:PADROWS + T, cols] = u
    y = bias
    for j in range(CONV_W):
        y = y + w[j:j + 1, :] * buf_ref[PADROWS - HIST + j:PADROWS - HIST + j + T, cols]
    buf_ref[PADROWS - HIST:PADROWS, cols] = u[T - HIST:T, :]
    return y


def _merge_kernel(x_ref, ya_ref, yb_ref, hist_ref, wc_ref, wg_ref, wbr_ref, wout_ref, cw_ref, cb_ref,
                  g_ref, b_ref, o_ref, hist_out_ref, buf_ref, *, T, alpha):
    @pl.when(pl.program_id(1) == 0)
    def _():
        buf_ref[PADROWS - HIST:PADROWS, :] = hist_ref[0]

    x = x_ref[0]
    xb = x.astype(bf16)
    c = jnp.dot(xb, wc_ref[...], preferred_element_type=f32)
    W = SCONV_WIDTH
    u = c[:, W:2 * W] * c[:, 2 * W:3 * W]
    u_conv = _conv_rows(buf_ref, slice(0, W), u, cw_ref[...], cb_ref[...], T)
    yc = (c[:, 0:W] * u_conv).astype(bf16)
    hist_out_ref[0] = buf_ref[PADROWS - HIST:PADROWS, :]

    merged = None
    for i, yb in enumerate((ya_ref[0], yb_ref[0], yc)):
        gate = _sigmoid(jnp.dot(xb, wg_ref[:, i * D_MODEL:(i + 1) * D_MODEL], preferred_element_type=f32))
        term = gate * jnp.dot(yb, wbr_ref[i], preferred_element_type=f32)
        merged = term if merged is None else merged + term
    out = jnp.dot(merged.astype(bf16), wout_ref[...], preferred_element_type=f32)
    o_ref[0] = _layer_norm(alpha * x + out, g_ref[...], b_ref[...])


def _merge(x, ya, yb, hist, wc, wg, wbr, wout, cw, cb, g, b, alpha):
    B, S, _ = x.shape
    T = _row_tile(S, 256)
    row = lambda w: pl.BlockSpec((1, T, w), lambda bb, t: (bb, t, 0))
    hspec = pl.BlockSpec((1, HIST, SCONV_WIDTH), lambda bb, t: (bb, 0, 0))
    return pl.pallas_call(
        functools.partial(_merge_kernel, T=T, alpha=alpha),
        grid=(B, S // T),
        in_specs=[row(D_MODEL), row(BRANCH_WIDTH), row(BRANCH_WIDTH), hspec,
                  _const_spec((D_MODEL, 3 * SCONV_WIDTH)), _const_spec((D_MODEL, N_BRANCH * D_MODEL)),
                  _const_spec((N_BRANCH, BRANCH_WIDTH, D_MODEL)), _const_spec((D_MODEL, D_MODEL)),
                  _const_spec((CONV_W, SCONV_WIDTH)), _const_spec((1, SCONV_WIDTH)),
                  _const_spec((1, D_MODEL)), _const_spec((1, D_MODEL))],
        out_specs=[row(D_MODEL), hspec],
        out_shape=[jax.ShapeDtypeStruct((B, S, D_MODEL), f32),
                   jax.ShapeDtypeStruct((B, HIST, SCONV_WIDTH), f32)],
        scratch_shapes=[pltpu.VMEM((PADROWS + T, SCONV_WIDTH), f32)],
        compiler_params=_params(("parallel", "arbitrary")),
        name="merge",
    )(x, ya, yb, hist, wc, wg, wbr, wout, cw, cb, g, b)


def _ffn_kernel(x_ref, hist_ref, wup_ref, cw_ref, cb_ref, wdn_ref, g_ref, b_ref, o_ref, hist_out_ref,
                *bufs, T, alpha):
    n_steps = D_FF // FFN_COLS
    col_ranges = [slice(off + c * FFN_COLS, off + (c + 1) * FFN_COLS)
                  for c in range(n_steps) for off in (0, D_FF)]

    @pl.when(pl.program_id(1) == 0)
    def _():
        for buf, cols in zip(bufs, col_ranges):
            buf[PADROWS - HIST:PADROWS, :] = hist_ref[0, :, cols]

    x = x_ref[0]
    xb = x.astype(bf16)

    def up_conv(cidx):
        halves = []
        for half in range(2):
            buf, cols = bufs[2 * cidx + half], col_ranges[2 * cidx + half]
            h = jnp.dot(xb, wup_ref[:, cols], preferred_element_type=f32)
            halves.append(_conv_rows(buf, slice(None), h, cw_ref[:, cols], cb_ref[:, cols], T))
            hist_out_ref[0, :, cols] = buf[PADROWS - HIST:PADROWS, :]
        return halves

    acc = jnp.zeros((T, D_MODEL), f32)
    nxt = up_conv(0)
    for cidx in range(n_steps):
        a_g, b_v = nxt
        if cidx + 1 < n_steps:
            nxt = up_conv(cidx + 1)
        act = (a_g * _sigmoid(a_g) * b_v).astype(bf16)
        acc = acc + jnp.dot(act, wdn_ref[cidx * FFN_COLS:(cidx + 1) * FFN_COLS, :],
                            preferred_element_type=f32)
    o_ref[0] = _layer_norm(alpha * x + acc, g_ref[...], b_ref[...])


def _ffn(x, hist, wup, cw, cb, wdn, g, b, alpha):
    B, S, _ = x.shape
    T = _row_tile(S, 256)
    row = pl.BlockSpec((1, T, D_MODEL), lambda bb, t: (bb, t, 0))
    hspec = pl.BlockSpec((1, HIST, 2 * D_FF), lambda bb, t: (bb, 0, 0))
    return pl.pallas_call(
        functools.partial(_ffn_kernel, T=T, alpha=alpha),
        grid=(B, S // T),
        in_specs=[row, hspec, _const_spec((D_MODEL, 2 * D_FF)), _const_spec((CONV_W, 2 * D_FF)),
                  _const_spec((1, 2 * D_FF)), _const_spec((D_FF, D_MODEL)),
                  _const_spec((1, D_MODEL)), _const_spec((1, D_MODEL))],
        out_specs=[row, hspec],
        out_shape=[jax.ShapeDtypeStruct((B, S, D_MODEL), f32),
                   jax.ShapeDtypeStruct((B, HIST, 2 * D_FF), f32)],
        scratch_shapes=[pltpu.VMEM((PADROWS + T, FFN_COLS), f32)] * (2 * D_FF // FFN_COLS),
        compiler_params=_params(("parallel", "arbitrary")),
        name="ffn",
    )(x, hist, wup, cw, cb, wdn, g, b)


def _cast_kernel(x_ref, o_ref):
    o_ref[...] = x_ref[...].astype(o_ref.dtype)


def _to_bf16(w):
    D, Rw, C = w.shape
    T = _row_tile(Rw, LANES)
    spec = pl.BlockSpec((1, T, C), lambda d, t: (d, t, 0))
    return pl.pallas_call(_cast_kernel, grid=(D, Rw // T), in_specs=[spec], out_specs=spec,
                          out_shape=jax.ShapeDtypeStruct(w.shape, bf16),
                          compiler_params=_params(("parallel", "parallel")), name="to_bf16")(w)


def _layer_weights(l, depth, w_in, hgrn_lb_logits, hgrn_norm_g, sconv_w, sconv_b, w_branch, w_out,
                   ln1_g, ln1_b, w_up, ffn_conv_w, ffn_conv_b, w_down, ln2_g, ln2_b):
    offs = np.concatenate([[0], np.cumsum(IN_SIZES)]).tolist()
    sec = lambda i, j=None: w_in[l][:, offs[i]:offs[(i if j is None else j) + 1]]
    a_q, a_k, a_v, i_q, i_k, i_w = (sec(i) for i in range(6))
    zpad = lambda n: jnp.zeros((D_MODEL, n), w_in.dtype)
    w_row = jnp.concatenate([a_k, i_k, zpad(LANES - IDX_DIM), a_v], axis=1).astype(bf16)
    w_t = jnp.concatenate([a_q, i_q, a_v, i_w, zpad(WT_ROWS - 900)], axis=1).T.astype(bf16)
    lbp = jax.nn.softmax(hgrn_lb_logits.astype(f32), axis=0)
    lb = (jnp.cumsum(lbp, axis=0) - lbp[0])[l].reshape(1, HGRN_W)
    row = lambda a: a.reshape(1, -1).astype(f32)
    return dict(
        w_row=w_row, w_t=w_t, w_h=sec(6, 9).astype(bf16), lb=lb, norm_g=row(hgrn_norm_g[l]),
        w_c=sec(10, 12).astype(bf16), w_g=sec(13).astype(bf16),
        w_br=w_branch[l].astype(bf16), w_out=w_out[l].astype(bf16),
        sconv_w=sconv_w[l].astype(f32), sconv_b=row(sconv_b[l]),
        ln1_g=row(ln1_g[l]), ln1_b=row(ln1_b[l]),
        w_up=w_up[l].astype(bf16), ffn_w=ffn_conv_w[l].astype(f32), ffn_b=row(ffn_conv_b[l]),
        w_down=w_down[l].astype(bf16), ln2_g=row(ln2_g[l]), ln2_b=row(ln2_b[l]))


def _key_tiles(a, TK):
    B, L, C = a.shape
    NT = -(-L // TK)
    a = jnp.pad(a, ((0, 0), (0, NT * TK - L), (0, 0)))
    return a.reshape(B, NT, TK, C)


def _trunk_layer(x, tables, P, k_past, v_past, ki_past, s0, sc_hist, ffn_hist, w, alpha):
    B, S, _ = x.shape
    TK = KEY_TILE
    a = _attn_proj(x, w["w_row"], w["w_t"], *tables)
    k, v, ki = a["k"], a["v"], a["ki"]
    L = P + S
    if k_past is None and S % TK == 0:
        kb = a["kb"].reshape(B, S // TK, TK, -1)
        kib = a["kib"].reshape(B, S // TK, TK, -1)
        vt = a["vt"]
    else:
        kb, kib, v_all = a["kb"], a["kib"], v.astype(bf16)
        if k_past is not None:
            kb = jnp.concatenate([k_past.reshape(B, P, -1).astype(bf16), kb], axis=1)
            kib = jnp.concatenate([ki_past.astype(bf16), kib], axis=1)
            v_all = jnp.concatenate([v_past.reshape(B, P, -1).astype(bf16), v_all], axis=1)
        kb, kib = _key_tiles(kb, TK), _key_tiles(kib, TK)
        vt = _key_tiles(v_all, TK).reshape(B, -1, TK, KV_HEADS, HEAD_DIM).transpose(0, 1, 3, 4, 2)
        ones = jnp.zeros(vt.shape[:3] + (VT_ROWS - HEAD_DIM, TK), bf16).at[:, :, :, 0, :].set(1.0)
        vt = jnp.concatenate([vt, ones], axis=3)
    y_a = _dsa(a["qt"], a["qit"], a["wit"], kb, kib, vt, P, min(TOPK_MAX, L // 4))
    y_b, s_new = _hgrn(x, w["w_h"], w["lb"], w["norm_g"], s0)
    x1, sc_new = _merge(x, y_a, y_b, sc_hist, w["w_c"], w["w_g"], w["w_br"], w["w_out"],
                        w["sconv_w"], w["sconv_b"], w["ln1_g"], w["ln1_b"], alpha)
    x2, ffn_new = _ffn(x1, ffn_hist, w["w_up"], w["ffn_w"], w["ffn_b"], w["w_down"],
                       w["ln2_g"], w["ln2_b"], alpha)
    new = (k.reshape(B, S, KV_HEADS, HEAD_DIM), v.reshape(B, S, KV_HEADS, HEAD_DIM), ki,
           s_new, sc_new, ffn_new)
    return x2, new


def kernel(x_prompt, x_sample, cache_attn_k, cache_attn_v, cache_idx_k, state_hgrn, state_sconv,
           state_ffn_conv, w_in, hgrn_lb_logits, hgrn_norm_g, sconv_w, sconv_b, w_branch, w_out,
           ln1_g, ln1_b, w_up, ffn_conv_w, ffn_conv_b, w_down, ln2_g, ln2_b):
    depth = w_in.shape[0]
    alpha = (2 * depth) ** 0.25
    B, S, _ = x_prompt.shape
    DB, DS, _ = x_sample.shape
    P = cache_attn_k.shape[2]
    cs_p = _rotary_tables(jnp.arange(S))
    cs_s = _rotary_tables(P + jnp.arange(DS))
    xp, xs = x_prompt, x_sample
    st_p = [[] for _ in range(6)]
    st_s = [[] for _ in range(6)]
    w_in_b = _to_bf16(w_in)
    for l in range(depth):
        w = _layer_weights(l, depth, w_in_b, hgrn_lb_logits, hgrn_norm_g, sconv_w, sconv_b, w_branch,
                           w_out, ln1_g, ln1_b, w_up, ffn_conv_w, ffn_conv_b, w_down, ln2_g, ln2_b)
        xp, new_p = _trunk_layer(
            xp, cs_p, 0, None, None, None,
            jnp.zeros((B, HGRN_HEADS, HGRN_DK, HGRN_DV), f32),
            jnp.zeros((B, HIST, SCONV_WIDTH), f32),
            jnp.zeros((B, HIST, 2 * D_FF), f32), w, alpha)
        xs, new_s = _trunk_layer(
            xs, cs_s, P, cache_attn_k[l], cache_attn_v[l], cache_idx_k[l], state_hgrn[l],
            state_sconv[l], state_ffn_conv[l], w, alpha)
        for j in range(6):
            st_p[j].append(new_p[j])
            st_s[j].append(new_s[j])
    outs_p = [jnp.stack(a, axis=0) for a in st_p]
    outs_s = [jnp.stack(a, axis=0) for a in st_s]
    return (xp, xs, *outs_p, *outs_s)
```

```python
import functools

import numpy as np
import jax
import jax.numpy as jnp
from jax import lax
from jax.experimental import pallas as pl
from jax.experimental.pallas import tpu as pltpu

D_MODEL = 1024
CHUNK = 64
N_HEADS = 8
HEAD_DIM = 64
KV_HEADS = 2
GROUPS = N_HEADS // KV_HEADS
IDX_HEADS = 4
IDX_DIM = 64
TOPK_MAX = 256
ROPE_THETA = 500000.0
ATTN_SCALE = HEAD_DIM ** -0.5
IDX_SCALE = (IDX_DIM ** -0.5) * (IDX_HEADS ** -0.5)
NEG = -1e30
HGRN_HEADS = 4
HGRN_DK = 128
HGRN_DV = 128
HGRN_W = HGRN_HEADS * HGRN_DK
SCONV_WIDTH = 512
CONV_W = 3
BRANCH_WIDTH = 512
N_BRANCH = 3
D_FF = 2816
LN_EPS = 1e-5
IN_SIZES = (N_HEADS * HEAD_DIM, KV_HEADS * HEAD_DIM, KV_HEADS * HEAD_DIM,
            IDX_HEADS * IDX_DIM, IDX_DIM, IDX_HEADS,
            HGRN_W, HGRN_W, HGRN_W, HGRN_W,
            SCONV_WIDTH, SCONV_WIDTH, SCONV_WIDTH,
            N_BRANCH * D_MODEL)

LANES = 128
VMEM_LIMIT = 56 * 1024 * 1024
KEY_TILE = 512
QUERY_TILE = 256
VT_ROWS = HEAD_DIM + 16
FFN_COLS = 256
INT_MIN = -2 ** 31

f32 = jnp.float32
bf16 = jnp.bfloat16
i32 = jnp.int32


def _row_tile(s, want):
    return want if s % want == 0 else s


def _const_spec(shape):
    nd = len(shape)
    return pl.BlockSpec(shape, lambda *_: (0,) * nd, pipeline_mode=pl.Buffered(1))


def _params(sem):
    return pltpu.CompilerParams(dimension_semantics=sem, vmem_limit_bytes=VMEM_LIMIT)


def _layer_norm(z, g, b):
    mu = jnp.mean(z, axis=-1, keepdims=True)
    d = z - mu
    var = jnp.mean(d * d, axis=-1, keepdims=True)
    return d * lax.rsqrt(var + LN_EPS) * g + b


def _sigmoid(x):
    return 1.0 / (1.0 + jnp.exp(-x))


Q_PRESCALE = ATTN_SCALE * float(np.log2(np.e))
ROT = HEAD_DIM // 4
WR_COLS = 384
WT_ROWS = 912
WI_ROWS = 8


def _attn_proj_kernel(x_ref, wr_ref, wt_ref, cs_ref, cst_ref, k_ref, v_ref, ki_ref, kb_ref, kib_ref,
                      vt_ref, qt_ref, qit_ref, wit_ref, *, T, TQ):
    xb = x_ref[0].astype(bf16)
    pr = jnp.dot(xb, wr_ref[...], preferred_element_type=f32)
    c, sa, sb = cs_ref[0], cs_ref[1], cs_ref[2]

    def rot(xg):
        return xg * c + pltpu.roll(xg, LANES - ROT // 2, 1) * sa + pltpu.roll(xg, ROT // 2, 1) * sb

    k = rot(pr[:, 0:LANES])
    ki = rot(pr[:, LANES:2 * LANES])[:, :IDX_DIM]
    k_ref[0] = k
    kb_ref[0] = k.astype(bf16)
    ki_ref[0] = ki
    kib_ref[0] = ki.astype(bf16)
    v_ref[0] = pr[:, 2 * LANES:3 * LANES]

    nt = (((1,), (1,)), ((), ()))
    pt = lax.dot_general(wt_ref[...], xb, nt, preferred_element_type=f32)
    ct, st = cst_ref[0], cst_ref[1]

    def rot_t(hb):
        lead = hb[0:ROT] * ct + jnp.concatenate([hb[ROT // 2:ROT], hb[0:ROT // 2]], axis=0) * st
        return jnp.concatenate([lead, hb[ROT:HEAD_DIM]], axis=0)

    nqb = T // TQ
    for h in range(N_HEADS):
        hb = (rot_t(pt[h * HEAD_DIM:(h + 1) * HEAD_DIM]) * Q_PRESCALE).astype(bf16)
        g, i = divmod(h, GROUPS)
        for n in range(nqb):
            qt_ref[0, n, g, :, i * TQ:(i + 1) * TQ] = hb[:, n * TQ:(n + 1) * TQ]
    base = N_HEADS * HEAD_DIM
    for h in range(IDX_HEADS):
        hb = rot_t(pt[base + h * IDX_DIM:base + (h + 1) * IDX_DIM]).astype(bf16)
        for n in range(nqb):
            qit_ref[0, n, :, h * TQ:(h + 1) * TQ] = hb[:, n * TQ:(n + 1) * TQ]
    base += IDX_HEADS * IDX_DIM
    ones_rows = jnp.where(lax.broadcasted_iota(i32, (VT_ROWS - HEAD_DIM, T), 0) == 0, 1.0, 0.0)
    for g in range(KV_HEADS):
        vt_ref[0, 0, g, 0:HEAD_DIM, :] = pt[base + g * HEAD_DIM:base + (g + 1) * HEAD_DIM].astype(bf16)
        vt_ref[0, 0, g, HEAD_DIM:VT_ROWS, :] = ones_rows.astype(bf16)
    base += KV_HEADS * HEAD_DIM
    for n in range(nqb):
        wit_ref[0, n] = pt[base:base + WI_ROWS, n * TQ:(n + 1) * TQ]


def _attn_proj(x, w_row, w_t, cs, cst):
    B, S, _ = x.shape
    T = _row_tile(S, KEY_TILE)
    TQ = _row_tile(S, QUERY_TILE)
    nqb = T // TQ
    R = GROUPS * TQ
    row = lambda w: pl.BlockSpec((1, T, w), lambda b, t: (b, t, 0))
    names = ("k", "v", "ki", "kb", "kib", "vt", "qt", "qit", "wit")
    outs = pl.pallas_call(
        functools.partial(_attn_proj_kernel, T=T, TQ=TQ),
        grid=(B, S // T),
        in_specs=[row(D_MODEL), _const_spec((D_MODEL, WR_COLS)), _const_spec((WT_ROWS, D_MODEL)),
                  pl.BlockSpec((3, T, LANES), lambda b, t: (0, t, 0)),
                  pl.BlockSpec((2, ROT, T), lambda b, t: (0, 0, t))],
        out_specs=[row(128), row(128), row(IDX_DIM), row(128), row(IDX_DIM),
                   pl.BlockSpec((1, 1, KV_HEADS, VT_ROWS, T), lambda b, t: (b, t, 0, 0, 0)),
                   pl.BlockSpec((1, nqb, KV_HEADS, HEAD_DIM, R), lambda b, t: (b, t, 0, 0, 0)),
                   pl.BlockSpec((1, nqb, IDX_DIM, IDX_HEADS * TQ), lambda b, t: (b, t, 0, 0)),
                   pl.BlockSpec((1, nqb, WI_ROWS, TQ), lambda b, t: (b, t, 0, 0))],
        out_shape=[jax.ShapeDtypeStruct((B, S, 128), f32),
                   jax.ShapeDtypeStruct((B, S, 128), f32),
                   jax.ShapeDtypeStruct((B, S, IDX_DIM), f32),
                   jax.ShapeDtypeStruct((B, S, 128), bf16),
                   jax.ShapeDtypeStruct((B, S, IDX_DIM), bf16),
                   jax.ShapeDtypeStruct((B, S // T, KV_HEADS, VT_ROWS, T), bf16),
                   jax.ShapeDtypeStruct((B, S // TQ, KV_HEADS, HEAD_DIM, R), bf16),
                   jax.ShapeDtypeStruct((B, S // TQ, IDX_DIM, IDX_HEADS * TQ), bf16),
                   jax.ShapeDtypeStruct((B, S // TQ, WI_ROWS, TQ), f32)],
        compiler_params=_params(("parallel", "parallel")),
        name="attn_proj",
    )(x, w_row, w_t, cs, cst)
    return dict(zip(names, outs))


def _rotary_tables(pos):
    half = ROT // 2
    inv_freq = ROPE_THETA ** (-jnp.arange(half, dtype=f32) / half)
    ang = pos.astype(f32)[:, None] * inv_freq[None, :]
    cos = jnp.cos(ang)
    sin = jnp.sin(ang)
    S = pos.shape[0]
    one = jnp.ones((S, HEAD_DIM - ROT), f32)
    zero = jnp.zeros((S, HEAD_DIM - ROT), f32)
    zh = jnp.zeros((S, half), f32)
    c = jnp.concatenate([cos, cos, one], axis=1)
    sa = jnp.concatenate([-sin, zh, zero], axis=1)
    sb = jnp.concatenate([zh, sin, zero], axis=1)
    tile2 = lambda a: jnp.concatenate([a, a], axis=1)
    cs = jnp.stack([tile2(c), tile2(sa), tile2(sb)], axis=0)
    cst = jnp.stack([jnp.concatenate([cos, cos], axis=1).T,
                     jnp.concatenate([-sin, sin], axis=1).T], axis=0)
    return cs, cst


def _dsa_kernel(qt_ref, qit_ref, wit_ref, ki_ref, k_ref, vt_ref, tri_ref, o_ref,
                s_ref, qpad_ref, m_ref, acc_ref, *bufs, TQ, TK, P, top):
    lm_refs = (bufs[0:KV_HEADS], bufs[KV_HEADS:2 * KV_HEADS])
    p_refs = (bufs[2 * KV_HEADS:3 * KV_HEADS], bufs[3 * KV_HEADS:4 * KV_HEADS])
    qb = pl.program_id(1)
    q0 = P + qb * TQ
    nk = (q0 + TQ + TK - 1) // TK
    n_full = q0 // TK
    NC = TK // LANES
    R = GROUPS * TQ
    q_lim = ((q0 + lax.broadcasted_iota(i32, (1, TQ), 1)) // CHUNK + 1) * CHUNK
    wit = wit_ref[0, 0] * IDX_SCALE
    zero_head = jnp.zeros((HEAD_DIM, R), bf16)
    qpad_ref[0] = jnp.concatenate([qt_ref[0, 0, 0], zero_head], axis=0)
    qpad_ref[1] = jnp.concatenate([zero_head, qt_ref[0, 0, 1]], axis=0)

    def score_tile(masked, j, carry):
        for c in range(NC):
            rows = slice(c * LANES, (c + 1) * LANES)
            rel = jnp.maximum(jnp.dot(ki_ref[0, j, rows, :], qit_ref[0, 0], preferred_element_type=f32), 0.0)
            score = wit[0:1, :] * rel[:, 0:TQ]
            for h in range(1, IDX_HEADS):
                score = score + wit[h:h + 1, :] * rel[:, h * TQ:(h + 1) * TQ]
            if masked:
                k_pos = j * TK + c * LANES + lax.broadcasted_iota(i32, (LANES, TQ), 0)
                score = jnp.where(k_pos < q_lim, score, NEG)
            bits = pltpu.bitcast(score, i32)
            s_ref[j, rows, :] = jnp.where(bits >= 0, bits, bits ^ 0x7FFFFFFF)
        return carry

    lax.fori_loop(0, n_full, functools.partial(score_tile, False), 0)
    lax.fori_loop(n_full, nk, functools.partial(score_tile, True), 0)

    def count(pred):
        def hits(j):
            accs = [None] * 4
            for r in range(TK // 8):
                hit = jnp.where(pred(s_ref[j, r * 8:(r + 1) * 8, :]), 1, 0)
                accs[r % 4] = hit if accs[r % 4] is None else accs[r % 4] + hit
            return (accs[0] + accs[1]) + (accs[2] + accs[3])
        acc = lax.fori_loop(0, nk // 2, lambda i, acc: acc + (hits(2 * i) + hits(2 * i + 1)),
                            jnp.zeros((8, TQ), i32))
        acc = lax.cond(nk % 2 == 1, lambda a: a + hits(nk - 1), lambda a: a, acc)
        return jnp.sum(acc, axis=0, keepdims=True)

    def step(bit, t, cnt):
        cand = t | jnp.left_shift(jnp.int32(1), bit)
        c = count(lambda k: k >= cand)
        ok = c >= top
        return jnp.where(ok, cand, t), jnp.where(ok, c, cnt)

    c0 = count(lambda k: k >= 0)
    t = jnp.where(c0 >= top, 0, INT_MIN).astype(i32)
    cnt = jnp.where(c0 >= top, c0, jnp.int32(2 ** 30))
    t, cnt = lax.fori_loop(0, 31, lambda i, st: step(30 - i, *st), (t, cnt))
    neg_key = np.array(NEG, np.float32).view(np.int32) ^ 0x7FFFFFFF
    any_excess = jnp.max(jnp.where(cnt > top, 1, 0)) > 0

    @pl.when(jnp.logical_not(any_excess))
    def _():
        thr = jnp.where(t == neg_key, t + 1, t)

        def fast(j, carry):
            s_ref[j] = pltpu.bitcast(jnp.where(s_ref[j] >= thr, 0.0, NEG), i32)
            return carry
        lax.fori_loop(0, nk, fast, 0)

    @pl.when(any_excess)
    def _():
        n_gt = count(lambda k: k > t)
        need = jnp.where(t == neg_key, 0, top - n_gt).astype(f32)

        def ranked(j, taken):
            keys = s_ref[j]
            for c in range(NC):
                kc = keys[c * LANES:(c + 1) * LANES, :]
                eq = jnp.where(kc == t, 1.0, 0.0)
                rank = taken + jnp.dot(tri_ref[...], eq.astype(bf16), preferred_element_type=f32)
                pick = jnp.where(kc > t, 1.0, jnp.where(rank <= need, eq, 0.0))
                s_ref[j, c * LANES:(c + 1) * LANES, :] = pltpu.bitcast(
                    jnp.where(pick > 0.5, 0.0, NEG), i32)
                taken = taken + jnp.sum(eq, axis=0, keepdims=True)
            return taken
        lax.fori_loop(0, nk, ranked, jnp.zeros((1, TQ), f32))

    m_ref[...] = jnp.full(m_ref.shape, NEG, f32)
    acc_ref[...] = jnp.zeros(acc_ref.shape, f32)

    def tile_step(nxt, cur):
        if cur is not None:
            j, slot, tile_max = cur
            m_old = [m_ref[g] for g in range(KV_HEADS)]
            m_new = [jnp.maximum(m_old[g], tile_max[g]) for g in range(KV_HEADS)]
            for g in range(KV_HEADS):
                m_ref[g] = m_new[g]
        if nxt is not None:
            jn, slot_n = nxt
            jc = jnp.minimum(jn, nk - 1)
        out = []
        for g in range(KV_HEADS):
            cmax = jnp.full((8, R), NEG, f32)
            for c in range(NC):
                rows = slice(c * LANES, (c + 1) * LANES)
                if nxt is not None:
                    bias = pltpu.bitcast(s_ref[jc, rows, :], f32)
                    lm = (jnp.dot(k_ref[0, jc, rows, :], qpad_ref[g], preferred_element_type=f32)
                          + jnp.concatenate([bias] * GROUPS, axis=1))
                    lm_refs[slot_n][g][rows, :] = lm
                    cmax = jnp.maximum(cmax, jnp.max(lm.reshape(LANES // 8, 8, R), axis=0))
                if cur is not None:
                    p_refs[slot][g][rows, :] = jnp.exp2(lm_refs[slot][g][rows, :] - m_new[g]).astype(bf16)
            out.append(jnp.max(cmax, axis=0, keepdims=True))
        if cur is not None:
            for g in range(KV_HEADS):
                acc_ref[g] = (jnp.exp2(m_old[g] - m_new[g]) * acc_ref[g]
                              + jnp.dot(vt_ref[0, j, g], p_refs[slot][g][...], preferred_element_type=f32))
        return tuple(out)

    def tile_pair(i, tile_max):
        nxt_max = tile_step((2 * i + 1, 1), (2 * i, 0, tile_max))
        return tile_step((2 * i + 2, 0), (2 * i + 1, 1, nxt_max))

    last_max = lax.fori_loop(0, nk // 2, tile_pair, tile_step((0, 0), None))

    @pl.when(nk % 2 == 1)
    def _():
        tile_step(None, (nk - 1, 0, last_max))

    for g in range(KV_HEADS):
        o_t = acc_ref[g, 0:HEAD_DIM, :] / acc_ref[g, HEAD_DIM:HEAD_DIM + 1, :]
        for i in range(0, GROUPS, 2):
            pair = jnp.concatenate([o_t[:, i * TQ:(i + 1) * TQ], o_t[:, (i + 1) * TQ:(i + 2) * TQ]], axis=0)
            h = g * GROUPS + i
            o_ref[0, :, h * HEAD_DIM:(h + 2) * HEAD_DIM] = pair.T.astype(bf16)


def _dsa(qt, qit, wit, kb, kib, vt, P, top):
    B, nq = qt.shape[:2]
    R = qt.shape[-1]
    TQ = R // GROUPS
    NT, TK = kb.shape[1:3]
    assert top <= TK and TQ % CHUNK == 0
    tri = jnp.asarray(np.tril(np.ones((LANES, LANES), np.float32)), bf16)
    return pl.pallas_call(
        functools.partial(_dsa_kernel, TQ=TQ, TK=TK, P=P, top=top),
        grid=(B, nq),
        in_specs=[pl.BlockSpec((1, 1, KV_HEADS, HEAD_DIM, R), lambda b, i: (b, i, 0, 0, 0)),
                  pl.BlockSpec((1, 1, IDX_DIM, IDX_HEADS * TQ), lambda b, i: (b, i, 0, 0)),
                  pl.BlockSpec((1, 1, WI_ROWS, TQ), lambda b, i: (b, i, 0, 0)),
                  pl.BlockSpec((1, NT, TK, IDX_DIM), lambda b, i: (b, 0, 0, 0)),
                  pl.BlockSpec((1, NT, TK, KV_HEADS * HEAD_DIM), lambda b, i: (b, 0, 0, 0)),
                  pl.BlockSpec((1, NT, KV_HEADS, VT_ROWS, TK), lambda b, i: (b, 0, 0, 0, 0)),
                  _const_spec((LANES, LANES))],
        out_specs=pl.BlockSpec((1, TQ, N_HEADS * HEAD_DIM), lambda b, i: (b, i, 0)),
        out_shape=jax.ShapeDtypeStruct((B, nq * TQ, N_HEADS * HEAD_DIM), bf16),
        scratch_shapes=[pltpu.VMEM((NT, TK, TQ), i32),
                        pltpu.VMEM((KV_HEADS, KV_HEADS * HEAD_DIM, R), bf16),
                        pltpu.VMEM((KV_HEADS, 1, R), f32),
                        pltpu.VMEM((KV_HEADS, VT_ROWS, R), f32)]
        + [pltpu.VMEM((TK, R), f32)] * (2 * KV_HEADS)
        + [pltpu.VMEM((TK, R), bf16)] * (2 * KV_HEADS),
        compiler_params=_params(("parallel", "arbitrary")),
        name="dsa",
    )(qt, qit, wit, kib, kb, vt, tri)


def _hgrn_mats(T):
    r = np.arange(T)[:, None]
    c = np.arange(T)[None, :]
    mats = [c <= r, c > r]
    b = T // 2
    while b >= 1:
        blk_r = r // (2 * b)
        mid = blk_r * 2 * b + b
        same = blk_r == c // (2 * b)
        late = r >= mid
        mats.append(same & ((late & (c >= mid) & (c <= r)) | (~late & (c > r) & (c < mid))))
        b //= 2
    return np.concatenate(mats, axis=0).astype(np.float32)


def _hgrn_kernel(x_ref, w_ref, lb_ref, g_ref, mats_ref, s0_ref, y_ref, st_out_ref, st_ref, *, T):
    t_idx = pl.program_id(1)

    @pl.when(t_idx == 0)
    def _():
        st_ref[...] = s0_ref[0]

    xb = x_ref[0].astype(bf16)
    proj = jnp.dot(xb, w_ref[...], preferred_element_type=f32)
    hq = proj[:, 0:HGRN_W]
    z = proj[:, HGRN_W:2 * HGRN_W]
    hv = proj[:, 2 * HGRN_W:3 * HGRN_W]
    hg = proj[:, 3 * HGRN_W:4 * HGRN_W]
    lb = lb_ref[...]
    logf = (jnp.minimum(z, 0.0) - jnp.log1p(jnp.exp(-jnp.abs(z)))) + jnp.log1p(lb * jnp.exp(-z))
    kk = (1.0 - lb) * _sigmoid(-z)
    qq = hq * _sigmoid(hq)
    gate = hg * _sigmoid(hg)

    hi = logf.astype(bf16)
    r1 = logf - hi.astype(f32)
    mid = r1.astype(bf16)
    lo = (r1 - mid.astype(f32)).astype(bf16)
    mats = mats_ref[...]
    e_all = (jnp.dot(mats, hi, preferred_element_type=f32)
             + jnp.dot(mats, mid, preferred_element_type=f32)
             + jnp.dot(mats, lo, preferred_element_type=f32))

    row = lax.broadcasted_iota(i32, (T, 1), 0)
    col = lax.broadcasted_iota(i32, (1, T), 1)
    nt = (((1,), (1,)), ((), ()))
    tn = (((0,), (0,)), ((), ()))
    for h in range(HGRN_HEADS):
        sl = slice(h * HGRN_DK, (h + 1) * HGRN_DK)
        q_h, k_h, v_h = qq[:, sl], kk[:, sl], hv[:, sl]
        vb = v_h.astype(bf16)
        cum = e_all[0:T, sl]
        suf = e_all[T:2 * T, sl]
        st = st_ref[h]
        o = lax.dot_general((q_h * jnp.exp(cum)).astype(bf16), st.astype(bf16), nt,
                            preferred_element_type=f32)
        o = o + jnp.sum(q_h * k_h, axis=-1, keepdims=True) * v_h
        scores = jnp.zeros((T, T), f32)
        b = T // 2
        lvl = 2
        while b >= 1:
            xdec = jnp.exp(e_all[lvl * T:(lvl + 1) * T, sl])
            late = ((row // b) % 2) == 1
            a_m = jnp.where(late, q_h * xdec, 0.0).astype(bf16)
            b_m = jnp.where(late, 0.0, k_h * xdec).astype(bf16)
            sc = lax.dot_general(a_m, b_m, nt, preferred_element_type=f32)
            if 2 * b < T:
                sc = jnp.where((row // (2 * b)) == (col // (2 * b)), sc, 0.0)
            scores = scores + sc
            b //= 2
            lvl += 1
        o = o + jnp.dot(scores.astype(bf16), vb, preferred_element_type=f32)
        kd = (k_h * jnp.exp(suf)).astype(bf16)
        st_ref[h] = (st * jnp.exp(cum[T - 1:T, :])
                     + lax.dot_general(vb, kd, tn, preferred_element_type=f32))
        o = o * lax.rsqrt(jnp.mean(o * o, axis=-1, keepdims=True) + LN_EPS) * g_ref[...]
        y_ref[0, :, sl] = (o * gate[:, sl]).astype(bf16)

    st_out_ref[0] = st_ref[...]


def _hgrn(x, w_h, lb, norm_g, s0):
    B, S, _ = x.shape
    T = _row_tile(S, 2 * CHUNK)
    mats = jnp.asarray(_hgrn_mats(T), bf16)
    nm = mats.shape[0]
    st0 = jnp.swapaxes(s0.astype(f32), 2, 3)
    y, st = pl.pallas_call(
        functools.partial(_hgrn_kernel, T=T),
        grid=(B, S // T),
        in_specs=[pl.BlockSpec((1, T, D_MODEL), lambda b, t: (b, t, 0)),
                  _const_spec((D_MODEL, 4 * HGRN_W)),
                  _const_spec((1, HGRN_W)),
                  _const_spec((1, HGRN_DV)),
                  _const_spec((nm, T)),
                  pl.BlockSpec((1, HGRN_HEADS, HGRN_DV, HGRN_DK), lambda b, t: (b, 0, 0, 0))],
        out_specs=[pl.BlockSpec((1, T, HGRN_W), lambda b, t: (b, t, 0)),
                   pl.BlockSpec((1, HGRN_HEADS, HGRN_DV, HGRN_DK), lambda b, t: (b, 0, 0, 0))],
        out_shape=[jax.ShapeDtypeStruct((B, S, HGRN_W), bf16),
                   jax.ShapeDtypeStruct((B, HGRN_HEADS, HGRN_DV, HGRN_DK), f32)],
        scratch_shapes=[pltpu.VMEM((HGRN_HEADS, HGRN_DV, HGRN_DK), f32)],
        compiler_params=_params(("parallel", "arbitrary")),
        name="hgrn",
    )(x, w_h, lb, norm_g, mats, st0)
    return y, jnp.swapaxes(st, 2, 3)


HIST = CONV_W - 1
PADROWS = 8


def _conv_rows(buf_ref, cols, u, w, bias, T):
    buf_ref[PADROWS:PADROWS + T, cols] = u
    y = bias
    for j in range(CONV_W):
        y = y + w[j:j + 1, :] * buf_ref[PADROWS - HIST + j:PADROWS - HIST + j + T, cols]
    buf_ref[PADROWS - HIST:PADROWS, cols] = u[T - HIST:T, :]
    return y


def _merge_kernel(x_ref, ya_ref, yb_ref, hist_ref, wc_ref, wg_ref, wbr_ref, wout_ref, cw_ref, cb_ref,
                  g_ref, b_ref, o_ref, hist_out_ref, buf_ref, *, T, alpha):
    @pl.when(pl.program_id(1) == 0)
    def _():
        buf_ref[PADROWS - HIST:PADROWS, :] = hist_ref[0]

    x = x_ref[0]
    xb = x.astype(bf16)
    c = jnp.dot(xb, wc_ref[...], preferred_element_type=f32)
    W = SCONV_WIDTH
    u = c[:, W:2 * W] * c[:, 2 * W:3 * W]
    u_conv = _conv_rows(buf_ref, slice(0, W), u, cw_ref[...], cb_ref[...], T)
    yc = (c[:, 0:W] * u_conv).astype(bf16)
    hist_out_ref[0] = buf_ref[PADROWS - HIST:PADROWS, :]

    merged = None
    for i, yb in enumerate((ya_ref[0], yb_ref[0], yc)):
        gate = _sigmoid(jnp.dot(xb, wg_ref[:, i * D_MODEL:(i + 1) * D_MODEL], preferred_element_type=f32))
        term = gate * jnp.dot(yb, wbr_ref[i], preferred_element_type=f32)
        merged = term if merged is None else merged + term
    out = jnp.dot(merged.astype(bf16), wout_ref[...], preferred_element_type=f32)
    o_ref[0] = _layer_norm(alpha * x + out, g_ref[...], b_ref[...])


def _merge(x, ya, yb, hist, wc, wg, wbr, wout, cw, cb, g, b, alpha):
    B, S, _ = x.shape
    T = _row_tile(S, 256)
    row = lambda w: pl.BlockSpec((1, T, w), lambda bb, t: (bb, t, 0))
    hspec = pl.BlockSpec((1, HIST, SCONV_WIDTH), lambda bb, t: (bb, 0, 0))
    return pl.pallas_call(
        functools.partial(_merge_kernel, T=T, alpha=alpha),
        grid=(B, S // T),
        in_specs=[row(D_MODEL), row(BRANCH_WIDTH), row(BRANCH_WIDTH), hspec,
                  _const_spec((D_MODEL, 3 * SCONV_WIDTH)), _const_spec((D_MODEL, N_BRANCH * D_MODEL)),
                  _const_spec((N_BRANCH, BRANCH_WIDTH, D_MODEL)), _const_spec((D_MODEL, D_MODEL)),
                  _const_spec((CONV_W, SCONV_WIDTH)), _const_spec((1, SCONV_WIDTH)),
                  _const_spec((1, D_MODEL)), _const_spec((1, D_MODEL))],
        out_specs=[row(D_MODEL), hspec],
        out_shape=[jax.ShapeDtypeStruct((B, S, D_MODEL), f32),
                   jax.ShapeDtypeStruct((B, HIST, SCONV_WIDTH), f32)],
        scratch_shapes=[pltpu.VMEM((PADROWS + T, SCONV_WIDTH), f32)],
        compiler_params=_params(("parallel", "arbitrary")),
        name="merge",
    )(x, ya, yb, hist, wc, wg, wbr, wout, cw, cb, g, b)


def _ffn_kernel(x_ref, hist_ref, wup_ref, cw_ref, cb_ref, wdn_ref, g_ref, b_ref, o_ref, hist_out_ref,
                *bufs, T, alpha):
    n_steps = D_FF // FFN_COLS
    col_ranges = [slice(off + c * FFN_COLS, off + (c + 1) * FFN_COLS)
                  for c in range(n_steps) for off in (0, D_FF)]

    @pl.when(pl.program_id(1) == 0)
    def _():
        for buf, cols in zip(bufs, col_ranges):
            buf[PADROWS - HIST:PADROWS, :] = hist_ref[0, :, cols]

    x = x_ref[0]
    xb = x.astype(bf16)

    def up_conv(cidx):
        halves = []
        for half in range(2):
            buf, cols = bufs[2 * cidx + half], col_ranges[2 * cidx + half]
            h = jnp.dot(xb, wup_ref[:, cols], preferred_element_type=f32)
            halves.append(_conv_rows(buf, slice(None), h, cw_ref[:, cols], cb_ref[:, cols], T))
            hist_out_ref[0, :, cols] = buf[PADROWS - HIST:PADROWS, :]
        return halves

    acc = jnp.zeros((T, D_MODEL), f32)
    nxt = up_conv(0)
    for cidx in range(n_steps):
        a_g, b_v = nxt
        if cidx + 1 < n_steps:
            nxt = up_conv(cidx + 1)
        act = (a_g * _sigmoid(a_g) * b_v).astype(bf16)
        acc = acc + jnp.dot(act, wdn_ref[cidx * FFN_COLS:(cidx + 1) * FFN_COLS, :],
                            preferred_element_type=f32)
    o_ref[0] = _layer_norm(alpha * x + acc, g_ref[...], b_ref[...])


def _ffn(x, hist, wup, cw, cb, wdn, g, b, alpha):
    B, S, _ = x.shape
    T = _row_tile(S, 256)
    row = pl.BlockSpec((1, T, D_MODEL), lambda bb, t: (bb, t, 0))
    hspec = pl.BlockSpec((1, HIST, 2 * D_FF), lambda bb, t: (bb, 0, 0))
    return pl.pallas_call(
        functools.partial(_ffn_kernel, T=T, alpha=alpha),
        grid=(B, S // T),
        in_specs=[row, hspec, _const_spec((D_MODEL, 2 * D_FF)), _const_spec((CONV_W, 2 * D_FF)),
                  _const_spec((1, 2 * D_FF)), _const_spec((D_FF, D_MODEL)),
                  _const_spec((1, D_MODEL)), _const_spec((1, D_MODEL))],
        out_specs=[row, hspec],
        out_shape=[jax.ShapeDtypeStruct((B, S, D_MODEL), f32),
                   jax.ShapeDtypeStruct((B, HIST, 2 * D_FF), f32)],
        scratch_shapes=[pltpu.VMEM((PADROWS + T, FFN_COLS), f32)] * (2 * D_FF // FFN_COLS),
        compiler_params=_params(("parallel", "arbitrary")),
        name="ffn",
    )(x, hist, wup, cw, cb, wdn, g, b)


def _cast_kernel(x_ref, o_ref):
    o_ref[...] = x_ref[...].astype(o_ref.dtype)


def _to_bf16(w):
    D, Rw, C = w.shape
    T = _row_tile(Rw, LANES)
    spec = pl.BlockSpec((1, T, C), lambda d, t: (d, t, 0))
    return pl.pallas_call(_cast_kernel, grid=(D, Rw // T), in_specs=[spec], out_specs=spec,
                          out_shape=jax.ShapeDtypeStruct(w.shape, bf16),
                          compiler_params=_params(("parallel", "parallel")), name="to_bf16")(w)


def _layer_weights(l, depth, w_in, hgrn_lb_logits, hgrn_norm_g, sconv_w, sconv_b, w_branch, w_out,
                   ln1_g, ln1_b, w_up, ffn_conv_w, ffn_conv_b, w_down, ln2_g, ln2_b):
    offs = np.concatenate([[0], np.cumsum(IN_SIZES)]).tolist()
    sec = lambda i, j=None: w_in[l][:, offs[i]:offs[(i if j is None else j) + 1]]
    a_q, a_k, a_v, i_q, i_k, i_w = (sec(i) for i in range(6))
    zpad = lambda n: jnp.zeros((D_MODEL, n), w_in.dtype)
    w_row = jnp.concatenate([a_k, i_k, zpad(LANES - IDX_DIM), a_v], axis=1).astype(bf16)
    w_t = jnp.concatenate([a_q, i_q, a_v, i_w, zpad(WT_ROWS - 900)], axis=1).T.astype(bf16)
    lbp = jax.nn.softmax(hgrn_lb_logits.astype(f32), axis=0)
    lb = (jnp.cumsum(lbp, axis=0) - lbp[0])[l].reshape(1, HGRN_W)
    row = lambda a: a.reshape(1, -1).astype(f32)
    return dict(
        w_row=w_row, w_t=w_t, w_h=sec(6, 9).astype(bf16), lb=lb, norm_g=row(hgrn_norm_g[l]),
        w_c=sec(10, 12).astype(bf16), w_g=sec(13).astype(bf16),
        w_br=w_branch[l].astype(bf16), w_out=w_out[l].astype(bf16),
        sconv_w=sconv_w[l].astype(f32), sconv_b=row(sconv_b[l]),
        ln1_g=row(ln1_g[l]), ln1_b=row(ln1_b[l]),
        w_up=w_up[l].astype(bf16), ffn_w=ffn_conv_w[l].astype(f32), ffn_b=row(ffn_conv_b[l]),
        w_down=w_down[l].astype(bf16), ln2_g=row(ln2_g[l]), ln2_b=row(ln2_b[l]))


def _key_tiles(a, TK):
    B, L, C = a.shape
    NT = -(-L // TK)
    a = jnp.pad(a, ((0, 0), (0, NT * TK - L), (0, 0)))
    return a.reshape(B, NT, TK, C)


def _trunk_layer(x, tables, P, k_past, v_past, ki_past, s0, sc_hist, ffn_hist, w, alpha):
    B, S, _ = x.shape
    TK = KEY_TILE
    a = _attn_proj(x, w["w_row"], w["w_t"], *tables)
    k, v, ki = a["k"], a["v"], a["ki"]
    L = P + S
    if k_past is None and S % TK == 0:
        kb = a["kb"].reshape(B, S // TK, TK, -1)
        kib = a["kib"].reshape(B, S // TK, TK, -1)
        vt = a["vt"]
    else:
        kb, kib, v_all = a["kb"], a["kib"], v.astype(bf16)
        if k_past is not None:
            kb = jnp.concatenate([k_past.reshape(B, P, -1).astype(bf16), kb], axis=1)
            kib = jnp.concatenate([ki_past.astype(bf16), kib], axis=1)
            v_all = jnp.concatenate([v_past.reshape(B, P, -1).astype(bf16), v_all], axis=1)
        kb, kib = _key_tiles(kb, TK), _key_tiles(kib, TK)
        vt = _key_tiles(v_all, TK).reshape(B, -1, TK, KV_HEADS, HEAD_DIM).transpose(0, 1, 3, 4, 2)
        ones = jnp.zeros(vt.shape[:3] + (VT_ROWS - HEAD_DIM, TK), bf16).at[:, :, :, 0, :].set(1.0)
        vt = jnp.concatenate([vt, ones], axis=3)
    y_a = _dsa(a["qt"], a["qit"], a["wit"], kb, kib, vt, P, min(TOPK_MAX, L // 4))
    y_b, s_new = _hgrn(x, w["w_h"], w["lb"], w["norm_g"], s0)
    x1, sc_new = _merge(x, y_a, y_b, sc_hist, w["w_c"], w["w_g"], w["w_br"], w["w_out"],
                        w["sconv_w"], w["sconv_b"], w["ln1_g"], w["ln1_b"], alpha)
    x2, ffn_new = _ffn(x1, ffn_hist, w["w_up"], w["ffn_w"], w["ffn_b"], w["w_down"],
                       w["ln2_g"], w["ln2_b"], alpha)
    new = (k.reshape(B, S, KV_HEADS, HEAD_DIM), v.reshape(B, S, KV_HEADS, HEAD_DIM), ki,
           s_new, sc_new, ffn_new)
    return x2, new


def kernel(x_prompt, x_sample, cache_attn_k, cache_attn_v, cache_idx_k, state_hgrn, state_sconv,
           state_ffn_conv, w_in, hgrn_lb_logits, hgrn_norm_g, sconv_w, sconv_b, w_branch, w_out,
           ln1_g, ln1_b, w_up, ffn_conv_w, ffn_conv_b, w_down, ln2_g, ln2_b):
    depth = w_in.shape[0]
    alpha = (2 * depth) ** 0.25
    B, S, _ = x_prompt.shape
    DB, DS, _ = x_sample.shape
    P = cache_attn_k.shape[2]
    cs_p = _rotary_tables(jnp.arange(S))
    cs_s = _rotary_tables(P + jnp.arange(DS))
    xp, xs = x_prompt, x_sample
    st_p = [[] for _ in range(6)]
    st_s = [[] for _ in range(6)]
    w_in_b = _to_bf16(w_in)
    for l in range(depth):
        w = _layer_weights(l, depth, w_in_b, hgrn_lb_logits, hgrn_norm_g, sconv_w, sconv_b, w_branch,
                           w_out, ln1_g, ln1_b, w_up, ffn_conv_w, ffn_conv_b, w_down, ln2_g, ln2_b)
        xp, new_p = _trunk_layer(
            xp, cs_p, 0, None, None, None,
            jnp.zeros((B, HGRN_HEADS, HGRN_DK, HGRN_DV), f32),
            jnp.zeros((B, HIST, SCONV_WIDTH), f32),
            jnp.zeros((B, HIST, 2 * D_FF), f32), w, alpha)
        xs, new_s = _trunk_layer(
            xs, cs_s, P, cache_attn_k[l], cache_attn_v[l], cache_idx_k[l], state_hgrn[l],
            state_sconv[l], state_ffn_conv[l], w, alpha)
        for j in range(6):
            st_p[j].append(new_p[j])
            st_s[j].append(new_s[j])
    outs_p = [jnp.stack(a, axis=0) for a in st_p]
    outs_s = [jnp.stack(a, axis=0) for a in st_s]
    return (xp, xs, *outs_p, *outs_s)
```

```python
import functools

import numpy as np
import jax
import jax.numpy as jnp
from jax import lax
from jax.experimental import pallas as pl
from jax.experimental.pallas import tpu as pltpu

D_MODEL = 1024
CHUNK = 64
N_HEADS = 8
HEAD_DIM = 64
KV_HEADS = 2
GROUPS = N_HEADS // KV_HEADS
IDX_HEADS = 4
IDX_DIM = 64
TOPK_MAX = 256
ROPE_THETA = 500000.0
ATTN_SCALE = HEAD_DIM ** -0.5
IDX_SCALE = (IDX_DIM ** -0.5) * (IDX_HEADS ** -0.5)
NEG = -1e30
HGRN_HEADS = 4
HGRN_DK = 128
HGRN_DV = 128
HGRN_W = HGRN_HEADS * HGRN_DK
SCONV_WIDTH = 512
CONV_W = 3
BRANCH_WIDTH = 512
N_BRANCH = 3
D_FF = 2816
LN_EPS = 1e-5
IN_SIZES = (N_HEADS * HEAD_DIM, KV_HEADS * HEAD_DIM, KV_HEADS * HEAD_DIM,
            IDX_HEADS * IDX_DIM, IDX_DIM, IDX_HEADS,
            HGRN_W, HGRN_W, HGRN_W, HGRN_W,
            SCONV_WIDTH, SCONV_WIDTH, SCONV_WIDTH,
            N_BRANCH * D_MODEL)

LANES = 128
VMEM_LIMIT = 56 * 1024 * 1024
KEY_TILE = 512
QUERY_TILE = 256
VT_ROWS = HEAD_DIM + 16
FFN_COLS = 256
PACK = 4
DIGITS = ((25, 7), (18, 7), (11, 7), (4, 7), (0, 4))
GUARD = np.int32(-0x7F7F7F80)
ONES = np.int32(0x01010101)

f32 = jnp.float32
bf16 = jnp.bfloat16
i32 = jnp.int32


def _row_tile(s, want):
    return want if s % want == 0 else s


def _const_spec(shape):
    nd = len(shape)
    return pl.BlockSpec(shape, lambda *_: (0,) * nd, pipeline_mode=pl.Buffered(1))


def _params(sem):
    return pltpu.CompilerParams(dimension_semantics=sem, vmem_limit_bytes=VMEM_LIMIT)


def _layer_norm(z, g, b):
    mu = jnp.mean(z, axis=-1, keepdims=True)
    d = z - mu
    var = jnp.mean(d * d, axis=-1, keepdims=True)
    return d * lax.rsqrt(var + LN_EPS) * g + b


def _sigmoid(x):
    return 1.0 / (1.0 + jnp.exp(-x))


Q_PRESCALE = ATTN_SCALE * float(np.log2(np.e))
ROT = HEAD_DIM // 4
WR_COLS = 384
WT_ROWS = 912
WI_ROWS = 8


def _attn_proj_kernel(x_ref, wr_ref, wt_ref, cs_ref, cst_ref, k_ref, v_ref, ki_ref, kb_ref, kib_ref,
                      vt_ref, qt_ref, qit_ref, wit_ref, *, T, TQ):
    xb = x_ref[0].astype(bf16)
    pr = jnp.dot(xb, wr_ref[...], preferred_element_type=f32)
    c, sa, sb = cs_ref[0], cs_ref[1], cs_ref[2]

    def rot(xg):
        return xg * c + pltpu.roll(xg, LANES - ROT // 2, 1) * sa + pltpu.roll(xg, ROT // 2, 1) * sb

    k = rot(pr[:, 0:LANES])
    ki = rot(pr[:, LANES:2 * LANES])[:, :IDX_DIM]
    k_ref[0] = k
    kb_ref[0] = k.astype(bf16)
    ki_ref[0] = ki
    kib_ref[0] = ki.astype(bf16)
    v_ref[0] = pr[:, 2 * LANES:3 * LANES]

    nt = (((1,), (1,)), ((), ()))
    pt = lax.dot_general(wt_ref[...], xb, nt, preferred_element_type=f32)
    ct, st = cst_ref[0], cst_ref[1]

    def rot_t(hb):
        lead = hb[0:ROT] * ct + jnp.concatenate([hb[ROT // 2:ROT], hb[0:ROT // 2]], axis=0) * st
        return jnp.concatenate([lead, hb[ROT:HEAD_DIM]], axis=0)

    nqb = T // TQ
    for h in range(N_HEADS):
        hb = (rot_t(pt[h * HEAD_DIM:(h + 1) * HEAD_DIM]) * Q_PRESCALE).astype(bf16)
        g, i = divmod(h, GROUPS)
        for n in range(nqb):
            qt_ref[0, n, g, :, i * TQ:(i + 1) * TQ] = hb[:, n * TQ:(n + 1) * TQ]
    base = N_HEADS * HEAD_DIM
    for h in range(IDX_HEADS):
        hb = rot_t(pt[base + h * IDX_DIM:base + (h + 1) * IDX_DIM]).astype(bf16)
        for n in range(nqb):
            qit_ref[0, n, :, h * TQ:(h + 1) * TQ] = hb[:, n * TQ:(n + 1) * TQ]
    base += IDX_HEADS * IDX_DIM
    ones_rows = jnp.where(lax.broadcasted_iota(i32, (VT_ROWS - HEAD_DIM, T), 0) == 0, 1.0, 0.0)
    for g in range(KV_HEADS):
        vt_ref[0, 0, g, 0:HEAD_DIM, :] = pt[base + g * HEAD_DIM:base + (g + 1) * HEAD_DIM].astype(bf16)
        vt_ref[0, 0, g, HEAD_DIM:VT_ROWS, :] = ones_rows.astype(bf16)
    base += KV_HEADS * HEAD_DIM
    for n in range(nqb):
        wit_ref[0, n] = pt[base:base + WI_ROWS, n * TQ:(n + 1) * TQ]


def _attn_proj(x, w_row, w_t, cs, cst):
    B, S, _ = x.shape
    T = _row_tile(S, KEY_TILE)
    TQ = _row_tile(S, QUERY_TILE)
    nqb = T // TQ
    R = GROUPS * TQ
    row = lambda w: pl.BlockSpec((1, T, w), lambda b, t: (b, t, 0))
    names = ("k", "v", "ki", "kb", "kib", "vt", "qt", "qit", "wit")
    outs = pl.pallas_call(
        functools.partial(_attn_proj_kernel, T=T, TQ=TQ),
        grid=(B, S // T),
        in_specs=[row(D_MODEL), _const_spec((D_MODEL, WR_COLS)), _const_spec((WT_ROWS, D_MODEL)),
                  pl.BlockSpec((3, T, LANES), lambda b, t: (0, t, 0)),
                  pl.BlockSpec((2, ROT, T), lambda b, t: (0, 0, t))],
        out_specs=[row(128), row(128), row(IDX_DIM), row(128), row(IDX_DIM),
                   pl.BlockSpec((1, 1, KV_HEADS, VT_ROWS, T), lambda b, t: (b, t, 0, 0, 0)),
                   pl.BlockSpec((1, nqb, KV_HEADS, HEAD_DIM, R), lambda b, t: (b, t, 0, 0, 0)),
                   pl.BlockSpec((1, nqb, IDX_DIM, IDX_HEADS * TQ), lambda b, t: (b, t, 0, 0)),
                   pl.BlockSpec((1, nqb, WI_ROWS, TQ), lambda b, t: (b, t, 0, 0))],
        out_shape=[jax.ShapeDtypeStruct((B, S, 128), f32),
                   jax.ShapeDtypeStruct((B, S, 128), f32),
                   jax.ShapeDtypeStruct((B, S, IDX_DIM), f32),
                   jax.ShapeDtypeStruct((B, S, 128), bf16),
                   jax.ShapeDtypeStruct((B, S, IDX_DIM), bf16),
                   jax.ShapeDtypeStruct((B, S // T, KV_HEADS, VT_ROWS, T), bf16),
                   jax.ShapeDtypeStruct((B, S // TQ, KV_HEADS, HEAD_DIM, R), bf16),
                   jax.ShapeDtypeStruct((B, S // TQ, IDX_DIM, IDX_HEADS * TQ), bf16),
                   jax.ShapeDtypeStruct((B, S // TQ, WI_ROWS, TQ), f32)],
        compiler_params=_params(("parallel", "parallel")),
        name="attn_proj",
    )(x, w_row, w_t, cs, cst)
    return dict(zip(names, outs))


def _rotary_tables(pos):
    half = ROT // 2
    inv_freq = ROPE_THETA ** (-jnp.arange(half, dtype=f32) / half)
    ang = pos.astype(f32)[:, None] * inv_freq[None, :]
    cos = jnp.cos(ang)
    sin = jnp.sin(ang)
    S = pos.shape[0]
    one = jnp.ones((S, HEAD_DIM - ROT), f32)
    zero = jnp.zeros((S, HEAD_DIM - ROT), f32)
    zh = jnp.zeros((S, half), f32)
    c = jnp.concatenate([cos, cos, one], axis=1)
    sa = jnp.concatenate([-sin, zh, zero], axis=1)
    sb = jnp.concatenate([zh, sin, zero], axis=1)
    tile2 = lambda a: jnp.concatenate([a, a], axis=1)
    cs = jnp.stack([tile2(c), tile2(sa), tile2(sb)], axis=0)
    cst = jnp.stack([jnp.concatenate([cos, cos], axis=1).T,
                     jnp.concatenate([-sin, sin], axis=1).T], axis=0)
    return cs, cst


def _dsa_kernel(qt_ref, qit_ref, wit_ref, ki_ref, k_ref, vt_ref, tri_ref, o_ref,
                s_ref, pk_ref, qpad_ref, m_ref, acc_ref, *bufs, TQ, TK, P, top):
    lm_refs = (bufs[0:KV_HEADS], bufs[KV_HEADS:2 * KV_HEADS])
    p_refs = (bufs[2 * KV_HEADS:3 * KV_HEADS], bufs[3 * KV_HEADS:4 * KV_HEADS])
    qb = pl.program_id(1)
    q0 = P + qb * TQ
    nk = (q0 + TQ + TK - 1) // TK
    n_full = q0 // TK
    NC = TK // LANES
    R = GROUPS * TQ
    q_lim = ((q0 + lax.broadcasted_iota(i32, (1, TQ), 1)) // CHUNK + 1) * CHUNK
    wit = wit_ref[0, 0] * IDX_SCALE
    zero_head = jnp.zeros((HEAD_DIM, R), bf16)
    qpad_ref[0] = jnp.concatenate([qt_ref[0, 0, 0], zero_head], axis=0)
    qpad_ref[1] = jnp.concatenate([zero_head, qt_ref[0, 0, 1]], axis=0)

    TKP = TK // PACK

    def lead_digit(key):
        return (key >> DIGITS[0][0]) + (1 << (DIGITS[0][1] - 1))

    def score_tile(masked, j, carry):
        for c in range(NC):
            rows = slice(c * LANES, (c + 1) * LANES)
            rel = jnp.maximum(jnp.dot(ki_ref[0, j, rows, :], qit_ref[0, 0], preferred_element_type=f32), 0.0)
            score = wit[0:1, :] * rel[:, 0:TQ]
            for h in range(1, IDX_HEADS):
                score = score + wit[h:h + 1, :] * rel[:, h * TQ:(h + 1) * TQ]
            if masked:
                k_pos = j * TK + c * LANES + lax.broadcasted_iota(i32, (LANES, TQ), 0)
                score = jnp.where(k_pos < q_lim, score, NEG)
            bits = pltpu.bitcast(score, i32)
            key = jnp.where(bits >= 0, bits, bits ^ 0x7FFFFFFF)
            s_ref[j, rows, :] = key
            if c == 0:
                pk_ref[j] = lead_digit(key) | GUARD
            else:
                pk_ref[j] = pk_ref[j] | (lead_digit(key) << (8 * c))
        return carry

    lax.fori_loop(0, n_full, functools.partial(score_tile, False), 0)
    lax.fori_loop(n_full, nk, functools.partial(score_tile, True), 0)

    n_chunks = (nk + PACK - 1) // PACK

    def fill_guard(j, carry):
        pk_ref[j] = jnp.full((TKP, TQ), GUARD, i32)
        return carry

    lax.fori_loop(nk, n_chunks * PACK, fill_guard, 0)

    def count_ge(cand):
        cvec = cand * ONES

        def chunk(ci, tot):
            accs = [None] * 4
            for tt in range(PACK):
                for r in range(TKP // 8):
                    slab = pk_ref[ci * PACK + tt, r * 8:(r + 1) * 8, :]
                    hit = lax.shift_right_logical(slab - cvec, 7) & ONES
                    n = (tt * (TKP // 8) + r) % 4
                    accs[n] = hit if accs[n] is None else accs[n] + hit
            a = (accs[0] + accs[1]) + (accs[2] + accs[3])
            return tot + ((a & 0xFF) + ((a >> 8) & 0xFF) + ((a >> 16) & 0xFF)
                          + lax.shift_right_logical(a, 24))

        tot = lax.fori_loop(0, n_chunks, chunk, jnp.zeros((8, TQ), i32))
        return jnp.sum(tot, axis=0, keepdims=True)

    def repack(lo, width, prefix):
        def tile(j, carry):
            for r in range(TKP // 8):
                word = jnp.full((8, TQ), GUARD, i32)
                for q in range(PACK):
                    key = s_ref[j, q * TKP + r * 8:q * TKP + (r + 1) * 8, :]
                    d = jnp.where((key >> (lo + width)) == prefix, (key >> lo) & ((1 << width) - 1), 0)
                    word = word | (d << (8 * q))
                pk_ref[j, r * 8:(r + 1) * 8, :] = word
            return carry
        lax.fori_loop(0, nk, tile, 0)

    base = jnp.zeros((1, TQ), i32)
    bsize = jnp.full((1, TQ), nk * TK, i32)
    prefix = None
    for stage, (lo, width) in enumerate(DIGITS):
        if stage > 0:
            repack(lo, width, prefix)
        need = top - base

        def bit_step(i, st, need=need, width=width):
            v, c_v = st
            cand = v | jnp.left_shift(jnp.int32(1), width - 1 - i)
            c = count_ge(cand)
            ok = c >= need
            return jnp.where(ok, cand, v), jnp.where(ok, c, c_v)

        v, c_v = lax.fori_loop(0, width, bit_step, (jnp.zeros((1, TQ), i32), bsize))
        c_gt = count_ge(v + 1)
        base, bsize = base + c_gt, c_v - c_gt
        prefix = v - (1 << (width - 1)) if stage == 0 else prefix * (1 << width) + v
    t = prefix
    neg_key = np.array(NEG, np.float32).view(np.int32) ^ 0x7FFFFFFF
    any_excess = jnp.max(jnp.where(base + bsize > top, 1, 0)) > 0

    @pl.when(jnp.logical_not(any_excess))
    def _():
        thr = jnp.where(t == neg_key, t + 1, t)

        def fast(j, carry):
            s_ref[j] = pltpu.bitcast(jnp.where(s_ref[j] >= thr, 0.0, NEG), i32)
            return carry
        lax.fori_loop(0, nk, fast, 0)

    @pl.when(any_excess)
    def _():
        need = jnp.where(t == neg_key, 0, top - base).astype(f32)

        def ranked(j, taken):
            keys = s_ref[j]
            for c in range(NC):
                kc = keys[c * LANES:(c + 1) * LANES, :]
                eq = jnp.where(kc == t, 1.0, 0.0)
                rank = taken + jnp.dot(tri_ref[...], eq.astype(bf16), preferred_element_type=f32)
                pick = jnp.where(kc > t, 1.0, jnp.where(rank <= need, eq, 0.0))
                s_ref[j, c * LANES:(c + 1) * LANES, :] = pltpu.bitcast(
                    jnp.where(pick > 0.5, 0.0, NEG), i32)
                taken = taken + jnp.sum(eq, axis=0, keepdims=True)
            return taken
        lax.fori_loop(0, nk, ranked, jnp.zeros((1, TQ), f32))

    m_ref[...] = jnp.full(m_ref.shape, NEG, f32)
    acc_ref[...] = jnp.zeros(acc_ref.shape, f32)

    def tile_step(nxt, cur):
        if cur is not None:
            j, slot, tile_max = cur
            m_old = [m_ref[g] for g in range(KV_HEADS)]
            m_new = [jnp.maximum(m_old[g], tile_max[g]) for g in range(KV_HEADS)]
            for g in range(KV_HEADS):
                m_ref[g] = m_new[g]
        if nxt is not None:
            jn, slot_n = nxt
            jc = jnp.minimum(jn, nk - 1)
        out = []
        for g in range(KV_HEADS):
            cmax = jnp.full((8, R), NEG, f32)
            for c in range(NC):
                rows = slice(c * LANES, (c + 1) * LANES)
                if nxt is not None:
                    bias = pltpu.bitcast(s_ref[jc, rows, :], f32)
                    lm = (jnp.dot(k_ref[0, jc, rows, :], qpad_ref[g], preferred_element_type=f32)
                          + jnp.concatenate([bias] * GROUPS, axis=1))
                    lm_refs[slot_n][g][rows, :] = lm
                    cmax = jnp.maximum(cmax, jnp.max(lm.reshape(LANES // 8, 8, R), axis=0))
                if cur is not None:
                    p_refs[slot][g][rows, :] = jnp.exp2(lm_refs[slot][g][rows, :] - m_new[g]).astype(bf16)
            out.append(jnp.max(cmax, axis=0, keepdims=True))
        if cur is not None:
            for g in range(KV_HEADS):
                acc_ref[g] = (jnp.exp2(m_old[g] - m_new[g]) * acc_ref[g]
                              + jnp.dot(vt_ref[0, j, g], p_refs[slot][g][...], preferred_element_type=f32))
        return tuple(out)

    def tile_pair(i, tile_max):
        nxt_max = tile_step((2 * i + 1, 1), (2 * i, 0, tile_max))
        return tile_step((2 * i + 2, 0), (2 * i + 1, 1, nxt_max))

    last_max = lax.fori_loop(0, nk // 2, tile_pair, tile_step((0, 0), None))

    @pl.when(nk % 2 == 1)
    def _():
        tile_step(None, (nk - 1, 0, last_max))

    for g in range(KV_HEADS):
        o_t = acc_ref[g, 0:HEAD_DIM, :] / acc_ref[g, HEAD_DIM:HEAD_DIM + 1, :]
        for i in range(0, GROUPS, 2):
            pair = jnp.concatenate([o_t[:, i * TQ:(i + 1) * TQ], o_t[:, (i + 1) * TQ:(i + 2) * TQ]], axis=0)
            h = g * GROUPS + i
            o_ref[0, :, h * HEAD_DIM:(h + 2) * HEAD_DIM] = pair.T.astype(bf16)


def _dsa(qt, qit, wit, kb, kib, vt, P, top):
    B, nq = qt.shape[:2]
    R = qt.shape[-1]
    TQ = R // GROUPS
    NT, TK = kb.shape[1:3]
    assert top <= TK and TQ % CHUNK == 0
    tri = jnp.asarray(np.tril(np.ones((LANES, LANES), np.float32)), bf16)
    return pl.pallas_call(
        functools.partial(_dsa_kernel, TQ=TQ, TK=TK, P=P, top=top),
        grid=(B, nq),
        in_specs=[pl.BlockSpec((1, 1, KV_HEADS, HEAD_DIM, R), lambda b, i: (b, i, 0, 0, 0)),
                  pl.BlockSpec((1, 1, IDX_DIM, IDX_HEADS * TQ), lambda b, i: (b, i, 0, 0)),
                  pl.BlockSpec((1, 1, WI_ROWS, TQ), lambda b, i: (b, i, 0, 0)),
                  pl.BlockSpec((1, NT, TK, IDX_DIM), lambda b, i: (b, 0, 0, 0)),
                  pl.BlockSpec((1, NT, TK, KV_HEADS * HEAD_DIM), lambda b, i: (b, 0, 0, 0)),
                  pl.BlockSpec((1, NT, KV_HEADS, VT_ROWS, TK), lambda b, i: (b, 0, 0, 0, 0)),
                  _const_spec((LANES, LANES))],
        out_specs=pl.BlockSpec((1, TQ, N_HEADS * HEAD_DIM), lambda b, i: (b, i, 0)),
        out_shape=jax.ShapeDtypeStruct((B, nq * TQ, N_HEADS * HEAD_DIM), bf16),
        scratch_shapes=[pltpu.VMEM((NT, TK, TQ), i32),
                        pltpu.VMEM((-(-NT // PACK) * PACK, TK // PACK, TQ), i32),
                        pltpu.VMEM((KV_HEADS, KV_HEADS * HEAD_DIM, R), bf16),
                        pltpu.VMEM((KV_HEADS, 1, R), f32),
                        pltpu.VMEM((KV_HEADS, VT_ROWS, R), f32)]
        + [pltpu.VMEM((TK, R), f32)] * (2 * KV_HEADS)
        + [pltpu.VMEM((TK, R), bf16)] * (2 * KV_HEADS),
        compiler_params=_params(("parallel", "arbitrary")),
        name="dsa",
    )(qt, qit, wit, kib, kb, vt, tri)


def _hgrn_mats(T):
    r = np.arange(T)[:, None]
    c = np.arange(T)[None, :]
    mats = [c <= r, c > r]
    b = T // 2
    while b >= 1:
        blk_r = r // (2 * b)
        mid = blk_r * 2 * b + b
        same = blk_r == c // (2 * b)
        late = r >= mid
        mats.append(same & ((late & (c >= mid) & (c <= r)) | (~late & (c > r) & (c < mid))))
        b //= 2
    return np.concatenate(mats, axis=0).astype(np.float32)


def _hgrn_kernel(x_ref, w_ref, lb_ref, g_ref, mats_ref, s0_ref, y_ref, st_out_ref, st_ref, *, T):
    t_idx = pl.program_id(1)

    @pl.when(t_idx == 0)
    def _():
        st_ref[...] = s0_ref[0]

    xb = x_ref[0].astype(bf16)
    proj = jnp.dot(xb, w_ref[...], preferred_element_type=f32)
    hq = proj[:, 0:HGRN_W]
    z = proj[:, HGRN_W:2 * HGRN_W]
    hv = proj[:, 2 * HGRN_W:3 * HGRN_W]
    hg = proj[:, 3 * HGRN_W:4 * HGRN_W]
    lb = lb_ref[...]
    logf = (jnp.minimum(z, 0.0) - jnp.log1p(jnp.exp(-jnp.abs(z)))) + jnp.log1p(lb * jnp.exp(-z))
    kk = (1.0 - lb) * _sigmoid(-z)
    qq = hq * _sigmoid(hq)
    gate = hg * _sigmoid(hg)

    hi = logf.astype(bf16)
    r1 = logf - hi.astype(f32)
    mid = r1.astype(bf16)
    lo = (r1 - mid.astype(f32)).astype(bf16)
    mats = mats_ref[...]
    e_all = (jnp.dot(mats, hi, preferred_element_type=f32)
             + jnp.dot(mats, mid, preferred_element_type=f32)
             + jnp.dot(mats, lo, preferred_element_type=f32))

    row = lax.broadcasted_iota(i32, (T, 1), 0)
    col = lax.broadcasted_iota(i32, (1, T), 1)
    nt = (((1,), (1,)), ((), ()))
    tn = (((0,), (0,)), ((), ()))
    for h in range(HGRN_HEADS):
        sl = slice(h * HGRN_DK, (h + 1) * HGRN_DK)
        q_h, k_h, v_h = qq[:, sl], kk[:, sl], hv[:, sl]
        vb = v_h.astype(bf16)
        cum = e_all[0:T, sl]
        suf = e_all[T:2 * T, sl]
        st = st_ref[h]
        o = lax.dot_general((q_h * jnp.exp(cum)).astype(bf16), st.astype(bf16), nt,
                            preferred_element_type=f32)
        o = o + jnp.sum(q_h * k_h, axis=-1, keepdims=True) * v_h
        scores = jnp.zeros((T, T), f32)
        b = T // 2
        lvl = 2
        while b >= 1:
            xdec = jnp.exp(e_all[lvl * T:(lvl + 1) * T, sl])
            late = ((row // b) % 2) == 1
            a_m = jnp.where(late, q_h * xdec, 0.0).astype(bf16)
            b_m = jnp.where(late, 0.0, k_h * xdec).astype(bf16)
            sc = lax.dot_general(a_m, b_m, nt, preferred_element_type=f32)
            if 2 * b < T:
                sc = jnp.where((row // (2 * b)) == (col // (2 * b)), sc, 0.0)
            scores = scores + sc
            b //= 2
            lvl += 1
        o = o + jnp.dot(scores.astype(bf16), vb, preferred_element_type=f32)
        kd = (k_h * jnp.exp(suf)).astype(bf16)
        st_ref[h] = (st * jnp.exp(cum[T - 1:T, :])
                     + lax.dot_general(vb, kd, tn, preferred_element_type=f32))
        o = o * lax.rsqrt(jnp.mean(o * o, axis=-1, keepdims=True) + LN_EPS) * g_ref[...]
        y_ref[0, :, sl] = (o * gate[:, sl]).astype(bf16)

    st_out_ref[0] = st_ref[...]


def _hgrn(x, w_h, lb, norm_g, s0):
    B, S, _ = x.shape
    T = _row_tile(S, 2 * CHUNK)
    mats = jnp.asarray(_hgrn_mats(T), bf16)
    nm = mats.shape[0]
    st0 = jnp.swapaxes(s0.astype(f32), 2, 3)
    y, st = pl.pallas_call(
        functools.partial(_hgrn_kernel, T=T),
        grid=(B, S // T),
        in_specs=[pl.BlockSpec((1, T, D_MODEL), lambda b, t: (b, t, 0)),
                  _const_spec((D_MODEL, 4 * HGRN_W)),
                  _const_spec((1, HGRN_W)),
                  _const_spec((1, HGRN_DV)),
                  _const_spec((nm, T)),
                  pl.BlockSpec((1, HGRN_HEADS, HGRN_DV, HGRN_DK), lambda b, t: (b, 0, 0, 0))],
        out_specs=[pl.BlockSpec((1, T, HGRN_W), lambda b, t: (b, t, 0)),
                   pl.BlockSpec((1, HGRN_HEADS, HGRN_DV, HGRN_DK), lambda b, t: (b, 0, 0, 0))],
        out_shape=[jax.ShapeDtypeStruct((B, S, HGRN_W), bf16),
                   jax.ShapeDtypeStruct((B, HGRN_HEADS, HGRN_DV, HGRN_DK), f32)],
        scratch_shapes=[pltpu.VMEM((HGRN_HEADS, HGRN_DV, HGRN_DK), f32)],
        compiler_params=_params(("parallel", "arbitrary")),
        name="hgrn",
    )(x, w_h, lb, norm_g, mats, st0)
    return y, jnp.swapaxes(st, 2, 3)


HIST = CONV_W - 1
PADROWS = 8


def _conv_rows(buf_ref, cols, u, w, bias, T):
    buf_ref[PADROWS:PADROWS + T, cols] = u
    y = bias
    for j in range(CONV_W):
        y = y + w[j:j + 1, :] * buf_ref[PADROWS - HIST + j:PADROWS - HIST + j + T, cols]
    buf_ref[PADROWS - HIST:PADROWS, cols] = u[T - HIST:T, :]
    return y


def _merge_kernel(x_ref, ya_ref, yb_ref, hist_ref, wc_ref, wg_ref, wbr_ref, wout_ref, cw_ref, cb_ref,
                  g_ref, b_ref, o_ref, hist_out_ref, buf_ref, *, T, alpha):
    @pl.when(pl.program_id(1) == 0)
    def _():
        buf_ref[PADROWS - HIST:PADROWS, :] = hist_ref[0]

    x = x_ref[0]
    xb = x.astype(bf16)
    c = jnp.dot(xb, wc_ref[...], preferred_element_type=f32)
    W = SCONV_WIDTH
    u = c[:, W:2 * W] * c[:, 2 * W:3 * W]
    u_conv = _conv_rows(buf_ref, slice(0, W), u, cw_ref[...], cb_ref[...], T)
    yc = (c[:, 0:W] * u_conv).astype(bf16)
    hist_out_ref[0] = buf_ref[PADROWS - HIST:PADROWS, :]

    merged = None
    for i, yb in enumerate((ya_ref[0], yb_ref[0], yc)):
        gate = _sigmoid(jnp.dot(xb, wg_ref[:, i * D_MODEL:(i + 1) * D_MODEL], preferred_element_type=f32))
        term = gate * jnp.dot(yb, wbr_ref[i], preferred_element_type=f32)
        merged = term if merged is None else merged + term
    out = jnp.dot(merged.astype(bf16), wout_ref[...], preferred_element_type=f32)
    o_ref[0] = _layer_norm(alpha * x + out, g_ref[...], b_ref[...])


def _merge(x, ya, yb, hist, wc, wg, wbr, wout, cw, cb, g, b, alpha):
    B, S, _ = x.shape
    T = _row_tile(S, 256)
    row = lambda w: pl.BlockSpec((1, T, w), lambda bb, t: (bb, t, 0))
    hspec = pl.BlockSpec((1, HIST, SCONV_WIDTH), lambda bb, t: (bb, 0, 0))
    return pl.pallas_call(
        functools.partial(_merge_kernel, T=T, alpha=alpha),
        grid=(B, S // T),
        in_specs=[row(D_MODEL), row(BRANCH_WIDTH), row(BRANCH_WIDTH), hspec,
                  _const_spec((D_MODEL, 3 * SCONV_WIDTH)), _const_spec((D_MODEL, N_BRANCH * D_MODEL)),
                  _const_spec((N_BRANCH, BRANCH_WIDTH, D_MODEL)), _const_spec((D_MODEL, D_MODEL)),
                  _const_spec((CONV_W, SCONV_WIDTH)), _const_spec((1, SCONV_WIDTH)),
                  _const_spec((1, D_MODEL)), _const_spec((1, D_MODEL))],
        out_specs=[row(D_MODEL), hspec],
        out_shape=[jax.ShapeDtypeStruct((B, S, D_MODEL), f32),
                   jax.ShapeDtypeStruct((B, HIST, SCONV_WIDTH), f32)],
        scratch_shapes=[pltpu.VMEM((PADROWS + T, SCONV_WIDTH), f32)],
        compiler_params=_params(("parallel", "arbitrary")),
        name="merge",
    )(x, ya, yb, hist, wc, wg, wbr, wout, cw, cb, g, b)


def _ffn_kernel(x_ref, hist_ref, wup_ref, cw_ref, cb_ref, wdn_ref, g_ref, b_ref, o_ref, hist_out_ref,
                *bufs, T, alpha):
    n_steps = D_FF // FFN_COLS
    col_ranges = [slice(off + c * FFN_COLS, off + (c + 1) * FFN_COLS)
                  for c in range(n_steps) for off in (0, D_FF)]

    @pl.when(pl.program_id(1) == 0)
    def _():
        for buf, cols in zip(bufs, col_ranges):
            buf[PADROWS - HIST:PADROWS, :] = hist_ref[0, :, cols]

    x = x_ref[0]
    xb = x.astype(bf16)

    def up_conv(cidx):
        halves = []
        for half in range(2):
            buf, cols = bufs[2 * cidx + half], col_ranges[2 * cidx + half]
            h = jnp.dot(xb, wup_ref[:, cols], preferred_element_type=f32)
            halves.append(_conv_rows(buf, slice(None), h, cw_ref[:, cols], cb_ref[:, cols], T))
            hist_out_ref[0, :, cols] = buf[PADROWS - HIST:PADROWS, :]
        return halves

    acc = jnp.zeros((T, D_MODEL), f32)
    nxt = up_conv(0)
    for cidx in range(n_steps):
        a_g, b_v = nxt
        if cidx + 1 < n_steps:
            nxt = up_conv(cidx + 1)
        act = (a_g * _sigmoid(a_g) * b_v).astype(bf16)
        acc = acc + jnp.dot(act, wdn_ref[cidx * FFN_COLS:(cidx + 1) * FFN_COLS, :],
                            preferred_element_type=f32)
    o_ref[0] = _layer_norm(alpha * x + acc, g_ref[...], b_ref[...])


def _ffn(x, hist, wup, cw, cb, wdn, g, b, alpha):
    B, S, _ = x.shape
    T = _row_tile(S, 256)
    row = pl.BlockSpec((1, T, D_MODEL), lambda bb, t: (bb, t, 0))
    hspec = pl.BlockSpec((1, HIST, 2 * D_FF), lambda bb, t: (bb, 0, 0))
    return pl.pallas_call(
        functools.partial(_ffn_kernel, T=T, alpha=alpha),
        grid=(B, S // T),
        in_specs=[row, hspec, _const_spec((D_MODEL, 2 * D_FF)), _const_spec((CONV_W, 2 * D_FF)),
                  _const_spec((1, 2 * D_FF)), _const_spec((D_FF, D_MODEL)),
                  _const_spec((1, D_MODEL)), _const_spec((1, D_MODEL))],
        out_specs=[row, hspec],
        out_shape=[jax.ShapeDtypeStruct((B, S, D_MODEL), f32),
                   jax.ShapeDtypeStruct((B, HIST, 2 * D_FF), f32)],
        scratch_shapes=[pltpu.VMEM((PADROWS + T, FFN_COLS), f32)] * (2 * D_FF // FFN_COLS),
        compiler_params=_params(("parallel", "arbitrary")),
        name="ffn",
    )(x, hist, wup, cw, cb, wdn, g, b)


def _cast_kernel(x_ref, o_ref):
    o_ref[...] = x_ref[...].astype(o_ref.dtype)


def _to_bf16(w):
    D, Rw, C = w.shape
    T = _row_tile(Rw, LANES)
    spec = pl.BlockSpec((1, T, C), lambda d, t: (d, t, 0))
    return pl.pallas_call(_cast_kernel, grid=(D, Rw // T), in_specs=[spec], out_specs=spec,
                          out_shape=jax.ShapeDtypeStruct(w.shape, bf16),
                          compiler_params=_params(("parallel", "parallel")), name="to_bf16")(w)


def _layer_weights(l, depth, w_in, hgrn_lb_logits, hgrn_norm_g, sconv_w, sconv_b, w_branch, w_out,
                   ln1_g, ln1_b, w_up, ffn_conv_w, ffn_conv_b, w_down, ln2_g, ln2_b):
    offs = np.concatenate([[0], np.cumsum(IN_SIZES)]).tolist()
    sec = lambda i, j=None: w_in[l][:, offs[i]:offs[(i if j is None else j) + 1]]
    a_q, a_k, a_v, i_q, i_k, i_w = (sec(i) for i in range(6))
    zpad = lambda n: jnp.zeros((D_MODEL, n), w_in.dtype)
    w_row = jnp.concatenate([a_k, i_k, zpad(LANES - IDX_DIM), a_v], axis=1).astype(bf16)
    w_t = jnp.concatenate([a_q, i_q, a_v, i_w, zpad(WT_ROWS - 900)], axis=1).T.astype(bf16)
    lbp = jax.nn.softmax(hgrn_lb_logits.astype(f32), axis=0)
    lb = (jnp.cumsum(lbp, axis=0) - lbp[0])[l].reshape(1, HGRN_W)
    row = lambda a: a.reshape(1, -1).astype(f32)
    return dict(
        w_row=w_row, w_t=w_t, w_h=sec(6, 9).astype(bf16), lb=lb, norm_g=row(hgrn_norm_g[l]),
        w_c=sec(10, 12).astype(bf16), w_g=sec(13).astype(bf16),
        w_br=w_branch[l].astype(bf16), w_out=w_out[l].astype(bf16),
        sconv_w=sconv_w[l].astype(f32), sconv_b=row(sconv_b[l]),
        ln1_g=row(ln1_g[l]), ln1_b=row(ln1_b[l]),
        w_up=w_up[l].astype(bf16), ffn_w=ffn_conv_w[l].astype(f32), ffn_b=row(ffn_conv_b[l]),
        w_down=w_down[l].astype(bf16), ln2_g=row(ln2_g[l]), ln2_b=row(ln2_b[l]))


def _key_tiles(a, TK):
    B, L, C = a.shape
    NT = -(-L // TK)
    a = jnp.pad(a, ((0, 0), (0, NT * TK - L), (0, 0)))
    return a.reshape(B, NT, TK, C)


def _trunk_layer(x, tables, P, k_past, v_past, ki_past, s0, sc_hist, ffn_hist, w, alpha):
    B, S, _ = x.shape
    TK = KEY_TILE
    a = _attn_proj(x, w["w_row"], w["w_t"], *tables)
    k, v, ki = a["k"], a["v"], a["ki"]
    L = P + S
    if k_past is None and S % TK == 0:
        kb = a["kb"].reshape(B, S // TK, TK, -1)
        kib = a["kib"].reshape(B, S // TK, TK, -1)
        vt = a["vt"]
    else:
        kb, kib, v_all = a["kb"], a["kib"], v.astype(bf16)
        if k_past is not None:
            kb = jnp.concatenate([k_past.reshape(B, P, -1).astype(bf16), kb], axis=1)
            kib = jnp.concatenate([ki_past.astype(bf16), kib], axis=1)
            v_all = jnp.concatenate([v_past.reshape(B, P, -1).astype(bf16), v_all], axis=1)
        kb, kib = _key_tiles(kb, TK), _key_tiles(kib, TK)
        vt = _key_tiles(v_all, TK).reshape(B, -1, TK, KV_HEADS, HEAD_DIM).transpose(0, 1, 3, 4, 2)
        ones = jnp.zeros(vt.shape[:3] + (VT_ROWS - HEAD_DIM, TK), bf16).at[:, :, :, 0, :].set(1.0)
        vt = jnp.concatenate([vt, ones], axis=3)
    y_a = _dsa(a["qt"], a["qit"], a["wit"], kb, kib, vt, P, min(TOPK_MAX, L // 4))
    y_b, s_new = _hgrn(x, w["w_h"], w["lb"], w["norm_g"], s0)
    x1, sc_new = _merge(x, y_a, y_b, sc_hist, w["w_c"], w["w_g"], w["w_br"], w["w_out"],
                        w["sconv_w"], w["sconv_b"], w["ln1_g"], w["ln1_b"], alpha)
    x2, ffn_new = _ffn(x1, ffn_hist, w["w_up"], w["ffn_w"], w["ffn_b"], w["w_down"],
                       w["ln2_g"], w["ln2_b"], alpha)
    new = (k.reshape(B, S, KV_HEADS, HEAD_DIM), v.reshape(B, S, KV_HEADS, HEAD_DIM), ki,
           s_new, sc_new, ffn_new)
    return x2, new


def kernel(x_prompt, x_sample, cache_attn_k, cache_attn_v, cache_idx_k, state_hgrn, state_sconv,
           state_ffn_conv, w_in, hgrn_lb_logits, hgrn_norm_g, sconv_w, sconv_b, w_branch, w_out,
           ln1_g, ln1_b, w_up, ffn_conv_w, ffn_conv_b, w_down, ln2_g, ln2_b):
    depth = w_in.shape[0]
    alpha = (2 * depth) ** 0.25
    B, S, _ = x_prompt.shape
    DB, DS, _ = x_sample.shape
    P = cache_attn_k.shape[2]
    cs_p = _rotary_tables(jnp.arange(S))
    cs_s = _rotary_tables(P + jnp.arange(DS))
    xp, xs = x_prompt, x_sample
    st_p = [[] for _ in range(6)]
    st_s = [[] for _ in range(6)]
    w_in_b = _to_bf16(w_in)
    for l in range(depth):
        w = _layer_weights(l, depth, w_in_b, hgrn_lb_logits, hgrn_norm_g, sconv_w, sconv_b, w_branch,
                           w_out, ln1_g, ln1_b, w_up, ffn_conv_w, ffn_conv_b, w_down, ln2_g, ln2_b)
        xp, new_p = _trunk_layer(
            xp, cs_p, 0, None, None, None,
            jnp.zeros((B, HGRN_HEADS, HGRN_DK, HGRN_DV), f32),
            jnp.zeros((B, HIST, SCONV_WIDTH), f32),
            jnp.zeros((B, HIST, 2 * D_FF), f32), w, alpha)
        xs, new_s = _trunk_layer(
            xs, cs_s, P, cache_attn_k[l], cache_attn_v[l], cache_idx_k[l], state_hgrn[l],
            state_sconv[l], state_ffn_conv[l], w, alpha)
        for j in range(6):
            st_p[j].append(new_p[j])
            st_s[j].append(new_s[j])
    outs_p = [jnp.stack(a, axis=0) for a in st_p]
    outs_s = [jnp.stack(a, axis=0) for a in st_s]
    return (xp, xs, *outs_p, *outs_s)
```

```python
import functools

import numpy as np
import jax
import jax.numpy as jnp
from jax import lax
from jax.experimental import pallas as pl
from jax.experimental.pallas import tpu as pltpu

D_MODEL = 1024
CHUNK = 64
N_HEADS = 8
HEAD_DIM = 64
KV_HEADS = 2
GROUPS = N_HEADS // KV_HEADS
IDX_HEADS = 4
IDX_DIM = 64
TOPK_MAX = 256
ROPE_THETA = 500000.0
ATTN_SCALE = HEAD_DIM ** -0.5
IDX_SCALE = (IDX_DIM ** -0.5) * (IDX_HEADS ** -0.5)
NEG = -1e30
HGRN_HEADS = 4
HGRN_DK = 128
HGRN_DV = 128
HGRN_W = HGRN_HEADS * HGRN_DK
SCONV_WIDTH = 512
CONV_W = 3
BRANCH_WIDTH = 512
N_BRANCH = 3
D_FF = 2816
LN_EPS = 1e-5
IN_SIZES = (N_HEADS * HEAD_DIM, KV_HEADS * HEAD_DIM, KV_HEADS * HEAD_DIM,
            IDX_HEADS * IDX_DIM, IDX_DIM, IDX_HEADS,
            HGRN_W, HGRN_W, HGRN_W, HGRN_W,
            SCONV_WIDTH, SCONV_WIDTH, SCONV_WIDTH,
            N_BRANCH * D_MODEL)

LANES = 128
VMEM_LIMIT = 56 * 1024 * 1024
KEY_TILE = 512
QUERY_TILE = 256
VT_ROWS = HEAD_DIM + 16
FFN_COLS = 256
PACK = 4
DIGITS = ((25, 7), (18, 7), (11, 7), (4, 7), (0, 4))
GUARD = np.int32(-0x7F7F7F80)
ONES = np.int32(0x01010101)

f32 = jnp.float32
bf16 = jnp.bfloat16
i32 = jnp.int32


def _row_tile(s, want):
    return want if s % want == 0 else s


def _const_spec(shape):
    nd = len(shape)
    return pl.BlockSpec(shape, lambda *_: (0,) * nd, pipeline_mode=pl.Buffered(1))


def _params(sem):
    return pltpu.CompilerParams(dimension_semantics=sem, vmem_limit_bytes=VMEM_LIMIT)


def _layer_norm(z, g, b):
    mu = jnp.mean(z, axis=-1, keepdims=True)
    d = z - mu
    var = jnp.mean(d * d, axis=-1, keepdims=True)
    return d * lax.rsqrt(var + LN_EPS) * g + b


def _sigmoid(x):
    return 1.0 / (1.0 + jnp.exp(-x))


Q_PRESCALE = ATTN_SCALE * float(np.log2(np.e))
ROT = HEAD_DIM // 4
WR_COLS = 384
WT_ROWS = 912
WI_ROWS = 8


def _attn_proj_kernel(x_ref, wr_ref, wt_ref, cs_ref, cst_ref, k_ref, v_ref, ki_ref, kb_ref, kib_ref,
                      vt_ref, qt_ref, qit_ref, wit_ref, *, T, TQ):
    xb = x_ref[0].astype(bf16)
    pr = jnp.dot(xb, wr_ref[...], preferred_element_type=f32)
    c, sa, sb = cs_ref[0], cs_ref[1], cs_ref[2]

    def rot(xg):
        return xg * c + pltpu.roll(xg, LANES - ROT // 2, 1) * sa + pltpu.roll(xg, ROT // 2, 1) * sb

    k = rot(pr[:, 0:LANES])
    ki = rot(pr[:, LANES:2 * LANES])[:, :IDX_DIM]
    k_ref[0] = k
    kb_ref[0] = k.astype(bf16)
    ki_ref[0] = ki
    kib_ref[0] = ki.astype(bf16)
    v_ref[0] = pr[:, 2 * LANES:3 * LANES]

    nt = (((1,), (1,)), ((), ()))
    pt = lax.dot_general(wt_ref[...], xb, nt, preferred_element_type=f32)
    ct, st = cst_ref[0], cst_ref[1]

    def rot_t(hb):
        lead = hb[0:ROT] * ct + jnp.concatenate([hb[ROT // 2:ROT], hb[0:ROT // 2]], axis=0) * st
        return jnp.concatenate([lead, hb[ROT:HEAD_DIM]], axis=0)

    nqb = T // TQ
    for h in range(N_HEADS):
        hb = (rot_t(pt[h * HEAD_DIM:(h + 1) * HEAD_DIM]) * Q_PRESCALE).astype(bf16)
        g, i = divmod(h, GROUPS)
        for n in range(nqb):
            qt_ref[0, n, g, :, i * TQ:(i + 1) * TQ] = hb[:, n * TQ:(n + 1) * TQ]
    base = N_HEADS * HEAD_DIM
    for h in range(IDX_HEADS):
        hb = rot_t(pt[base + h * IDX_DIM:base + (h + 1) * IDX_DIM]).astype(bf16)
        for n in range(nqb):
            qit_ref[0, n, :, h * TQ:(h + 1) * TQ] = hb[:, n * TQ:(n + 1) * TQ]
    base += IDX_HEADS * IDX_DIM
    ones_rows = jnp.where(lax.broadcasted_iota(i32, (VT_ROWS - HEAD_DIM, T), 0) == 0, 1.0, 0.0)
    for g in range(KV_HEADS):
        vt_ref[0, 0, g, 0:HEAD_DIM, :] = pt[base + g * HEAD_DIM:base + (g + 1) * HEAD_DIM].astype(bf16)
        vt_ref[0, 0, g, HEAD_DIM:VT_ROWS, :] = ones_rows.astype(bf16)
    base += KV_HEADS * HEAD_DIM
    for n in range(nqb):
        wit_ref[0, n] = pt[base:base + WI_ROWS, n * TQ:(n + 1) * TQ]


def _attn_proj(x, w_row, w_t, cs, cst):
    B, S, _ = x.shape
    T = _row_tile(S, KEY_TILE)
    TQ = _row_tile(S, QUERY_TILE)
    nqb = T // TQ
    R = GROUPS * TQ
    row = lambda w: pl.BlockSpec((1, T, w), lambda b, t: (b, t, 0))
    names = ("k", "v", "ki", "kb", "kib", "vt", "qt", "qit", "wit")
    outs = pl.pallas_call(
        functools.partial(_attn_proj_kernel, T=T, TQ=TQ),
        grid=(B, S // T),
        in_specs=[row(D_MODEL), _const_spec((D_MODEL, WR_COLS)), _const_spec((WT_ROWS, D_MODEL)),
                  pl.BlockSpec((3, T, LANES), lambda b, t: (0, t, 0)),
                  pl.BlockSpec((2, ROT, T), lambda b, t: (0, 0, t))],
        out_specs=[row(128), row(128), row(IDX_DIM), row(128), row(IDX_DIM),
                   pl.BlockSpec((1, 1, KV_HEADS, VT_ROWS, T), lambda b, t: (b, t, 0, 0, 0)),
                   pl.BlockSpec((1, nqb, KV_HEADS, HEAD_DIM, R), lambda b, t: (b, t, 0, 0, 0)),
                   pl.BlockSpec((1, nqb, IDX_DIM, IDX_HEADS * TQ), lambda b, t: (b, t, 0, 0)),
                   pl.BlockSpec((1, nqb, WI_ROWS, TQ), lambda b, t: (b, t, 0, 0))],
        out_shape=[jax.ShapeDtypeStruct((B, S, 128), f32),
                   jax.ShapeDtypeStruct((B, S, 128), f32),
                   jax.ShapeDtypeStruct((B, S, IDX_DIM), f32),
                   jax.ShapeDtypeStruct((B, S, 128), bf16),
                   jax.ShapeDtypeStruct((B, S, IDX_DIM), bf16),
                   jax.ShapeDtypeStruct((B, S // T, KV_HEADS, VT_ROWS, T), bf16),
                   jax.ShapeDtypeStruct((B, S // TQ, KV_HEADS, HEAD_DIM, R), bf16),
                   jax.ShapeDtypeStruct((B, S // TQ, IDX_DIM, IDX_HEADS * TQ), bf16),
                   jax.ShapeDtypeStruct((B, S // TQ, WI_ROWS, TQ), f32)],
        compiler_params=_params(("parallel", "parallel")),
        name="attn_proj",
    )(x, w_row, w_t, cs, cst)
    return dict(zip(names, outs))


def _rotary_tables(pos):
    half = ROT // 2
    inv_freq = ROPE_THETA ** (-jnp.arange(half, dtype=f32) / half)
    ang = pos.astype(f32)[:, None] * inv_freq[None, :]
    cos = jnp.cos(ang)
    sin = jnp.sin(ang)
    S = pos.shape[0]
    one = jnp.ones((S, HEAD_DIM - ROT), f32)
    zero = jnp.zeros((S, HEAD_DIM - ROT), f32)
    zh = jnp.zeros((S, half), f32)
    c = jnp.concatenate([cos, cos, one], axis=1)
    sa = jnp.concatenate([-sin, zh, zero], axis=1)
    sb = jnp.concatenate([zh, sin, zero], axis=1)
    tile2 = lambda a: jnp.concatenate([a, a], axis=1)
    cs = jnp.stack([tile2(c), tile2(sa), tile2(sb)], axis=0)
    cst = jnp.stack([jnp.concatenate([cos, cos], axis=1).T,
                     jnp.concatenate([-sin, sin], axis=1).T], axis=0)
    return cs, cst


def _dsa_kernel(qt_ref, qit_ref, wit_ref, ki_ref, k_ref, vt_ref, tri_ref, o_ref,
                s_ref, pk_ref, qpad_ref, m_ref, acc_ref, *bufs, TQ, TK, P, top):
    lm_refs = (bufs[0:KV_HEADS], bufs[KV_HEADS:2 * KV_HEADS])
    p_refs = (bufs[2 * KV_HEADS:3 * KV_HEADS], bufs[3 * KV_HEADS:4 * KV_HEADS])
    qb = pl.program_id(1)
    q0 = P + qb * TQ
    nk = (q0 + TQ + TK - 1) // TK
    n_full = q0 // TK
    NC = TK // LANES
    R = GROUPS * TQ
    q_lim = ((q0 + lax.broadcasted_iota(i32, (1, TQ), 1)) // CHUNK + 1) * CHUNK
    wit = wit_ref[0, 0] * IDX_SCALE
    zero_head = jnp.zeros((HEAD_DIM, R), bf16)
    qpad_ref[0] = jnp.concatenate([qt_ref[0, 0, 0], zero_head], axis=0)
    qpad_ref[1] = jnp.concatenate([zero_head, qt_ref[0, 0, 1]], axis=0)

    TKP = TK // PACK

    def lead_digit(key):
        return (key >> DIGITS[0][0]) + (1 << (DIGITS[0][1] - 1))

    def score_tile(masked, j, carry):
        for c in range(NC):
            rows = slice(c * LANES, (c + 1) * LANES)
            rel = jnp.maximum(jnp.dot(ki_ref[0, j, rows, :], qit_ref[0, 0], preferred_element_type=f32), 0.0)
            score = wit[0:1, :] * rel[:, 0:TQ]
            for h in range(1, IDX_HEADS):
                score = score + wit[h:h + 1, :] * rel[:, h * TQ:(h + 1) * TQ]
            if masked:
                k_pos = j * TK + c * LANES + lax.broadcasted_iota(i32, (LANES, TQ), 0)
                score = jnp.where(k_pos < q_lim, score, NEG)
            bits = pltpu.bitcast(score, i32)
            key = jnp.where(bits >= 0, bits, bits ^ 0x7FFFFFFF)
            s_ref[j, rows, :] = key
            if c == 0:
                pk_ref[j] = lead_digit(key) | GUARD
            else:
                pk_ref[j] = pk_ref[j] | (lead_digit(key) << (8 * c))
        return carry

    lax.fori_loop(0, n_full, functools.partial(score_tile, False), 0)
    lax.fori_loop(n_full, nk, functools.partial(score_tile, True), 0)

    n_chunks = (nk + PACK - 1) // PACK

    def fill_guard(j, carry):
        pk_ref[j] = jnp.full((TKP, TQ), GUARD, i32)
        return carry

    lax.fori_loop(nk, n_chunks * PACK, fill_guard, 0)

    def count_ge(cand):
        cvec = cand * ONES

        def chunk(ci, tot):
            accs = [None] * 4
            for tt in range(PACK):
                for r in range(TKP // 8):
                    slab = pk_ref[ci * PACK + tt, r * 8:(r + 1) * 8, :]
                    hit = lax.shift_right_logical(slab - cvec, 7) & ONES
                    n = (tt * (TKP // 8) + r) % 4
                    accs[n] = hit if accs[n] is None else accs[n] + hit
            a = (accs[0] + accs[1]) + (accs[2] + accs[3])
            return tot + ((a & 0xFF) + ((a >> 8) & 0xFF) + ((a >> 16) & 0xFF)
                          + lax.shift_right_logical(a, 24))

        tot = lax.fori_loop(0, n_chunks, chunk, jnp.zeros((8, TQ), i32))
        return jnp.sum(tot, axis=0, keepdims=True)

    def repack(lo, width, prefix):
        origin = prefix * (1 << width)

        def tile(j, carry):
            for r in range(TKP // 8):
                word = jnp.full((8, TQ), GUARD, i32)
                for q in range(PACK):
                    key = s_ref[j, q * TKP + r * 8:q * TKP + (r + 1) * 8, :]
                    d = (key >> lo) - origin
                    in_range = pltpu.bitcast(d, jnp.uint32) < jnp.uint32(1 << width)
                    word = word | (jnp.where(in_range, d, 0) << (8 * q))
                pk_ref[j, r * 8:(r + 1) * 8, :] = word
            return carry
        lax.fori_loop(0, nk, tile, 0)

    base = jnp.zeros((1, TQ), i32)
    bsize = jnp.full((1, TQ), nk * TK, i32)
    prefix = None
    for stage, (lo, width) in enumerate(DIGITS):
        if stage > 0:
            repack(lo, width, prefix)
        need = top - base

        def bit_step(i, st, need=need, width=width):
            v, c_v, c_rej = st
            cand = v | jnp.left_shift(jnp.int32(1), width - 1 - i)
            c = count_ge(cand)
            ok = c >= need
            return jnp.where(ok, cand, v), jnp.where(ok, c, c_v), jnp.where(ok, c_rej, c)

        zero = jnp.zeros((1, TQ), i32)
        v, c_v, c_gt = lax.fori_loop(0, width, bit_step, (zero, bsize, zero))
        base, bsize = base + c_gt, c_v - c_gt
        prefix = v - (1 << (width - 1)) if stage == 0 else prefix * (1 << width) + v
    t = prefix
    neg_key = np.array(NEG, np.float32).view(np.int32) ^ 0x7FFFFFFF
    any_excess = jnp.max(jnp.where(base + bsize > top, 1, 0)) > 0

    @pl.when(jnp.logical_not(any_excess))
    def _():
        thr = jnp.where(t == neg_key, t + 1, t)

        def fast(j, carry):
            s_ref[j] = pltpu.bitcast(jnp.where(s_ref[j] >= thr, 0.0, NEG), i32)
            return carry
        lax.fori_loop(0, nk, fast, 0)

    @pl.when(any_excess)
    def _():
        need = jnp.where(t == neg_key, 0, top - base).astype(f32)

        def ranked(j, taken):
            keys = s_ref[j]
            for c in range(NC):
                kc = keys[c * LANES:(c + 1) * LANES, :]
                eq = jnp.where(kc == t, 1.0, 0.0)
                rank = taken + jnp.dot(tri_ref[...], eq.astype(bf16), preferred_element_type=f32)
                pick = jnp.where(kc > t, 1.0, jnp.where(rank <= need, eq, 0.0))
                s_ref[j, c * LANES:(c + 1) * LANES, :] = pltpu.bitcast(
                    jnp.where(pick > 0.5, 0.0, NEG), i32)
                taken = taken + jnp.sum(eq, axis=0, keepdims=True)
            return taken
        lax.fori_loop(0, nk, ranked, jnp.zeros((1, TQ), f32))

    m_ref[...] = jnp.full(m_ref.shape, NEG, f32)
    acc_ref[...] = jnp.zeros(acc_ref.shape, f32)

    def tile_step(nxt, cur):
        if cur is not None:
            j, slot, tile_max = cur
            m_old = [m_ref[g] for g in range(KV_HEADS)]
            m_new = [jnp.maximum(m_old[g], tile_max[g]) for g in range(KV_HEADS)]
            for g in range(KV_HEADS):
                m_ref[g] = m_new[g]
        if nxt is not None:
            jn, slot_n = nxt
            jc = jnp.minimum(jn, nk - 1)
        out = []
        for g in range(KV_HEADS):
            cmax = jnp.full((8, R), NEG, f32)
            for c in range(NC):
                rows = slice(c * LANES, (c + 1) * LANES)
                if nxt is not None:
                    bias = pltpu.bitcast(s_ref[jc, rows, :], f32)
                    lm = (jnp.dot(k_ref[0, jc, rows, :], qpad_ref[g], preferred_element_type=f32)
                          + jnp.concatenate([bias] * GROUPS, axis=1))
                    lm_refs[slot_n][g][rows, :] = lm
                    cmax = jnp.maximum(cmax, jnp.max(lm.reshape(LANES // 8, 8, R), axis=0))
                if cur is not None:
                    p_refs[slot][g][rows, :] = jnp.exp2(lm_refs[slot][g][rows, :] - m_new[g]).astype(bf16)
            out.append(jnp.max(cmax, axis=0, keepdims=True))
        if cur is not None:
            for g in range(KV_HEADS):
                acc_ref[g] = (jnp.exp2(m_old[g] - m_new[g]) * acc_ref[g]
                              + jnp.dot(vt_ref[0, j, g], p_refs[slot][g][...], preferred_element_type=f32))
        return tuple(out)

    def tile_pair(i, tile_max):
        nxt_max = tile_step((2 * i + 1, 1), (2 * i, 0, tile_max))
        return tile_step((2 * i + 2, 0), (2 * i + 1, 1, nxt_max))

    last_max = lax.fori_loop(0, nk // 2, tile_pair, tile_step((0, 0), None))

    @pl.when(nk % 2 == 1)
    def _():
        tile_step(None, (nk - 1, 0, last_max))

    for g in range(KV_HEADS):
        o_t = acc_ref[g, 0:HEAD_DIM, :] / acc_ref[g, HEAD_DIM:HEAD_DIM + 1, :]
        for i in range(0, GROUPS, 2):
            pair = jnp.concatenate([o_t[:, i * TQ:(i + 1) * TQ], o_t[:, (i + 1) * TQ:(i + 2) * TQ]], axis=0)
            h = g * GROUPS + i
            o_ref[0, :, h * HEAD_DIM:(h + 2) * HEAD_DIM] = pair.T.astype(bf16)


def _dsa(qt, qit, wit, kb, kib, vt, P, top):
    B, nq = qt.shape[:2]
    R = qt.shape[-1]
    TQ = R // GROUPS
    NT, TK = kb.shape[1:3]
    assert top <= TK and TQ % CHUNK == 0
    tri = jnp.asarray(np.tril(np.ones((LANES, LANES), np.float32)), bf16)
    return pl.pallas_call(
        functools.partial(_dsa_kernel, TQ=TQ, TK=TK, P=P, top=top),
        grid=(B, nq),
        in_specs=[pl.BlockSpec((1, 1, KV_HEADS, HEAD_DIM, R), lambda b, i: (b, i, 0, 0, 0)),
                  pl.BlockSpec((1, 1, IDX_DIM, IDX_HEADS * TQ), lambda b, i: (b, i, 0, 0)),
                  pl.BlockSpec((1, 1, WI_ROWS, TQ), lambda b, i: (b, i, 0, 0)),
                  pl.BlockSpec((1, NT, TK, IDX_DIM), lambda b, i: (b, 0, 0, 0)),
                  pl.BlockSpec((1, NT, TK, KV_HEADS * HEAD_DIM), lambda b, i: (b, 0, 0, 0)),
                  pl.BlockSpec((1, NT, KV_HEADS, VT_ROWS, TK), lambda b, i: (b, 0, 0, 0, 0)),
                  _const_spec((LANES, LANES))],
        out_specs=pl.BlockSpec((1, TQ, N_HEADS * HEAD_DIM), lambda b, i: (b, i, 0)),
        out_shape=jax.ShapeDtypeStruct((B, nq * TQ, N_HEADS * HEAD_DIM), bf16),
        scratch_shapes=[pltpu.VMEM((NT, TK, TQ), i32),
                        pltpu.VMEM((-(-NT // PACK) * PACK, TK // PACK, TQ), i32),
                        pltpu.VMEM((KV_HEADS, KV_HEADS * HEAD_DIM, R), bf16),
                        pltpu.VMEM((KV_HEADS, 1, R), f32),
                        pltpu.VMEM((KV_HEADS, VT_ROWS, R), f32)]
        + [pltpu.VMEM((TK, R), f32)] * (2 * KV_HEADS)
        + [pltpu.VMEM((TK, R), bf16)] * (2 * KV_HEADS),
        compiler_params=_params(("parallel", "arbitrary")),
        name="dsa",
    )(qt, qit, wit, kib, kb, vt, tri)


def _hgrn_mats(T):
    r = np.arange(T)[:, None]
    c = np.arange(T)[None, :]
    mats = [c <= r, c > r]
    b = T // 2
    while b >= 1:
        blk_r = r // (2 * b)
        mid = blk_r * 2 * b + b
        same = blk_r == c // (2 * b)
        late = r >= mid
        mats.append(same & ((late & (c >= mid) & (c <= r)) | (~late & (c > r) & (c < mid))))
        b //= 2
    return np.concatenate(mats, axis=0).astype(np.float32)


def _hgrn_kernel(x_ref, w_ref, lb_ref, g_ref, mats_ref, s0_ref, y_ref, st_out_ref, st_ref, *, T):
    t_idx = pl.program_id(1)

    @pl.when(t_idx == 0)
    def _():
        st_ref[...] = s0_ref[0]

    xb = x_ref[0].astype(bf16)
    proj = jnp.dot(xb, w_ref[...], preferred_element_type=f32)
    hq = proj[:, 0:HGRN_W]
    z = proj[:, HGRN_W:2 * HGRN_W]
    hv = proj[:, 2 * HGRN_W:3 * HGRN_W]
    hg = proj[:, 3 * HGRN_W:4 * HGRN_W]
    lb = lb_ref[...]
    logf = (jnp.minimum(z, 0.0) - jnp.log1p(jnp.exp(-jnp.abs(z)))) + jnp.log1p(lb * jnp.exp(-z))
    kk = (1.0 - lb) * _sigmoid(-z)
    qq = hq * _sigmoid(hq)
    gate = hg * _sigmoid(hg)

    hi = logf.astype(bf16)
    r1 = logf - hi.astype(f32)
    mid = r1.astype(bf16)
    lo = (r1 - mid.astype(f32)).astype(bf16)
    mats = mats_ref[...]
    e_all = (jnp.dot(mats, hi, preferred_element_type=f32)
             + jnp.dot(mats, mid, preferred_element_type=f32)
             + jnp.dot(mats, lo, preferred_element_type=f32))

    row = lax.broadcasted_iota(i32, (T, 1), 0)
    col = lax.broadcasted_iota(i32, (1, T), 1)
    nt = (((1,), (1,)), ((), ()))
    tn = (((0,), (0,)), ((), ()))
    for h in range(HGRN_HEADS):
        sl = slice(h * HGRN_DK, (h + 1) * HGRN_DK)
        q_h, k_h, v_h = qq[:, sl], kk[:, sl], hv[:, sl]
        vb = v_h.astype(bf16)
        cum = e_all[0:T, sl]
        suf = e_all[T:2 * T, sl]
        st = st_ref[h]
        o = lax.dot_general((q_h * jnp.exp(cum)).astype(bf16), st.astype(bf16), nt,
                            preferred_element_type=f32)
        o = o + jnp.sum(q_h * k_h, axis=-1, keepdims=True) * v_h
        scores = jnp.zeros((T, T), f32)
        b = T // 2
        lvl = 2
        while b >= 1:
            xdec = jnp.exp(e_all[lvl * T:(lvl + 1) * T, sl])
            late = ((row // b) % 2) == 1
            a_m = jnp.where(late, q_h * xdec, 0.0).astype(bf16)
            b_m = jnp.where(late, 0.0, k_h * xdec).astype(bf16)
            sc = lax.dot_general(a_m, b_m, nt, preferred_element_type=f32)
            if 2 * b < T:
                sc = jnp.where((row // (2 * b)) == (col // (2 * b)), sc, 0.0)
            scores = scores + sc
            b //= 2
            lvl += 1
        o = o + jnp.dot(scores.astype(bf16), vb, preferred_element_type=f32)
        kd = (k_h * jnp.exp(suf)).astype(bf16)
        st_ref[h] = (st * jnp.exp(cum[T - 1:T, :])
                     + lax.dot_general(vb, kd, tn, preferred_element_type=f32))
        o = o * lax.rsqrt(jnp.mean(o * o, axis=-1, keepdims=True) + LN_EPS) * g_ref[...]
        y_ref[0, :, sl] = (o * gate[:, sl]).astype(bf16)

    st_out_ref[0] = st_ref[...]


def _hgrn(x, w_h, lb, norm_g, s0):
    B, S, _ = x.shape
    T = _row_tile(S, 2 * CHUNK)
    mats = jnp.asarray(_hgrn_mats(T), bf16)
    nm = mats.shape[0]
    st0 = jnp.swapaxes(s0.astype(f32), 2, 3)
    y, st = pl.pallas_call(
        functools.partial(_hgrn_kernel, T=T),
        grid=(B, S // T),
        in_specs=[pl.BlockSpec((1, T, D_MODEL), lambda b, t: (b, t, 0)),
                  _const_spec((D_MODEL, 4 * HGRN_W)),
                  _const_spec((1, HGRN_W)),
                  _const_spec((1, HGRN_DV)),
                  _const_spec((nm, T)),
                  pl.BlockSpec((1, HGRN_HEADS, HGRN_DV, HGRN_DK), lambda b, t: (b, 0, 0, 0))],
        out_specs=[pl.BlockSpec((1, T, HGRN_W), lambda b, t: (b, t, 0)),
                   pl.BlockSpec((1, HGRN_HEADS, HGRN_DV, HGRN_DK), lambda b, t: (b, 0, 0, 0))],
        out_shape=[jax.ShapeDtypeStruct((B, S, HGRN_W), bf16),
                   jax.ShapeDtypeStruct((B, HGRN_HEADS, HGRN_DV, HGRN_DK), f32)],
        scratch_shapes=[pltpu.VMEM((HGRN_HEADS, HGRN_DV, HGRN_DK), f32)],
        compiler_params=_params(("parallel", "arbitrary")),
        name="hgrn",
    )(x, w_h, lb, norm_g, mats, st0)
    return y, jnp.swapaxes(st, 2, 3)


HIST = CONV_W - 1
PADROWS = 8


def _conv_rows(buf_ref, cols, u, w, bias, T):
    buf_ref[PADROWS:PADROWS + T, cols] = u
    y = bias
    for j in range(CONV_W):
        y = y + w[j:j + 1, :] * buf_ref[PADROWS - HIST + j:PADROWS - HIST + j + T, cols]
    buf_ref[PADROWS - HIST:PADROWS, cols] = u[T - HIST:T, :]
    return y


def _merge_kernel(x_ref, ya_ref, yb_ref, hist_ref, wc_ref, wg_ref, wbr_ref, wout_ref, cw_ref, cb_ref,
                  g_ref, b_ref, o_ref, hist_out_ref, buf_ref, *, T, alpha):
    @pl.when(pl.program_id(1) == 0)
    def _():
        buf_ref[PADROWS - HIST:PADROWS, :] = hist_ref[0]

    x = x_ref[0]
    xb = x.astype(bf16)
    c = jnp.dot(xb, wc_ref[...], preferred_element_type=f32)
    W = SCONV_WIDTH
    u = c[:, W:2 * W] * c[:, 2 * W:3 * W]
    u_conv = _conv_rows(buf_ref, slice(0, W), u, cw_ref[...], cb_ref[...], T)
    yc = (c[:, 0:W] * u_conv).astype(bf16)
    hist_out_ref[0] = buf_ref[PADROWS - HIST:PADROWS, :]

    merged = None
    for i, yb in enumerate((ya_ref[0], yb_ref[0], yc)):
        gate = _sigmoid(jnp.dot(xb, wg_ref[:, i * D_MODEL:(i + 1) * D_MODEL], preferred_element_type=f32))
        term = gate * jnp.dot(yb, wbr_ref[i], preferred_element_type=f32)
        merged = term if merged is None else merged + term
    out = jnp.dot(merged.astype(bf16), wout_ref[...], preferred_element_type=f32)
    o_ref[0] = _layer_norm(alpha * x + out, g_ref[...], b_ref[...])


def _merge(x, ya, yb, hist, wc, wg, wbr, wout, cw, cb, g, b, alpha):
    B, S, _ = x.shape
    T = _row_tile(S, 256)
    row = lambda w: pl.BlockSpec((1, T, w), lambda bb, t: (bb, t, 0))
    hspec = pl.BlockSpec((1, HIST, SCONV_WIDTH), lambda bb, t: (bb, 0, 0))
    return pl.pallas_call(
        functools.partial(_merge_kernel, T=T, alpha=alpha),
        grid=(B, S // T),
        in_specs=[row(D_MODEL), row(BRANCH_WIDTH), row(BRANCH_WIDTH), hspec,
                  _const_spec((D_MODEL, 3 * SCONV_WIDTH)), _const_spec((D_MODEL, N_BRANCH * D_MODEL)),
                  _const_spec((N_BRANCH, BRANCH_WIDTH, D_MODEL)), _const_spec((D_MODEL, D_MODEL)),
                  _const_spec((CONV_W, SCONV_WIDTH)), _const_spec((1, SCONV_WIDTH)),
                  _const_spec((1, D_MODEL)), _const_spec((1, D_MODEL))],
        out_specs=[row(D_MODEL), hspec],
        out_shape=[jax.ShapeDtypeStruct((B, S, D_MODEL), f32),
                   jax.ShapeDtypeStruct((B, HIST, SCONV_WIDTH), f32)],
        scratch_shapes=[pltpu.VMEM((PADROWS + T, SCONV_WIDTH), f32)],
        compiler_params=_params(("parallel", "arbitrary")),
        name="merge",
    )(x, ya, yb, hist, wc, wg, wbr, wout, cw, cb, g, b)


def _ffn_kernel(x_ref, hist_ref, wup_ref, cw_ref, cb_ref, wdn_ref, g_ref, b_ref, o_ref, hist_out_ref,
                *bufs, T, alpha):
    n_steps = D_FF // FFN_COLS
    col_ranges = [slice(off + c * FFN_COLS, off + (c + 1) * FFN_COLS)
                  for c in range(n_steps) for off in (0, D_FF)]

    @pl.when(pl.program_id(1) == 0)
    def _():
        for buf, cols in zip(bufs, col_ranges):
            buf[PADROWS - HIST:PADROWS, :] = hist_ref[0, :, cols]

    x = x_ref[0]
    xb = x.astype(bf16)

    def up_conv(cidx):
        halves = []
        for half in range(2):
            buf, cols = bufs[2 * cidx + half], col_ranges[2 * cidx + half]
            h = jnp.dot(xb, wup_ref[:, cols], preferred_element_type=f32)
            halves.append(_conv_rows(buf, slice(None), h, cw_ref[:, cols], cb_ref[:, cols], T))
            hist_out_ref[0, :, cols] = buf[PADROWS - HIST:PADROWS, :]
        return halves

    acc = jnp.zeros((T, D_MODEL), f32)
    nxt = up_conv(0)
    for cidx in range(n_steps):
        a_g, b_v = nxt
        if cidx + 1 < n_steps:
            nxt = up_conv(cidx + 1)
        act = (a_g * _sigmoid(a_g) * b_v).astype(bf16)
        acc = acc + jnp.dot(act, wdn_ref[cidx * FFN_COLS:(cidx + 1) * FFN_COLS, :],
                            preferred_element_type=f32)
    o_ref[0] = _layer_norm(alpha * x + acc, g_ref[...], b_ref[...])


def _ffn(x, hist, wup, cw, cb, wdn, g, b, alpha):
    B, S, _ = x.shape
    T = _row_tile(S, 256)
    row = pl.BlockSpec((1, T, D_MODEL), lambda bb, t: (bb, t, 0))
    hspec = pl.BlockSpec((1, HIST, 2 * D_FF), lambda bb, t: (bb, 0, 0))
    return pl.pallas_call(
        functools.partial(_ffn_kernel, T=T, alpha=alpha),
        grid=(B, S // T),
        in_specs=[row, hspec, _const_spec((D_MODEL, 2 * D_FF)), _const_spec((CONV_W, 2 * D_FF)),
                  _const_spec((1, 2 * D_FF)), _const_spec((D_FF, D_MODEL)),
                  _const_spec((1, D_MODEL)), _const_spec((1, D_MODEL))],
        out_specs=[row, hspec],
        out_shape=[jax.ShapeDtypeStruct((B, S, D_MODEL), f32),
                   jax.ShapeDtypeStruct((B, HIST, 2 * D_FF), f32)],
        scratch_shapes=[pltpu.VMEM((PADROWS + T, FFN_COLS), f32)] * (2 * D_FF // FFN_COLS),
        compiler_params=_params(("parallel", "arbitrary")),
        name="ffn",
    )(x, hist, wup, cw, cb, wdn, g, b)


def _cast_kernel(x_ref, o_ref):
    o_ref[...] = x_ref[...].astype(o_ref.dtype)


def _to_bf16(w):
    D, Rw, C = w.shape
    T = _row_tile(Rw, LANES)
    spec = pl.BlockSpec((1, T, C), lambda d, t: (d, t, 0))
    return pl.pallas_call(_cast_kernel, grid=(D, Rw // T), in_specs=[spec], out_specs=spec,
                          out_shape=jax.ShapeDtypeStruct(w.shape, bf16),
                          compiler_params=_params(("parallel", "parallel")), name="to_bf16")(w)


def _layer_weights(l, depth, w_in, hgrn_lb_logits, hgrn_norm_g, sconv_w, sconv_b, w_branch, w_out,
                   ln1_g, ln1_b, w_up, ffn_conv_w, ffn_conv_b, w_down, ln2_g, ln2_b):
    offs = np.concatenate([[0], np.cumsum(IN_SIZES)]).tolist()
    sec = lambda i, j=None: w_in[l][:, offs[i]:offs[(i if j is None else j) + 1]]
    a_q, a_k, a_v, i_q, i_k, i_w = (sec(i) for i in range(6))
    zpad = lambda n: jnp.zeros((D_MODEL, n), w_in.dtype)
    w_row = jnp.concatenate([a_k, i_k, zpad(LANES - IDX_DIM), a_v], axis=1).astype(bf16)
    w_t = jnp.concatenate([a_q, i_q, a_v, i_w, zpad(WT_ROWS - 900)], axis=1).T.astype(bf16)
    lbp = jax.nn.softmax(hgrn_lb_logits.astype(f32), axis=0)
    lb = (jnp.cumsum(lbp, axis=0) - lbp[0])[l].reshape(1, HGRN_W)
    row = lambda a: a.reshape(1, -1).astype(f32)
    return dict(
        w_row=w_row, w_t=w_t, w_h=sec(6, 9).astype(bf16), lb=lb, norm_g=row(hgrn_norm_g[l]),
        w_c=sec(10, 12).astype(bf16), w_g=sec(13).astype(bf16),
        w_br=w_branch[l].astype(bf16), w_out=w_out[l].astype(bf16),
        sconv_w=sconv_w[l].astype(f32), sconv_b=row(sconv_b[l]),
        ln1_g=row(ln1_g[l]), ln1_b=row(ln1_b[l]),
        w_up=w_up[l].astype(bf16), ffn_w=ffn_conv_w[l].astype(f32), ffn_b=row(ffn_conv_b[l]),
        w_down=w_down[l].astype(bf16), ln2_g=row(ln2_g[l]), ln2_b=row(ln2_b[l]))


def _key_tiles(a, TK):
    B, L, C = a.shape
    NT = -(-L // TK)
    a = jnp.pad(a, ((0, 0), (0, NT * TK - L), (0, 0)))
    return a.reshape(B, NT, TK, C)


def _trunk_layer(x, tables, P, k_past, v_past, ki_past, s0, sc_hist, ffn_hist, w, alpha):
    B, S, _ = x.shape
    TK = KEY_TILE
    a = _attn_proj(x, w["w_row"], w["w_t"], *tables)
    k, v, ki = a["k"], a["v"], a["ki"]
    L = P + S
    if k_past is None and S % TK == 0:
        kb = a["kb"].reshape(B, S // TK, TK, -1)
        kib = a["kib"].reshape(B, S // TK, TK, -1)
        vt = a["vt"]
    else:
        kb, kib, v_all = a["kb"], a["kib"], v.astype(bf16)
        if k_past is not None:
            kb = jnp.concatenate([k_past.reshape(B, P, -1).astype(bf16), kb], axis=1)
            kib = jnp.concatenate([ki_past.astype(bf16), kib], axis=1)
            v_all = jnp.concatenate([v_past.reshape(B, P, -1).astype(bf16), v_all], axis=1)
        kb, kib = _key_tiles(kb, TK), _key_tiles(kib, TK)
        vt = _key_tiles(v_all, TK).reshape(B, -1, TK, KV_HEADS, HEAD_DIM).transpose(0, 1, 3, 4, 2)
        ones = jnp.zeros(vt.shape[:3] + (VT_ROWS - HEAD_DIM, TK), bf16).at[:, :, :, 0, :].set(1.0)
        vt = jnp.concatenate([vt, ones], axis=3)
    y_a = _dsa(a["qt"], a["qit"], a["wit"], kb, kib, vt, P, min(TOPK_MAX, L // 4))
    y_b, s_new = _hgrn(x, w["w_h"], w["lb"], w["norm_g"], s0)
    x1, sc_new = _merge(x, y_a, y_b, sc_hist, w["w_c"], w["w_g"], w["w_br"], w["w_out"],
                        w["sconv_w"], w["sconv_b"], w["ln1_g"], w["ln1_b"], alpha)
    x2, ffn_new = _ffn(x1, ffn_hist, w["w_up"], w["ffn_w"], w["ffn_b"], w["w_down"],
                       w["ln2_g"], w["ln2_b"], alpha)
    new = (k.reshape(B, S, KV_HEADS, HEAD_DIM), v.reshape(B, S, KV_HEADS, HEAD_DIM), ki,
           s_new, sc_new, ffn_new)
    return x2, new


def kernel(x_prompt, x_sample, cache_attn_k, cache_attn_v, cache_idx_k, state_hgrn, state_sconv,
           state_ffn_conv, w_in, hgrn_lb_logits, hgrn_norm_g, sconv_w, sconv_b, w_branch, w_out,
           ln1_g, ln1_b, w_up, ffn_conv_w, ffn_conv_b, w_down, ln2_g, ln2_b):
    depth = w_in.shape[0]
    alpha = (2 * depth) ** 0.25
    B, S, _ = x_prompt.shape
    DB, DS, _ = x_sample.shape
    P = cache_attn_k.shape[2]
    cs_p = _rotary_tables(jnp.arange(S))
    cs_s = _rotary_tables(P + jnp.arange(DS))
    xp, xs = x_prompt, x_sample
    st_p = [[] for _ in range(6)]
    st_s = [[] for _ in range(6)]
    w_in_b = _to_bf16(w_in)
    for l in range(depth):
        w = _layer_weights(l, depth, w_in_b, hgrn_lb_logits, hgrn_norm_g, sconv_w, sconv_b, w_branch,
                           w_out, ln1_g, ln1_b, w_up, ffn_conv_w, ffn_conv_b, w_down, ln2_g, ln2_b)
        xp, new_p = _trunk_layer(
            xp, cs_p, 0, None, None, None,
            jnp.zeros((B, HGRN_HEADS, HGRN_DK, HGRN_DV), f32),
            jnp.zeros((B, HIST, SCONV_WIDTH), f32),
            jnp.zeros((B, HIST, 2 * D_FF), f32), w, alpha)
        xs, new_s = _trunk_layer(
            xs, cs_s, P, cache_attn_k[l], cache_attn_v[l], cache_idx_k[l], state_hgrn[l],
            state_sconv[l], state_ffn_conv[l], w, alpha)
        for j in range(6):
            st_p[j].append(new_p[j])
            st_s[j].append(new_s[j])
    outs_p = [jnp.stack(a, axis=0) for a in st_p]
    outs_s = [jnp.stack(a, axis=0) for a in st_s]
    return (xp, xs, *outs_p, *outs_s)
```

```python
import functools

import numpy as np
import jax
import jax.numpy as jnp
from jax import lax
from jax.experimental import pallas as pl
from jax.experimental.pallas import tpu as pltpu

D_MODEL = 1024
CHUNK = 64
N_HEADS = 8
HEAD_DIM = 64
KV_HEADS = 2
GROUPS = N_HEADS // KV_HEADS
IDX_HEADS = 4
IDX_DIM = 64
TOPK_MAX = 256
ROPE_THETA = 500000.0
ATTN_SCALE = HEAD_DIM ** -0.5
IDX_SCALE = (IDX_DIM ** -0.5) * (IDX_HEADS ** -0.5)
NEG = -1e30
HGRN_HEADS = 4
HGRN_DK = 128
HGRN_DV = 128
HGRN_W = HGRN_HEADS * HGRN_DK
SCONV_WIDTH = 512
CONV_W = 3
BRANCH_WIDTH = 512
N_BRANCH = 3
D_FF = 2816
LN_EPS = 1e-5
IN_SIZES = (N_HEADS * HEAD_DIM, KV_HEADS * HEAD_DIM, KV_HEADS * HEAD_DIM,
            IDX_HEADS * IDX_DIM, IDX_DIM, IDX_HEADS,
            HGRN_W, HGRN_W, HGRN_W, HGRN_W,
            SCONV_WIDTH, SCONV_WIDTH, SCONV_WIDTH,
            N_BRANCH * D_MODEL)

LANES = 128
VMEM_LIMIT = 56 * 1024 * 1024
KEY_TILE = 512
QUERY_TILE = 256
VT_ROWS = HEAD_DIM + 16
FFN_COLS = 256
HGRN_SUBTILES = 4
PACK = 4
DIGITS = ((25, 7), (18, 7), (11, 7), (4, 7), (0, 4))
GUARD = np.int32(-0x7F7F7F80)
ONES = np.int32(0x01010101)

f32 = jnp.float32
bf16 = jnp.bfloat16
i32 = jnp.int32


def _row_tile(s, want):
    return want if s % want == 0 else s


def _const_spec(shape):
    nd = len(shape)
    return pl.BlockSpec(shape, lambda *_: (0,) * nd, pipeline_mode=pl.Buffered(1))


def _params(sem):
    return pltpu.CompilerParams(dimension_semantics=sem, vmem_limit_bytes=VMEM_LIMIT)


def _layer_norm(z, g, b):
    mu = jnp.mean(z, axis=-1, keepdims=True)
    d = z - mu
    var = jnp.mean(d * d, axis=-1, keepdims=True)
    return d * lax.rsqrt(var + LN_EPS) * g + b


def _sigmoid(x):
    return 1.0 / (1.0 + jnp.exp(-x))


Q_PRESCALE = ATTN_SCALE * float(np.log2(np.e))
ROT = HEAD_DIM // 4
WR_COLS = 384
WT_ROWS = 912
WI_ROWS = 8


def _attn_proj_kernel(x_ref, wr_ref, wt_ref, cs_ref, cst_ref, k_ref, v_ref, ki_ref, kb_ref, kib_ref,
                      vt_ref, qt_ref, qit_ref, wit_ref, *, T, TQ):
    xb = x_ref[0].astype(bf16)
    pr = jnp.dot(xb, wr_ref[...], preferred_element_type=f32)
    c, sa, sb = cs_ref[0], cs_ref[1], cs_ref[2]

    def rot(xg):
        return xg * c + pltpu.roll(xg, LANES - ROT // 2, 1) * sa + pltpu.roll(xg, ROT // 2, 1) * sb

    k = rot(pr[:, 0:LANES])
    ki = rot(pr[:, LANES:2 * LANES])[:, :IDX_DIM]
    k_ref[0] = k
    kb_ref[0] = k.astype(bf16)
    ki_ref[0] = ki
    kib_ref[0] = ki.astype(bf16)
    v_ref[0] = pr[:, 2 * LANES:3 * LANES]

    nt = (((1,), (1,)), ((), ()))
    pt = lax.dot_general(wt_ref[...], xb, nt, preferred_element_type=f32)
    ct, st = cst_ref[0], cst_ref[1]

    def rot_t(hb):
        lead = hb[0:ROT] * ct + jnp.concatenate([hb[ROT // 2:ROT], hb[0:ROT // 2]], axis=0) * st
        return jnp.concatenate([lead, hb[ROT:HEAD_DIM]], axis=0)

    nqb = T // TQ
    for h in range(N_HEADS):
        hb = (rot_t(pt[h * HEAD_DIM:(h + 1) * HEAD_DIM]) * Q_PRESCALE).astype(bf16)
        g, i = divmod(h, GROUPS)
        for n in range(nqb):
            qt_ref[0, n, g, :, i * TQ:(i + 1) * TQ] = hb[:, n * TQ:(n + 1) * TQ]
    base = N_HEADS * HEAD_DIM
    for h in range(IDX_HEADS):
        hb = rot_t(pt[base + h * IDX_DIM:base + (h + 1) * IDX_DIM]).astype(bf16)
        for n in range(nqb):
            qit_ref[0, n, :, h * TQ:(h + 1) * TQ] = hb[:, n * TQ:(n + 1) * TQ]
    base += IDX_HEADS * IDX_DIM
    ones_rows = jnp.where(lax.broadcasted_iota(i32, (VT_ROWS - HEAD_DIM, T), 0) == 0, 1.0, 0.0)
    for g in range(KV_HEADS):
        vt_ref[0, 0, g, 0:HEAD_DIM, :] = pt[base + g * HEAD_DIM:base + (g + 1) * HEAD_DIM].astype(bf16)
        vt_ref[0, 0, g, HEAD_DIM:VT_ROWS, :] = ones_rows.astype(bf16)
    base += KV_HEADS * HEAD_DIM
    for n in range(nqb):
        wit_ref[0, n] = pt[base:base + WI_ROWS, n * TQ:(n + 1) * TQ]


def _attn_proj(x, w_row, w_t, cs, cst):
    B, S, _ = x.shape
    T = _row_tile(S, KEY_TILE)
    TQ = _row_tile(S, QUERY_TILE)
    nqb = T // TQ
    R = GROUPS * TQ
    row = lambda w: pl.BlockSpec((1, T, w), lambda b, t: (b, t, 0))
    names = ("k", "v", "ki", "kb", "kib", "vt", "qt", "qit", "wit")
    outs = pl.pallas_call(
        functools.partial(_attn_proj_kernel, T=T, TQ=TQ),
        grid=(B, S // T),
        in_specs=[row(D_MODEL), _const_spec((D_MODEL, WR_COLS)), _const_spec((WT_ROWS, D_MODEL)),
                  pl.BlockSpec((3, T, LANES), lambda b, t: (0, t, 0)),
                  pl.BlockSpec((2, ROT, T), lambda b, t: (0, 0, t))],
        out_specs=[row(128), row(128), row(IDX_DIM), row(128), row(IDX_DIM),
                   pl.BlockSpec((1, 1, KV_HEADS, VT_ROWS, T), lambda b, t: (b, t, 0, 0, 0)),
                   pl.BlockSpec((1, nqb, KV_HEADS, HEAD_DIM, R), lambda b, t: (b, t, 0, 0, 0)),
                   pl.BlockSpec((1, nqb, IDX_DIM, IDX_HEADS * TQ), lambda b, t: (b, t, 0, 0)),
                   pl.BlockSpec((1, nqb, WI_ROWS, TQ), lambda b, t: (b, t, 0, 0))],
        out_shape=[jax.ShapeDtypeStruct((B, S, 128), f32),
                   jax.ShapeDtypeStruct((B, S, 128), f32),
                   jax.ShapeDtypeStruct((B, S, IDX_DIM), f32),
                   jax.ShapeDtypeStruct((B, S, 128), bf16),
                   jax.ShapeDtypeStruct((B, S, IDX_DIM), bf16),
                   jax.ShapeDtypeStruct((B, S // T, KV_HEADS, VT_ROWS, T), bf16),
                   jax.ShapeDtypeStruct((B, S // TQ, KV_HEADS, HEAD_DIM, R), bf16),
                   jax.ShapeDtypeStruct((B, S // TQ, IDX_DIM, IDX_HEADS * TQ), bf16),
                   jax.ShapeDtypeStruct((B, S // TQ, WI_ROWS, TQ), f32)],
        compiler_params=_params(("parallel", "parallel")),
        name="attn_proj",
    )(x, w_row, w_t, cs, cst)
    return dict(zip(names, outs))


def _rotary_tables(pos):
    half = ROT // 2
    inv_freq = ROPE_THETA ** (-jnp.arange(half, dtype=f32) / half)
    ang = pos.astype(f32)[:, None] * inv_freq[None, :]
    cos = jnp.cos(ang)
    sin = jnp.sin(ang)
    S = pos.shape[0]
    one = jnp.ones((S, HEAD_DIM - ROT), f32)
    zero = jnp.zeros((S, HEAD_DIM - ROT), f32)
    zh = jnp.zeros((S, half), f32)
    c = jnp.concatenate([cos, cos, one], axis=1)
    sa = jnp.concatenate([-sin, zh, zero], axis=1)
    sb = jnp.concatenate([zh, sin, zero], axis=1)
    tile2 = lambda a: jnp.concatenate([a, a], axis=1)
    cs = jnp.stack([tile2(c), tile2(sa), tile2(sb)], axis=0)
    cst = jnp.stack([jnp.concatenate([cos, cos], axis=1).T,
                     jnp.concatenate([-sin, sin], axis=1).T], axis=0)
    return cs, cst


def _dsa_kernel(qt_ref, qit_ref, wit_ref, ki_ref, k_ref, vt_ref, tri_ref, o_ref,
                s_ref, pk_ref, qpad_ref, m_ref, acc_ref, *bufs, TQ, TK, P, top):
    lm_refs = (bufs[0:KV_HEADS], bufs[KV_HEADS:2 * KV_HEADS])
    p_refs = (bufs[2 * KV_HEADS:3 * KV_HEADS], bufs[3 * KV_HEADS:4 * KV_HEADS])
    qb = pl.program_id(1)
    q0 = P + qb * TQ
    nk = (q0 + TQ + TK - 1) // TK
    n_full = q0 // TK
    NC = TK // LANES
    R = GROUPS * TQ
    q_lim = ((q0 + lax.broadcasted_iota(i32, (1, TQ), 1)) // CHUNK + 1) * CHUNK
    wit = wit_ref[0, 0] * IDX_SCALE
    zero_head = jnp.zeros((HEAD_DIM, R), bf16)
    qpad_ref[0] = jnp.concatenate([qt_ref[0, 0, 0], zero_head], axis=0)
    qpad_ref[1] = jnp.concatenate([zero_head, qt_ref[0, 0, 1]], axis=0)

    TKP = TK // PACK

    def lead_digit(key):
        return (key >> DIGITS[0][0]) + (1 << (DIGITS[0][1] - 1))

    def score_tile(masked, j, carry):
        for c in range(NC):
            rows = slice(c * LANES, (c + 1) * LANES)
            rel = jnp.maximum(jnp.dot(ki_ref[0, j, rows, :], qit_ref[0, 0], preferred_element_type=f32), 0.0)
            score = wit[0:1, :] * rel[:, 0:TQ]
            for h in range(1, IDX_HEADS):
                score = score + wit[h:h + 1, :] * rel[:, h * TQ:(h + 1) * TQ]
            if masked:
                k_pos = j * TK + c * LANES + lax.broadcasted_iota(i32, (LANES, TQ), 0)
                score = jnp.where(k_pos < q_lim, score, NEG)
            bits = pltpu.bitcast(score, i32)
            key = jnp.where(bits >= 0, bits, bits ^ 0x7FFFFFFF)
            s_ref[j, rows, :] = key
            if c == 0:
                pk_ref[j] = lead_digit(key) | GUARD
            else:
                pk_ref[j] = pk_ref[j] | (lead_digit(key) << (8 * c))
        return carry

    lax.fori_loop(0, n_full, functools.partial(score_tile, False), 0)
    lax.fori_loop(n_full, nk, functools.partial(score_tile, True), 0)

    n_chunks = (nk + PACK - 1) // PACK

    def fill_guard(j, carry):
        pk_ref[j] = jnp.full((TKP, TQ), GUARD, i32)
        return carry

    lax.fori_loop(nk, n_chunks * PACK, fill_guard, 0)

    def count_ge(cand):
        cvec = cand * ONES

        def chunk(ci, tot):
            accs = [None] * 4
            for tt in range(PACK):
                for r in range(TKP // 8):
                    slab = pk_ref[ci * PACK + tt, r * 8:(r + 1) * 8, :]
                    hit = lax.shift_right_logical(slab - cvec, 7) & ONES
                    n = (tt * (TKP // 8) + r) % 4
                    accs[n] = hit if accs[n] is None else accs[n] + hit
            a = (accs[0] + accs[1]) + (accs[2] + accs[3])
            return tot + ((a & 0xFF) + ((a >> 8) & 0xFF) + ((a >> 16) & 0xFF)
                          + lax.shift_right_logical(a, 24))

        tot = lax.fori_loop(0, n_chunks, chunk, jnp.zeros((8, TQ), i32))
        return jnp.sum(tot, axis=0, keepdims=True)

    def repack(lo, width, prefix):
        origin = prefix * (1 << width)

        def tile(j, carry):
            for r in range(TKP // 8):
                word = jnp.full((8, TQ), GUARD, i32)
                for q in range(PACK):
                    key = s_ref[j, q * TKP + r * 8:q * TKP + (r + 1) * 8, :]
                    d = (key >> lo) - origin
                    in_range = pltpu.bitcast(d, jnp.uint32) < jnp.uint32(1 << width)
                    word = word | (jnp.where(in_range, d, 0) << (8 * q))
                pk_ref[j, r * 8:(r + 1) * 8, :] = word
            return carry
        lax.fori_loop(0, nk, tile, 0)

    base = jnp.zeros((1, TQ), i32)
    bsize = jnp.full((1, TQ), nk * TK, i32)
    prefix = None
    for stage, (lo, width) in enumerate(DIGITS):
        if stage > 0:
            repack(lo, width, prefix)
        need = top - base

        def bit_step(i, st, need=need, width=width):
            v, c_v, c_rej = st
            cand = v | jnp.left_shift(jnp.int32(1), width - 1 - i)
            c = count_ge(cand)
            ok = c >= need
            return jnp.where(ok, cand, v), jnp.where(ok, c, c_v), jnp.where(ok, c_rej, c)

        zero = jnp.zeros((1, TQ), i32)
        v, c_v, c_gt = lax.fori_loop(0, width, bit_step, (zero, bsize, zero))
        base, bsize = base + c_gt, c_v - c_gt
        prefix = v - (1 << (width - 1)) if stage == 0 else prefix * (1 << width) + v
    t = prefix
    neg_key = np.array(NEG, np.float32).view(np.int32) ^ 0x7FFFFFFF
    any_excess = jnp.max(jnp.where(base + bsize > top, 1, 0)) > 0

    @pl.when(jnp.logical_not(any_excess))
    def _():
        thr = jnp.where(t == neg_key, t + 1, t)

        def fast(j, carry):
            s_ref[j] = pltpu.bitcast(jnp.where(s_ref[j] >= thr, 0.0, NEG), i32)
            return carry
        lax.fori_loop(0, nk, fast, 0)

    @pl.when(any_excess)
    def _():
        need = jnp.where(t == neg_key, 0, top - base).astype(f32)

        def ranked(j, taken):
            keys = s_ref[j]
            for c in range(NC):
                kc = keys[c * LANES:(c + 1) * LANES, :]
                eq = jnp.where(kc == t, 1.0, 0.0)
                rank = taken + jnp.dot(tri_ref[...], eq.astype(bf16), preferred_element_type=f32)
                pick = jnp.where(kc > t, 1.0, jnp.where(rank <= need, eq, 0.0))
                s_ref[j, c * LANES:(c + 1) * LANES, :] = pltpu.bitcast(
                    jnp.where(pick > 0.5, 0.0, NEG), i32)
                taken = taken + jnp.sum(eq, axis=0, keepdims=True)
            return taken
        lax.fori_loop(0, nk, ranked, jnp.zeros((1, TQ), f32))

    m_ref[...] = jnp.full(m_ref.shape, NEG, f32)
    acc_ref[...] = jnp.zeros(acc_ref.shape, f32)

    def tile_step(nxt, cur):
        if cur is not None:
            j, slot, tile_max = cur
            m_old = [m_ref[g] for g in range(KV_HEADS)]
            m_new = [jnp.maximum(m_old[g], tile_max[g]) for g in range(KV_HEADS)]
            for g in range(KV_HEADS):
                m_ref[g] = m_new[g]
        if nxt is not None:
            jn, slot_n = nxt
            jc = jnp.minimum(jn, nk - 1)
        out = []
        for g in range(KV_HEADS):
            cmax = jnp.full((8, R), NEG, f32)
            for c in range(NC):
                rows = slice(c * LANES, (c + 1) * LANES)
                if nxt is not None:
                    bias = pltpu.bitcast(s_ref[jc, rows, :], f32)
                    lm = (jnp.dot(k_ref[0, jc, rows, :], qpad_ref[g], preferred_element_type=f32)
                          + jnp.concatenate([bias] * GROUPS, axis=1))
                    lm_refs[slot_n][g][rows, :] = lm
                    cmax = jnp.maximum(cmax, jnp.max(lm.reshape(LANES // 8, 8, R), axis=0))
                if cur is not None:
                    p_refs[slot][g][rows, :] = jnp.exp2(lm_refs[slot][g][rows, :] - m_new[g]).astype(bf16)
            out.append(jnp.max(cmax, axis=0, keepdims=True))
        if cur is not None:
            for g in range(KV_HEADS):
                acc_ref[g] = (jnp.exp2(m_old[g] - m_new[g]) * acc_ref[g]
                              + jnp.dot(vt_ref[0, j, g], p_refs[slot][g][...], preferred_element_type=f32))
        return tuple(out)

    def tile_pair(i, tile_max):
        nxt_max = tile_step((2 * i + 1, 1), (2 * i, 0, tile_max))
        return tile_step((2 * i + 2, 0), (2 * i + 1, 1, nxt_max))

    last_max = lax.fori_loop(0, nk // 2, tile_pair, tile_step((0, 0), None))

    @pl.when(nk % 2 == 1)
    def _():
        tile_step(None, (nk - 1, 0, last_max))

    for g in range(KV_HEADS):
        o_t = acc_ref[g, 0:HEAD_DIM, :] / acc_ref[g, HEAD_DIM:HEAD_DIM + 1, :]
        for i in range(0, GROUPS, 2):
            pair = jnp.concatenate([o_t[:, i * TQ:(i + 1) * TQ], o_t[:, (i + 1) * TQ:(i + 2) * TQ]], axis=0)
            h = g * GROUPS + i
            o_ref[0, :, h * HEAD_DIM:(h + 2) * HEAD_DIM] = pair.T.astype(bf16)


def _dsa(qt, qit, wit, kb, kib, vt, P, top):
    B, nq = qt.shape[:2]
    R = qt.shape[-1]
    TQ = R // GROUPS
    NT, TK = kb.shape[1:3]
    assert top <= TK and TQ % CHUNK == 0
    tri = jnp.asarray(np.tril(np.ones((LANES, LANES), np.float32)), bf16)
    return pl.pallas_call(
        functools.partial(_dsa_kernel, TQ=TQ, TK=TK, P=P, top=top),
        grid=(B, nq),
        in_specs=[pl.BlockSpec((1, 1, KV_HEADS, HEAD_DIM, R), lambda b, i: (b, i, 0, 0, 0)),
                  pl.BlockSpec((1, 1, IDX_DIM, IDX_HEADS * TQ), lambda b, i: (b, i, 0, 0)),
                  pl.BlockSpec((1, 1, WI_ROWS, TQ), lambda b, i: (b, i, 0, 0)),
                  pl.BlockSpec((1, NT, TK, IDX_DIM), lambda b, i: (b, 0, 0, 0)),
                  pl.BlockSpec((1, NT, TK, KV_HEADS * HEAD_DIM), lambda b, i: (b, 0, 0, 0)),
                  pl.BlockSpec((1, NT, KV_HEADS, VT_ROWS, TK), lambda b, i: (b, 0, 0, 0, 0)),
                  _const_spec((LANES, LANES))],
        out_specs=pl.BlockSpec((1, TQ, N_HEADS * HEAD_DIM), lambda b, i: (b, i, 0)),
        out_shape=jax.ShapeDtypeStruct((B, nq * TQ, N_HEADS * HEAD_DIM), bf16),
        scratch_shapes=[pltpu.VMEM((NT, TK, TQ), i32),
                        pltpu.VMEM((-(-NT // PACK) * PACK, TK // PACK, TQ), i32),
                        pltpu.VMEM((KV_HEADS, KV_HEADS * HEAD_DIM, R), bf16),
                        pltpu.VMEM((KV_HEADS, 1, R), f32),
                        pltpu.VMEM((KV_HEADS, VT_ROWS, R), f32)]
        + [pltpu.VMEM((TK, R), f32)] * (2 * KV_HEADS)
        + [pltpu.VMEM((TK, R), bf16)] * (2 * KV_HEADS),
        compiler_params=_params(("parallel", "arbitrary")),
        name="dsa",
    )(qt, qit, wit, kib, kb, vt, tri)


def _hgrn_mats(T):
    r = np.arange(T)[:, None]
    c = np.arange(T)[None, :]
    mats = [c <= r, c > r]
    b = T // 2
    while b >= 1:
        blk_r = r // (2 * b)
        mid = blk_r * 2 * b + b
        same = blk_r == c // (2 * b)
        late = r >= mid
        mats.append(same & ((late & (c >= mid) & (c <= r)) | (~late & (c > r) & (c < mid))))
        b //= 2
    return np.concatenate(mats, axis=0).astype(np.float32)


def _hgrn_kernel(x_ref, w_ref, lb_ref, g_ref, mats_ref, s0_ref, y_ref, st_out_ref, st_ref, *, T, SUBS):
    t_idx = pl.program_id(1)

    @pl.when(t_idx == 0)
    def _():
        st_ref[...] = s0_ref[0]

    lb = lb_ref[...]
    mats = mats_ref[...]
    row = lax.broadcasted_iota(i32, (T, 1), 0)
    col = lax.broadcasted_iota(i32, (1, T), 1)
    nt = (((1,), (1,)), ((), ()))
    tn = (((0,), (0,)), ((), ()))

    def prepare(i):
        xb = x_ref[0, i * T:(i + 1) * T, :].astype(bf16)
        proj = jnp.dot(xb, w_ref[...], preferred_element_type=f32)
        hq = proj[:, 0:HGRN_W]
        z = proj[:, HGRN_W:2 * HGRN_W]
        hv = proj[:, 2 * HGRN_W:3 * HGRN_W]
        hg = proj[:, 3 * HGRN_W:4 * HGRN_W]
        logf = (jnp.minimum(z, 0.0) - jnp.log1p(jnp.exp(-jnp.abs(z)))) + jnp.log1p(lb * jnp.exp(-z))
        kk = (1.0 - lb) * _sigmoid(-z)
        qq = hq * _sigmoid(hq)
        gate = hg * _sigmoid(hg)
        hi = logf.astype(bf16)
        r1 = logf - hi.astype(f32)
        mid = r1.astype(bf16)
        lo = (r1 - mid.astype(f32)).astype(bf16)
        e_all = (jnp.dot(mats, hi, preferred_element_type=f32)
                 + jnp.dot(mats, mid, preferred_element_type=f32)
                 + jnp.dot(mats, lo, preferred_element_type=f32))
        return qq, kk, hv, gate, e_all

    def recur(i, prepared):
        qq, kk, hv, gate, e_all = prepared
        for h in range(HGRN_HEADS):
            sl = slice(h * HGRN_DK, (h + 1) * HGRN_DK)
            q_h, k_h, v_h = qq[:, sl], kk[:, sl], hv[:, sl]
            vb = v_h.astype(bf16)
            cum = e_all[0:T, sl]
            suf = e_all[T:2 * T, sl]
            st = st_ref[h]
            o = lax.dot_general((q_h * jnp.exp(cum)).astype(bf16), st.astype(bf16), nt,
                                preferred_element_type=f32)
            o = o + jnp.sum(q_h * k_h, axis=-1, keepdims=True) * v_h
            scores = jnp.zeros((T, T), f32)
            b = T // 2
            lvl = 2
            while b >= 1:
                xdec = jnp.exp(e_all[lvl * T:(lvl + 1) * T, sl])
                late = ((row // b) % 2) == 1
                a_m = jnp.where(late, q_h * xdec, 0.0).astype(bf16)
                b_m = jnp.where(late, 0.0, k_h * xdec).astype(bf16)
                sc = lax.dot_general(a_m, b_m, nt, preferred_element_type=f32)
                if 2 * b < T:
                    sc = jnp.where((row // (2 * b)) == (col // (2 * b)), sc, 0.0)
                scores = scores + sc
                b //= 2
                lvl += 1
            o = o + jnp.dot(scores.astype(bf16), vb, preferred_element_type=f32)
            kd = (k_h * jnp.exp(suf)).astype(bf16)
            st_ref[h] = (st * jnp.exp(cum[T - 1:T, :])
                         + lax.dot_general(vb, kd, tn, preferred_element_type=f32))
            o = o * lax.rsqrt(jnp.mean(o * o, axis=-1, keepdims=True) + LN_EPS) * g_ref[...]
            y_ref[0, i * T:(i + 1) * T, sl] = (o * gate[:, sl]).astype(bf16)

    nxt = prepare(0)
    for i in range(SUBS):
        cur = nxt
        if i + 1 < SUBS:
            nxt = prepare(i + 1)
        recur(i, cur)

    st_out_ref[0] = st_ref[...]


def _hgrn(x, w_h, lb, norm_g, s0):
    B, S, _ = x.shape
    T = _row_tile(S, 2 * CHUNK)
    SUBS = HGRN_SUBTILES if S % (T * HGRN_SUBTILES) == 0 else 1
    TS = T * SUBS
    mats = jnp.asarray(_hgrn_mats(T), bf16)
    nm = mats.shape[0]
    st0 = jnp.swapaxes(s0.astype(f32), 2, 3)
    y, st = pl.pallas_call(
        functools.partial(_hgrn_kernel, T=T, SUBS=SUBS),
        grid=(B, S // TS),
        in_specs=[pl.BlockSpec((1, TS, D_MODEL), lambda b, t: (b, t, 0)),
                  _const_spec((D_MODEL, 4 * HGRN_W)),
                  _const_spec((1, HGRN_W)),
                  _const_spec((1, HGRN_DV)),
                  _const_spec((nm, T)),
                  pl.BlockSpec((1, HGRN_HEADS, HGRN_DV, HGRN_DK), lambda b, t: (b, 0, 0, 0))],
        out_specs=[pl.BlockSpec((1, TS, HGRN_W), lambda b, t: (b, t, 0)),
                   pl.BlockSpec((1, HGRN_HEADS, HGRN_DV, HGRN_DK), lambda b, t: (b, 0, 0, 0))],
        out_shape=[jax.ShapeDtypeStruct((B, S, HGRN_W), bf16),
                   jax.ShapeDtypeStruct((B, HGRN_HEADS, HGRN_DV, HGRN_DK), f32)],
        scratch_shapes=[pltpu.VMEM((HGRN_HEADS, HGRN_DV, HGRN_DK), f32)],
        compiler_params=_params(("parallel", "arbitrary")),
        name="hgrn",
    )(x, w_h, lb, norm_g, mats, st0)
    return y, jnp.swapaxes(st, 2, 3)


HIST = CONV_W - 1
PADROWS = 8


def _conv_rows(buf_ref, cols, u, w, bias, T):
    buf_ref[PADROWS:PADROWS + T, cols] = u
    y = bias
    for j in range(CONV_W):
        y = y + w[j:j + 1, :] * buf_ref[PADROWS - HIST + j:PADROWS - HIST + j + T, cols]
    buf_ref[PADROWS - HIST:PADROWS, cols] = u[T - HIST:T, :]
    return y


def _merge_kernel(x_ref, ya_ref, yb_ref, hist_ref, wc_ref, wg_ref, wbr_ref, wout_ref, cw_ref, cb_ref,
                  g_ref, b_ref, o_ref, hist_out_ref, buf_ref, *, T, alpha):
    @pl.when(pl.program_id(1) == 0)
    def _():
        buf_ref[PADROWS - HIST:PADROWS, :] = hist_ref[0]

    x = x_ref[0]
    xb = x.astype(bf16)
    c = jnp.dot(xb, wc_ref[...], preferred_element_type=f32)
    W = SCONV_WIDTH
    u = c[:, W:2 * W] * c[:, 2 * W:3 * W]
    u_conv = _conv_rows(buf_ref, slice(0, W), u, cw_ref[...], cb_ref[...], T)
    yc = (c[:, 0:W] * u_conv).astype(bf16)
    hist_out_ref[0] = buf_ref[PADROWS - HIST:PADROWS, :]

    merged = None
    for i, yb in enumerate((ya_ref[0], yb_ref[0], yc)):
        gate = _sigmoid(jnp.dot(xb, wg_ref[:, i * D_MODEL:(i + 1) * D_MODEL], preferred_element_type=f32))
        term = gate * jnp.dot(yb, wbr_ref[i], preferred_element_type=f32)
        merged = term if merged is None else merged + term
    out = jnp.dot(merged.astype(bf16), wout_ref[...], preferred_element_type=f32)
    o_ref[0] = _layer_norm(alpha * x + out, g_ref[...], b_ref[...])


def _merge(x, ya, yb, hist, wc, wg, wbr, wout, cw, cb, g, b, alpha):
    B, S, _ = x.shape
    T = _row_tile(S, 256)
    row = lambda w: pl.BlockSpec((1, T, w), lambda bb, t: (bb, t, 0))
    hspec = pl.BlockSpec((1, HIST, SCONV_WIDTH), lambda bb, t: (bb, 0, 0))
    return pl.pallas_call(
        functools.partial(_merge_kernel, T=T, alpha=alpha),
        grid=(B, S // T),
        in_specs=[row(D_MODEL), row(BRANCH_WIDTH), row(BRANCH_WIDTH), hspec,
                  _const_spec((D_MODEL, 3 * SCONV_WIDTH)), _const_spec((D_MODEL, N_BRANCH * D_MODEL)),
                  _const_spec((N_BRANCH, BRANCH_WIDTH, D_MODEL)), _const_spec((D_MODEL, D_MODEL)),
                  _const_spec((CONV_W, SCONV_WIDTH)), _const_spec((1, SCONV_WIDTH)),
                  _const_spec((1, D_MODEL)), _const_spec((1, D_MODEL))],
        out_specs=[row(D_MODEL), hspec],
        out_shape=[jax.ShapeDtypeStruct((B, S, D_MODEL), f32),
                   jax.ShapeDtypeStruct((B, HIST, SCONV_WIDTH), f32)],
        scratch_shapes=[pltpu.VMEM((PADROWS + T, SCONV_WIDTH), f32)],
        compiler_params=_params(("parallel", "arbitrary")),
        name="merge",
    )(x, ya, yb, hist, wc, wg, wbr, wout, cw, cb, g, b)


def _ffn_kernel(x_ref, hist_ref, wup_ref, cw_ref, cb_ref, wdn_ref, g_ref, b_ref, o_ref, hist_out_ref,
                *bufs, T, alpha):
    n_steps = D_FF // FFN_COLS
    col_ranges = [slice(off + c * FFN_COLS, off + (c + 1) * FFN_COLS)
                  for c in range(n_steps) for off in (0, D_FF)]

    @pl.when(pl.program_id(1) == 0)
    def _():
        for buf, cols in zip(bufs, col_ranges):
            buf[PADROWS - HIST:PADROWS, :] = hist_ref[0, :, cols]

    x = x_ref[0]
    xb = x.astype(bf16)

    def up_conv(cidx):
        halves = []
        for half in range(2):
            buf, cols = bufs[2 * cidx + half], col_ranges[2 * cidx + half]
            h = jnp.dot(xb, wup_ref[:, cols], preferred_element_type=f32)
            halves.append(_conv_rows(buf, slice(None), h, cw_ref[:, cols], cb_ref[:, cols], T))
            hist_out_ref[0, :, cols] = buf[PADROWS - HIST:PADROWS, :]
        return halves

    acc = jnp.zeros((T, D_MODEL), f32)
    nxt = up_conv(0)
    for cidx in range(n_steps):
        a_g, b_v = nxt
        if cidx + 1 < n_steps:
            nxt = up_conv(cidx + 1)
        act = (a_g * _sigmoid(a_g) * b_v).astype(bf16)
        acc = acc + jnp.dot(act, wdn_ref[cidx * FFN_COLS:(cidx + 1) * FFN_COLS, :],
                            preferred_element_type=f32)
    o_ref[0] = _layer_norm(alpha * x + acc, g_ref[...], b_ref[...])


def _ffn(x, hist, wup, cw, cb, wdn, g, b, alpha):
    B, S, _ = x.shape
    T = _row_tile(S, 256)
    row = pl.BlockSpec((1, T, D_MODEL), lambda bb, t: (bb, t, 0))
    hspec = pl.BlockSpec((1, HIST, 2 * D_FF), lambda bb, t: (bb, 0, 0))
    return pl.pallas_call(
        functools.partial(_ffn_kernel, T=T, alpha=alpha),
        grid=(B, S // T),
        in_specs=[row, hspec, _const_spec((D_MODEL, 2 * D_FF)), _const_spec((CONV_W, 2 * D_FF)),
                  _const_spec((1, 2 * D_FF)), _const_spec((D_FF, D_MODEL)),
                  _const_spec((1, D_MODEL)), _const_spec((1, D_MODEL))],
        out_specs=[row, hspec],
        out_shape=[jax.ShapeDtypeStruct((B, S, D_MODEL), f32),
                   jax.ShapeDtypeStruct((B, HIST, 2 * D_FF), f32)],
        scratch_shapes=[pltpu.VMEM((PADROWS + T, FFN_COLS), f32)] * (2 * D_FF // FFN_COLS),
        compiler_params=_params(("parallel", "arbitrary")),
        name="ffn",
    )(x, hist, wup, cw, cb, wdn, g, b)


def _cast_kernel(x_ref, o_ref):
    o_ref[...] = x_ref[...].astype(o_ref.dtype)


def _to_bf16(w):
    D, Rw, C = w.shape
    T = _row_tile(Rw, LANES)
    spec = pl.BlockSpec((1, T, C), lambda d, t: (d, t, 0))
    return pl.pallas_call(_cast_kernel, grid=(D, Rw // T), in_specs=[spec], out_specs=spec,
                          out_shape=jax.ShapeDtypeStruct(w.shape, bf16),
                          compiler_params=_params(("parallel", "parallel")), name="to_bf16")(w)


def _layer_weights(l, depth, w_in, hgrn_lb_logits, hgrn_norm_g, sconv_w, sconv_b, w_branch, w_out,
                   ln1_g, ln1_b, w_up, ffn_conv_w, ffn_conv_b, w_down, ln2_g, ln2_b):
    offs = np.concatenate([[0], np.cumsum(IN_SIZES)]).tolist()
    sec = lambda i, j=None: w_in[l][:, offs[i]:offs[(i if j is None else j) + 1]]
    a_q, a_k, a_v, i_q, i_k, i_w = (sec(i) for i in range(6))
    zpad = lambda n: jnp.zeros((D_MODEL, n), w_in.dtype)
    w_row = jnp.concatenate([a_k, i_k, zpad(LANES - IDX_DIM), a_v], axis=1).astype(bf16)
    w_t = jnp.concatenate([a_q, i_q, a_v, i_w, zpad(WT_ROWS - 900)], axis=1).T.astype(bf16)
    lbp = jax.nn.softmax(hgrn_lb_logits.astype(f32), axis=0)
    lb = (jnp.cumsum(lbp, axis=0) - lbp[0])[l].reshape(1, HGRN_W)
    row = lambda a: a.reshape(1, -1).astype(f32)
    return dict(
        w_row=w_row, w_t=w_t, w_h=sec(6, 9).astype(bf16), lb=lb, norm_g=row(hgrn_norm_g[l]),
        w_c=sec(10, 12).astype(bf16), w_g=sec(13).astype(bf16),
        w_br=w_branch[l].astype(bf16), w_out=w_out[l].astype(bf16),
        sconv_w=sconv_w[l].astype(f32), sconv_b=row(sconv_b[l]),
        ln1_g=row(ln1_g[l]), ln1_b=row(ln1_b[l]),
        w_up=w_up[l].astype(bf16), ffn_w=ffn_conv_w[l].astype(f32), ffn_b=row(ffn_conv_b[l]),
        w_down=w_down[l].astype(bf16), ln2_g=row(ln2_g[l]), ln2_b=row(ln2_b[l]))


def _key_tiles(a, TK):
    B, L, C = a.shape
    NT = -(-L // TK)
    a = jnp.pad(a, ((0, 0), (0, NT * TK - L), (0, 0)))
    return a.reshape(B, NT, TK, C)


def _trunk_layer(x, tables, P, k_past, v_past, ki_past, s0, sc_hist, ffn_hist, w, alpha):
    B, S, _ = x.shape
    TK = KEY_TILE
    a = _attn_proj(x, w["w_row"], w["w_t"], *tables)
    k, v, ki = a["k"], a["v"], a["ki"]
    L = P + S
    if k_past is None and S % TK == 0:
        kb = a["kb"].reshape(B, S // TK, TK, -1)
        kib = a["kib"].reshape(B, S // TK, TK, -1)
        vt = a["vt"]
    else:
        kb, kib, v_all = a["kb"], a["kib"], v.astype(bf16)
        if k_past is not None:
            kb = jnp.concatenate([k_past.reshape(B, P, -1).astype(bf16), kb], axis=1)
            kib = jnp.concatenate([ki_past.astype(bf16), kib], axis=1)
            v_all = jnp.concatenate([v_past.reshape(B, P, -1).astype(bf16), v_all], axis=1)
        kb, kib = _key_tiles(kb, TK), _key_tiles(kib, TK)
        vt = _key_tiles(v_all, TK).reshape(B, -1, TK, KV_HEADS, HEAD_DIM).transpose(0, 1, 3, 4, 2)
        ones = jnp.zeros(vt.shape[:3] + (VT_ROWS - HEAD_DIM, TK), bf16).at[:, :, :, 0, :].set(1.0)
        vt = jnp.concatenate([vt, ones], axis=3)
    y_a = _dsa(a["qt"], a["qit"], a["wit"], kb, kib, vt, P, min(TOPK_MAX, L // 4))
    y_b, s_new = _hgrn(x, w["w_h"], w["lb"], w["norm_g"], s0)
    x1, sc_new = _merge(x, y_a, y_b, sc_hist, w["w_c"], w["w_g"], w["w_br"], w["w_out"],
                        w["sconv_w"], w["sconv_b"], w["ln1_g"], w["ln1_b"], alpha)
    x2, ffn_new = _ffn(x1, ffn_hist, w["w_up"], w["ffn_w"], w["ffn_b"], w["w_down"],
                       w["ln2_g"], w["ln2_b"], alpha)
    new = (k.reshape(B, S, KV_HEADS, HEAD_DIM), v.reshape(B, S, KV_HEADS, HEAD_DIM), ki,
           s_new, sc_new, ffn_new)
    return x2, new


def kernel(x_prompt, x_sample, cache_attn_k, cache_attn_v, cache_idx_k, state_hgrn, state_sconv,
           state_ffn_conv, w_in, hgrn_lb_logits, hgrn_norm_g, sconv_w, sconv_b, w_branch, w_out,
           ln1_g, ln1_b, w_up, ffn_conv_w, ffn_conv_b, w_down, ln2_g, ln2_b):
    depth = w_in.shape[0]
    alpha = (2 * depth) ** 0.25
    B, S, _ = x_prompt.shape
    DB, DS, _ = x_sample.shape
    P = cache_attn_k.shape[2]
    cs_p = _rotary_tables(jnp.arange(S))
    cs_s = _rotary_tables(P + jnp.arange(DS))
    xp, xs = x_prompt, x_sample
    st_p = [[] for _ in range(6)]
    st_s = [[] for _ in range(6)]
    w_in_b = _to_bf16(w_in)
    for l in range(depth):
        w = _layer_weights(l, depth, w_in_b, hgrn_lb_logits, hgrn_norm_g, sconv_w, sconv_b, w_branch,
                           w_out, ln1_g, ln1_b, w_up, ffn_conv_w, ffn_conv_b, w_down, ln2_g, ln2_b)
        xp, new_p = _trunk_layer(
            xp, cs_p, 0, None, None, None,
            jnp.zeros((B, HGRN_HEADS, HGRN_DK, HGRN_DV), f32),
            jnp.zeros((B, HIST, SCONV_WIDTH), f32),
            jnp.zeros((B, HIST, 2 * D_FF), f32), w, alpha)
        xs, new_s = _trunk_layer(
            xs, cs_s, P, cache_attn_k[l], cache_attn_v[l], cache_idx_k[l], state_hgrn[l],
            state_sconv[l], state_ffn_conv[l], w, alpha)
        for j in range(6):
            st_p[j].append(new_p[j])
            st_s[j].append(new_s[j])
    outs_p = [jnp.stack(a, axis=0) for a in st_p]
    outs_s = [jnp.stack(a, axis=0) for a in st_s]
    return (xp, xs, *outs_p, *outs_s)
```

```python
import functools

import numpy as np
import jax
import jax.numpy as jnp
from jax import lax
from jax.experimental import pallas as pl
from jax.experimental.pallas import tpu as pltpu

D_MODEL = 1024
CHUNK = 64
N_HEADS = 8
HEAD_DIM = 64
KV_HEADS = 2
GROUPS = N_HEADS // KV_HEADS
IDX_HEADS = 4
IDX_DIM = 64
TOPK_MAX = 256
ROPE_THETA = 500000.0
ATTN_SCALE = HEAD_DIM ** -0.5
IDX_SCALE = (IDX_DIM ** -0.5) * (IDX_HEADS ** -0.5)
NEG = -1e30
HGRN_HEADS = 4
HGRN_DK = 128
HGRN_DV = 128
HGRN_W = HGRN_HEADS * HGRN_DK
SCONV_WIDTH = 512
CONV_W = 3
BRANCH_WIDTH = 512
N_BRANCH = 3
D_FF = 2816
LN_EPS = 1e-5
IN_SIZES = (N_HEADS * HEAD_DIM, KV_HEADS * HEAD_DIM, KV_HEADS * HEAD_DIM,
            IDX_HEADS * IDX_DIM, IDX_DIM, IDX_HEADS,
            HGRN_W, HGRN_W, HGRN_W, HGRN_W,
            SCONV_WIDTH, SCONV_WIDTH, SCONV_WIDTH,
            N_BRANCH * D_MODEL)

LANES = 128
VMEM_LIMIT = 56 * 1024 * 1024
KEY_TILE = 512
QUERY_TILE = 256
VT_ROWS = HEAD_DIM + 16
FFN_COLS = 256
HGRN_SUBTILES = 4
PACK = 4
DIGITS = ((25, 7), (18, 7), (11, 7), (4, 7), (0, 4))
GUARD = np.int32(-0x7F7F7F80)
ONES = np.int32(0x01010101)

f32 = jnp.float32
bf16 = jnp.bfloat16
i32 = jnp.int32


def _row_tile(s, want):
    return want if s % want == 0 else s


def _const_spec(shape):
    nd = len(shape)
    return pl.BlockSpec(shape, lambda *_: (0,) * nd, pipeline_mode=pl.Buffered(1))


def _params(sem):
    return pltpu.CompilerParams(dimension_semantics=sem, vmem_limit_bytes=VMEM_LIMIT)


def _layer_norm(z, g, b):
    mu = jnp.mean(z, axis=-1, keepdims=True)
    d = z - mu
    var = jnp.mean(d * d, axis=-1, keepdims=True)
    return d * lax.rsqrt(var + LN_EPS) * g + b


def _sigmoid(x):
    return 1.0 / (1.0 + jnp.exp(-x))


Q_PRESCALE = ATTN_SCALE * float(np.log2(np.e))
ROT = HEAD_DIM // 4
WR_COLS = 384
WT_ROWS = 912
WI_ROWS = 8


def _attn_proj_kernel(x_ref, wr_ref, wt_ref, cs_ref, cst_ref, k_ref, v_ref, ki_ref, kb_ref, kib_ref,
                      vt_ref, qt_ref, qit_ref, wit_ref, *, T, TQ):
    xb = x_ref[0].astype(bf16)
    pr = jnp.dot(xb, wr_ref[...], preferred_element_type=f32)
    c, sa, sb = cs_ref[0], cs_ref[1], cs_ref[2]

    def rot(xg):
        return xg * c + pltpu.roll(xg, LANES - ROT // 2, 1) * sa + pltpu.roll(xg, ROT // 2, 1) * sb

    k = rot(pr[:, 0:LANES])
    ki = rot(pr[:, LANES:2 * LANES])[:, :IDX_DIM]
    k_ref[0] = k
    kb_ref[0] = k.astype(bf16)
    ki_ref[0] = ki
    kib_ref[0] = ki.astype(bf16)
    v_ref[0] = pr[:, 2 * LANES:3 * LANES]

    nt = (((1,), (1,)), ((), ()))
    pt = lax.dot_general(wt_ref[...], xb, nt, preferred_element_type=f32)
    ct, st = cst_ref[0], cst_ref[1]

    def rot_t(hb):
        lead = hb[0:ROT] * ct + jnp.concatenate([hb[ROT // 2:ROT], hb[0:ROT // 2]], axis=0) * st
        return jnp.concatenate([lead, hb[ROT:HEAD_DIM]], axis=0)

    nqb = T // TQ
    for h in range(N_HEADS):
        hb = (rot_t(pt[h * HEAD_DIM:(h + 1) * HEAD_DIM]) * Q_PRESCALE).astype(bf16)
        g, i = divmod(h, GROUPS)
        for n in range(nqb):
            qt_ref[0, n, g, :, i * TQ:(i + 1) * TQ] = hb[:, n * TQ:(n + 1) * TQ]
    base = N_HEADS * HEAD_DIM
    for h in range(IDX_HEADS):
        hb = rot_t(pt[base + h * IDX_DIM:base + (h + 1) * IDX_DIM]).astype(bf16)
        for n in range(nqb):
            qit_ref[0, n, :, h * TQ:(h + 1) * TQ] = hb[:, n * TQ:(n + 1) * TQ]
    base += IDX_HEADS * IDX_DIM
    ones_rows = jnp.where(lax.broadcasted_iota(i32, (VT_ROWS - HEAD_DIM, T), 0) == 0, 1.0, 0.0)
    for g in range(KV_HEADS):
        vt_ref[0, 0, g, 0:HEAD_DIM, :] = pt[base + g * HEAD_DIM:base + (g + 1) * HEAD_DIM].astype(bf16)
        vt_ref[0, 0, g, HEAD_DIM:VT_ROWS, :] = ones_rows.astype(bf16)
    base += KV_HEADS * HEAD_DIM
    for n in range(nqb):
        wit_ref[0, n] = pt[base:base + WI_ROWS, n * TQ:(n + 1) * TQ]


def _attn_proj(x, w_row, w_t, cs, cst):
    B, S, _ = x.shape
    T = _row_tile(S, KEY_TILE)
    TQ = _row_tile(S, QUERY_TILE)
    nqb = T // TQ
    R = GROUPS * TQ
    row = lambda w: pl.BlockSpec((1, T, w), lambda b, t: (b, t, 0))
    names = ("k", "v", "ki", "kb", "kib", "vt", "qt", "qit", "wit")
    outs = pl.pallas_call(
        functools.partial(_attn_proj_kernel, T=T, TQ=TQ),
        grid=(B, S // T),
        in_specs=[row(D_MODEL), _const_spec((D_MODEL, WR_COLS)), _const_spec((WT_ROWS, D_MODEL)),
                  pl.BlockSpec((3, T, LANES), lambda b, t: (0, t, 0)),
                  pl.BlockSpec((2, ROT, T), lambda b, t: (0, 0, t))],
        out_specs=[row(128), row(128), row(IDX_DIM), row(128), row(IDX_DIM),
                   pl.BlockSpec((1, 1, KV_HEADS, VT_ROWS, T), lambda b, t: (b, t, 0, 0, 0)),
                   pl.BlockSpec((1, nqb, KV_HEADS, HEAD_DIM, R), lambda b, t: (b, t, 0, 0, 0)),
                   pl.BlockSpec((1, nqb, IDX_DIM, IDX_HEADS * TQ), lambda b, t: (b, t, 0, 0)),
                   pl.BlockSpec((1, nqb, WI_ROWS, TQ), lambda b, t: (b, t, 0, 0))],
        out_shape=[jax.ShapeDtypeStruct((B, S, 128), f32),
                   jax.ShapeDtypeStruct((B, S, 128), f32),
                   jax.ShapeDtypeStruct((B, S, IDX_DIM), f32),
                   jax.ShapeDtypeStruct((B, S, 128), bf16),
                   jax.ShapeDtypeStruct((B, S, IDX_DIM), bf16),
                   jax.ShapeDtypeStruct((B, S // T, KV_HEADS, VT_ROWS, T), bf16),
                   jax.ShapeDtypeStruct((B, S // TQ, KV_HEADS, HEAD_DIM, R), bf16),
                   jax.ShapeDtypeStruct((B, S // TQ, IDX_DIM, IDX_HEADS * TQ), bf16),
                   jax.ShapeDtypeStruct((B, S // TQ, WI_ROWS, TQ), f32)],
        compiler_params=_params(("parallel", "parallel")),
        name="attn_proj",
    )(x, w_row, w_t, cs, cst)
    return dict(zip(names, outs))


def _rotary_tables(pos):
    half = ROT // 2
    inv_freq = ROPE_THETA ** (-jnp.arange(half, dtype=f32) / half)
    ang = pos.astype(f32)[:, None] * inv_freq[None, :]
    cos = jnp.cos(ang)
    sin = jnp.sin(ang)
    S = pos.shape[0]
    one = jnp.ones((S, HEAD_DIM - ROT), f32)
    zero = jnp.zeros((S, HEAD_DIM - ROT), f32)
    zh = jnp.zeros((S, half), f32)
    c = jnp.concatenate([cos, cos, one], axis=1)
    sa = jnp.concatenate([-sin, zh, zero], axis=1)
    sb = jnp.concatenate([zh, sin, zero], axis=1)
    tile2 = lambda a: jnp.concatenate([a, a], axis=1)
    cs = jnp.stack([tile2(c), tile2(sa), tile2(sb)], axis=0)
    cst = jnp.stack([jnp.concatenate([cos, cos], axis=1).T,
                     jnp.concatenate([-sin, sin], axis=1).T], axis=0)
    return cs, cst


def _dsa_kernel(qt_ref, qit_ref, wit_ref, ki_ref, k_ref, vt_ref, tri_ref, o_ref,
                s_ref, pk_ref, qpad_ref, m_ref, acc_ref, *bufs, TQ, TK, P, top):
    lm_refs = (bufs[0:KV_HEADS], bufs[KV_HEADS:2 * KV_HEADS])
    p_refs = (bufs[2 * KV_HEADS:3 * KV_HEADS], bufs[3 * KV_HEADS:4 * KV_HEADS])
    qb = pl.program_id(1)
    q0 = P + qb * TQ
    nk = (q0 + TQ + TK - 1) // TK
    n_full = q0 // TK
    NC = TK // LANES
    R = GROUPS * TQ
    q_lim = ((q0 + lax.broadcasted_iota(i32, (1, TQ), 1)) // CHUNK + 1) * CHUNK
    wit = wit_ref[0, 0] * IDX_SCALE
    zero_head = jnp.zeros((HEAD_DIM, R), bf16)
    qpad_ref[0] = jnp.concatenate([qt_ref[0, 0, 0], zero_head], axis=0)
    qpad_ref[1] = jnp.concatenate([zero_head, qt_ref[0, 0, 1]], axis=0)

    TKP = TK // PACK

    def lead_digit(key):
        return (key >> DIGITS[0][0]) + (1 << (DIGITS[0][1] - 1))

    def score_tile(masked, j, carry):
        for c in range(NC):
            rows = slice(c * LANES, (c + 1) * LANES)
            rel = jnp.maximum(jnp.dot(ki_ref[0, j, rows, :], qit_ref[0, 0], preferred_element_type=f32), 0.0)
            score = wit[0:1, :] * rel[:, 0:TQ]
            for h in range(1, IDX_HEADS):
                score = score + wit[h:h + 1, :] * rel[:, h * TQ:(h + 1) * TQ]
            if masked:
                k_pos = j * TK + c * LANES + lax.broadcasted_iota(i32, (LANES, TQ), 0)
                score = jnp.where(k_pos < q_lim, score, NEG)
            bits = pltpu.bitcast(score, i32)
            key = jnp.where(bits >= 0, bits, bits ^ 0x7FFFFFFF)
            s_ref[j, rows, :] = key
            if c == 0:
                pk_ref[j] = lead_digit(key) | GUARD
            else:
                pk_ref[j] = pk_ref[j] | (lead_digit(key) << (8 * c))
        return carry

    lax.fori_loop(0, n_full, functools.partial(score_tile, False), 0)
    lax.fori_loop(n_full, nk, functools.partial(score_tile, True), 0)

    n_chunks = (nk + PACK - 1) // PACK

    def fill_guard(j, carry):
        pk_ref[j] = jnp.full((TKP, TQ), GUARD, i32)
        return carry

    lax.fori_loop(nk, n_chunks * PACK, fill_guard, 0)

    def count_ge(cand):
        cvec = cand * ONES

        def chunk(ci, tot):
            accs = [None] * 4
            for tt in range(PACK):
                for r in range(TKP // 8):
                    slab = pk_ref[ci * PACK + tt, r * 8:(r + 1) * 8, :]
                    hit = lax.shift_right_logical(slab - cvec, 7) & ONES
                    n = (tt * (TKP // 8) + r) % 4
                    accs[n] = hit if accs[n] is None else accs[n] + hit
            a = (accs[0] + accs[1]) + (accs[2] + accs[3])
            return tot + ((a & 0xFF) + ((a >> 8) & 0xFF) + ((a >> 16) & 0xFF)
                          + lax.shift_right_logical(a, 24))

        tot = lax.fori_loop(0, n_chunks, chunk, jnp.zeros((8, TQ), i32))
        return jnp.sum(tot, axis=0, keepdims=True)

    def repack(lo, width, prefix):
        origin = prefix * (1 << width)

        def tile(j, carry):
            for r in range(TKP // 8):
                word = jnp.full((8, TQ), GUARD, i32)
                for q in range(PACK):
                    key = s_ref[j, q * TKP + r * 8:q * TKP + (r + 1) * 8, :]
                    d = (key >> lo) - origin
                    in_range = pltpu.bitcast(d, jnp.uint32) < jnp.uint32(1 << width)
                    word = word | (jnp.where(in_range, d, 0) << (8 * q))
                pk_ref[j, r * 8:(r + 1) * 8, :] = word
            return carry
        lax.fori_loop(0, nk, tile, 0)

    base = jnp.zeros((1, TQ), i32)
    bsize = jnp.full((1, TQ), nk * TK, i32)
    prefix = None
    for stage, (lo, width) in enumerate(DIGITS):
        if stage > 0:
            repack(lo, width, prefix)
        need = top - base

        def bit_step(i, st, need=need, width=width):
            v, c_v, c_rej = st
            cand = v | jnp.left_shift(jnp.int32(1), width - 1 - i)
            c = count_ge(cand)
            ok = c >= need
            return jnp.where(ok, cand, v), jnp.where(ok, c, c_v), jnp.where(ok, c_rej, c)

        zero = jnp.zeros((1, TQ), i32)
        v, c_v, c_gt = lax.fori_loop(0, width, bit_step, (zero, bsize, zero))
        base, bsize = base + c_gt, c_v - c_gt
        prefix = v - (1 << (width - 1)) if stage == 0 else prefix * (1 << width) + v
    t = prefix
    neg_key = np.array(NEG, np.float32).view(np.int32) ^ 0x7FFFFFFF
    any_excess = jnp.max(jnp.where(base + bsize > top, 1, 0)) > 0

    @pl.when(jnp.logical_not(any_excess))
    def _():
        thr = jnp.where(t == neg_key, t + 1, t)

        def fast(j, carry):
            s_ref[j] = pltpu.bitcast(jnp.where(s_ref[j] >= thr, 0.0, NEG), i32)
            return carry
        lax.fori_loop(0, nk, fast, 0)

    @pl.when(any_excess)
    def _():
        need = jnp.where(t == neg_key, 0, top - base).astype(f32)

        def ranked(j, taken):
            keys = s_ref[j]
            for c in range(NC):
                kc = keys[c * LANES:(c + 1) * LANES, :]
                eq = jnp.where(kc == t, 1.0, 0.0)
                rank = taken + jnp.dot(tri_ref[...], eq.astype(bf16), preferred_element_type=f32)
                pick = jnp.where(kc > t, 1.0, jnp.where(rank <= need, eq, 0.0))
                s_ref[j, c * LANES:(c + 1) * LANES, :] = pltpu.bitcast(
                    jnp.where(pick > 0.5, 0.0, NEG), i32)
                taken = taken + jnp.sum(eq, axis=0, keepdims=True)
            return taken
        lax.fori_loop(0, nk, ranked, jnp.zeros((1, TQ), f32))

    m_ref[...] = jnp.full(m_ref.shape, NEG, f32)
    acc_ref[...] = jnp.zeros(acc_ref.shape, f32)

    def tile_step(nxt, cur):
        if cur is not None:
            j, slot, tile_max = cur
            m_old = [m_ref[g] for g in range(KV_HEADS)]
            m_new = [jnp.maximum(m_old[g], tile_max[g]) for g in range(KV_HEADS)]
            for g in range(KV_HEADS):
                m_ref[g] = m_new[g]
        if nxt is not None:
            jn, slot_n = nxt
            jc = jnp.minimum(jn, nk - 1)
        out = []
        for g in range(KV_HEADS):
            cmax = jnp.full((8, R), NEG, f32)
            for c in range(NC):
                rows = slice(c * LANES, (c + 1) * LANES)
                if nxt is not None:
                    bias = pltpu.bitcast(s_ref[jc, rows, :], f32)
                    lm = (jnp.dot(k_ref[0, jc, rows, :], qpad_ref[g], preferred_element_type=f32)
                          + jnp.concatenate([bias] * GROUPS, axis=1))
                    lm_refs[slot_n][g][rows, :] = lm
                    cmax = jnp.maximum(cmax, jnp.max(lm.reshape(LANES // 8, 8, R), axis=0))
                if cur is not None:
                    p_refs[slot][g][rows, :] = jnp.exp2(lm_refs[slot][g][rows, :] - m_new[g]).astype(bf16)
            out.append(jnp.max(cmax, axis=0, keepdims=True))
            if cur is not None:
                acc_ref[g] = (jnp.exp2(m_old[g] - m_new[g]) * acc_ref[g]
                              + jnp.dot(vt_ref[0, j, g], p_refs[slot][g][...], preferred_element_type=f32))
        return tuple(out)

    def tile_pair(i, tile_max):
        nxt_max = tile_step((2 * i + 1, 1), (2 * i, 0, tile_max))
        return tile_step((2 * i + 2, 0), (2 * i + 1, 1, nxt_max))

    last_max = lax.fori_loop(0, nk // 2, tile_pair, tile_step((0, 0), None))

    @pl.when(nk % 2 == 1)
    def _():
        tile_step(None, (nk - 1, 0, last_max))

    for g in range(KV_HEADS):
        o_t = acc_ref[g, 0:HEAD_DIM, :] / acc_ref[g, HEAD_DIM:HEAD_DIM + 1, :]
        for i in range(0, GROUPS, 2):
            pair = jnp.concatenate([o_t[:, i * TQ:(i + 1) * TQ], o_t[:, (i + 1) * TQ:(i + 2) * TQ]], axis=0)
            h = g * GROUPS + i
            o_ref[0, :, h * HEAD_DIM:(h + 2) * HEAD_DIM] = pair.T.astype(bf16)


def _dsa(qt, qit, wit, kb, kib, vt, P, top):
    B, nq = qt.shape[:2]
    R = qt.shape[-1]
    TQ = R // GROUPS
    NT, TK = kb.shape[1:3]
    assert top <= TK and TQ % CHUNK == 0
    tri = jnp.asarray(np.tril(np.ones((LANES, LANES), np.float32)), bf16)
    return pl.pallas_call(
        functools.partial(_dsa_kernel, TQ=TQ, TK=TK, P=P, top=top),
        grid=(B, nq),
        in_specs=[pl.BlockSpec((1, 1, KV_HEADS, HEAD_DIM, R), lambda b, i: (b, i, 0, 0, 0)),
                  pl.BlockSpec((1, 1, IDX_DIM, IDX_HEADS * TQ), lambda b, i: (b, i, 0, 0)),
                  pl.BlockSpec((1, 1, WI_ROWS, TQ), lambda b, i: (b, i, 0, 0)),
                  pl.BlockSpec((1, NT, TK, IDX_DIM), lambda b, i: (b, 0, 0, 0)),
                  pl.BlockSpec((1, NT, TK, KV_HEADS * HEAD_DIM), lambda b, i: (b, 0, 0, 0)),
                  pl.BlockSpec((1, NT, KV_HEADS, VT_ROWS, TK), lambda b, i: (b, 0, 0, 0, 0)),
                  _const_spec((LANES, LANES))],
        out_specs=pl.BlockSpec((1, TQ, N_HEADS * HEAD_DIM), lambda b, i: (b, i, 0)),
        out_shape=jax.ShapeDtypeStruct((B, nq * TQ, N_HEADS * HEAD_DIM), bf16),
        scratch_shapes=[pltpu.VMEM((NT, TK, TQ), i32),
                        pltpu.VMEM((-(-NT // PACK) * PACK, TK // PACK, TQ), i32),
                        pltpu.VMEM((KV_HEADS, KV_HEADS * HEAD_DIM, R), bf16),
                        pltpu.VMEM((KV_HEADS, 1, R), f32),
                        pltpu.VMEM((KV_HEADS, VT_ROWS, R), f32)]
        + [pltpu.VMEM((TK, R), f32)] * (2 * KV_HEADS)
        + [pltpu.VMEM((TK, R), bf16)] * (2 * KV_HEADS),
        compiler_params=_params(("parallel", "arbitrary")),
        name="dsa",
    )(qt, qit, wit, kib, kb, vt, tri)


def _hgrn_mats(T):
    r = np.arange(T)[:, None]
    c = np.arange(T)[None, :]
    mats = [c <= r, c > r]
    b = T // 2
    while b >= 1:
        blk_r = r // (2 * b)
        mid = blk_r * 2 * b + b
        same = blk_r == c // (2 * b)
        late = r >= mid
        mats.append(same & ((late & (c >= mid) & (c <= r)) | (~late & (c > r) & (c < mid))))
        b //= 2
    return np.concatenate(mats, axis=0).astype(np.float32)


def _hgrn_kernel(x_ref, w_ref, lb_ref, g_ref, mats_ref, s0_ref, y_ref, st_out_ref, st_ref, *, T, SUBS):
    t_idx = pl.program_id(1)

    @pl.when(t_idx == 0)
    def _():
        st_ref[...] = s0_ref[0]

    lb = lb_ref[...]
    mats = mats_ref[...]
    row = lax.broadcasted_iota(i32, (T, 1), 0)
    col = lax.broadcasted_iota(i32, (1, T), 1)
    nt = (((1,), (1,)), ((), ()))
    tn = (((0,), (0,)), ((), ()))

    def prepare(i):
        xb = x_ref[0, i * T:(i + 1) * T, :].astype(bf16)
        proj = jnp.dot(xb, w_ref[...], preferred_element_type=f32)
        hq = proj[:, 0:HGRN_W]
        z = proj[:, HGRN_W:2 * HGRN_W]
        hv = proj[:, 2 * HGRN_W:3 * HGRN_W]
        hg = proj[:, 3 * HGRN_W:4 * HGRN_W]
        logf = (jnp.minimum(z, 0.0) - jnp.log1p(jnp.exp(-jnp.abs(z)))) + jnp.log1p(lb * jnp.exp(-z))
        kk = (1.0 - lb) * _sigmoid(-z)
        qq = hq * _sigmoid(hq)
        gate = hg * _sigmoid(hg)
        hi = logf.astype(bf16)
        r1 = logf - hi.astype(f32)
        mid = r1.astype(bf16)
        lo = (r1 - mid.astype(f32)).astype(bf16)
        e_all = (jnp.dot(mats, hi, preferred_element_type=f32)
                 + jnp.dot(mats, mid, preferred_element_type=f32)
                 + jnp.dot(mats, lo, preferred_element_type=f32))
        return qq, kk, hv, gate, e_all

    def recur(i, prepared):
        qq, kk, hv, gate, e_all = prepared
        for h in range(HGRN_HEADS):
            sl = slice(h * HGRN_DK, (h + 1) * HGRN_DK)
            q_h, k_h, v_h = qq[:, sl], kk[:, sl], hv[:, sl]
            vb = v_h.astype(bf16)
            cum = e_all[0:T, sl]
            suf = e_all[T:2 * T, sl]
            st = st_ref[h]
            o = lax.dot_general((q_h * jnp.exp(cum)).astype(bf16), st.astype(bf16), nt,
                                preferred_element_type=f32)
            o = o + jnp.sum(q_h * k_h, axis=-1, keepdims=True) * v_h
            scores = jnp.zeros((T, T), f32)
            b = T // 2
            lvl = 2
            while b >= 1:
                xdec = jnp.exp(e_all[lvl * T:(lvl + 1) * T, sl])
                late = ((row // b) % 2) == 1
                a_m = jnp.where(late, q_h * xdec, 0.0).astype(bf16)
                b_m = jnp.where(late, 0.0, k_h * xdec).astype(bf16)
                sc = lax.dot_general(a_m, b_m, nt, preferred_element_type=f32)
                if 2 * b < T:
                    sc = jnp.where((row // (2 * b)) == (col // (2 * b)), sc, 0.0)
                scores = scores + sc
                b //= 2
                lvl += 1
            o = o + jnp.dot(scores.astype(bf16), vb, preferred_element_type=f32)
            kd = (k_h * jnp.exp(suf)).astype(bf16)
            st_ref[h] = (st * jnp.exp(cum[T - 1:T, :])
                         + lax.dot_general(vb, kd, tn, preferred_element_type=f32))
            o = o * lax.rsqrt(jnp.mean(o * o, axis=-1, keepdims=True) + LN_EPS) * g_ref[...]
            y_ref[0, i * T:(i + 1) * T, sl] = (o * gate[:, sl]).astype(bf16)

    nxt = prepare(0)
    for i in range(SUBS):
        cur = nxt
        if i + 1 < SUBS:
            nxt = prepare(i + 1)
        recur(i, cur)

    st_out_ref[0] = st_ref[...]


def _hgrn(x, w_h, lb, norm_g, s0):
    B, S, _ = x.shape
    T = _row_tile(S, 2 * CHUNK)
    SUBS = HGRN_SUBTILES if S % (T * HGRN_SUBTILES) == 0 else 1
    TS = T * SUBS
    mats = jnp.asarray(_hgrn_mats(T), bf16)
    nm = mats.shape[0]
    st0 = jnp.swapaxes(s0.astype(f32), 2, 3)
    y, st = pl.pallas_call(
        functools.partial(_hgrn_kernel, T=T, SUBS=SUBS),
        grid=(B, S // TS),
        in_specs=[pl.BlockSpec((1, TS, D_MODEL), lambda b, t: (b, t, 0)),
                  _const_spec((D_MODEL, 4 * HGRN_W)),
                  _const_spec((1, HGRN_W)),
                  _const_spec((1, HGRN_DV)),
                  _const_spec((nm, T)),
                  pl.BlockSpec((1, HGRN_HEADS, HGRN_DV, HGRN_DK), lambda b, t: (b, 0, 0, 0))],
        out_specs=[pl.BlockSpec((1, TS, HGRN_W), lambda b, t: (b, t, 0)),
                   pl.BlockSpec((1, HGRN_HEADS, HGRN_DV, HGRN_DK), lambda b, t: (b, 0, 0, 0))],
        out_shape=[jax.ShapeDtypeStruct((B, S, HGRN_W), bf16),
                   jax.ShapeDtypeStruct((B, HGRN_HEADS, HGRN_DV, HGRN_DK), f32)],
        scratch_shapes=[pltpu.VMEM((HGRN_HEADS, HGRN_DV, HGRN_DK), f32)],
        compiler_params=_params(("parallel", "arbitrary")),
        name="hgrn",
    )(x, w_h, lb, norm_g, mats, st0)
    return y, jnp.swapaxes(st, 2, 3)


HIST = CONV_W - 1
PADROWS = 8


def _conv_rows(buf_ref, cols, u, w, bias, T):
    buf_ref[PADROWS:PADROWS + T, cols] = u
    y = bias
    for j in range(CONV_W):
        y = y + w[j:j + 1, :] * buf_ref[PADROWS - HIST + j:PADROWS - HIST + j + T, cols]
    buf_ref[PADROWS - HIST:PADROWS, cols] = u[T - HIST:T, :]
    return y


def _merge_kernel(x_ref, ya_ref, yb_ref, hist_ref, wc_ref, wg_ref, wbr_ref, wout_ref, cw_ref, cb_ref,
                  g_ref, b_ref, o_ref, hist_out_ref, buf_ref, *, T, alpha):
    @pl.when(pl.program_id(1) == 0)
    def _():
        buf_ref[PADROWS - HIST:PADROWS, :] = hist_ref[0]

    x = x_ref[0]
    xb = x.astype(bf16)
    c = jnp.dot(xb, wc_ref[...], preferred_element_type=f32)
    W = SCONV_WIDTH
    u = c[:, W:2 * W] * c[:, 2 * W:3 * W]
    u_conv = _conv_rows(buf_ref, slice(0, W), u, cw_ref[...], cb_ref[...], T)
    yc = (c[:, 0:W] * u_conv).astype(bf16)
    hist_out_ref[0] = buf_ref[PADROWS - HIST:PADROWS, :]

    merged = None
    for i, yb in enumerate((ya_ref[0], yb_ref[0], yc)):
        gate = _sigmoid(jnp.dot(xb, wg_ref[:, i * D_MODEL:(i + 1) * D_MODEL], preferred_element_type=f32))
        term = gate * jnp.dot(yb, wbr_ref[i], preferred_element_type=f32)
        merged = term if merged is None else merged + term
    out = jnp.dot(merged.astype(bf16), wout_ref[...], preferred_element_type=f32)
    o_ref[0] = _layer_norm(alpha * x + out, g_ref[...], b_ref[...])


def _merge(x, ya, yb, hist, wc, wg, wbr, wout, cw, cb, g, b, alpha):
    B, S, _ = x.shape
    T = _row_tile(S, 256)
    row = lambda w: pl.BlockSpec((1, T, w), lambda bb, t: (bb, t, 0))
    hspec = pl.BlockSpec((1, HIST, SCONV_WIDTH), lambda bb, t: (bb, 0, 0))
    return pl.pallas_call(
        functools.partial(_merge_kernel, T=T, alpha=alpha),
        grid=(B, S // T),
        in_specs=[row(D_MODEL), row(BRANCH_WIDTH), row(BRANCH_WIDTH), hspec,
                  _const_spec((D_MODEL, 3 * SCONV_WIDTH)), _const_spec((D_MODEL, N_BRANCH * D_MODEL)),
                  _const_spec((N_BRANCH, BRANCH_WIDTH, D_MODEL)), _const_spec((D_MODEL, D_MODEL)),
                  _const_spec((CONV_W, SCONV_WIDTH)), _const_spec((1, SCONV_WIDTH)),
                  _const_spec((1, D_MODEL)), _const_spec((1, D_MODEL))],
        out_specs=[row(D_MODEL), hspec],
        out_shape=[jax.ShapeDtypeStruct((B, S, D_MODEL), f32),
                   jax.ShapeDtypeStruct((B, HIST, SCONV_WIDTH), f32)],
        scratch_shapes=[pltpu.VMEM((PADROWS + T, SCONV_WIDTH), f32)],
        compiler_params=_params(("parallel", "arbitrary")),
        name="merge",
    )(x, ya, yb, hist, wc, wg, wbr, wout, cw, cb, g, b)


def _ffn_kernel(x_ref, hist_ref, wup_ref, cw_ref, cb_ref, wdn_ref, g_ref, b_ref, o_ref, hist_out_ref,
                *bufs, T, alpha):
    n_steps = D_FF // FFN_COLS
    col_ranges = [slice(off + c * FFN_COLS, off + (c + 1) * FFN_COLS)
                  for c in range(n_steps) for off in (0, D_FF)]

    @pl.when(pl.program_id(1) == 0)
    def _():
        for buf, cols in zip(bufs, col_ranges):
            buf[PADROWS - HIST:PADROWS, :] = hist_ref[0, :, cols]

    x = x_ref[0]
    xb = x.astype(bf16)

    def up_conv(cidx):
        halves = []
        for half in range(2):
            buf, cols = bufs[2 * cidx + half], col_ranges[2 * cidx + half]
            h = jnp.dot(xb, wup_ref[:, cols], preferred_element_type=f32)
            halves.append(_conv_rows(buf, slice(None), h, cw_ref[:, cols], cb_ref[:, cols], T))
            hist_out_ref[0, :, cols] = buf[PADROWS - HIST:PADROWS, :]
        return halves

    groups = [list(range(c, min(c + 2, n_steps))) for c in range(0, n_steps, 2)]
    acc = jnp.zeros((T, D_MODEL), f32)
    nxt = [up_conv(c) for c in groups[0]]
    for gi, grp in enumerate(groups):
        cur = nxt
        if gi + 1 < len(groups):
            nxt = [up_conv(c) for c in groups[gi + 1]]
        act = jnp.concatenate([(a_g * _sigmoid(a_g) * b_v).astype(bf16) for a_g, b_v in cur], axis=1)
        acc = acc + jnp.dot(act, wdn_ref[grp[0] * FFN_COLS:(grp[-1] + 1) * FFN_COLS, :],
                            preferred_element_type=f32)
    o_ref[0] = _layer_norm(alpha * x + acc, g_ref[...], b_ref[...])


def _ffn(x, hist, wup, cw, cb, wdn, g, b, alpha):
    B, S, _ = x.shape
    T = _row_tile(S, 256)
    row = pl.BlockSpec((1, T, D_MODEL), lambda bb, t: (bb, t, 0))
    hspec = pl.BlockSpec((1, HIST, 2 * D_FF), lambda bb, t: (bb, 0, 0))
    return pl.pallas_call(
        functools.partial(_ffn_kernel, T=T, alpha=alpha),
        grid=(B, S // T),
        in_specs=[row, hspec, _const_spec((D_MODEL, 2 * D_FF)), _const_spec((CONV_W, 2 * D_FF)),
                  _const_spec((1, 2 * D_FF)), _const_spec((D_FF, D_MODEL)),
                  _const_spec((1, D_MODEL)), _const_spec((1, D_MODEL))],
        out_specs=[row, hspec],
        out_shape=[jax.ShapeDtypeStruct((B, S, D_MODEL), f32),
                   jax.ShapeDtypeStruct((B, HIST, 2 * D_FF), f32)],
        scratch_shapes=[pltpu.VMEM((PADROWS + T, FFN_COLS), f32)] * (2 * D_FF // FFN_COLS),
        compiler_params=_params(("parallel", "arbitrary")),
        name="ffn",
    )(x, hist, wup, cw, cb, wdn, g, b)


def _cast_kernel(x_ref, o_ref):
    o_ref[...] = x_ref[...].astype(o_ref.dtype)


def _to_bf16(w):
    D, Rw, C = w.shape
    T = _row_tile(Rw, LANES)
    spec = pl.BlockSpec((1, T, C), lambda d, t: (d, t, 0))
    return pl.pallas_call(_cast_kernel, grid=(D, Rw // T), in_specs=[spec], out_specs=spec,
                          out_shape=jax.ShapeDtypeStruct(w.shape, bf16),
                          compiler_params=_params(("parallel", "parallel")), name="to_bf16")(w)


def _layer_weights(l, depth, w_in, hgrn_lb_logits, hgrn_norm_g, sconv_w, sconv_b, w_branch, w_out,
                   ln1_g, ln1_b, w_up, ffn_conv_w, ffn_conv_b, w_down, ln2_g, ln2_b):
    offs = np.concatenate([[0], np.cumsum(IN_SIZES)]).tolist()
    sec = lambda i, j=None: w_in[l][:, offs[i]:offs[(i if j is None else j) + 1]]
    a_q, a_k, a_v, i_q, i_k, i_w = (sec(i) for i in range(6))
    zpad = lambda n: jnp.zeros((D_MODEL, n), w_in.dtype)
    w_row = jnp.concatenate([a_k, i_k, zpad(LANES - IDX_DIM), a_v], axis=1).astype(bf16)
    w_t = jnp.concatenate([a_q, i_q, a_v, i_w, zpad(WT_ROWS - 900)], axis=1).T.astype(bf16)
    lbp = jax.nn.softmax(hgrn_lb_logits.astype(f32), axis=0)
    lb = (jnp.cumsum(lbp, axis=0) - lbp[0])[l].reshape(1, HGRN_W)
    row = lambda a: a.reshape(1, -1).astype(f32)
    return dict(
        w_row=w_row, w_t=w_t, w_h=sec(6, 9).astype(bf16), lb=lb, norm_g=row(hgrn_norm_g[l]),
        w_c=sec(10, 12).astype(bf16), w_g=sec(13).astype(bf16),
        w_br=w_branch[l].astype(bf16), w_out=w_out[l].astype(bf16),
        sconv_w=sconv_w[l].astype(f32), sconv_b=row(sconv_b[l]),
        ln1_g=row(ln1_g[l]), ln1_b=row(ln1_b[l]),
        w_up=w_up[l].astype(bf16), ffn_w=ffn_conv_w[l].astype(f32), ffn_b=row(ffn_conv_b[l]),
        w_down=w_down[l].astype(bf16), ln2_g=row(ln2_g[l]), ln2_b=row(ln2_b[l]))


def _key_tiles(a, TK):
    B, L, C = a.shape
    NT = -(-L // TK)
    a = jnp.pad(a, ((0, 0), (0, NT * TK - L), (0, 0)))
    return a.reshape(B, NT, TK, C)


def _trunk_layer(x, tables, P, k_past, v_past, ki_past, s0, sc_hist, ffn_hist, w, alpha):
    B, S, _ = x.shape
    TK = KEY_TILE
    a = _attn_proj(x, w["w_row"], w["w_t"], *tables)
    k, v, ki = a["k"], a["v"], a["ki"]
    L = P + S
    if k_past is None and S % TK == 0:
        kb = a["kb"].reshape(B, S // TK, TK, -1)
        kib = a["kib"].reshape(B, S // TK, TK, -1)
        vt = a["vt"]
    else:
        kb, kib, v_all = a["kb"], a["kib"], v.astype(bf16)
        if k_past is not None:
            kb = jnp.concatenate([k_past.reshape(B, P, -1).astype(bf16), kb], axis=1)
            kib = jnp.concatenate([ki_past.astype(bf16), kib], axis=1)
            v_all = jnp.concatenate([v_past.reshape(B, P, -1).astype(bf16), v_all], axis=1)
        kb, kib = _key_tiles(kb, TK), _key_tiles(kib, TK)
        vt = _key_tiles(v_all, TK).reshape(B, -1, TK, KV_HEADS, HEAD_DIM).transpose(0, 1, 3, 4, 2)
        ones = jnp.zeros(vt.shape[:3] + (VT_ROWS - HEAD_DIM, TK), bf16).at[:, :, :, 0, :].set(1.0)
        vt = jnp.concatenate([vt, ones], axis=3)
    y_a = _dsa(a["qt"], a["qit"], a["wit"], kb, kib, vt, P, min(TOPK_MAX, L // 4))
    y_b, s_new = _hgrn(x, w["w_h"], w["lb"], w["norm_g"], s0)
    x1, sc_new = _merge(x, y_a, y_b, sc_hist, w["w_c"], w["w_g"], w["w_br"], w["w_out"],
                        w["sconv_w"], w["sconv_b"], w["ln1_g"], w["ln1_b"], alpha)
    x2, ffn_new = _ffn(x1, ffn_hist, w["w_up"], w["ffn_w"], w["ffn_b"], w["w_down"],
                       w["ln2_g"], w["ln2_b"], alpha)
    new = (k.reshape(B, S, KV_HEADS, HEAD_DIM), v.reshape(B, S, KV_HEADS, HEAD_DIM), ki,
           s_new, sc_new, ffn_new)
    return x2, new


def kernel(x_prompt, x_sample, cache_attn_k, cache_attn_v, cache_idx_k, state_hgrn, state_sconv,
           state_ffn_conv, w_in, hgrn_lb_logits, hgrn_norm_g, sconv_w, sconv_b, w_branch, w_out,
           ln1_g, ln1_b, w_up, ffn_conv_w, ffn_conv_b, w_down, ln2_g, ln2_b):
    depth = w_in.shape[0]
    alpha = (2 * depth) ** 0.25
    B, S, _ = x_prompt.shape
    DB, DS, _ = x_sample.shape
    P = cache_attn_k.shape[2]
    cs_p = _rotary_tables(jnp.arange(S))
    cs_s = _rotary_tables(P + jnp.arange(DS))
    xp, xs = x_prompt, x_sample
    st_p = [[] for _ in range(6)]
    st_s = [[] for _ in range(6)]
    w_in_b = _to_bf16(w_in)
    for l in range(depth):
        w = _layer_weights(l, depth, w_in_b, hgrn_lb_logits, hgrn_norm_g, sconv_w, sconv_b, w_branch,
                           w_out, ln1_g, ln1_b, w_up, ffn_conv_w, ffn_conv_b, w_down, ln2_g, ln2_b)
        xp, new_p = _trunk_layer(
            xp, cs_p, 0, None, None, None,
            jnp.zeros((B, HGRN_HEADS, HGRN_DK, HGRN_DV), f32),
            jnp.zeros((B, HIST, SCONV_WIDTH), f32),
            jnp.zeros((B, HIST, 2 * D_FF), f32), w, alpha)
        xs, new_s = _trunk_layer(
            xs, cs_s, P, cache_attn_k[l], cache_attn_v[l], cache_idx_k[l], state_hgrn[l],
            state_sconv[l], state_ffn_conv[l], w, alpha)
        for j in range(6):
            st_p[j].append(new_p[j])
            st_s[j].append(new_s[j])
    outs_p = [jnp.stack(a, axis=0) for a in st_p]
    outs_s = [jnp.stack(a, axis=0) for a in st_s]
    return (xp, xs, *outs_p, *outs_s)
```

```python
import functools

import numpy as np
import jax
import jax.numpy as jnp
from jax import lax
from jax.experimental import pallas as pl
from jax.experimental.pallas import tpu as pltpu

D_MODEL = 1024
CHUNK = 64
N_HEADS = 8
HEAD_DIM = 64
KV_HEADS = 2
GROUPS = N_HEADS // KV_HEADS
IDX_HEADS = 4
IDX_DIM = 64
TOPK_MAX = 256
ROPE_THETA = 500000.0
ATTN_SCALE = HEAD_DIM ** -0.5
IDX_SCALE = (IDX_DIM ** -0.5) * (IDX_HEADS ** -0.5)
NEG = -1e30
HGRN_HEADS = 4
HGRN_DK = 128
HGRN_DV = 128
HGRN_W = HGRN_HEADS * HGRN_DK
SCONV_WIDTH = 512
CONV_W = 3
BRANCH_WIDTH = 512
N_BRANCH = 3
D_FF = 2816
LN_EPS = 1e-5
IN_SIZES = (N_HEADS * HEAD_DIM, KV_HEADS * HEAD_DIM, KV_HEADS * HEAD_DIM,
            IDX_HEADS * IDX_DIM, IDX_DIM, IDX_HEADS,
            HGRN_W, HGRN_W, HGRN_W, HGRN_W,
            SCONV_WIDTH, SCONV_WIDTH, SCONV_WIDTH,
            N_BRANCH * D_MODEL)

LANES = 128
VMEM_LIMIT = 56 * 1024 * 1024
KEY_TILE = 512
QUERY_TILE = 256
VT_ROWS = HEAD_DIM + 16
FFN_COLS = 256
FFN_GROUP = 5
HGRN_SUBTILES = 4
PACK = 4
DIGITS = ((25, 7), (18, 7), (11, 7), (4, 7), (0, 4))
GUARD = np.int32(-0x7F7F7F80)
ONES = np.int32(0x01010101)

f32 = jnp.float32
bf16 = jnp.bfloat16
i32 = jnp.int32


def _row_tile(s, want):
    return want if s % want == 0 else s


def _const_spec(shape):
    nd = len(shape)
    return pl.BlockSpec(shape, lambda *_: (0,) * nd, pipeline_mode=pl.Buffered(1))


def _params(sem):
    return pltpu.CompilerParams(dimension_semantics=sem, vmem_limit_bytes=VMEM_LIMIT)


def _layer_norm(z, g, b):
    mu = jnp.mean(z, axis=-1, keepdims=True)
    d = z - mu
    var = jnp.mean(d * d, axis=-1, keepdims=True)
    return d * lax.rsqrt(var + LN_EPS) * g + b


def _sigmoid(x):
    return 1.0 / (1.0 + jnp.exp(-x))


Q_PRESCALE = ATTN_SCALE * float(np.log2(np.e))
ROT = HEAD_DIM // 4
WR_COLS = 384
WT_ROWS = 912
WI_ROWS = 8


def _attn_proj_kernel(x_ref, wr_ref, wt_ref, cs_ref, cst_ref, k_ref, v_ref, ki_ref, kb_ref, kib_ref,
                      vt_ref, qt_ref, qit_ref, wit_ref, *, T, TQ):
    xb = x_ref[0].astype(bf16)
    pr = jnp.dot(xb, wr_ref[...], preferred_element_type=f32)
    c, sa, sb = cs_ref[0], cs_ref[1], cs_ref[2]

    def rot(xg):
        return xg * c + pltpu.roll(xg, LANES - ROT // 2, 1) * sa + pltpu.roll(xg, ROT // 2, 1) * sb

    k = rot(pr[:, 0:LANES])
    ki = rot(pr[:, LANES:2 * LANES])[:, :IDX_DIM]
    k_ref[0] = k
    kb_ref[0] = k.astype(bf16)
    ki_ref[0] = ki
    kib_ref[0] = ki.astype(bf16)
    v_ref[0] = pr[:, 2 * LANES:3 * LANES]

    nt = (((1,), (1,)), ((), ()))
    pt = lax.dot_general(wt_ref[...], xb, nt, preferred_element_type=f32)
    ct, st = cst_ref[0], cst_ref[1]

    def rot_t(hb):
        lead = hb[0:ROT] * ct + jnp.concatenate([hb[ROT // 2:ROT], hb[0:ROT // 2]], axis=0) * st
        return jnp.concatenate([lead, hb[ROT:HEAD_DIM]], axis=0)

    nqb = T // TQ
    for h in range(N_HEADS):
        hb = (rot_t(pt[h * HEAD_DIM:(h + 1) * HEAD_DIM]) * Q_PRESCALE).astype(bf16)
        g, i = divmod(h, GROUPS)
        for n in range(nqb):
            qt_ref[0, n, g, :, i * TQ:(i + 1) * TQ] = hb[:, n * TQ:(n + 1) * TQ]
    base = N_HEADS * HEAD_DIM
    for h in range(IDX_HEADS):
        hb = rot_t(pt[base + h * IDX_DIM:base + (h + 1) * IDX_DIM]).astype(bf16)
        for n in range(nqb):
            qit_ref[0, n, :, h * TQ:(h + 1) * TQ] = hb[:, n * TQ:(n + 1) * TQ]
    base += IDX_HEADS * IDX_DIM
    ones_rows = jnp.where(lax.broadcasted_iota(i32, (VT_ROWS - HEAD_DIM, T), 0) == 0, 1.0, 0.0)
    for g in range(KV_HEADS):
        vt_ref[0, 0, g, 0:HEAD_DIM, :] = pt[base + g * HEAD_DIM:base + (g + 1) * HEAD_DIM].astype(bf16)
        vt_ref[0, 0, g, HEAD_DIM:VT_ROWS, :] = ones_rows.astype(bf16)
    base += KV_HEADS * HEAD_DIM
    for n in range(nqb):
        wit_ref[0, n] = pt[base:base + WI_ROWS, n * TQ:(n + 1) * TQ]


def _attn_proj(x, w_row, w_t, cs, cst):
    B, S, _ = x.shape
    T = _row_tile(S, KEY_TILE)
    TQ = _row_tile(S, QUERY_TILE)
    nqb = T // TQ
    R = GROUPS * TQ
    row = lambda w: pl.BlockSpec((1, T, w), lambda b, t: (b, t, 0))
    names = ("k", "v", "ki", "kb", "kib", "vt", "qt", "qit", "wit")
    outs = pl.pallas_call(
        functools.partial(_attn_proj_kernel, T=T, TQ=TQ),
        grid=(B, S // T),
        in_specs=[row(D_MODEL), _const_spec((D_MODEL, WR_COLS)), _const_spec((WT_ROWS, D_MODEL)),
                  pl.BlockSpec((3, T, LANES), lambda b, t: (0, t, 0)),
                  pl.BlockSpec((2, ROT, T), lambda b, t: (0, 0, t))],
        out_specs=[row(128), row(128), row(IDX_DIM), row(128), row(IDX_DIM),
                   pl.BlockSpec((1, 1, KV_HEADS, VT_ROWS, T), lambda b, t: (b, t, 0, 0, 0)),
                   pl.BlockSpec((1, nqb, KV_HEADS, HEAD_DIM, R), lambda b, t: (b, t, 0, 0, 0)),
                   pl.BlockSpec((1, nqb, IDX_DIM, IDX_HEADS * TQ), lambda b, t: (b, t, 0, 0)),
                   pl.BlockSpec((1, nqb, WI_ROWS, TQ), lambda b, t: (b, t, 0, 0))],
        out_shape=[jax.ShapeDtypeStruct((B, S, 128), f32),
                   jax.ShapeDtypeStruct((B, S, 128), f32),
                   jax.ShapeDtypeStruct((B, S, IDX_DIM), f32),
                   jax.ShapeDtypeStruct((B, S, 128), bf16),
                   jax.ShapeDtypeStruct((B, S, IDX_DIM), bf16),
                   jax.ShapeDtypeStruct((B, S // T, KV_HEADS, VT_ROWS, T), bf16),
                   jax.ShapeDtypeStruct((B, S // TQ, KV_HEADS, HEAD_DIM, R), bf16),
                   jax.ShapeDtypeStruct((B, S // TQ, IDX_DIM, IDX_HEADS * TQ), bf16),
                   jax.ShapeDtypeStruct((B, S // TQ, WI_ROWS, TQ), f32)],
        compiler_params=_params(("parallel", "parallel")),
        name="attn_proj",
    )(x, w_row, w_t, cs, cst)
    return dict(zip(names, outs))


def _rotary_tables(pos):
    half = ROT // 2
    inv_freq = ROPE_THETA ** (-jnp.arange(half, dtype=f32) / half)
    ang = pos.astype(f32)[:, None] * inv_freq[None, :]
    cos = jnp.cos(ang)
    sin = jnp.sin(ang)
    S = pos.shape[0]
    one = jnp.ones((S, HEAD_DIM - ROT), f32)
    zero = jnp.zeros((S, HEAD_DIM - ROT), f32)
    zh = jnp.zeros((S, half), f32)
    c = jnp.concatenate([cos, cos, one], axis=1)
    sa = jnp.concatenate([-sin, zh, zero], axis=1)
    sb = jnp.concatenate([zh, sin, zero], axis=1)
    tile2 = lambda a: jnp.concatenate([a, a], axis=1)
    cs = jnp.stack([tile2(c), tile2(sa), tile2(sb)], axis=0)
    cst = jnp.stack([jnp.concatenate([cos, cos], axis=1).T,
                     jnp.concatenate([-sin, sin], axis=1).T], axis=0)
    return cs, cst


def _dsa_kernel(qt_ref, qit_ref, wit_ref, ki_ref, k_ref, vt_ref, tri_ref, o_ref,
                s_ref, pk_ref, qpad_ref, m_ref, acc_ref, *bufs, TQ, TK, P, top):
    lm_refs = (bufs[0:KV_HEADS], bufs[KV_HEADS:2 * KV_HEADS])
    p_refs = (bufs[2 * KV_HEADS:3 * KV_HEADS], bufs[3 * KV_HEADS:4 * KV_HEADS])
    qb = pl.program_id(1)
    q0 = P + qb * TQ
    nk = (q0 + TQ + TK - 1) // TK
    n_full = q0 // TK
    NC = TK // LANES
    R = GROUPS * TQ
    q_lim = ((q0 + lax.broadcasted_iota(i32, (1, TQ), 1)) // CHUNK + 1) * CHUNK
    wit = wit_ref[0, 0] * IDX_SCALE
    zero_head = jnp.zeros((HEAD_DIM, R), bf16)
    qpad_ref[0] = jnp.concatenate([qt_ref[0, 0, 0], zero_head], axis=0)
    qpad_ref[1] = jnp.concatenate([zero_head, qt_ref[0, 0, 1]], axis=0)

    TKP = TK // PACK

    def lead_digit(key):
        return (key >> DIGITS[0][0]) + (1 << (DIGITS[0][1] - 1))

    def score_tile(masked, j, carry):
        for c in range(NC):
            rows = slice(c * LANES, (c + 1) * LANES)
            rel = jnp.maximum(jnp.dot(ki_ref[0, j, rows, :], qit_ref[0, 0], preferred_element_type=f32), 0.0)
            score = wit[0:1, :] * rel[:, 0:TQ]
            for h in range(1, IDX_HEADS):
                score = score + wit[h:h + 1, :] * rel[:, h * TQ:(h + 1) * TQ]
            if masked:
                k_pos = j * TK + c * LANES + lax.broadcasted_iota(i32, (LANES, TQ), 0)
                score = jnp.where(k_pos < q_lim, score, NEG)
            bits = pltpu.bitcast(score, i32)
            key = jnp.where(bits >= 0, bits, bits ^ 0x7FFFFFFF)
            s_ref[j, rows, :] = key
            if c == 0:
                pk_ref[j] = lead_digit(key) | GUARD
            else:
                pk_ref[j] = pk_ref[j] | (lead_digit(key) << (8 * c))
        return carry

    lax.fori_loop(0, n_full, functools.partial(score_tile, False), 0)
    lax.fori_loop(n_full, nk, functools.partial(score_tile, True), 0)

    n_chunks = (nk + PACK - 1) // PACK

    def fill_guard(j, carry):
        pk_ref[j] = jnp.full((TKP, TQ), GUARD, i32)
        return carry

    lax.fori_loop(nk, n_chunks * PACK, fill_guard, 0)

    def count_ge(cand):
        cvec = cand * ONES

        def chunk(ci, tot):
            accs = [None] * 4
            for tt in range(PACK):
                for r in range(TKP // 8):
                    slab = pk_ref[ci * PACK + tt, r * 8:(r + 1) * 8, :]
                    hit = lax.shift_right_logical(slab - cvec, 7) & ONES
                    n = (tt * (TKP // 8) + r) % 4
                    accs[n] = hit if accs[n] is None else accs[n] + hit
            a = (accs[0] + accs[1]) + (accs[2] + accs[3])
            return tot + ((a & 0xFF) + ((a >> 8) & 0xFF) + ((a >> 16) & 0xFF)
                          + lax.shift_right_logical(a, 24))

        tot = lax.fori_loop(0, n_chunks, chunk, jnp.zeros((8, TQ), i32))
        return jnp.sum(tot, axis=0, keepdims=True)

    def repack(lo, width, prefix):
        origin = prefix * (1 << width)

        def tile(j, carry):
            for r in range(TKP // 8):
                word = jnp.full((8, TQ), GUARD, i32)
                for q in range(PACK):
                    key = s_ref[j, q * TKP + r * 8:q * TKP + (r + 1) * 8, :]
                    d = (key >> lo) - origin
                    in_range = pltpu.bitcast(d, jnp.uint32) < jnp.uint32(1 << width)
                    word = word | (jnp.where(in_range, d, 0) << (8 * q))
                pk_ref[j, r * 8:(r + 1) * 8, :] = word
            return carry
        lax.fori_loop(0, nk, tile, 0)

    base = jnp.zeros((1, TQ), i32)
    bsize = jnp.full((1, TQ), nk * TK, i32)
    prefix = None
    for stage, (lo, width) in enumerate(DIGITS):
        if stage > 0:
            repack(lo, width, prefix)
        need = top - base

        def bit_step(i, st, need=need, width=width):
            v, c_v, c_rej = st
            cand = v | jnp.left_shift(jnp.int32(1), width - 1 - i)
            c = count_ge(cand)
            ok = c >= need
            return jnp.where(ok, cand, v), jnp.where(ok, c, c_v), jnp.where(ok, c_rej, c)

        zero = jnp.zeros((1, TQ), i32)
        v, c_v, c_gt = lax.fori_loop(0, width, bit_step, (zero, bsize, zero))
        base, bsize = base + c_gt, c_v - c_gt
        prefix = v - (1 << (width - 1)) if stage == 0 else prefix * (1 << width) + v
    t = prefix
    neg_key = np.array(NEG, np.float32).view(np.int32) ^ 0x7FFFFFFF
    any_excess = jnp.max(jnp.where(base + bsize > top, 1, 0)) > 0

    @pl.when(jnp.logical_not(any_excess))
    def _():
        thr = jnp.where(t == neg_key, t + 1, t)

        def fast(j, carry):
            s_ref[j] = pltpu.bitcast(jnp.where(s_ref[j] >= thr, 0.0, NEG), i32)
            return carry
        lax.fori_loop(0, nk, fast, 0)

    @pl.when(any_excess)
    def _():
        need = jnp.where(t == neg_key, 0, top - base).astype(f32)

        def ranked(j, taken):
            keys = s_ref[j]
            for c in range(NC):
                kc = keys[c * LANES:(c + 1) * LANES, :]
                eq = jnp.where(kc == t, 1.0, 0.0)
                rank = taken + jnp.dot(tri_ref[...], eq.astype(bf16), preferred_element_type=f32)
                pick = jnp.where(kc > t, 1.0, jnp.where(rank <= need, eq, 0.0))
                s_ref[j, c * LANES:(c + 1) * LANES, :] = pltpu.bitcast(
                    jnp.where(pick > 0.5, 0.0, NEG), i32)
                taken = taken + jnp.sum(eq, axis=0, keepdims=True)
            return taken
        lax.fori_loop(0, nk, ranked, jnp.zeros((1, TQ), f32))

    m_ref[...] = jnp.full(m_ref.shape, NEG, f32)
    acc_ref[...] = jnp.zeros(acc_ref.shape, f32)

    def tile_step(nxt, cur):
        if cur is not None:
            j, slot, tile_max = cur
            m_old = [m_ref[g] for g in range(KV_HEADS)]
            m_new = [jnp.maximum(m_old[g], tile_max[g]) for g in range(KV_HEADS)]
            for g in range(KV_HEADS):
                m_ref[g] = m_new[g]
        if nxt is not None:
            jn, slot_n = nxt
            jc = jnp.minimum(jn, nk - 1)
        out = []
        for g in range(KV_HEADS):
            cmax = jnp.full((8, R), NEG, f32)
            for c in range(NC):
                rows = slice(c * LANES, (c + 1) * LANES)
                if nxt is not None:
                    bias = pltpu.bitcast(s_ref[jc, rows, :], f32)
                    lm = (jnp.dot(k_ref[0, jc, rows, :], qpad_ref[g], preferred_element_type=f32)
                          + jnp.concatenate([bias] * GROUPS, axis=1))
                    lm_refs[slot_n][g][rows, :] = lm
                    cmax = jnp.maximum(cmax, jnp.max(lm.reshape(LANES // 8, 8, R), axis=0))
                if cur is not None:
                    p_refs[slot][g][rows, :] = jnp.exp2(lm_refs[slot][g][rows, :] - m_new[g]).astype(bf16)
            out.append(jnp.max(cmax, axis=0, keepdims=True))
            if cur is not None:
                acc_ref[g] = (jnp.exp2(m_old[g] - m_new[g]) * acc_ref[g]
                              + jnp.dot(vt_ref[0, j, g], p_refs[slot][g][...], preferred_element_type=f32))
        return tuple(out)

    def tile_pair(i, tile_max):
        nxt_max = tile_step((2 * i + 1, 1), (2 * i, 0, tile_max))
        return tile_step((2 * i + 2, 0), (2 * i + 1, 1, nxt_max))

    last_max = lax.fori_loop(0, nk // 2, tile_pair, tile_step((0, 0), None))

    @pl.when(nk % 2 == 1)
    def _():
        tile_step(None, (nk - 1, 0, last_max))

    for g in range(KV_HEADS):
        o_t = acc_ref[g, 0:HEAD_DIM, :] / acc_ref[g, HEAD_DIM:HEAD_DIM + 1, :]
        for i in range(0, GROUPS, 2):
            pair = jnp.concatenate([o_t[:, i * TQ:(i + 1) * TQ], o_t[:, (i + 1) * TQ:(i + 2) * TQ]], axis=0)
            h = g * GROUPS + i
            o_ref[0, :, h * HEAD_DIM:(h + 2) * HEAD_DIM] = pair.T.astype(bf16)


def _dsa(qt, qit, wit, kb, kib, vt, P, top):
    B, nq = qt.shape[:2]
    R = qt.shape[-1]
    TQ = R // GROUPS
    NT, TK = kb.shape[1:3]
    assert top <= TK and TQ % CHUNK == 0
    tri = jnp.asarray(np.tril(np.ones((LANES, LANES), np.float32)), bf16)
    return pl.pallas_call(
        functools.partial(_dsa_kernel, TQ=TQ, TK=TK, P=P, top=top),
        grid=(B, nq),
        in_specs=[pl.BlockSpec((1, 1, KV_HEADS, HEAD_DIM, R), lambda b, i: (b, i, 0, 0, 0)),
                  pl.BlockSpec((1, 1, IDX_DIM, IDX_HEADS * TQ), lambda b, i: (b, i, 0, 0)),
                  pl.BlockSpec((1, 1, WI_ROWS, TQ), lambda b, i: (b, i, 0, 0)),
                  pl.BlockSpec((1, NT, TK, IDX_DIM), lambda b, i: (b, 0, 0, 0)),
                  pl.BlockSpec((1, NT, TK, KV_HEADS * HEAD_DIM), lambda b, i: (b, 0, 0, 0)),
                  pl.BlockSpec((1, NT, KV_HEADS, VT_ROWS, TK), lambda b, i: (b, 0, 0, 0, 0)),
                  _const_spec((LANES, LANES))],
        out_specs=pl.BlockSpec((1, TQ, N_HEADS * HEAD_DIM), lambda b, i: (b, i, 0)),
        out_shape=jax.ShapeDtypeStruct((B, nq * TQ, N_HEADS * HEAD_DIM), bf16),
        scratch_shapes=[pltpu.VMEM((NT, TK, TQ), i32),
                        pltpu.VMEM((-(-NT // PACK) * PACK, TK // PACK, TQ), i32),
                        pltpu.VMEM((KV_HEADS, KV_HEADS * HEAD_DIM, R), bf16),
                        pltpu.VMEM((KV_HEADS, 1, R), f32),
                        pltpu.VMEM((KV_HEADS, VT_ROWS, R), f32)]
        + [pltpu.VMEM((TK, R), f32)] * (2 * KV_HEADS)
        + [pltpu.VMEM((TK, R), bf16)] * (2 * KV_HEADS),
        compiler_params=_params(("parallel", "arbitrary")),
        name="dsa",
    )(qt, qit, wit, kib, kb, vt, tri)


def _hgrn_mats(T):
    r = np.arange(T)[:, None]
    c = np.arange(T)[None, :]
    mats = [c <= r, c > r]
    b = T // 2
    while b >= 1:
        blk_r = r // (2 * b)
        mid = blk_r * 2 * b + b
        same = blk_r == c // (2 * b)
        late = r >= mid
        mats.append(same & ((late & (c >= mid) & (c <= r)) | (~late & (c > r) & (c < mid))))
        b //= 2
    return np.concatenate(mats, axis=0).astype(np.float32)


def _hgrn_kernel(x_ref, w_ref, lb_ref, g_ref, mats_ref, s0_ref, y_ref, st_out_ref, st_ref, *, T, SUBS):
    t_idx = pl.program_id(1)

    @pl.when(t_idx == 0)
    def _():
        st_ref[...] = s0_ref[0]

    lb = lb_ref[...]
    mats = mats_ref[...]
    row = lax.broadcasted_iota(i32, (T, 1), 0)
    col = lax.broadcasted_iota(i32, (1, T), 1)
    nt = (((1,), (1,)), ((), ()))
    tn = (((0,), (0,)), ((), ()))

    def prepare(i):
        xb = x_ref[0, i * T:(i + 1) * T, :].astype(bf16)
        proj = jnp.dot(xb, w_ref[...], preferred_element_type=f32)
        hq = proj[:, 0:HGRN_W]
        z = proj[:, HGRN_W:2 * HGRN_W]
        hv = proj[:, 2 * HGRN_W:3 * HGRN_W]
        hg = proj[:, 3 * HGRN_W:4 * HGRN_W]
        logf = (jnp.minimum(z, 0.0) - jnp.log1p(jnp.exp(-jnp.abs(z)))) + jnp.log1p(lb * jnp.exp(-z))
        kk = (1.0 - lb) * _sigmoid(-z)
        qq = hq * _sigmoid(hq)
        gate = hg * _sigmoid(hg)
        hi = logf.astype(bf16)
        r1 = logf - hi.astype(f32)
        mid = r1.astype(bf16)
        lo = (r1 - mid.astype(f32)).astype(bf16)
        e_all = (jnp.dot(mats, hi, preferred_element_type=f32)
                 + jnp.dot(mats, mid, preferred_element_type=f32)
                 + jnp.dot(mats, lo, preferred_element_type=f32))
        return qq, kk, hv, gate, e_all

    def recur(i, prepared):
        qq, kk, hv, gate, e_all = prepared
        for h in range(HGRN_HEADS):
            sl = slice(h * HGRN_DK, (h + 1) * HGRN_DK)
            q_h, k_h, v_h = qq[:, sl], kk[:, sl], hv[:, sl]
            vb = v_h.astype(bf16)
            cum = e_all[0:T, sl]
            suf = e_all[T:2 * T, sl]
            st = st_ref[h]
            o = lax.dot_general((q_h * jnp.exp(cum)).astype(bf16), st.astype(bf16), nt,
                                preferred_element_type=f32)
            o = o + jnp.sum(q_h * k_h, axis=-1, keepdims=True) * v_h
            scores = jnp.zeros((T, T), f32)
            b = T // 2
            lvl = 2
            while b >= 1:
                xdec = jnp.exp(e_all[lvl * T:(lvl + 1) * T, sl])
                late = ((row // b) % 2) == 1
                a_m = jnp.where(late, q_h * xdec, 0.0).astype(bf16)
                b_m = jnp.where(late, 0.0, k_h * xdec).astype(bf16)
                sc = lax.dot_general(a_m, b_m, nt, preferred_element_type=f32)
                if 2 * b < T:
                    sc = jnp.where((row // (2 * b)) == (col // (2 * b)), sc, 0.0)
                scores = scores + sc
                b //= 2
                lvl += 1
            o = o + jnp.dot(scores.astype(bf16), vb, preferred_element_type=f32)
            kd = (k_h * jnp.exp(suf)).astype(bf16)
            st_ref[h] = (st * jnp.exp(cum[T - 1:T, :])
                         + lax.dot_general(vb, kd, tn, preferred_element_type=f32))
            o = o * lax.rsqrt(jnp.mean(o * o, axis=-1, keepdims=True) + LN_EPS) * g_ref[...]
            y_ref[0, i * T:(i + 1) * T, sl] = (o * gate[:, sl]).astype(bf16)

    nxt = prepare(0)
    for i in range(SUBS):
        cur = nxt
        if i + 1 < SUBS:
            nxt = prepare(i + 1)
        recur(i, cur)

    st_out_ref[0] = st_ref[...]


def _hgrn(x, w_h, lb, norm_g, s0):
    B, S, _ = x.shape
    T = _row_tile(S, 2 * CHUNK)
    SUBS = HGRN_SUBTILES if S % (T * HGRN_SUBTILES) == 0 else 1
    TS = T * SUBS
    mats = jnp.asarray(_hgrn_mats(T), bf16)
    nm = mats.shape[0]
    st0 = jnp.swapaxes(s0.astype(f32), 2, 3)
    y, st = pl.pallas_call(
        functools.partial(_hgrn_kernel, T=T, SUBS=SUBS),
        grid=(B, S // TS),
        in_specs=[pl.BlockSpec((1, TS, D_MODEL), lambda b, t: (b, t, 0)),
                  _const_spec((D_MODEL, 4 * HGRN_W)),
                  _const_spec((1, HGRN_W)),
                  _const_spec((1, HGRN_DV)),
                  _const_spec((nm, T)),
                  pl.BlockSpec((1, HGRN_HEADS, HGRN_DV, HGRN_DK), lambda b, t: (b, 0, 0, 0))],
        out_specs=[pl.BlockSpec((1, TS, HGRN_W), lambda b, t: (b, t, 0)),
                   pl.BlockSpec((1, HGRN_HEADS, HGRN_DV, HGRN_DK), lambda b, t: (b, 0, 0, 0))],
        out_shape=[jax.ShapeDtypeStruct((B, S, HGRN_W), bf16),
                   jax.ShapeDtypeStruct((B, HGRN_HEADS, HGRN_DV, HGRN_DK), f32)],
        scratch_shapes=[pltpu.VMEM((HGRN_HEADS, HGRN_DV, HGRN_DK), f32)],
        compiler_params=_params(("parallel", "arbitrary")),
        name="hgrn",
    )(x, w_h, lb, norm_g, mats, st0)
    return y, jnp.swapaxes(st, 2, 3)


HIST = CONV_W - 1
PADROWS = 8


def _conv_rows(buf_ref, cols, u, w, bias, T):
    buf_ref[PADROWS:PADROWS + T, cols] = u
    y = bias
    for j in range(CONV_W):
        y = y + w[j:j + 1, :] * buf_ref[PADROWS - HIST + j:PADROWS - HIST + j + T, cols]
    buf_ref[PADROWS - HIST:PADROWS, cols] = u[T - HIST:T, :]
    return y


def _merge_kernel(x_ref, ya_ref, yb_ref, hist_ref, wc_ref, wg_ref, wbr_ref, wout_ref, cw_ref, cb_ref,
                  g_ref, b_ref, o_ref, hist_out_ref, buf_ref, *, T, alpha):
    @pl.when(pl.program_id(1) == 0)
    def _():
        buf_ref[PADROWS - HIST:PADROWS, :] = hist_ref[0]

    x = x_ref[0]
    xb = x.astype(bf16)
    c = jnp.dot(xb, wc_ref[...], preferred_element_type=f32)
    W = SCONV_WIDTH
    u = c[:, W:2 * W] * c[:, 2 * W:3 * W]
    u_conv = _conv_rows(buf_ref, slice(0, W), u, cw_ref[...], cb_ref[...], T)
    yc = (c[:, 0:W] * u_conv).astype(bf16)
    hist_out_ref[0] = buf_ref[PADROWS - HIST:PADROWS, :]

    merged = None
    for i, yb in enumerate((ya_ref[0], yb_ref[0], yc)):
        gate = _sigmoid(jnp.dot(xb, wg_ref[:, i * D_MODEL:(i + 1) * D_MODEL], preferred_element_type=f32))
        term = gate * jnp.dot(yb, wbr_ref[i], preferred_element_type=f32)
        merged = term if merged is None else merged + term
    out = jnp.dot(merged.astype(bf16), wout_ref[...], preferred_element_type=f32)
    o_ref[0] = _layer_norm(alpha * x + out, g_ref[...], b_ref[...])


def _merge(x, ya, yb, hist, wc, wg, wbr, wout, cw, cb, g, b, alpha):
    B, S, _ = x.shape
    T = _row_tile(S, 256)
    row = lambda w: pl.BlockSpec((1, T, w), lambda bb, t: (bb, t, 0))
    hspec = pl.BlockSpec((1, HIST, SCONV_WIDTH), lambda bb, t: (bb, 0, 0))
    return pl.pallas_call(
        functools.partial(_merge_kernel, T=T, alpha=alpha),
        grid=(B, S // T),
        in_specs=[row(D_MODEL), row(BRANCH_WIDTH), row(BRANCH_WIDTH), hspec,
                  _const_spec((D_MODEL, 3 * SCONV_WIDTH)), _const_spec((D_MODEL, N_BRANCH * D_MODEL)),
                  _const_spec((N_BRANCH, BRANCH_WIDTH, D_MODEL)), _const_spec((D_MODEL, D_MODEL)),
                  _const_spec((CONV_W, SCONV_WIDTH)), _const_spec((1, SCONV_WIDTH)),
                  _const_spec((1, D_MODEL)), _const_spec((1, D_MODEL))],
        out_specs=[row(D_MODEL), hspec],
        out_shape=[jax.ShapeDtypeStruct((B, S, D_MODEL), f32),
                   jax.ShapeDtypeStruct((B, HIST, SCONV_WIDTH), f32)],
        scratch_shapes=[pltpu.VMEM((PADROWS + T, SCONV_WIDTH), f32)],
        compiler_params=_params(("parallel", "arbitrary")),
        name="merge",
    )(x, ya, yb, hist, wc, wg, wbr, wout, cw, cb, g, b)


def _ffn_kernel(x_ref, hist_ref, wup_ref, cw_ref, cb_ref, wdn_ref, g_ref, b_ref, o_ref, hist_out_ref,
                *bufs, T, alpha):
    n_steps = D_FF // FFN_COLS
    col_ranges = [slice(off + c * FFN_COLS, off + (c + 1) * FFN_COLS)
                  for c in range(n_steps) for off in (0, D_FF)]

    @pl.when(pl.program_id(1) == 0)
    def _():
        for buf, cols in zip(bufs, col_ranges):
            buf[PADROWS - HIST:PADROWS, :] = hist_ref[0, :, cols]

    x = x_ref[0]
    xb = x.astype(bf16)

    def up_conv(cidx):
        halves = []
        for half in range(2):
            buf, cols = bufs[2 * cidx + half], col_ranges[2 * cidx + half]
            h = jnp.dot(xb, wup_ref[:, cols], preferred_element_type=f32)
            halves.append(_conv_rows(buf, slice(None), h, cw_ref[:, cols], cb_ref[:, cols], T))
            hist_out_ref[0, :, cols] = buf[PADROWS - HIST:PADROWS, :]
        return halves

    groups = [list(range(c, min(c + FFN_GROUP, n_steps))) for c in range(0, n_steps, FFN_GROUP)]
    acc = jnp.zeros((T, D_MODEL), f32)
    nxt = [up_conv(c) for c in groups[0]]
    for gi, grp in enumerate(groups):
        cur = nxt
        if gi + 1 < len(groups):
            nxt = [up_conv(c) for c in groups[gi + 1]]
        act = jnp.concatenate([(a_g * _sigmoid(a_g) * b_v).astype(bf16) for a_g, b_v in cur], axis=1)
        acc = acc + jnp.dot(act, wdn_ref[grp[0] * FFN_COLS:(grp[-1] + 1) * FFN_COLS, :],
                            preferred_element_type=f32)
    o_ref[0] = _layer_norm(alpha * x + acc, g_ref[...], b_ref[...])


def _ffn(x, hist, wup, cw, cb, wdn, g, b, alpha):
    B, S, _ = x.shape
    T = _row_tile(S, 256)
    row = pl.BlockSpec((1, T, D_MODEL), lambda bb, t: (bb, t, 0))
    hspec = pl.BlockSpec((1, HIST, 2 * D_FF), lambda bb, t: (bb, 0, 0))
    return pl.pallas_call(
        functools.partial(_ffn_kernel, T=T, alpha=alpha),
        grid=(B, S // T),
        in_specs=[row, hspec, _const_spec((D_MODEL, 2 * D_FF)), _const_spec((CONV_W, 2 * D_FF)),
                  _const_spec((1, 2 * D_FF)), _const_spec((D_FF, D_MODEL)),
                  _const_spec((1, D_MODEL)), _const_spec((1, D_MODEL))],
        out_specs=[row, hspec],
        out_shape=[jax.ShapeDtypeStruct((B, S, D_MODEL), f32),
                   jax.ShapeDtypeStruct((B, HIST, 2 * D_FF), f32)],
        scratch_shapes=[pltpu.VMEM((PADROWS + T, FFN_COLS), f32)] * (2 * D_FF // FFN_COLS),
        compiler_params=_params(("parallel", "arbitrary")),
        name="ffn",
    )(x, hist, wup, cw, cb, wdn, g, b)


def _cast_kernel(x_ref, o_ref):
    o_ref[...] = x_ref[...].astype(o_ref.dtype)


def _to_bf16(w):
    D, Rw, C = w.shape
    T = _row_tile(Rw, LANES)
    spec = pl.BlockSpec((1, T, C), lambda d, t: (d, t, 0))
    return pl.pallas_call(_cast_kernel, grid=(D, Rw // T), in_specs=[spec], out_specs=spec,
                          out_shape=jax.ShapeDtypeStruct(w.shape, bf16),
                          compiler_params=_params(("parallel", "parallel")), name="to_bf16")(w)


def _layer_weights(l, depth, w_in, hgrn_lb_logits, hgrn_norm_g, sconv_w, sconv_b, w_branch, w_out,
                   ln1_g, ln1_b, w_up, ffn_conv_w, ffn_conv_b, w_down, ln2_g, ln2_b):
    offs = np.concatenate([[0], np.cumsum(IN_SIZES)]).tolist()
    sec = lambda i, j=None: w_in[l][:, offs[i]:offs[(i if j is None else j) + 1]]
    a_q, a_k, a_v, i_q, i_k, i_w = (sec(i) for i in range(6))
    zpad = lambda n: jnp.zeros((D_MODEL, n), w_in.dtype)
    w_row = jnp.concatenate([a_k, i_k, zpad(LANES - IDX_DIM), a_v], axis=1).astype(bf16)
    w_t = jnp.concatenate([a_q, i_q, a_v, i_w, zpad(WT_ROWS - 900)], axis=1).T.astype(bf16)
    lbp = jax.nn.softmax(hgrn_lb_logits.astype(f32), axis=0)
    lb = (jnp.cumsum(lbp, axis=0) - lbp[0])[l].reshape(1, HGRN_W)
    row = lambda a: a.reshape(1, -1).astype(f32)
    return dict(
        w_row=w_row, w_t=w_t, w_h=sec(6, 9).astype(bf16), lb=lb, norm_g=row(hgrn_norm_g[l]),
        w_c=sec(10, 12).astype(bf16), w_g=sec(13).astype(bf16),
        w_br=w_branch[l].astype(bf16), w_out=w_out[l].astype(bf16),
        sconv_w=sconv_w[l].astype(f32), sconv_b=row(sconv_b[l]),
        ln1_g=row(ln1_g[l]), ln1_b=row(ln1_b[l]),
        w_up=w_up[l].astype(bf16), ffn_w=ffn_conv_w[l].astype(f32), ffn_b=row(ffn_conv_b[l]),
        w_down=w_down[l].astype(bf16), ln2_g=row(ln2_g[l]), ln2_b=row(ln2_b[l]))


def _key_tiles(a, TK):
    B, L, C = a.shape
    NT = -(-L // TK)
    a = jnp.pad(a, ((0, 0), (0, NT * TK - L), (0, 0)))
    return a.reshape(B, NT, TK, C)


def _trunk_layer(x, tables, P, k_past, v_past, ki_past, s0, sc_hist, ffn_hist, w, alpha):
    B, S, _ = x.shape
    TK = KEY_TILE
    a = _attn_proj(x, w["w_row"], w["w_t"], *tables)
    k, v, ki = a["k"], a["v"], a["ki"]
    L = P + S
    if k_past is None and S % TK == 0:
        kb = a["kb"].reshape(B, S // TK, TK, -1)
        kib = a["kib"].reshape(B, S // TK, TK, -1)
        vt = a["vt"]
    else:
        kb, kib, v_all = a["kb"], a["kib"], v.astype(bf16)
        if k_past is not None:
            kb = jnp.concatenate([k_past.reshape(B, P, -1).astype(bf16), kb], axis=1)
            kib = jnp.concatenate([ki_past.astype(bf16), kib], axis=1)
            v_all = jnp.concatenate([v_past.reshape(B, P, -1).astype(bf16), v_all], axis=1)
        kb, kib = _key_tiles(kb, TK), _key_tiles(kib, TK)
        vt = _key_tiles(v_all, TK).reshape(B, -1, TK, KV_HEADS, HEAD_DIM).transpose(0, 1, 3, 4, 2)
        ones = jnp.zeros(vt.shape[:3] + (VT_ROWS - HEAD_DIM, TK), bf16).at[:, :, :, 0, :].set(1.0)
        vt = jnp.concatenate([vt, ones], axis=3)
    y_a = _dsa(a["qt"], a["qit"], a["wit"], kb, kib, vt, P, min(TOPK_MAX, L // 4))
    y_b, s_new = _hgrn(x, w["w_h"], w["lb"], w["norm_g"], s0)
    x1, sc_new = _merge(x, y_a, y_b, sc_hist, w["w_c"], w["w_g"], w["w_br"], w["w_out"],
                        w["sconv_w"], w["sconv_b"], w["ln1_g"], w["ln1_b"], alpha)
    x2, ffn_new = _ffn(x1, ffn_hist, w["w_up"], w["ffn_w"], w["ffn_b"], w["w_down"],
                       w["ln2_g"], w["ln2_b"], alpha)
    new = (k.reshape(B, S, KV_HEADS, HEAD_DIM), v.reshape(B, S, KV_HEADS, HEAD_DIM), ki,
           s_new, sc_new, ffn_new)
    return x2, new


def kernel(x_prompt, x_sample, cache_attn_k, cache_attn_v, cache_idx_k, state_hgrn, state_sconv,
           state_ffn_conv, w_in, hgrn_lb_logits, hgrn_norm_g, sconv_w, sconv_b, w_branch, w_out,
           ln1_g, ln1_b, w_up, ffn_conv_w, ffn_conv_b, w_down, ln2_g, ln2_b):
    depth = w_in.shape[0]
    alpha = (2 * depth) ** 0.25
    B, S, _ = x_prompt.shape
    DB, DS, _ = x_sample.shape
    P = cache_attn_k.shape[2]
    cs_p = _rotary_tables(jnp.arange(S))
    cs_s = _rotary_tables(P + jnp.arange(DS))
    xp, xs = x_prompt, x_sample
    st_p = [[] for _ in range(6)]
    st_s = [[] for _ in range(6)]
    w_in_b = _to_bf16(w_in)
    for l in range(depth):
        w = _layer_weights(l, depth, w_in_b, hgrn_lb_logits, hgrn_norm_g, sconv_w, sconv_b, w_branch,
                           w_out, ln1_g, ln1_b, w_up, ffn_conv_w, ffn_conv_b, w_down, ln2_g, ln2_b)
        xp, new_p = _trunk_layer(
            xp, cs_p, 0, None, None, None,
            jnp.zeros((B, HGRN_HEADS, HGRN_DK, HGRN_DV), f32),
            jnp.zeros((B, HIST, SCONV_WIDTH), f32),
            jnp.zeros((B, HIST, 2 * D_FF), f32), w, alpha)
        xs, new_s = _trunk_layer(
            xs, cs_s, P, cache_attn_k[l], cache_attn_v[l], cache_idx_k[l], state_hgrn[l],
            state_sconv[l], state_ffn_conv[l], w, alpha)
        for j in range(6):
            st_p[j].append(new_p[j])
            st_s[j].append(new_s[j])
    outs_p = [jnp.stack(a, axis=0) for a in st_p]
    outs_s = [jnp.stack(a, axis=0) for a in st_s]
    return (xp, xs, *outs_p, *outs_s)
```

```python
import functools

import numpy as np
import jax
import jax.numpy as jnp
from jax import lax
from jax.experimental import pallas as pl
from jax.experimental.pallas import tpu as pltpu

D_MODEL = 1024
CHUNK = 64
N_HEADS = 8
HEAD_DIM = 64
KV_HEADS = 2
GROUPS = N_HEADS // KV_HEADS
IDX_HEADS = 4
IDX_DIM = 64
TOPK_MAX = 256
ROPE_THETA = 500000.0
ATTN_SCALE = HEAD_DIM ** -0.5
IDX_SCALE = (IDX_DIM ** -0.5) * (IDX_HEADS ** -0.5)
NEG = -1e30
HGRN_HEADS = 4
HGRN_DK = 128
HGRN_DV = 128
HGRN_W = HGRN_HEADS * HGRN_DK
SCONV_WIDTH = 512
CONV_W = 3
BRANCH_WIDTH = 512
N_BRANCH = 3
D_FF = 2816
LN_EPS = 1e-5
IN_SIZES = (N_HEADS * HEAD_DIM, KV_HEADS * HEAD_DIM, KV_HEADS * HEAD_DIM,
            IDX_HEADS * IDX_DIM, IDX_DIM, IDX_HEADS,
            HGRN_W, HGRN_W, HGRN_W, HGRN_W,
            SCONV_WIDTH, SCONV_WIDTH, SCONV_WIDTH,
            N_BRANCH * D_MODEL)

LANES = 128
VMEM_LIMIT = 56 * 1024 * 1024
KEY_TILE = 512
QUERY_TILE = 256
VT_ROWS = HEAD_DIM + 16
FFN_COLS = 256
FFN_GROUP = 5
HGRN_SUBTILES = 4
PACK = 4
DIGITS = ((25, 7), (18, 7), (11, 7), (4, 7), (0, 4))
GUARD = np.int32(-0x7F7F7F80)
ONES = np.int32(0x01010101)

f32 = jnp.float32
bf16 = jnp.bfloat16
i32 = jnp.int32


def _row_tile(s, want):
    return want if s % want == 0 else s


def _const_spec(shape):
    nd = len(shape)
    return pl.BlockSpec(shape, lambda *_: (0,) * nd, pipeline_mode=pl.Buffered(1))


def _params(sem):
    return pltpu.CompilerParams(dimension_semantics=sem, vmem_limit_bytes=VMEM_LIMIT)


def _layer_norm(z, g, b):
    mu = jnp.mean(z, axis=-1, keepdims=True)
    d = z - mu
    var = jnp.mean(d * d, axis=-1, keepdims=True)
    return d * lax.rsqrt(var + LN_EPS) * g + b


def _sigmoid(x):
    return 1.0 / (1.0 + jnp.exp(-x))


Q_PRESCALE = ATTN_SCALE * float(np.log2(np.e))
ROT = HEAD_DIM // 4
WR_COLS = 384
WT_ROWS = 912
WI_ROWS = 8


def _attn_proj_kernel(x_ref, wr_ref, wt_ref, cs_ref, cst_ref, k_ref, v_ref, ki_ref, kb_ref, kib_ref,
                      vt_ref, qt_ref, qit_ref, wit_ref, *, T, TQ):
    xb = x_ref[0].astype(bf16)
    pr = jnp.dot(xb, wr_ref[...], preferred_element_type=f32)
    c, sa, sb = cs_ref[0], cs_ref[1], cs_ref[2]

    def rot(xg):
        return xg * c + pltpu.roll(xg, LANES - ROT // 2, 1) * sa + pltpu.roll(xg, ROT // 2, 1) * sb

    k = rot(pr[:, 0:LANES])
    ki = rot(pr[:, LANES:2 * LANES])[:, :IDX_DIM]
    k_ref[0] = k
    kb_ref[0] = k.astype(bf16)
    ki_ref[0] = ki
    kib_ref[0] = ki.astype(bf16)
    v_ref[0] = pr[:, 2 * LANES:3 * LANES]

    nt = (((1,), (1,)), ((), ()))
    pt = lax.dot_general(wt_ref[...], xb, nt, preferred_element_type=f32)
    ct, st = cst_ref[0], cst_ref[1]

    def rot_t(hb):
        lead = hb[0:ROT] * ct + jnp.concatenate([hb[ROT // 2:ROT], hb[0:ROT // 2]], axis=0) * st
        return jnp.concatenate([lead, hb[ROT:HEAD_DIM]], axis=0)

    nqb = T // TQ
    for h in range(N_HEADS):
        hb = (rot_t(pt[h * HEAD_DIM:(h + 1) * HEAD_DIM]) * Q_PRESCALE).astype(bf16)
        g, i = divmod(h, GROUPS)
        for n in range(nqb):
            qt_ref[0, n, g, :, i * TQ:(i + 1) * TQ] = hb[:, n * TQ:(n + 1) * TQ]
    base = N_HEADS * HEAD_DIM
    for h in range(IDX_HEADS):
        hb = rot_t(pt[base + h * IDX_DIM:base + (h + 1) * IDX_DIM]).astype(bf16)
        for n in range(nqb):
            qit_ref[0, n, :, h * TQ:(h + 1) * TQ] = hb[:, n * TQ:(n + 1) * TQ]
    base += IDX_HEADS * IDX_DIM
    ones_rows = jnp.where(lax.broadcasted_iota(i32, (VT_ROWS - HEAD_DIM, T), 0) == 0, 1.0, 0.0)
    for g in range(KV_HEADS):
        vt_ref[0, 0, g, 0:HEAD_DIM, :] = pt[base + g * HEAD_DIM:base + (g + 1) * HEAD_DIM].astype(bf16)
        vt_ref[0, 0, g, HEAD_DIM:VT_ROWS, :] = ones_rows.astype(bf16)
    base += KV_HEADS * HEAD_DIM
    for n in range(nqb):
        wit_ref[0, n] = pt[base:base + WI_ROWS, n * TQ:(n + 1) * TQ]


def _attn_proj(x, w_row, w_t, cs, cst):
    B, S, _ = x.shape
    T = _row_tile(S, KEY_TILE)
    TQ = _row_tile(S, QUERY_TILE)
    nqb = T // TQ
    R = GROUPS * TQ
    row = lambda w: pl.BlockSpec((1, T, w), lambda b, t: (b, t, 0))
    names = ("k", "v", "ki", "kb", "kib", "vt", "qt", "qit", "wit")
    outs = pl.pallas_call(
        functools.partial(_attn_proj_kernel, T=T, TQ=TQ),
        grid=(B, S // T),
        in_specs=[row(D_MODEL), _const_spec((D_MODEL, WR_COLS)), _const_spec((WT_ROWS, D_MODEL)),
                  pl.BlockSpec((3, T, LANES), lambda b, t: (0, t, 0)),
                  pl.BlockSpec((2, ROT, T), lambda b, t: (0, 0, t))],
        out_specs=[row(128), row(128), row(IDX_DIM), row(128), row(IDX_DIM),
                   pl.BlockSpec((1, 1, KV_HEADS, VT_ROWS, T), lambda b, t: (b, t, 0, 0, 0)),
                   pl.BlockSpec((1, nqb, KV_HEADS, HEAD_DIM, R), lambda b, t: (b, t, 0, 0, 0)),
                   pl.BlockSpec((1, nqb, IDX_DIM, IDX_HEADS * TQ), lambda b, t: (b, t, 0, 0)),
                   pl.BlockSpec((1, nqb, WI_ROWS, TQ), lambda b, t: (b, t, 0, 0))],
        out_shape=[jax.ShapeDtypeStruct((B, S, 128), f32),
                   jax.ShapeDtypeStruct((B, S, 128), f32),
                   jax.ShapeDtypeStruct((B, S, IDX_DIM), f32),
                   jax.ShapeDtypeStruct((B, S, 128), bf16),
                   jax.ShapeDtypeStruct((B, S, IDX_DIM), bf16),
                   jax.ShapeDtypeStruct((B, S // T, KV_HEADS, VT_ROWS, T), bf16),
                   jax.ShapeDtypeStruct((B, S // TQ, KV_HEADS, HEAD_DIM, R), bf16),
                   jax.ShapeDtypeStruct((B, S // TQ, IDX_DIM, IDX_HEADS * TQ), bf16),
                   jax.ShapeDtypeStruct((B, S // TQ, WI_ROWS, TQ), f32)],
        compiler_params=_params(("parallel", "parallel")),
        name="attn_proj",
    )(x, w_row, w_t, cs, cst)
    return dict(zip(names, outs))


def _rotary_tables(pos):
    half = ROT // 2
    inv_freq = ROPE_THETA ** (-jnp.arange(half, dtype=f32) / half)
    ang = pos.astype(f32)[:, None] * inv_freq[None, :]
    cos = jnp.cos(ang)
    sin = jnp.sin(ang)
    S = pos.shape[0]
    one = jnp.ones((S, HEAD_DIM - ROT), f32)
    zero = jnp.zeros((S, HEAD_DIM - ROT), f32)
    zh = jnp.zeros((S, half), f32)
    c = jnp.concatenate([cos, cos, one], axis=1)
    sa = jnp.concatenate([-sin, zh, zero], axis=1)
    sb = jnp.concatenate([zh, sin, zero], axis=1)
    tile2 = lambda a: jnp.concatenate([a, a], axis=1)
    cs = jnp.stack([tile2(c), tile2(sa), tile2(sb)], axis=0)
    cst = jnp.stack([jnp.concatenate([cos, cos], axis=1).T,
                     jnp.concatenate([-sin, sin], axis=1).T], axis=0)
    return cs, cst


def _dsa_kernel(qt_ref, qit_ref, wit_ref, ki_ref, k_ref, vt_ref, tri_ref, o_ref,
                s_ref, pk_ref, qpad_ref, m_ref, acc_ref, *bufs, TQ, TK, P, top):
    lm_refs = (bufs[0:KV_HEADS], bufs[KV_HEADS:2 * KV_HEADS])
    p_refs = (bufs[2 * KV_HEADS:3 * KV_HEADS], bufs[3 * KV_HEADS:4 * KV_HEADS])
    qb = pl.program_id(1)
    q0 = P + qb * TQ
    nk = (q0 + TQ + TK - 1) // TK
    n_full = q0 // TK
    NC = TK // LANES
    R = GROUPS * TQ
    q_lim = ((q0 + lax.broadcasted_iota(i32, (1, TQ), 1)) // CHUNK + 1) * CHUNK
    wit = wit_ref[0, 0] * IDX_SCALE
    zero_head = jnp.zeros((HEAD_DIM, R), bf16)
    qpad_ref[0] = jnp.concatenate([qt_ref[0, 0, 0], zero_head], axis=0)
    qpad_ref[1] = jnp.concatenate([zero_head, qt_ref[0, 0, 1]], axis=0)

    TKP = TK // PACK

    def lead_digit(key):
        return (key >> DIGITS[0][0]) + (1 << (DIGITS[0][1] - 1))

    def score_tile(masked, j, carry):
        for c in range(NC):
            rows = slice(c * LANES, (c + 1) * LANES)
            rel = jnp.maximum(jnp.dot(ki_ref[0, j, rows, :], qit_ref[0, 0], preferred_element_type=f32), 0.0)
            score = wit[0:1, :] * rel[:, 0:TQ]
            for h in range(1, IDX_HEADS):
                score = score + wit[h:h + 1, :] * rel[:, h * TQ:(h + 1) * TQ]
            if masked:
                k_pos = j * TK + c * LANES + lax.broadcasted_iota(i32, (LANES, TQ), 0)
                score = jnp.where(k_pos < q_lim, score, NEG)
            bits = pltpu.bitcast(score, i32)
            key = jnp.where(bits >= 0, bits, bits ^ 0x7FFFFFFF)
            s_ref[j, rows, :] = key
            if c == 0:
                pk_ref[j] = lead_digit(key) | GUARD
            else:
                pk_ref[j] = pk_ref[j] | (lead_digit(key) << (8 * c))
        return carry

    lax.fori_loop(0, n_full, functools.partial(score_tile, False), 0)
    lax.fori_loop(n_full, nk, functools.partial(score_tile, True), 0)

    n_chunks = (nk + PACK - 1) // PACK

    def fill_guard(j, carry):
        pk_ref[j] = jnp.full((TKP, TQ), GUARD, i32)
        return carry

    lax.fori_loop(nk, n_chunks * PACK, fill_guard, 0)

    def count_ge(cand):
        cvec = cand * ONES

        def chunk(ci, tot):
            accs = [None] * 4
            for tt in range(PACK):
                for r in range(TKP // 8):
                    slab = pk_ref[ci * PACK + tt, r * 8:(r + 1) * 8, :]
                    hit = lax.shift_right_logical(slab - cvec, 7) & ONES
                    n = (tt * (TKP // 8) + r) % 4
                    accs[n] = hit if accs[n] is None else accs[n] + hit
            a = (accs[0] + accs[1]) + (accs[2] + accs[3])
            return tot + ((a & 0xFF) + ((a >> 8) & 0xFF) + ((a >> 16) & 0xFF)
                          + lax.shift_right_logical(a, 24))

        tot = lax.fori_loop(0, n_chunks, chunk, jnp.zeros((8, TQ), i32))
        return jnp.sum(tot, axis=0, keepdims=True)

    def repack(lo, width, prefix):
        origin = prefix * (1 << width)

        def tile(j, carry):
            for r in range(TKP // 8):
                word = jnp.full((8, TQ), GUARD, i32)
                for q in range(PACK):
                    key = s_ref[j, q * TKP + r * 8:q * TKP + (r + 1) * 8, :]
                    d = (key >> lo) - origin
                    in_range = pltpu.bitcast(d, jnp.uint32) < jnp.uint32(1 << width)
                    word = word | (jnp.where(in_range, d, 0) << (8 * q))
                pk_ref[j, r * 8:(r + 1) * 8, :] = word
            return carry
        lax.fori_loop(0, nk, tile, 0)

    base = jnp.zeros((1, TQ), i32)
    bsize = jnp.full((1, TQ), nk * TK, i32)
    prefix = None
    zero = jnp.zeros((1, TQ), i32)
    for stage, (lo, width) in enumerate(DIGITS):
        need = top - base

        def search_digit(stage=stage, lo=lo, width=width, prefix=prefix, need=need, bsize=bsize):
            if stage > 0:
                repack(lo, width, prefix)

            def bit_step(i, st):
                v, c_v, c_rej = st
                cand = v | jnp.left_shift(jnp.int32(1), width - 1 - i)
                c = count_ge(cand)
                ok = c >= need
                return jnp.where(ok, cand, v), jnp.where(ok, c, c_v), jnp.where(ok, c_rej, c)

            return lax.fori_loop(0, width, bit_step, (zero, bsize, zero))

        v, c_v, c_gt = search_digit()
        base, bsize = base + c_gt, c_v - c_gt
        prefix = v - (1 << (width - 1)) if stage == 0 else prefix * (1 << width) + v
    t = prefix
    neg_key = np.array(NEG, np.float32).view(np.int32) ^ 0x7FFFFFFF
    any_excess = jnp.max(jnp.where(base + bsize > top, 1, 0)) > 0

    @pl.when(jnp.logical_not(any_excess))
    def _():
        thr = jnp.where(t == neg_key, t + 1, t)

        def fast(j, carry):
            s_ref[j] = pltpu.bitcast(jnp.where(s_ref[j] >= thr, 0.0, NEG), i32)
            return carry
        lax.fori_loop(0, nk, fast, 0)

    @pl.when(any_excess)
    def _():
        need = jnp.where(t == neg_key, 0, top - base).astype(f32)

        def ranked(j, taken):
            keys = s_ref[j]
            for c in range(NC):
                kc = keys[c * LANES:(c + 1) * LANES, :]
                eq = jnp.where(kc == t, 1.0, 0.0)
                rank = taken + jnp.dot(tri_ref[...], eq.astype(bf16), preferred_element_type=f32)
                pick = jnp.where(kc > t, 1.0, jnp.where(rank <= need, eq, 0.0))
                s_ref[j, c * LANES:(c + 1) * LANES, :] = pltpu.bitcast(
                    jnp.where(pick > 0.5, 0.0, NEG), i32)
                taken = taken + jnp.sum(eq, axis=0, keepdims=True)
            return taken
        lax.fori_loop(0, nk, ranked, jnp.zeros((1, TQ), f32))

    m_ref[...] = jnp.full(m_ref.shape, NEG, f32)
    acc_ref[...] = jnp.zeros(acc_ref.shape, f32)

    def tile_step(nxt, cur):
        if cur is not None:
            j, slot, tile_max = cur
            m_old = [m_ref[g] for g in range(KV_HEADS)]
            m_new = [jnp.maximum(m_old[g], tile_max[g]) for g in range(KV_HEADS)]
            for g in range(KV_HEADS):
                m_ref[g] = m_new[g]
        if nxt is not None:
            jn, slot_n = nxt
            jc = jnp.minimum(jn, nk - 1)
        out = []
        for g in range(KV_HEADS):
            cmax = jnp.full((8, R), NEG, f32)
            if nxt is not None:
                raw = jnp.dot(k_ref[0, jc], qpad_ref[g], preferred_element_type=f32)
            for c in range(NC):
                rows = slice(c * LANES, (c + 1) * LANES)
                if nxt is not None:
                    bias = pltpu.bitcast(s_ref[jc, rows, :], f32)
                    lm = raw[rows, :] + jnp.concatenate([bias] * GROUPS, axis=1)
                    lm_refs[slot_n][g][rows, :] = lm
                    cmax = jnp.maximum(cmax, jnp.max(lm.reshape(LANES // 8, 8, R), axis=0))
                if cur is not None:
                    p_refs[slot][g][rows, :] = jnp.exp2(lm_refs[slot][g][rows, :] - m_new[g]).astype(bf16)
            out.append(jnp.max(cmax, axis=0, keepdims=True))
            if cur is not None:
                acc_ref[g] = (jnp.exp2(m_old[g] - m_new[g]) * acc_ref[g]
                              + jnp.dot(vt_ref[0, j, g], p_refs[slot][g][...], preferred_element_type=f32))
        return tuple(out)

    def tile_pair(i, tile_max):
        nxt_max = tile_step((2 * i + 1, 1), (2 * i, 0, tile_max))
        return tile_step((2 * i + 2, 0), (2 * i + 1, 1, nxt_max))

    last_max = lax.fori_loop(0, nk // 2, tile_pair, tile_step((0, 0), None))

    @pl.when(nk % 2 == 1)
    def _():
        tile_step(None, (nk - 1, 0, last_max))

    for g in range(KV_HEADS):
        o_t = acc_ref[g, 0:HEAD_DIM, :] / acc_ref[g, HEAD_DIM:HEAD_DIM + 1, :]
        for i in range(0, GROUPS, 2):
            pair = jnp.concatenate([o_t[:, i * TQ:(i + 1) * TQ], o_t[:, (i + 1) * TQ:(i + 2) * TQ]], axis=0)
            h = g * GROUPS + i
            o_ref[0, :, h * HEAD_DIM:(h + 2) * HEAD_DIM] = pair.T.astype(bf16)


def _dsa(qt, qit, wit, kb, kib, vt, P, top):
    B, nq = qt.shape[:2]
    R = qt.shape[-1]
    TQ = R // GROUPS
    NT, TK = kb.shape[1:3]
    assert top <= TK and TQ % CHUNK == 0
    tri = jnp.asarray(np.tril(np.ones((LANES, LANES), np.float32)), bf16)
    return pl.pallas_call(
        functools.partial(_dsa_kernel, TQ=TQ, TK=TK, P=P, top=top),
        grid=(B, nq),
        in_specs=[pl.BlockSpec((1, 1, KV_HEADS, HEAD_DIM, R), lambda b, i: (b, i, 0, 0, 0)),
                  pl.BlockSpec((1, 1, IDX_DIM, IDX_HEADS * TQ), lambda b, i: (b, i, 0, 0)),
                  pl.BlockSpec((1, 1, WI_ROWS, TQ), lambda b, i: (b, i, 0, 0)),
                  pl.BlockSpec((1, NT, TK, IDX_DIM), lambda b, i: (b, 0, 0, 0)),
                  pl.BlockSpec((1, NT, TK, KV_HEADS * HEAD_DIM), lambda b, i: (b, 0, 0, 0)),
                  pl.BlockSpec((1, NT, KV_HEADS, VT_ROWS, TK), lambda b, i: (b, 0, 0, 0, 0)),
                  _const_spec((LANES, LANES))],
        out_specs=pl.BlockSpec((1, TQ, N_HEADS * HEAD_DIM), lambda b, i: (b, i, 0)),
        out_shape=jax.ShapeDtypeStruct((B, nq * TQ, N_HEADS * HEAD_DIM), bf16),
        scratch_shapes=[pltpu.VMEM((NT, TK, TQ), i32),
                        pltpu.VMEM((-(-NT // PACK) * PACK, TK // PACK, TQ), i32),
                        pltpu.VMEM((KV_HEADS, KV_HEADS * HEAD_DIM, R), bf16),
                        pltpu.VMEM((KV_HEADS, 1, R), f32),
                        pltpu.VMEM((KV_HEADS, VT_ROWS, R), f32)]
        + [pltpu.VMEM((TK, R), f32)] * (2 * KV_HEADS)
        + [pltpu.VMEM((TK, R), bf16)] * (2 * KV_HEADS),
        compiler_params=_params(("parallel", "arbitrary")),
        name="dsa",
    )(qt, qit, wit, kib, kb, vt, tri)


def _hgrn_mats(T):
    r = np.arange(T)[:, None]
    c = np.arange(T)[None, :]
    mats = [c <= r, c > r]
    b = T // 2
    while b >= 1:
        blk_r = r // (2 * b)
        mid = blk_r * 2 * b + b
        same = blk_r == c // (2 * b)
        late = r >= mid
        mats.append(same & ((late & (c >= mid) & (c <= r)) | (~late & (c > r) & (c < mid))))
        b //= 2
    return np.concatenate(mats, axis=0).astype(np.float32)


def _hgrn_kernel(x_ref, w_ref, lb_ref, g_ref, mats_ref, s0_ref, y_ref, st_out_ref, st_ref, *, T, SUBS):
    t_idx = pl.program_id(1)

    @pl.when(t_idx == 0)
    def _():
        st_ref[...] = s0_ref[0]

    lb = lb_ref[...]
    mats = mats_ref[...]
    row = lax.broadcasted_iota(i32, (T, 1), 0)
    col = lax.broadcasted_iota(i32, (1, T), 1)
    nt = (((1,), (1,)), ((), ()))
    tn = (((0,), (0,)), ((), ()))

    def prepare(i):
        xb = x_ref[0, i * T:(i + 1) * T, :].astype(bf16)
        proj = jnp.dot(xb, w_ref[...], preferred_element_type=f32)
        hq = proj[:, 0:HGRN_W]
        z = proj[:, HGRN_W:2 * HGRN_W]
        hv = proj[:, 2 * HGRN_W:3 * HGRN_W]
        hg = proj[:, 3 * HGRN_W:4 * HGRN_W]
        logf = (jnp.minimum(z, 0.0) - jnp.log1p(jnp.exp(-jnp.abs(z)))) + jnp.log1p(lb * jnp.exp(-z))
        kk = (1.0 - lb) * _sigmoid(-z)
        qq = hq * _sigmoid(hq)
        gate = hg * _sigmoid(hg)
        hi = logf.astype(bf16)
        r1 = logf - hi.astype(f32)
        mid = r1.astype(bf16)
        lo = (r1 - mid.astype(f32)).astype(bf16)
        e_all = jnp.dot(mats, jnp.concatenate([hi, mid, lo], axis=0),
                        preferred_element_type=f32)
        return qq, kk, hv, gate, e_all

    def recur(i, prepared):
        qq, kk, hv, gate, e_all = prepared
        for h in range(HGRN_HEADS):
            sl = slice(h * HGRN_DK, (h + 1) * HGRN_DK)
            q_h, k_h, v_h = qq[:, sl], kk[:, sl], hv[:, sl]
            vb = v_h.astype(bf16)
            cum = e_all[0:T, sl]
            suf = e_all[T:2 * T, sl]
            st = st_ref[h]
            o = lax.dot_general((q_h * jnp.exp(cum)).astype(bf16), st.astype(bf16), nt,
                                preferred_element_type=f32)
            o = o + jnp.sum(q_h * k_h, axis=-1, keepdims=True) * v_h
            scores = jnp.zeros((T, T), f32)
            b = T // 2
            lvl = 2
            while b >= 1:
                xdec = jnp.exp(e_all[lvl * T:(lvl + 1) * T, sl])
                late = ((row // b) % 2) == 1
                a_m = jnp.where(late, q_h * xdec, 0.0).astype(bf16)
                b_m = jnp.where(late, 0.0, k_h * xdec).astype(bf16)
                sc = lax.dot_general(a_m, b_m, nt, preferred_element_type=f32)
                if 2 * b < T:
                    sc = jnp.where((row // (2 * b)) == (col // (2 * b)), sc, 0.0)
                scores = scores + sc
                b //= 2
                lvl += 1
            o = o + jnp.dot(scores.astype(bf16), vb, preferred_element_type=f32)
            kd = (k_h * jnp.exp(suf)).astype(bf16)
            st_ref[h] = (st * jnp.exp(cum[T - 1:T, :])
                         + lax.dot_general(vb, kd, tn, preferred_element_type=f32))
            o = o * lax.rsqrt(jnp.mean(o * o, axis=-1, keepdims=True) + LN_EPS) * g_ref[...]
            y_ref[0, i * T:(i + 1) * T, sl] = (o * gate[:, sl]).astype(bf16)

    nxt = prepare(0)
    for i in range(SUBS):
        cur = nxt
        if i + 1 < SUBS:
            nxt = prepare(i + 1)
        recur(i, cur)

    st_out_ref[0] = st_ref[...]


def _hgrn(x, w_h, lb, norm_g, s0):
    B, S, _ = x.shape
    T = _row_tile(S, 2 * CHUNK)
    SUBS = HGRN_SUBTILES if S % (T * HGRN_SUBTILES) == 0 else 1
    TS = T * SUBS
    mats = jnp.asarray(np.tile(_hgrn_mats(T), (1, 3)), bf16)
    nm = mats.shape[0]
    st0 = jnp.swapaxes(s0.astype(f32), 2, 3)
    y, st = pl.pallas_call(
        functools.partial(_hgrn_kernel, T=T, SUBS=SUBS),
        grid=(B, S // TS),
        in_specs=[pl.BlockSpec((1, TS, D_MODEL), lambda b, t: (b, t, 0)),
                  _const_spec((D_MODEL, 4 * HGRN_W)),
                  _const_spec((1, HGRN_W)),
                  _const_spec((1, HGRN_DV)),
                  _const_spec((nm, 3 * T)),
                  pl.BlockSpec((1, HGRN_HEADS, HGRN_DV, HGRN_DK), lambda b, t: (b, 0, 0, 0))],
        out_specs=[pl.BlockSpec((1, TS, HGRN_W), lambda b, t: (b, t, 0)),
                   pl.BlockSpec((1, HGRN_HEADS, HGRN_DV, HGRN_DK), lambda b, t: (b, 0, 0, 0))],
        out_shape=[jax.ShapeDtypeStruct((B, S, HGRN_W), bf16),
                   jax.ShapeDtypeStruct((B, HGRN_HEADS, HGRN_DV, HGRN_DK), f32)],
        scratch_shapes=[pltpu.VMEM((HGRN_HEADS, HGRN_DV, HGRN_DK), f32)],
        compiler_params=_params(("parallel", "arbitrary")),
        name="hgrn",
    )(x, w_h, lb, norm_g, mats, st0)
    return y, jnp.swapaxes(st, 2, 3)


HIST = CONV_W - 1
PADROWS = 8


def _conv_rows(buf_ref, cols, u, w, bias, T):
    buf_ref[PADROWS:PADROWS + T, cols] = u
    y = bias
    for j in range(CONV_W):
        y = y + w[j:j + 1, :] * buf_ref[PADROWS - HIST + j:PADROWS - HIST + j + T, cols]
    buf_ref[PADROWS - HIST:PADROWS, cols] = u[T - HIST:T, :]
    return y


def _merge_kernel(x_ref, ya_ref, yb_ref, hist_ref, wc_ref, wg_ref, wbr_ref, wout_ref, cw_ref, cb_ref,
                  g_ref, b_ref, o_ref, hist_out_ref, buf_ref, *, T, alpha):
    @pl.when(pl.program_id(1) == 0)
    def _():
        buf_ref[PADROWS - HIST:PADROWS, :] = hist_ref[0]

    x = x_ref[0]
    xb = x.astype(bf16)
    c = jnp.dot(xb, wc_ref[...], preferred_element_type=f32)
    W = SCONV_WIDTH
    u = c[:, W:2 * W] * c[:, 2 * W:3 * W]
    u_conv = _conv_rows(buf_ref, slice(0, W), u, cw_ref[...], cb_ref[...], T)
    yc = (c[:, 0:W] * u_conv).astype(bf16)
    hist_out_ref[0] = buf_ref[PADROWS - HIST:PADROWS, :]

    merged = None
    for i, yb in enumerate((ya_ref[0], yb_ref[0], yc)):
        gate = _sigmoid(jnp.dot(xb, wg_ref[:, i * D_MODEL:(i + 1) * D_MODEL], preferred_element_type=f32))
        term = gate * jnp.dot(yb, wbr_ref[i], preferred_element_type=f32)
        merged = term if merged is None else merged + term
    out = jnp.dot(merged.astype(bf16), wout_ref[...], preferred_element_type=f32)
    o_ref[0] = _layer_norm(alpha * x + out, g_ref[...], b_ref[...])


def _merge(x, ya, yb, hist, wc, wg, wbr, wout, cw, cb, g, b, alpha):
    B, S, _ = x.shape
    T = _row_tile(S, 256)
    row = lambda w: pl.BlockSpec((1, T, w), lambda bb, t: (bb, t, 0))
    hspec = pl.BlockSpec((1, HIST, SCONV_WIDTH), lambda bb, t: (bb, 0, 0))
    return pl.pallas_call(
        functools.partial(_merge_kernel, T=T, alpha=alpha),
        grid=(B, S // T),
        in_specs=[row(D_MODEL), row(BRANCH_WIDTH), row(BRANCH_WIDTH), hspec,
                  _const_spec((D_MODEL, 3 * SCONV_WIDTH)), _const_spec((D_MODEL, N_BRANCH * D_MODEL)),
                  _const_spec((N_BRANCH, BRANCH_WIDTH, D_MODEL)), _const_spec((D_MODEL, D_MODEL)),
                  _const_spec((CONV_W, SCONV_WIDTH)), _const_spec((1, SCONV_WIDTH)),
                  _const_spec((1, D_MODEL)), _const_spec((1, D_MODEL))],
        out_specs=[row(D_MODEL), hspec],
        out_shape=[jax.ShapeDtypeStruct((B, S, D_MODEL), f32),
                   jax.ShapeDtypeStruct((B, HIST, SCONV_WIDTH), f32)],
        scratch_shapes=[pltpu.VMEM((PADROWS + T, SCONV_WIDTH), f32)],
        compiler_params=_params(("parallel", "arbitrary")),
        name="merge",
    )(x, ya, yb, hist, wc, wg, wbr, wout, cw, cb, g, b)


def _ffn_kernel(x_ref, hist_ref, wup_ref, cw_ref, cb_ref, wdn_ref, g_ref, b_ref, o_ref, hist_out_ref,
                *bufs, T, alpha):
    n_steps = D_FF // FFN_COLS
    col_ranges = [slice(off + c * FFN_COLS, off + (c + 1) * FFN_COLS)
                  for c in range(n_steps) for off in (0, D_FF)]

    @pl.when(pl.program_id(1) == 0)
    def _():
        for buf, cols in zip(bufs, col_ranges):
            buf[PADROWS - HIST:PADROWS, :] = hist_ref[0, :, cols]

    x = x_ref[0]
    xb = x.astype(bf16)

    def up_conv(cidx):
        halves = []
        for half in range(2):
            buf, cols = bufs[2 * cidx + half], col_ranges[2 * cidx + half]
            h = jnp.dot(xb, wup_ref[:, cols], preferred_element_type=f32)
            halves.append(_conv_rows(buf, slice(None), h, cw_ref[:, cols], cb_ref[:, cols], T))
            hist_out_ref[0, :, cols] = buf[PADROWS - HIST:PADROWS, :]
        return halves

    groups = [list(range(c, min(c + FFN_GROUP, n_steps))) for c in range(0, n_steps, FFN_GROUP)]
    acc = jnp.zeros((T, D_MODEL), f32)
    nxt = [up_conv(c) for c in groups[0]]
    for gi, grp in enumerate(groups):
        cur = nxt
        if gi + 1 < len(groups):
            nxt = [up_conv(c) for c in groups[gi + 1]]
        act = jnp.concatenate([(a_g * _sigmoid(a_g) * b_v).astype(bf16) for a_g, b_v in cur], axis=1)
        acc = acc + jnp.dot(act, wdn_ref[grp[0] * FFN_COLS:(grp[-1] + 1) * FFN_COLS, :],
                            preferred_element_type=f32)
    o_ref[0] = _layer_norm(alpha * x + acc, g_ref[...], b_ref[...])


def _ffn(x, hist, wup, cw, cb, wdn, g, b, alpha):
    B, S, _ = x.shape
    T = _row_tile(S, 256)
    row = pl.BlockSpec((1, T, D_MODEL), lambda bb, t: (bb, t, 0))
    hspec = pl.BlockSpec((1, HIST, 2 * D_FF), lambda bb, t: (bb, 0, 0))
    return pl.pallas_call(
        functools.partial(_ffn_kernel, T=T, alpha=alpha),
        grid=(B, S // T),
        in_specs=[row, hspec, _const_spec((D_MODEL, 2 * D_FF)), _const_spec((CONV_W, 2 * D_FF)),
                  _const_spec((1, 2 * D_FF)), _const_spec((D_FF, D_MODEL)),
                  _const_spec((1, D_MODEL)), _const_spec((1, D_MODEL))],
        out_specs=[row, hspec],
        out_shape=[jax.ShapeDtypeStruct((B, S, D_MODEL), f32),
                   jax.ShapeDtypeStruct((B, HIST, 2 * D_FF), f32)],
        scratch_shapes=[pltpu.VMEM((PADROWS + T, FFN_COLS), f32)] * (2 * D_FF // FFN_COLS),
        compiler_params=_params(("parallel", "arbitrary")),
        name="ffn",
    )(x, hist, wup, cw, cb, wdn, g, b)


def _cast_kernel(x_ref, o_ref):
    o_ref[...] = x_ref[...].astype(o_ref.dtype)


def _to_bf16(w):
    D, Rw, C = w.shape
    T = _row_tile(Rw, LANES)
    spec = pl.BlockSpec((1, T, C), lambda d, t: (d, t, 0))
    return pl.pallas_call(_cast_kernel, grid=(D, Rw // T), in_specs=[spec], out_specs=spec,
                          out_shape=jax.ShapeDtypeStruct(w.shape, bf16),
                          compiler_params=_params(("parallel", "parallel")), name="to_bf16")(w)


def _layer_weights(l, depth, w_in, hgrn_lb_logits, hgrn_norm_g, sconv_w, sconv_b, w_branch, w_out,
                   ln1_g, ln1_b, w_up, ffn_conv_w, ffn_conv_b, w_down, ln2_g, ln2_b):
    offs = np.concatenate([[0], np.cumsum(IN_SIZES)]).tolist()
    sec = lambda i, j=None: w_in[l][:, offs[i]:offs[(i if j is None else j) + 1]]
    a_q, a_k, a_v, i_q, i_k, i_w = (sec(i) for i in range(6))
    zpad = lambda n: jnp.zeros((D_MODEL, n), w_in.dtype)
    w_row = jnp.concatenate([a_k, i_k, zpad(LANES - IDX_DIM), a_v], axis=1).astype(bf16)
    w_t = jnp.concatenate([a_q, i_q, a_v, i_w, zpad(WT_ROWS - 900)], axis=1).T.astype(bf16)
    lbp = jax.nn.softmax(hgrn_lb_logits.astype(f32), axis=0)
    lb = (jnp.cumsum(lbp, axis=0) - lbp[0])[l].reshape(1, HGRN_W)
    row = lambda a: a.reshape(1, -1).astype(f32)
    return dict(
        w_row=w_row, w_t=w_t, w_h=sec(6, 9).astype(bf16), lb=lb, norm_g=row(hgrn_norm_g[l]),
        w_c=sec(10, 12).astype(bf16), w_g=sec(13).astype(bf16),
        w_br=w_branch[l].astype(bf16), w_out=w_out[l].astype(bf16),
        sconv_w=sconv_w[l].astype(f32), sconv_b=row(sconv_b[l]),
        ln1_g=row(ln1_g[l]), ln1_b=row(ln1_b[l]),
        w_up=w_up[l].astype(bf16), ffn_w=ffn_conv_w[l].astype(f32), ffn_b=row(ffn_conv_b[l]),
        w_down=w_down[l].astype(bf16), ln2_g=row(ln2_g[l]), ln2_b=row(ln2_b[l]))


def _key_tiles(a, TK):
    B, L, C = a.shape
    NT = -(-L // TK)
    a = jnp.pad(a, ((0, 0), (0, NT * TK - L), (0, 0)))
    return a.reshape(B, NT, TK, C)


def _trunk_layer(x, tables, P, k_past, v_past, ki_past, s0, sc_hist, ffn_hist, w, alpha):
    B, S, _ = x.shape
    TK = KEY_TILE
    a = _attn_proj(x, w["w_row"], w["w_t"], *tables)
    k, v, ki = a["k"], a["v"], a["ki"]
    L = P + S
    if k_past is None and S % TK == 0:
        kb = a["kb"].reshape(B, S // TK, TK, -1)
        kib = a["kib"].reshape(B, S // TK, TK, -1)
        vt = a["vt"]
    else:
        kb, kib, v_all = a["kb"], a["kib"], v.astype(bf16)
        if k_past is not None:
            kb = jnp.concatenate([k_past.reshape(B, P, -1).astype(bf16), kb], axis=1)
            kib = jnp.concatenate([ki_past.astype(bf16), kib], axis=1)
            v_all = jnp.concatenate([v_past.reshape(B, P, -1).astype(bf16), v_all], axis=1)
        kb, kib = _key_tiles(kb, TK), _key_tiles(kib, TK)
        vt = _key_tiles(v_all, TK).reshape(B, -1, TK, KV_HEADS, HEAD_DIM).transpose(0, 1, 3, 4, 2)
        ones = jnp.zeros(vt.shape[:3] + (VT_ROWS - HEAD_DIM, TK), bf16).at[:, :, :, 0, :].set(1.0)
        vt = jnp.concatenate([vt, ones], axis=3)
    y_a = _dsa(a["qt"], a["qit"], a["wit"], kb, kib, vt, P, min(TOPK_MAX, L // 4))
    y_b, s_new = _hgrn(x, w["w_h"], w["lb"], w["norm_g"], s0)
    x1, sc_new = _merge(x, y_a, y_b, sc_hist, w["w_c"], w["w_g"], w["w_br"], w["w_out"],
                        w["sconv_w"], w["sconv_b"], w["ln1_g"], w["ln1_b"], alpha)
    x2, ffn_new = _ffn(x1, ffn_hist, w["w_up"], w["ffn_w"], w["ffn_b"], w["w_down"],
                       w["ln2_g"], w["ln2_b"], alpha)
    new = (k.reshape(B, S, KV_HEADS, HEAD_DIM), v.reshape(B, S, KV_HEADS, HEAD_DIM), ki,
           s_new, sc_new, ffn_new)
    return x2, new


def kernel(x_prompt, x_sample, cache_attn_k, cache_attn_v, cache_idx_k, state_hgrn, state_sconv,
           state_ffn_conv, w_in, hgrn_lb_logits, hgrn_norm_g, sconv_w, sconv_b, w_branch, w_out,
           ln1_g, ln1_b, w_up, ffn_conv_w, ffn_conv_b, w_down, ln2_g, ln2_b):
    depth = w_in.shape[0]
    alpha = (2 * depth) ** 0.25
    B, S, _ = x_prompt.shape
    DB, DS, _ = x_sample.shape
    P = cache_attn_k.shape[2]
    cs_p = _rotary_tables(jnp.arange(S))
    cs_s = _rotary_tables(P + jnp.arange(DS))
    xp, xs = x_prompt, x_sample
    st_p = [[] for _ in range(6)]
    st_s = [[] for _ in range(6)]
    w_in_b = _to_bf16(w_in)
    for l in range(depth):
        w = _layer_weights(l, depth, w_in_b, hgrn_lb_logits, hgrn_norm_g, sconv_w, sconv_b, w_branch,
                           w_out, ln1_g, ln1_b, w_up, ffn_conv_w, ffn_conv_b, w_down, ln2_g, ln2_b)
        xp, new_p = _trunk_layer(
            xp, cs_p, 0, None, None, None,
            jnp.zeros((B, HGRN_HEADS, HGRN_DK, HGRN_DV), f32),
            jnp.zeros((B, HIST, SCONV_WIDTH), f32),
            jnp.zeros((B, HIST, 2 * D_FF), f32), w, alpha)
        xs, new_s = _trunk_layer(
            xs, cs_s, P, cache_attn_k[l], cache_attn_v[l], cache_idx_k[l], state_hgrn[l],
            state_sconv[l], state_ffn_conv[l], w, alpha)
        for j in range(6):
            st_p[j].append(new_p[j])
            st_s[j].append(new_s[j])
    outs_p = [jnp.stack(a, axis=0) for a in st_p]
    outs_s = [jnp.stack(a, axis=0) for a in st_s]
    return (xp, xs, *outs_p, *outs_s)
```

```python
import functools

import numpy as np
import jax
import jax.numpy as jnp
from jax import lax
from jax.experimental import pallas as pl
from jax.experimental.pallas import tpu as pltpu

D_MODEL = 1024
CHUNK = 64
N_HEADS = 8
HEAD_DIM = 64
KV_HEADS = 2
GROUPS = N_HEADS // KV_HEADS
IDX_HEADS = 4
IDX_DIM = 64
TOPK_MAX = 256
ROPE_THETA = 500000.0
ATTN_SCALE = HEAD_DIM ** -0.5
IDX_SCALE = (IDX_DIM ** -0.5) * (IDX_HEADS ** -0.5)
NEG = -1e30
HGRN_HEADS = 4
HGRN_DK = 128
HGRN_DV = 128
HGRN_W = HGRN_HEADS * HGRN_DK
SCONV_WIDTH = 512
CONV_W = 3
BRANCH_WIDTH = 512
N_BRANCH = 3
D_FF = 2816
LN_EPS = 1e-5
IN_SIZES = (N_HEADS * HEAD_DIM, KV_HEADS * HEAD_DIM, KV_HEADS * HEAD_DIM,
            IDX_HEADS * IDX_DIM, IDX_DIM, IDX_HEADS,
            HGRN_W, HGRN_W, HGRN_W, HGRN_W,
            SCONV_WIDTH, SCONV_WIDTH, SCONV_WIDTH,
            N_BRANCH * D_MODEL)

LANES = 128
VMEM_LIMIT = 56 * 1024 * 1024
KEY_TILE = 512
QUERY_TILE = 256
VT_ROWS = HEAD_DIM + 16
FFN_COLS = 256
FFN_GROUP = 5
HGRN_SUBTILES = 4
PACK = 4
DIGITS = ((25, 7), (18, 7), (11, 7), (4, 7), (0, 4))
GUARD = np.int32(-0x7F7F7F80)
ONES = np.int32(0x01010101)

f32 = jnp.float32
bf16 = jnp.bfloat16
i32 = jnp.int32


def _row_tile(s, want):
    return want if s % want == 0 else s


def _const_spec(shape):
    nd = len(shape)
    return pl.BlockSpec(shape, lambda *_: (0,) * nd, pipeline_mode=pl.Buffered(1))


def _params(sem):
    return pltpu.CompilerParams(dimension_semantics=sem, vmem_limit_bytes=VMEM_LIMIT)


def _layer_norm(z, g, b):
    mu = jnp.mean(z, axis=-1, keepdims=True)
    d = z - mu
    var = jnp.mean(d * d, axis=-1, keepdims=True)
    return d * lax.rsqrt(var + LN_EPS) * g + b


def _sigmoid(x):
    return 1.0 / (1.0 + jnp.exp(-x))


Q_PRESCALE = ATTN_SCALE * float(np.log2(np.e))
ROT = HEAD_DIM // 4
WR_COLS = 384
WT_ROWS = 912
WI_ROWS = 8


def _attn_proj_kernel(x_ref, wr_ref, wt_ref, cs_ref, cst_ref, k_ref, v_ref, ki_ref, kb_ref, kib_ref,
                      vt_ref, qt_ref, qit_ref, wit_ref, *, T, TQ):
    xb = x_ref[0].astype(bf16)
    pr = jnp.dot(xb, wr_ref[...], preferred_element_type=f32)
    c, sa, sb = cs_ref[0], cs_ref[1], cs_ref[2]

    def rot(xg):
        return xg * c + pltpu.roll(xg, LANES - ROT // 2, 1) * sa + pltpu.roll(xg, ROT // 2, 1) * sb

    k = rot(pr[:, 0:LANES])
    ki = rot(pr[:, LANES:2 * LANES])[:, :IDX_DIM]
    k_ref[0] = k
    kb_ref[0] = k.astype(bf16)
    ki_ref[0] = ki
    kib_ref[0] = ki.astype(bf16)
    v_ref[0] = pr[:, 2 * LANES:3 * LANES]

    nt = (((1,), (1,)), ((), ()))
    pt = lax.dot_general(wt_ref[...], xb, nt, preferred_element_type=f32)
    ct, st = cst_ref[0], cst_ref[1]

    def rot_t(hb):
        lead = hb[0:ROT] * ct + jnp.concatenate([hb[ROT // 2:ROT], hb[0:ROT // 2]], axis=0) * st
        return jnp.concatenate([lead, hb[ROT:HEAD_DIM]], axis=0)

    nqb = T // TQ
    for h in range(N_HEADS):
        hb = (rot_t(pt[h * HEAD_DIM:(h + 1) * HEAD_DIM]) * Q_PRESCALE).astype(bf16)
        g, i = divmod(h, GROUPS)
        for n in range(nqb):
            qt_ref[0, n, g, :, i * TQ:(i + 1) * TQ] = hb[:, n * TQ:(n + 1) * TQ]
    base = N_HEADS * HEAD_DIM
    for h in range(IDX_HEADS):
        hb = rot_t(pt[base + h * IDX_DIM:base + (h + 1) * IDX_DIM]).astype(bf16)
        for n in range(nqb):
            qit_ref[0, n, :, h * TQ:(h + 1) * TQ] = hb[:, n * TQ:(n + 1) * TQ]
    base += IDX_HEADS * IDX_DIM
    ones_rows = jnp.where(lax.broadcasted_iota(i32, (VT_ROWS - HEAD_DIM, T), 0) == 0, 1.0, 0.0)
    for g in range(KV_HEADS):
        vt_ref[0, 0, g, 0:HEAD_DIM, :] = pt[base + g * HEAD_DIM:base + (g + 1) * HEAD_DIM].astype(bf16)
        vt_ref[0, 0, g, HEAD_DIM:VT_ROWS, :] = ones_rows.astype(bf16)
    base += KV_HEADS * HEAD_DIM
    for n in range(nqb):
        wit_ref[0, n] = pt[base:base + WI_ROWS, n * TQ:(n + 1) * TQ]


def _attn_proj(x, w_row, w_t, cs, cst):
    B, S, _ = x.shape
    T = _row_tile(S, KEY_TILE)
    TQ = _row_tile(S, QUERY_TILE)
    nqb = T // TQ
    R = GROUPS * TQ
    row = lambda w: pl.BlockSpec((1, T, w), lambda b, t: (b, t, 0))
    names = ("k", "v", "ki", "kb", "kib", "vt", "qt", "qit", "wit")
    outs = pl.pallas_call(
        functools.partial(_attn_proj_kernel, T=T, TQ=TQ),
        grid=(B, S // T),
        in_specs=[row(D_MODEL), _const_spec((D_MODEL, WR_COLS)), _const_spec((WT_ROWS, D_MODEL)),
                  pl.BlockSpec((3, T, LANES), lambda b, t: (0, t, 0)),
                  pl.BlockSpec((2, ROT, T), lambda b, t: (0, 0, t))],
        out_specs=[row(128), row(128), row(IDX_DIM), row(128), row(IDX_DIM),
                   pl.BlockSpec((1, 1, KV_HEADS, VT_ROWS, T), lambda b, t: (b, t, 0, 0, 0)),
                   pl.BlockSpec((1, nqb, KV_HEADS, HEAD_DIM, R), lambda b, t: (b, t, 0, 0, 0)),
                   pl.BlockSpec((1, nqb, IDX_DIM, IDX_HEADS * TQ), lambda b, t: (b, t, 0, 0)),
                   pl.BlockSpec((1, nqb, WI_ROWS, TQ), lambda b, t: (b, t, 0, 0))],
        out_shape=[jax.ShapeDtypeStruct((B, S, 128), f32),
                   jax.ShapeDtypeStruct((B, S, 128), f32),
                   jax.ShapeDtypeStruct((B, S, IDX_DIM), f32),
                   jax.ShapeDtypeStruct((B, S, 128), bf16),
                   jax.ShapeDtypeStruct((B, S, IDX_DIM), bf16),
                   jax.ShapeDtypeStruct((B, S // T, KV_HEADS, VT_ROWS, T), bf16),
                   jax.ShapeDtypeStruct((B, S // TQ, KV_HEADS, HEAD_DIM, R), bf16),
                   jax.ShapeDtypeStruct((B, S // TQ, IDX_DIM, IDX_HEADS * TQ), bf16),
                   jax.ShapeDtypeStruct((B, S // TQ, WI_ROWS, TQ), f32)],
        compiler_params=_params(("parallel", "parallel")),
        name="attn_proj",
    )(x, w_row, w_t, cs, cst)
    return dict(zip(names, outs))


def _rotary_tables(pos):
    half = ROT // 2
    inv_freq = ROPE_THETA ** (-jnp.arange(half, dtype=f32) / half)
    ang = pos.astype(f32)[:, None] * inv_freq[None, :]
    cos = jnp.cos(ang)
    sin = jnp.sin(ang)
    S = pos.shape[0]
    one = jnp.ones((S, HEAD_DIM - ROT), f32)
    zero = jnp.zeros((S, HEAD_DIM - ROT), f32)
    zh = jnp.zeros((S, half), f32)
    c = jnp.concatenate([cos, cos, one], axis=1)
    sa = jnp.concatenate([-sin, zh, zero], axis=1)
    sb = jnp.concatenate([zh, sin, zero], axis=1)
    tile2 = lambda a: jnp.concatenate([a, a], axis=1)
    cs = jnp.stack([tile2(c), tile2(sa), tile2(sb)], axis=0)
    cst = jnp.stack([jnp.concatenate([cos, cos], axis=1).T,
                     jnp.concatenate([-sin, sin], axis=1).T], axis=0)
    return cs, cst


def _dsa_kernel(qt_ref, qit_ref, wit_ref, ki_ref, k_ref, vt_ref, tri_ref, o_ref,
                s_ref, pk_ref, qpad_ref, m_ref, acc_ref, *bufs, TQ, TK, P, top):
    lm_refs = (bufs[0:KV_HEADS], bufs[KV_HEADS:2 * KV_HEADS])
    p_refs = (bufs[2 * KV_HEADS:3 * KV_HEADS], bufs[3 * KV_HEADS:4 * KV_HEADS])
    qb = pl.program_id(1)
    q0 = P + qb * TQ
    nk = (q0 + TQ + TK - 1) // TK
    n_full = q0 // TK
    NC = TK // LANES
    R = GROUPS * TQ
    q_lim = ((q0 + lax.broadcasted_iota(i32, (1, TQ), 1)) // CHUNK + 1) * CHUNK
    wit = wit_ref[0, 0] * IDX_SCALE
    zero_head = jnp.zeros((HEAD_DIM, R), bf16)
    qpad_ref[0] = jnp.concatenate([qt_ref[0, 0, 0], zero_head], axis=0)
    qpad_ref[1] = jnp.concatenate([zero_head, qt_ref[0, 0, 1]], axis=0)

    TKP = TK // PACK

    def lead_digit(key):
        return (key >> DIGITS[0][0]) + (1 << (DIGITS[0][1] - 1))

    def score_tile(masked, j, carry):
        for c in range(NC):
            rows = slice(c * LANES, (c + 1) * LANES)
            rel = jnp.maximum(jnp.dot(ki_ref[0, j, rows, :], qit_ref[0, 0], preferred_element_type=f32), 0.0)
            score = wit[0:1, :] * rel[:, 0:TQ]
            for h in range(1, IDX_HEADS):
                score = score + wit[h:h + 1, :] * rel[:, h * TQ:(h + 1) * TQ]
            if masked:
                k_pos = j * TK + c * LANES + lax.broadcasted_iota(i32, (LANES, TQ), 0)
                score = jnp.where(k_pos < q_lim, score, NEG)
            bits = pltpu.bitcast(score, i32)
            key = jnp.where(bits >= 0, bits, bits ^ 0x7FFFFFFF)
            s_ref[j, rows, :] = key
            if c == 0:
                pk_ref[j] = lead_digit(key) | GUARD
            else:
                pk_ref[j] = pk_ref[j] | (lead_digit(key) << (8 * c))
        return carry

    lax.fori_loop(0, n_full, functools.partial(score_tile, False), 0)
    lax.fori_loop(n_full, nk, functools.partial(score_tile, True), 0)

    n_chunks = (nk + PACK - 1) // PACK

    def fill_guard(j, carry):
        pk_ref[j] = jnp.full((TKP, TQ), GUARD, i32)
        return carry

    lax.fori_loop(nk, n_chunks * PACK, fill_guard, 0)

    def count_ge(cand):
        cvec = cand * ONES

        def chunk(ci, tot):
            accs = [None] * 4
            for tt in range(PACK):
                for r in range(TKP // 8):
                    slab = pk_ref[ci * PACK + tt, r * 8:(r + 1) * 8, :]
                    hit = lax.shift_right_logical(slab - cvec, 7) & ONES
                    n = (tt * (TKP // 8) + r) % 4
                    accs[n] = hit if accs[n] is None else accs[n] + hit
            a = (accs[0] + accs[1]) + (accs[2] + accs[3])
            return tot + ((a & 0xFF) + ((a >> 8) & 0xFF) + ((a >> 16) & 0xFF)
                          + lax.shift_right_logical(a, 24))

        tot = lax.fori_loop(0, n_chunks, chunk, jnp.zeros((8, TQ), i32))
        return jnp.sum(tot, axis=0, keepdims=True)

    def repack(lo, width, prefix):
        origin = prefix * (1 << width)

        def tile(j, carry):
            for r in range(TKP // 8):
                word = jnp.full((8, TQ), GUARD, i32)
                for q in range(PACK):
                    key = s_ref[j, q * TKP + r * 8:q * TKP + (r + 1) * 8, :]
                    d = (key >> lo) - origin
                    in_range = pltpu.bitcast(d, jnp.uint32) < jnp.uint32(1 << width)
                    word = word | (jnp.where(in_range, d, 0) << (8 * q))
                pk_ref[j, r * 8:(r + 1) * 8, :] = word
            return carry
        lax.fori_loop(0, nk, tile, 0)

    base = jnp.zeros((1, TQ), i32)
    bsize = jnp.full((1, TQ), nk * TK, i32)
    prefix = None
    zero = jnp.zeros((1, TQ), i32)
    for stage, (lo, width) in enumerate(DIGITS):
        need = top - base

        def search_digit(stage=stage, lo=lo, width=width, prefix=prefix, need=need, bsize=bsize):
            if stage > 0:
                repack(lo, width, prefix)

            def bit_step(i, st):
                v, c_v, c_rej = st
                cand = v | jnp.left_shift(jnp.int32(1), width - 1 - i)
                c = count_ge(cand)
                ok = c >= need
                return jnp.where(ok, cand, v), jnp.where(ok, c, c_v), jnp.where(ok, c_rej, c)

            return lax.fori_loop(0, width, bit_step, (zero, bsize, zero))

        v, c_v, c_gt = search_digit()
        base, bsize = base + c_gt, c_v - c_gt
        prefix = v - (1 << (width - 1)) if stage == 0 else prefix * (1 << width) + v
    t = prefix
    neg_key = np.array(NEG, np.float32).view(np.int32) ^ 0x7FFFFFFF
    any_excess = jnp.max(jnp.where(base + bsize > top, 1, 0)) > 0

    @pl.when(jnp.logical_not(any_excess))
    def _():
        thr = jnp.where(t == neg_key, t + 1, t)

        def fast(j, carry):
            s_ref[j] = pltpu.bitcast(jnp.where(s_ref[j] >= thr, 0.0, NEG), i32)
            return carry
        lax.fori_loop(0, nk, fast, 0)

    @pl.when(any_excess)
    def _():
        need = jnp.where(t == neg_key, 0, top - base).astype(f32)

        def ranked(j, taken):
            keys = s_ref[j]
            for c in range(NC):
                kc = keys[c * LANES:(c + 1) * LANES, :]
                eq = jnp.where(kc == t, 1.0, 0.0)
                rank = taken + jnp.dot(tri_ref[...], eq.astype(bf16), preferred_element_type=f32)
                pick = jnp.where(kc > t, 1.0, jnp.where(rank <= need, eq, 0.0))
                s_ref[j, c * LANES:(c + 1) * LANES, :] = pltpu.bitcast(
                    jnp.where(pick > 0.5, 0.0, NEG), i32)
                taken = taken + jnp.sum(eq, axis=0, keepdims=True)
            return taken
        lax.fori_loop(0, nk, ranked, jnp.zeros((1, TQ), f32))

    m_ref[...] = jnp.full(m_ref.shape, NEG, f32)
    acc_ref[...] = jnp.zeros(acc_ref.shape, f32)

    def tile_step(nxt, cur):
        if cur is not None:
            j, slot, tile_max = cur
            m_old = [m_ref[g] for g in range(KV_HEADS)]
            m_new = [jnp.maximum(m_old[g], tile_max[g]) for g in range(KV_HEADS)]
            for g in range(KV_HEADS):
                m_ref[g] = m_new[g]
        if nxt is not None:
            jn, slot_n = nxt
            jc = jnp.minimum(jn, nk - 1)
        out = []
        for g in range(KV_HEADS):
            cmax = jnp.full((8, R), NEG, f32)
            if nxt is not None:
                raw = jnp.dot(k_ref[0, jc], qpad_ref[g], preferred_element_type=f32)
            for c in range(NC):
                rows = slice(c * LANES, (c + 1) * LANES)
                if nxt is not None:
                    bias = pltpu.bitcast(s_ref[jc, rows, :], f32)
                    lm = raw[rows, :] + jnp.concatenate([bias] * GROUPS, axis=1)
                    lm_refs[slot_n][g][rows, :] = lm
                    cmax = jnp.maximum(cmax, jnp.max(lm.reshape(LANES // 8, 8, R), axis=0))
                if cur is not None:
                    p_refs[slot][g][rows, :] = jnp.exp2(lm_refs[slot][g][rows, :] - m_new[g]).astype(bf16)
            out.append(jnp.max(cmax, axis=0, keepdims=True))
            if cur is not None:
                acc_ref[g] = (jnp.exp2(m_old[g] - m_new[g]) * acc_ref[g]
                              + jnp.dot(vt_ref[0, j, g], p_refs[slot][g][...], preferred_element_type=f32))
        return tuple(out)

    def tile_pair(i, tile_max):
        nxt_max = tile_step((2 * i + 1, 1), (2 * i, 0, tile_max))
        return tile_step((2 * i + 2, 0), (2 * i + 1, 1, nxt_max))

    last_max = lax.fori_loop(0, nk // 2, tile_pair, tile_step((0, 0), None))

    @pl.when(nk % 2 == 1)
    def _():
        tile_step(None, (nk - 1, 0, last_max))

    for g in range(KV_HEADS):
        o_t = acc_ref[g, 0:HEAD_DIM, :] / acc_ref[g, HEAD_DIM:HEAD_DIM + 1, :]
        for i in range(0, GROUPS, 2):
            pair = jnp.concatenate([o_t[:, i * TQ:(i + 1) * TQ], o_t[:, (i + 1) * TQ:(i + 2) * TQ]], axis=0)
            h = g * GROUPS + i
            o_ref[0, :, h * HEAD_DIM:(h + 2) * HEAD_DIM] = pair.T.astype(bf16)


def _dsa(qt, qit, wit, kb, kib, vt, P, top):
    B, nq = qt.shape[:2]
    R = qt.shape[-1]
    TQ = R // GROUPS
    NT, TK = kb.shape[1:3]
    assert top <= TK and TQ % CHUNK == 0
    tri = jnp.asarray(np.tril(np.ones((LANES, LANES), np.float32)), bf16)
    return pl.pallas_call(
        functools.partial(_dsa_kernel, TQ=TQ, TK=TK, P=P, top=top),
        grid=(B, nq),
        in_specs=[pl.BlockSpec((1, 1, KV_HEADS, HEAD_DIM, R), lambda b, i: (b, i, 0, 0, 0)),
                  pl.BlockSpec((1, 1, IDX_DIM, IDX_HEADS * TQ), lambda b, i: (b, i, 0, 0)),
                  pl.BlockSpec((1, 1, WI_ROWS, TQ), lambda b, i: (b, i, 0, 0)),
                  pl.BlockSpec((1, NT, TK, IDX_DIM), lambda b, i: (b, 0, 0, 0)),
                  pl.BlockSpec((1, NT, TK, KV_HEADS * HEAD_DIM), lambda b, i: (b, 0, 0, 0)),
                  pl.BlockSpec((1, NT, KV_HEADS, VT_ROWS, TK), lambda b, i: (b, 0, 0, 0, 0)),
                  _const_spec((LANES, LANES))],
        out_specs=pl.BlockSpec((1, TQ, N_HEADS * HEAD_DIM), lambda b, i: (b, i, 0)),
        out_shape=jax.ShapeDtypeStruct((B, nq * TQ, N_HEADS * HEAD_DIM), bf16),
        scratch_shapes=[pltpu.VMEM((NT, TK, TQ), i32),
                        pltpu.VMEM((-(-NT // PACK) * PACK, TK // PACK, TQ), i32),
                        pltpu.VMEM((KV_HEADS, KV_HEADS * HEAD_DIM, R), bf16),
                        pltpu.VMEM((KV_HEADS, 1, R), f32),
                        pltpu.VMEM((KV_HEADS, VT_ROWS, R), f32)]
        + [pltpu.VMEM((TK, R), f32)] * (2 * KV_HEADS)
        + [pltpu.VMEM((TK, R), bf16)] * (2 * KV_HEADS),
        compiler_params=_params(("parallel", "arbitrary")),
        name="dsa",
    )(qt, qit, wit, kib, kb, vt, tri)


def _hgrn_mats(T):
    r = np.arange(T)[:, None]
    c = np.arange(T)[None, :]
    mats = [c <= r]
    b = T // 2
    while b >= 2:
        blk_r = r // (2 * b)
        mid = blk_r * 2 * b + b
        same = blk_r == c // (2 * b)
        late = r >= mid
        mats.append(same & ((late & (c >= mid) & (c <= r)) | (~late & (c > r) & (c < mid))))
        b //= 2
    return np.concatenate(mats, axis=0).astype(np.float32)


def _hgrn_kernel(x_ref, w_ref, lb_ref, g_ref, mats_ref, s0_ref, y_ref, st_out_ref, st_ref, *, T, SUBS):
    t_idx = pl.program_id(1)

    @pl.when(t_idx == 0)
    def _():
        st_ref[...] = s0_ref[0]

    lb = lb_ref[...]
    mats = mats_ref[...]
    row = lax.broadcasted_iota(i32, (T, 1), 0)
    col = lax.broadcasted_iota(i32, (1, T), 1)
    nt = (((1,), (1,)), ((), ()))
    tn = (((0,), (0,)), ((), ()))

    def prepare(i):
        xb = x_ref[0, i * T:(i + 1) * T, :].astype(bf16)
        proj = jnp.dot(xb, w_ref[...], preferred_element_type=f32)
        hq = proj[:, 0:HGRN_W]
        z = proj[:, HGRN_W:2 * HGRN_W]
        hv = proj[:, 2 * HGRN_W:3 * HGRN_W]
        hg = proj[:, 3 * HGRN_W:4 * HGRN_W]
        logf = (jnp.minimum(z, 0.0) - jnp.log1p(jnp.exp(-jnp.abs(z)))) + jnp.log1p(lb * jnp.exp(-z))
        kk = (1.0 - lb) * _sigmoid(-z)
        qq = hq * _sigmoid(hq)
        gate = hg * _sigmoid(hg)
        hi = logf.astype(bf16)
        r1 = logf - hi.astype(f32)
        mid = r1.astype(bf16)
        lo = (r1 - mid.astype(f32)).astype(bf16)
        e_all = jnp.dot(mats, jnp.concatenate([hi, mid, lo], axis=0),
                        preferred_element_type=f32)
        return qq, kk, hv, gate, logf, e_all

    def recur(i, prepared):
        qq, kk, hv, gate, logf, e_all = prepared
        for h in range(HGRN_HEADS):
            sl = slice(h * HGRN_DK, (h + 1) * HGRN_DK)
            q_h, k_h, v_h = qq[:, sl], kk[:, sl], hv[:, sl]
            vb = v_h.astype(bf16)
            cum = e_all[0:T, sl]
            suf = cum[T - 1:T, :] - cum
            st = st_ref[h]
            o = lax.dot_general((q_h * jnp.exp(cum)).astype(bf16), st.astype(bf16), nt,
                                preferred_element_type=f32)
            o = o + jnp.sum(q_h * k_h, axis=-1, keepdims=True) * v_h
            scores = jnp.zeros((T, T), f32)
            b = T // 2
            lvl = 1
            while b >= 1:
                late = ((row // b) % 2) == 1
                if b >= 2:
                    xdec = jnp.exp(e_all[lvl * T:(lvl + 1) * T, sl])
                else:
                    xdec = jnp.where(late, jnp.exp(logf[:, sl]), 1.0)
                a_m = jnp.where(late, q_h * xdec, 0.0).astype(bf16)
                b_m = jnp.where(late, 0.0, k_h * xdec).astype(bf16)
                sc = lax.dot_general(a_m, b_m, nt, preferred_element_type=f32)
                if 2 * b < T:
                    sc = jnp.where((row // (2 * b)) == (col // (2 * b)), sc, 0.0)
                scores = scores + sc
                b //= 2
                lvl += 1
            o = o + jnp.dot(scores.astype(bf16), vb, preferred_element_type=f32)
            kd = (k_h * jnp.exp(suf)).astype(bf16)
            st_ref[h] = (st * jnp.exp(cum[T - 1:T, :])
                         + lax.dot_general(vb, kd, tn, preferred_element_type=f32))
            o = o * lax.rsqrt(jnp.mean(o * o, axis=-1, keepdims=True) + LN_EPS) * g_ref[...]
            y_ref[0, i * T:(i + 1) * T, sl] = (o * gate[:, sl]).astype(bf16)

    nxt = prepare(0)
    for i in range(SUBS):
        cur = nxt
        if i + 1 < SUBS:
            nxt = prepare(i + 1)
        recur(i, cur)

    st_out_ref[0] = st_ref[...]


def _hgrn(x, w_h, lb, norm_g, s0):
    B, S, _ = x.shape
    T = _row_tile(S, 2 * CHUNK)
    SUBS = HGRN_SUBTILES if S % (T * HGRN_SUBTILES) == 0 else 1
    TS = T * SUBS
    mats = jnp.asarray(np.tile(_hgrn_mats(T), (1, 3)), bf16)
    nm = mats.shape[0]
    st0 = jnp.swapaxes(s0.astype(f32), 2, 3)
    y, st = pl.pallas_call(
        functools.partial(_hgrn_kernel, T=T, SUBS=SUBS),
        grid=(B, S // TS),
        in_specs=[pl.BlockSpec((1, TS, D_MODEL), lambda b, t: (b, t, 0)),
                  _const_spec((D_MODEL, 4 * HGRN_W)),
                  _const_spec((1, HGRN_W)),
                  _const_spec((1, HGRN_DV)),
                  _const_spec((nm, 3 * T)),
                  pl.BlockSpec((1, HGRN_HEADS, HGRN_DV, HGRN_DK), lambda b, t: (b, 0, 0, 0))],
        out_specs=[pl.BlockSpec((1, TS, HGRN_W), lambda b, t: (b, t, 0)),
                   pl.BlockSpec((1, HGRN_HEADS, HGRN_DV, HGRN_DK), lambda b, t: (b, 0, 0, 0))],
        out_shape=[jax.ShapeDtypeStruct((B, S, HGRN_W), bf16),
                   jax.ShapeDtypeStruct((B, HGRN_HEADS, HGRN_DV, HGRN_DK), f32)],
        scratch_shapes=[pltpu.VMEM((HGRN_HEADS, HGRN_DV, HGRN_DK), f32)],
        compiler_params=_params(("parallel", "arbitrary")),
        name="hgrn",
    )(x, w_h, lb, norm_g, mats, st0)
    return y, jnp.swapaxes(st, 2, 3)


HIST = CONV_W - 1
PADROWS = 8


def _conv_rows(buf_ref, cols, u, w, bias, T):
    buf_ref[PADROWS:PADROWS + T, cols] = u
    y = bias
    for j in range(CONV_W):
        y = y + w[j:j + 1, :] * buf_ref[PADROWS - HIST + j:PADROWS - HIST + j + T, cols]
    buf_ref[PADROWS - HIST:PADROWS, cols] = u[T - HIST:T, :]
    return y


def _merge_kernel(x_ref, ya_ref, yb_ref, hist_ref, wc_ref, wg_ref, wbr_ref, wout_ref, cw_ref, cb_ref,
                  g_ref, b_ref, o_ref, hist_out_ref, buf_ref, *, T, alpha):
    @pl.when(pl.program_id(1) == 0)
    def _():
        buf_ref[PADROWS - HIST:PADROWS, :] = hist_ref[0]

    x = x_ref[0]
    xb = x.astype(bf16)
    c = jnp.dot(xb, wc_ref[...], preferred_element_type=f32)
    W = SCONV_WIDTH
    u = c[:, W:2 * W] * c[:, 2 * W:3 * W]
    u_conv = _conv_rows(buf_ref, slice(0, W), u, cw_ref[...], cb_ref[...], T)
    yc = (c[:, 0:W] * u_conv).astype(bf16)
    hist_out_ref[0] = buf_ref[PADROWS - HIST:PADROWS, :]

    merged = None
    for i, yb in enumerate((ya_ref[0], yb_ref[0], yc)):
        gate = _sigmoid(jnp.dot(xb, wg_ref[:, i * D_MODEL:(i + 1) * D_MODEL], preferred_element_type=f32))
        term = gate * jnp.dot(yb, wbr_ref[i], preferred_element_type=f32)
        merged = term if merged is None else merged + term
    out = jnp.dot(merged.astype(bf16), wout_ref[...], preferred_element_type=f32)
    o_ref[0] = _layer_norm(alpha * x + out, g_ref[...], b_ref[...])


def _merge(x, ya, yb, hist, wc, wg, wbr, wout, cw, cb, g, b, alpha):
    B, S, _ = x.shape
    T = _row_tile(S, 256)
    row = lambda w: pl.BlockSpec((1, T, w), lambda bb, t: (bb, t, 0))
    hspec = pl.BlockSpec((1, HIST, SCONV_WIDTH), lambda bb, t: (bb, 0, 0))
    return pl.pallas_call(
        functools.partial(_merge_kernel, T=T, alpha=alpha),
        grid=(B, S // T),
        in_specs=[row(D_MODEL), row(BRANCH_WIDTH), row(BRANCH_WIDTH), hspec,
                  _const_spec((D_MODEL, 3 * SCONV_WIDTH)), _const_spec((D_MODEL, N_BRANCH * D_MODEL)),
                  _const_spec((N_BRANCH, BRANCH_WIDTH, D_MODEL)), _const_spec((D_MODEL, D_MODEL)),
                  _const_spec((CONV_W, SCONV_WIDTH)), _const_spec((1, SCONV_WIDTH)),
                  _const_spec((1, D_MODEL)), _const_spec((1, D_MODEL))],
        out_specs=[row(D_MODEL), hspec],
        out_shape=[jax.ShapeDtypeStruct((B, S, D_MODEL), f32),
                   jax.ShapeDtypeStruct((B, HIST, SCONV_WIDTH), f32)],
        scratch_shapes=[pltpu.VMEM((PADROWS + T, SCONV_WIDTH), f32)],
        compiler_params=_params(("parallel", "arbitrary")),
        name="merge",
    )(x, ya, yb, hist, wc, wg, wbr, wout, cw, cb, g, b)


def _ffn_kernel(x_ref, hist_ref, wup_ref, cw_ref, cb_ref, wdn_ref, g_ref, b_ref, o_ref, hist_out_ref,
                *bufs, T, alpha):
    n_steps = D_FF // FFN_COLS
    col_ranges = [slice(off + c * FFN_COLS, off + (c + 1) * FFN_COLS)
                  for c in range(n_steps) for off in (0, D_FF)]

    @pl.when(pl.program_id(1) == 0)
    def _():
        for buf, cols in zip(bufs, col_ranges):
            buf[PADROWS - HIST:PADROWS, :] = hist_ref[0, :, cols]

    x = x_ref[0]
    xb = x.astype(bf16)

    def up_conv(cidx):
        halves = []
        for half in range(2):
            buf, cols = bufs[2 * cidx + half], col_ranges[2 * cidx + half]
            h = jnp.dot(xb, wup_ref[:, cols], preferred_element_type=f32)
            halves.append(_conv_rows(buf, slice(None), h, cw_ref[:, cols], cb_ref[:, cols], T))
            hist_out_ref[0, :, cols] = buf[PADROWS - HIST:PADROWS, :]
        return halves

    groups = [list(range(c, min(c + FFN_GROUP, n_steps))) for c in range(0, n_steps, FFN_GROUP)]
    acc = jnp.zeros((T, D_MODEL), f32)
    nxt = [up_conv(c) for c in groups[0]]
    for gi, grp in enumerate(groups):
        cur = nxt
        if gi + 1 < len(groups):
            nxt = [up_conv(c) for c in groups[gi + 1]]
        act = jnp.concatenate([(a_g * _sigmoid(a_g) * b_v).astype(bf16) for a_g, b_v in cur], axis=1)
        acc = acc + jnp.dot(act, wdn_ref[grp[0] * FFN_COLS:(grp[-1] + 1) * FFN_COLS, :],
                            preferred_element_type=f32)
    o_ref[0] = _layer_norm(alpha * x + acc, g_ref[...], b_ref[...])


def _ffn(x, hist, wup, cw, cb, wdn, g, b, alpha):
    B, S, _ = x.shape
    T = _row_tile(S, 256)
    row = pl.BlockSpec((1, T, D_MODEL), lambda bb, t: (bb, t, 0))
    hspec = pl.BlockSpec((1, HIST, 2 * D_FF), lambda bb, t: (bb, 0, 0))
    return pl.pallas_call(
        functools.partial(_ffn_kernel, T=T, alpha=alpha),
        grid=(B, S // T),
        in_specs=[row, hspec, _const_spec((D_MODEL, 2 * D_FF)), _const_spec((CONV_W, 2 * D_FF)),
                  _const_spec((1, 2 * D_FF)), _const_spec((D_FF, D_MODEL)),
                  _const_spec((1, D_MODEL)), _const_spec((1, D_MODEL))],
        out_specs=[row, hspec],
        out_shape=[jax.ShapeDtypeStruct((B, S, D_MODEL), f32),
                   jax.ShapeDtypeStruct((B, HIST, 2 * D_FF), f32)],
        scratch_shapes=[pltpu.VMEM((PADROWS + T, FFN_COLS), f32)] * (2 * D_FF // FFN_COLS),
        compiler_params=_params(("parallel", "arbitrary")),
        name="ffn",
    )(x, hist, wup, cw, cb, wdn, g, b)


def _cast_kernel(x_ref, o_ref):
    o_ref[...] = x_ref[...].astype(o_ref.dtype)


def _to_bf16(w):
    D, Rw, C = w.shape
    T = _row_tile(Rw, LANES)
    spec = pl.BlockSpec((1, T, C), lambda d, t: (d, t, 0))
    return pl.pallas_call(_cast_kernel, grid=(D, Rw // T), in_specs=[spec], out_specs=spec,
                          out_shape=jax.ShapeDtypeStruct(w.shape, bf16),
                          compiler_params=_params(("parallel", "parallel")), name="to_bf16")(w)


def _layer_weights(l, depth, w_in, hgrn_lb_logits, hgrn_norm_g, sconv_w, sconv_b, w_branch, w_out,
                   ln1_g, ln1_b, w_up, ffn_conv_w, ffn_conv_b, w_down, ln2_g, ln2_b):
    offs = np.concatenate([[0], np.cumsum(IN_SIZES)]).tolist()
    sec = lambda i, j=None: w_in[l][:, offs[i]:offs[(i if j is None else j) + 1]]
    a_q, a_k, a_v, i_q, i_k, i_w = (sec(i) for i in range(6))
    zpad = lambda n: jnp.zeros((D_MODEL, n), w_in.dtype)
    w_row = jnp.concatenate([a_k, i_k, zpad(LANES - IDX_DIM), a_v], axis=1).astype(bf16)
    w_t = jnp.concatenate([a_q, i_q, a_v, i_w, zpad(WT_ROWS - 900)], axis=1).T.astype(bf16)
    lbp = jax.nn.softmax(hgrn_lb_logits.astype(f32), axis=0)
    lb = (jnp.cumsum(lbp, axis=0) - lbp[0])[l].reshape(1, HGRN_W)
    row = lambda a: a.reshape(1, -1).astype(f32)
    return dict(
        w_row=w_row, w_t=w_t, w_h=sec(6, 9).astype(bf16), lb=lb, norm_g=row(hgrn_norm_g[l]),
        w_c=sec(10, 12).astype(bf16), w_g=sec(13).astype(bf16),
        w_br=w_branch[l].astype(bf16), w_out=w_out[l].astype(bf16),
        sconv_w=sconv_w[l].astype(f32), sconv_b=row(sconv_b[l]),
        ln1_g=row(ln1_g[l]), ln1_b=row(ln1_b[l]),
        w_up=w_up[l].astype(bf16), ffn_w=ffn_conv_w[l].astype(f32), ffn_b=row(ffn_conv_b[l]),
        w_down=w_down[l].astype(bf16), ln2_g=row(ln2_g[l]), ln2_b=row(ln2_b[l]))


def _key_tiles(a, TK):
    B, L, C = a.shape
    NT = -(-L // TK)
    a = jnp.pad(a, ((0, 0), (0, NT * TK - L), (0, 0)))
    return a.reshape(B, NT, TK, C)


def _trunk_layer(x, tables, P, k_past, v_past, ki_past, s0, sc_hist, ffn_hist, w, alpha):
    B, S, _ = x.shape
    TK = KEY_TILE
    a = _attn_proj(x, w["w_row"], w["w_t"], *tables)
    k, v, ki = a["k"], a["v"], a["ki"]
    L = P + S
    if k_past is None and S % TK == 0:
        kb = a["kb"].reshape(B, S // TK, TK, -1)
        kib = a["kib"].reshape(B, S // TK, TK, -1)
        vt = a["vt"]
    else:
        kb, kib, v_all = a["kb"], a["kib"], v.astype(bf16)
        if k_past is not None:
            kb = jnp.concatenate([k_past.reshape(B, P, -1).astype(bf16), kb], axis=1)
            kib = jnp.concatenate([ki_past.astype(bf16), kib], axis=1)
            v_all = jnp.concatenate([v_past.reshape(B, P, -1).astype(bf16), v_all], axis=1)
        kb, kib = _key_tiles(kb, TK), _key_tiles(kib, TK)
        vt = _key_tiles(v_all, TK).reshape(B, -1, TK, KV_HEADS, HEAD_DIM).transpose(0, 1, 3, 4, 2)
        ones = jnp.zeros(vt.shape[:3] + (VT_ROWS - HEAD_DIM, TK), bf16).at[:, :, :, 0, :].set(1.0)
        vt = jnp.concatenate([vt, ones], axis=3)
    y_a = _dsa(a["qt"], a["qit"], a["wit"], kb, kib, vt, P, min(TOPK_MAX, L // 4))
    y_b, s_new = _hgrn(x, w["w_h"], w["lb"], w["norm_g"], s0)
    x1, sc_new = _merge(x, y_a, y_b, sc_hist, w["w_c"], w["w_g"], w["w_br"], w["w_out"],
                        w["sconv_w"], w["sconv_b"], w["ln1_g"], w["ln1_b"], alpha)
    x2, ffn_new = _ffn(x1, ffn_hist, w["w_up"], w["ffn_w"], w["ffn_b"], w["w_down"],
                       w["ln2_g"], w["ln2_b"], alpha)
    new = (k.reshape(B, S, KV_HEADS, HEAD_DIM), v.reshape(B, S, KV_HEADS, HEAD_DIM), ki,
           s_new, sc_new, ffn_new)
    return x2, new


def kernel(x_prompt, x_sample, cache_attn_k, cache_attn_v, cache_idx_k, state_hgrn, state_sconv,
           state_ffn_conv, w_in, hgrn_lb_logits, hgrn_norm_g, sconv_w, sconv_b, w_branch, w_out,
           ln1_g, ln1_b, w_up, ffn_conv_w, ffn_conv_b, w_down, ln2_g, ln2_b):
    depth = w_in.shape[0]
    alpha = (2 * depth) ** 0.25
    B, S, _ = x_prompt.shape
    DB, DS, _ = x_sample.shape
    P = cache_attn_k.shape[2]
    cs_p = _rotary_tables(jnp.arange(S))
    cs_s = _rotary_tables(P + jnp.arange(DS))
    xp, xs = x_prompt, x_sample
    st_p = [[] for _ in range(6)]
    st_s = [[] for _ in range(6)]
    w_in_b = _to_bf16(w_in)
    for l in range(depth):
        w = _layer_weights(l, depth, w_in_b, hgrn_lb_logits, hgrn_norm_g, sconv_w, sconv_b, w_branch,
                           w_out, ln1_g, ln1_b, w_up, ffn_conv_w, ffn_conv_b, w_down, ln2_g, ln2_b)
        xp, new_p = _trunk_layer(
            xp, cs_p, 0, None, None, None,
            jnp.zeros((B, HGRN_HEADS, HGRN_DK, HGRN_DV), f32),
            jnp.zeros((B, HIST, SCONV_WIDTH), f32),
            jnp.zeros((B, HIST, 2 * D_FF), f32), w, alpha)
        xs, new_s = _trunk_layer(
            xs, cs_s, P, cache_attn_k[l], cache_attn_v[l], cache_idx_k[l], state_hgrn[l],
            state_sconv[l], state_ffn_conv[l], w, alpha)
        for j in range(6):
            st_p[j].append(new_p[j])
            st_s[j].append(new_s[j])
    outs_p = [jnp.stack(a, axis=0) for a in st_p]
    outs_s = [jnp.stack(a, axis=0) for a in st_s]
    return (xp, xs, *outs_p, *outs_s)
```

```python
import functools

import numpy as np
import jax
import jax.numpy as jnp
from jax import lax
from jax.experimental import pallas as pl
from jax.experimental.pallas import tpu as pltpu

D_MODEL = 1024
CHUNK = 64
N_HEADS = 8
HEAD_DIM = 64
KV_HEADS = 2
GROUPS = N_HEADS // KV_HEADS
IDX_HEADS = 4
IDX_DIM = 64
TOPK_MAX = 256
ROPE_THETA = 500000.0
ATTN_SCALE = HEAD_DIM ** -0.5
IDX_SCALE = (IDX_DIM ** -0.5) * (IDX_HEADS ** -0.5)
NEG = -1e30
HGRN_HEADS = 4
HGRN_DK = 128
HGRN_DV = 128
HGRN_W = HGRN_HEADS * HGRN_DK
SCONV_WIDTH = 512
CONV_W = 3
BRANCH_WIDTH = 512
N_BRANCH = 3
D_FF = 2816
LN_EPS = 1e-5
IN_SIZES = (N_HEADS * HEAD_DIM, KV_HEADS * HEAD_DIM, KV_HEADS * HEAD_DIM,
            IDX_HEADS * IDX_DIM, IDX_DIM, IDX_HEADS,
            HGRN_W, HGRN_W, HGRN_W, HGRN_W,
            SCONV_WIDTH, SCONV_WIDTH, SCONV_WIDTH,
            N_BRANCH * D_MODEL)

LANES = 128
VMEM_LIMIT = 56 * 1024 * 1024
KEY_TILE = 512
QUERY_TILE = 256
SCORE_UNROLL = 4
VT_ROWS = HEAD_DIM + 16
FFN_COLS = 256
FFN_GROUP = 5
HGRN_SUBTILES = 4
PACK = 4
DIGITS = ((25, 7), (18, 7), (11, 7), (4, 7), (0, 4))
GUARD = np.int32(-0x7F7F7F80)
ONES = np.int32(0x01010101)

f32 = jnp.float32
bf16 = jnp.bfloat16
i32 = jnp.int32


def _row_tile(s, want):
    return want if s % want == 0 else s


def _const_spec(shape):
    nd = len(shape)
    return pl.BlockSpec(shape, lambda *_: (0,) * nd, pipeline_mode=pl.Buffered(1))


def _params(sem):
    return pltpu.CompilerParams(dimension_semantics=sem, vmem_limit_bytes=VMEM_LIMIT)


def _layer_norm(z, g, b):
    mu = jnp.mean(z, axis=-1, keepdims=True)
    d = z - mu
    var = jnp.mean(d * d, axis=-1, keepdims=True)
    return d * lax.rsqrt(var + LN_EPS) * g + b


def _sigmoid(x):
    return 1.0 / (1.0 + jnp.exp(-x))


Q_PRESCALE = ATTN_SCALE * float(np.log2(np.e))
ROT = HEAD_DIM // 4
WR_COLS = 384
WT_ROWS = 912
WI_ROWS = 8


def _attn_proj_kernel(x_ref, wr_ref, wt_ref, cs_ref, cst_ref, k_ref, v_ref, ki_ref, kb_ref, kib_ref,
                      vt_ref, qt_ref, qit_ref, wit_ref, *, T, TQ):
    xb = x_ref[0].astype(bf16)
    pr = jnp.dot(xb, wr_ref[...], preferred_element_type=f32)
    c, sa, sb = cs_ref[0], cs_ref[1], cs_ref[2]

    def rot(xg):
        return xg * c + pltpu.roll(xg, LANES - ROT // 2, 1) * sa + pltpu.roll(xg, ROT // 2, 1) * sb

    k = rot(pr[:, 0:LANES])
    ki = rot(pr[:, LANES:2 * LANES])[:, :IDX_DIM]
    k_ref[0] = k
    kb_ref[0] = k.astype(bf16)
    ki_ref[0] = ki
    kib_ref[0] = ki.astype(bf16)
    v_ref[0] = pr[:, 2 * LANES:3 * LANES]

    nt = (((1,), (1,)), ((), ()))
    pt = lax.dot_general(wt_ref[...], xb, nt, preferred_element_type=f32)
    ct, st = cst_ref[0], cst_ref[1]

    def rot_t(hb):
        lead = hb[0:ROT] * ct + jnp.concatenate([hb[ROT // 2:ROT], hb[0:ROT // 2]], axis=0) * st
        return jnp.concatenate([lead, hb[ROT:HEAD_DIM]], axis=0)

    nqb = T // TQ
    for h in range(N_HEADS):
        hb = (rot_t(pt[h * HEAD_DIM:(h + 1) * HEAD_DIM]) * Q_PRESCALE).astype(bf16)
        g, i = divmod(h, GROUPS)
        for n in range(nqb):
            qt_ref[0, n, g, :, i * TQ:(i + 1) * TQ] = hb[:, n * TQ:(n + 1) * TQ]
    base = N_HEADS * HEAD_DIM
    for h in range(IDX_HEADS):
        hb = rot_t(pt[base + h * IDX_DIM:base + (h + 1) * IDX_DIM]).astype(bf16)
        for n in range(nqb):
            qit_ref[0, n, :, h * TQ:(h + 1) * TQ] = hb[:, n * TQ:(n + 1) * TQ]
    base += IDX_HEADS * IDX_DIM
    ones_rows = jnp.where(lax.broadcasted_iota(i32, (VT_ROWS - HEAD_DIM, T), 0) == 0, 1.0, 0.0)
    for g in range(KV_HEADS):
        vt_ref[0, 0, g, 0:HEAD_DIM, :] = pt[base + g * HEAD_DIM:base + (g + 1) * HEAD_DIM].astype(bf16)
        vt_ref[0, 0, g, HEAD_DIM:VT_ROWS, :] = ones_rows.astype(bf16)
    base += KV_HEADS * HEAD_DIM
    for n in range(nqb):
        wit_ref[0, n] = pt[base:base + WI_ROWS, n * TQ:(n + 1) * TQ]


def _attn_proj(x, w_row, w_t, cs, cst):
    B, S, _ = x.shape
    T = _row_tile(S, KEY_TILE)
    TQ = _row_tile(S, QUERY_TILE)
    nqb = T // TQ
    R = GROUPS * TQ
    row = lambda w: pl.BlockSpec((1, T, w), lambda b, t: (b, t, 0))
    names = ("k", "v", "ki", "kb", "kib", "vt", "qt", "qit", "wit")
    outs = pl.pallas_call(
        functools.partial(_attn_proj_kernel, T=T, TQ=TQ),
        grid=(B, S // T),
        in_specs=[row(D_MODEL), _const_spec((D_MODEL, WR_COLS)), _const_spec((WT_ROWS, D_MODEL)),
                  pl.BlockSpec((3, T, LANES), lambda b, t: (0, t, 0)),
                  pl.BlockSpec((2, ROT, T), lambda b, t: (0, 0, t))],
        out_specs=[row(128), row(128), row(IDX_DIM), row(128), row(IDX_DIM),
                   pl.BlockSpec((1, 1, KV_HEADS, VT_ROWS, T), lambda b, t: (b, t, 0, 0, 0)),
                   pl.BlockSpec((1, nqb, KV_HEADS, HEAD_DIM, R), lambda b, t: (b, t, 0, 0, 0)),
                   pl.BlockSpec((1, nqb, IDX_DIM, IDX_HEADS * TQ), lambda b, t: (b, t, 0, 0)),
                   pl.BlockSpec((1, nqb, WI_ROWS, TQ), lambda b, t: (b, t, 0, 0))],
        out_shape=[jax.ShapeDtypeStruct((B, S, 128), f32),
                   jax.ShapeDtypeStruct((B, S, 128), f32),
                   jax.ShapeDtypeStruct((B, S, IDX_DIM), f32),
                   jax.ShapeDtypeStruct((B, S, 128), bf16),
                   jax.ShapeDtypeStruct((B, S, IDX_DIM), bf16),
                   jax.ShapeDtypeStruct((B, S // T, KV_HEADS, VT_ROWS, T), bf16),
                   jax.ShapeDtypeStruct((B, S // TQ, KV_HEADS, HEAD_DIM, R), bf16),
                   jax.ShapeDtypeStruct((B, S // TQ, IDX_DIM, IDX_HEADS * TQ), bf16),
                   jax.ShapeDtypeStruct((B, S // TQ, WI_ROWS, TQ), f32)],
        compiler_params=_params(("parallel", "parallel")),
        name="attn_proj",
    )(x, w_row, w_t, cs, cst)
    return dict(zip(names, outs))


def _rotary_tables(pos):
    half = ROT // 2
    inv_freq = ROPE_THETA ** (-jnp.arange(half, dtype=f32) / half)
    ang = pos.astype(f32)[:, None] * inv_freq[None, :]
    cos = jnp.cos(ang)
    sin = jnp.sin(ang)
    S = pos.shape[0]
    one = jnp.ones((S, HEAD_DIM - ROT), f32)
    zero = jnp.zeros((S, HEAD_DIM - ROT), f32)
    zh = jnp.zeros((S, half), f32)
    c = jnp.concatenate([cos, cos, one], axis=1)
    sa = jnp.concatenate([-sin, zh, zero], axis=1)
    sb = jnp.concatenate([zh, sin, zero], axis=1)
    tile2 = lambda a: jnp.concatenate([a, a], axis=1)
    cs = jnp.stack([tile2(c), tile2(sa), tile2(sb)], axis=0)
    cst = jnp.stack([jnp.concatenate([cos, cos], axis=1).T,
                     jnp.concatenate([-sin, sin], axis=1).T], axis=0)
    return cs, cst


def _dsa_kernel(qt_ref, qit_ref, wit_ref, ki_ref, k_ref, vt_ref, tri_ref, o_ref,
                s_ref, pk_ref, qpad_ref, m_ref, acc_ref, *bufs, TQ, TK, P, top):
    lm_refs = (bufs[0:KV_HEADS], bufs[KV_HEADS:2 * KV_HEADS])
    p_refs = (bufs[2 * KV_HEADS:3 * KV_HEADS], bufs[3 * KV_HEADS:4 * KV_HEADS])
    qb = pl.program_id(1)
    q0 = P + qb * TQ
    nk = (q0 + TQ + TK - 1) // TK
    n_full = q0 // TK
    NC = TK // LANES
    R = GROUPS * TQ
    q_lim = ((q0 + lax.broadcasted_iota(i32, (1, TQ), 1)) // CHUNK + 1) * CHUNK
    wit = wit_ref[0, 0] * IDX_SCALE
    zero_head = jnp.zeros((HEAD_DIM, R), bf16)
    qpad_ref[0] = jnp.concatenate([qt_ref[0, 0, 0], zero_head], axis=0)
    qpad_ref[1] = jnp.concatenate([zero_head, qt_ref[0, 0, 1]], axis=0)

    TKP = TK // PACK

    def lead_digit(key):
        return (key >> DIGITS[0][0]) + (1 << (DIGITS[0][1] - 1))

    def score_tile(masked, j, carry):
        for c in range(NC):
            rows = slice(c * LANES, (c + 1) * LANES)
            rel = jnp.maximum(jnp.dot(ki_ref[0, j, rows, :], qit_ref[0, 0], preferred_element_type=f32), 0.0)
            score = wit[0:1, :] * rel[:, 0:TQ]
            for h in range(1, IDX_HEADS):
                score = score + wit[h:h + 1, :] * rel[:, h * TQ:(h + 1) * TQ]
            if masked:
                k_pos = j * TK + c * LANES + lax.broadcasted_iota(i32, (LANES, TQ), 0)
                score = jnp.where(k_pos < q_lim, score, NEG)
            bits = pltpu.bitcast(score, i32)
            key = jnp.where(bits >= 0, bits, bits ^ 0x7FFFFFFF)
            s_ref[j, rows, :] = key
            if c == 0:
                pk_ref[j] = lead_digit(key) | GUARD
            else:
                pk_ref[j] = pk_ref[j] | (lead_digit(key) << (8 * c))
        return carry

    def score_group(i, carry):
        for u in range(SCORE_UNROLL):
            carry = score_tile(False, SCORE_UNROLL * i + u, carry)
        return carry

    lax.fori_loop(0, n_full // SCORE_UNROLL, score_group, 0)
    lax.fori_loop(n_full // SCORE_UNROLL * SCORE_UNROLL, n_full, functools.partial(score_tile, False), 0)
    lax.fori_loop(n_full, nk, functools.partial(score_tile, True), 0)

    n_chunks = (nk + PACK - 1) // PACK

    def fill_guard(j, carry):
        pk_ref[j] = jnp.full((TKP, TQ), GUARD, i32)
        return carry

    lax.fori_loop(nk, n_chunks * PACK, fill_guard, 0)

    def count_ge(cand):
        cvec = cand * ONES

        def chunk(ci, tot):
            accs = [None] * 4
            for tt in range(PACK):
                for r in range(TKP // 8):
                    slab = pk_ref[ci * PACK + tt, r * 8:(r + 1) * 8, :]
                    hit = lax.shift_right_logical(slab - cvec, 7) & ONES
                    n = (tt * (TKP // 8) + r) % 4
                    accs[n] = hit if accs[n] is None else accs[n] + hit
            a = (accs[0] + accs[1]) + (accs[2] + accs[3])
            return tot + ((a & 0xFF) + ((a >> 8) & 0xFF) + ((a >> 16) & 0xFF)
                          + lax.shift_right_logical(a, 24))

        tot = lax.fori_loop(0, n_chunks, chunk, jnp.zeros((8, TQ), i32))
        return jnp.sum(tot, axis=0, keepdims=True)

    def repack(lo, width, prefix):
        origin = prefix * (1 << width)

        def tile(j, carry):
            for r in range(TKP // 8):
                word = jnp.full((8, TQ), GUARD, i32)
                for q in range(PACK):
                    key = s_ref[j, q * TKP + r * 8:q * TKP + (r + 1) * 8, :]
                    d = (key >> lo) - origin
                    in_range = pltpu.bitcast(d, jnp.uint32) < jnp.uint32(1 << width)
                    word = word | (jnp.where(in_range, d, 0) << (8 * q))
                pk_ref[j, r * 8:(r + 1) * 8, :] = word
            return carry
        lax.fori_loop(0, nk, tile, 0)

    base = jnp.zeros((1, TQ), i32)
    bsize = jnp.full((1, TQ), nk * TK, i32)
    prefix = None
    zero = jnp.zeros((1, TQ), i32)
    for stage, (lo, width) in enumerate(DIGITS):
        need = top - base

        def search_digit(stage=stage, lo=lo, width=width, prefix=prefix, need=need, bsize=bsize):
            if stage > 0:
                repack(lo, width, prefix)

            def bit_step(i, st):
                v, c_v, c_rej = st
                cand = v | jnp.left_shift(jnp.int32(1), width - 1 - i)
                c = count_ge(cand)
                ok = c >= need
                return jnp.where(ok, cand, v), jnp.where(ok, c, c_v), jnp.where(ok, c_rej, c)

            return lax.fori_loop(0, width, bit_step, (zero, bsize, zero))

        v, c_v, c_gt = search_digit()
        base, bsize = base + c_gt, c_v - c_gt
        prefix = v - (1 << (width - 1)) if stage == 0 else prefix * (1 << width) + v
    t = prefix
    neg_key = np.array(NEG, np.float32).view(np.int32) ^ 0x7FFFFFFF
    any_excess = jnp.max(jnp.where(base + bsize > top, 1, 0)) > 0

    @pl.when(jnp.logical_not(any_excess))
    def _():
        thr = jnp.where(t == neg_key, t + 1, t)

        def fast(j, carry):
            s_ref[j] = pltpu.bitcast(jnp.where(s_ref[j] >= thr, 0.0, NEG), i32)
            return carry
        lax.fori_loop(0, nk, fast, 0)

    @pl.when(any_excess)
    def _():
        need = jnp.where(t == neg_key, 0, top - base).astype(f32)

        def ranked(j, taken):
            keys = s_ref[j]
            for c in range(NC):
                kc = keys[c * LANES:(c + 1) * LANES, :]
                eq = jnp.where(kc == t, 1.0, 0.0)
                rank = taken + jnp.dot(tri_ref[...], eq.astype(bf16), preferred_element_type=f32)
                pick = jnp.where(kc > t, 1.0, jnp.where(rank <= need, eq, 0.0))
                s_ref[j, c * LANES:(c + 1) * LANES, :] = pltpu.bitcast(
                    jnp.where(pick > 0.5, 0.0, NEG), i32)
                taken = taken + jnp.sum(eq, axis=0, keepdims=True)
            return taken
        lax.fori_loop(0, nk, ranked, jnp.zeros((1, TQ), f32))

    m_ref[...] = jnp.full(m_ref.shape, NEG, f32)
    acc_ref[...] = jnp.zeros(acc_ref.shape, f32)

    def tile_step(nxt, cur):
        if cur is not None:
            j, slot, tile_max = cur
            m_old = [m_ref[g] for g in range(KV_HEADS)]
            m_new = [jnp.maximum(m_old[g], tile_max[g]) for g in range(KV_HEADS)]
            for g in range(KV_HEADS):
                m_ref[g] = m_new[g]
        if nxt is not None:
            jn, slot_n = nxt
            jc = jnp.minimum(jn, nk - 1)
        out = []
        for g in range(KV_HEADS):
            cmax = jnp.full((8, R), NEG, f32)
            if nxt is not None:
                raw = jnp.dot(k_ref[0, jc], qpad_ref[g], preferred_element_type=f32)
            for c in range(NC):
                rows = slice(c * LANES, (c + 1) * LANES)
                if nxt is not None:
                    bias = pltpu.bitcast(s_ref[jc, rows, :], f32)
                    lm = raw[rows, :] + jnp.concatenate([bias] * GROUPS, axis=1)
                    lm_refs[slot_n][g][rows, :] = lm
                    cmax = jnp.maximum(cmax, jnp.max(lm.reshape(LANES // 8, 8, R), axis=0))
                if cur is not None:
                    p_refs[slot][g][rows, :] = jnp.exp2(lm_refs[slot][g][rows, :] - m_new[g]).astype(bf16)
            out.append(jnp.max(cmax, axis=0, keepdims=True))
            if cur is not None:
                acc_ref[g] = (jnp.exp2(m_old[g] - m_new[g]) * acc_ref[g]
                              + jnp.dot(vt_ref[0, j, g], p_refs[slot][g][...], preferred_element_type=f32))
        return tuple(out)

    def tile_pair(i, tile_max):
        nxt_max = tile_step((2 * i + 1, 1), (2 * i, 0, tile_max))
        return tile_step((2 * i + 2, 0), (2 * i + 1, 1, nxt_max))

    last_max = lax.fori_loop(0, nk // 2, tile_pair, tile_step((0, 0), None))

    @pl.when(nk % 2 == 1)
    def _():
        tile_step(None, (nk - 1, 0, last_max))

    for g in range(KV_HEADS):
        o_t = acc_ref[g, 0:HEAD_DIM, :] / acc_ref[g, HEAD_DIM:HEAD_DIM + 1, :]
        for i in range(0, GROUPS, 2):
            pair = jnp.concatenate([o_t[:, i * TQ:(i + 1) * TQ], o_t[:, (i + 1) * TQ:(i + 2) * TQ]], axis=0)
            h = g * GROUPS + i
            o_ref[0, :, h * HEAD_DIM:(h + 2) * HEAD_DIM] = pair.T.astype(bf16)


def _dsa(qt, qit, wit, kb, kib, vt, P, top):
    B, nq = qt.shape[:2]
    R = qt.shape[-1]
    TQ = R // GROUPS
    NT, TK = kb.shape[1:3]
    assert top <= TK and TQ % CHUNK == 0
    tri = jnp.asarray(np.tril(np.ones((LANES, LANES), np.float32)), bf16)
    return pl.pallas_call(
        functools.partial(_dsa_kernel, TQ=TQ, TK=TK, P=P, top=top),
        grid=(B, nq),
        in_specs=[pl.BlockSpec((1, 1, KV_HEADS, HEAD_DIM, R), lambda b, i: (b, i, 0, 0, 0)),
                  pl.BlockSpec((1, 1, IDX_DIM, IDX_HEADS * TQ), lambda b, i: (b, i, 0, 0)),
                  pl.BlockSpec((1, 1, WI_ROWS, TQ), lambda b, i: (b, i, 0, 0)),
                  pl.BlockSpec((1, NT, TK, IDX_DIM), lambda b, i: (b, 0, 0, 0)),
                  pl.BlockSpec((1, NT, TK, KV_HEADS * HEAD_DIM), lambda b, i: (b, 0, 0, 0)),
                  pl.BlockSpec((1, NT, KV_HEADS, VT_ROWS, TK), lambda b, i: (b, 0, 0, 0, 0)),
                  _const_spec((LANES, LANES))],
        out_specs=pl.BlockSpec((1, TQ, N_HEADS * HEAD_DIM), lambda b, i: (b, i, 0)),
        out_shape=jax.ShapeDtypeStruct((B, nq * TQ, N_HEADS * HEAD_DIM), bf16),
        scratch_shapes=[pltpu.VMEM((NT, TK, TQ), i32),
                        pltpu.VMEM((-(-NT // PACK) * PACK, TK // PACK, TQ), i32),
                        pltpu.VMEM((KV_HEADS, KV_HEADS * HEAD_DIM, R), bf16),
                        pltpu.VMEM((KV_HEADS, 1, R), f32),
                        pltpu.VMEM((KV_HEADS, VT_ROWS, R), f32)]
        + [pltpu.VMEM((TK, R), f32)] * (2 * KV_HEADS)
        + [pltpu.VMEM((TK, R), bf16)] * (2 * KV_HEADS),
        compiler_params=_params(("parallel", "arbitrary")),
        name="dsa",
    )(qt, qit, wit, kib, kb, vt, tri)


def _hgrn_mats(T):
    r = np.arange(T)[:, None]
    c = np.arange(T)[None, :]
    mats = [c <= r]
    b = T // 2
    while b >= 2:
        blk_r = r // (2 * b)
        mid = blk_r * 2 * b + b
        same = blk_r == c // (2 * b)
        late = r >= mid
        mats.append(same & ((late & (c >= mid) & (c <= r)) | (~late & (c > r) & (c < mid))))
        b //= 2
    return np.concatenate(mats, axis=0).astype(np.float32)


def _hgrn_kernel(x_ref, w_ref, lb_ref, g_ref, mats_ref, s0_ref, y_ref, st_out_ref, st_ref, *, T, SUBS):
    t_idx = pl.program_id(1)

    @pl.when(t_idx == 0)
    def _():
        st_ref[...] = s0_ref[0]

    lb = lb_ref[...]
    mats = mats_ref[...]
    row = lax.broadcasted_iota(i32, (T, 1), 0)
    col = lax.broadcasted_iota(i32, (1, T), 1)
    nt = (((1,), (1,)), ((), ()))
    tn = (((0,), (0,)), ((), ()))

    def prepare(i):
        xb = x_ref[0, i * T:(i + 1) * T, :].astype(bf16)
        proj = jnp.dot(xb, w_ref[...], preferred_element_type=f32)
        hq = proj[:, 0:HGRN_W]
        z = proj[:, HGRN_W:2 * HGRN_W]
        hv = proj[:, 2 * HGRN_W:3 * HGRN_W]
        hg = proj[:, 3 * HGRN_W:4 * HGRN_W]
        logf = (jnp.minimum(z, 0.0) - jnp.log1p(jnp.exp(-jnp.abs(z)))) + jnp.log1p(lb * jnp.exp(-z))
        kk = (1.0 - lb) * _sigmoid(-z)
        qq = hq * _sigmoid(hq)
        gate = hg * _sigmoid(hg)
        hi = logf.astype(bf16)
        r1 = logf - hi.astype(f32)
        mid = r1.astype(bf16)
        lo = (r1 - mid.astype(f32)).astype(bf16)
        e_all = jnp.dot(mats, jnp.concatenate([hi, mid, lo], axis=0),
                        preferred_element_type=f32)
        return qq, kk, hv, gate, logf, e_all

    def recur(i, prepared):
        qq, kk, hv, gate, logf, e_all = prepared
        for h in range(HGRN_HEADS):
            sl = slice(h * HGRN_DK, (h + 1) * HGRN_DK)
            q_h, k_h, v_h = qq[:, sl], kk[:, sl], hv[:, sl]
            vb = v_h.astype(bf16)
            cum = e_all[0:T, sl]
            suf = cum[T - 1:T, :] - cum
            st = st_ref[h]
            qd = (q_h * jnp.exp(cum)).astype(bf16)
            o = jnp.sum(q_h * k_h, axis=-1, keepdims=True) * v_h
            scores = jnp.zeros((T, T), f32)
            b = T // 2
            lvl = 1
            while b >= 1:
                late = ((row // b) % 2) == 1
                if b >= 2:
                    xdec = jnp.exp(e_all[lvl * T:(lvl + 1) * T, sl])
                else:
                    xdec = jnp.where(late, jnp.exp(logf[:, sl]), 1.0)
                a_m = jnp.where(late, q_h * xdec, 0.0).astype(bf16)
                b_m = jnp.where(late, 0.0, k_h * xdec).astype(bf16)
                sc = lax.dot_general(a_m, b_m, nt, preferred_element_type=f32)
                if 2 * b < T:
                    sc = jnp.where((row // (2 * b)) == (col // (2 * b)), sc, 0.0)
                scores = scores + sc
                b //= 2
                lvl += 1
            o = o + jnp.dot(jnp.concatenate([qd, scores.astype(bf16)], axis=1),
                            jnp.concatenate([st.T.astype(bf16), vb], axis=0), preferred_element_type=f32)
            kd = (k_h * jnp.exp(suf)).astype(bf16)
            st_ref[h] = (st * jnp.exp(cum[T - 1:T, :])
                         + lax.dot_general(vb, kd, tn, preferred_element_type=f32))
            o = o * lax.rsqrt(jnp.mean(o * o, axis=-1, keepdims=True) + LN_EPS) * g_ref[...]
            y_ref[0, i * T:(i + 1) * T, sl] = (o * gate[:, sl]).astype(bf16)

    nxt = prepare(0)
    for i in range(SUBS):
        cur = nxt
        if i + 1 < SUBS:
            nxt = prepare(i + 1)
        recur(i, cur)

    st_out_ref[0] = st_ref[...]


def _hgrn(x, w_h, lb, norm_g, s0):
    B, S, _ = x.shape
    T = _row_tile(S, 2 * CHUNK)
    SUBS = HGRN_SUBTILES if S % (T * HGRN_SUBTILES) == 0 else 1
    TS = T * SUBS
    mats = jnp.asarray(np.tile(_hgrn_mats(T), (1, 3)), bf16)
    nm = mats.shape[0]
    st0 = jnp.swapaxes(s0.astype(f32), 2, 3)
    y, st = pl.pallas_call(
        functools.partial(_hgrn_kernel, T=T, SUBS=SUBS),
        grid=(B, S // TS),
        in_specs=[pl.BlockSpec((1, TS, D_MODEL), lambda b, t: (b, t, 0)),
                  _const_spec((D_MODEL, 4 * HGRN_W)),
                  _const_spec((1, HGRN_W)),
                  _const_spec((1, HGRN_DV)),
                  _const_spec((nm, 3 * T)),
                  pl.BlockSpec((1, HGRN_HEADS, HGRN_DV, HGRN_DK), lambda b, t: (b, 0, 0, 0))],
        out_specs=[pl.BlockSpec((1, TS, HGRN_W), lambda b, t: (b, t, 0)),
                   pl.BlockSpec((1, HGRN_HEADS, HGRN_DV, HGRN_DK), lambda b, t: (b, 0, 0, 0))],
        out_shape=[jax.ShapeDtypeStruct((B, S, HGRN_W), bf16),
                   jax.ShapeDtypeStruct((B, HGRN_HEADS, HGRN_DV, HGRN_DK), f32)],
        scratch_shapes=[pltpu.VMEM((HGRN_HEADS, HGRN_DV, HGRN_DK), f32)],
        compiler_params=_params(("parallel", "arbitrary")),
        name="hgrn",
    )(x, w_h, lb, norm_g, mats, st0)
    return y, jnp.swapaxes(st, 2, 3)


HIST = CONV_W - 1
PADROWS = 8


def _conv_rows(buf_ref, cols, u, w, bias, T):
    buf_ref[PADROWS:PADROWS + T, cols] = u
    y = bias
    for j in range(CONV_W):
        y = y + w[j:j + 1, :] * buf_ref[PADROWS - HIST + j:PADROWS - HIST + j + T, cols]
    buf_ref[PADROWS - HIST:PADROWS, cols] = u[T - HIST:T, :]
    return y


def _merge_kernel(x_ref, ya_ref, yb_ref, hist_ref, wc_ref, wg_ref, wbr_ref, wout_ref, cw_ref, cb_ref,
                  g_ref, b_ref, o_ref, hist_out_ref, buf_ref, *, T, alpha):
    @pl.when(pl.program_id(1) == 0)
    def _():
        buf_ref[PADROWS - HIST:PADROWS, :] = hist_ref[0]

    x = x_ref[0]
    xb = x.astype(bf16)
    c = jnp.dot(xb, wc_ref[...], preferred_element_type=f32)
    W = SCONV_WIDTH
    u = c[:, W:2 * W] * c[:, 2 * W:3 * W]
    u_conv = _conv_rows(buf_ref, slice(0, W), u, cw_ref[...], cb_ref[...], T)
    yc = (c[:, 0:W] * u_conv).astype(bf16)
    hist_out_ref[0] = buf_ref[PADROWS - HIST:PADROWS, :]

    merged = None
    for i, yb in enumerate((ya_ref[0], yb_ref[0], yc)):
        gate = _sigmoid(jnp.dot(xb, wg_ref[:, i * D_MODEL:(i + 1) * D_MODEL], preferred_element_type=f32))
        term = gate * jnp.dot(yb, wbr_ref[i], preferred_element_type=f32)
        merged = term if merged is None else merged + term
    out = jnp.dot(merged.astype(bf16), wout_ref[...], preferred_element_type=f32)
    o_ref[0] = _layer_norm(alpha * x + out, g_ref[...], b_ref[...])


def _merge(x, ya, yb, hist, wc, wg, wbr, wout, cw, cb, g, b, alpha):
    B, S, _ = x.shape
    T = _row_tile(S, 256)
    row = lambda w: pl.BlockSpec((1, T, w), lambda bb, t: (bb, t, 0))
    hspec = pl.BlockSpec((1, HIST, SCONV_WIDTH), lambda bb, t: (bb, 0, 0))
    return pl.pallas_call(
        functools.partial(_merge_kernel, T=T, alpha=alpha),
        grid=(B, S // T),
        in_specs=[row(D_MODEL), row(BRANCH_WIDTH), row(BRANCH_WIDTH), hspec,
                  _const_spec((D_MODEL, 3 * SCONV_WIDTH)), _const_spec((D_MODEL, N_BRANCH * D_MODEL)),
                  _const_spec((N_BRANCH, BRANCH_WIDTH, D_MODEL)), _const_spec((D_MODEL, D_MODEL)),
                  _const_spec((CONV_W, SCONV_WIDTH)), _const_spec((1, SCONV_WIDTH)),
                  _const_spec((1, D_MODEL)), _const_spec((1, D_MODEL))],
        out_specs=[row(D_MODEL), hspec],
        out_shape=[jax.ShapeDtypeStruct((B, S, D_MODEL), f32),
                   jax.ShapeDtypeStruct((B, HIST, SCONV_WIDTH), f32)],
        scratch_shapes=[pltpu.VMEM((PADROWS + T, SCONV_WIDTH), f32)],
        compiler_params=_params(("parallel", "arbitrary")),
        name="merge",
    )(x, ya, yb, hist, wc, wg, wbr, wout, cw, cb, g, b)


def _ffn_kernel(x_ref, hist_ref, wup_ref, cw_ref, cb_ref, wdn_ref, g_ref, b_ref, o_ref, hist_out_ref,
                *bufs, T, alpha):
    n_steps = D_FF // FFN_COLS
    col_ranges = [slice(off + c * FFN_COLS, off + (c + 1) * FFN_COLS)
                  for c in range(n_steps) for off in (0, D_FF)]

    @pl.when(pl.program_id(1) == 0)
    def _():
        for buf, cols in zip(bufs, col_ranges):
            buf[PADROWS - HIST:PADROWS, :] = hist_ref[0, :, cols]

    x = x_ref[0]
    xb = x.astype(bf16)

    def up_conv(cidx):
        halves = []
        for half in range(2):
            buf, cols = bufs[2 * cidx + half], col_ranges[2 * cidx + half]
            h = jnp.dot(xb, wup_ref[:, cols], preferred_element_type=f32)
            halves.append(_conv_rows(buf, slice(None), h, cw_ref[:, cols], cb_ref[:, cols], T))
            hist_out_ref[0, :, cols] = buf[PADROWS - HIST:PADROWS, :]
        return halves

    groups = [list(range(c, min(c + FFN_GROUP, n_steps))) for c in range(0, n_steps, FFN_GROUP)]
    acc = jnp.zeros((T, D_MODEL), f32)
    nxt = [up_conv(c) for c in groups[0]]
    for gi, grp in enumerate(groups):
        cur = nxt
        if gi + 1 < len(groups):
            nxt = [up_conv(c) for c in groups[gi + 1]]
        act = jnp.concatenate([(a_g * _sigmoid(a_g) * b_v).astype(bf16) for a_g, b_v in cur], axis=1)
        acc = acc + jnp.dot(act, wdn_ref[grp[0] * FFN_COLS:(grp[-1] + 1) * FFN_COLS, :],
                            preferred_element_type=f32)
    o_ref[0] = _layer_norm(alpha * x + acc, g_ref[...], b_ref[...])


def _ffn(x, hist, wup, cw, cb, wdn, g, b, alpha):
    B, S, _ = x.shape
    T = _row_tile(S, 256)
    row = pl.BlockSpec((1, T, D_MODEL), lambda bb, t: (bb, t, 0))
    hspec = pl.BlockSpec((1, HIST, 2 * D_FF), lambda bb, t: (bb, 0, 0))
    return pl.pallas_call(
        functools.partial(_ffn_kernel, T=T, alpha=alpha),
        grid=(B, S // T),
        in_specs=[row, hspec, _const_spec((D_MODEL, 2 * D_FF)), _const_spec((CONV_W, 2 * D_FF)),
                  _const_spec((1, 2 * D_FF)), _const_spec((D_FF, D_MODEL)),
                  _const_spec((1, D_MODEL)), _const_spec((1, D_MODEL))],
        out_specs=[row, hspec],
        out_shape=[jax.ShapeDtypeStruct((B, S, D_MODEL), f32),
                   jax.ShapeDtypeStruct((B, HIST, 2 * D_FF), f32)],
        scratch_shapes=[pltpu.VMEM((PADROWS + T, FFN_COLS), f32)] * (2 * D_FF // FFN_COLS),
        compiler_params=_params(("parallel", "arbitrary")),
        name="ffn",
    )(x, hist, wup, cw, cb, wdn, g, b)


def _cast_kernel(x_ref, o_ref):
    o_ref[...] = x_ref[...].astype(o_ref.dtype)


def _to_bf16(w):
    D, Rw, C = w.shape
    T = _row_tile(Rw, LANES)
    spec = pl.BlockSpec((1, T, C), lambda d, t: (d, t, 0))
    return pl.pallas_call(_cast_kernel, grid=(D, Rw // T), in_specs=[spec], out_specs=spec,
                          out_shape=jax.ShapeDtypeStruct(w.shape, bf16),
                          compiler_params=_params(("parallel", "parallel")), name="to_bf16")(w)


def _layer_weights(l, depth, w_in, hgrn_lb_logits, hgrn_norm_g, sconv_w, sconv_b, w_branch, w_out,
                   ln1_g, ln1_b, w_up, ffn_conv_w, ffn_conv_b, w_down, ln2_g, ln2_b):
    offs = np.concatenate([[0], np.cumsum(IN_SIZES)]).tolist()
    sec = lambda i, j=None: w_in[l][:, offs[i]:offs[(i if j is None else j) + 1]]
    a_q, a_k, a_v, i_q, i_k, i_w = (sec(i) for i in range(6))
    zpad = lambda n: jnp.zeros((D_MODEL, n), w_in.dtype)
    w_row = jnp.concatenate([a_k, i_k, zpad(LANES - IDX_DIM), a_v], axis=1).astype(bf16)
    w_t = jnp.concatenate([a_q, i_q, a_v, i_w, zpad(WT_ROWS - 900)], axis=1).T.astype(bf16)
    lbp = jax.nn.softmax(hgrn_lb_logits.astype(f32), axis=0)
    lb = (jnp.cumsum(lbp, axis=0) - lbp[0])[l].reshape(1, HGRN_W)
    row = lambda a: a.reshape(1, -1).astype(f32)
    return dict(
        w_row=w_row, w_t=w_t, w_h=sec(6, 9).astype(bf16), lb=lb, norm_g=row(hgrn_norm_g[l]),
        w_c=sec(10, 12).astype(bf16), w_g=sec(13).astype(bf16),
        w_br=w_branch[l].astype(bf16), w_out=w_out[l].astype(bf16),
        sconv_w=sconv_w[l].astype(f32), sconv_b=row(sconv_b[l]),
        ln1_g=row(ln1_g[l]), ln1_b=row(ln1_b[l]),
        w_up=w_up[l].astype(bf16), ffn_w=ffn_conv_w[l].astype(f32), ffn_b=row(ffn_conv_b[l]),
        w_down=w_down[l].astype(bf16), ln2_g=row(ln2_g[l]), ln2_b=row(ln2_b[l]))


def _key_tiles(a, TK):
    B, L, C = a.shape
    NT = -(-L // TK)
    a = jnp.pad(a, ((0, 0), (0, NT * TK - L), (0, 0)))
    return a.reshape(B, NT, TK, C)


def _trunk_layer(x, tables, P, k_past, v_past, ki_past, s0, sc_hist, ffn_hist, w, alpha):
    B, S, _ = x.shape
    TK = KEY_TILE
    a = _attn_proj(x, w["w_row"], w["w_t"], *tables)
    k, v, ki = a["k"], a["v"], a["ki"]
    L = P + S
    if k_past is None and S % TK == 0:
        kb = a["kb"].reshape(B, S // TK, TK, -1)
        kib = a["kib"].reshape(B, S // TK, TK, -1)
        vt = a["vt"]
    else:
        kb, kib, v_all = a["kb"], a["kib"], v.astype(bf16)
        if k_past is not None:
            kb = jnp.concatenate([k_past.reshape(B, P, -1).astype(bf16), kb], axis=1)
            kib = jnp.concatenate([ki_past.astype(bf16), kib], axis=1)
            v_all = jnp.concatenate([v_past.reshape(B, P, -1).astype(bf16), v_all], axis=1)
        kb, kib = _key_tiles(kb, TK), _key_tiles(kib, TK)
        vt = _key_tiles(v_all, TK).reshape(B, -1, TK, KV_HEADS, HEAD_DIM).transpose(0, 1, 3, 4, 2)
        ones = jnp.zeros(vt.shape[:3] + (VT_ROWS - HEAD_DIM, TK), bf16).at[:, :, :, 0, :].set(1.0)
        vt = jnp.concatenate([vt, ones], axis=3)
    y_a = _dsa(a["qt"], a["qit"], a["wit"], kb, kib, vt, P, min(TOPK_MAX, L // 4))
    y_b, s_new = _hgrn(x, w["w_h"], w["lb"], w["norm_g"], s0)
    x1, sc_new = _merge(x, y_a, y_b, sc_hist, w["w_c"], w["w_g"], w["w_br"], w["w_out"],
                        w["sconv_w"], w["sconv_b"], w["ln1_g"], w["ln1_b"], alpha)
    x2, ffn_new = _ffn(x1, ffn_hist, w["w_up"], w["ffn_w"], w["ffn_b"], w["w_down"],
                       w["ln2_g"], w["ln2_b"], alpha)
    new = (k.reshape(B, S, KV_HEADS, HEAD_DIM), v.reshape(B, S, KV_HEADS, HEAD_DIM), ki,
           s_new, sc_new, ffn_new)
    return x2, new


def kernel(x_prompt, x_sample, cache_attn_k, cache_attn_v, cache_idx_k, state_hgrn, state_sconv,
           state_ffn_conv, w_in, hgrn_lb_logits, hgrn_norm_g, sconv_w, sconv_b, w_branch, w_out,
           ln1_g, ln1_b, w_up, ffn_conv_w, ffn_conv_b, w_down, ln2_g, ln2_b):
    depth = w_in.shape[0]
    alpha = (2 * depth) ** 0.25
    B, S, _ = x_prompt.shape
    DB, DS, _ = x_sample.shape
    P = cache_attn_k.shape[2]
    cs_p = _rotary_tables(jnp.arange(S))
    cs_s = _rotary_tables(P + jnp.arange(DS))
    xp, xs = x_prompt, x_sample
    st_p = [[] for _ in range(6)]
    st_s = [[] for _ in range(6)]
    w_in_b = _to_bf16(w_in)
    for l in range(depth):
        w = _layer_weights(l, depth, w_in_b, hgrn_lb_logits, hgrn_norm_g, sconv_w, sconv_b, w_branch,
                           w_out, ln1_g, ln1_b, w_up, ffn_conv_w, ffn_conv_b, w_down, ln2_g, ln2_b)
        xp, new_p = _trunk_layer(
            xp, cs_p, 0, None, None, None,
            jnp.zeros((B, HGRN_HEADS, HGRN_DK, HGRN_DV), f32),
            jnp.zeros((B, HIST, SCONV_WIDTH), f32),
            jnp.zeros((B, HIST, 2 * D_FF), f32), w, alpha)
        xs, new_s = _trunk_layer(
            xs, cs_s, P, cache_attn_k[l], cache_attn_v[l], cache_idx_k[l], state_hgrn[l],
            state_sconv[l], state_ffn_conv[l], w, alpha)
        for j in range(6):
            st_p[j].append(new_p[j])
            st_s[j].append(new_s[j])
    outs_p = [jnp.stack(a, axis=0) for a in st_p]
    outs_s = [jnp.stack(a, axis=0) for a in st_s]
    return (xp, xs, *outs_p, *outs_s)
```

```python
import functools

import numpy as np
import jax
import jax.numpy as jnp
from jax import lax
from jax.experimental import pallas as pl
from jax.experimental.pallas import tpu as pltpu

D_MODEL = 1024
CHUNK = 64
N_HEADS = 8
HEAD_DIM = 64
KV_HEADS = 2
GROUPS = N_HEADS // KV_HEADS
IDX_HEADS = 4
IDX_DIM = 64
TOPK_MAX = 256
ROPE_THETA = 500000.0
ATTN_SCALE = HEAD_DIM ** -0.5
IDX_SCALE = (IDX_DIM ** -0.5) * (IDX_HEADS ** -0.5)
NEG = -1e30
HGRN_HEADS = 4
HGRN_DK = 128
HGRN_DV = 128
HGRN_W = HGRN_HEADS * HGRN_DK
SCONV_WIDTH = 512
CONV_W = 3
BRANCH_WIDTH = 512
N_BRANCH = 3
D_FF = 2816
LN_EPS = 1e-5
IN_SIZES = (N_HEADS * HEAD_DIM, KV_HEADS * HEAD_DIM, KV_HEADS * HEAD_DIM,
            IDX_HEADS * IDX_DIM, IDX_DIM, IDX_HEADS,
            HGRN_W, HGRN_W, HGRN_W, HGRN_W,
            SCONV_WIDTH, SCONV_WIDTH, SCONV_WIDTH,
            N_BRANCH * D_MODEL)

LANES = 128
VMEM_LIMIT = 56 * 1024 * 1024
KEY_TILE = 512
QUERY_TILE = 256
SCORE_UNROLL = 4
VT_ROWS = HEAD_DIM + 16
FFN_COLS = 256
FFN_GROUP = 5
MERGE_SUBTILES = 2
HGRN_SUBTILES = 4
PACK = 4
DIGITS = ((25, 7), (18, 7), (11, 7), (4, 7), (0, 4))
GUARD = np.int32(-0x7F7F7F80)
ONES = np.int32(0x01010101)

f32 = jnp.float32
bf16 = jnp.bfloat16
i32 = jnp.int32


def _row_tile(s, want):
    return want if s % want == 0 else s


def _const_spec(shape):
    nd = len(shape)
    return pl.BlockSpec(shape, lambda *_: (0,) * nd, pipeline_mode=pl.Buffered(1))


def _params(sem):
    return pltpu.CompilerParams(dimension_semantics=sem, vmem_limit_bytes=VMEM_LIMIT)


def _layer_norm(z, g, b):
    mu = jnp.mean(z, axis=-1, keepdims=True)
    d = z - mu
    var = jnp.mean(d * d, axis=-1, keepdims=True)
    return d * lax.rsqrt(var + LN_EPS) * g + b


def _sigmoid(x):
    return 1.0 / (1.0 + jnp.exp(-x))


Q_PRESCALE = ATTN_SCALE * float(np.log2(np.e))
ROT = HEAD_DIM // 4
WR_COLS = 384
WT_ROWS = 912
WI_ROWS = 8


def _attn_proj_kernel(x_ref, wr_ref, wt_ref, cs_ref, cst_ref, k_ref, v_ref, ki_ref, kb_ref, kib_ref,
                      vt_ref, qt_ref, qit_ref, wit_ref, *, T, TQ):
    xb = x_ref[0].astype(bf16)
    pr = jnp.dot(xb, wr_ref[...], preferred_element_type=f32)
    c, sa, sb = cs_ref[0], cs_ref[1], cs_ref[2]

    def rot(xg):
        return xg * c + pltpu.roll(xg, LANES - ROT // 2, 1) * sa + pltpu.roll(xg, ROT // 2, 1) * sb

    k = rot(pr[:, 0:LANES])
    ki = rot(pr[:, LANES:2 * LANES])[:, :IDX_DIM]
    k_ref[0] = k
    kb_ref[0] = k.astype(bf16)
    ki_ref[0] = ki
    kib_ref[0] = ki.astype(bf16)
    v_ref[0] = pr[:, 2 * LANES:3 * LANES]

    nt = (((1,), (1,)), ((), ()))
    pt = lax.dot_general(wt_ref[...], xb, nt, preferred_element_type=f32)
    ct, st = cst_ref[0], cst_ref[1]

    def rot_t(hb):
        lead = hb[0:ROT] * ct + jnp.concatenate([hb[ROT // 2:ROT], hb[0:ROT // 2]], axis=0) * st
        return jnp.concatenate([lead, hb[ROT:HEAD_DIM]], axis=0)

    nqb = T // TQ
    for h in range(N_HEADS):
        hb = (rot_t(pt[h * HEAD_DIM:(h + 1) * HEAD_DIM]) * Q_PRESCALE).astype(bf16)
        g, i = divmod(h, GROUPS)
        for n in range(nqb):
            qt_ref[0, n, g, :, i * TQ:(i + 1) * TQ] = hb[:, n * TQ:(n + 1) * TQ]
    base = N_HEADS * HEAD_DIM
    for h in range(IDX_HEADS):
        hb = rot_t(pt[base + h * IDX_DIM:base + (h + 1) * IDX_DIM]).astype(bf16)
        for n in range(nqb):
            qit_ref[0, n, :, h * TQ:(h + 1) * TQ] = hb[:, n * TQ:(n + 1) * TQ]
    base += IDX_HEADS * IDX_DIM
    ones_rows = jnp.where(lax.broadcasted_iota(i32, (VT_ROWS - HEAD_DIM, T), 0) == 0, 1.0, 0.0)
    for g in range(KV_HEADS):
        vt_ref[0, 0, g, 0:HEAD_DIM, :] = pt[base + g * HEAD_DIM:base + (g + 1) * HEAD_DIM].astype(bf16)
        vt_ref[0, 0, g, HEAD_DIM:VT_ROWS, :] = ones_rows.astype(bf16)
    base += KV_HEADS * HEAD_DIM
    for n in range(nqb):
        wit_ref[0, n] = pt[base:base + WI_ROWS, n * TQ:(n + 1) * TQ]


def _attn_proj(x, w_row, w_t, cs, cst):
    B, S, _ = x.shape
    T = _row_tile(S, KEY_TILE)
    TQ = _row_tile(S, QUERY_TILE)
    nqb = T // TQ
    R = GROUPS * TQ
    row = lambda w: pl.BlockSpec((1, T, w), lambda b, t: (b, t, 0))
    names = ("k", "v", "ki", "kb", "kib", "vt", "qt", "qit", "wit")
    outs = pl.pallas_call(
        functools.partial(_attn_proj_kernel, T=T, TQ=TQ),
        grid=(B, S // T),
        in_specs=[row(D_MODEL), _const_spec((D_MODEL, WR_COLS)), _const_spec((WT_ROWS, D_MODEL)),
                  pl.BlockSpec((3, T, LANES), lambda b, t: (0, t, 0)),
                  pl.BlockSpec((2, ROT, T), lambda b, t: (0, 0, t))],
        out_specs=[row(128), row(128), row(IDX_DIM), row(128), row(IDX_DIM),
                   pl.BlockSpec((1, 1, KV_HEADS, VT_ROWS, T), lambda b, t: (b, t, 0, 0, 0)),
                   pl.BlockSpec((1, nqb, KV_HEADS, HEAD_DIM, R), lambda b, t: (b, t, 0, 0, 0)),
                   pl.BlockSpec((1, nqb, IDX_DIM, IDX_HEADS * TQ), lambda b, t: (b, t, 0, 0)),
                   pl.BlockSpec((1, nqb, WI_ROWS, TQ), lambda b, t: (b, t, 0, 0))],
        out_shape=[jax.ShapeDtypeStruct((B, S, 128), f32),
                   jax.ShapeDtypeStruct((B, S, 128), f32),
                   jax.ShapeDtypeStruct((B, S, IDX_DIM), f32),
                   jax.ShapeDtypeStruct((B, S, 128), bf16),
                   jax.ShapeDtypeStruct((B, S, IDX_DIM), bf16),
                   jax.ShapeDtypeStruct((B, S // T, KV_HEADS, VT_ROWS, T), bf16),
                   jax.ShapeDtypeStruct((B, S // TQ, KV_HEADS, HEAD_DIM, R), bf16),
                   jax.ShapeDtypeStruct((B, S // TQ, IDX_DIM, IDX_HEADS * TQ), bf16),
                   jax.ShapeDtypeStruct((B, S // TQ, WI_ROWS, TQ), f32)],
        compiler_params=_params(("parallel", "parallel")),
        name="attn_proj",
    )(x, w_row, w_t, cs, cst)
    return dict(zip(names, outs))


def _rotary_tables(pos):
    half = ROT // 2
    inv_freq = ROPE_THETA ** (-jnp.arange(half, dtype=f32) / half)
    ang = pos.astype(f32)[:, None] * inv_freq[None, :]
    cos = jnp.cos(ang)
    sin = jnp.sin(ang)
    S = pos.shape[0]
    one = jnp.ones((S, HEAD_DIM - ROT), f32)
    zero = jnp.zeros((S, HEAD_DIM - ROT), f32)
    zh = jnp.zeros((S, half), f32)
    c = jnp.concatenate([cos, cos, one], axis=1)
    sa = jnp.concatenate([-sin, zh, zero], axis=1)
    sb = jnp.concatenate([zh, sin, zero], axis=1)
    tile2 = lambda a: jnp.concatenate([a, a], axis=1)
    cs = jnp.stack([tile2(c), tile2(sa), tile2(sb)], axis=0)
    cst = jnp.stack([jnp.concatenate([cos, cos], axis=1).T,
                     jnp.concatenate([-sin, sin], axis=1).T], axis=0)
    return cs, cst


def _dsa_kernel(qt_ref, qit_ref, wit_ref, ki_ref, k_ref, vt_ref, tri_ref, o_ref,
                s_ref, pk_ref, qpad_ref, m_ref, acc_ref, *bufs, TQ, TK, P, top):
    lm_refs = (bufs[0:KV_HEADS], bufs[KV_HEADS:2 * KV_HEADS])
    p_refs = (bufs[2 * KV_HEADS:3 * KV_HEADS], bufs[3 * KV_HEADS:4 * KV_HEADS])
    qb = pl.program_id(1)
    q0 = P + qb * TQ
    nk = (q0 + TQ + TK - 1) // TK
    n_full = q0 // TK
    NC = TK // LANES
    R = GROUPS * TQ
    q_lim = ((q0 + lax.broadcasted_iota(i32, (1, TQ), 1)) // CHUNK + 1) * CHUNK
    wit = wit_ref[0, 0] * IDX_SCALE
    zero_head = jnp.zeros((HEAD_DIM, R), bf16)
    qpad_ref[0] = jnp.concatenate([qt_ref[0, 0, 0], zero_head], axis=0)
    qpad_ref[1] = jnp.concatenate([zero_head, qt_ref[0, 0, 1]], axis=0)

    TKP = TK // PACK

    def lead_digit(key):
        return (key >> DIGITS[0][0]) + (1 << (DIGITS[0][1] - 1))

    def score_tile(masked, j, carry):
        for c in range(NC):
            rows = slice(c * LANES, (c + 1) * LANES)
            rel = jnp.maximum(jnp.dot(ki_ref[0, j, rows, :], qit_ref[0, 0], preferred_element_type=f32), 0.0)
            score = wit[0:1, :] * rel[:, 0:TQ]
            for h in range(1, IDX_HEADS):
                score = score + wit[h:h + 1, :] * rel[:, h * TQ:(h + 1) * TQ]
            if masked:
                k_pos = j * TK + c * LANES + lax.broadcasted_iota(i32, (LANES, TQ), 0)
                score = jnp.where(k_pos < q_lim, score, NEG)
            bits = pltpu.bitcast(score, i32)
            key = jnp.where(bits >= 0, bits, bits ^ 0x7FFFFFFF)
            s_ref[j, rows, :] = key
            if c == 0:
                pk_ref[j] = lead_digit(key) | GUARD
            else:
                pk_ref[j] = pk_ref[j] | (lead_digit(key) << (8 * c))
        return carry

    def score_group(i, carry):
        for u in range(SCORE_UNROLL):
            carry = score_tile(False, SCORE_UNROLL * i + u, carry)
        return carry

    lax.fori_loop(0, n_full // SCORE_UNROLL, score_group, 0)
    lax.fori_loop(n_full // SCORE_UNROLL * SCORE_UNROLL, n_full, functools.partial(score_tile, False), 0)
    lax.fori_loop(n_full, nk, functools.partial(score_tile, True), 0)

    n_chunks = (nk + PACK - 1) // PACK

    def fill_guard(j, carry):
        pk_ref[j] = jnp.full((TKP, TQ), GUARD, i32)
        return carry

    lax.fori_loop(nk, n_chunks * PACK, fill_guard, 0)

    def count_ge(cand):
        cvec = cand * ONES

        def chunk(ci, tot):
            accs = [None] * 4
            for tt in range(PACK):
                for r in range(TKP // 8):
                    slab = pk_ref[ci * PACK + tt, r * 8:(r + 1) * 8, :]
                    hit = lax.shift_right_logical(slab - cvec, 7) & ONES
                    n = (tt * (TKP // 8) + r) % 4
                    accs[n] = hit if accs[n] is None else accs[n] + hit
            a = (accs[0] + accs[1]) + (accs[2] + accs[3])
            return tot + ((a & 0xFF) + ((a >> 8) & 0xFF) + ((a >> 16) & 0xFF)
                          + lax.shift_right_logical(a, 24))

        tot = lax.fori_loop(0, n_chunks, chunk, jnp.zeros((8, TQ), i32))
        return jnp.sum(tot, axis=0, keepdims=True)

    def repack(lo, width, prefix):
        origin = prefix * (1 << width)

        def tile(j, carry):
            for r in range(TKP // 8):
                word = jnp.full((8, TQ), GUARD, i32)
                for q in range(PACK):
                    key = s_ref[j, q * TKP + r * 8:q * TKP + (r + 1) * 8, :]
                    d = (key >> lo) - origin
                    in_range = pltpu.bitcast(d, jnp.uint32) < jnp.uint32(1 << width)
                    word = word | (jnp.where(in_range, d, 0) << (8 * q))
                pk_ref[j, r * 8:(r + 1) * 8, :] = word
            return carry
        lax.fori_loop(0, nk, tile, 0)

    base = jnp.zeros((1, TQ), i32)
    bsize = jnp.full((1, TQ), nk * TK, i32)
    prefix = None
    zero = jnp.zeros((1, TQ), i32)
    for stage, (lo, width) in enumerate(DIGITS):
        need = top - base

        def search_digit(stage=stage, lo=lo, width=width, prefix=prefix, need=need, bsize=bsize):
            if stage > 0:
                repack(lo, width, prefix)

            def bit_step(i, st):
                v, c_v, c_rej = st
                cand = v | jnp.left_shift(jnp.int32(1), width - 1 - i)
                c = count_ge(cand)
                ok = c >= need
                return jnp.where(ok, cand, v), jnp.where(ok, c, c_v), jnp.where(ok, c_rej, c)

            return lax.fori_loop(0, width, bit_step, (zero, bsize, zero))

        v, c_v, c_gt = search_digit()
        base, bsize = base + c_gt, c_v - c_gt
        prefix = v - (1 << (width - 1)) if stage == 0 else prefix * (1 << width) + v
    t = prefix
    neg_key = np.array(NEG, np.float32).view(np.int32) ^ 0x7FFFFFFF
    any_excess = jnp.max(jnp.where(base + bsize > top, 1, 0)) > 0

    @pl.when(jnp.logical_not(any_excess))
    def _():
        thr = jnp.where(t == neg_key, t + 1, t)

        def fast(j, carry):
            s_ref[j] = pltpu.bitcast(jnp.where(s_ref[j] >= thr, 0.0, NEG), i32)
            return carry
        lax.fori_loop(0, nk, fast, 0)

    @pl.when(any_excess)
    def _():
        need = jnp.where(t == neg_key, 0, top - base).astype(f32)

        def ranked(j, taken):
            keys = s_ref[j]
            for c in range(NC):
                kc = keys[c * LANES:(c + 1) * LANES, :]
                eq = jnp.where(kc == t, 1.0, 0.0)
                rank = taken + jnp.dot(tri_ref[...], eq.astype(bf16), preferred_element_type=f32)
                pick = jnp.where(kc > t, 1.0, jnp.where(rank <= need, eq, 0.0))
                s_ref[j, c * LANES:(c + 1) * LANES, :] = pltpu.bitcast(
                    jnp.where(pick > 0.5, 0.0, NEG), i32)
                taken = taken + jnp.sum(eq, axis=0, keepdims=True)
            return taken
        lax.fori_loop(0, nk, ranked, jnp.zeros((1, TQ), f32))

    m_ref[...] = jnp.full(m_ref.shape, NEG, f32)
    acc_ref[...] = jnp.zeros(acc_ref.shape, f32)

    def tile_step(nxt, cur):
        if cur is not None:
            j, slot, tile_max = cur
            m_old = [m_ref[g] for g in range(KV_HEADS)]
            m_new = [jnp.maximum(m_old[g], tile_max[g]) for g in range(KV_HEADS)]
            for g in range(KV_HEADS):
                m_ref[g] = m_new[g]
        if nxt is not None:
            jn, slot_n = nxt
            jc = jnp.minimum(jn, nk - 1)
        out = []
        for g in range(KV_HEADS):
            cmax = jnp.full((8, R), NEG, f32)
            if nxt is not None:
                raw = jnp.dot(k_ref[0, jc], qpad_ref[g], preferred_element_type=f32)
            for c in range(NC):
                rows = slice(c * LANES, (c + 1) * LANES)
                if nxt is not None:
                    bias = pltpu.bitcast(s_ref[jc, rows, :], f32)
                    lm = raw[rows, :] + jnp.concatenate([bias] * GROUPS, axis=1)
                    lm_refs[slot_n][g][rows, :] = lm
                    cmax = jnp.maximum(cmax, jnp.max(lm.reshape(LANES // 8, 8, R), axis=0))
                if cur is not None:
                    p_refs[slot][g][rows, :] = jnp.exp2(lm_refs[slot][g][rows, :] - m_new[g]).astype(bf16)
            out.append(jnp.max(cmax, axis=0, keepdims=True))
            if cur is not None:
                acc_ref[g] = (jnp.exp2(m_old[g] - m_new[g]) * acc_ref[g]
                              + jnp.dot(vt_ref[0, j, g], p_refs[slot][g][...], preferred_element_type=f32))
        return tuple(out)

    def tile_pair(i, tile_max):
        nxt_max = tile_step((2 * i + 1, 1), (2 * i, 0, tile_max))
        return tile_step((2 * i + 2, 0), (2 * i + 1, 1, nxt_max))

    last_max = lax.fori_loop(0, nk // 2, tile_pair, tile_step((0, 0), None))

    @pl.when(nk % 2 == 1)
    def _():
        tile_step(None, (nk - 1, 0, last_max))

    for g in range(KV_HEADS):
        o_t = acc_ref[g, 0:HEAD_DIM, :] / acc_ref[g, HEAD_DIM:HEAD_DIM + 1, :]
        for i in range(0, GROUPS, 2):
            pair = jnp.concatenate([o_t[:, i * TQ:(i + 1) * TQ], o_t[:, (i + 1) * TQ:(i + 2) * TQ]], axis=0)
            h = g * GROUPS + i
            o_ref[0, :, h * HEAD_DIM:(h + 2) * HEAD_DIM] = pair.T.astype(bf16)


def _dsa(qt, qit, wit, kb, kib, vt, P, top):
    B, nq = qt.shape[:2]
    R = qt.shape[-1]
    TQ = R // GROUPS
    NT, TK = kb.shape[1:3]
    assert top <= TK and TQ % CHUNK == 0
    tri = jnp.asarray(np.tril(np.ones((LANES, LANES), np.float32)), bf16)
    return pl.pallas_call(
        functools.partial(_dsa_kernel, TQ=TQ, TK=TK, P=P, top=top),
        grid=(B, nq),
        in_specs=[pl.BlockSpec((1, 1, KV_HEADS, HEAD_DIM, R), lambda b, i: (b, i, 0, 0, 0)),
                  pl.BlockSpec((1, 1, IDX_DIM, IDX_HEADS * TQ), lambda b, i: (b, i, 0, 0)),
                  pl.BlockSpec((1, 1, WI_ROWS, TQ), lambda b, i: (b, i, 0, 0)),
                  pl.BlockSpec((1, NT, TK, IDX_DIM), lambda b, i: (b, 0, 0, 0)),
                  pl.BlockSpec((1, NT, TK, KV_HEADS * HEAD_DIM), lambda b, i: (b, 0, 0, 0)),
                  pl.BlockSpec((1, NT, KV_HEADS, VT_ROWS, TK), lambda b, i: (b, 0, 0, 0, 0)),
                  _const_spec((LANES, LANES))],
        out_specs=pl.BlockSpec((1, TQ, N_HEADS * HEAD_DIM), lambda b, i: (b, i, 0)),
        out_shape=jax.ShapeDtypeStruct((B, nq * TQ, N_HEADS * HEAD_DIM), bf16),
        scratch_shapes=[pltpu.VMEM((NT, TK, TQ), i32),
                        pltpu.VMEM((-(-NT // PACK) * PACK, TK // PACK, TQ), i32),
                        pltpu.VMEM((KV_HEADS, KV_HEADS * HEAD_DIM, R), bf16),
                        pltpu.VMEM((KV_HEADS, 1, R), f32),
                        pltpu.VMEM((KV_HEADS, VT_ROWS, R), f32)]
        + [pltpu.VMEM((TK, R), f32)] * (2 * KV_HEADS)
        + [pltpu.VMEM((TK, R), bf16)] * (2 * KV_HEADS),
        compiler_params=_params(("parallel", "arbitrary")),
        name="dsa",
    )(qt, qit, wit, kib, kb, vt, tri)


def _hgrn_mats(T):
    r = np.arange(T)[:, None]
    c = np.arange(T)[None, :]
    mats = [c <= r]
    b = T // 2
    while b >= 2:
        blk_r = r // (2 * b)
        mid = blk_r * 2 * b + b
        same = blk_r == c // (2 * b)
        late = r >= mid
        mats.append(same & ((late & (c >= mid) & (c <= r)) | (~late & (c > r) & (c < mid))))
        b //= 2
    return np.concatenate(mats, axis=0).astype(np.float32)


def _hgrn_kernel(x_ref, w_ref, lb_ref, g_ref, mats_ref, s0_ref, y_ref, st_out_ref, st_ref, *, T, SUBS):
    t_idx = pl.program_id(1)

    @pl.when(t_idx == 0)
    def _():
        st_ref[...] = s0_ref[0]

    lb = lb_ref[...]
    mats = mats_ref[...]
    row = lax.broadcasted_iota(i32, (T, 1), 0)
    col = lax.broadcasted_iota(i32, (1, T), 1)
    nt = (((1,), (1,)), ((), ()))
    tn = (((0,), (0,)), ((), ()))

    def prepare(i):
        xb = x_ref[0, i * T:(i + 1) * T, :].astype(bf16)
        proj = jnp.dot(xb, w_ref[...], preferred_element_type=f32)
        hq = proj[:, 0:HGRN_W]
        z = proj[:, HGRN_W:2 * HGRN_W]
        hv = proj[:, 2 * HGRN_W:3 * HGRN_W]
        hg = proj[:, 3 * HGRN_W:4 * HGRN_W]
        logf = (jnp.minimum(z, 0.0) - jnp.log1p(jnp.exp(-jnp.abs(z)))) + jnp.log1p(lb * jnp.exp(-z))
        kk = (1.0 - lb) * _sigmoid(-z)
        qq = hq * _sigmoid(hq)
        gate = hg * _sigmoid(hg)
        hi = logf.astype(bf16)
        r1 = logf - hi.astype(f32)
        mid = r1.astype(bf16)
        lo = (r1 - mid.astype(f32)).astype(bf16)
        e_all = jnp.dot(mats, jnp.concatenate([hi, mid, lo], axis=0),
                        preferred_element_type=f32)
        return qq, kk, hv, gate, logf, e_all

    def recur(i, prepared):
        qq, kk, hv, gate, logf, e_all = prepared
        for h in range(HGRN_HEADS):
            sl = slice(h * HGRN_DK, (h + 1) * HGRN_DK)
            q_h, k_h, v_h = qq[:, sl], kk[:, sl], hv[:, sl]
            vb = v_h.astype(bf16)
            cum = e_all[0:T, sl]
            suf = cum[T - 1:T, :] - cum
            st = st_ref[h]
            qd = (q_h * jnp.exp(cum)).astype(bf16)
            o = jnp.sum(q_h * k_h, axis=-1, keepdims=True) * v_h
            scores = jnp.zeros((T, T), f32)
            b = T // 2
            lvl = 1
            while b >= 1:
                late = ((row // b) % 2) == 1
                if b >= 2:
                    xdec = jnp.exp(e_all[lvl * T:(lvl + 1) * T, sl])
                else:
                    xdec = jnp.where(late, jnp.exp(logf[:, sl]), 1.0)
                a_m = jnp.where(late, q_h * xdec, 0.0).astype(bf16)
                b_m = jnp.where(late, 0.0, k_h * xdec).astype(bf16)
                sc = lax.dot_general(a_m, b_m, nt, preferred_element_type=f32)
                if 2 * b < T:
                    sc = jnp.where((row // (2 * b)) == (col // (2 * b)), sc, 0.0)
                scores = scores + sc
                b //= 2
                lvl += 1
            o = o + jnp.dot(jnp.concatenate([qd, scores.astype(bf16)], axis=1),
                            jnp.concatenate([st.T.astype(bf16), vb], axis=0), preferred_element_type=f32)
            kd = (k_h * jnp.exp(suf)).astype(bf16)
            st_ref[h] = (st * jnp.exp(cum[T - 1:T, :])
                         + lax.dot_general(vb, kd, tn, preferred_element_type=f32))
            o = o * lax.rsqrt(jnp.mean(o * o, axis=-1, keepdims=True) + LN_EPS) * g_ref[...]
            y_ref[0, i * T:(i + 1) * T, sl] = (o * gate[:, sl]).astype(bf16)

    nxt = prepare(0)
    for i in range(SUBS):
        cur = nxt
        if i + 1 < SUBS:
            nxt = prepare(i + 1)
        recur(i, cur)

    st_out_ref[0] = st_ref[...]


def _hgrn(x, w_h, lb, norm_g, s0):
    B, S, _ = x.shape
    T = _row_tile(S, 2 * CHUNK)
    SUBS = HGRN_SUBTILES if S % (T * HGRN_SUBTILES) == 0 else 1
    TS = T * SUBS
    mats = jnp.asarray(np.tile(_hgrn_mats(T), (1, 3)), bf16)
    nm = mats.shape[0]
    st0 = jnp.swapaxes(s0.astype(f32), 2, 3)
    y, st = pl.pallas_call(
        functools.partial(_hgrn_kernel, T=T, SUBS=SUBS),
        grid=(B, S // TS),
        in_specs=[pl.BlockSpec((1, TS, D_MODEL), lambda b, t: (b, t, 0)),
                  _const_spec((D_MODEL, 4 * HGRN_W)),
                  _const_spec((1, HGRN_W)),
                  _const_spec((1, HGRN_DV)),
                  _const_spec((nm, 3 * T)),
                  pl.BlockSpec((1, HGRN_HEADS, HGRN_DV, HGRN_DK), lambda b, t: (b, 0, 0, 0))],
        out_specs=[pl.BlockSpec((1, TS, HGRN_W), lambda b, t: (b, t, 0)),
                   pl.BlockSpec((1, HGRN_HEADS, HGRN_DV, HGRN_DK), lambda b, t: (b, 0, 0, 0))],
        out_shape=[jax.ShapeDtypeStruct((B, S, HGRN_W), bf16),
                   jax.ShapeDtypeStruct((B, HGRN_HEADS, HGRN_DV, HGRN_DK), f32)],
        scratch_shapes=[pltpu.VMEM((HGRN_HEADS, HGRN_DV, HGRN_DK), f32)],
        compiler_params=_params(("parallel", "arbitrary")),
        name="hgrn",
    )(x, w_h, lb, norm_g, mats, st0)
    return y, jnp.swapaxes(st, 2, 3)


HIST = CONV_W - 1
PADROWS = 8


def _conv_rows(buf_ref, cols, u, w, bias, T):
    buf_ref[PADROWS:PADROWS + T, cols] = u
    y = bias
    for j in range(CONV_W):
        y = y + w[j:j + 1, :] * buf_ref[PADROWS - HIST + j:PADROWS - HIST + j + T, cols]
    buf_ref[PADROWS - HIST:PADROWS, cols] = u[T - HIST:T, :]
    return y


def _merge_kernel(x_ref, ya_ref, yb_ref, hist_ref, wc_ref, wg_ref, wbr_ref, wout_ref, cw_ref, cb_ref,
                  g_ref, b_ref, o_ref, hist_out_ref, buf_ref, *, T, SUBS, alpha):
    @pl.when(pl.program_id(1) == 0)
    def _():
        buf_ref[PADROWS - HIST:PADROWS, :] = hist_ref[0]

    for i in range(SUBS):
        rows = slice(i * T, (i + 1) * T)
        x = x_ref[0, rows, :]
        xb = x.astype(bf16)
        c = jnp.dot(xb, wc_ref[...], preferred_element_type=f32)
        W = SCONV_WIDTH
        u = c[:, W:2 * W] * c[:, 2 * W:3 * W]
        u_conv = _conv_rows(buf_ref, slice(0, W), u, cw_ref[...], cb_ref[...], T)
        yc = (c[:, 0:W] * u_conv).astype(bf16)
        merged = None
        for n, yb in enumerate((ya_ref[0, rows, :], yb_ref[0, rows, :], yc)):
            gate = _sigmoid(jnp.dot(xb, wg_ref[:, n * D_MODEL:(n + 1) * D_MODEL], preferred_element_type=f32))
            term = gate * jnp.dot(yb, wbr_ref[n], preferred_element_type=f32)
            merged = term if merged is None else merged + term
        out = jnp.dot(merged.astype(bf16), wout_ref[...], preferred_element_type=f32)
        o_ref[0, rows, :] = _layer_norm(alpha * x + out, g_ref[...], b_ref[...])
    hist_out_ref[0] = buf_ref[PADROWS - HIST:PADROWS, :]


def _merge(x, ya, yb, hist, wc, wg, wbr, wout, cw, cb, g, b, alpha):
    B, S, _ = x.shape
    T = _row_tile(S, 256)
    SUBS = MERGE_SUBTILES if S % (T * MERGE_SUBTILES) == 0 else 1
    row = lambda w: pl.BlockSpec((1, T * SUBS, w), lambda bb, t: (bb, t, 0))
    hspec = pl.BlockSpec((1, HIST, SCONV_WIDTH), lambda bb, t: (bb, 0, 0))
    return pl.pallas_call(
        functools.partial(_merge_kernel, T=T, SUBS=SUBS, alpha=alpha),
        grid=(B, S // (T * SUBS)),
        in_specs=[row(D_MODEL), row(BRANCH_WIDTH), row(BRANCH_WIDTH), hspec,
                  _const_spec((D_MODEL, 3 * SCONV_WIDTH)), _const_spec((D_MODEL, N_BRANCH * D_MODEL)),
                  _const_spec((N_BRANCH, BRANCH_WIDTH, D_MODEL)), _const_spec((D_MODEL, D_MODEL)),
                  _const_spec((CONV_W, SCONV_WIDTH)), _const_spec((1, SCONV_WIDTH)),
                  _const_spec((1, D_MODEL)), _const_spec((1, D_MODEL))],
        out_specs=[row(D_MODEL), hspec],
        out_shape=[jax.ShapeDtypeStruct((B, S, D_MODEL), f32),
                   jax.ShapeDtypeStruct((B, HIST, SCONV_WIDTH), f32)],
        scratch_shapes=[pltpu.VMEM((PADROWS + T, SCONV_WIDTH), f32)],
        compiler_params=_params(("parallel", "arbitrary")),
        name="merge",
    )(x, ya, yb, hist, wc, wg, wbr, wout, cw, cb, g, b)


def _ffn_kernel(x_ref, hist_ref, wup_ref, cw_ref, cb_ref, wdn_ref, g_ref, b_ref, o_ref, hist_out_ref,
                *bufs, T, alpha):
    n_steps = D_FF // FFN_COLS
    col_ranges = [slice(off + c * FFN_COLS, off + (c + 1) * FFN_COLS)
                  for c in range(n_steps) for off in (0, D_FF)]

    @pl.when(pl.program_id(1) == 0)
    def _():
        for buf, cols in zip(bufs, col_ranges):
            buf[PADROWS - HIST:PADROWS, :] = hist_ref[0, :, cols]

    x = x_ref[0]
    xb = x.astype(bf16)

    def up_conv(cidx):
        halves = []
        for half in range(2):
            buf, cols = bufs[2 * cidx + half], col_ranges[2 * cidx + half]
            h = jnp.dot(xb, wup_ref[:, cols], preferred_element_type=f32)
            halves.append(_conv_rows(buf, slice(None), h, cw_ref[:, cols], cb_ref[:, cols], T))
            hist_out_ref[0, :, cols] = buf[PADROWS - HIST:PADROWS, :]
        return halves

    groups = [list(range(c, min(c + FFN_GROUP, n_steps))) for c in range(0, n_steps, FFN_GROUP)]
    acc = jnp.zeros((T, D_MODEL), f32)
    nxt = [up_conv(c) for c in groups[0]]
    for gi, grp in enumerate(groups):
        cur = nxt
        if gi + 1 < len(groups):
            nxt = [up_conv(c) for c in groups[gi + 1]]
        act = jnp.concatenate([(a_g * _sigmoid(a_g) * b_v).astype(bf16) for a_g, b_v in cur], axis=1)
        acc = acc + jnp.dot(act, wdn_ref[grp[0] * FFN_COLS:(grp[-1] + 1) * FFN_COLS, :],
                            preferred_element_type=f32)
    o_ref[0] = _layer_norm(alpha * x + acc, g_ref[...], b_ref[...])


def _ffn(x, hist, wup, cw, cb, wdn, g, b, alpha):
    B, S, _ = x.shape
    T = _row_tile(S, 256)
    row = pl.BlockSpec((1, T, D_MODEL), lambda bb, t: (bb, t, 0))
    hspec = pl.BlockSpec((1, HIST, 2 * D_FF), lambda bb, t: (bb, 0, 0))
    return pl.pallas_call(
        functools.partial(_ffn_kernel, T=T, alpha=alpha),
        grid=(B, S // T),
        in_specs=[row, hspec, _const_spec((D_MODEL, 2 * D_FF)), _const_spec((CONV_W, 2 * D_FF)),
                  _const_spec((1, 2 * D_FF)), _const_spec((D_FF, D_MODEL)),
                  _const_spec((1, D_MODEL)), _const_spec((1, D_MODEL))],
        out_specs=[row, hspec],
        out_shape=[jax.ShapeDtypeStruct((B, S, D_MODEL), f32),
                   jax.ShapeDtypeStruct((B, HIST, 2 * D_FF), f32)],
        scratch_shapes=[pltpu.VMEM((PADROWS + T, FFN_COLS), f32)] * (2 * D_FF // FFN_COLS),
        compiler_params=_params(("parallel", "arbitrary")),
        name="ffn",
    )(x, hist, wup, cw, cb, wdn, g, b)


def _cast_kernel(x_ref, o_ref):
    o_ref[...] = x_ref[...].astype(o_ref.dtype)


def _to_bf16(w):
    D, Rw, C = w.shape
    T = _row_tile(Rw, LANES)
    spec = pl.BlockSpec((1, T, C), lambda d, t: (d, t, 0))
    return pl.pallas_call(_cast_kernel, grid=(D, Rw // T), in_specs=[spec], out_specs=spec,
                          out_shape=jax.ShapeDtypeStruct(w.shape, bf16),
                          compiler_params=_params(("parallel", "parallel")), name="to_bf16")(w)


def _layer_weights(l, depth, w_in, hgrn_lb_logits, hgrn_norm_g, sconv_w, sconv_b, w_branch, w_out,
                   ln1_g, ln1_b, w_up, ffn_conv_w, ffn_conv_b, w_down, ln2_g, ln2_b):
    offs = np.concatenate([[0], np.cumsum(IN_SIZES)]).tolist()
    sec = lambda i, j=None: w_in[l][:, offs[i]:offs[(i if j is None else j) + 1]]
    a_q, a_k, a_v, i_q, i_k, i_w = (sec(i) for i in range(6))
    zpad = lambda n: jnp.zeros((D_MODEL, n), w_in.dtype)
    w_row = jnp.concatenate([a_k, i_k, zpad(LANES - IDX_DIM), a_v], axis=1).astype(bf16)
    w_t = jnp.concatenate([a_q, i_q, a_v, i_w, zpad(WT_ROWS - 900)], axis=1).T.astype(bf16)
    lbp = jax.nn.softmax(hgrn_lb_logits.astype(f32), axis=0)
    lb = (jnp.cumsum(lbp, axis=0) - lbp[0])[l].reshape(1, HGRN_W)
    row = lambda a: a.reshape(1, -1).astype(f32)
    return dict(
        w_row=w_row, w_t=w_t, w_h=sec(6, 9).astype(bf16), lb=lb, norm_g=row(hgrn_norm_g[l]),
        w_c=sec(10, 12).astype(bf16), w_g=sec(13).astype(bf16),
        w_br=w_branch[l].astype(bf16), w_out=w_out[l].astype(bf16),
        sconv_w=sconv_w[l].astype(f32), sconv_b=row(sconv_b[l]),
        ln1_g=row(ln1_g[l]), ln1_b=row(ln1_b[l]),
        w_up=w_up[l].astype(bf16), ffn_w=ffn_conv_w[l].astype(f32), ffn_b=row(ffn_conv_b[l]),
        w_down=w_down[l].astype(bf16), ln2_g=row(ln2_g[l]), ln2_b=row(ln2_b[l]))


def _key_tiles(a, TK):
    B, L, C = a.shape
    NT = -(-L // TK)
    a = jnp.pad(a, ((0, 0), (0, NT * TK - L), (0, 0)))
    return a.reshape(B, NT, TK, C)


def _trunk_layer(x, tables, P, k_past, v_past, ki_past, s0, sc_hist, ffn_hist, w, alpha):
    B, S, _ = x.shape
    TK = KEY_TILE
    a = _attn_proj(x, w["w_row"], w["w_t"], *tables)
    k, v, ki = a["k"], a["v"], a["ki"]
    L = P + S
    if k_past is None and S % TK == 0:
        kb = a["kb"].reshape(B, S // TK, TK, -1)
        kib = a["kib"].reshape(B, S // TK, TK, -1)
        vt = a["vt"]
    else:
        kb, kib, v_all = a["kb"], a["kib"], v.astype(bf16)
        if k_past is not None:
            kb = jnp.concatenate([k_past.reshape(B, P, -1).astype(bf16), kb], axis=1)
            kib = jnp.concatenate([ki_past.astype(bf16), kib], axis=1)
            v_all = jnp.concatenate([v_past.reshape(B, P, -1).astype(bf16), v_all], axis=1)
        kb, kib = _key_tiles(kb, TK), _key_tiles(kib, TK)
        vt = _key_tiles(v_all, TK).reshape(B, -1, TK, KV_HEADS, HEAD_DIM).transpose(0, 1, 3, 4, 2)
        ones = jnp.zeros(vt.shape[:3] + (VT_ROWS - HEAD_DIM, TK), bf16).at[:, :, :, 0, :].set(1.0)
        vt = jnp.concatenate([vt, ones], axis=3)
    y_a = _dsa(a["qt"], a["qit"], a["wit"], kb, kib, vt, P, min(TOPK_MAX, L // 4))
    y_b, s_new = _hgrn(x, w["w_h"], w["lb"], w["norm_g"], s0)
    x1, sc_new = _merge(x, y_a, y_b, sc_hist, w["w_c"], w["w_g"], w["w_br"], w["w_out"],
                        w["sconv_w"], w["sconv_b"], w["ln1_g"], w["ln1_b"], alpha)
    x2, ffn_new = _ffn(x1, ffn_hist, w["w_up"], w["ffn_w"], w["ffn_b"], w["w_down"],
                       w["ln2_g"], w["ln2_b"], alpha)
    new = (k.reshape(B, S, KV_HEADS, HEAD_DIM), v.reshape(B, S, KV_HEADS, HEAD_DIM), ki,
           s_new, sc_new, ffn_new)
    return x2, new


def kernel(x_prompt, x_sample, cache_attn_k, cache_attn_v, cache_idx_k, state_hgrn, state_sconv,
           state_ffn_conv, w_in, hgrn_lb_logits, hgrn_norm_g, sconv_w, sconv_b, w_branch, w_out,
           ln1_g, ln1_b, w_up, ffn_conv_w, ffn_conv_b, w_down, ln2_g, ln2_b):
    depth = w_in.shape[0]
    alpha = (2 * depth) ** 0.25
    B, S, _ = x_prompt.shape
    DB, DS, _ = x_sample.shape
    P = cache_attn_k.shape[2]
    cs_p = _rotary_tables(jnp.arange(S))
    cs_s = _rotary_tables(P + jnp.arange(DS))
    xp, xs = x_prompt, x_sample
    st_p = [[] for _ in range(6)]
    st_s = [[] for _ in range(6)]
    w_in_b = _to_bf16(w_in)
    for l in range(depth):
        w = _layer_weights(l, depth, w_in_b, hgrn_lb_logits, hgrn_norm_g, sconv_w, sconv_b, w_branch,
                           w_out, ln1_g, ln1_b, w_up, ffn_conv_w, ffn_conv_b, w_down, ln2_g, ln2_b)
        xp, new_p = _trunk_layer(
            xp, cs_p, 0, None, None, None,
            jnp.zeros((B, HGRN_HEADS, HGRN_DK, HGRN_DV), f32),
            jnp.zeros((B, HIST, SCONV_WIDTH), f32),
            jnp.zeros((B, HIST, 2 * D_FF), f32), w, alpha)
        xs, new_s = _trunk_layer(
            xs, cs_s, P, cache_attn_k[l], cache_attn_v[l], cache_idx_k[l], state_hgrn[l],
            state_sconv[l], state_ffn_conv[l], w, alpha)
        for j in range(6):
            st_p[j].append(new_p[j])
            st_s[j].append(new_s[j])
    outs_p = [jnp.stack(a, axis=0) for a in st_p]
    outs_s = [jnp.stack(a, axis=0) for a in st_s]
    return (xp, xs, *outs_p, *outs_s)
```

```python
import functools

import numpy as np
import jax
import jax.numpy as jnp
from jax import lax
from jax.experimental import pallas as pl
from jax.experimental.pallas import tpu as pltpu

D_MODEL = 1024
CHUNK = 64
N_HEADS = 8
HEAD_DIM = 64
KV_HEADS = 2
GROUPS = N_HEADS // KV_HEADS
IDX_HEADS = 4
IDX_DIM = 64
TOPK_MAX = 256
ROPE_THETA = 500000.0
ATTN_SCALE = HEAD_DIM ** -0.5
IDX_SCALE = (IDX_DIM ** -0.5) * (IDX_HEADS ** -0.5)
NEG = -1e30
HGRN_HEADS = 4
HGRN_DK = 128
HGRN_DV = 128
HGRN_W = HGRN_HEADS * HGRN_DK
SCONV_WIDTH = 512
CONV_W = 3
BRANCH_WIDTH = 512
N_BRANCH = 3
D_FF = 2816
LN_EPS = 1e-5
IN_SIZES = (N_HEADS * HEAD_DIM, KV_HEADS * HEAD_DIM, KV_HEADS * HEAD_DIM,
            IDX_HEADS * IDX_DIM, IDX_DIM, IDX_HEADS,
            HGRN_W, HGRN_W, HGRN_W, HGRN_W,
            SCONV_WIDTH, SCONV_WIDTH, SCONV_WIDTH,
            N_BRANCH * D_MODEL)

LANES = 128
VMEM_LIMIT = 56 * 1024 * 1024
KEY_TILE = 512
QUERY_TILE = 256
SCORE_UNROLL = 4
VT_ROWS = HEAD_DIM + 16
FFN_COLS = 256
FFN_GROUP = 5
MERGE_SUBTILES = 2
HGRN_SUBTILES = 4
PACK = 4
DIGITS = ((25, 7), (18, 7), (11, 7), (4, 7), (0, 4))
GUARD = np.int32(-0x7F7F7F80)
ONES = np.int32(0x01010101)

f32 = jnp.float32
bf16 = jnp.bfloat16
i32 = jnp.int32


def _row_tile(s, want):
    return want if s % want == 0 else s


def _const_spec(shape):
    nd = len(shape)
    return pl.BlockSpec(shape, lambda *_: (0,) * nd, pipeline_mode=pl.Buffered(1))


def _params(sem):
    return pltpu.CompilerParams(dimension_semantics=sem, vmem_limit_bytes=VMEM_LIMIT)


def _layer_norm(z, g, b):
    mu = jnp.mean(z, axis=-1, keepdims=True)
    d = z - mu
    var = jnp.mean(d * d, axis=-1, keepdims=True)
    return d * lax.rsqrt(var + LN_EPS) * g + b


def _sigmoid(x):
    return 1.0 / (1.0 + jnp.exp(-x))


Q_PRESCALE = ATTN_SCALE * float(np.log2(np.e))
ROT = HEAD_DIM // 4
WR_COLS = 384
WT_ROWS = 912
WI_ROWS = 8


def _attn_proj_kernel(x_ref, wr_ref, wt_ref, cs_ref, cst_ref, k_ref, v_ref, ki_ref, kb_ref, kib_ref,
                      vt_ref, qt_ref, qit_ref, wit_ref, *, T, TQ):
    xb = x_ref[0].astype(bf16)
    pr = jnp.dot(xb, wr_ref[...], preferred_element_type=f32)
    c, sa, sb = cs_ref[0], cs_ref[1], cs_ref[2]

    def rot(xg):
        return xg * c + pltpu.roll(xg, LANES - ROT // 2, 1) * sa + pltpu.roll(xg, ROT // 2, 1) * sb

    k = rot(pr[:, 0:LANES])
    ki = rot(pr[:, LANES:2 * LANES])[:, :IDX_DIM]
    k_ref[0] = k
    kb_ref[0] = k.astype(bf16)
    ki_ref[0] = ki
    kib_ref[0] = ki.astype(bf16)
    v_ref[0] = pr[:, 2 * LANES:3 * LANES]

    nt = (((1,), (1,)), ((), ()))
    pt = lax.dot_general(wt_ref[...], xb, nt, preferred_element_type=f32)
    ct, st = cst_ref[0], cst_ref[1]

    def rot_t(hb):
        lead = hb[0:ROT] * ct + jnp.concatenate([hb[ROT // 2:ROT], hb[0:ROT // 2]], axis=0) * st
        return jnp.concatenate([lead, hb[ROT:HEAD_DIM]], axis=0)

    nqb = T // TQ
    for h in range(N_HEADS):
        hb = (rot_t(pt[h * HEAD_DIM:(h + 1) * HEAD_DIM]) * Q_PRESCALE).astype(bf16)
        g, i = divmod(h, GROUPS)
        for n in range(nqb):
            qt_ref[0, n, g, :, i * TQ:(i + 1) * TQ] = hb[:, n * TQ:(n + 1) * TQ]
    base = N_HEADS * HEAD_DIM
    for h in range(IDX_HEADS):
        hb = rot_t(pt[base + h * IDX_DIM:base + (h + 1) * IDX_DIM]).astype(bf16)
        for n in range(nqb):
            qit_ref[0, n, :, h * TQ:(h + 1) * TQ] = hb[:, n * TQ:(n + 1) * TQ]
    base += IDX_HEADS * IDX_DIM
    ones_rows = jnp.where(lax.broadcasted_iota(i32, (VT_ROWS - HEAD_DIM, T), 0) == 0, 1.0, 0.0)
    for g in range(KV_HEADS):
        vt_ref[0, 0, g, 0:HEAD_DIM, :] = pt[base + g * HEAD_DIM:base + (g + 1) * HEAD_DIM].astype(bf16)
        vt_ref[0, 0, g, HEAD_DIM:VT_ROWS, :] = ones_rows.astype(bf16)
    base += KV_HEADS * HEAD_DIM
    for n in range(nqb):
        wit_ref[0, n] = pt[base:base + WI_ROWS, n * TQ:(n + 1) * TQ]


def _attn_proj(x, w_row, w_t, cs, cst):
    B, S, _ = x.shape
    T = _row_tile(S, KEY_TILE)
    TQ = _row_tile(S, QUERY_TILE)
    nqb = T // TQ
    R = GROUPS * TQ
    row = lambda w: pl.BlockSpec((1, T, w), lambda b, t: (b, t, 0))
    names = ("k", "v", "ki", "kb", "kib", "vt", "qt", "qit", "wit")
    outs = pl.pallas_call(
        functools.partial(_attn_proj_kernel, T=T, TQ=TQ),
        grid=(B, S // T),
        in_specs=[row(D_MODEL), _const_spec((D_MODEL, WR_COLS)), _const_spec((WT_ROWS, D_MODEL)),
                  pl.BlockSpec((3, T, LANES), lambda b, t: (0, t, 0)),
                  pl.BlockSpec((2, ROT, T), lambda b, t: (0, 0, t))],
        out_specs=[row(128), row(128), row(IDX_DIM), row(128), row(IDX_DIM),
                   pl.BlockSpec((1, 1, KV_HEADS, VT_ROWS, T), lambda b, t: (b, t, 0, 0, 0)),
                   pl.BlockSpec((1, nqb, KV_HEADS, HEAD_DIM, R), lambda b, t: (b, t, 0, 0, 0)),
                   pl.BlockSpec((1, nqb, IDX_DIM, IDX_HEADS * TQ), lambda b, t: (b, t, 0, 0)),
                   pl.BlockSpec((1, nqb, WI_ROWS, TQ), lambda b, t: (b, t, 0, 0))],
        out_shape=[jax.ShapeDtypeStruct((B, S, 128), f32),
                   jax.ShapeDtypeStruct((B, S, 128), f32),
                   jax.ShapeDtypeStruct((B, S, IDX_DIM), f32),
                   jax.ShapeDtypeStruct((B, S, 128), bf16),
                   jax.ShapeDtypeStruct((B, S, IDX_DIM), bf16),
                   jax.ShapeDtypeStruct((B, S // T, KV_HEADS, VT_ROWS, T), bf16),
                   jax.ShapeDtypeStruct((B, S // TQ, KV_HEADS, HEAD_DIM, R), bf16),
                   jax.ShapeDtypeStruct((B, S // TQ, IDX_DIM, IDX_HEADS * TQ), bf16),
                   jax.ShapeDtypeStruct((B, S // TQ, WI_ROWS, TQ), f32)],
        compiler_params=_params(("parallel", "parallel")),
        name="attn_proj",
    )(x, w_row, w_t, cs, cst)
    return dict(zip(names, outs))


def _rotary_tables(pos):
    half = ROT // 2
    inv_freq = ROPE_THETA ** (-jnp.arange(half, dtype=f32) / half)
    ang = pos.astype(f32)[:, None] * inv_freq[None, :]
    cos = jnp.cos(ang)
    sin = jnp.sin(ang)
    S = pos.shape[0]
    one = jnp.ones((S, HEAD_DIM - ROT), f32)
    zero = jnp.zeros((S, HEAD_DIM - ROT), f32)
    zh = jnp.zeros((S, half), f32)
    c = jnp.concatenate([cos, cos, one], axis=1)
    sa = jnp.concatenate([-sin, zh, zero], axis=1)
    sb = jnp.concatenate([zh, sin, zero], axis=1)
    tile2 = lambda a: jnp.concatenate([a, a], axis=1)
    cs = jnp.stack([tile2(c), tile2(sa), tile2(sb)], axis=0)
    cst = jnp.stack([jnp.concatenate([cos, cos], axis=1).T,
                     jnp.concatenate([-sin, sin], axis=1).T], axis=0)
    return cs, cst


def _dsa_kernel(qt_ref, qit_ref, wit_ref, ki_ref, k_ref, vt_ref, tri_ref, o_ref,
                s_ref, pk_ref, qpad_ref, m_ref, acc_ref, *bufs, TQ, TK, P, top):
    lm_refs = (bufs[0:KV_HEADS], bufs[KV_HEADS:2 * KV_HEADS])
    p_refs = (bufs[2 * KV_HEADS:3 * KV_HEADS], bufs[3 * KV_HEADS:4 * KV_HEADS])
    qb = pl.program_id(1)
    q0 = P + qb * TQ
    nk = (q0 + TQ + TK - 1) // TK
    n_full = q0 // TK
    NC = TK // LANES
    R = GROUPS * TQ
    q_lim = ((q0 + lax.broadcasted_iota(i32, (1, TQ), 1)) // CHUNK + 1) * CHUNK
    wit = wit_ref[0, 0] * IDX_SCALE
    zero_head = jnp.zeros((HEAD_DIM, R), bf16)
    qpad_ref[0] = jnp.concatenate([qt_ref[0, 0, 0], zero_head], axis=0)
    qpad_ref[1] = jnp.concatenate([zero_head, qt_ref[0, 0, 1]], axis=0)

    TKP = TK // PACK

    def lead_digit(key):
        return (key >> DIGITS[0][0]) + (1 << (DIGITS[0][1] - 1))

    def score_tile(masked, j, carry):
        for c in range(NC):
            rows = slice(c * LANES, (c + 1) * LANES)
            rel = jnp.maximum(jnp.dot(ki_ref[0, j, rows, :], qit_ref[0, 0], preferred_element_type=f32), 0.0)
            score = wit[0:1, :] * rel[:, 0:TQ]
            for h in range(1, IDX_HEADS):
                score = score + wit[h:h + 1, :] * rel[:, h * TQ:(h + 1) * TQ]
            if masked:
                k_pos = j * TK + c * LANES + lax.broadcasted_iota(i32, (LANES, TQ), 0)
                score = jnp.where(k_pos < q_lim, score, NEG)
            bits = pltpu.bitcast(score, i32)
            key = jnp.where(bits >= 0, bits, bits ^ 0x7FFFFFFF)
            s_ref[j, rows, :] = key
            if c == 0:
                pk_ref[j] = lead_digit(key) | GUARD
            else:
                pk_ref[j] = pk_ref[j] | (lead_digit(key) << (8 * c))
        return carry

    def score_group(i, carry):
        for u in range(SCORE_UNROLL):
            carry = score_tile(False, SCORE_UNROLL * i + u, carry)
        return carry

    lax.fori_loop(0, n_full // SCORE_UNROLL, score_group, 0)
    lax.fori_loop(n_full // SCORE_UNROLL * SCORE_UNROLL, n_full, functools.partial(score_tile, False), 0)
    lax.fori_loop(n_full, nk, functools.partial(score_tile, True), 0)

    n_chunks = (nk + PACK - 1) // PACK

    def fill_guard(j, carry):
        pk_ref[j] = jnp.full((TKP, TQ), GUARD, i32)
        return carry

    lax.fori_loop(nk, n_chunks * PACK, fill_guard, 0)

    def count_ge(cand):
        cvec = cand * ONES

        def chunk(ci, tot):
            accs = [None] * 4
            for tt in range(PACK):
                for r in range(TKP // 8):
                    slab = pk_ref[ci * PACK + tt, r * 8:(r + 1) * 8, :]
                    hit = lax.shift_right_logical(slab - cvec, 7) & ONES
                    n = (tt * (TKP // 8) + r) % 4
                    accs[n] = hit if accs[n] is None else accs[n] + hit
            a = (accs[0] + accs[1]) + (accs[2] + accs[3])
            return tot + ((a & 0xFF) + ((a >> 8) & 0xFF) + ((a >> 16) & 0xFF)
                          + lax.shift_right_logical(a, 24))

        tot = lax.fori_loop(0, n_chunks, chunk, jnp.zeros((8, TQ), i32))
        return jnp.sum(tot, axis=0, keepdims=True)

    def repack(lo, width, prefix):
        origin = prefix * (1 << width)

        def tile(j, carry):
            for r in range(TKP // 8):
                word = jnp.full((8, TQ), GUARD, i32)
                for q in range(PACK):
                    key = s_ref[j, q * TKP + r * 8:q * TKP + (r + 1) * 8, :]
                    d = (key >> lo) - origin
                    in_range = pltpu.bitcast(d, jnp.uint32) < jnp.uint32(1 << width)
                    word = word | (jnp.where(in_range, d, 0) << (8 * q))
                pk_ref[j, r * 8:(r + 1) * 8, :] = word
            return carry
        lax.fori_loop(0, nk, tile, 0)

    base = jnp.zeros((1, TQ), i32)
    bsize = jnp.full((1, TQ), nk * TK, i32)
    prefix = None
    zero = jnp.zeros((1, TQ), i32)
    for stage, (lo, width) in enumerate(DIGITS):
        need = top - base

        def search_digit(stage=stage, lo=lo, width=width, prefix=prefix, need=need, bsize=bsize):
            if stage > 0:
                repack(lo, width, prefix)

            def bit_step(i, st):
                v, c_v, c_rej = st
                cand = v | jnp.left_shift(jnp.int32(1), width - 1 - i)
                c = count_ge(cand)
                ok = c >= need
                return jnp.where(ok, cand, v), jnp.where(ok, c, c_v), jnp.where(ok, c_rej, c)

            return lax.fori_loop(0, width, bit_step, (zero, bsize, zero))

        v, c_v, c_gt = search_digit()
        base, bsize = base + c_gt, c_v - c_gt
        prefix = v - (1 << (width - 1)) if stage == 0 else prefix * (1 << width) + v
    t = prefix
    neg_key = np.array(NEG, np.float32).view(np.int32) ^ 0x7FFFFFFF
    any_excess = jnp.max(jnp.where(base + bsize > top, 1, 0)) > 0

    @pl.when(jnp.logical_not(any_excess))
    def _():
        thr = jnp.where(t == neg_key, t + 1, t)

        def fast(j, carry):
            s_ref[j] = pltpu.bitcast(jnp.where(s_ref[j] >= thr, 0.0, NEG), i32)
            return carry
        lax.fori_loop(0, nk, fast, 0)

    @pl.when(any_excess)
    def _():
        need = jnp.where(t == neg_key, 0, top - base).astype(f32)

        def ranked(j, taken):
            keys = s_ref[j]
            for c in range(NC):
                kc = keys[c * LANES:(c + 1) * LANES, :]
                eq = jnp.where(kc == t, 1.0, 0.0)
                rank = taken + jnp.dot(tri_ref[...], eq.astype(bf16), preferred_element_type=f32)
                pick = jnp.where(kc > t, 1.0, jnp.where(rank <= need, eq, 0.0))
                s_ref[j, c * LANES:(c + 1) * LANES, :] = pltpu.bitcast(
                    jnp.where(pick > 0.5, 0.0, NEG), i32)
                taken = taken + jnp.sum(eq, axis=0, keepdims=True)
            return taken
        lax.fori_loop(0, nk, ranked, jnp.zeros((1, TQ), f32))

    m_ref[...] = jnp.full(m_ref.shape, NEG, f32)
    acc_ref[...] = jnp.zeros(acc_ref.shape, f32)

    def tile_step(nxt, cur):
        if cur is not None:
            j, slot, tile_max = cur
            m_old = [m_ref[g] for g in range(KV_HEADS)]
            m_new = [jnp.maximum(m_old[g], tile_max[g]) for g in range(KV_HEADS)]
            for g in range(KV_HEADS):
                m_ref[g] = m_new[g]
        if nxt is not None:
            jn, slot_n = nxt
            jc = jnp.minimum(jn, nk - 1)
        out = []
        for g in range(KV_HEADS):
            cmax = jnp.full((8, R), NEG, f32)
            if nxt is not None:
                raw = jnp.dot(k_ref[0, jc], qpad_ref[g], preferred_element_type=f32)
            for c in range(NC):
                rows = slice(c * LANES, (c + 1) * LANES)
                if nxt is not None:
                    bias = pltpu.bitcast(s_ref[jc, rows, :], f32)
                    lm = raw[rows, :] + jnp.concatenate([bias] * GROUPS, axis=1)
                    lm_refs[slot_n][g][rows, :] = lm
                    cmax = jnp.maximum(cmax, jnp.max(lm.reshape(LANES // 8, 8, R), axis=0))
                if cur is not None:
                    p_refs[slot][g][rows, :] = jnp.exp2(lm_refs[slot][g][rows, :] - m_new[g]).astype(bf16)
            out.append(jnp.max(cmax, axis=0, keepdims=True))
            if cur is not None:
                acc_ref[g] = (jnp.exp2(m_old[g] - m_new[g]) * acc_ref[g]
                              + jnp.dot(vt_ref[0, j, g], p_refs[slot][g][...], preferred_element_type=f32))
        return tuple(out)

    def tile_pair(i, tile_max):
        nxt_max = tile_step((2 * i + 1, 1), (2 * i, 0, tile_max))
        return tile_step((2 * i + 2, 0), (2 * i + 1, 1, nxt_max))

    last_max = lax.fori_loop(0, nk // 2, tile_pair, tile_step((0, 0), None))

    @pl.when(nk % 2 == 1)
    def _():
        tile_step(None, (nk - 1, 0, last_max))

    for g in range(KV_HEADS):
        o_t = acc_ref[g, 0:HEAD_DIM, :] / acc_ref[g, HEAD_DIM:HEAD_DIM + 1, :]
        for i in range(0, GROUPS, 2):
            pair = jnp.concatenate([o_t[:, i * TQ:(i + 1) * TQ], o_t[:, (i + 1) * TQ:(i + 2) * TQ]], axis=0)
            h = g * GROUPS + i
            o_ref[0, :, h * HEAD_DIM:(h + 2) * HEAD_DIM] = pair.T.astype(bf16)


def _dsa(qt, qit, wit, kb, kib, vt, P, top):
    B, nq = qt.shape[:2]
    R = qt.shape[-1]
    TQ = R // GROUPS
    NT, TK = kb.shape[1:3]
    assert top <= TK and TQ % CHUNK == 0
    tri = jnp.asarray(np.tril(np.ones((LANES, LANES), np.float32)), bf16)
    return pl.pallas_call(
        functools.partial(_dsa_kernel, TQ=TQ, TK=TK, P=P, top=top),
        grid=(B, nq),
        in_specs=[pl.BlockSpec((1, 1, KV_HEADS, HEAD_DIM, R), lambda b, i: (b, i, 0, 0, 0)),
                  pl.BlockSpec((1, 1, IDX_DIM, IDX_HEADS * TQ), lambda b, i: (b, i, 0, 0)),
                  pl.BlockSpec((1, 1, WI_ROWS, TQ), lambda b, i: (b, i, 0, 0)),
                  pl.BlockSpec((1, NT, TK, IDX_DIM), lambda b, i: (b, 0, 0, 0)),
                  pl.BlockSpec((1, NT, TK, KV_HEADS * HEAD_DIM), lambda b, i: (b, 0, 0, 0)),
                  pl.BlockSpec((1, NT, KV_HEADS, VT_ROWS, TK), lambda b, i: (b, 0, 0, 0, 0)),
                  _const_spec((LANES, LANES))],
        out_specs=pl.BlockSpec((1, TQ, N_HEADS * HEAD_DIM), lambda b, i: (b, i, 0)),
        out_shape=jax.ShapeDtypeStruct((B, nq * TQ, N_HEADS * HEAD_DIM), bf16),
        scratch_shapes=[pltpu.VMEM((NT, TK, TQ), i32),
                        pltpu.VMEM((-(-NT // PACK) * PACK, TK // PACK, TQ), i32),
                        pltpu.VMEM((KV_HEADS, KV_HEADS * HEAD_DIM, R), bf16),
                        pltpu.VMEM((KV_HEADS, 1, R), f32),
                        pltpu.VMEM((KV_HEADS, VT_ROWS, R), f32)]
        + [pltpu.VMEM((TK, R), f32)] * (2 * KV_HEADS)
        + [pltpu.VMEM((TK, R), bf16)] * (2 * KV_HEADS),
        compiler_params=_params(("parallel", "arbitrary")),
        name="dsa",
    )(qt, qit, wit, kib, kb, vt, tri)


def _hgrn_mats(T):
    r = np.arange(T)[:, None]
    c = np.arange(T)[None, :]
    mats = [c <= r]
    b = T // 2
    while b >= 2:
        blk_r = r // (2 * b)
        mid = blk_r * 2 * b + b
        same = blk_r == c // (2 * b)
        late = r >= mid
        mats.append(same & ((late & (c >= mid) & (c <= r)) | (~late & (c > r) & (c < mid))))
        b //= 2
    return np.concatenate(mats, axis=0).astype(np.float32)


def _hgrn_kernel(x_ref, w_ref, lb_ref, g_ref, mats_ref, s0_ref, y_ref, st_out_ref, st_ref, *, T, SUBS):
    t_idx = pl.program_id(1)

    @pl.when(t_idx == 0)
    def _():
        st_ref[...] = s0_ref[0]

    lb = lb_ref[...]
    mats = mats_ref[...]
    row = lax.broadcasted_iota(i32, (T, 1), 0)
    col = lax.broadcasted_iota(i32, (1, T), 1)
    nt = (((1,), (1,)), ((), ()))
    tn = (((0,), (0,)), ((), ()))

    def prepare(i):
        xb = x_ref[0, i * T:(i + 1) * T, :].astype(bf16)
        proj = jnp.dot(xb, w_ref[...], preferred_element_type=f32)
        hq = proj[:, 0:HGRN_W]
        z = proj[:, HGRN_W:2 * HGRN_W]
        hv = proj[:, 2 * HGRN_W:3 * HGRN_W]
        hg = proj[:, 3 * HGRN_W:4 * HGRN_W]
        logf = (jnp.minimum(z, 0.0) - jnp.log1p(jnp.exp(-jnp.abs(z)))) + jnp.log1p(lb * jnp.exp(-z))
        kk = (1.0 - lb) * _sigmoid(-z)
        qq = hq * _sigmoid(hq)
        gate = hg * _sigmoid(hg)
        hi = logf.astype(bf16)
        r1 = logf - hi.astype(f32)
        mid = r1.astype(bf16)
        lo = (r1 - mid.astype(f32)).astype(bf16)
        e_all = jnp.dot(mats, jnp.concatenate([hi, mid, lo], axis=0),
                        preferred_element_type=f32)
        return qq, kk, hv, gate, logf, e_all

    def recur(i, prepared):
        qq, kk, hv, gate, logf, e_all = prepared
        lane2 = lax.broadcasted_iota(i32, (1, 2 * HGRN_DK), 1)
        col2 = lax.broadcasted_iota(i32, (1, 2 * T), 1) % T
        for hp in range(0, HGRN_HEADS, 2):
            sl2 = slice(hp * HGRN_DK, (hp + 2) * HGRN_DK)
            q2, k2 = qq[:, sl2], kk[:, sl2]
            scores2 = jnp.zeros((T, 2 * T), f32)
            b = T // 2
            lvl = 1
            while b >= 1:
                late = ((row // b) % 2) == 1
                if b >= 2:
                    xdec = jnp.exp(e_all[lvl * T:(lvl + 1) * T, sl2])
                else:
                    xdec = jnp.where(late, jnp.exp(logf[:, sl2]), 1.0)
                a_m = jnp.where(late, q2 * xdec, 0.0).astype(bf16)
                b_m = jnp.where(late, 0.0, k2 * xdec)
                b_diag = jnp.concatenate([jnp.where(lane2 < HGRN_DK, b_m, 0.0),
                                          jnp.where(lane2 >= HGRN_DK, b_m, 0.0)], axis=0).astype(bf16)
                sc = lax.dot_general(a_m, b_diag, nt, preferred_element_type=f32)
                if 2 * b < T:
                    sc = jnp.where((row // (2 * b)) == (col2 // (2 * b)), sc, 0.0)
                scores2 = scores2 + sc
                b //= 2
                lvl += 1
            for h in (hp, hp + 1):
                sl = slice(h * HGRN_DK, (h + 1) * HGRN_DK)
                q_h, k_h, v_h = qq[:, sl], kk[:, sl], hv[:, sl]
                vb = v_h.astype(bf16)
                cum = e_all[0:T, sl]
                suf = cum[T - 1:T, :] - cum
                st = st_ref[h]
                qd = (q_h * jnp.exp(cum)).astype(bf16)
                scores = scores2[:, (h - hp) * T:(h - hp + 1) * T]
                o = jnp.sum(q_h * k_h, axis=-1, keepdims=True) * v_h
                o = o + jnp.dot(jnp.concatenate([qd, scores.astype(bf16)], axis=1),
                                jnp.concatenate([st.T.astype(bf16), vb], axis=0), preferred_element_type=f32)
                kd = (k_h * jnp.exp(suf)).astype(bf16)
                st_ref[h] = (st * jnp.exp(cum[T - 1:T, :])
                             + lax.dot_general(vb, kd, tn, preferred_element_type=f32))
                o = o * lax.rsqrt(jnp.mean(o * o, axis=-1, keepdims=True) + LN_EPS) * g_ref[...]
                y_ref[0, i * T:(i + 1) * T, sl] = (o * gate[:, sl]).astype(bf16)

    nxt = prepare(0)
    for i in range(SUBS):
        cur = nxt
        if i + 1 < SUBS:
            nxt = prepare(i + 1)
        recur(i, cur)

    st_out_ref[0] = st_ref[...]


def _hgrn(x, w_h, lb, norm_g, s0):
    B, S, _ = x.shape
    T = _row_tile(S, 2 * CHUNK)
    SUBS = HGRN_SUBTILES if S % (T * HGRN_SUBTILES) == 0 else 1
    TS = T * SUBS
    mats = jnp.asarray(np.tile(_hgrn_mats(T), (1, 3)), bf16)
    nm = mats.shape[0]
    st0 = jnp.swapaxes(s0.astype(f32), 2, 3)
    y, st = pl.pallas_call(
        functools.partial(_hgrn_kernel, T=T, SUBS=SUBS),
        grid=(B, S // TS),
        in_specs=[pl.BlockSpec((1, TS, D_MODEL), lambda b, t: (b, t, 0)),
                  _const_spec((D_MODEL, 4 * HGRN_W)),
                  _const_spec((1, HGRN_W)),
                  _const_spec((1, HGRN_DV)),
                  _const_spec((nm, 3 * T)),
                  pl.BlockSpec((1, HGRN_HEADS, HGRN_DV, HGRN_DK), lambda b, t: (b, 0, 0, 0))],
        out_specs=[pl.BlockSpec((1, TS, HGRN_W), lambda b, t: (b, t, 0)),
                   pl.BlockSpec((1, HGRN_HEADS, HGRN_DV, HGRN_DK), lambda b, t: (b, 0, 0, 0))],
        out_shape=[jax.ShapeDtypeStruct((B, S, HGRN_W), bf16),
                   jax.ShapeDtypeStruct((B, HGRN_HEADS, HGRN_DV, HGRN_DK), f32)],
        scratch_shapes=[pltpu.VMEM((HGRN_HEADS, HGRN_DV, HGRN_DK), f32)],
        compiler_params=_params(("parallel", "arbitrary")),
        name="hgrn",
    )(x, w_h, lb, norm_g, mats, st0)
    return y, jnp.swapaxes(st, 2, 3)


HIST = CONV_W - 1
PADROWS = 8


def _conv_rows(buf_ref, cols, u, w, bias, T):
    buf_ref[PADROWS:PADROWS + T, cols] = u
    y = bias
    for j in range(CONV_W):
        y = y + w[j:j + 1, :] * buf_ref[PADROWS - HIST + j:PADROWS - HIST + j + T, cols]
    buf_ref[PADROWS - HIST:PADROWS, cols] = u[T - HIST:T, :]
    return y


def _merge_kernel(x_ref, ya_ref, yb_ref, hist_ref, wc_ref, wg_ref, wbr_ref, wout_ref, cw_ref, cb_ref,
                  g_ref, b_ref, o_ref, hist_out_ref, buf_ref, *, T, SUBS, alpha):
    @pl.when(pl.program_id(1) == 0)
    def _():
        buf_ref[PADROWS - HIST:PADROWS, :] = hist_ref[0]

    for i in range(SUBS):
        rows = slice(i * T, (i + 1) * T)
        x = x_ref[0, rows, :]
        xb = x.astype(bf16)
        c = jnp.dot(xb, wc_ref[...], preferred_element_type=f32)
        W = SCONV_WIDTH
        u = c[:, W:2 * W] * c[:, 2 * W:3 * W]
        u_conv = _conv_rows(buf_ref, slice(0, W), u, cw_ref[...], cb_ref[...], T)
        yc = (c[:, 0:W] * u_conv).astype(bf16)
        merged = None
        for n, yb in enumerate((ya_ref[0, rows, :], yb_ref[0, rows, :], yc)):
            gate = _sigmoid(jnp.dot(xb, wg_ref[:, n * D_MODEL:(n + 1) * D_MODEL], preferred_element_type=f32))
            term = gate * jnp.dot(yb, wbr_ref[n], preferred_element_type=f32)
            merged = term if merged is None else merged + term
        out = jnp.dot(merged.astype(bf16), wout_ref[...], preferred_element_type=f32)
        o_ref[0, rows, :] = _layer_norm(alpha * x + out, g_ref[...], b_ref[...])
    hist_out_ref[0] = buf_ref[PADROWS - HIST:PADROWS, :]


def _merge(x, ya, yb, hist, wc, wg, wbr, wout, cw, cb, g, b, alpha):
    B, S, _ = x.shape
    T = _row_tile(S, 256)
    SUBS = MERGE_SUBTILES if S % (T * MERGE_SUBTILES) == 0 else 1
    row = lambda w: pl.BlockSpec((1, T * SUBS, w), lambda bb, t: (bb, t, 0))
    hspec = pl.BlockSpec((1, HIST, SCONV_WIDTH), lambda bb, t: (bb, 0, 0))
    return pl.pallas_call(
        functools.partial(_merge_kernel, T=T, SUBS=SUBS, alpha=alpha),
        grid=(B, S // (T * SUBS)),
        in_specs=[row(D_MODEL), row(BRANCH_WIDTH), row(BRANCH_WIDTH), hspec,
                  _const_spec((D_MODEL, 3 * SCONV_WIDTH)), _const_spec((D_MODEL, N_BRANCH * D_MODEL)),
                  _const_spec((N_BRANCH, BRANCH_WIDTH, D_MODEL)), _const_spec((D_MODEL, D_MODEL)),
                  _const_spec((CONV_W, SCONV_WIDTH)), _const_spec((1, SCONV_WIDTH)),
                  _const_spec((1, D_MODEL)), _const_spec((1, D_MODEL))],
        out_specs=[row(D_MODEL), hspec],
        out_shape=[jax.ShapeDtypeStruct((B, S, D_MODEL), f32),
                   jax.ShapeDtypeStruct((B, HIST, SCONV_WIDTH), f32)],
        scratch_shapes=[pltpu.VMEM((PADROWS + T, SCONV_WIDTH), f32)],
        compiler_params=_params(("parallel", "arbitrary")),
        name="merge",
    )(x, ya, yb, hist, wc, wg, wbr, wout, cw, cb, g, b)


def _ffn_kernel(x_ref, hist_ref, wup_ref, cw_ref, cb_ref, wdn_ref, g_ref, b_ref, o_ref, hist_out_ref,
                *bufs, T, alpha):
    n_steps = D_FF // FFN_COLS
    col_ranges = [slice(off + c * FFN_COLS, off + (c + 1) * FFN_COLS)
                  for c in range(n_steps) for off in (0, D_FF)]

    @pl.when(pl.program_id(1) == 0)
    def _():
        for buf, cols in zip(bufs, col_ranges):
            buf[PADROWS - HIST:PADROWS, :] = hist_ref[0, :, cols]

    x = x_ref[0]
    xb = x.astype(bf16)

    def up_conv(cidx):
        halves = []
        for half in range(2):
            buf, cols = bufs[2 * cidx + half], col_ranges[2 * cidx + half]
            h = jnp.dot(xb, wup_ref[:, cols], preferred_element_type=f32)
            halves.append(_conv_rows(buf, slice(None), h, cw_ref[:, cols], cb_ref[:, cols], T))
            hist_out_ref[0, :, cols] = buf[PADROWS - HIST:PADROWS, :]
        return halves

    groups = [list(range(c, min(c + FFN_GROUP, n_steps))) for c in range(0, n_steps, FFN_GROUP)]
    acc = jnp.zeros((T, D_MODEL), f32)
    nxt = [up_conv(c) for c in groups[0]]
    for gi, grp in enumerate(groups):
        cur = nxt
        if gi + 1 < len(groups):
            nxt = [up_conv(c) for c in groups[gi + 1]]
        act = jnp.concatenate([(a_g * _sigmoid(a_g) * b_v).astype(bf16) for a_g, b_v in cur], axis=1)
        acc = acc + jnp.dot(act, wdn_ref[grp[0] * FFN_COLS:(grp[-1] + 1) * FFN_COLS, :],
                            preferred_element_type=f32)
    o_ref[0] = _layer_norm(alpha * x + acc, g_ref[...], b_ref[...])


def _ffn(x, hist, wup, cw, cb, wdn, g, b, alpha):
    B, S, _ = x.shape
    T = _row_tile(S, 256)
    row = pl.BlockSpec((1, T, D_MODEL), lambda bb, t: (bb, t, 0))
    hspec = pl.BlockSpec((1, HIST, 2 * D_FF), lambda bb, t: (bb, 0, 0))
    return pl.pallas_call(
        functools.partial(_ffn_kernel, T=T, alpha=alpha),
        grid=(B, S // T),
        in_specs=[row, hspec, _const_spec((D_MODEL, 2 * D_FF)), _const_spec((CONV_W, 2 * D_FF)),
                  _const_spec((1, 2 * D_FF)), _const_spec((D_FF, D_MODEL)),
                  _const_spec((1, D_MODEL)), _const_spec((1, D_MODEL))],
        out_specs=[row, hspec],
        out_shape=[jax.ShapeDtypeStruct((B, S, D_MODEL), f32),
                   jax.ShapeDtypeStruct((B, HIST, 2 * D_FF), f32)],
        scratch_shapes=[pltpu.VMEM((PADROWS + T, FFN_COLS), f32)] * (2 * D_FF // FFN_COLS),
        compiler_params=_params(("parallel", "arbitrary")),
        name="ffn",
    )(x, hist, wup, cw, cb, wdn, g, b)


def _cast_kernel(x_ref, o_ref):
    o_ref[...] = x_ref[...].astype(o_ref.dtype)


def _to_bf16(w):
    D, Rw, C = w.shape
    T = _row_tile(Rw, LANES)
    spec = pl.BlockSpec((1, T, C), lambda d, t: (d, t, 0))
    return pl.pallas_call(_cast_kernel, grid=(D, Rw // T), in_specs=[spec], out_specs=spec,
                          out_shape=jax.ShapeDtypeStruct(w.shape, bf16),
                          compiler_params=_params(("parallel", "parallel")), name="to_bf16")(w)


def _layer_weights(l, depth, w_in, hgrn_lb_logits, hgrn_norm_g, sconv_w, sconv_b, w_branch, w_out,
                   ln1_g, ln1_b, w_up, ffn_conv_w, ffn_conv_b, w_down, ln2_g, ln2_b):
    offs = np.concatenate([[0], np.cumsum(IN_SIZES)]).tolist()
    sec = lambda i, j=None: w_in[l][:, offs[i]:offs[(i if j is None else j) + 1]]
    a_q, a_k, a_v, i_q, i_k, i_w = (sec(i) for i in range(6))
    zpad = lambda n: jnp.zeros((D_MODEL, n), w_in.dtype)
    w_row = jnp.concatenate([a_k, i_k, zpad(LANES - IDX_DIM), a_v], axis=1).astype(bf16)
    w_t = jnp.concatenate([a_q, i_q, a_v, i_w, zpad(WT_ROWS - 900)], axis=1).T.astype(bf16)
    lbp = jax.nn.softmax(hgrn_lb_logits.astype(f32), axis=0)
    lb = (jnp.cumsum(lbp, axis=0) - lbp[0])[l].reshape(1, HGRN_W)
    row = lambda a: a.reshape(1, -1).astype(f32)
    return dict(
        w_row=w_row, w_t=w_t, w_h=sec(6, 9).astype(bf16), lb=lb, norm_g=row(hgrn_norm_g[l]),
        w_c=sec(10, 12).astype(bf16), w_g=sec(13).astype(bf16),
        w_br=w_branch[l].astype(bf16), w_out=w_out[l].astype(bf16),
        sconv_w=sconv_w[l].astype(f32), sconv_b=row(sconv_b[l]),
        ln1_g=row(ln1_g[l]), ln1_b=row(ln1_b[l]),
        w_up=w_up[l].astype(bf16), ffn_w=ffn_conv_w[l].astype(f32), ffn_b=row(ffn_conv_b[l]),
        w_down=w_down[l].astype(bf16), ln2_g=row(ln2_g[l]), ln2_b=row(ln2_b[l]))


def _key_tiles(a, TK):
    B, L, C = a.shape
    NT = -(-L // TK)
    a = jnp.pad(a, ((0, 0), (0, NT * TK - L), (0, 0)))
    return a.reshape(B, NT, TK, C)


def _trunk_layer(x, tables, P, k_past, v_past, ki_past, s0, sc_hist, ffn_hist, w, alpha):
    B, S, _ = x.shape
    TK = KEY_TILE
    a = _attn_proj(x, w["w_row"], w["w_t"], *tables)
    k, v, ki = a["k"], a["v"], a["ki"]
    L = P + S
    if k_past is None and S % TK == 0:
        kb = a["kb"].reshape(B, S // TK, TK, -1)
        kib = a["kib"].reshape(B, S // TK, TK, -1)
        vt = a["vt"]
    else:
        kb, kib, v_all = a["kb"], a["kib"], v.astype(bf16)
        if k_past is not None:
            kb = jnp.concatenate([k_past.reshape(B, P, -1).astype(bf16), kb], axis=1)
            kib = jnp.concatenate([ki_past.astype(bf16), kib], axis=1)
            v_all = jnp.concatenate([v_past.reshape(B, P, -1).astype(bf16), v_all], axis=1)
        kb, kib = _key_tiles(kb, TK), _key_tiles(kib, TK)
        vt = _key_tiles(v_all, TK).reshape(B, -1, TK, KV_HEADS, HEAD_DIM).transpose(0, 1, 3, 4, 2)
        ones = jnp.zeros(vt.shape[:3] + (VT_ROWS - HEAD_DIM, TK), bf16).at[:, :, :, 0, :].set(1.0)
        vt = jnp.concatenate([vt, ones], axis=3)
    y_a = _dsa(a["qt"], a["qit"], a["wit"], kb, kib, vt, P, min(TOPK_MAX, L // 4))
    y_b, s_new = _hgrn(x, w["w_h"], w["lb"], w["norm_g"], s0)
    x1, sc_new = _merge(x, y_a, y_b, sc_hist, w["w_c"], w["w_g"], w["w_br"], w["w_out"],
                        w["sconv_w"], w["sconv_b"], w["ln1_g"], w["ln1_b"], alpha)
    x2, ffn_new = _ffn(x1, ffn_hist, w["w_up"], w["ffn_w"], w["ffn_b"], w["w_down"],
                       w["ln2_g"], w["ln2_b"], alpha)
    new = (k.reshape(B, S, KV_HEADS, HEAD_DIM), v.reshape(B, S, KV_HEADS, HEAD_DIM), ki,
           s_new, sc_new, ffn_new)
    return x2, new


def kernel(x_prompt, x_sample, cache_attn_k, cache_attn_v, cache_idx_k, state_hgrn, state_sconv,
           state_ffn_conv, w_in, hgrn_lb_logits, hgrn_norm_g, sconv_w, sconv_b, w_branch, w_out,
           ln1_g, ln1_b, w_up, ffn_conv_w, ffn_conv_b, w_down, ln2_g, ln2_b):
    depth = w_in.shape[0]
    alpha = (2 * depth) ** 0.25
    B, S, _ = x_prompt.shape
    DB, DS, _ = x_sample.shape
    P = cache_attn_k.shape[2]
    cs_p = _rotary_tables(jnp.arange(S))
    cs_s = _rotary_tables(P + jnp.arange(DS))
    xp, xs = x_prompt, x_sample
    st_p = [[] for _ in range(6)]
    st_s = [[] for _ in range(6)]
    w_in_b = _to_bf16(w_in)
    for l in range(depth):
        w = _layer_weights(l, depth, w_in_b, hgrn_lb_logits, hgrn_norm_g, sconv_w, sconv_b, w_branch,
                           w_out, ln1_g, ln1_b, w_up, ffn_conv_w, ffn_conv_b, w_down, ln2_g, ln2_b)
        xp, new_p = _trunk_layer(
            xp, cs_p, 0, None, None, None,
            jnp.zeros((B, HGRN_HEADS, HGRN_DK, HGRN_DV), f32),
            jnp.zeros((B, HIST, SCONV_WIDTH), f32),
            jnp.zeros((B, HIST, 2 * D_FF), f32), w, alpha)
        xs, new_s = _trunk_layer(
            xs, cs_s, P, cache_attn_k[l], cache_attn_v[l], cache_idx_k[l], state_hgrn[l],
            state_sconv[l], state_ffn_conv[l], w, alpha)
        for j in range(6):
            st_p[j].append(new_p[j])
            st_s[j].append(new_s[j])
    outs_p = [jnp.stack(a, axis=0) for a in st_p]
    outs_s = [jnp.stack(a, axis=0) for a in st_s]
    return (xp, xs, *outs_p, *outs_s)
```

```python
import functools

import numpy as np
import jax
import jax.numpy as jnp
from jax import lax
from jax.experimental import pallas as pl
from jax.experimental.pallas import tpu as pltpu

D_MODEL = 1024
CHUNK = 64
N_HEADS = 8
HEAD_DIM = 64
KV_HEADS = 2
GROUPS = N_HEADS // KV_HEADS
IDX_HEADS = 4
IDX_DIM = 64
TOPK_MAX = 256
ROPE_THETA = 500000.0
ATTN_SCALE = HEAD_DIM ** -0.5
IDX_SCALE = (IDX_DIM ** -0.5) * (IDX_HEADS ** -0.5)
NEG = -1e30
HGRN_HEADS = 4
HGRN_DK = 128
HGRN_DV = 128
HGRN_W = HGRN_HEADS * HGRN_DK
SCONV_WIDTH = 512
CONV_W = 3
BRANCH_WIDTH = 512
N_BRANCH = 3
D_FF = 2816
LN_EPS = 1e-5
IN_SIZES = (N_HEADS * HEAD_DIM, KV_HEADS * HEAD_DIM, KV_HEADS * HEAD_DIM,
            IDX_HEADS * IDX_DIM, IDX_DIM, IDX_HEADS,
            HGRN_W, HGRN_W, HGRN_W, HGRN_W,
            SCONV_WIDTH, SCONV_WIDTH, SCONV_WIDTH,
            N_BRANCH * D_MODEL)

LANES = 128
VMEM_LIMIT = 56 * 1024 * 1024
KEY_TILE = 512
QUERY_TILE = 256
SCORE_UNROLL = 4
VT_ROWS = HEAD_DIM + 16
FFN_COLS = 256
FFN_GROUP = 5
MERGE_SUBTILES = 2
HGRN_SUBTILES = 4
PACK = 4
DIGITS = ((25, 7), (18, 7), (11, 7), (4, 7), (0, 4))
GUARD = np.int32(-0x7F7F7F80)
ONES = np.int32(0x01010101)

f32 = jnp.float32
bf16 = jnp.bfloat16
i32 = jnp.int32


def _row_tile(s, want):
    return want if s % want == 0 else s


def _const_spec(shape):
    nd = len(shape)
    return pl.BlockSpec(shape, lambda *_: (0,) * nd, pipeline_mode=pl.Buffered(1))


def _params(sem):
    return pltpu.CompilerParams(dimension_semantics=sem, vmem_limit_bytes=VMEM_LIMIT)


def _layer_norm(z, g, b):
    mu = jnp.mean(z, axis=-1, keepdims=True)
    d = z - mu
    var = jnp.mean(d * d, axis=-1, keepdims=True)
    return d * lax.rsqrt(var + LN_EPS) * g + b


def _sigmoid(x):
    return 1.0 / (1.0 + jnp.exp(-x))


Q_PRESCALE = ATTN_SCALE * float(np.log2(np.e))
ROT = HEAD_DIM // 4
WR_COLS = 384
WT_ROWS = 912
WI_ROWS = 8


def _attn_proj_kernel(x_ref, wr_ref, wt_ref, cs_ref, cst_ref, k_ref, v_ref, ki_ref, kb_ref, kib_ref,
                      vt_ref, qt_ref, qit_ref, wit_ref, *, T, TQ):
    xb = x_ref[0].astype(bf16)
    pr = jnp.dot(xb, wr_ref[...], preferred_element_type=f32)
    c, sa, sb = cs_ref[0], cs_ref[1], cs_ref[2]

    def rot(xg):
        return xg * c + pltpu.roll(xg, LANES - ROT // 2, 1) * sa + pltpu.roll(xg, ROT // 2, 1) * sb

    k = rot(pr[:, 0:LANES])
    ki = rot(pr[:, LANES:2 * LANES])[:, :IDX_DIM]
    k_ref[0] = k
    kb_ref[0] = k.astype(bf16)
    ki_ref[0] = ki
    kib_ref[0] = ki.astype(bf16)
    v_ref[0] = pr[:, 2 * LANES:3 * LANES]

    nt = (((1,), (1,)), ((), ()))
    pt = lax.dot_general(wt_ref[...], xb, nt, preferred_element_type=f32)
    ct, st = cst_ref[0], cst_ref[1]

    def rot_t(hb):
        lead = hb[0:ROT] * ct + jnp.concatenate([hb[ROT // 2:ROT], hb[0:ROT // 2]], axis=0) * st
        return jnp.concatenate([lead, hb[ROT:HEAD_DIM]], axis=0)

    nqb = T // TQ
    for h in range(N_HEADS):
        hb = (rot_t(pt[h * HEAD_DIM:(h + 1) * HEAD_DIM]) * Q_PRESCALE).astype(bf16)
        g, i = divmod(h, GROUPS)
        for n in range(nqb):
            qt_ref[0, n, g, :, i * TQ:(i + 1) * TQ] = hb[:, n * TQ:(n + 1) * TQ]
    base = N_HEADS * HEAD_DIM
    for h in range(IDX_HEADS):
        hb = rot_t(pt[base + h * IDX_DIM:base + (h + 1) * IDX_DIM]).astype(bf16)
        for n in range(nqb):
            qit_ref[0, n, :, h * TQ:(h + 1) * TQ] = hb[:, n * TQ:(n + 1) * TQ]
    base += IDX_HEADS * IDX_DIM
    ones_rows = jnp.where(lax.broadcasted_iota(i32, (VT_ROWS - HEAD_DIM, T), 0) == 0, 1.0, 0.0)
    for g in range(KV_HEADS):
        vt_ref[0, 0, g, 0:HEAD_DIM, :] = pt[base + g * HEAD_DIM:base + (g + 1) * HEAD_DIM].astype(bf16)
        vt_ref[0, 0, g, HEAD_DIM:VT_ROWS, :] = ones_rows.astype(bf16)
    base += KV_HEADS * HEAD_DIM
    for n in range(nqb):
        wit_ref[0, n] = pt[base:base + WI_ROWS, n * TQ:(n + 1) * TQ]


def _attn_proj(x, w_row, w_t, cs, cst):
    B, S, _ = x.shape
    T = _row_tile(S, KEY_TILE)
    TQ = _row_tile(S, QUERY_TILE)
    nqb = T // TQ
    R = GROUPS * TQ
    row = lambda w: pl.BlockSpec((1, T, w), lambda b, t: (b, t, 0))
    names = ("k", "v", "ki", "kb", "kib", "vt", "qt", "qit", "wit")
    outs = pl.pallas_call(
        functools.partial(_attn_proj_kernel, T=T, TQ=TQ),
        grid=(B, S // T),
        in_specs=[row(D_MODEL), _const_spec((D_MODEL, WR_COLS)), _const_spec((WT_ROWS, D_MODEL)),
                  pl.BlockSpec((3, T, LANES), lambda b, t: (0, t, 0)),
                  pl.BlockSpec((2, ROT, T), lambda b, t: (0, 0, t))],
        out_specs=[row(128), row(128), row(IDX_DIM), row(128), row(IDX_DIM),
                   pl.BlockSpec((1, 1, KV_HEADS, VT_ROWS, T), lambda b, t: (b, t, 0, 0, 0)),
                   pl.BlockSpec((1, nqb, KV_HEADS, HEAD_DIM, R), lambda b, t: (b, t, 0, 0, 0)),
                   pl.BlockSpec((1, nqb, IDX_DIM, IDX_HEADS * TQ), lambda b, t: (b, t, 0, 0)),
                   pl.BlockSpec((1, nqb, WI_ROWS, TQ), lambda b, t: (b, t, 0, 0))],
        out_shape=[jax.ShapeDtypeStruct((B, S, 128), f32),
                   jax.ShapeDtypeStruct((B, S, 128), f32),
                   jax.ShapeDtypeStruct((B, S, IDX_DIM), f32),
                   jax.ShapeDtypeStruct((B, S, 128), bf16),
                   jax.ShapeDtypeStruct((B, S, IDX_DIM), bf16),
                   jax.ShapeDtypeStruct((B, S // T, KV_HEADS, VT_ROWS, T), bf16),
                   jax.ShapeDtypeStruct((B, S // TQ, KV_HEADS, HEAD_DIM, R), bf16),
                   jax.ShapeDtypeStruct((B, S // TQ, IDX_DIM, IDX_HEADS * TQ), bf16),
                   jax.ShapeDtypeStruct((B, S // TQ, WI_ROWS, TQ), f32)],
        compiler_params=_params(("parallel", "parallel")),
        name="attn_proj",
    )(x, w_row, w_t, cs, cst)
    return dict(zip(names, outs))


def _rotary_tables(pos):
    half = ROT // 2
    inv_freq = ROPE_THETA ** (-jnp.arange(half, dtype=f32) / half)
    ang = pos.astype(f32)[:, None] * inv_freq[None, :]
    cos = jnp.cos(ang)
    sin = jnp.sin(ang)
    S = pos.shape[0]
    one = jnp.ones((S, HEAD_DIM - ROT), f32)
    zero = jnp.zeros((S, HEAD_DIM - ROT), f32)
    zh = jnp.zeros((S, half), f32)
    c = jnp.concatenate([cos, cos, one], axis=1)
    sa = jnp.concatenate([-sin, zh, zero], axis=1)
    sb = jnp.concatenate([zh, sin, zero], axis=1)
    tile2 = lambda a: jnp.concatenate([a, a], axis=1)
    cs = jnp.stack([tile2(c), tile2(sa), tile2(sb)], axis=0)
    cst = jnp.stack([jnp.concatenate([cos, cos], axis=1).T,
                     jnp.concatenate([-sin, sin], axis=1).T], axis=0)
    return cs, cst


def _dsa_kernel(qt_ref, qit_ref, wit_ref, ki_ref, k_ref, vt_ref, tri_ref, o_ref,
                s_ref, pk_ref, qpad_ref, m_ref, acc_ref, *bufs, TQ, TK, P, top):
    lm_refs = (bufs[0:KV_HEADS], bufs[KV_HEADS:2 * KV_HEADS])
    p_refs = (bufs[2 * KV_HEADS:3 * KV_HEADS], bufs[3 * KV_HEADS:4 * KV_HEADS])
    qb = pl.program_id(1)
    q0 = P + qb * TQ
    nk = (q0 + TQ + TK - 1) // TK
    n_full = q0 // TK
    NC = TK // LANES
    R = GROUPS * TQ
    q_lim = ((q0 + lax.broadcasted_iota(i32, (1, TQ), 1)) // CHUNK + 1) * CHUNK
    wit = wit_ref[0, 0] * IDX_SCALE
    zero_head = jnp.zeros((HEAD_DIM, R), bf16)
    qpad_ref[0] = jnp.concatenate([qt_ref[0, 0, 0], zero_head], axis=0)
    qpad_ref[1] = jnp.concatenate([zero_head, qt_ref[0, 0, 1]], axis=0)

    TKP = TK // PACK

    def lead_digit(key):
        return (key >> DIGITS[0][0]) + (1 << (DIGITS[0][1] - 1))

    def score_tile(masked, j, carry):
        for c in range(NC):
            rows = slice(c * LANES, (c + 1) * LANES)
            rel = jnp.maximum(jnp.dot(ki_ref[0, j, rows, :], qit_ref[0, 0], preferred_element_type=f32), 0.0)
            score = wit[0:1, :] * rel[:, 0:TQ]
            for h in range(1, IDX_HEADS):
                score = score + wit[h:h + 1, :] * rel[:, h * TQ:(h + 1) * TQ]
            if masked:
                k_pos = j * TK + c * LANES + lax.broadcasted_iota(i32, (LANES, TQ), 0)
                score = jnp.where(k_pos < q_lim, score, NEG)
            bits = pltpu.bitcast(score, i32)
            key = jnp.where(bits >= 0, bits, bits ^ 0x7FFFFFFF)
            s_ref[j, rows, :] = key
            if c == 0:
                pk_ref[j] = lead_digit(key) | GUARD
            else:
                pk_ref[j] = pk_ref[j] | (lead_digit(key) << (8 * c))
        return carry

    def score_group(i, carry):
        for u in range(SCORE_UNROLL):
            carry = score_tile(False, SCORE_UNROLL * i + u, carry)
        return carry

    lax.fori_loop(0, n_full // SCORE_UNROLL, score_group, 0)
    lax.fori_loop(n_full // SCORE_UNROLL * SCORE_UNROLL, n_full, functools.partial(score_tile, False), 0)
    lax.fori_loop(n_full, nk, functools.partial(score_tile, True), 0)

    n_chunks = (nk + PACK - 1) // PACK

    def fill_guard(j, carry):
        pk_ref[j] = jnp.full((TKP, TQ), GUARD, i32)
        return carry

    lax.fori_loop(nk, n_chunks * PACK, fill_guard, 0)

    def count_ge(cand):
        cvec = cand * ONES

        def chunk(ci, tot):
            accs = [None] * 4
            for tt in range(PACK):
                for r in range(TKP // 8):
                    slab = pk_ref[ci * PACK + tt, r * 8:(r + 1) * 8, :]
                    hit = lax.shift_right_logical(slab - cvec, 7) & ONES
                    n = (tt * (TKP // 8) + r) % 4
                    accs[n] = hit if accs[n] is None else accs[n] + hit
            a = (accs[0] + accs[1]) + (accs[2] + accs[3])
            return tot + ((a & 0xFF) + ((a >> 8) & 0xFF) + ((a >> 16) & 0xFF)
                          + lax.shift_right_logical(a, 24))

        tot = lax.fori_loop(0, n_chunks, chunk, jnp.zeros((8, TQ), i32))
        return jnp.sum(tot, axis=0, keepdims=True)

    def repack(lo, width, prefix):
        origin = prefix * (1 << width)

        def tile(j, carry):
            for r in range(TKP // 8):
                word = jnp.full((8, TQ), GUARD, i32)
                for q in range(PACK):
                    key = s_ref[j, q * TKP + r * 8:q * TKP + (r + 1) * 8, :]
                    d = (key >> lo) - origin
                    in_range = pltpu.bitcast(d, jnp.uint32) < jnp.uint32(1 << width)
                    word = word | (jnp.where(in_range, d, 0) << (8 * q))
                pk_ref[j, r * 8:(r + 1) * 8, :] = word
            return carry
        lax.fori_loop(0, nk, tile, 0)

    base = jnp.zeros((1, TQ), i32)
    bsize = jnp.full((1, TQ), nk * TK, i32)
    prefix = None
    zero = jnp.zeros((1, TQ), i32)
    for stage, (lo, width) in enumerate(DIGITS):
        need = top - base

        def search_digit(stage=stage, lo=lo, width=width, prefix=prefix, need=need, bsize=bsize):
            if stage > 0:
                repack(lo, width, prefix)

            def bit_step(i, st):
                v, c_v, c_rej = st
                cand = v | jnp.left_shift(jnp.int32(1), width - 1 - i)
                c = count_ge(cand)
                ok = c >= need
                return jnp.where(ok, cand, v), jnp.where(ok, c, c_v), jnp.where(ok, c_rej, c)

            return lax.fori_loop(0, width, bit_step, (zero, bsize, zero))

        v, c_v, c_gt = search_digit()
        base, bsize = base + c_gt, c_v - c_gt
        prefix = v - (1 << (width - 1)) if stage == 0 else prefix * (1 << width) + v
    t = prefix
    neg_key = np.array(NEG, np.float32).view(np.int32) ^ 0x7FFFFFFF
    any_excess = jnp.max(jnp.where(base + bsize > top, 1, 0)) > 0

    @pl.when(jnp.logical_not(any_excess))
    def _():
        thr = jnp.where(t == neg_key, t + 1, t)

        def fast(j, carry):
            s_ref[j] = pltpu.bitcast(jnp.where(s_ref[j] >= thr, 0.0, NEG), i32)
            return carry
        lax.fori_loop(0, nk, fast, 0)

    @pl.when(any_excess)
    def _():
        need = jnp.where(t == neg_key, 0, top - base).astype(f32)

        def ranked(j, taken):
            keys = s_ref[j]
            for c in range(NC):
                kc = keys[c * LANES:(c + 1) * LANES, :]
                eq = jnp.where(kc == t, 1.0, 0.0)
                rank = taken + jnp.dot(tri_ref[...], eq.astype(bf16), preferred_element_type=f32)
                pick = jnp.where(kc > t, 1.0, jnp.where(rank <= need, eq, 0.0))
                s_ref[j, c * LANES:(c + 1) * LANES, :] = pltpu.bitcast(
                    jnp.where(pick > 0.5, 0.0, NEG), i32)
                taken = taken + jnp.sum(eq, axis=0, keepdims=True)
            return taken
        lax.fori_loop(0, nk, ranked, jnp.zeros((1, TQ), f32))

    m_ref[...] = jnp.full(m_ref.shape, NEG, f32)
    acc_ref[...] = jnp.zeros(acc_ref.shape, f32)

    def tile_step(nxt, cur):
        if cur is not None:
            j, slot, tile_max = cur
            m_old = [m_ref[g] for g in range(KV_HEADS)]
            m_new = [jnp.maximum(m_old[g], tile_max[g]) for g in range(KV_HEADS)]
            for g in range(KV_HEADS):
                m_ref[g] = m_new[g]
        if nxt is not None:
            jn, slot_n = nxt
            jc = jnp.minimum(jn, nk - 1)
        out = []
        for g in range(KV_HEADS):
            cmax = jnp.full((8, R), NEG, f32)
            if nxt is not None:
                raw = jnp.dot(k_ref[0, jc], qpad_ref[g], preferred_element_type=f32)
            for c in range(NC):
                rows = slice(c * LANES, (c + 1) * LANES)
                if nxt is not None:
                    bias = pltpu.bitcast(s_ref[jc, rows, :], f32)
                    lm = raw[rows, :] + jnp.concatenate([bias] * GROUPS, axis=1)
                    lm_refs[slot_n][g][rows, :] = lm
                    cmax = jnp.maximum(cmax, jnp.max(lm.reshape(LANES // 8, 8, R), axis=0))
                if cur is not None:
                    p_refs[slot][g][rows, :] = jnp.exp2(lm_refs[slot][g][rows, :] - m_new[g]).astype(bf16)
            out.append(jnp.max(cmax, axis=0, keepdims=True))
            if cur is not None:
                acc_ref[g] = (jnp.exp2(m_old[g] - m_new[g]) * acc_ref[g]
                              + jnp.dot(vt_ref[0, j, g], p_refs[slot][g][...], preferred_element_type=f32))
        return tuple(out)

    def tile_pair(i, tile_max):
        nxt_max = tile_step((2 * i + 1, 1), (2 * i, 0, tile_max))
        return tile_step((2 * i + 2, 0), (2 * i + 1, 1, nxt_max))

    last_max = lax.fori_loop(0, nk // 2, tile_pair, tile_step((0, 0), None))

    @pl.when(nk % 2 == 1)
    def _():
        tile_step(None, (nk - 1, 0, last_max))

    for g in range(KV_HEADS):
        o_t = acc_ref[g, 0:HEAD_DIM, :] / acc_ref[g, HEAD_DIM:HEAD_DIM + 1, :]
        for i in range(0, GROUPS, 2):
            pair = jnp.concatenate([o_t[:, i * TQ:(i + 1) * TQ], o_t[:, (i + 1) * TQ:(i + 2) * TQ]], axis=0)
            h = g * GROUPS + i
            o_ref[0, :, h * HEAD_DIM:(h + 2) * HEAD_DIM] = pair.T.astype(bf16)


def _dsa(qt, qit, wit, kb, kib, vt, P, top):
    B, nq = qt.shape[:2]
    R = qt.shape[-1]
    TQ = R // GROUPS
    NT, TK = kb.shape[1:3]
    assert top <= TK and TQ % CHUNK == 0
    tri = jnp.asarray(np.tril(np.ones((LANES, LANES), np.float32)), bf16)
    return pl.pallas_call(
        functools.partial(_dsa_kernel, TQ=TQ, TK=TK, P=P, top=top),
        grid=(B, nq),
        in_specs=[pl.BlockSpec((1, 1, KV_HEADS, HEAD_DIM, R), lambda b, i: (b, i, 0, 0, 0)),
                  pl.BlockSpec((1, 1, IDX_DIM, IDX_HEADS * TQ), lambda b, i: (b, i, 0, 0)),
                  pl.BlockSpec((1, 1, WI_ROWS, TQ), lambda b, i: (b, i, 0, 0)),
                  pl.BlockSpec((1, NT, TK, IDX_DIM), lambda b, i: (b, 0, 0, 0)),
                  pl.BlockSpec((1, NT, TK, KV_HEADS * HEAD_DIM), lambda b, i: (b, 0, 0, 0)),
                  pl.BlockSpec((1, NT, KV_HEADS, VT_ROWS, TK), lambda b, i: (b, 0, 0, 0, 0)),
                  _const_spec((LANES, LANES))],
        out_specs=pl.BlockSpec((1, TQ, N_HEADS * HEAD_DIM), lambda b, i: (b, i, 0)),
        out_shape=jax.ShapeDtypeStruct((B, nq * TQ, N_HEADS * HEAD_DIM), bf16),
        scratch_shapes=[pltpu.VMEM((NT, TK, TQ), i32),
                        pltpu.VMEM((-(-NT // PACK) * PACK, TK // PACK, TQ), i32),
                        pltpu.VMEM((KV_HEADS, KV_HEADS * HEAD_DIM, R), bf16),
                        pltpu.VMEM((KV_HEADS, 1, R), f32),
                        pltpu.VMEM((KV_HEADS, VT_ROWS, R), f32)]
        + [pltpu.VMEM((TK, R), f32)] * (2 * KV_HEADS)
        + [pltpu.VMEM((TK, R), bf16)] * (2 * KV_HEADS),
        compiler_params=_params(("parallel", "arbitrary")),
        name="dsa",
    )(qt, qit, wit, kib, kb, vt, tri)


def _hgrn_mats(T):
    r = np.arange(T)[:, None]
    c = np.arange(T)[None, :]
    mats = [c <= r]
    b = T // 2
    while b >= 2:
        blk_r = r // (2 * b)
        mid = blk_r * 2 * b + b
        same = blk_r == c // (2 * b)
        late = r >= mid
        mats.append(same & ((late & (c >= mid) & (c <= r)) | (~late & (c > r) & (c < mid))))
        b //= 2
    return np.concatenate(mats, axis=0).astype(np.float32)


def _hgrn_kernel(x_ref, w_ref, lb_ref, g_ref, mats_ref, s0_ref, y_ref, st_out_ref, st_ref, *, T, SUBS):
    t_idx = pl.program_id(1)

    @pl.when(t_idx == 0)
    def _():
        st_ref[...] = s0_ref[0]

    lb = lb_ref[...]
    mats = mats_ref[...]
    row = lax.broadcasted_iota(i32, (T, 1), 0)
    nt = (((1,), (1,)), ((), ()))
    tn = (((0,), (0,)), ((), ()))

    def prepare(i):
        xb = x_ref[0, i * T:(i + 1) * T, :].astype(bf16)
        proj = jnp.dot(xb, w_ref[...], preferred_element_type=f32)
        hq = proj[:, 0:HGRN_W]
        z = proj[:, HGRN_W:2 * HGRN_W]
        hv = proj[:, 2 * HGRN_W:3 * HGRN_W]
        hg = proj[:, 3 * HGRN_W:4 * HGRN_W]
        logf = (jnp.minimum(z, 0.0) - jnp.log1p(jnp.exp(-jnp.abs(z)))) + jnp.log1p(lb * jnp.exp(-z))
        kk = (1.0 - lb) * _sigmoid(-z)
        qq = hq * _sigmoid(hq)
        gate = hg * _sigmoid(hg)
        hi = logf.astype(bf16)
        r1 = logf - hi.astype(f32)
        mid = r1.astype(bf16)
        lo = (r1 - mid.astype(f32)).astype(bf16)
        e_all = jnp.dot(mats, jnp.concatenate([hi, mid, lo], axis=0),
                        preferred_element_type=f32)
        return qq, kk, hv, gate, logf, e_all

    def recur(i, prepared):
        qq, kk, hv, gate, logf, e_all = prepared
        lane2 = lax.broadcasted_iota(i32, (1, 2 * HGRN_DK), 1)
        col2 = lax.broadcasted_iota(i32, (1, 2 * T), 1) % T
        for hp in range(0, HGRN_HEADS, 2):
            sl2 = slice(hp * HGRN_DK, (hp + 2) * HGRN_DK)
            q2, k2 = qq[:, sl2], kk[:, sl2]
            scores2 = jnp.zeros((T, 2 * T), f32)
            b = T // 2
            lvl = 1
            while b >= 1:
                late = ((row // b) % 2) == 1
                if b >= 2:
                    xdec = jnp.exp(e_all[lvl * T:(lvl + 1) * T, sl2])
                else:
                    xdec = jnp.where(late, jnp.exp(logf[:, sl2]), 1.0)
                a_m = jnp.where(late, q2 * xdec, 0.0).astype(bf16)
                b_m = jnp.where(late, 0.0, k2 * xdec)
                b_diag = jnp.concatenate([jnp.where(lane2 < HGRN_DK, b_m, 0.0),
                                          jnp.where(lane2 >= HGRN_DK, b_m, 0.0)], axis=0).astype(bf16)
                sc = lax.dot_general(a_m, b_diag, nt, preferred_element_type=f32)
                if 2 * b < T:
                    sc = jnp.where((row // (2 * b)) == (col2 // (2 * b)), sc, 0.0)
                scores2 = scores2 + sc
                b //= 2
                lvl += 1
            for h in (hp, hp + 1):
                sl = slice(h * HGRN_DK, (h + 1) * HGRN_DK)
                q_h, k_h, v_h = qq[:, sl], kk[:, sl], hv[:, sl]
                vb = v_h.astype(bf16)
                cum = e_all[0:T, sl]
                suf = cum[T - 1:T, :] - cum
                st = st_ref[h]
                qd = (q_h * jnp.exp(cum)).astype(bf16)
                scores = scores2[:, (h - hp) * T:(h - hp + 1) * T]
                o = jnp.sum(q_h * k_h, axis=-1, keepdims=True) * v_h
                o = o + jnp.dot(jnp.concatenate([qd, scores.astype(bf16)], axis=1),
                                jnp.concatenate([st.T.astype(bf16), vb], axis=0), preferred_element_type=f32)
                kd = (k_h * jnp.exp(suf)).astype(bf16)
                st_ref[h] = (st * jnp.exp(cum[T - 1:T, :])
                             + lax.dot_general(vb, kd, tn, preferred_element_type=f32))
                o = o * lax.rsqrt(jnp.mean(o * o, axis=-1, keepdims=True) + LN_EPS) * g_ref[...]
                y_ref[0, i * T:(i + 1) * T, sl] = (o * gate[:, sl]).astype(bf16)

    nxt = prepare(0)
    for i in range(SUBS):
        cur = nxt
        if i + 1 < SUBS:
            nxt = prepare(i + 1)
        recur(i, cur)

    st_out_ref[0] = st_ref[...]


def _hgrn(x, w_h, lb, norm_g, s0):
    B, S, _ = x.shape
    T = _row_tile(S, 2 * CHUNK)
    SUBS = HGRN_SUBTILES if S % (T * HGRN_SUBTILES) == 0 else 1
    TS = T * SUBS
    mats = jnp.asarray(np.tile(_hgrn_mats(T), (1, 3)), bf16)
    nm = mats.shape[0]
    st0 = jnp.swapaxes(s0.astype(f32), 2, 3)
    y, st = pl.pallas_call(
        functools.partial(_hgrn_kernel, T=T, SUBS=SUBS),
        grid=(B, S // TS),
        in_specs=[pl.BlockSpec((1, TS, D_MODEL), lambda b, t: (b, t, 0)),
                  _const_spec((D_MODEL, 4 * HGRN_W)),
                  _const_spec((1, HGRN_W)),
                  _const_spec((1, HGRN_DV)),
                  _const_spec((nm, 3 * T)),
                  pl.BlockSpec((1, HGRN_HEADS, HGRN_DV, HGRN_DK), lambda b, t: (b, 0, 0, 0))],
        out_specs=[pl.BlockSpec((1, TS, HGRN_W), lambda b, t: (b, t, 0)),
                   pl.BlockSpec((1, HGRN_HEADS, HGRN_DV, HGRN_DK), lambda b, t: (b, 0, 0, 0))],
        out_shape=[jax.ShapeDtypeStruct((B, S, HGRN_W), bf16),
                   jax.ShapeDtypeStruct((B, HGRN_HEADS, HGRN_DV, HGRN_DK), f32)],
        scratch_shapes=[pltpu.VMEM((HGRN_HEADS, HGRN_DV, HGRN_DK), f32)],
        compiler_params=_params(("parallel", "arbitrary")),
        name="hgrn",
    )(x, w_h, lb, norm_g, mats, st0)
    return y, jnp.swapaxes(st, 2, 3)


HIST = CONV_W - 1
PADROWS = 8


def _conv_rows(buf_ref, cols, u, w, bias, T):
    buf_ref[PADROWS:PADROWS + T, cols] = u
    y = bias
    for j in range(CONV_W):
        y = y + w[j:j + 1, :] * buf_ref[PADROWS - HIST + j:PADROWS - HIST + j + T, cols]
    buf_ref[PADROWS - HIST:PADROWS, cols] = u[T - HIST:T, :]
    return y


def _merge_kernel(x_ref, ya_ref, yb_ref, hist_ref, wc_ref, wg_ref, wbr_ref, wout_ref, cw_ref, cb_ref,
                  g_ref, b_ref, o_ref, hist_out_ref, buf_ref, *, T, SUBS, alpha):
    @pl.when(pl.program_id(1) == 0)
    def _():
        buf_ref[PADROWS - HIST:PADROWS, :] = hist_ref[0]

    for i in range(SUBS):
        rows = slice(i * T, (i + 1) * T)
        x = x_ref[0, rows, :]
        xb = x.astype(bf16)
        c = jnp.dot(xb, wc_ref[...], preferred_element_type=f32)
        W = SCONV_WIDTH
        u = c[:, W:2 * W] * c[:, 2 * W:3 * W]
        u_conv = _conv_rows(buf_ref, slice(0, W), u, cw_ref[...], cb_ref[...], T)
        yc = (c[:, 0:W] * u_conv).astype(bf16)
        merged = None
        for n, yb in enumerate((ya_ref[0, rows, :], yb_ref[0, rows, :], yc)):
            gate = _sigmoid(jnp.dot(xb, wg_ref[:, n * D_MODEL:(n + 1) * D_MODEL], preferred_element_type=f32))
            term = gate * jnp.dot(yb, wbr_ref[n], preferred_element_type=f32)
            merged = term if merged is None else merged + term
        out = jnp.dot(merged.astype(bf16), wout_ref[...], preferred_element_type=f32)
        o_ref[0, rows, :] = _layer_norm(alpha * x + out, g_ref[...], b_ref[...])
    hist_out_ref[0] = buf_ref[PADROWS - HIST:PADROWS, :]


def _merge(x, ya, yb, hist, wc, wg, wbr, wout, cw, cb, g, b, alpha):
    B, S, _ = x.shape
    T = _row_tile(S, 256)
    SUBS = MERGE_SUBTILES if S % (T * MERGE_SUBTILES) == 0 else 1
    row = lambda w: pl.BlockSpec((1, T * SUBS, w), lambda bb, t: (bb, t, 0))
    hspec = pl.BlockSpec((1, HIST, SCONV_WIDTH), lambda bb, t: (bb, 0, 0))
    return pl.pallas_call(
        functools.partial(_merge_kernel, T=T, SUBS=SUBS, alpha=alpha),
        grid=(B, S // (T * SUBS)),
        in_specs=[row(D_MODEL), row(BRANCH_WIDTH), row(BRANCH_WIDTH), hspec,
                  _const_spec((D_MODEL, 3 * SCONV_WIDTH)), _const_spec((D_MODEL, N_BRANCH * D_MODEL)),
                  _const_spec((N_BRANCH, BRANCH_WIDTH, D_MODEL)), _const_spec((D_MODEL, D_MODEL)),
                  _const_spec((CONV_W, SCONV_WIDTH)), _const_spec((1, SCONV_WIDTH)),
                  _const_spec((1, D_MODEL)), _const_spec((1, D_MODEL))],
        out_specs=[row(D_MODEL), hspec],
        out_shape=[jax.ShapeDtypeStruct((B, S, D_MODEL), f32),
                   jax.ShapeDtypeStruct((B, HIST, SCONV_WIDTH), f32)],
        scratch_shapes=[pltpu.VMEM((PADROWS + T, SCONV_WIDTH), f32)],
        compiler_params=_params(("parallel", "arbitrary")),
        name="merge",
    )(x, ya, yb, hist, wc, wg, wbr, wout, cw, cb, g, b)


def _ffn_kernel(x_ref, hist_ref, wup_ref, cw_ref, cb_ref, wdn_ref, g_ref, b_ref, o_ref, hist_out_ref,
                *bufs, T, alpha):
    n_steps = D_FF // FFN_COLS
    col_ranges = [slice(off + c * FFN_COLS, off + (c + 1) * FFN_COLS)
                  for c in range(n_steps) for off in (0, D_FF)]

    @pl.when(pl.program_id(1) == 0)
    def _():
        for buf, cols in zip(bufs, col_ranges):
            buf[PADROWS - HIST:PADROWS, :] = hist_ref[0, :, cols]

    x = x_ref[0]
    xb = x.astype(bf16)

    def up_conv(cidx):
        halves = []
        for half in range(2):
            buf, cols = bufs[2 * cidx + half], col_ranges[2 * cidx + half]
            h = jnp.dot(xb, wup_ref[:, cols], preferred_element_type=f32)
            halves.append(_conv_rows(buf, slice(None), h, cw_ref[:, cols], cb_ref[:, cols], T))
            hist_out_ref[0, :, cols] = buf[PADROWS - HIST:PADROWS, :]
        return halves

    groups = [list(range(c, min(c + FFN_GROUP, n_steps))) for c in range(0, n_steps, FFN_GROUP)]
    acc = jnp.zeros((T, D_MODEL), f32)
    nxt = [up_conv(c) for c in groups[0]]
    for gi, grp in enumerate(groups):
        cur = nxt
        if gi + 1 < len(groups):
            nxt = [up_conv(c) for c in groups[gi + 1]]
        act = jnp.concatenate([(a_g * _sigmoid(a_g) * b_v).astype(bf16) for a_g, b_v in cur], axis=1)
        acc = acc + jnp.dot(act, wdn_ref[grp[0] * FFN_COLS:(grp[-1] + 1) * FFN_COLS, :],
                            preferred_element_type=f32)
    o_ref[0] = _layer_norm(alpha * x + acc, g_ref[...], b_ref[...])


def _ffn(x, hist, wup, cw, cb, wdn, g, b, alpha):
    B, S, _ = x.shape
    T = _row_tile(S, 256)
    row = pl.BlockSpec((1, T, D_MODEL), lambda bb, t: (bb, t, 0))
    hspec = pl.BlockSpec((1, HIST, 2 * D_FF), lambda bb, t: (bb, 0, 0))
    return pl.pallas_call(
        functools.partial(_ffn_kernel, T=T, alpha=alpha),
        grid=(B, S // T),
        in_specs=[row, hspec, _const_spec((D_MODEL, 2 * D_FF)), _const_spec((CONV_W, 2 * D_FF)),
                  _const_spec((1, 2 * D_FF)), _const_spec((D_FF, D_MODEL)),
                  _const_spec((1, D_MODEL)), _const_spec((1, D_MODEL))],
        out_specs=[row, hspec],
        out_shape=[jax.ShapeDtypeStruct((B, S, D_MODEL), f32),
                   jax.ShapeDtypeStruct((B, HIST, 2 * D_FF), f32)],
        scratch_shapes=[pltpu.VMEM((PADROWS + T, FFN_COLS), f32)] * (2 * D_FF // FFN_COLS),
        compiler_params=_params(("parallel", "arbitrary")),
        name="ffn",
    )(x, hist, wup, cw, cb, wdn, g, b)


def _cast_kernel(x_ref, o_ref):
    o_ref[...] = x_ref[...].astype(o_ref.dtype)


def _to_bf16(w):
    D, Rw, C = w.shape
    T = _row_tile(Rw, LANES)
    spec = pl.BlockSpec((1, T, C), lambda d, t: (d, t, 0))
    return pl.pallas_call(_cast_kernel, grid=(D, Rw // T), in_specs=[spec], out_specs=spec,
                          out_shape=jax.ShapeDtypeStruct(w.shape, bf16),
                          compiler_params=_params(("parallel", "parallel")), name="to_bf16")(w)


def _layer_weights(l, depth, w_in, hgrn_lb_logits, hgrn_norm_g, sconv_w, sconv_b, w_branch, w_out,
                   ln1_g, ln1_b, w_up, ffn_conv_w, ffn_conv_b, w_down, ln2_g, ln2_b):
    offs = np.concatenate([[0], np.cumsum(IN_SIZES)]).tolist()
    sec = lambda i, j=None: w_in[l][:, offs[i]:offs[(i if j is None else j) + 1]]
    a_q, a_k, a_v, i_q, i_k, i_w = (sec(i) for i in range(6))
    zpad = lambda n: jnp.zeros((D_MODEL, n), w_in.dtype)
    w_row = jnp.concatenate([a_k, i_k, zpad(LANES - IDX_DIM), a_v], axis=1).astype(bf16)
    w_t = jnp.concatenate([a_q, i_q, a_v, i_w, zpad(WT_ROWS - 900)], axis=1).T.astype(bf16)
    lbp = jax.nn.softmax(hgrn_lb_logits.astype(f32), axis=0)
    lb = (jnp.cumsum(lbp, axis=0) - lbp[0])[l].reshape(1, HGRN_W)
    row = lambda a: a.reshape(1, -1).astype(f32)
    return dict(
        w_row=w_row, w_t=w_t, w_h=sec(6, 9).astype(bf16), lb=lb, norm_g=row(hgrn_norm_g[l]),
        w_c=sec(10, 12).astype(bf16), w_g=sec(13).astype(bf16),
        w_br=w_branch[l].astype(bf16), w_out=w_out[l].astype(bf16),
        sconv_w=sconv_w[l].astype(f32), sconv_b=row(sconv_b[l]),
        ln1_g=row(ln1_g[l]), ln1_b=row(ln1_b[l]),
        w_up=w_up[l].astype(bf16), ffn_w=ffn_conv_w[l].astype(f32), ffn_b=row(ffn_conv_b[l]),
        w_down=w_down[l].astype(bf16), ln2_g=row(ln2_g[l]), ln2_b=row(ln2_b[l]))


def _key_tiles(a, TK):
    B, L, C = a.shape
    NT = -(-L // TK)
    a = jnp.pad(a, ((0, 0), (0, NT * TK - L), (0, 0)))
    return a.reshape(B, NT, TK, C)


def _trunk_layer(x, tables, P, k_past, v_past, ki_past, s0, sc_hist, ffn_hist, w, alpha):
    B, S, _ = x.shape
    TK = KEY_TILE
    a = _attn_proj(x, w["w_row"], w["w_t"], *tables)
    k, v, ki = a["k"], a["v"], a["ki"]
    L = P + S
    if k_past is None and S % TK == 0:
        kb = a["kb"].reshape(B, S // TK, TK, -1)
        kib = a["kib"].reshape(B, S // TK, TK, -1)
        vt = a["vt"]
    else:
        kb, kib, v_all = a["kb"], a["kib"], v.astype(bf16)
        if k_past is not None:
            kb = jnp.concatenate([k_past.reshape(B, P, -1).astype(bf16), kb], axis=1)
            kib = jnp.concatenate([ki_past.astype(bf16), kib], axis=1)
            v_all = jnp.concatenate([v_past.reshape(B, P, -1).astype(bf16), v_all], axis=1)
        kb, kib = _key_tiles(kb, TK), _key_tiles(kib, TK)
        vt = _key_tiles(v_all, TK).reshape(B, -1, TK, KV_HEADS, HEAD_DIM).transpose(0, 1, 3, 4, 2)
        ones = jnp.zeros(vt.shape[:3] + (VT_ROWS - HEAD_DIM, TK), bf16).at[:, :, :, 0, :].set(1.0)
        vt = jnp.concatenate([vt, ones], axis=3)
    y_a = _dsa(a["qt"], a["qit"], a["wit"], kb, kib, vt, P, min(TOPK_MAX, L // 4))
    y_b, s_new = _hgrn(x, w["w_h"], w["lb"], w["norm_g"], s0)
    x1, sc_new = _merge(x, y_a, y_b, sc_hist, w["w_c"], w["w_g"], w["w_br"], w["w_out"],
                        w["sconv_w"], w["sconv_b"], w["ln1_g"], w["ln1_b"], alpha)
    x2, ffn_new = _ffn(x1, ffn_hist, w["w_up"], w["ffn_w"], w["ffn_b"], w["w_down"],
                       w["ln2_g"], w["ln2_b"], alpha)
    new = (k.reshape(B, S, KV_HEADS, HEAD_DIM), v.reshape(B, S, KV_HEADS, HEAD_DIM), ki,
           s_new, sc_new, ffn_new)
    return x2, new


def kernel(x_prompt, x_sample, cache_attn_k, cache_attn_v, cache_idx_k, state_hgrn, state_sconv,
           state_ffn_conv, w_in, hgrn_lb_logits, hgrn_norm_g, sconv_w, sconv_b, w_branch, w_out,
           ln1_g, ln1_b, w_up, ffn_conv_w, ffn_conv_b, w_down, ln2_g, ln2_b):
    depth = w_in.shape[0]
    alpha = (2 * depth) ** 0.25
    B, S, _ = x_prompt.shape
    DB, DS, _ = x_sample.shape
    P = cache_attn_k.shape[2]
    cs_p = _rotary_tables(jnp.arange(S))
    cs_s = _rotary_tables(P + jnp.arange(DS))
    xp, xs = x_prompt, x_sample
    st_p = [[] for _ in range(6)]
    st_s = [[] for _ in range(6)]
    w_in_b = _to_bf16(w_in)
    for l in range(depth):
        w = _layer_weights(l, depth, w_in_b, hgrn_lb_logits, hgrn_norm_g, sconv_w, sconv_b, w_branch,
                           w_out, ln1_g, ln1_b, w_up, ffn_conv_w, ffn_conv_b, w_down, ln2_g, ln2_b)
        xp, new_p = _trunk_layer(
            xp, cs_p, 0, None, None, None,
            jnp.zeros((B, HGRN_HEADS, HGRN_DK, HGRN_DV), f32),
            jnp.zeros((B, HIST, SCONV_WIDTH), f32),
            jnp.zeros((B, HIST, 2 * D_FF), f32), w, alpha)
        xs, new_s = _trunk_layer(
            xs, cs_s, P, cache_attn_k[l], cache_attn_v[l], cache_idx_k[l], state_hgrn[l],
            state_sconv[l], state_ffn_conv[l], w, alpha)
        for j in range(6):
            st_p[j].append(new_p[j])
            st_s[j].append(new_s[j])
    outs_p = [jnp.stack(a, axis=0) for a in st_p]
    outs_s = [jnp.stack(a, axis=0) for a in st_s]
    return (xp, xs, *outs_p, *outs_s)
```

```python
import functools

import numpy as np
import jax
import jax.numpy as jnp
from jax import lax
from jax.experimental import pallas as pl
from jax.experimental.pallas import tpu as pltpu

D_MODEL = 1024
CHUNK = 64
N_HEADS = 8
HEAD_DIM = 64
KV_HEADS = 2
GROUPS = N_HEADS // KV_HEADS
IDX_HEADS = 4
IDX_DIM = 64
TOPK_MAX = 256
ROPE_THETA = 500000.0
ATTN_SCALE = HEAD_DIM ** -0.5
IDX_SCALE = (IDX_DIM ** -0.5) * (IDX_HEADS ** -0.5)
NEG = -1e30
HGRN_HEADS = 4
HGRN_DK = 128
HGRN_DV = 128
HGRN_W = HGRN_HEADS * HGRN_DK
SCONV_WIDTH = 512
CONV_W = 3
BRANCH_WIDTH = 512
N_BRANCH = 3
D_FF = 2816
LN_EPS = 1e-5
IN_SIZES = (N_HEADS * HEAD_DIM, KV_HEADS * HEAD_DIM, KV_HEADS * HEAD_DIM,
            IDX_HEADS * IDX_DIM, IDX_DIM, IDX_HEADS,
            HGRN_W, HGRN_W, HGRN_W, HGRN_W,
            SCONV_WIDTH, SCONV_WIDTH, SCONV_WIDTH,
            N_BRANCH * D_MODEL)

LANES = 128
VMEM_LIMIT = 56 * 1024 * 1024
KEY_TILE = 512
QUERY_TILE = 256
SCORE_UNROLL = 4
VT_ROWS = HEAD_DIM + 16
FFN_COLS = 256
FFN_GROUP = 5
MERGE_SUBTILES = 2
HGRN_SUBTILES = 4
PACK = 4
DIGITS = ((25, 7), (18, 7), (11, 7), (4, 7), (0, 4))
GUARD = np.int32(-0x7F7F7F80)
ONES = np.int32(0x01010101)

f32 = jnp.float32
bf16 = jnp.bfloat16
i32 = jnp.int32


def _row_tile(s, want):
    return want if s % want == 0 else s


def _const_spec(shape):
    nd = len(shape)
    return pl.BlockSpec(shape, lambda *_: (0,) * nd, pipeline_mode=pl.Buffered(1))


def _params(sem):
    return pltpu.CompilerParams(dimension_semantics=sem, vmem_limit_bytes=VMEM_LIMIT)


def _layer_norm(z, g, b):
    mu = jnp.mean(z, axis=-1, keepdims=True)
    d = z - mu
    var = jnp.mean(d * d, axis=-1, keepdims=True)
    return d * lax.rsqrt(var + LN_EPS) * g + b


def _sigmoid(x):
    return 1.0 / (1.0 + jnp.exp(-x))


Q_PRESCALE = ATTN_SCALE * float(np.log2(np.e))
ROT = HEAD_DIM // 4
WR_COLS = 384
WT_ROWS = 912
WI_ROWS = 8


def _attn_proj_kernel(x_ref, wr_ref, wt_ref, cs_ref, cst_ref, k_ref, v_ref, ki_ref, kb_ref, kib_ref,
                      vt_ref, qt_ref, qit_ref, wit_ref, *, T, TQ):
    xb = x_ref[0].astype(bf16)
    pr = jnp.dot(xb, wr_ref[...], preferred_element_type=f32)
    c, sa, sb = cs_ref[0], cs_ref[1], cs_ref[2]

    def rot(xg):
        return xg * c + pltpu.roll(xg, LANES - ROT // 2, 1) * sa + pltpu.roll(xg, ROT // 2, 1) * sb

    k = rot(pr[:, 0:LANES])
    ki = rot(pr[:, LANES:2 * LANES])[:, :IDX_DIM]
    k_ref[0] = k
    kb_ref[0] = k.astype(bf16)
    ki_ref[0] = ki
    kib_ref[0] = ki.astype(bf16)
    v_ref[0] = pr[:, 2 * LANES:3 * LANES]

    nt = (((1,), (1,)), ((), ()))
    pt = lax.dot_general(wt_ref[...], xb, nt, preferred_element_type=f32)
    ct, st = cst_ref[0], cst_ref[1]

    def rot_t(hb):
        lead = hb[0:ROT] * ct + jnp.concatenate([hb[ROT // 2:ROT], hb[0:ROT // 2]], axis=0) * st
        return jnp.concatenate([lead, hb[ROT:HEAD_DIM]], axis=0)

    nqb = T // TQ
    for h in range(N_HEADS):
        hb = (rot_t(pt[h * HEAD_DIM:(h + 1) * HEAD_DIM]) * Q_PRESCALE).astype(bf16)
        g, i = divmod(h, GROUPS)
        for n in range(nqb):
            qt_ref[0, n, g, :, i * TQ:(i + 1) * TQ] = hb[:, n * TQ:(n + 1) * TQ]
    base = N_HEADS * HEAD_DIM
    for h in range(IDX_HEADS):
        hb = rot_t(pt[base + h * IDX_DIM:base + (h + 1) * IDX_DIM]).astype(bf16)
        for n in range(nqb):
            qit_ref[0, n, :, h * TQ:(h + 1) * TQ] = hb[:, n * TQ:(n + 1) * TQ]
    base += IDX_HEADS * IDX_DIM
    ones_rows = jnp.where(lax.broadcasted_iota(i32, (VT_ROWS - HEAD_DIM, T), 0) == 0, 1.0, 0.0)
    for g in range(KV_HEADS):
        vt_ref[0, 0, g, 0:HEAD_DIM, :] = pt[base + g * HEAD_DIM:base + (g + 1) * HEAD_DIM].astype(bf16)
        vt_ref[0, 0, g, HEAD_DIM:VT_ROWS, :] = ones_rows.astype(bf16)
    base += KV_HEADS * HEAD_DIM
    for n in range(nqb):
        wit_ref[0, n] = pt[base:base + WI_ROWS, n * TQ:(n + 1) * TQ]


def _attn_proj(x, w_row, w_t, cs, cst):
    B, S, _ = x.shape
    T = _row_tile(S, KEY_TILE)
    TQ = _row_tile(S, QUERY_TILE)
    nqb = T // TQ
    R = GROUPS * TQ
    row = lambda w: pl.BlockSpec((1, T, w), lambda b, t: (b, t, 0))
    names = ("k", "v", "ki", "kb", "kib", "vt", "qt", "qit", "wit")
    outs = pl.pallas_call(
        functools.partial(_attn_proj_kernel, T=T, TQ=TQ),
        grid=(B, S // T),
        in_specs=[row(D_MODEL), _const_spec((D_MODEL, WR_COLS)), _const_spec((WT_ROWS, D_MODEL)),
                  pl.BlockSpec((3, T, LANES), lambda b, t: (0, t, 0)),
                  pl.BlockSpec((2, ROT, T), lambda b, t: (0, 0, t))],
        out_specs=[row(128), row(128), row(IDX_DIM), row(128), row(IDX_DIM),
                   pl.BlockSpec((1, 1, KV_HEADS, VT_ROWS, T), lambda b, t: (b, t, 0, 0, 0)),
                   pl.BlockSpec((1, nqb, KV_HEADS, HEAD_DIM, R), lambda b, t: (b, t, 0, 0, 0)),
                   pl.BlockSpec((1, nqb, IDX_DIM, IDX_HEADS * TQ), lambda b, t: (b, t, 0, 0)),
                   pl.BlockSpec((1, nqb, WI_ROWS, TQ), lambda b, t: (b, t, 0, 0))],
        out_shape=[jax.ShapeDtypeStruct((B, S, 128), f32),
                   jax.ShapeDtypeStruct((B, S, 128), f32),
                   jax.ShapeDtypeStruct((B, S, IDX_DIM), f32),
                   jax.ShapeDtypeStruct((B, S, 128), bf16),
                   jax.ShapeDtypeStruct((B, S, IDX_DIM), bf16),
                   jax.ShapeDtypeStruct((B, S // T, KV_HEADS, VT_ROWS, T), bf16),
                   jax.ShapeDtypeStruct((B, S // TQ, KV_HEADS, HEAD_DIM, R), bf16),
                   jax.ShapeDtypeStruct((B, S // TQ, IDX_DIM, IDX_HEADS * TQ), bf16),
                   jax.ShapeDtypeStruct((B, S // TQ, WI_ROWS, TQ), f32)],
        compiler_params=_params(("parallel", "parallel")),
        name="attn_proj",
    )(x, w_row, w_t, cs, cst)
    return dict(zip(names, outs))


def _rotary_tables(pos):
    half = ROT // 2
    inv_freq = ROPE_THETA ** (-jnp.arange(half, dtype=f32) / half)
    ang = pos.astype(f32)[:, None] * inv_freq[None, :]
    cos = jnp.cos(ang)
    sin = jnp.sin(ang)
    S = pos.shape[0]
    one = jnp.ones((S, HEAD_DIM - ROT), f32)
    zero = jnp.zeros((S, HEAD_DIM - ROT), f32)
    zh = jnp.zeros((S, half), f32)
    c = jnp.concatenate([cos, cos, one], axis=1)
    sa = jnp.concatenate([-sin, zh, zero], axis=1)
    sb = jnp.concatenate([zh, sin, zero], axis=1)
    tile2 = lambda a: jnp.concatenate([a, a], axis=1)
    cs = jnp.stack([tile2(c), tile2(sa), tile2(sb)], axis=0)
    cst = jnp.stack([jnp.concatenate([cos, cos], axis=1).T,
                     jnp.concatenate([-sin, sin], axis=1).T], axis=0)
    return cs, cst


def _dsa_kernel(qt_ref, qit_ref, wit_ref, ki_ref, k_ref, vt_ref, tri_ref, o_ref,
                s_ref, pk_ref, qpad_ref, m_ref, acc_ref, *bufs, TQ, TK, P, top):
    lm_refs = (bufs[0:KV_HEADS], bufs[KV_HEADS:2 * KV_HEADS])
    p_refs = (bufs[2 * KV_HEADS:3 * KV_HEADS], bufs[3 * KV_HEADS:4 * KV_HEADS])
    qb = pl.program_id(1)
    q0 = P + qb * TQ
    nk = (q0 + TQ + TK - 1) // TK
    n_full = q0 // TK
    NC = TK // LANES
    R = GROUPS * TQ
    q_lim = ((q0 + lax.broadcasted_iota(i32, (1, TQ), 1)) // CHUNK + 1) * CHUNK
    wit = wit_ref[0, 0] * IDX_SCALE
    zero_head = jnp.zeros((HEAD_DIM, R), bf16)
    qpad_ref[0] = jnp.concatenate([qt_ref[0, 0, 0], zero_head], axis=0)
    qpad_ref[1] = jnp.concatenate([zero_head, qt_ref[0, 0, 1]], axis=0)

    TKP = TK // PACK

    def lead_digit(key):
        return (key >> DIGITS[0][0]) + (1 << (DIGITS[0][1] - 1))

    def score_tile(masked, j, carry):
        for c in range(NC):
            rows = slice(c * LANES, (c + 1) * LANES)
            rel = jnp.maximum(jnp.dot(ki_ref[0, j, rows, :], qit_ref[0, 0], preferred_element_type=f32), 0.0)
            score = wit[0:1, :] * rel[:, 0:TQ]
            for h in range(1, IDX_HEADS):
                score = score + wit[h:h + 1, :] * rel[:, h * TQ:(h + 1) * TQ]
            if masked:
                k_pos = j * TK + c * LANES + lax.broadcasted_iota(i32, (LANES, TQ), 0)
                score = jnp.where(k_pos < q_lim, score, NEG)
            bits = pltpu.bitcast(score, i32)
            key = jnp.where(bits >= 0, bits, bits ^ 0x7FFFFFFF)
            s_ref[j, rows, :] = key
            if c == 0:
                pk_ref[j] = lead_digit(key) | GUARD
            else:
                pk_ref[j] = pk_ref[j] | (lead_digit(key) << (8 * c))
        return carry

    def score_group(i, carry):
        for u in range(SCORE_UNROLL):
            carry = score_tile(False, SCORE_UNROLL * i + u, carry)
        return carry

    lax.fori_loop(0, n_full // SCORE_UNROLL, score_group, 0)
    lax.fori_loop(n_full // SCORE_UNROLL * SCORE_UNROLL, n_full, functools.partial(score_tile, False), 0)
    lax.fori_loop(n_full, nk, functools.partial(score_tile, True), 0)

    n_chunks = (nk + PACK - 1) // PACK

    def fill_guard(j, carry):
        pk_ref[j] = jnp.full((TKP, TQ), GUARD, i32)
        return carry

    lax.fori_loop(nk, n_chunks * PACK, fill_guard, 0)

    def count_ge(cand):
        cvec = cand * ONES

        def chunk(ci, tot):
            accs = [None] * 4
            for tt in range(PACK):
                for r in range(TKP // 8):
                    slab = pk_ref[ci * PACK + tt, r * 8:(r + 1) * 8, :]
                    hit = lax.shift_right_logical(slab - cvec, 7) & ONES
                    n = (tt * (TKP // 8) + r) % 4
                    accs[n] = hit if accs[n] is None else accs[n] + hit
            a = (accs[0] + accs[1]) + (accs[2] + accs[3])
            return tot + ((a & 0xFF) + ((a >> 8) & 0xFF) + ((a >> 16) & 0xFF)
                          + lax.shift_right_logical(a, 24))

        tot = lax.fori_loop(0, n_chunks, chunk, jnp.zeros((8, TQ), i32))
        return jnp.sum(tot, axis=0, keepdims=True)

    def repack(lo, width, prefix):
        origin = prefix * (1 << width)

        def tile(j, carry):
            for r in range(TKP // 8):
                word = jnp.full((8, TQ), GUARD, i32)
                for q in range(PACK):
                    key = s_ref[j, q * TKP + r * 8:q * TKP + (r + 1) * 8, :]
                    d = (key >> lo) - origin
                    in_range = pltpu.bitcast(d, jnp.uint32) < jnp.uint32(1 << width)
                    word = word | (jnp.where(in_range, d, 0) << (8 * q))
                pk_ref[j, r * 8:(r + 1) * 8, :] = word
            return carry
        lax.fori_loop(0, nk, tile, 0)

    base = jnp.zeros((1, TQ), i32)
    bsize = jnp.full((1, TQ), nk * TK, i32)
    prefix = None
    zero = jnp.zeros((1, TQ), i32)
    for stage, (lo, width) in enumerate(DIGITS):
        need = top - base

        def search_digit(stage=stage, lo=lo, width=width, prefix=prefix, need=need, bsize=bsize):
            if stage > 0:
                repack(lo, width, prefix)

            def bit_step(i, st):
                v, c_v, c_rej = st
                cand = v | jnp.left_shift(jnp.int32(1), width - 1 - i)
                c = count_ge(cand)
                ok = c >= need
                return jnp.where(ok, cand, v), jnp.where(ok, c, c_v), jnp.where(ok, c_rej, c)

            return lax.fori_loop(0, width, bit_step, (zero, bsize, zero))

        v, c_v, c_gt = search_digit()
        base, bsize = base + c_gt, c_v - c_gt
        prefix = v - (1 << (width - 1)) if stage == 0 else prefix * (1 << width) + v
    t = prefix
    neg_key = np.array(NEG, np.float32).view(np.int32) ^ 0x7FFFFFFF
    any_excess = jnp.max(jnp.where(base + bsize > top, 1, 0)) > 0

    @pl.when(jnp.logical_not(any_excess))
    def _():
        thr = jnp.where(t == neg_key, t + 1, t)

        def fast(j, carry):
            s_ref[j] = pltpu.bitcast(jnp.where(s_ref[j] >= thr, 0.0, NEG), i32)
            return carry
        lax.fori_loop(0, nk, fast, 0)

    @pl.when(any_excess)
    def _():
        need = jnp.where(t == neg_key, 0, top - base).astype(f32)

        def ranked(j, taken):
            keys = s_ref[j]
            for c in range(NC):
                kc = keys[c * LANES:(c + 1) * LANES, :]
                eq = jnp.where(kc == t, 1.0, 0.0)
                rank = taken + jnp.dot(tri_ref[...], eq.astype(bf16), preferred_element_type=f32)
                pick = jnp.where(kc > t, 1.0, jnp.where(rank <= need, eq, 0.0))
                s_ref[j, c * LANES:(c + 1) * LANES, :] = pltpu.bitcast(
                    jnp.where(pick > 0.5, 0.0, NEG), i32)
                taken = taken + jnp.sum(eq, axis=0, keepdims=True)
            return taken
        lax.fori_loop(0, nk, ranked, jnp.zeros((1, TQ), f32))

    m_ref[...] = jnp.full(m_ref.shape, NEG, f32)
    acc_ref[...] = jnp.zeros(acc_ref.shape, f32)

    def tile_step(nxt, cur):
        if cur is not None:
            j, slot, tile_max = cur
            m_old = [m_ref[g] for g in range(KV_HEADS)]
            m_new = [jnp.maximum(m_old[g], tile_max[g]) for g in range(KV_HEADS)]
            for g in range(KV_HEADS):
                m_ref[g] = m_new[g]
        if nxt is not None:
            jn, slot_n = nxt
            jc = jnp.minimum(jn, nk - 1)
        out = []
        for g in range(KV_HEADS):
            cmax = jnp.full((8, R), NEG, f32)
            if nxt is not None:
                raw = jnp.dot(k_ref[0, jc], qpad_ref[g], preferred_element_type=f32)
            for c in range(NC):
                rows = slice(c * LANES, (c + 1) * LANES)
                if nxt is not None:
                    bias = pltpu.bitcast(s_ref[jc, rows, :], f32)
                    lm = raw[rows, :] + jnp.concatenate([bias] * GROUPS, axis=1)
                    lm_refs[slot_n][g][rows, :] = lm
                    cmax = jnp.maximum(cmax, jnp.max(lm.reshape(LANES // 8, 8, R), axis=0))
                if cur is not None:
                    p_refs[slot][g][rows, :] = jnp.exp2(lm_refs[slot][g][rows, :] - m_new[g]).astype(bf16)
            out.append(jnp.max(cmax, axis=0, keepdims=True))
            if cur is not None:
                acc_ref[g] = (jnp.exp2(m_old[g] - m_new[g]) * acc_ref[g]
                              + jnp.dot(vt_ref[0, j, g], p_refs[slot][g][...], preferred_element_type=f32))
        return tuple(out)

    def tile_pair(i, tile_max):
        nxt_max = tile_step((2 * i + 1, 1), (2 * i, 0, tile_max))
        return tile_step((2 * i + 2, 0), (2 * i + 1, 1, nxt_max))

    last_max = lax.fori_loop(0, nk // 2, tile_pair, tile_step((0, 0), None))

    @pl.when(nk % 2 == 1)
    def _():
        tile_step(None, (nk - 1, 0, last_max))

    for g in range(KV_HEADS):
        o_t = acc_ref[g, 0:HEAD_DIM, :] / acc_ref[g, HEAD_DIM:HEAD_DIM + 1, :]
        for i in range(0, GROUPS, 2):
            pair = jnp.concatenate([o_t[:, i * TQ:(i + 1) * TQ], o_t[:, (i + 1) * TQ:(i + 2) * TQ]], axis=0)
            h = g * GROUPS + i
            o_ref[0, :, h * HEAD_DIM:(h + 2) * HEAD_DIM] = pair.T.astype(bf16)


def _dsa(qt, qit, wit, kb, kib, vt, P, top):
    B, nq = qt.shape[:2]
    R = qt.shape[-1]
    TQ = R // GROUPS
    NT, TK = kb.shape[1:3]
    assert top <= TK and TQ % CHUNK == 0
    tri = jnp.asarray(np.tril(np.ones((LANES, LANES), np.float32)), bf16)
    return pl.pallas_call(
        functools.partial(_dsa_kernel, TQ=TQ, TK=TK, P=P, top=top),
        grid=(B, nq),
        in_specs=[pl.BlockSpec((1, 1, KV_HEADS, HEAD_DIM, R), lambda b, i: (b, i, 0, 0, 0)),
                  pl.BlockSpec((1, 1, IDX_DIM, IDX_HEADS * TQ), lambda b, i: (b, i, 0, 0)),
                  pl.BlockSpec((1, 1, WI_ROWS, TQ), lambda b, i: (b, i, 0, 0)),
                  pl.BlockSpec((1, NT, TK, IDX_DIM), lambda b, i: (b, 0, 0, 0)),
                  pl.BlockSpec((1, NT, TK, KV_HEADS * HEAD_DIM), lambda b, i: (b, 0, 0, 0)),
                  pl.BlockSpec((1, NT, KV_HEADS, VT_ROWS, TK), lambda b, i: (b, 0, 0, 0, 0)),
                  _const_spec((LANES, LANES))],
        out_specs=pl.BlockSpec((1, TQ, N_HEADS * HEAD_DIM), lambda b, i: (b, i, 0)),
        out_shape=jax.ShapeDtypeStruct((B, nq * TQ, N_HEADS * HEAD_DIM), bf16),
        scratch_shapes=[pltpu.VMEM((NT, TK, TQ), i32),
                        pltpu.VMEM((-(-NT // PACK) * PACK, TK // PACK, TQ), i32),
                        pltpu.VMEM((KV_HEADS, KV_HEADS * HEAD_DIM, R), bf16),
                        pltpu.VMEM((KV_HEADS, 1, R), f32),
                        pltpu.VMEM((KV_HEADS, VT_ROWS, R), f32)]
        + [pltpu.VMEM((TK, R), f32)] * (2 * KV_HEADS)
        + [pltpu.VMEM((TK, R), bf16)] * (2 * KV_HEADS),
        compiler_params=_params(("parallel", "arbitrary")),
        name="dsa",
    )(qt, qit, wit, kib, kb, vt, tri)


def _hgrn_mats(T):
    r = np.arange(T)[:, None]
    c = np.arange(T)[None, :]
    mats = [c <= r]
    b = T // 2
    while b >= 2:
        blk_r = r // (2 * b)
        mid = blk_r * 2 * b + b
        same = blk_r == c // (2 * b)
        late = r >= mid
        mats.append(same & ((late & (c >= mid) & (c <= r)) | (~late & (c > r) & (c < mid))))
        b //= 2
    return np.concatenate(mats, axis=0).astype(np.float32)


def _hgrn_kernel(x_ref, w_ref, lb_ref, g_ref, mats_ref, s0_ref, y_ref, st_out_ref, st_ref, *, T, SUBS):
    t_idx = pl.program_id(1)

    @pl.when(t_idx == 0)
    def _():
        st_ref[...] = s0_ref[0]

    lb = lb_ref[...]
    mats = mats_ref[...]
    row = lax.broadcasted_iota(i32, (T, 1), 0)
    nt = (((1,), (1,)), ((), ()))
    tn = (((0,), (0,)), ((), ()))

    def prepare(i):
        xb = x_ref[0, i * T:(i + 1) * T, :].astype(bf16)
        proj = jnp.dot(xb, w_ref[...], preferred_element_type=f32)
        hq = proj[:, 0:HGRN_W]
        z = proj[:, HGRN_W:2 * HGRN_W]
        hv = proj[:, 2 * HGRN_W:3 * HGRN_W]
        hg = proj[:, 3 * HGRN_W:4 * HGRN_W]
        logf = (jnp.minimum(z, 0.0) - jnp.log1p(jnp.exp(-jnp.abs(z)))) + jnp.log1p(lb * jnp.exp(-z))
        kk = (1.0 - lb) * _sigmoid(-z)
        qq = hq * _sigmoid(hq)
        gate = hg * _sigmoid(hg)
        hi = logf.astype(bf16)
        r1 = logf - hi.astype(f32)
        mid = r1.astype(bf16)
        lo = (r1 - mid.astype(f32)).astype(bf16)
        e_all = jnp.dot(mats, jnp.concatenate([hi, mid, lo], axis=0),
                        preferred_element_type=f32)
        return qq, kk, hv, gate, logf, e_all

    def recur(i, prepared):
        qq, kk, hv, gate, logf, e_all = prepared
        lane2 = lax.broadcasted_iota(i32, (1, 2 * HGRN_DK), 1)
        col2 = lax.broadcasted_iota(i32, (1, 2 * T), 1) % T
        for hp in range(0, HGRN_HEADS, 2):
            sl2 = slice(hp * HGRN_DK, (hp + 2) * HGRN_DK)
            q2, k2 = qq[:, sl2], kk[:, sl2]
            scores2 = jnp.zeros((T, 2 * T), f32)
            b = T // 2
            lvl = 1
            while b >= 1:
                late = ((row // b) % 2) == 1
                if b >= 2:
                    xdec = jnp.exp(e_all[lvl * T:(lvl + 1) * T, sl2])
                else:
                    xdec = jnp.where(late, jnp.exp(logf[:, sl2]), 1.0)
                a_m = jnp.where(late, q2 * xdec, 0.0).astype(bf16)
                b_m = jnp.where(late, 0.0, k2 * xdec)
                b_diag = jnp.concatenate([jnp.where(lane2 < HGRN_DK, b_m, 0.0),
                                          jnp.where(lane2 >= HGRN_DK, b_m, 0.0)], axis=0).astype(bf16)
                sc = lax.dot_general(a_m, b_diag, nt, preferred_element_type=f32)
                if 2 * b < T:
                    sc = jnp.where((row // (2 * b)) == (col2 // (2 * b)), sc, 0.0)
                scores2 = scores2 + sc
                b //= 2
                lvl += 1
            for h in (hp, hp + 1):
                sl = slice(h * HGRN_DK, (h + 1) * HGRN_DK)
                q_h, k_h, v_h = qq[:, sl], kk[:, sl], hv[:, sl]
                vb = v_h.astype(bf16)
                cum = e_all[0:T, sl]
                suf = cum[T - 1:T, :] - cum
                st = st_ref[h]
                qd = (q_h * jnp.exp(cum)).astype(bf16)
                scores = scores2[:, (h - hp) * T:(h - hp + 1) * T]
                o = jnp.sum(q_h * k_h, axis=-1, keepdims=True) * v_h
                o = o + jnp.dot(jnp.concatenate([qd, scores.astype(bf16)], axis=1),
                                jnp.concatenate([st.T.astype(bf16), vb], axis=0), preferred_element_type=f32)
                kd = (k_h * jnp.exp(suf)).astype(bf16)
                st_ref[h] = (st * jnp.exp(cum[T - 1:T, :])
                             + lax.dot_general(vb, kd, tn, preferred_element_type=f32))
                o = o * lax.rsqrt(jnp.mean(o * o, axis=-1, keepdims=True) + LN_EPS) * g_ref[...]
                y_ref[0, i * T:(i + 1) * T, sl] = (o * gate[:, sl]).astype(bf16)

    nxt = prepare(0)
    for i in range(SUBS):
        cur = nxt
        if i + 1 < SUBS:
            nxt = prepare(i + 1)
        recur(i, cur)

    st_out_ref[0] = st_ref[...]


def _hgrn(x, w_h, lb, norm_g, s0):
    B, S, _ = x.shape
    T = _row_tile(S, 2 * CHUNK)
    SUBS = HGRN_SUBTILES if S % (T * HGRN_SUBTILES) == 0 else 1
    TS = T * SUBS
    mats = jnp.asarray(np.tile(_hgrn_mats(T), (1, 3)), bf16)
    nm = mats.shape[0]
    st0 = jnp.swapaxes(s0.astype(f32), 2, 3)
    y, st = pl.pallas_call(
        functools.partial(_hgrn_kernel, T=T, SUBS=SUBS),
        grid=(B, S // TS),
        in_specs=[pl.BlockSpec((1, TS, D_MODEL), lambda b, t: (b, t, 0)),
                  _const_spec((D_MODEL, 4 * HGRN_W)),
                  _const_spec((1, HGRN_W)),
                  _const_spec((1, HGRN_DV)),
                  _const_spec((nm, 3 * T)),
                  pl.BlockSpec((1, HGRN_HEADS, HGRN_DV, HGRN_DK), lambda b, t: (b, 0, 0, 0))],
        out_specs=[pl.BlockSpec((1, TS, HGRN_W), lambda b, t: (b, t, 0)),
                   pl.BlockSpec((1, HGRN_HEADS, HGRN_DV, HGRN_DK), lambda b, t: (b, 0, 0, 0))],
        out_shape=[jax.ShapeDtypeStruct((B, S, HGRN_W), bf16),
                   jax.ShapeDtypeStruct((B, HGRN_HEADS, HGRN_DV, HGRN_DK), f32)],
        scratch_shapes=[pltpu.VMEM((HGRN_HEADS, HGRN_DV, HGRN_DK), f32)],
        compiler_params=_params(("parallel", "arbitrary")),
        name="hgrn",
    )(x, w_h, lb, norm_g, mats, st0)
    return y, jnp.swapaxes(st, 2, 3)


HIST = CONV_W - 1
PADROWS = 8


def _conv_rows(buf_ref, cols, u, w, bias, T):
    buf_ref[PADROWS:PADROWS + T, cols] = u
    y = bias
    for j in range(CONV_W):
        y = y + w[j:j + 1, :] * buf_ref[PADROWS - HIST + j:PADROWS - HIST + j + T, cols]
    buf_ref[PADROWS - HIST:PADROWS, cols] = u[T - HIST:T, :]
    return y


def _merge_kernel(x_ref, ya_ref, yb_ref, hist_ref, wc_ref, wg_ref, wbr_ref, wout_ref, cw_ref, cb_ref,
                  g_ref, b_ref, o_ref, hist_out_ref, buf_ref, *, T, SUBS, alpha):
    @pl.when(pl.program_id(1) == 0)
    def _():
        buf_ref[PADROWS - HIST:PADROWS, :] = hist_ref[0]

    for i in range(SUBS):
        rows = slice(i * T, (i + 1) * T)
        x = x_ref[0, rows, :]
        xb = x.astype(bf16)
        c = jnp.dot(xb, wc_ref[...], preferred_element_type=f32)
        W = SCONV_WIDTH
        u = c[:, W:2 * W] * c[:, 2 * W:3 * W]
        u_conv = _conv_rows(buf_ref, slice(0, W), u, cw_ref[...], cb_ref[...], T)
        yc = (c[:, 0:W] * u_conv).astype(bf16)
        merged = None
        for n, yb in enumerate((ya_ref[0, rows, :], yb_ref[0, rows, :], yc)):
            gate = _sigmoid(jnp.dot(xb, wg_ref[:, n * D_MODEL:(n + 1) * D_MODEL], preferred_element_type=f32))
            term = gate * jnp.dot(yb, wbr_ref[n], preferred_element_type=f32)
            merged = term if merged is None else merged + term
        out = jnp.dot(merged.astype(bf16), wout_ref[...], preferred_element_type=f32)
        o_ref[0, rows, :] = _layer_norm(alpha * x + out, g_ref[...], b_ref[...])
    hist_out_ref[0] = buf_ref[PADROWS - HIST:PADROWS, :]


def _merge(x, ya, yb, hist, wc, wg, wbr, wout, cw, cb, g, b, alpha):
    B, S, _ = x.shape
    T = _row_tile(S, 256)
    SUBS = MERGE_SUBTILES if S % (T * MERGE_SUBTILES) == 0 else 1
    row = lambda w: pl.BlockSpec((1, T * SUBS, w), lambda bb, t: (bb, t, 0))
    hspec = pl.BlockSpec((1, HIST, SCONV_WIDTH), lambda bb, t: (bb, 0, 0))
    return pl.pallas_call(
        functools.partial(_merge_kernel, T=T, SUBS=SUBS, alpha=alpha),
        grid=(B, S // (T * SUBS)),
        in_specs=[row(D_MODEL), row(BRANCH_WIDTH), row(BRANCH_WIDTH), hspec,
                  _const_spec((D_MODEL, 3 * SCONV_WIDTH)), _const_spec((D_MODEL, N_BRANCH * D_MODEL)),
                  _const_spec((N_BRANCH, BRANCH_WIDTH, D_MODEL)), _const_spec((D_MODEL, D_MODEL)),
                  _const_spec((CONV_W, SCONV_WIDTH)), _const_spec((1, SCONV_WIDTH)),
                  _const_spec((1, D_MODEL)), _const_spec((1, D_MODEL))],
        out_specs=[row(D_MODEL), hspec],
        out_shape=[jax.ShapeDtypeStruct((B, S, D_MODEL), f32),
                   jax.ShapeDtypeStruct((B, HIST, SCONV_WIDTH), f32)],
        scratch_shapes=[pltpu.VMEM((PADROWS + T, SCONV_WIDTH), f32)],
        compiler_params=_params(("parallel", "arbitrary")),
        name="merge",
    )(x, ya, yb, hist, wc, wg, wbr, wout, cw, cb, g, b)


def _ffn_kernel(x_ref, hist_ref, wup_ref, cw_ref, cb_ref, wdn_ref, g_ref, b_ref, o_ref, hist_out_ref,
                *bufs, T, alpha):
    n_steps = D_FF // FFN_COLS
    col_ranges = [slice(off + c * FFN_COLS, off + (c + 1) * FFN_COLS)
                  for c in range(n_steps) for off in (0, D_FF)]

    @pl.when(pl.program_id(1) == 0)
    def _():
        for buf, cols in zip(bufs, col_ranges):
            buf[PADROWS - HIST:PADROWS, :] = hist_ref[0, :, cols]

    x = x_ref[0]
    xb = x.astype(bf16)

    def up_conv(cidx):
        halves = []
        for half in range(2):
            buf, cols = bufs[2 * cidx + half], col_ranges[2 * cidx + half]
            h = jnp.dot(xb, wup_ref[:, cols], preferred_element_type=f32)
            halves.append(_conv_rows(buf, slice(None), h, cw_ref[:, cols], cb_ref[:, cols], T))
            hist_out_ref[0, :, cols] = buf[PADROWS - HIST:PADROWS, :]
        return halves

    groups = [list(range(c, min(c + FFN_GROUP, n_steps))) for c in range(0, n_steps, FFN_GROUP)]
    acc = jnp.zeros((T, D_MODEL), f32)
    nxt = [up_conv(c) for c in groups[0]]
    for gi, grp in enumerate(groups):
        cur = nxt
        if gi + 1 < len(groups):
            nxt = [up_conv(c) for c in groups[gi + 1]]
        act = jnp.concatenate([(a_g * _sigmoid(a_g) * b_v).astype(bf16) for a_g, b_v in cur], axis=1)
        acc = acc + jnp.dot(act, wdn_ref[grp[0] * FFN_COLS:(grp[-1] + 1) * FFN_COLS, :],
                            preferred_element_type=f32)
    o_ref[0] = _layer_norm(alpha * x + acc, g_ref[...], b_ref[...])


def _ffn(x, hist, wup, cw, cb, wdn, g, b, alpha):
    B, S, _ = x.shape
    T = _row_tile(S, 256)
    row = pl.BlockSpec((1, T, D_MODEL), lambda bb, t: (bb, t, 0))
    hspec = pl.BlockSpec((1, HIST, 2 * D_FF), lambda bb, t: (bb, 0, 0))
    return pl.pallas_call(
        functools.partial(_ffn_kernel, T=T, alpha=alpha),
        grid=(B, S // T),
        in_specs=[row, hspec, _const_spec((D_MODEL, 2 * D_FF)), _const_spec((CONV_W, 2 * D_FF)),
                  _const_spec((1, 2 * D_FF)), _const_spec((D_FF, D_MODEL)),
                  _const_spec((1, D_MODEL)), _const_spec((1, D_MODEL))],
        out_specs=[row, hspec],
        out_shape=[jax.ShapeDtypeStruct((B, S, D_MODEL), f32),
                   jax.ShapeDtypeStruct((B, HIST, 2 * D_FF), f32)],
        scratch_shapes=[pltpu.VMEM((PADROWS + T, FFN_COLS), f32)] * (2 * D_FF // FFN_COLS),
        compiler_params=_params(("parallel", "arbitrary")),
        name="ffn",
    )(x, hist, wup, cw, cb, wdn, g, b)


def _merge_ffn_kernel(x_ref, ya_ref, yb_ref, hist_ref, wc_ref, wg_ref, wbr_ref, wout_ref, cw_ref, cb_ref,
                      g1_ref, b1_ref, fhist_ref, wup_ref, fcw_ref, fcb_ref, wdn_ref, g2_ref, b2_ref,
                      o_ref, hist_out_ref, fhist_out_ref, buf_ref, *bufs, T, alpha):
    n_steps = D_FF // FFN_COLS
    col_ranges = [slice(off + c * FFN_COLS, off + (c + 1) * FFN_COLS)
                  for c in range(n_steps) for off in (0, D_FF)]

    @pl.when(pl.program_id(1) == 0)
    def _():
        buf_ref[PADROWS - HIST:PADROWS, :] = hist_ref[0]
        for buf, cols in zip(bufs, col_ranges):
            buf[PADROWS - HIST:PADROWS, :] = fhist_ref[0, :, cols]

    x = x_ref[0]
    xb = x.astype(bf16)
    c = jnp.dot(xb, wc_ref[...], preferred_element_type=f32)
    W = SCONV_WIDTH
    u = c[:, W:2 * W] * c[:, 2 * W:3 * W]
    u_conv = _conv_rows(buf_ref, slice(0, W), u, cw_ref[...], cb_ref[...], T)
    yc = (c[:, 0:W] * u_conv).astype(bf16)
    hist_out_ref[0] = buf_ref[PADROWS - HIST:PADROWS, :]
    merged = None
    for n, yb in enumerate((ya_ref[0], yb_ref[0], yc)):
        gate = _sigmoid(jnp.dot(xb, wg_ref[:, n * D_MODEL:(n + 1) * D_MODEL], preferred_element_type=f32))
        term = gate * jnp.dot(yb, wbr_ref[n], preferred_element_type=f32)
        merged = term if merged is None else merged + term
    out = jnp.dot(merged.astype(bf16), wout_ref[...], preferred_element_type=f32)
    x1 = _layer_norm(alpha * x + out, g1_ref[...], b1_ref[...])
    x1b = x1.astype(bf16)

    def up_conv(cidx):
        halves = []
        for half in range(2):
            buf, cols = bufs[2 * cidx + half], col_ranges[2 * cidx + half]
            h = jnp.dot(x1b, wup_ref[:, cols], preferred_element_type=f32)
            halves.append(_conv_rows(buf, slice(None), h, fcw_ref[:, cols], fcb_ref[:, cols], T))
            fhist_out_ref[0, :, cols] = buf[PADROWS - HIST:PADROWS, :]
        return halves

    groups = [list(range(c0, min(c0 + FFN_GROUP, n_steps))) for c0 in range(0, n_steps, FFN_GROUP)]
    acc = jnp.zeros((T, D_MODEL), f32)
    nxt = [up_conv(c0) for c0 in groups[0]]
    for gi, grp in enumerate(groups):
        cur = nxt
        if gi + 1 < len(groups):
            nxt = [up_conv(c0) for c0 in groups[gi + 1]]
        act = jnp.concatenate([(a_g * _sigmoid(a_g) * b_v).astype(bf16) for a_g, b_v in cur], axis=1)
        acc = acc + jnp.dot(act, wdn_ref[grp[0] * FFN_COLS:(grp[-1] + 1) * FFN_COLS, :],
                            preferred_element_type=f32)
    o_ref[0] = _layer_norm(alpha * x1 + acc, g2_ref[...], b2_ref[...])


def _merge_ffn(x, ya, yb, hist, wc, wg, wbr, wout, cw, cb, g1, b1, fhist, wup, fcw, fcb, wdn, g2, b2, alpha):
    B, S, _ = x.shape
    T = _row_tile(S, 256)
    row = lambda w: pl.BlockSpec((1, T, w), lambda bb, t: (bb, t, 0))
    hspec = pl.BlockSpec((1, HIST, SCONV_WIDTH), lambda bb, t: (bb, 0, 0))
    fspec = pl.BlockSpec((1, HIST, 2 * D_FF), lambda bb, t: (bb, 0, 0))
    vec = lambda n: _const_spec((1, n))
    return pl.pallas_call(
        functools.partial(_merge_ffn_kernel, T=T, alpha=alpha),
        grid=(B, S // T),
        in_specs=[row(D_MODEL), row(BRANCH_WIDTH), row(BRANCH_WIDTH), hspec,
                  _const_spec((D_MODEL, 3 * SCONV_WIDTH)), _const_spec((D_MODEL, N_BRANCH * D_MODEL)),
                  _const_spec((N_BRANCH, BRANCH_WIDTH, D_MODEL)), _const_spec((D_MODEL, D_MODEL)),
                  _const_spec((CONV_W, SCONV_WIDTH)), vec(SCONV_WIDTH), vec(D_MODEL), vec(D_MODEL),
                  fspec, _const_spec((D_MODEL, 2 * D_FF)), _const_spec((CONV_W, 2 * D_FF)), vec(2 * D_FF),
                  _const_spec((D_FF, D_MODEL)), vec(D_MODEL), vec(D_MODEL)],
        out_specs=[row(D_MODEL), hspec, fspec],
        out_shape=[jax.ShapeDtypeStruct((B, S, D_MODEL), f32),
                   jax.ShapeDtypeStruct((B, HIST, SCONV_WIDTH), f32),
                   jax.ShapeDtypeStruct((B, HIST, 2 * D_FF), f32)],
        scratch_shapes=[pltpu.VMEM((PADROWS + T, SCONV_WIDTH), f32)]
        + [pltpu.VMEM((PADROWS + T, FFN_COLS), f32)] * (2 * D_FF // FFN_COLS),
        compiler_params=_params(("parallel", "arbitrary")),
        name="merge_ffn",
    )(x, ya, yb, hist, wc, wg, wbr, wout, cw, cb, g1, b1, fhist, wup, fcw, fcb, wdn, g2, b2)


def _cast_kernel(x_ref, o_ref):
    o_ref[...] = x_ref[...].astype(o_ref.dtype)


def _to_bf16(w):
    D, Rw, C = w.shape
    T = _row_tile(Rw, LANES)
    spec = pl.BlockSpec((1, T, C), lambda d, t: (d, t, 0))
    return pl.pallas_call(_cast_kernel, grid=(D, Rw // T), in_specs=[spec], out_specs=spec,
                          out_shape=jax.ShapeDtypeStruct(w.shape, bf16),
                          compiler_params=_params(("parallel", "parallel")), name="to_bf16")(w)


def _layer_weights(l, depth, w_in, hgrn_lb_logits, hgrn_norm_g, sconv_w, sconv_b, w_branch, w_out,
                   ln1_g, ln1_b, w_up, ffn_conv_w, ffn_conv_b, w_down, ln2_g, ln2_b):
    offs = np.concatenate([[0], np.cumsum(IN_SIZES)]).tolist()
    sec = lambda i, j=None: w_in[l][:, offs[i]:offs[(i if j is None else j) + 1]]
    a_q, a_k, a_v, i_q, i_k, i_w = (sec(i) for i in range(6))
    zpad = lambda n: jnp.zeros((D_MODEL, n), w_in.dtype)
    w_row = jnp.concatenate([a_k, i_k, zpad(LANES - IDX_DIM), a_v], axis=1).astype(bf16)
    w_t = jnp.concatenate([a_q, i_q, a_v, i_w, zpad(WT_ROWS - 900)], axis=1).T.astype(bf16)
    lbp = jax.nn.softmax(hgrn_lb_logits.astype(f32), axis=0)
    lb = (jnp.cumsum(lbp, axis=0) - lbp[0])[l].reshape(1, HGRN_W)
    row = lambda a: a.reshape(1, -1).astype(f32)
    return dict(
        w_row=w_row, w_t=w_t, w_h=sec(6, 9).astype(bf16), lb=lb, norm_g=row(hgrn_norm_g[l]),
        w_c=sec(10, 12).astype(bf16), w_g=sec(13).astype(bf16),
        w_br=w_branch[l].astype(bf16), w_out=w_out[l].astype(bf16),
        sconv_w=sconv_w[l].astype(f32), sconv_b=row(sconv_b[l]),
        ln1_g=row(ln1_g[l]), ln1_b=row(ln1_b[l]),
        w_up=w_up[l].astype(bf16), ffn_w=ffn_conv_w[l].astype(f32), ffn_b=row(ffn_conv_b[l]),
        w_down=w_down[l].astype(bf16), ln2_g=row(ln2_g[l]), ln2_b=row(ln2_b[l]))


def _key_tiles(a, TK):
    B, L, C = a.shape
    NT = -(-L // TK)
    a = jnp.pad(a, ((0, 0), (0, NT * TK - L), (0, 0)))
    return a.reshape(B, NT, TK, C)


def _trunk_layer(x, tables, P, k_past, v_past, ki_past, s0, sc_hist, ffn_hist, w, alpha):
    B, S, _ = x.shape
    TK = KEY_TILE
    a = _attn_proj(x, w["w_row"], w["w_t"], *tables)
    k, v, ki = a["k"], a["v"], a["ki"]
    L = P + S
    if k_past is None and S % TK == 0:
        kb = a["kb"].reshape(B, S // TK, TK, -1)
        kib = a["kib"].reshape(B, S // TK, TK, -1)
        vt = a["vt"]
    else:
        kb, kib, v_all = a["kb"], a["kib"], v.astype(bf16)
        if k_past is not None:
            kb = jnp.concatenate([k_past.reshape(B, P, -1).astype(bf16), kb], axis=1)
            kib = jnp.concatenate([ki_past.astype(bf16), kib], axis=1)
            v_all = jnp.concatenate([v_past.reshape(B, P, -1).astype(bf16), v_all], axis=1)
        kb, kib = _key_tiles(kb, TK), _key_tiles(kib, TK)
        vt = _key_tiles(v_all, TK).reshape(B, -1, TK, KV_HEADS, HEAD_DIM).transpose(0, 1, 3, 4, 2)
        ones = jnp.zeros(vt.shape[:3] + (VT_ROWS - HEAD_DIM, TK), bf16).at[:, :, :, 0, :].set(1.0)
        vt = jnp.concatenate([vt, ones], axis=3)
    y_a = _dsa(a["qt"], a["qit"], a["wit"], kb, kib, vt, P, min(TOPK_MAX, L // 4))
    y_b, s_new = _hgrn(x, w["w_h"], w["lb"], w["norm_g"], s0)
    x2, sc_new, ffn_new = _merge_ffn(
        x, y_a, y_b, sc_hist, w["w_c"], w["w_g"], w["w_br"], w["w_out"], w["sconv_w"], w["sconv_b"],
        w["ln1_g"], w["ln1_b"], ffn_hist, w["w_up"], w["ffn_w"], w["ffn_b"], w["w_down"],
        w["ln2_g"], w["ln2_b"], alpha)
    new = (k.reshape(B, S, KV_HEADS, HEAD_DIM), v.reshape(B, S, KV_HEADS, HEAD_DIM), ki,
           s_new, sc_new, ffn_new)
    return x2, new


def kernel(x_prompt, x_sample, cache_attn_k, cache_attn_v, cache_idx_k, state_hgrn, state_sconv,
           state_ffn_conv, w_in, hgrn_lb_logits, hgrn_norm_g, sconv_w, sconv_b, w_branch, w_out,
           ln1_g, ln1_b, w_up, ffn_conv_w, ffn_conv_b, w_down, ln2_g, ln2_b):
    depth = w_in.shape[0]
    alpha = (2 * depth) ** 0.25
    B, S, _ = x_prompt.shape
    DB, DS, _ = x_sample.shape
    P = cache_attn_k.shape[2]
    cs_p = _rotary_tables(jnp.arange(S))
    cs_s = _rotary_tables(P + jnp.arange(DS))
    xp, xs = x_prompt, x_sample
    st_p = [[] for _ in range(6)]
    st_s = [[] for _ in range(6)]
    w_in_b = _to_bf16(w_in)
    for l in range(depth):
        w = _layer_weights(l, depth, w_in_b, hgrn_lb_logits, hgrn_norm_g, sconv_w, sconv_b, w_branch,
                           w_out, ln1_g, ln1_b, w_up, ffn_conv_w, ffn_conv_b, w_down, ln2_g, ln2_b)
        xp, new_p = _trunk_layer(
            xp, cs_p, 0, None, None, None,
            jnp.zeros((B, HGRN_HEADS, HGRN_DK, HGRN_DV), f32),
            jnp.zeros((B, HIST, SCONV_WIDTH), f32),
            jnp.zeros((B, HIST, 2 * D_FF), f32), w, alpha)
        xs, new_s = _trunk_layer(
            xs, cs_s, P, cache_attn_k[l], cache_attn_v[l], cache_idx_k[l], state_hgrn[l],
            state_sconv[l], state_ffn_conv[l], w, alpha)
        for j in range(6):
            st_p[j].append(new_p[j])
            st_s[j].append(new_s[j])
    outs_p = [jnp.stack(a, axis=0) for a in st_p]
    outs_s = [jnp.stack(a, axis=0) for a in st_s]
    return (xp, xs, *outs_p, *outs_s)
```

```python
import functools

import numpy as np
import jax
import jax.numpy as jnp
from jax import lax
from jax.experimental import pallas as pl
from jax.experimental.pallas import tpu as pltpu

D_MODEL = 1024
CHUNK = 64
N_HEADS = 8
HEAD_DIM = 64
KV_HEADS = 2
GROUPS = N_HEADS // KV_HEADS
IDX_HEADS = 4
IDX_DIM = 64
TOPK_MAX = 256
ROPE_THETA = 500000.0
ATTN_SCALE = HEAD_DIM ** -0.5
IDX_SCALE = (IDX_DIM ** -0.5) * (IDX_HEADS ** -0.5)
NEG = -1e30
HGRN_HEADS = 4
HGRN_DK = 128
HGRN_DV = 128
HGRN_W = HGRN_HEADS * HGRN_DK
SCONV_WIDTH = 512
CONV_W = 3
BRANCH_WIDTH = 512
N_BRANCH = 3
D_FF = 2816
LN_EPS = 1e-5
IN_SIZES = (N_HEADS * HEAD_DIM, KV_HEADS * HEAD_DIM, KV_HEADS * HEAD_DIM,
            IDX_HEADS * IDX_DIM, IDX_DIM, IDX_HEADS,
            HGRN_W, HGRN_W, HGRN_W, HGRN_W,
            SCONV_WIDTH, SCONV_WIDTH, SCONV_WIDTH,
            N_BRANCH * D_MODEL)

LANES = 128
VMEM_LIMIT = 56 * 1024 * 1024
KEY_TILE = 512
QUERY_TILE = 256
SCORE_UNROLL = 4
VT_ROWS = HEAD_DIM + 16
FFN_COLS = 256
FFN_GROUP = 5
MERGE_SUBTILES = 2
HGRN_SUBTILES = 4
PACK = 4
DIGITS = ((25, 7), (18, 7), (11, 7), (4, 7), (0, 4))
GUARD = np.int32(-0x7F7F7F80)
ONES = np.int32(0x01010101)

f32 = jnp.float32
bf16 = jnp.bfloat16
i32 = jnp.int32


def _row_tile(s, want):
    return want if s % want == 0 else s


def _const_spec(shape):
    nd = len(shape)
    return pl.BlockSpec(shape, lambda *_: (0,) * nd, pipeline_mode=pl.Buffered(1))


def _params(sem):
    return pltpu.CompilerParams(dimension_semantics=sem, vmem_limit_bytes=VMEM_LIMIT)


def _layer_norm(z, g, b):
    mu = jnp.mean(z, axis=-1, keepdims=True)
    d = z - mu
    var = jnp.mean(d * d, axis=-1, keepdims=True)
    return d * lax.rsqrt(var + LN_EPS) * g + b


def _sigmoid(x):
    return 1.0 / (1.0 + jnp.exp(-x))


Q_PRESCALE = ATTN_SCALE * float(np.log2(np.e))
ROT = HEAD_DIM // 4
WR_COLS = 384
WT_ROWS = 912
WI_ROWS = 8


def _attn_proj_kernel(x_ref, wr_ref, wt_ref, cs_ref, cst_ref, k_ref, v_ref, ki_ref, kb_ref, kib_ref,
                      vt_ref, qt_ref, qit_ref, wit_ref, *, T, TQ):
    xb = x_ref[0].astype(bf16)
    pr = jnp.dot(xb, wr_ref[...], preferred_element_type=f32)
    c, sa, sb = cs_ref[0], cs_ref[1], cs_ref[2]

    def rot(xg):
        return xg * c + pltpu.roll(xg, LANES - ROT // 2, 1) * sa + pltpu.roll(xg, ROT // 2, 1) * sb

    k = rot(pr[:, 0:LANES])
    ki = rot(pr[:, LANES:2 * LANES])[:, :IDX_DIM]
    k_ref[0] = k
    kb_ref[0] = k.astype(bf16)
    ki_ref[0] = ki
    kib_ref[0] = ki.astype(bf16)
    v_ref[0] = pr[:, 2 * LANES:3 * LANES]

    nt = (((1,), (1,)), ((), ()))
    pt = lax.dot_general(wt_ref[...], xb, nt, preferred_element_type=f32)
    ct, st = cst_ref[0], cst_ref[1]

    def rot_t(hb):
        lead = hb[0:ROT] * ct + jnp.concatenate([hb[ROT // 2:ROT], hb[0:ROT // 2]], axis=0) * st
        return jnp.concatenate([lead, hb[ROT:HEAD_DIM]], axis=0)

    nqb = T // TQ
    for h in range(N_HEADS):
        hb = (rot_t(pt[h * HEAD_DIM:(h + 1) * HEAD_DIM]) * Q_PRESCALE).astype(bf16)
        g, i = divmod(h, GROUPS)
        for n in range(nqb):
            qt_ref[0, n, g, :, i * TQ:(i + 1) * TQ] = hb[:, n * TQ:(n + 1) * TQ]
    base = N_HEADS * HEAD_DIM
    for h in range(IDX_HEADS):
        hb = rot_t(pt[base + h * IDX_DIM:base + (h + 1) * IDX_DIM]).astype(bf16)
        for n in range(nqb):
            qit_ref[0, n, :, h * TQ:(h + 1) * TQ] = hb[:, n * TQ:(n + 1) * TQ]
    base += IDX_HEADS * IDX_DIM
    ones_rows = jnp.where(lax.broadcasted_iota(i32, (VT_ROWS - HEAD_DIM, T), 0) == 0, 1.0, 0.0)
    for g in range(KV_HEADS):
        vt_ref[0, 0, g, 0:HEAD_DIM, :] = pt[base + g * HEAD_DIM:base + (g + 1) * HEAD_DIM].astype(bf16)
        vt_ref[0, 0, g, HEAD_DIM:VT_ROWS, :] = ones_rows.astype(bf16)
    base += KV_HEADS * HEAD_DIM
    for n in range(nqb):
        wit_ref[0, n] = pt[base:base + WI_ROWS, n * TQ:(n + 1) * TQ]


def _attn_proj(x, w_row, w_t, cs, cst):
    B, S, _ = x.shape
    T = _row_tile(S, KEY_TILE)
    TQ = _row_tile(S, QUERY_TILE)
    nqb = T // TQ
    R = GROUPS * TQ
    row = lambda w: pl.BlockSpec((1, T, w), lambda b, t: (b, t, 0))
    names = ("k", "v", "ki", "kb", "kib", "vt", "qt", "qit", "wit")
    outs = pl.pallas_call(
        functools.partial(_attn_proj_kernel, T=T, TQ=TQ),
        grid=(B, S // T),
        in_specs=[row(D_MODEL), _const_spec((D_MODEL, WR_COLS)), _const_spec((WT_ROWS, D_MODEL)),
                  pl.BlockSpec((3, T, LANES), lambda b, t: (0, t, 0)),
                  pl.BlockSpec((2, ROT, T), lambda b, t: (0, 0, t))],
        out_specs=[row(128), row(128), row(IDX_DIM), row(128), row(IDX_DIM),
                   pl.BlockSpec((1, 1, KV_HEADS, VT_ROWS, T), lambda b, t: (b, t, 0, 0, 0)),
                   pl.BlockSpec((1, nqb, KV_HEADS, HEAD_DIM, R), lambda b, t: (b, t, 0, 0, 0)),
                   pl.BlockSpec((1, nqb, IDX_DIM, IDX_HEADS * TQ), lambda b, t: (b, t, 0, 0)),
                   pl.BlockSpec((1, nqb, WI_ROWS, TQ), lambda b, t: (b, t, 0, 0))],
        out_shape=[jax.ShapeDtypeStruct((B, S, 128), f32),
                   jax.ShapeDtypeStruct((B, S, 128), f32),
                   jax.ShapeDtypeStruct((B, S, IDX_DIM), f32),
                   jax.ShapeDtypeStruct((B, S, 128), bf16),
                   jax.ShapeDtypeStruct((B, S, IDX_DIM), bf16),
                   jax.ShapeDtypeStruct((B, S // T, KV_HEADS, VT_ROWS, T), bf16),
                   jax.ShapeDtypeStruct((B, S // TQ, KV_HEADS, HEAD_DIM, R), bf16),
                   jax.ShapeDtypeStruct((B, S // TQ, IDX_DIM, IDX_HEADS * TQ), bf16),
                   jax.ShapeDtypeStruct((B, S // TQ, WI_ROWS, TQ), f32)],
        compiler_params=_params(("parallel", "parallel")),
        name="attn_proj",
    )(x, w_row, w_t, cs, cst)
    return dict(zip(names, outs))


def _rotary_tables(pos):
    half = ROT // 2
    inv_freq = ROPE_THETA ** (-jnp.arange(half, dtype=f32) / half)
    ang = pos.astype(f32)[:, None] * inv_freq[None, :]
    cos = jnp.cos(ang)
    sin = jnp.sin(ang)
    S = pos.shape[0]
    one = jnp.ones((S, HEAD_DIM - ROT), f32)
    zero = jnp.zeros((S, HEAD_DIM - ROT), f32)
    zh = jnp.zeros((S, half), f32)
    c = jnp.concatenate([cos, cos, one], axis=1)
    sa = jnp.concatenate([-sin, zh, zero], axis=1)
    sb = jnp.concatenate([zh, sin, zero], axis=1)
    tile2 = lambda a: jnp.concatenate([a, a], axis=1)
    cs = jnp.stack([tile2(c), tile2(sa), tile2(sb)], axis=0)
    cst = jnp.stack([jnp.concatenate([cos, cos], axis=1).T,
                     jnp.concatenate([-sin, sin], axis=1).T], axis=0)
    return cs, cst


def _dsa_kernel(qt_ref, qit_ref, wit_ref, ki_ref, k_ref, vt_ref, tri_ref, o_ref,
                s_ref, pk_ref, qpad_ref, m_ref, acc_ref, *bufs, TQ, TK, P, top):
    lm_refs = (bufs[0:KV_HEADS], bufs[KV_HEADS:2 * KV_HEADS])
    p_refs = (bufs[2 * KV_HEADS:3 * KV_HEADS], bufs[3 * KV_HEADS:4 * KV_HEADS])
    qb = pl.program_id(1)
    q0 = P + qb * TQ
    nk = (q0 + TQ + TK - 1) // TK
    n_full = q0 // TK
    NC = TK // LANES
    R = GROUPS * TQ
    q_lim = ((q0 + lax.broadcasted_iota(i32, (1, TQ), 1)) // CHUNK + 1) * CHUNK
    wit = wit_ref[0, 0] * IDX_SCALE
    zero_head = jnp.zeros((HEAD_DIM, R), bf16)
    qpad_ref[0] = jnp.concatenate([qt_ref[0, 0, 0], zero_head], axis=0)
    qpad_ref[1] = jnp.concatenate([zero_head, qt_ref[0, 0, 1]], axis=0)

    TKP = TK // PACK

    def lead_digit(key):
        return (key >> DIGITS[0][0]) + (1 << (DIGITS[0][1] - 1))

    def score_tile(masked, j, carry):
        for c in range(NC):
            rows = slice(c * LANES, (c + 1) * LANES)
            rel = jnp.maximum(jnp.dot(ki_ref[0, j, rows, :], qit_ref[0, 0], preferred_element_type=f32), 0.0)
            score = wit[0:1, :] * rel[:, 0:TQ]
            for h in range(1, IDX_HEADS):
                score = score + wit[h:h + 1, :] * rel[:, h * TQ:(h + 1) * TQ]
            if masked:
                k_pos = j * TK + c * LANES + lax.broadcasted_iota(i32, (LANES, TQ), 0)
                score = jnp.where(k_pos < q_lim, score, NEG)
            bits = pltpu.bitcast(score, i32)
            key = jnp.where(bits >= 0, bits, bits ^ 0x7FFFFFFF)
            s_ref[j, rows, :] = key
            if c == 0:
                pk_ref[j] = lead_digit(key) | GUARD
            else:
                pk_ref[j] = pk_ref[j] | (lead_digit(key) << (8 * c))
        return carry

    def score_group(i, carry):
        for u in range(SCORE_UNROLL):
            carry = score_tile(False, SCORE_UNROLL * i + u, carry)
        return carry

    lax.fori_loop(0, n_full // SCORE_UNROLL, score_group, 0)
    lax.fori_loop(n_full // SCORE_UNROLL * SCORE_UNROLL, n_full, functools.partial(score_tile, False), 0)
    lax.fori_loop(n_full, nk, functools.partial(score_tile, True), 0)

    n_chunks = (nk + PACK - 1) // PACK

    def fill_guard(j, carry):
        pk_ref[j] = jnp.full((TKP, TQ), GUARD, i32)
        return carry

    lax.fori_loop(nk, n_chunks * PACK, fill_guard, 0)

    def count_ge(cand):
        cvec = cand * ONES

        def chunk(ci, tot):
            accs = [None] * 4
            for tt in range(PACK):
                for r in range(TKP // 8):
                    slab = pk_ref[ci * PACK + tt, r * 8:(r + 1) * 8, :]
                    hit = lax.shift_right_logical(slab - cvec, 7) & ONES
                    n = (tt * (TKP // 8) + r) % 4
                    accs[n] = hit if accs[n] is None else accs[n] + hit
            a = (accs[0] + accs[1]) + (accs[2] + accs[3])
            return tot + ((a & 0xFF) + ((a >> 8) & 0xFF) + ((a >> 16) & 0xFF)
                          + lax.shift_right_logical(a, 24))

        tot = lax.fori_loop(0, n_chunks, chunk, jnp.zeros((8, TQ), i32))
        return jnp.sum(tot, axis=0, keepdims=True)

    def repack(lo, width, prefix):
        origin = prefix * (1 << width)

        def tile(j, carry):
            for r in range(TKP // 8):
                word = jnp.full((8, TQ), GUARD, i32)
                for q in range(PACK):
                    key = s_ref[j, q * TKP + r * 8:q * TKP + (r + 1) * 8, :]
                    d = (key >> lo) - origin
                    in_range = pltpu.bitcast(d, jnp.uint32) < jnp.uint32(1 << width)
                    word = word | (jnp.where(in_range, d, 0) << (8 * q))
                pk_ref[j, r * 8:(r + 1) * 8, :] = word
            return carry
        lax.fori_loop(0, nk, tile, 0)

    base = jnp.zeros((1, TQ), i32)
    bsize = jnp.full((1, TQ), nk * TK, i32)
    prefix = None
    zero = jnp.zeros((1, TQ), i32)
    for stage, (lo, width) in enumerate(DIGITS):
        need = top - base

        def search_digit(stage=stage, lo=lo, width=width, prefix=prefix, need=need, bsize=bsize):
            if stage > 0:
                repack(lo, width, prefix)

            def bit_step(i, st):
                v, c_v, c_rej = st
                cand = v | jnp.left_shift(jnp.int32(1), width - 1 - i)
                c = count_ge(cand)
                ok = c >= need
                return jnp.where(ok, cand, v), jnp.where(ok, c, c_v), jnp.where(ok, c_rej, c)

            return lax.fori_loop(0, width, bit_step, (zero, bsize, zero))

        v, c_v, c_gt = search_digit()
        base, bsize = base + c_gt, c_v - c_gt
        prefix = v - (1 << (width - 1)) if stage == 0 else prefix * (1 << width) + v
    t = prefix
    neg_key = np.array(NEG, np.float32).view(np.int32) ^ 0x7FFFFFFF
    any_excess = jnp.max(jnp.where(base + bsize > top, 1, 0)) > 0

    @pl.when(jnp.logical_not(any_excess))
    def _():
        thr = jnp.where(t == neg_key, t + 1, t)

        def fast(j, carry):
            s_ref[j] = pltpu.bitcast(jnp.where(s_ref[j] >= thr, 0.0, NEG), i32)
            return carry
        lax.fori_loop(0, nk, fast, 0)

    @pl.when(any_excess)
    def _():
        need = jnp.where(t == neg_key, 0, top - base).astype(f32)

        def ranked(j, taken):
            keys = s_ref[j]
            for c in range(NC):
                kc = keys[c * LANES:(c + 1) * LANES, :]
                eq = jnp.where(kc == t, 1.0, 0.0)
                rank = taken + jnp.dot(tri_ref[...], eq.astype(bf16), preferred_element_type=f32)
                pick = jnp.where(kc > t, 1.0, jnp.where(rank <= need, eq, 0.0))
                s_ref[j, c * LANES:(c + 1) * LANES, :] = pltpu.bitcast(
                    jnp.where(pick > 0.5, 0.0, NEG), i32)
                taken = taken + jnp.sum(eq, axis=0, keepdims=True)
            return taken
        lax.fori_loop(0, nk, ranked, jnp.zeros((1, TQ), f32))

    m_ref[...] = jnp.full(m_ref.shape, NEG, f32)
    acc_ref[...] = jnp.zeros(acc_ref.shape, f32)

    def tile_step(nxt, cur):
        if cur is not None:
            j, slot, tile_max = cur
            m_old = [m_ref[g] for g in range(KV_HEADS)]
            m_new = [jnp.maximum(m_old[g], tile_max[g]) for g in range(KV_HEADS)]
            for g in range(KV_HEADS):
                m_ref[g] = m_new[g]
        if nxt is not None:
            jn, slot_n = nxt
            jc = jnp.minimum(jn, nk - 1)
        out = []
        for g in range(KV_HEADS):
            cmax = jnp.full((8, R), NEG, f32)
            if nxt is not None:
                raw = jnp.dot(k_ref[0, jc], qpad_ref[g], preferred_element_type=f32)
            for c in range(NC):
                rows = slice(c * LANES, (c + 1) * LANES)
                if nxt is not None:
                    bias = pltpu.bitcast(s_ref[jc, rows, :], f32)
                    lm = raw[rows, :] + jnp.concatenate([bias] * GROUPS, axis=1)
                    lm_refs[slot_n][g][rows, :] = lm
                    cmax = jnp.maximum(cmax, jnp.max(lm.reshape(LANES // 8, 8, R), axis=0))
                if cur is not None:
                    p_refs[slot][g][rows, :] = jnp.exp2(lm_refs[slot][g][rows, :] - m_new[g]).astype(bf16)
            out.append(jnp.max(cmax, axis=0, keepdims=True))
            if cur is not None:
                acc_ref[g] = (jnp.exp2(m_old[g] - m_new[g]) * acc_ref[g]
                              + jnp.dot(vt_ref[0, j, g], p_refs[slot][g][...], preferred_element_type=f32))
        return tuple(out)

    def tile_pair(i, tile_max):
        nxt_max = tile_step((2 * i + 1, 1), (2 * i, 0, tile_max))
        return tile_step((2 * i + 2, 0), (2 * i + 1, 1, nxt_max))

    last_max = lax.fori_loop(0, nk // 2, tile_pair, tile_step((0, 0), None))

    @pl.when(nk % 2 == 1)
    def _():
        tile_step(None, (nk - 1, 0, last_max))

    for g in range(KV_HEADS):
        o_t = acc_ref[g, 0:HEAD_DIM, :] / acc_ref[g, HEAD_DIM:HEAD_DIM + 1, :]
        for i in range(0, GROUPS, 2):
            pair = jnp.concatenate([o_t[:, i * TQ:(i + 1) * TQ], o_t[:, (i + 1) * TQ:(i + 2) * TQ]], axis=0)
            h = g * GROUPS + i
            o_ref[0, :, h * HEAD_DIM:(h + 2) * HEAD_DIM] = pair.T.astype(bf16)


def _dsa(qt, qit, wit, kb, kib, vt, P, top):
    B, nq = qt.shape[:2]
    R = qt.shape[-1]
    TQ = R // GROUPS
    NT, TK = kb.shape[1:3]
    assert top <= TK and TQ % CHUNK == 0
    tri = jnp.asarray(np.tril(np.ones((LANES, LANES), np.float32)), bf16)
    return pl.pallas_call(
        functools.partial(_dsa_kernel, TQ=TQ, TK=TK, P=P, top=top),
        grid=(B, nq),
        in_specs=[pl.BlockSpec((1, 1, KV_HEADS, HEAD_DIM, R), lambda b, i: (b, i, 0, 0, 0)),
                  pl.BlockSpec((1, 1, IDX_DIM, IDX_HEADS * TQ), lambda b, i: (b, i, 0, 0)),
                  pl.BlockSpec((1, 1, WI_ROWS, TQ), lambda b, i: (b, i, 0, 0)),
                  pl.BlockSpec((1, NT, TK, IDX_DIM), lambda b, i: (b, 0, 0, 0), pipeline_mode=pl.Buffered(1)),
                  pl.BlockSpec((1, NT, TK, KV_HEADS * HEAD_DIM), lambda b, i: (b, 0, 0, 0),
                               pipeline_mode=pl.Buffered(1)),
                  pl.BlockSpec((1, NT, KV_HEADS, VT_ROWS, TK), lambda b, i: (b, 0, 0, 0, 0),
                               pipeline_mode=pl.Buffered(1)),
                  _const_spec((LANES, LANES))],
        out_specs=pl.BlockSpec((1, TQ, N_HEADS * HEAD_DIM), lambda b, i: (b, i, 0)),
        out_shape=jax.ShapeDtypeStruct((B, nq * TQ, N_HEADS * HEAD_DIM), bf16),
        scratch_shapes=[pltpu.VMEM((NT, TK, TQ), i32),
                        pltpu.VMEM((-(-NT // PACK) * PACK, TK // PACK, TQ), i32),
                        pltpu.VMEM((KV_HEADS, KV_HEADS * HEAD_DIM, R), bf16),
                        pltpu.VMEM((KV_HEADS, 1, R), f32),
                        pltpu.VMEM((KV_HEADS, VT_ROWS, R), f32)]
        + [pltpu.VMEM((TK, R), f32)] * (2 * KV_HEADS)
        + [pltpu.VMEM((TK, R), bf16)] * (2 * KV_HEADS),
        compiler_params=_params(("parallel", "arbitrary")),
        name="dsa",
    )(qt, qit, wit, kib, kb, vt, tri)


def _hgrn_mats(T):
    r = np.arange(T)[:, None]
    c = np.arange(T)[None, :]
    mats = [c <= r]
    b = T // 2
    while b >= 2:
        blk_r = r // (2 * b)
        mid = blk_r * 2 * b + b
        same = blk_r == c // (2 * b)
        late = r >= mid
        mats.append(same & ((late & (c >= mid) & (c <= r)) | (~late & (c > r) & (c < mid))))
        b //= 2
    return np.concatenate(mats, axis=0).astype(np.float32)


def _hgrn_kernel(x_ref, w_ref, lb_ref, g_ref, mats_ref, s0_ref, y_ref, st_out_ref, st_ref, *, T, SUBS):
    t_idx = pl.program_id(1)

    @pl.when(t_idx == 0)
    def _():
        st_ref[...] = s0_ref[0]

    lb = lb_ref[...]
    mats = mats_ref[...]
    row = lax.broadcasted_iota(i32, (T, 1), 0)
    nt = (((1,), (1,)), ((), ()))
    tn = (((0,), (0,)), ((), ()))

    def prepare(i):
        xb = x_ref[0, i * T:(i + 1) * T, :].astype(bf16)
        proj = jnp.dot(xb, w_ref[...], preferred_element_type=f32)
        hq = proj[:, 0:HGRN_W]
        z = proj[:, HGRN_W:2 * HGRN_W]
        hv = proj[:, 2 * HGRN_W:3 * HGRN_W]
        hg = proj[:, 3 * HGRN_W:4 * HGRN_W]
        logf = (jnp.minimum(z, 0.0) - jnp.log1p(jnp.exp(-jnp.abs(z)))) + jnp.log1p(lb * jnp.exp(-z))
        kk = (1.0 - lb) * _sigmoid(-z)
        qq = hq * _sigmoid(hq)
        gate = hg * _sigmoid(hg)
        hi = logf.astype(bf16)
        r1 = logf - hi.astype(f32)
        mid = r1.astype(bf16)
        lo = (r1 - mid.astype(f32)).astype(bf16)
        e_all = jnp.dot(mats, jnp.concatenate([hi, mid, lo], axis=0),
                        preferred_element_type=f32)
        return qq, kk, hv, gate, logf, e_all

    def recur(i, prepared):
        qq, kk, hv, gate, logf, e_all = prepared
        lane2 = lax.broadcasted_iota(i32, (1, 2 * HGRN_DK), 1)
        col2 = lax.broadcasted_iota(i32, (1, 2 * T), 1) % T
        for hp in range(0, HGRN_HEADS, 2):
            sl2 = slice(hp * HGRN_DK, (hp + 2) * HGRN_DK)
            q2, k2 = qq[:, sl2], kk[:, sl2]
            scores2 = jnp.zeros((T, 2 * T), f32)
            b = T // 2
            lvl = 1
            while b >= 1:
                late = ((row // b) % 2) == 1
                if b >= 2:
                    xdec = jnp.exp(e_all[lvl * T:(lvl + 1) * T, sl2])
                else:
                    xdec = jnp.where(late, jnp.exp(logf[:, sl2]), 1.0)
                a_m = jnp.where(late, q2 * xdec, 0.0).astype(bf16)
                b_m = jnp.where(late, 0.0, k2 * xdec)
                b_diag = jnp.concatenate([jnp.where(lane2 < HGRN_DK, b_m, 0.0),
                                          jnp.where(lane2 >= HGRN_DK, b_m, 0.0)], axis=0).astype(bf16)
                sc = lax.dot_general(a_m, b_diag, nt, preferred_element_type=f32)
                if 2 * b < T:
                    sc = jnp.where((row // (2 * b)) == (col2 // (2 * b)), sc, 0.0)
                scores2 = scores2 + sc
                b //= 2
                lvl += 1
            for h in (hp, hp + 1):
                sl = slice(h * HGRN_DK, (h + 1) * HGRN_DK)
                q_h, k_h, v_h = qq[:, sl], kk[:, sl], hv[:, sl]
                vb = v_h.astype(bf16)
                cum = e_all[0:T, sl]
                suf = cum[T - 1:T, :] - cum
                st = st_ref[h]
                qd = (q_h * jnp.exp(cum)).astype(bf16)
                scores = scores2[:, (h - hp) * T:(h - hp + 1) * T]
                o = jnp.sum(q_h * k_h, axis=-1, keepdims=True) * v_h
                o = o + jnp.dot(jnp.concatenate([qd, scores.astype(bf16)], axis=1),
                                jnp.concatenate([st.T.astype(bf16), vb], axis=0), preferred_element_type=f32)
                kd = (k_h * jnp.exp(suf)).astype(bf16)
                st_ref[h] = (st * jnp.exp(cum[T - 1:T, :])
                             + lax.dot_general(vb, kd, tn, preferred_element_type=f32))
                o = o * lax.rsqrt(jnp.mean(o * o, axis=-1, keepdims=True) + LN_EPS) * g_ref[...]
                y_ref[0, i * T:(i + 1) * T, sl] = (o * gate[:, sl]).astype(bf16)

    nxt = prepare(0)
    for i in range(SUBS):
        cur = nxt
        if i + 1 < SUBS:
            nxt = prepare(i + 1)
        recur(i, cur)

    st_out_ref[0] = st_ref[...]


def _hgrn(x, w_h, lb, norm_g, s0):
    B, S, _ = x.shape
    T = _row_tile(S, 2 * CHUNK)
    SUBS = HGRN_SUBTILES if S % (T * HGRN_SUBTILES) == 0 else 1
    TS = T * SUBS
    mats = jnp.asarray(np.tile(_hgrn_mats(T), (1, 3)), bf16)
    nm = mats.shape[0]
    st0 = jnp.swapaxes(s0.astype(f32), 2, 3)
    y, st = pl.pallas_call(
        functools.partial(_hgrn_kernel, T=T, SUBS=SUBS),
        grid=(B, S // TS),
        in_specs=[pl.BlockSpec((1, TS, D_MODEL), lambda b, t: (b, t, 0)),
                  _const_spec((D_MODEL, 4 * HGRN_W)),
                  _const_spec((1, HGRN_W)),
                  _const_spec((1, HGRN_DV)),
                  _const_spec((nm, 3 * T)),
                  pl.BlockSpec((1, HGRN_HEADS, HGRN_DV, HGRN_DK), lambda b, t: (b, 0, 0, 0))],
        out_specs=[pl.BlockSpec((1, TS, HGRN_W), lambda b, t: (b, t, 0)),
                   pl.BlockSpec((1, HGRN_HEADS, HGRN_DV, HGRN_DK), lambda b, t: (b, 0, 0, 0))],
        out_shape=[jax.ShapeDtypeStruct((B, S, HGRN_W), bf16),
                   jax.ShapeDtypeStruct((B, HGRN_HEADS, HGRN_DV, HGRN_DK), f32)],
        scratch_shapes=[pltpu.VMEM((HGRN_HEADS, HGRN_DV, HGRN_DK), f32)],
        compiler_params=_params(("parallel", "arbitrary")),
        name="hgrn",
    )(x, w_h, lb, norm_g, mats, st0)
    return y, jnp.swapaxes(st, 2, 3)


HIST = CONV_W - 1
PADROWS = 8


def _conv_rows(buf_ref, cols, u, w, bias, T):
    buf_ref[PADROWS:PADROWS + T, cols] = u
    y = bias
    for j in range(CONV_W):
        y = y + w[j:j + 1, :] * buf_ref[PADROWS - HIST + j:PADROWS - HIST + j + T, cols]
    buf_ref[PADROWS - HIST:PADROWS, cols] = u[T - HIST:T, :]
    return y


def _merge_kernel(x_ref, ya_ref, yb_ref, hist_ref, wc_ref, wg_ref, wbr_ref, wout_ref, cw_ref, cb_ref,
                  g_ref, b_ref, o_ref, hist_out_ref, buf_ref, *, T, SUBS, alpha):
    @pl.when(pl.program_id(1) == 0)
    def _():
        buf_ref[PADROWS - HIST:PADROWS, :] = hist_ref[0]

    for i in range(SUBS):
        rows = slice(i * T, (i + 1) * T)
        x = x_ref[0, rows, :]
        xb = x.astype(bf16)
        c = jnp.dot(xb, wc_ref[...], preferred_element_type=f32)
        W = SCONV_WIDTH
        u = c[:, W:2 * W] * c[:, 2 * W:3 * W]
        u_conv = _conv_rows(buf_ref, slice(0, W), u, cw_ref[...], cb_ref[...], T)
        yc = (c[:, 0:W] * u_conv).astype(bf16)
        merged = None
        for n, yb in enumerate((ya_ref[0, rows, :], yb_ref[0, rows, :], yc)):
            gate = _sigmoid(jnp.dot(xb, wg_ref[:, n * D_MODEL:(n + 1) * D_MODEL], preferred_element_type=f32))
            term = gate * jnp.dot(yb, wbr_ref[n], preferred_element_type=f32)
            merged = term if merged is None else merged + term
        out = jnp.dot(merged.astype(bf16), wout_ref[...], preferred_element_type=f32)
        o_ref[0, rows, :] = _layer_norm(alpha * x + out, g_ref[...], b_ref[...])
    hist_out_ref[0] = buf_ref[PADROWS - HIST:PADROWS, :]


def _merge(x, ya, yb, hist, wc, wg, wbr, wout, cw, cb, g, b, alpha):
    B, S, _ = x.shape
    T = _row_tile(S, 256)
    SUBS = MERGE_SUBTILES if S % (T * MERGE_SUBTILES) == 0 else 1
    row = lambda w: pl.BlockSpec((1, T * SUBS, w), lambda bb, t: (bb, t, 0))
    hspec = pl.BlockSpec((1, HIST, SCONV_WIDTH), lambda bb, t: (bb, 0, 0))
    return pl.pallas_call(
        functools.partial(_merge_kernel, T=T, SUBS=SUBS, alpha=alpha),
        grid=(B, S // (T * SUBS)),
        in_specs=[row(D_MODEL), row(BRANCH_WIDTH), row(BRANCH_WIDTH), hspec,
                  _const_spec((D_MODEL, 3 * SCONV_WIDTH)), _const_spec((D_MODEL, N_BRANCH * D_MODEL)),
                  _const_spec((N_BRANCH, BRANCH_WIDTH, D_MODEL)), _const_spec((D_MODEL, D_MODEL)),
                  _const_spec((CONV_W, SCONV_WIDTH)), _const_spec((1, SCONV_WIDTH)),
                  _const_spec((1, D_MODEL)), _const_spec((1, D_MODEL))],
        out_specs=[row(D_MODEL), hspec],
        out_shape=[jax.ShapeDtypeStruct((B, S, D_MODEL), f32),
                   jax.ShapeDtypeStruct((B, HIST, SCONV_WIDTH), f32)],
        scratch_shapes=[pltpu.VMEM((PADROWS + T, SCONV_WIDTH), f32)],
        compiler_params=_params(("parallel", "arbitrary")),
        name="merge",
    )(x, ya, yb, hist, wc, wg, wbr, wout, cw, cb, g, b)


def _ffn_kernel(x_ref, hist_ref, wup_ref, cw_ref, cb_ref, wdn_ref, g_ref, b_ref, o_ref, hist_out_ref,
                *bufs, T, alpha):
    n_steps = D_FF // FFN_COLS
    col_ranges = [slice(off + c * FFN_COLS, off + (c + 1) * FFN_COLS)
                  for c in range(n_steps) for off in (0, D_FF)]

    @pl.when(pl.program_id(1) == 0)
    def _():
        for buf, cols in zip(bufs, col_ranges):
            buf[PADROWS - HIST:PADROWS, :] = hist_ref[0, :, cols]

    x = x_ref[0]
    xb = x.astype(bf16)

    def up_conv(cidx):
        halves = []
        for half in range(2):
            buf, cols = bufs[2 * cidx + half], col_ranges[2 * cidx + half]
            h = jnp.dot(xb, wup_ref[:, cols], preferred_element_type=f32)
            halves.append(_conv_rows(buf, slice(None), h, cw_ref[:, cols], cb_ref[:, cols], T))
            hist_out_ref[0, :, cols] = buf[PADROWS - HIST:PADROWS, :]
        return halves

    groups = [list(range(c, min(c + FFN_GROUP, n_steps))) for c in range(0, n_steps, FFN_GROUP)]
    acc = jnp.zeros((T, D_MODEL), f32)
    nxt = [up_conv(c) for c in groups[0]]
    for gi, grp in enumerate(groups):
        cur = nxt
        if gi + 1 < len(groups):
            nxt = [up_conv(c) for c in groups[gi + 1]]
        act = jnp.concatenate([(a_g * _sigmoid(a_g) * b_v).astype(bf16) for a_g, b_v in cur], axis=1)
        acc = acc + jnp.dot(act, wdn_ref[grp[0] * FFN_COLS:(grp[-1] + 1) * FFN_COLS, :],
                            preferred_element_type=f32)
    o_ref[0] = _layer_norm(alpha * x + acc, g_ref[...], b_ref[...])


def _ffn(x, hist, wup, cw, cb, wdn, g, b, alpha):
    B, S, _ = x.shape
    T = _row_tile(S, 256)
    row = pl.BlockSpec((1, T, D_MODEL), lambda bb, t: (bb, t, 0))
    hspec = pl.BlockSpec((1, HIST, 2 * D_FF), lambda bb, t: (bb, 0, 0))
    return pl.pallas_call(
        functools.partial(_ffn_kernel, T=T, alpha=alpha),
        grid=(B, S // T),
        in_specs=[row, hspec, _const_spec((D_MODEL, 2 * D_FF)), _const_spec((CONV_W, 2 * D_FF)),
                  _const_spec((1, 2 * D_FF)), _const_spec((D_FF, D_MODEL)),
                  _const_spec((1, D_MODEL)), _const_spec((1, D_MODEL))],
        out_specs=[row, hspec],
        out_shape=[jax.ShapeDtypeStruct((B, S, D_MODEL), f32),
                   jax.ShapeDtypeStruct((B, HIST, 2 * D_FF), f32)],
        scratch_shapes=[pltpu.VMEM((PADROWS + T, FFN_COLS), f32)] * (2 * D_FF // FFN_COLS),
        compiler_params=_params(("parallel", "arbitrary")),
        name="ffn",
    )(x, hist, wup, cw, cb, wdn, g, b)


def _cast_kernel(x_ref, o_ref):
    o_ref[...] = x_ref[...].astype(o_ref.dtype)


def _to_bf16(w):
    D, Rw, C = w.shape
    T = _row_tile(Rw, LANES)
    spec = pl.BlockSpec((1, T, C), lambda d, t: (d, t, 0))
    return pl.pallas_call(_cast_kernel, grid=(D, Rw // T), in_specs=[spec], out_specs=spec,
                          out_shape=jax.ShapeDtypeStruct(w.shape, bf16),
                          compiler_params=_params(("parallel", "parallel")), name="to_bf16")(w)


def _layer_weights(l, depth, w_in, hgrn_lb_logits, hgrn_norm_g, sconv_w, sconv_b, w_branch, w_out,
                   ln1_g, ln1_b, w_up, ffn_conv_w, ffn_conv_b, w_down, ln2_g, ln2_b):
    offs = np.concatenate([[0], np.cumsum(IN_SIZES)]).tolist()
    sec = lambda i, j=None: w_in[l][:, offs[i]:offs[(i if j is None else j) + 1]]
    a_q, a_k, a_v, i_q, i_k, i_w = (sec(i) for i in range(6))
    zpad = lambda n: jnp.zeros((D_MODEL, n), w_in.dtype)
    w_row = jnp.concatenate([a_k, i_k, zpad(LANES - IDX_DIM), a_v], axis=1).astype(bf16)
    w_t = jnp.concatenate([a_q, i_q, a_v, i_w, zpad(WT_ROWS - 900)], axis=1).T.astype(bf16)
    lbp = jax.nn.softmax(hgrn_lb_logits.astype(f32), axis=0)
    lb = (jnp.cumsum(lbp, axis=0) - lbp[0])[l].reshape(1, HGRN_W)
    row = lambda a: a.reshape(1, -1).astype(f32)
    return dict(
        w_row=w_row, w_t=w_t, w_h=sec(6, 9).astype(bf16), lb=lb, norm_g=row(hgrn_norm_g[l]),
        w_c=sec(10, 12).astype(bf16), w_g=sec(13).astype(bf16),
        w_br=w_branch[l].astype(bf16), w_out=w_out[l].astype(bf16),
        sconv_w=sconv_w[l].astype(f32), sconv_b=row(sconv_b[l]),
        ln1_g=row(ln1_g[l]), ln1_b=row(ln1_b[l]),
        w_up=w_up[l].astype(bf16), ffn_w=ffn_conv_w[l].astype(f32), ffn_b=row(ffn_conv_b[l]),
        w_down=w_down[l].astype(bf16), ln2_g=row(ln2_g[l]), ln2_b=row(ln2_b[l]))


def _key_tiles(a, TK):
    B, L, C = a.shape
    NT = -(-L // TK)
    a = jnp.pad(a, ((0, 0), (0, NT * TK - L), (0, 0)))
    return a.reshape(B, NT, TK, C)


def _trunk_layer(x, tables, P, k_past, v_past, ki_past, s0, sc_hist, ffn_hist, w, alpha):
    B, S, _ = x.shape
    TK = KEY_TILE
    a = _attn_proj(x, w["w_row"], w["w_t"], *tables)
    k, v, ki = a["k"], a["v"], a["ki"]
    L = P + S
    if k_past is None and S % TK == 0:
        kb = a["kb"].reshape(B, S // TK, TK, -1)
        kib = a["kib"].reshape(B, S // TK, TK, -1)
        vt = a["vt"]
    else:
        kb, kib, v_all = a["kb"], a["kib"], v.astype(bf16)
        if k_past is not None:
            kb = jnp.concatenate([k_past.reshape(B, P, -1).astype(bf16), kb], axis=1)
            kib = jnp.concatenate([ki_past.astype(bf16), kib], axis=1)
            v_all = jnp.concatenate([v_past.reshape(B, P, -1).astype(bf16), v_all], axis=1)
        kb, kib = _key_tiles(kb, TK), _key_tiles(kib, TK)
        vt = _key_tiles(v_all, TK).reshape(B, -1, TK, KV_HEADS, HEAD_DIM).transpose(0, 1, 3, 4, 2)
        ones = jnp.zeros(vt.shape[:3] + (VT_ROWS - HEAD_DIM, TK), bf16).at[:, :, :, 0, :].set(1.0)
        vt = jnp.concatenate([vt, ones], axis=3)
    y_a = _dsa(a["qt"], a["qit"], a["wit"], kb, kib, vt, P, min(TOPK_MAX, L // 4))
    y_b, s_new = _hgrn(x, w["w_h"], w["lb"], w["norm_g"], s0)
    x1, sc_new = _merge(x, y_a, y_b, sc_hist, w["w_c"], w["w_g"], w["w_br"], w["w_out"],
                        w["sconv_w"], w["sconv_b"], w["ln1_g"], w["ln1_b"], alpha)
    x2, ffn_new = _ffn(x1, ffn_hist, w["w_up"], w["ffn_w"], w["ffn_b"], w["w_down"],
                       w["ln2_g"], w["ln2_b"], alpha)
    new = (k.reshape(B, S, KV_HEADS, HEAD_DIM), v.reshape(B, S, KV_HEADS, HEAD_DIM), ki,
           s_new, sc_new, ffn_new)
    return x2, new


def kernel(x_prompt, x_sample, cache_attn_k, cache_attn_v, cache_idx_k, state_hgrn, state_sconv,
           state_ffn_conv, w_in, hgrn_lb_logits, hgrn_norm_g, sconv_w, sconv_b, w_branch, w_out,
           ln1_g, ln1_b, w_up, ffn_conv_w, ffn_conv_b, w_down, ln2_g, ln2_b):
    depth = w_in.shape[0]
    alpha = (2 * depth) ** 0.25
    B, S, _ = x_prompt.shape
    DB, DS, _ = x_sample.shape
    P = cache_attn_k.shape[2]
    cs_p = _rotary_tables(jnp.arange(S))
    cs_s = _rotary_tables(P + jnp.arange(DS))
    xp, xs = x_prompt, x_sample
    st_p = [[] for _ in range(6)]
    st_s = [[] for _ in range(6)]
    w_in_b = _to_bf16(w_in)
    for l in range(depth):
        w = _layer_weights(l, depth, w_in_b, hgrn_lb_logits, hgrn_norm_g, sconv_w, sconv_b, w_branch,
                           w_out, ln1_g, ln1_b, w_up, ffn_conv_w, ffn_conv_b, w_down, ln2_g, ln2_b)
        xp, new_p = _trunk_layer(
            xp, cs_p, 0, None, None, None,
            jnp.zeros((B, HGRN_HEADS, HGRN_DK, HGRN_DV), f32),
            jnp.zeros((B, HIST, SCONV_WIDTH), f32),
            jnp.zeros((B, HIST, 2 * D_FF), f32), w, alpha)
        xs, new_s = _trunk_layer(
            xs, cs_s, P, cache_attn_k[l], cache_attn_v[l], cache_idx_k[l], state_hgrn[l],
            state_sconv[l], state_ffn_conv[l], w, alpha)
        for j in range(6):
            st_p[j].append(new_p[j])
            st_s[j].append(new_s[j])
    outs_p = [jnp.stack(a, axis=0) for a in st_p]
    outs_s = [jnp.stack(a, axis=0) for a in st_s]
    return (xp, xs, *outs_p, *outs_s)
```
